```python
import jax, jax.numpy as jnp
from jax import lax
import numpy as np

D_MODEL = 1024
BATCH = 8
SEQ = 16384
DEPTH = 1

D_MIX = D_MODEL
GLA_HEADS = 4
GLA_DV = 128
GLA_DK = 64
GLA_KEY = GLA_HEADS * GLA_DK
GLA_VAL = GLA_HEADS * GLA_DV
GATE_RANK = 16
GATE_TAU = 16.0
CHUNK = 64
CONV_CH = D_MIX - GLA_VAL
CONV_GROUPS = 8
CONV_W = 31
D_FF = 4 * D_MODEL
EPS = 1e-6

OFF_Q = 0
OFF_K = OFF_Q + GLA_KEY
OFF_V = OFF_K + GLA_KEY
OFF_G = OFF_V + GLA_VAL
OFF_A = OFF_G + GLA_VAL
OFF_C = OFF_A + GATE_RANK
D_IN = OFF_C + 2 * CONV_CH

kernel_name = "hybrid_gla_conformer_conv_block"


def rmsnorm(x, g):
    xf = x.astype(jnp.float32)
    y = xf * lax.rsqrt(jnp.mean(xf * xf, axis=-1, keepdims=True) + EPS)
    return (y * g.astype(jnp.float32)).astype(x.dtype)


def gla_chunked(q, k, v, log_a):
    B, S, H, DK = q.shape
    DV = v.shape[-1]
    nc = S // CHUNK

    def to_chunks(t):
        return t.astype(jnp.float32).reshape(B, nc, CHUNK, H, t.shape[-1]).transpose(1, 0, 3, 2, 4)

    qc, kc, vc, ac = map(to_chunks, (q, k, v, log_a))
    bc = jnp.cumsum(ac, axis=3)
    causal = jnp.tril(jnp.ones((CHUNK, CHUNK), dtype=bool))

    def step(state, inp):
        qi, ki, vi, bi = inp
        o_inter = jnp.einsum('bhcd,bhde->bhce', qi * jnp.exp(bi), state)
        diff = bi[:, :, :, None, :] - bi[:, :, None, :, :]
        decay = jnp.exp(jnp.where(causal[:, :, None], diff, -jnp.inf))
        scores = jnp.einsum('bhid,bhjd,bhijd->bhij', qi, ki, decay)
        o_intra = jnp.einsum('bhij,bhje->bhie', scores, vi)
        b_last = bi[:, :, -1, :]
        k_dec = ki * jnp.exp(b_last[:, :, None, :] - bi)
        new_state = jnp.exp(b_last)[..., None] * state + jnp.einsum('bhcd,bhce->bhde', k_dec, vi)
        return new_state, o_inter + o_intra

    s0 = jnp.zeros((B, H, DK, DV), jnp.float32)
    _, o = lax.scan(step, s0, (qc, kc, vc, bc))
    return o.transpose(1, 0, 3, 2, 4).reshape(B, S, H, DV).astype(v.dtype)


def causal_depthwise_conv(u, w, b):
    u_pad = jnp.pad(u, ((0, 0), (CONV_W - 1, 0), (0, 0)))
    y = lax.conv_general_dilated(
        u_pad, w[:, None, :].astype(u.dtype), window_strides=(1,), padding='VALID',
        dimension_numbers=('NWC', 'WIO', 'NWC'), feature_group_count=u.shape[-1])
    return y + b.astype(u.dtype)


def channel_group_norm(u, g, b):
    B, S, C = u.shape
    uf = u.astype(jnp.float32).reshape(B, S, CONV_GROUPS, C // CONV_GROUPS)
    mu = jnp.mean(uf, axis=-1, keepdims=True)
    var = jnp.mean(jnp.square(uf - mu), axis=-1, keepdims=True)
    y = ((uf - mu) * lax.rsqrt(var + EPS)).reshape(B, S, C)
    return (y * g.astype(jnp.float32) + b.astype(jnp.float32)).astype(u.dtype)


def _fwd_setup_inputs(seed: int = 0) -> dict:
    key = jax.random.key(seed)
    ks = jax.random.split(key, 16)
    f32 = jnp.float32
    nrm = lambda k, shape, s: jax.random.normal(k, shape, f32) * s
    return {
        "x": jax.random.normal(ks[0], (BATCH, SEQ, D_MODEL), f32),
        "norm1_g": 1.0 + nrm(ks[1], (DEPTH, D_MODEL), 0.02),
        "w_in": nrm(ks[2], (DEPTH, D_MODEL, D_IN), D_MODEL ** -0.5),
        "w_gate_up": nrm(ks[3], (DEPTH, GATE_RANK, GLA_KEY), GATE_RANK ** -0.5),
        "b_gate": nrm(ks[4], (DEPTH, GLA_KEY), 0.1),
        "gla_norm_g": 1.0 + nrm(ks[5], (DEPTH, GLA_DV), 0.02),
        "conv_w": nrm(ks[6], (DEPTH, CONV_W, CONV_CH), CONV_W ** -0.5),
        "conv_b": nrm(ks[7], (DEPTH, CONV_CH), 0.02),
        "conv_norm_g": 1.0 + nrm(ks[8], (DEPTH, CONV_CH), 0.02),
        "conv_norm_b": nrm(ks[9], (DEPTH, CONV_CH), 0.02),
        "w_out": nrm(ks[10], (DEPTH, D_MIX, D_MODEL), D_MIX ** -0.5),
        "norm2_g": 1.0 + nrm(ks[11], (DEPTH, D_MODEL), 0.02),
        "w_mlp_in": nrm(ks[12], (DEPTH, D_MODEL, D_FF), D_MODEL ** -0.5),
        "w_mlp_out": nrm(ks[13], (DEPTH, D_FF, D_MODEL), D_FF ** -0.5),
        "final_norm_g": 1.0 + nrm(ks[14], (D_MODEL,), 0.02),
    }


def _fwd_reference(x, norm1_g, w_in, w_gate_up, b_gate, gla_norm_g, conv_w, conv_b,
              conv_norm_g, conv_norm_b, w_out, norm2_g, w_mlp_in, w_mlp_out,
              final_norm_g):
    B, S, _ = x.shape
    h = x
    for l in range(DEPTH):
        xn = rmsnorm(h, norm1_g[l])
        proj = jnp.einsum('bsd,de->bse', xn, w_in[l])

        q = proj[..., OFF_Q:OFF_K].reshape(B, S, GLA_HEADS, GLA_DK) * (GLA_DK ** -0.5)
        k = proj[..., OFF_K:OFF_V].reshape(B, S, GLA_HEADS, GLA_DK)
        v = proj[..., OFF_V:OFF_G].reshape(B, S, GLA_HEADS, GLA_DV)
        g = proj[..., OFF_G:OFF_A]
        z = proj[..., OFF_A:OFF_C]
        a_logit = jnp.einsum('bsr,rk->bsk', z, w_gate_up[l]) + b_gate[l]
        log_a = (jax.nn.log_sigmoid(a_logit.astype(jnp.float32)) / GATE_TAU).reshape(B, S, GLA_HEADS, GLA_DK)
        o_gla = gla_chunked(q, k, v, log_a)
        o_gla = rmsnorm(o_gla, gla_norm_g[l]).reshape(B, S, GLA_VAL) * jax.nn.silu(g)

        c_in = proj[..., OFF_C:OFF_C + CONV_CH]
        c_gate = proj[..., OFF_C + CONV_CH:D_IN]
        u = c_in * jax.nn.sigmoid(c_gate)
        u = causal_depthwise_conv(u, conv_w[l], conv_b[l])
        o_conv = jax.nn.silu(channel_group_norm(u, conv_norm_g[l], conv_norm_b[l]))

        mixed = jnp.concatenate([o_gla, o_conv], axis=-1)
        h = h + jnp.einsum('bse,ed->bsd', mixed, w_out[l])

        hn = rmsnorm(h, norm2_g[l])
        ff = jnp.square(jax.nn.relu(jnp.einsum('bsd,df->bsf', hn, w_mlp_in[l])))
        h = h + jnp.einsum('bsf,fd->bsd', ff, w_mlp_out[l])
    return rmsnorm(h, final_norm_g)


import jax as _jax
import jax.numpy as _jnp

TWIN_FORMAT = 'train_step'
FWD_PARAMS = ['x', 'norm1_g', 'w_in', 'w_gate_up', 'b_gate', 'gla_norm_g', 'conv_w', 'conv_b', 'conv_norm_g', 'conv_norm_b', 'w_out', 'norm2_g', 'w_mlp_in', 'w_mlp_out', 'final_norm_g']
TWIN_WEIGHTS = ['norm1_g', 'w_in', 'w_gate_up', 'b_gate', 'gla_norm_g', 'conv_w', 'conv_b', 'conv_norm_g', 'conv_norm_b', 'w_out', 'norm2_g', 'w_mlp_in', 'w_mlp_out', 'final_norm_g']
TWIN_DIFF_INPUT = 'x'
TWIN_INPUTS = ['x', 'norm1_g', 'w_in', 'w_gate_up', 'b_gate', 'gla_norm_g', 'conv_w', 'conv_b', 'conv_norm_g', 'conv_norm_b', 'w_out', 'norm2_g', 'w_mlp_in', 'w_mlp_out', 'final_norm_g', 'loss_target', 'm_norm1_g', 'm_w_in', 'm_w_gate_up', 'm_b_gate', 'm_gla_norm_g', 'm_conv_w', 'm_conv_b', 'm_conv_norm_g', 'm_conv_norm_b', 'm_w_out', 'm_norm2_g', 'm_w_mlp_in', 'm_w_mlp_out', 'm_final_norm_g', 'v_norm1_g', 'v_w_in', 'v_w_gate_up', 'v_b_gate', 'v_gla_norm_g', 'v_conv_w', 'v_conv_b', 'v_conv_norm_g', 'v_conv_norm_b', 'v_w_out', 'v_norm2_g', 'v_w_mlp_in', 'v_w_mlp_out', 'v_final_norm_g']
TWIN_OUTPUTS = ['loss', 'grad_x', 'grad_norm1_g', 'grad_w_in', 'grad_w_gate_up', 'grad_b_gate', 'grad_gla_norm_g', 'grad_conv_w', 'grad_conv_b', 'grad_conv_norm_g', 'grad_conv_norm_b', 'grad_w_out', 'grad_norm2_g', 'grad_w_mlp_in', 'grad_w_mlp_out', 'grad_final_norm_g', 'delta_norm1_g', 'delta_w_in', 'delta_w_gate_up', 'delta_b_gate', 'delta_gla_norm_g', 'delta_conv_w', 'delta_conv_b', 'delta_conv_norm_g', 'delta_conv_norm_b', 'delta_w_out', 'delta_norm2_g', 'delta_w_mlp_in', 'delta_w_mlp_out', 'delta_final_norm_g', 'new_m_norm1_g', 'new_m_w_in', 'new_m_w_gate_up', 'new_m_b_gate', 'new_m_gla_norm_g', 'new_m_conv_w', 'new_m_conv_b', 'new_m_conv_norm_g', 'new_m_conv_norm_b', 'new_m_w_out', 'new_m_norm2_g', 'new_m_w_mlp_in', 'new_m_w_mlp_out', 'new_m_final_norm_g', 'new_v_norm1_g', 'new_v_w_in', 'new_v_w_gate_up', 'new_v_b_gate', 'new_v_gla_norm_g', 'new_v_conv_w', 'new_v_conv_b', 'new_v_conv_norm_g', 'new_v_conv_norm_b', 'new_v_w_out', 'new_v_norm2_g', 'new_v_w_mlp_in', 'new_v_w_mlp_out', 'new_v_final_norm_g']
TWIN_LEAF_KINDS = {'loss': 'loss', 'grad_x': 'grad_x', 'grad_norm1_g': 'grad_w', 'grad_w_in': 'grad_w', 'grad_w_gate_up': 'grad_w', 'grad_b_gate': 'grad_w', 'grad_gla_norm_g': 'grad_w', 'grad_conv_w': 'grad_w', 'grad_conv_b': 'grad_w', 'grad_conv_norm_g': 'grad_w', 'grad_conv_norm_b': 'grad_w', 'grad_w_out': 'grad_w', 'grad_norm2_g': 'grad_w', 'grad_w_mlp_in': 'grad_w', 'grad_w_mlp_out': 'grad_w', 'grad_final_norm_g': 'grad_w', 'delta_norm1_g': 'delta_w', 'delta_w_in': 'delta_w', 'delta_w_gate_up': 'delta_w', 'delta_b_gate': 'delta_w', 'delta_gla_norm_g': 'delta_w', 'delta_conv_w': 'delta_w', 'delta_conv_b': 'delta_w', 'delta_conv_norm_g': 'delta_w', 'delta_conv_norm_b': 'delta_w', 'delta_w_out': 'delta_w', 'delta_norm2_g': 'delta_w', 'delta_w_mlp_in': 'delta_w', 'delta_w_mlp_out': 'delta_w', 'delta_final_norm_g': 'delta_w', 'new_m_norm1_g': 'new_m', 'new_m_w_in': 'new_m', 'new_m_w_gate_up': 'new_m', 'new_m_b_gate': 'new_m', 'new_m_gla_norm_g': 'new_m', 'new_m_conv_w': 'new_m', 'new_m_conv_b': 'new_m', 'new_m_conv_norm_g': 'new_m', 'new_m_conv_norm_b': 'new_m', 'new_m_w_out': 'new_m', 'new_m_norm2_g': 'new_m', 'new_m_w_mlp_in': 'new_m', 'new_m_w_mlp_out': 'new_m', 'new_m_final_norm_g': 'new_m', 'new_v_norm1_g': 'new_v', 'new_v_w_in': 'new_v', 'new_v_w_gate_up': 'new_v', 'new_v_b_gate': 'new_v', 'new_v_gla_norm_g': 'new_v', 'new_v_conv_w': 'new_v', 'new_v_conv_b': 'new_v', 'new_v_conv_norm_g': 'new_v', 'new_v_conv_norm_b': 'new_v', 'new_v_w_out': 'new_v', 'new_v_norm2_g': 'new_v', 'new_v_w_mlp_in': 'new_v', 'new_v_w_mlp_out': 'new_v', 'new_v_final_norm_g': 'new_v'}


def _forward(args):
    return _fwd_reference(*[args[k] for k in FWD_PARAMS])


def _output_shape():
    def fwd():
        inp = _fwd_setup_inputs(0)
        return _fwd_reference(*[inp[k] for k in FWD_PARAMS])
    out = _jax.eval_shape(fwd)
    return out.shape, out.dtype

N_MICROBATCH = 1
ADAM_LR = 0.001
ADAM_B1 = 0.9
ADAM_B2 = 0.999
ADAM_EPS = 1e-08
ADAM_WD = 0.01
ADAM_STEP = 10
PER_EXAMPLE_BATCH_AXIS = {'x': 0, 'loss_target': 0}
SHARED_INPUTS = []
_WEIGHT_DTYPES = {'norm1_g': _jnp.float32, 'w_in': _jnp.float32, 'w_gate_up': _jnp.float32, 'b_gate': _jnp.float32, 'gla_norm_g': _jnp.float32, 'conv_w': _jnp.float32, 'conv_b': _jnp.float32, 'conv_norm_g': _jnp.float32, 'conv_norm_b': _jnp.float32, 'w_out': _jnp.float32, 'norm2_g': _jnp.float32, 'w_mlp_in': _jnp.float32, 'w_mlp_out': _jnp.float32, 'final_norm_g': _jnp.float32}
MOMENT_SCALE = {'norm1_g': 3.288155e-01, 'w_in': 2.073119e-01, 'w_gate_up': 3.269236e-02, 'b_gate': 1.196099e-01, 'gla_norm_g': 4.071477e-01, 'conv_w': 2.089019e-01, 'conv_b': 4.767315e-01, 'conv_norm_g': 2.737389e-01, 'conv_norm_b': 3.099449e-01, 'w_out': 2.091405e-01, 'norm2_g': 2.918259e-01, 'w_mlp_in': 1.506208e-01, 'w_mlp_out': 3.304559e-01, 'final_norm_g': 1.290159e+02}


def _to_microbatches(a, axis):
    t = _jnp.moveaxis(a, axis, 0)
    t = t.reshape((N_MICROBATCH, t.shape[0] // N_MICROBATCH) + t.shape[1:])
    return _jnp.moveaxis(t, 1, axis + 1)


def setup_inputs(seed: int = 0) -> dict:
    inp = _fwd_setup_inputs(seed)
    key = _jax.random.fold_in(_jax.random.key(seed), 7919)
    shape, _ = _output_shape()
    out = dict(inp)
    out["loss_target"] = _jax.random.normal(_jax.random.fold_in(key, 0), shape, _jnp.float32)
    for i, name in enumerate(TWIN_WEIGHTS):
        w = inp[name].astype(_jnp.float32)
        if MOMENT_SCALE is None:
            s = _jnp.sqrt(_jnp.mean(_jnp.square(w)) + 1e-30)
        else:
            s = MOMENT_SCALE[name]
        km, kv = _jax.random.split(_jax.random.fold_in(key, i + 1))
        out[name] = w
        out["m_" + name] = s * _jax.random.normal(km, w.shape, _jnp.float32)
        out["v_" + name] = (s * s) * _jax.random.uniform(kv, w.shape, _jnp.float32, 0.5, 1.5)
    if N_MICROBATCH > 1:
        for name, axis in PER_EXAMPLE_BATCH_AXIS.items():
            out[name] = _to_microbatches(out[name], axis)
    return {'x': out['x'], 'norm1_g': out['norm1_g'], 'w_in': out['w_in'], 'w_gate_up': out['w_gate_up'], 'b_gate': out['b_gate'], 'gla_norm_g': out['gla_norm_g'], 'conv_w': out['conv_w'], 'conv_b': out['conv_b'], 'conv_norm_g': out['conv_norm_g'], 'conv_norm_b': out['conv_norm_b'], 'w_out': out['w_out'], 'norm2_g': out['norm2_g'], 'w_mlp_in': out['w_mlp_in'], 'w_mlp_out': out['w_mlp_out'], 'final_norm_g': out['final_norm_g'], 'loss_target': out['loss_target'], 'm_norm1_g': out['m_norm1_g'], 'm_w_in': out['m_w_in'], 'm_w_gate_up': out['m_w_gate_up'], 'm_b_gate': out['m_b_gate'], 'm_gla_norm_g': out['m_gla_norm_g'], 'm_conv_w': out['m_conv_w'], 'm_conv_b': out['m_conv_b'], 'm_conv_norm_g': out['m_conv_norm_g'], 'm_conv_norm_b': out['m_conv_norm_b'], 'm_w_out': out['m_w_out'], 'm_norm2_g': out['m_norm2_g'], 'm_w_mlp_in': out['m_w_mlp_in'], 'm_w_mlp_out': out['m_w_mlp_out'], 'm_final_norm_g': out['m_final_norm_g'], 'v_norm1_g': out['v_norm1_g'], 'v_w_in': out['v_w_in'], 'v_w_gate_up': out['v_w_gate_up'], 'v_b_gate': out['v_b_gate'], 'v_gla_norm_g': out['v_gla_norm_g'], 'v_conv_w': out['v_conv_w'], 'v_conv_b': out['v_conv_b'], 'v_conv_norm_g': out['v_conv_norm_g'], 'v_conv_norm_b': out['v_conv_norm_b'], 'v_w_out': out['v_w_out'], 'v_norm2_g': out['v_norm2_g'], 'v_w_mlp_in': out['v_w_mlp_in'], 'v_w_mlp_out': out['v_w_mlp_out'], 'v_final_norm_g': out['v_final_norm_g']}


def _loss(weights, diff, rest, loss_target):
    with _jax.named_scope("forward"):
        args = {**rest, TWIN_DIFF_INPUT: diff, **{k: w.astype(_WEIGHT_DTYPES[k]) for k, w in weights.items()}}
        y = _forward(args)
    with _jax.named_scope("loss_head"):
        err = _jnp.square(y.astype(_jnp.float32) - loss_target)
        return 0.5 * _jnp.sum(_jnp.mean(err, axis=-1)) if err.ndim else 0.5 * err


def _adamw(w, g, m, v):
    m = ADAM_B1 * m + (1.0 - ADAM_B1) * g
    v = ADAM_B2 * v + (1.0 - ADAM_B2) * _jnp.square(g)
    m_hat = m / (1.0 - ADAM_B1 ** ADAM_STEP)
    v_hat = v / (1.0 - ADAM_B2 ** ADAM_STEP)
    delta = -ADAM_LR * (m_hat / (_jnp.sqrt(v_hat) + ADAM_EPS) + ADAM_WD * w)
    return delta, m, v


def reference(x, norm1_g, w_in, w_gate_up, b_gate, gla_norm_g, conv_w, conv_b, conv_norm_g, conv_norm_b, w_out, norm2_g, w_mlp_in, w_mlp_out, final_norm_g, loss_target, m_norm1_g, m_w_in, m_w_gate_up, m_b_gate, m_gla_norm_g, m_conv_w, m_conv_b, m_conv_norm_g, m_conv_norm_b, m_w_out, m_norm2_g, m_w_mlp_in, m_w_mlp_out, m_final_norm_g, v_norm1_g, v_w_in, v_w_gate_up, v_b_gate, v_gla_norm_g, v_conv_w, v_conv_b, v_conv_norm_g, v_conv_norm_b, v_w_out, v_norm2_g, v_w_mlp_in, v_w_mlp_out, v_final_norm_g):
    given = dict(x=x, norm1_g=norm1_g, w_in=w_in, w_gate_up=w_gate_up, b_gate=b_gate, gla_norm_g=gla_norm_g, conv_w=conv_w, conv_b=conv_b, conv_norm_g=conv_norm_g, conv_norm_b=conv_norm_b, w_out=w_out, norm2_g=norm2_g, w_mlp_in=w_mlp_in, w_mlp_out=w_mlp_out, final_norm_g=final_norm_g, loss_target=loss_target, m_norm1_g=m_norm1_g, m_w_in=m_w_in, m_w_gate_up=m_w_gate_up, m_b_gate=m_b_gate, m_gla_norm_g=m_gla_norm_g, m_conv_w=m_conv_w, m_conv_b=m_conv_b, m_conv_norm_g=m_conv_norm_g, m_conv_norm_b=m_conv_norm_b, m_w_out=m_w_out, m_norm2_g=m_norm2_g, m_w_mlp_in=m_w_mlp_in, m_w_mlp_out=m_w_mlp_out, m_final_norm_g=m_final_norm_g, v_norm1_g=v_norm1_g, v_w_in=v_w_in, v_w_gate_up=v_w_gate_up, v_b_gate=v_b_gate, v_gla_norm_g=v_gla_norm_g, v_conv_w=v_conv_w, v_conv_b=v_conv_b, v_conv_norm_g=v_conv_norm_g, v_conv_norm_b=v_conv_norm_b, v_w_out=v_w_out, v_norm2_g=v_norm2_g, v_w_mlp_in=v_w_mlp_in, v_w_mlp_out=v_w_mlp_out, v_final_norm_g=v_final_norm_g)
    weights = {n: given[n] for n in TWIN_WEIGHTS}
    shared = {n: given[n] for n in SHARED_INPUTS}
    per_example = {n: given[n] for n in ['x']}
    grad_fn = _jax.value_and_grad(_loss, argnums=(0, 1))

    def one_microbatch(ex, loss_target):
        ex = dict(ex)
        diff = ex.pop(TWIN_DIFF_INPUT)
        return grad_fn(weights, diff, {**shared, **ex}, loss_target)

    if N_MICROBATCH == 1:
        loss, (grad_w, grad_x) = one_microbatch(per_example, given["loss_target"])
    else:
        def body(carry, xs):
            loss_sum, grad_sum = carry
            l_k, (gw_k, gx_k) = one_microbatch(xs[0], xs[1])
            with _jax.named_scope("update"):
                return (loss_sum + l_k, _jax.tree.map(_jnp.add, grad_sum, gw_k)), gx_k

        init = (_jnp.zeros((), _jnp.float32), _jax.tree.map(_jnp.zeros_like, weights))
        (loss, grad_w), grad_x = _jax.lax.scan(body, init, (per_example, given["loss_target"]))
    with _jax.named_scope("update"):
        delta_w, new_m, new_v = {}, {}, {}
        for n in TWIN_WEIGHTS:
            delta_w[n], new_m[n], new_v[n] = _adamw(weights[n], grad_w[n], given["m_" + n], given["v_" + n])
    return (loss, grad_x, *[grad_w[n] for n in TWIN_WEIGHTS], *[delta_w[n] for n in TWIN_WEIGHTS],
            *[new_m[n] for n in TWIN_WEIGHTS], *[new_v[n] for n in TWIN_WEIGHTS])
```

```python
import functools

import jax
import jax.numpy as jnp
from jax import lax
from jax.experimental import pallas as pl
from jax.experimental.pallas import tpu as pltpu

F32 = jnp.float32
BF16 = jnp.bfloat16

N_DEV = 8
D_MODEL = 1024
HEADS = 4
DK = 64
DV = 128
KEY = HEADS * DK
VAL = HEADS * DV
RANK = 16
CONV = 512
GROUPS = 8
CONV_W = 31
HALO = 32
D_FF = 4096
D_IN = 2576
SHARD_IN = D_IN // N_DEV
SHARD_IN_PAD = 384
CHUNK = 64
EPS = 1e-6
GATE_TAU = 16.0
Q_SCALE = DK ** -0.5

P_Q, P_K, P_V, P_G, P_CI, P_CG, P_Z = 0, 256, 512, 1024, 1536, 2048, 2560
D_INP = 2688
Z_PAD = D_INP - P_Z

ADAM_LR = 0.001
ADAM_B1 = 0.9
ADAM_B2 = 0.999
ADAM_EPS = 1e-08
ADAM_WD = 0.01
ADAM_STEP = 10

VMEM_LIMIT = 56 * 1024 * 1024

MESH = pl.DeviceIdType.MESH
ANY = pl.BlockSpec(memory_space=pl.ANY)


def _nn(a, b):
    return jnp.dot(a, b, preferred_element_type=F32)


def _nt(a, b):
    return lax.dot_general(a, b, (((1,), (1,)), ((), ())), preferred_element_type=F32)


def _tn(a, b):
    return lax.dot_general(a, b, (((0,), (0,)), ((), ())), preferred_element_type=F32)


def _params(sem=None):
    return pltpu.CompilerParams(dimension_semantics=sem, vmem_limit_bytes=VMEM_LIMIT)


def _const(shape):
    return pl.BlockSpec(shape, lambda *_: (0,) * len(shape), pipeline_mode=pl.Buffered(1))


def _colsum(v):
    return jnp.sum(v, axis=0, keepdims=True)


def _rowmean(v):
    return jnp.mean(v, axis=-1, keepdims=True)


def _split_bf16(v):
    hi = v.astype(BF16)
    return hi, (v - hi.astype(F32)).astype(BF16)


def _my_place():
    return lax.axis_index("x"), lax.axis_index("y"), lax.axis_index("c")


def _peer(j):
    x, y, c = _my_place()
    jx, jy, jc = (j >> 2) & 1, (j >> 1) & 1, j & 1
    px = 1 - x if jx else x
    py = 1 - y if jy else y
    pc = 1 - c if jc else c
    return (px, py, pc), 4 * px + 2 * py + pc


def _all_gather(shards):
    n = len(shards)

    def body(*refs):
        ins, outs = refs[:n], refs[n:2 * n]
        send_sems, recv_sems, local_sems = refs[2 * n:]
        x, y, c = _my_place()
        me = 4 * x + 2 * y + c
        local = [pltpu.make_async_copy(ins[k], outs[k].at[me], local_sems.at[k]) for k in range(n)]
        for cp in local:
            cp.start()
        sends = []
        for j in range(1, N_DEV):
            peer, _ = _peer(j)
            for k in range(n):
                cp = pltpu.make_async_remote_copy(
                    src_ref=ins[k], dst_ref=outs[k].at[me], send_sem=send_sems.at[k, j - 1],
                    recv_sem=recv_sems.at[k, j - 1], device_id=peer, device_id_type=MESH)
                cp.start()
                sends.append(cp)
        for j in range(1, N_DEV):
            peer, peer_idx = _peer(j)
            for k in range(n):
                pltpu.make_async_remote_copy(
                    src_ref=ins[k], dst_ref=outs[k].at[peer_idx], send_sem=send_sems.at[k, j - 1],
                    recv_sem=recv_sems.at[k, j - 1], device_id=peer, device_id_type=MESH).wait_recv()
        for cp in sends:
            cp.wait_send()
        for cp in local:
            cp.wait()

    return pl.pallas_call(
        body, name="all_gather",
        out_shape=[jax.ShapeDtypeStruct((N_DEV,) + s.shape, s.dtype) for s in shards],
        in_specs=[ANY] * n, out_specs=[ANY] * n,
        scratch_shapes=[pltpu.SemaphoreType.DMA((n, N_DEV - 1)), pltpu.SemaphoreType.DMA((n, N_DEV - 1)),
                        pltpu.SemaphoreType.DMA((n,))],
    )(*shards)


def _exchange(parts):
    n = len(parts)

    def body(*refs):
        ins, outs = refs[:n], refs[n:2 * n]
        send_sems, recv_sems, local_sems = refs[2 * n:]
        x, y, c = _my_place()
        me = 4 * x + 2 * y + c
        local = [pltpu.make_async_copy(ins[k].at[me], outs[k].at[me], local_sems.at[k]) for k in range(n)]
        for cp in local:
            cp.start()
        sends = []
        for j in range(1, N_DEV):
            peer, peer_idx = _peer(j)
            for k in range(n):
                cp = pltpu.make_async_remote_copy(
                    src_ref=ins[k].at[peer_idx], dst_ref=outs[k].at[me], send_sem=send_sems.at[k, j - 1],
                    recv_sem=recv_sems.at[k, j - 1], device_id=peer, device_id_type=MESH)
                cp.start()
                sends.append(cp)
        for j in range(1, N_DEV):
            peer, peer_idx = _peer(j)
            for k in range(n):
                pltpu.make_async_remote_copy(
                    src_ref=ins[k].at[me], dst_ref=outs[k].at[peer_idx], send_sem=send_sems.at[k, j - 1],
                    recv_sem=recv_sems.at[k, j - 1], device_id=peer, device_id_type=MESH).wait_recv()
        for cp in sends:
            cp.wait_send()
        for cp in local:
            cp.wait()

    return pl.pallas_call(
        body, name="grad_exchange",
        out_shape=[jax.ShapeDtypeStruct(p.shape, p.dtype) for p in parts],
        in_specs=[ANY] * n, out_specs=[ANY] * n,
        scratch_shapes=[pltpu.SemaphoreType.DMA((n, N_DEV - 1)), pltpu.SemaphoreType.DMA((n, N_DEV - 1)),
                        pltpu.SemaphoreType.DMA((n,))],
    )(*parts)


def _inproj_fwd(x, g1, w_in_p):
    T = x.shape[0]
    tm = min(T, 512)

    def body(x_ref, g_ref, w_ref, proj_ref, xn_ref):
        xv = x_ref[...]
        r = lax.rsqrt(_rowmean(xv * xv) + EPS)
        xn = (xv * r * g_ref[...]).astype(BF16)
        xn_ref[...] = xn
        proj_ref[...] = _nn(xn, w_ref[...]).astype(BF16)

    return pl.pallas_call(
        body, name="inproj_fwd", grid=(T // tm,),
        out_shape=[jax.ShapeDtypeStruct((T, D_INP), BF16), jax.ShapeDtypeStruct((T, D_MODEL), BF16)],
        in_specs=[pl.BlockSpec((tm, D_MODEL), lambda i: (i, 0)), _const((1, D_MODEL)), _const((D_MODEL, D_INP))],
        out_specs=[pl.BlockSpec((tm, D_INP), lambda i: (i, 0)), pl.BlockSpec((tm, D_MODEL), lambda i: (i, 0))],
        compiler_params=_params(("arbitrary",)),
    )(x, g1, w_in_p)


def _head_masks():
    lane = lax.broadcasted_iota(jnp.int32, (1, KEY), 1)
    return [((lane >= h * DK) & (lane < (h + 1) * DK)).astype(F32) for h in range(HEADS)]


def _chunk_decay(z, wg, bg, tri):
    al = _nn(z, wg) + bg
    la = (jnp.minimum(al, 0.0) - jnp.log(1.0 + jnp.exp(-jnp.abs(al)))) * (1.0 / GATE_TAU)
    hi, lo = _split_bf16(la)
    b = _nn(tri, hi) + _nn(tri, lo)
    row = lax.broadcasted_iota(jnp.int32, la.shape, 0)
    b_last = _colsum(la)
    b_mid = _colsum(jnp.where(row < CHUNK // 2, la, 0.0))
    return al, b, b_last, b_mid


def _gla_fwd(proj, wg, bg, gn):
    T = proj.shape[0]
    tb = min(T, 512)
    cpb = tb // CHUNK

    def body(q_ref, k_ref, v_ref, g_ref, z_ref, wg_ref, bg_ref, gn_ref, mix_ref, o_ref, st_ref, state):
        @pl.when(pl.program_id(0) == 0)
        def _():
            state[...] = jnp.zeros_like(state)

        r_i = lax.broadcasted_iota(jnp.int32, (CHUNK, CHUNK), 0)
        c_i = lax.broadcasted_iota(jnp.int32, (CHUNK, CHUNK), 1)
        causal = r_i >= c_i
        tri = causal.astype(BF16)
        masks = _head_masks()
        wgv, bgv = wg_ref[...], bg_ref[...]

        def chunk(ci, carry):
            rows = pl.ds(pl.multiple_of(ci * CHUNK, CHUNK), CHUNK)
            _, b, b_last, b_mid = _chunk_decay(z_ref[rows, :], wgv, bgv, tri)
            q = q_ref[rows, :].astype(F32) * Q_SCALE
            k = k_ref[rows, :].astype(F32)
            qe0 = q * jnp.exp(b)
            qem = q * jnp.exp(b - b_mid)
            kem = (k * jnp.exp(b_mid - b)).astype(BF16)
            kdec = k * jnp.exp(b_last - b)
            ebl = jnp.exp(b_last)
            for h in range(HEADS):
                hm = masks[h]
                st = state[h]
                st_b = st.astype(BF16)
                st_ref[ci, h] = st_b
                vh = v_ref[rows, h * DV:(h + 1) * DV]
                a = jnp.where(causal, _nt((qem * hm).astype(BF16), kem), 0.0)
                o_h = _nt((qe0 * hm).astype(BF16), st_b) + _nn(a.astype(BF16), vh)
                o_ref[rows, h * DV:(h + 1) * DV] = o_h
                state[h] = st * ebl + _tn(vh, (kdec * hm).astype(BF16))
            return carry

        lax.fori_loop(0, cpb, chunk, 0)

        gnv = gn_ref[...]
        for h in range(HEADS):
            cols = slice(h * DV, (h + 1) * DV)
            oh = o_ref[:, cols]
            r = lax.rsqrt(_rowmean(oh * oh) + EPS)
            gh = g_ref[:, cols].astype(F32)
            mix_ref[:, cols] = (oh * r * gnv * (gh * jax.nn.sigmoid(gh))).astype(BF16)

    nc = T // CHUNK
    return pl.pallas_call(
        body, name="gla_fwd", grid=(T // tb,),
        out_shape=[jax.ShapeDtypeStruct((T, VAL), BF16), jax.ShapeDtypeStruct((T, VAL), F32),
                   jax.ShapeDtypeStruct((nc, HEADS, DV, KEY), BF16)],
        in_specs=[pl.BlockSpec((tb, KEY), lambda i: (i, P_Q // KEY)), pl.BlockSpec((tb, KEY), lambda i: (i, P_K // KEY)),
                  pl.BlockSpec((tb, VAL), lambda i: (i, P_V // VAL)), pl.BlockSpec((tb, VAL), lambda i: (i, P_G // VAL)),
                  pl.BlockSpec((tb, Z_PAD), lambda i: (i, P_Z // Z_PAD)),
                  _const((Z_PAD, KEY)), _const((1, KEY)), _const((1, DV))],
        out_specs=[pl.BlockSpec((tb, VAL), lambda i: (i, 0)), pl.BlockSpec((tb, VAL), lambda i: (i, 0)),
                   pl.BlockSpec((cpb, HEADS, DV, KEY), lambda i: (i, 0, 0, 0))],
        scratch_shapes=[pltpu.VMEM((HEADS, DV, KEY), F32)],
        compiler_params=_params(("arbitrary",)),
    )(proj, proj, proj, proj, proj, wg, bg, gn)


def _group_mean(v, gmat):
    return _nn(v.astype(BF16), gmat)


def _conv_fwd(proj, conv_w, conv_b, cn_g, cn_b, gmat):
    T = proj.shape[0]
    tm = min(T, 512)

    def body(ci_ref, cg_ref, w_ref, b_ref, g_ref, be_ref, gm_ref, mix_ref, uc_ref, ubuf):
        @pl.when(pl.program_id(0) == 0)
        def _():
            ubuf[0:HALO, :] = jnp.zeros((HALO, CONV), F32)

        ubuf[HALO:, :] = ci_ref[...].astype(F32) * jax.nn.sigmoid(cg_ref[...].astype(F32))
        acc = jnp.zeros((tm, CONV), F32) + b_ref[...]
        for j in range(CONV_W):
            acc = acc + w_ref[j:j + 1, :] * ubuf[pl.ds(HALO - (CONV_W - 1) + j, tm), :]
        uc_ref[...] = acc
        ubuf[0:HALO, :] = ubuf[tm:tm + HALO, :]
        gm = gm_ref[...]
        d = acc - _group_mean(acc, gm)
        var = _group_mean(d * d, gm)
        yn = d * lax.rsqrt(var + EPS) * g_ref[...] + be_ref[...]
        mix_ref[...] = (yn * jax.nn.sigmoid(yn)).astype(BF16)

    return pl.pallas_call(
        body, name="conv_fwd", grid=(T // tm,),
        out_shape=[jax.ShapeDtypeStruct((T, CONV), BF16), jax.ShapeDtypeStruct((T, CONV), F32)],
        in_specs=[pl.BlockSpec((tm, CONV), lambda i: (i, P_CI // CONV)), pl.BlockSpec((tm, CONV), lambda i: (i, P_CG // CONV)),
                  _const((HALO, CONV)), _const((1, CONV)), _const((1, CONV)), _const((1, CONV)), _const((CONV, CONV))],
        out_specs=[pl.BlockSpec((tm, CONV), lambda i: (i, 0)), pl.BlockSpec((tm, CONV), lambda i: (i, 0))],
        scratch_shapes=[pltpu.VMEM((tm + HALO, CONV), F32)],
        compiler_params=_params(("arbitrary",)),
    )(proj, proj, conv_w, conv_b, cn_g, cn_b, gmat)


def _rms_bwd(dy, xhat, r, g):
    dyg = dy * g
    return r * (dyg - xhat * _rowmean(dyg * xhat))


def _mlp_fwd_bwd(x, mix_a, mix_c, tgt, w_out, g2, w1, w2, gf):
    T = x.shape[0]
    tm = min(T, 256)
    inv_d = 1.0 / D_MODEL

    def body(x_ref, ma_ref, mc_ref, t_ref, wo_ref, g2_ref, w1_ref, w2_ref, gf_ref,
             dh1_ref, dmix_ref, hn_ref, ff_ref, da_ref, dh2_ref, loss_ref, dgf_ref, dg2_ref):
        @pl.when(pl.program_id(0) == 0)
        def _():
            loss_ref[...] = jnp.zeros_like(loss_ref)
            dgf_ref[...] = jnp.zeros_like(dgf_ref)
            dg2_ref[...] = jnp.zeros_like(dg2_ref)

        g2v, gfv = g2_ref[...], gf_ref[...]
        h1 = x_ref[...] + _nn(ma_ref[...], wo_ref[0:VAL, :]) + _nn(mc_ref[...], wo_ref[VAL:, :])
        r2 = lax.rsqrt(_rowmean(h1 * h1) + EPS)
        h1hat = h1 * r2
        hn = (h1hat * g2v).astype(BF16)
        hn_ref[...] = hn
        relu_a = jnp.maximum(_nn(hn, w1_ref[...]), 0.0)
        ff = (relu_a * relu_a).astype(BF16)
        ff_ref[...] = ff
        h2 = h1 + _nn(ff, w2_ref[...])
        rf = lax.rsqrt(_rowmean(h2 * h2) + EPS)
        h2hat = h2 * rf
        err = h2hat * gfv - t_ref[...]
        loss_ref[...] += (0.5 * inv_d) * _colsum(jnp.sum(err * err, axis=1, keepdims=True))
        dy = err * inv_d
        dgf_ref[...] += _colsum(dy * h2hat)
        dh2 = _rms_bwd(dy, h2hat, rf, gfv)
        dh2_b = dh2.astype(BF16)
        dh2_ref[...] = dh2_b
        da = (_nt(dh2_b, w2_ref[...]) * (2.0 * relu_a)).astype(BF16)
        da_ref[...] = da
        dhn = _nt(da, w1_ref[...])
        dg2_ref[...] += _colsum(dhn * h1hat)
        dh1 = dh2 + _rms_bwd(dhn, h1hat, r2, g2v)
        dh1_ref[...] = dh1
        dmix_ref[...] = _nt(dh1.astype(BF16), wo_ref[...]).astype(BF16)

    tok = lambda w: pl.BlockSpec((tm, w), lambda i: (i, 0))
    return pl.pallas_call(
        body, name="mlp_fwd_bwd", grid=(T // tm,),
        out_shape=[jax.ShapeDtypeStruct((T, D_MODEL), F32), jax.ShapeDtypeStruct((T, D_MODEL), BF16),
                   jax.ShapeDtypeStruct((T, D_MODEL), BF16), jax.ShapeDtypeStruct((T, D_FF), BF16),
                   jax.ShapeDtypeStruct((T, D_FF), BF16), jax.ShapeDtypeStruct((T, D_MODEL), BF16),
                   jax.ShapeDtypeStruct((1, 1), F32), jax.ShapeDtypeStruct((1, D_MODEL), F32),
                   jax.ShapeDtypeStruct((1, D_MODEL), F32)],
        in_specs=[tok(D_MODEL), tok(VAL), tok(CONV), tok(D_MODEL), _const((D_MODEL, D_MODEL)), _const((1, D_MODEL)),
                  _const((D_MODEL, D_FF)), _const((D_FF, D_MODEL)), _const((1, D_MODEL))],
        out_specs=[tok(D_MODEL), tok(D_MODEL), tok(D_MODEL), tok(D_FF), tok(D_FF), tok(D_MODEL),
                   pl.BlockSpec((1, 1), lambda i: (0, 0)), pl.BlockSpec((1, D_MODEL), lambda i: (0, 0)),
                   pl.BlockSpec((1, D_MODEL), lambda i: (0, 0))],
        compiler_params=_params(("arbitrary",)),
    )(x, mix_a, mix_c, tgt, w_out, g2, w1, w2, gf)


def _silu_grad(v, s):
    return s * (1.0 + v * (1.0 - s))


def _conv_bwd(proj, uc, dmix, conv_w, cn_g, cn_b, gmat):
    T = proj.shape[0]
    tm = min(T, 512)
    nt = T // tm
    hb = tm // HALO

    def body(ci_ref, cg_ref, cih_ref, cgh_ref, uc_ref, dm_ref, w_ref, g_ref, be_ref, gm_ref,
             dci_ref, dcg_ref, dw_ref, db_ref, dg_ref, dbe_ref, ubuf, dbuf):
        step = pl.program_id(0)

        @pl.when(step == 0)
        def _():
            dbuf[tm:, :] = jnp.zeros((HALO, CONV), F32)
            dw_ref[...] = jnp.zeros_like(dw_ref)
            db_ref[...] = jnp.zeros_like(db_ref)
            dg_ref[...] = jnp.zeros_like(dg_ref)
            dbe_ref[...] = jnp.zeros_like(dbe_ref)

        cin = ci_ref[...].astype(F32)
        sg = jax.nn.sigmoid(cg_ref[...].astype(F32))
        ubuf[HALO:, :] = cin * sg
        halo = cih_ref[...].astype(F32) * jax.nn.sigmoid(cgh_ref[...].astype(F32))
        ubuf[0:HALO, :] = jnp.where(step == nt - 1, 0.0, halo)

        gm, gv = gm_ref[...], g_ref[...]
        ucv = uc_ref[...]
        d = ucv - _group_mean(ucv, gm)
        rs = lax.rsqrt(_group_mean(d * d, gm) + EPS)
        yhat = d * rs
        yn = yhat * gv + be_ref[...]
        dyn = dm_ref[...].astype(F32) * _silu_grad(yn, jax.nn.sigmoid(yn))
        dg_ref[...] += _colsum(dyn * yhat)
        dbe_ref[...] += _colsum(dyn)
        dyh = dyn * gv
        duc = rs * (dyh - _group_mean(dyh, gm) - yhat * _group_mean(dyh * yhat, gm))
        db_ref[...] += _colsum(duc)
        dbuf[0:tm, :] = duc

        du = jnp.zeros((tm, CONV), F32)
        for j in range(CONV_W):
            du = du + w_ref[j:j + 1, :] * dbuf[pl.ds(CONV_W - 1 - j, tm), :]
            dw_ref[j:j + 1, :] += _colsum(duc * ubuf[pl.ds(HALO - (CONV_W - 1) + j, tm), :])
        dbuf[tm:, :] = dbuf[0:HALO, :]
        dci_ref[...] = (du * sg).astype(BF16)
        dcg_ref[...] = (du * cin * sg * (1.0 - sg)).astype(BF16)

    rev = lambda i: nt - 1 - i
    halo_idx = lambda i: jnp.maximum(rev(i) * hb - 1, 0)
    tile = lambda col: pl.BlockSpec((tm, CONV), lambda i: (rev(i), col))
    acc = lambda rows: pl.BlockSpec((rows, CONV), lambda i: (0, 0))
    return pl.pallas_call(
        body, name="conv_bwd", grid=(nt,),
        out_shape=[jax.ShapeDtypeStruct((T, CONV), BF16), jax.ShapeDtypeStruct((T, CONV), BF16),
                   jax.ShapeDtypeStruct((HALO, CONV), F32), jax.ShapeDtypeStruct((1, CONV), F32),
                   jax.ShapeDtypeStruct((1, CONV), F32), jax.ShapeDtypeStruct((1, CONV), F32)],
        in_specs=[tile(P_CI // CONV), tile(P_CG // CONV),
                  pl.BlockSpec((HALO, CONV), lambda i: (halo_idx(i), P_CI // CONV)),
                  pl.BlockSpec((HALO, CONV), lambda i: (halo_idx(i), P_CG // CONV)),
                  tile(0), tile(1), _const((HALO, CONV)), _const((1, CONV)), _const((1, CONV)), _const((CONV, CONV))],
        out_specs=[tile(0), tile(0), acc(HALO), acc(1), acc(1), acc(1)],
        scratch_shapes=[pltpu.VMEM((tm + HALO, CONV), F32), pltpu.VMEM((tm + HALO, CONV), F32)],
        compiler_params=_params(("arbitrary",)),
    )(proj, proj, proj, proj, uc, dmix, conv_w, cn_g, cn_b, gmat)


def _gla_bwd(proj, o, states, dmix, wg, bg, gn):
    T = proj.shape[0]
    tb = min(T, 512)
    cpb = tb // CHUNK
    nb = T // tb

    def body(q_ref, k_ref, v_ref, g_ref, z_ref, o_ref, st_ref, dm_ref, wg_ref, bg_ref, gn_ref,
             dq_ref, dk_ref, dv_ref, dg_ref, dz_ref, dwg_ref, dbg_ref, dgn_ref, dstate, do_scr):
        @pl.when(pl.program_id(0) == 0)
        def _():
            dstate[...] = jnp.zeros_like(dstate)
            dwg_ref[...] = jnp.zeros_like(dwg_ref)
            dbg_ref[...] = jnp.zeros_like(dbg_ref)
            dgn_ref[...] = jnp.zeros_like(dgn_ref)

        gnv = gn_ref[...]
        dgn = jnp.zeros((1, DV), F32)
        for h in range(HEADS):
            cols = slice(h * DV, (h + 1) * DV)
            oh = o_ref[:, cols]
            r = lax.rsqrt(_rowmean(oh * oh) + EPS)
            ohat = oh * r
            gh = g_ref[:, cols].astype(F32)
            sg = jax.nn.sigmoid(gh)
            dmx = dm_ref[:, cols].astype(F32)
            don = dmx * (gh * sg)
            dg_ref[:, cols] = (dmx * (ohat * gnv) * _silu_grad(gh, sg)).astype(BF16)
            dgn = dgn + _colsum(don * ohat)
            do_scr[:, cols] = _rms_bwd(don, ohat, r, gnv)
        dgn_ref[...] += dgn

        r_i = lax.broadcasted_iota(jnp.int32, (CHUNK, CHUNK), 0)
        c_i = lax.broadcasted_iota(jnp.int32, (CHUNK, CHUNK), 1)
        causal = r_i >= c_i
        causal_t = r_i <= c_i
        tri = causal.astype(BF16)
        tri_t = causal_t.astype(BF16)
        masks = _head_masks()
        wgv, bgv = wg_ref[...], bg_ref[...]

        def chunk(idx, carry):
            dwg, dbg = carry
            ci = cpb - 1 - idx
            rows = pl.ds(pl.multiple_of(ci * CHUNK, CHUNK), CHUNK)
            zc = z_ref[rows, :]
            al, b, b_last, b_mid = _chunk_decay(zc, wgv, bgv, tri)
            q = q_ref[rows, :].astype(F32) * Q_SCALE
            k = k_ref[rows, :].astype(F32)
            eb = jnp.exp(b)
            ebm = jnp.exp(b - b_mid)
            emb = jnp.exp(b_mid - b)
            elb = jnp.exp(b_last - b)
            ebl = jnp.exp(b_last)
            qe0 = q * eb
            qem_b = (q * ebm).astype(BF16)
            kem_b = (k * emb).astype(BF16)
            kdec = k * elb
            dq = jnp.zeros((CHUNK, KEY), F32)
            dk = jnp.zeros((CHUNK, KEY), F32)
            kst = jnp.zeros((CHUNK, KEY), F32)
            sds = jnp.zeros((1, KEY), F32)
            for h in range(HEADS):
                hm = masks[h]
                cols = slice(h * DV, (h + 1) * DV)
                st_b = st_ref[ci, h]
                ds = dstate[h]
                ds_b = ds.astype(BF16)
                do_b = do_scr[rows, cols].astype(BF16)
                vh = v_ref[rows, cols]
                da = jnp.where(causal, _nt(do_b, vh), 0.0).astype(BF16)
                da_t = jnp.where(causal_t, _nt(vh, do_b), 0.0).astype(BF16)
                a_t = jnp.where(causal_t, _nt(kem_b, (q * ebm * hm).astype(BF16)), 0.0).astype(BF16)
                dk_state = _nn(vh, ds_b) * elb
                dq = dq + hm * (_nn(do_b, st_b) * eb + _nn(da, kem_b) * ebm)
                dk = dk + hm * (_nn(da_t, qem_b) * emb + dk_state)
                kst = kst + hm * (k * dk_state)
                sds = sds + hm * _colsum(st_b.astype(F32) * ds)
                dv_ref[rows, cols] = (_nn(a_t, do_b) + _nt((kdec * hm).astype(BF16), ds_b)).astype(BF16)
                dstate[h] = ds * ebl + _tn(do_b, (qe0 * hm).astype(BF16))
            dq_ref[rows, :] = (dq * Q_SCALE).astype(BF16)
            dk_ref[rows, :] = dk.astype(BF16)
            db = q * dq - k * dk
            hi, lo = _split_bf16(db)
            dla = _nn(tri_t, hi) + _nn(tri_t, lo) + _colsum(kst) + ebl * sds
            dal = (dla * (1.0 / GATE_TAU) * jax.nn.sigmoid(-al))
            dal_b = dal.astype(BF16)
            dz_ref[rows, :] = _nt(dal_b, wgv).astype(BF16)
            return dwg + _tn(zc, dal_b), dbg + _colsum(dal)

        dwg, dbg = lax.fori_loop(0, cpb, chunk, (jnp.zeros((Z_PAD, KEY), F32), jnp.zeros((1, KEY), F32)))
        dwg_ref[...] += dwg
        dbg_ref[...] += dbg

    rev = lambda i: nb - 1 - i
    blk = lambda w, col: pl.BlockSpec((tb, w), lambda i: (rev(i), col))
    return pl.pallas_call(
        body, name="gla_bwd", grid=(nb,),
        out_shape=[jax.ShapeDtypeStruct((T, KEY), BF16), jax.ShapeDtypeStruct((T, KEY), BF16),
                   jax.ShapeDtypeStruct((T, VAL), BF16), jax.ShapeDtypeStruct((T, VAL), BF16),
                   jax.ShapeDtypeStruct((T, Z_PAD), BF16), jax.ShapeDtypeStruct((Z_PAD, KEY), F32),
                   jax.ShapeDtypeStruct((1, KEY), F32), jax.ShapeDtypeStruct((1, DV), F32)],
        in_specs=[blk(KEY, P_Q // KEY), blk(KEY, P_K // KEY), blk(VAL, P_V // VAL), blk(VAL, P_G // VAL),
                  blk(Z_PAD, P_Z // Z_PAD), blk(VAL, 0),
                  pl.BlockSpec((cpb, HEADS, DV, KEY), lambda i: (rev(i), 0, 0, 0)), blk(VAL, 0),
                  _const((Z_PAD, KEY)), _const((1, KEY)), _const((1, DV))],
        out_specs=[blk(KEY, 0), blk(KEY, 0), blk(VAL, 0), blk(VAL, 0), blk(Z_PAD, 0),
                   pl.BlockSpec((Z_PAD, KEY), lambda i: (0, 0)), pl.BlockSpec((1, KEY), lambda i: (0, 0)),
                   pl.BlockSpec((1, DV), lambda i: (0, 0))],
        scratch_shapes=[pltpu.VMEM((HEADS, DV, KEY), F32), pltpu.VMEM((tb, VAL), F32)],
        compiler_params=_params(("arbitrary",)),
    )(proj, proj, proj, proj, proj, o, states, dmix, wg, bg, gn)


def _inproj_bwd(x, g1, w_in_p, dh1, dq, dk, dv, dg, dci, dcg, dz):
    T = x.shape[0]
    tm = min(T, 512)

    def body(x_ref, g_ref, w_ref, dh1_ref, dq_ref, dk_ref, dv_ref, dg_ref, dci_ref, dcg_ref, dz_ref,
             dx_ref, dp_ref, dg1_ref):
        @pl.when(pl.program_id(0) == 0)
        def _():
            dg1_ref[...] = jnp.zeros_like(dg1_ref)

        dp_ref[:, P_Q:P_K] = dq_ref[...]
        dp_ref[:, P_K:P_V] = dk_ref[...]
        dp_ref[:, P_V:P_G] = dv_ref[...]
        dp_ref[:, P_G:P_CI] = dg_ref[...]
        dp_ref[:, P_CI:P_CG] = dci_ref[...]
        dp_ref[:, P_CG:P_Z] = dcg_ref[...]
        dp_ref[:, P_Z:] = dz_ref[...]
        dxn = _nt(dp_ref[...], w_ref[...])
        xv = x_ref[...]
        r = lax.rsqrt(_rowmean(xv * xv) + EPS)
        xhat = xv * r
        dg1_ref[...] += _colsum(dxn * xhat)
        dx_ref[...] = dh1_ref[...] + _rms_bwd(dxn, xhat, r, g_ref[...])

    tok = lambda w: pl.BlockSpec((tm, w), lambda i: (i, 0))
    return pl.pallas_call(
        body, name="inproj_bwd", grid=(T // tm,),
        out_shape=[jax.ShapeDtypeStruct((T, D_MODEL), F32), jax.ShapeDtypeStruct((T, D_INP), BF16),
                   jax.ShapeDtypeStruct((1, D_MODEL), F32)],
        in_specs=[tok(D_MODEL), _const((1, D_MODEL)), _const((D_MODEL, D_INP)), tok(D_MODEL), tok(KEY), tok(KEY),
                  tok(VAL), tok(VAL), tok(CONV), tok(CONV), tok(Z_PAD)],
        out_specs=[tok(D_MODEL), tok(D_INP), pl.BlockSpec((1, D_MODEL), lambda i: (0, 0))],
        compiler_params=_params(("arbitrary",)),
    )(x, g1, w_in_p, dh1, dq, dk, dv, dg, dci, dcg, dz)


def _wgrad(a, b, name, tk, tn, split_n=False):
    T, K = a.shape
    N = b.shape[1]
    tt = min(T, 1024)
    nt = T // tt

    def body(a_ref, b_ref, o_ref):
        @pl.when(pl.program_id(2) == 0)
        def _():
            o_ref[...] = jnp.zeros_like(o_ref)

        o_ref[...] += _tn(a_ref[...], b_ref[...])

    if split_n:
        assert tk == K
        out_shape = jax.ShapeDtypeStruct((N // tn, K, tn), F32)
        out_spec = pl.BlockSpec((None, tk, tn), lambda i, j, t: (j, 0, 0))
    else:
        out_shape = jax.ShapeDtypeStruct((K, N), F32)
        out_spec = pl.BlockSpec((tk, tn), lambda i, j, t: (i, j))
    return pl.pallas_call(
        body, name=name, grid=(K // tk, N // tn, nt), out_shape=out_shape,
        in_specs=[pl.BlockSpec((tt, tk), lambda i, j, t: (t, i)), pl.BlockSpec((tt, tn), lambda i, j, t: (t, j))],
        out_specs=out_spec,
        compiler_params=_params(("arbitrary", "arbitrary", "arbitrary")),
    )(a, b)


def _adam_math(w, g, m, v):
    m = ADAM_B1 * m + (1.0 - ADAM_B1) * g
    v = ADAM_B2 * v + (1.0 - ADAM_B2) * (g * g)
    m_hat = m / (1.0 - ADAM_B1 ** ADAM_STEP)
    v_hat = v / (1.0 - ADAM_B2 ** ADAM_STEP)
    delta = -ADAM_LR * (m_hat / (jnp.sqrt(v_hat) + ADAM_EPS) + ADAM_WD * w)
    return delta, m, v


def _sum8(ref):
    g = ref[0]
    for s in range(1, N_DEV):
        g = g + ref[s]
    return g


def _adam_big(parts, w, m, v, name):
    R, C = w.shape
    tr = min(R, 128)

    def body(p_ref, w_ref, m_ref, v_ref, g_ref, d_ref, nm_ref, nv_ref):
        g = _sum8(p_ref)
        g_ref[...] = g
        d_ref[...], nm_ref[...], nv_ref[...] = _adam_math(w_ref[...], g, m_ref[...], v_ref[...])

    row = pl.BlockSpec((tr, C), lambda i: (i, 0))
    return pl.pallas_call(
        body, name=name, grid=(R // tr,), out_shape=[jax.ShapeDtypeStruct((R, C), F32)] * 4,
        in_specs=[pl.BlockSpec((N_DEV, tr, C), lambda i: (0, i, 0)), row, row, row], out_specs=[row] * 4,
        compiler_params=_params(("arbitrary",)),
    )(parts, w, m, v)


def _sum_small(parts):
    def body(p_ref, o_ref):
        o_ref[...] = _sum8(p_ref)

    return pl.pallas_call(body, name="sum_small", out_shape=jax.ShapeDtypeStruct(parts.shape[1:], F32))(parts)


def _adam_small(gs, ws, ms, vs):
    n = len(gs)

    def body(*refs):
        g_refs, w_refs, m_refs, v_refs = refs[:n], refs[n:2 * n], refs[2 * n:3 * n], refs[3 * n:4 * n]
        outs = refs[4 * n:]
        for i in range(n):
            d, nm, nv = _adam_math(w_refs[i][...], g_refs[i][...], m_refs[i][...], v_refs[i][...])
            outs[i][...] = d
            outs[n + i][...] = nm
            outs[2 * n + i][...] = nv

    shapes = [jax.ShapeDtypeStruct(w.shape, F32) for w in ws]
    res = pl.pallas_call(body, name="adam_small", out_shape=shapes * 3)(*gs, *ws, *ms, *vs)
    return res[:n], res[n:2 * n], res[2 * n:]


def _permute_in(w):
    pad = jnp.zeros(w.shape[:-1] + (D_INP - D_IN,), w.dtype)
    return jnp.concatenate([w[..., :1536], w[..., 1552:], w[..., 1536:1552], pad], axis=-1)


def _unpermute_in(w):
    return jnp.concatenate([w[..., :P_CI], w[..., P_Z:P_Z + RANK], w[..., P_CI:P_Z]], axis=-1)


def _group_matrix():
    gi = lax.broadcasted_iota(jnp.int32, (CONV, CONV), 0) // (CONV // GROUPS)
    gj = lax.broadcasted_iota(jnp.int32, (CONV, CONV), 1) // (CONV // GROUPS)
    return jnp.where(gi == gj, GROUPS / CONV, 0.0).astype(BF16)


def _local_grads(x, tgt, g1, w_in_p, wg, bg, gn, conv_w, conv_b, cn_g, cn_b, w_out, g2, w1, w2, gf):
    gmat = _group_matrix()
    proj, xn = _inproj_fwd(x, g1, w_in_p)
    mix_a, o, states = _gla_fwd(proj, wg, bg, gn)
    mix_c, uc = _conv_fwd(proj, conv_w, conv_b, cn_g, cn_b, gmat)
    dh1, dmix, hn, ff, da, dh2, loss, dgf, dg2 = _mlp_fwd_bwd(x, mix_a, mix_c, tgt, w_out, g2, w1, w2, gf)
    dci, dcg, dconv_w, dconv_b, dcn_g, dcn_b = _conv_bwd(proj, uc, dmix, conv_w, cn_g, cn_b, gmat)
    dq, dk, dv, dg, dz, dwg, dbg, dgn = _gla_bwd(proj, o, states, dmix, wg, bg, gn)
    dx, dproj, dg1 = _inproj_bwd(x, g1, w_in_p, dh1, dq, dk, dv, dg, dci, dcg, dz)
    dh1_b = dh1.astype(BF16)
    dw_in_p = _wgrad(xn, dproj, "wgrad_in", 1024, 896)
    dw_out = jnp.concatenate([_wgrad(mix_a, dh1_b, "wgrad_out_a", VAL, 1024),
                              _wgrad(mix_c, dh1_b, "wgrad_out_c", CONV, 1024)], axis=0)
    dw1 = _wgrad(hn, da, "wgrad_mlp_in", 1024, 512, split_n=True)
    dw2 = _wgrad(ff, dh2, "wgrad_mlp_out", 1024, 1024)
    return dict(loss=loss, dx=dx, dg1=dg1, dw_in_p=dw_in_p, dwg=dwg, dbg=dbg, dgn=dgn, dconv_w=dconv_w,
                dconv_b=dconv_b, dcn_g=dcn_g, dcn_b=dcn_b, dw_out=dw_out, dg2=dg2, dw1=dw1, dw2=dw2, dgf=dgf)


_SMALL = [("loss", 8), ("dg1", 8), ("dbg", 2), ("dgn", 1), ("dconv_b", 4), ("dcn_g", 4), ("dcn_b", 4), ("dg2", 8),
          ("dgf", 8), ("dwg", 32), ("dconv_w", 124)]


def _pad8(rows):
    return -(-rows // 8) * 8


def kernel(x, norm1_g, w_in, w_gate_up, b_gate, gla_norm_g, conv_w, conv_b, conv_norm_g, conv_norm_b, w_out, norm2_g, w_mlp_in, w_mlp_out, final_norm_g, loss_target, m_norm1_g, m_w_in, m_w_gate_up, m_b_gate, m_gla_norm_g, m_conv_w, m_conv_b, m_conv_norm_g, m_conv_norm_b, m_w_out, m_norm2_g, m_w_mlp_in, m_w_mlp_out, m_final_norm_g, v_norm1_g, v_w_in, v_w_gate_up, v_b_gate, v_gla_norm_g, v_conv_w, v_conv_b, v_conv_norm_g, v_conv_norm_b, v_w_out, v_norm2_g, v_w_mlp_in, v_w_mlp_out, v_final_norm_g):
    x_idx = lax.axis_index("x")
    y_idx = lax.axis_index("y")
    c_idx = lax.axis_index("c")
    me = 4 * x_idx + 2 * y_idx + c_idx
    pad_in = lambda a: jnp.pad(a, ((0, 0), (0, SHARD_IN_PAD - SHARD_IN)))

    small_shard = jnp.zeros((48, 128), F32)
    small_shard = small_shard.at[0:RANK, 0:KEY // N_DEV].set(w_gate_up[0])
    small_shard = small_shard.at[RANK:RANK + CONV_W, 0:CONV // N_DEV].set(conv_w[0])
    g_in, g_out, g_w1, g_w2, g_small = _all_gather([
        pad_in(w_in[0]).astype(BF16), w_out[0].astype(BF16), w_mlp_in[0].astype(BF16), w_mlp_out[0].astype(BF16),
        small_shard])
    w_in_full = jnp.concatenate([g_in[d, :, :SHARD_IN] for d in range(N_DEV)], axis=1)
    w_in_p = _permute_in(w_in_full)
    w_out_full = g_out.reshape(D_MODEL, D_MODEL)
    w1_full = jnp.concatenate([g_w1[d] for d in range(N_DEV)], axis=1)
    w2_full = g_w2.reshape(D_FF, D_MODEL)
    wg_full = jnp.concatenate([g_small[d, 0:RANK, 0:KEY // N_DEV] for d in range(N_DEV)], axis=1)
    wg_pad = jnp.pad(wg_full, ((0, Z_PAD - RANK), (0, 0))).astype(BF16)
    conv_w_full = jnp.concatenate([g_small[d, RANK:RANK + CONV_W, 0:CONV // N_DEV] for d in range(N_DEV)], axis=1)
    conv_w_pad = jnp.pad(conv_w_full, ((0, HALO - CONV_W), (0, 0)))

    r = _local_grads(x[0], loss_target[0], norm1_g, w_in_p, wg_pad, b_gate, gla_norm_g, conv_w_pad, conv_b,
                     conv_norm_g, conv_norm_b, w_out_full, norm2_g, w1_full, w2_full, final_norm_g.reshape(1, D_MODEL))

    dw_in = _unpermute_in(r["dw_in_p"]).reshape(D_MODEL, N_DEV, SHARD_IN).transpose(1, 0, 2)
    dw_in = jnp.pad(dw_in, ((0, 0), (0, 0), (0, SHARD_IN_PAD - SHARD_IN)))
    r["dwg"] = r["dwg"][0:RANK]
    r["dconv_w"] = r["dconv_w"][0:CONV_W]
    r["loss"] = jnp.zeros((8, 128), F32) + r["loss"]
    pack = jnp.concatenate([jnp.pad(r[name].reshape(rows, 128), ((0, _pad8(rows) - rows), (0, 0)))
                            for name, rows in _SMALL], axis=0)
    (g_pack,) = _all_gather([pack])
    p_in, p_out, p_w1, p_w2 = _exchange([
        dw_in, r["dw_out"].reshape(N_DEV, D_MODEL // N_DEV, D_MODEL), r["dw1"],
        r["dw2"].reshape(N_DEV, D_FF // N_DEV, D_MODEL)])

    gi, di, mi, vi = _adam_big(p_in, pad_in(w_in[0]), pad_in(m_w_in[0]), pad_in(v_w_in[0]), "adam_w_in")
    go, do, mo, vo = _adam_big(p_out, w_out[0], m_w_out[0], v_w_out[0], "adam_w_out")
    ga, da, ma, va = _adam_big(p_w1, w_mlp_in[0], m_w_mlp_in[0], v_w_mlp_in[0], "adam_w_mlp_in")
    gb, db, mb, vb = _adam_big(p_w2, w_mlp_out[0], m_w_mlp_out[0], v_w_mlp_out[0], "adam_w_mlp_out")
    cut = lambda a: a[:, :SHARD_IN][None]

    summed = _sum_small(g_pack)
    small_g = {}
    at = 0
    for name, rows in _SMALL:
        small_g[name] = summed[at:at + rows]
        at += _pad8(rows)
    loss = small_g["loss"][0, 0]
    wg_cols = KEY // N_DEV
    cw_cols = CONV // N_DEV
    g_small_list = [
        small_g["dg1"].reshape(1, D_MODEL),
        lax.dynamic_slice_in_dim(small_g["dwg"].reshape(RANK, KEY), me * wg_cols, wg_cols, axis=1)[None],
        small_g["dbg"].reshape(1, KEY), small_g["dgn"].reshape(1, DV),
        lax.dynamic_slice_in_dim(small_g["dconv_w"].reshape(CONV_W, CONV), me * cw_cols, cw_cols, axis=1)[None],
        small_g["dconv_b"].reshape(1, CONV), small_g["dcn_g"].reshape(1, CONV), small_g["dcn_b"].reshape(1, CONV),
        small_g["dg2"].reshape(1, D_MODEL), small_g["dgf"].reshape(1, D_MODEL),
    ]
    row = lambda a: a.reshape(1, D_MODEL)
    w_small = [norm1_g, w_gate_up, b_gate, gla_norm_g, conv_w, conv_b, conv_norm_g, conv_norm_b, norm2_g,
               row(final_norm_g)]
    m_small = [m_norm1_g, m_w_gate_up, m_b_gate, m_gla_norm_g, m_conv_w, m_conv_b, m_conv_norm_g, m_conv_norm_b,
               m_norm2_g, row(m_final_norm_g)]
    v_small = [v_norm1_g, v_w_gate_up, v_b_gate, v_gla_norm_g, v_conv_w, v_conv_b, v_conv_norm_g, v_conv_norm_b,
               v_norm2_g, row(v_final_norm_g)]
    d_small, nm_small, nv_small = _adam_small(g_small_list, w_small, m_small, v_small)
    flat = lambda lst: list(lst[:-1]) + [lst[-1].reshape(D_MODEL)]
    g_small_list, d_small, nm_small, nv_small = flat(g_small_list), flat(d_small), flat(nm_small), flat(nv_small)

    def order(s, w_in_v, w_out_v, w1_v, w2_v):
        return [s[0], w_in_v, s[1], s[2], s[3], s[4], s[5], s[6], s[7], w_out_v, s[8], w1_v, w2_v, s[9]]

    grads = order(g_small_list, cut(gi), go[None], ga[None], gb[None])
    deltas = order(d_small, cut(di), do[None], da[None], db[None])
    new_m = order(nm_small, cut(mi), mo[None], ma[None], mb[None])
    new_v = order(nv_small, cut(vi), vo[None], va[None], vb[None])
    return (loss, r["dx"][None], *grads, *deltas, *new_m, *new_v)
```

```python
import jax
import jax.numpy as jnp
from jax import lax
from jax.experimental import pallas as pl
from jax.experimental.pallas import tpu as pltpu

F32 = jnp.float32
BF16 = jnp.bfloat16

N_DEV = 8
D_MODEL = 1024
HEADS = 4
DK = 64
DV = 128
KEY = HEADS * DK
VAL = HEADS * DV
RANK = 16
CONV = 512
GROUPS = 8
CONV_W = 31
HALO = 32
SUBLANES = 8
STRIP = 32
D_FF = 4096
D_IN = 2576
SHARD_IN = D_IN // N_DEV
SHARD_IN_PAD = 384
CHUNK = 64
EPS = 1e-6
GATE_TAU = 16.0
Q_SCALE = DK ** -0.5

P_Q, P_K, P_V, P_G, P_CI, P_CG, P_Z = 0, 256, 512, 1024, 1536, 2048, 2560
D_INP = 2688
Z_PAD = D_INP - P_Z

ADAM_LR = 0.001
ADAM_B1 = 0.9
ADAM_B2 = 0.999
ADAM_EPS = 1e-08
ADAM_WD = 0.01
ADAM_STEP = 10

VMEM_LIMIT = 56 * 1024 * 1024

MESH = pl.DeviceIdType.MESH
ANY = pl.BlockSpec(memory_space=pl.ANY)


def _nn(a, b):
    return jnp.dot(a, b, preferred_element_type=F32)


def _nt(a, b):
    return lax.dot_general(a, b, (((1,), (1,)), ((), ())), preferred_element_type=F32)


def _tn(a, b):
    return lax.dot_general(a, b, (((0,), (0,)), ((), ())), preferred_element_type=F32)


def _params(sem=None):
    return pltpu.CompilerParams(dimension_semantics=sem, vmem_limit_bytes=VMEM_LIMIT)


def _const(shape):
    return pl.BlockSpec(shape, lambda *_: (0,) * len(shape), pipeline_mode=pl.Buffered(1))


def _colsum(v):
    return jnp.sum(v, axis=0, keepdims=True)


def _rowmean(v):
    return jnp.mean(v, axis=-1, keepdims=True)


def _split_bf16(v):
    hi = v.astype(BF16)
    return hi, (v - hi.astype(F32)).astype(BF16)


def _my_place():
    return lax.axis_index("x"), lax.axis_index("y"), lax.axis_index("c")


def _peer(j):
    x, y, c = _my_place()
    jx, jy, jc = (j >> 2) & 1, (j >> 1) & 1, j & 1
    px = 1 - x if jx else x
    py = 1 - y if jy else y
    pc = 1 - c if jc else c
    return (px, py, pc), 4 * px + 2 * py + pc


def _comm_plan(kinds, ins, outs, send_sems, recv_sems, local_sems, receives=True):
    x, y, c = _my_place()
    me = 4 * x + 2 * y + c
    own = lambda k, idx: ins[k] if kinds[k] == "gather" else ins[k].at[idx]
    local = [pltpu.make_async_copy(own(k, me), outs[k].at[me], local_sems.at[k]) for k in range(len(kinds))]
    sends, recvs = [], []
    for j in range(1, N_DEV):
        peer, peer_idx = _peer(j)
        for k in range(len(kinds)):
            sems = dict(send_sem=send_sems.at[k, j - 1], recv_sem=recv_sems.at[k, j - 1], device_id=peer,
                        device_id_type=MESH)
            sends.append(pltpu.make_async_remote_copy(src_ref=own(k, peer_idx), dst_ref=outs[k].at[me], **sems))
            if receives:
                recvs.append(pltpu.make_async_remote_copy(src_ref=own(k, me), dst_ref=outs[k].at[peer_idx], **sems))
    return local, sends, recvs


def _comm_start(plan):
    local, sends, _ = plan
    for cp in local + sends:
        cp.start()


def _comm_wait(plan):
    local, sends, recvs = plan
    for cp in recvs:
        cp.wait_recv()
    for cp in sends:
        cp.wait_send()
    for cp in local:
        cp.wait()


def _comm_scratch(n):
    return [pltpu.SemaphoreType.DMA((n, N_DEV - 1)), pltpu.SemaphoreType.DMA((n, N_DEV - 1)),
            pltpu.SemaphoreType.DMA((n,))]


def _comm_out_shapes(kinds, arrays):
    return [jax.ShapeDtypeStruct(((N_DEV,) + a.shape) if kind == "gather" else a.shape, a.dtype)
            for kind, a in zip(kinds, arrays)]


def _comm(kinds, arrays, name):
    n = len(arrays)

    def body(*refs):
        plan = _comm_plan(kinds, refs[:n], refs[n:2 * n], *refs[2 * n:])
        _comm_start(plan)
        _comm_wait(plan)

    return pl.pallas_call(
        body, name=name, out_shape=_comm_out_shapes(kinds, arrays), in_specs=[ANY] * n, out_specs=[ANY] * n,
        scratch_shapes=_comm_scratch(n),
    )(*arrays)


def _hosted_comm(kinds, n_in, n_out, n_comm, n_steps):
    def plan_of(refs, receives):
        ins = refs[n_in:n_in + n_comm]
        outs = refs[n_in + n_comm + n_out:n_in + 2 * n_comm + n_out]
        return _comm_plan(kinds, ins, outs, *refs[-3:], receives=receives)

    def start(refs):
        @pl.when(pl.program_id(0) == 0)
        def _():
            _comm_start(plan_of(refs, False))

    def wait(refs):
        @pl.when(pl.program_id(0) == n_steps - 1)
        def _():
            _comm_wait(plan_of(refs, True))

    return start, wait


def _inproj_fwd(x, g1, w_in_p):
    T = x.shape[0]
    tm = min(T, 512)

    def body(x_ref, g_ref, w_ref, proj_ref, xn_ref):
        xv = x_ref[...]
        r = lax.rsqrt(_rowmean(xv * xv) + EPS)
        xn = (xv * r * g_ref[...]).astype(BF16)
        xn_ref[...] = xn
        proj_ref[...] = _nn(xn, w_ref[...]).astype(BF16)

    return pl.pallas_call(
        body, name="inproj_fwd", grid=(T // tm,),
        out_shape=[jax.ShapeDtypeStruct((T, D_INP), BF16), jax.ShapeDtypeStruct((T, D_MODEL), BF16)],
        in_specs=[pl.BlockSpec((tm, D_MODEL), lambda i: (i, 0)), _const((1, D_MODEL)), _const((D_MODEL, D_INP))],
        out_specs=[pl.BlockSpec((tm, D_INP), lambda i: (i, 0)), pl.BlockSpec((tm, D_MODEL), lambda i: (i, 0))],
        compiler_params=_params(("arbitrary",)),
    )(x, g1, w_in_p)


def _head_masks():
    lane = lax.broadcasted_iota(jnp.int32, (1, KEY), 1)
    return [((lane >= h * DK) & (lane < (h + 1) * DK)).astype(F32) for h in range(HEADS)]


def _chunk_decay(z, wg, bg, tri):
    al = _nn(z, wg) + bg
    la = (jnp.minimum(al, 0.0) - jnp.log(1.0 + jnp.exp(-jnp.abs(al)))) * (1.0 / GATE_TAU)
    hi, lo = _split_bf16(la)
    b = _nn(tri, hi) + _nn(tri, lo)
    row = lax.broadcasted_iota(jnp.int32, la.shape, 0)
    b_last = _colsum(la)
    b_mid = _colsum(jnp.where(row < CHUNK // 2, la, 0.0))
    return al, b, b_last, b_mid


def _gla_fwd(proj, wg, bg, gn, shards):
    T = proj.shape[0]
    tb = min(T, 512)
    cpb = tb // CHUNK
    n_comm = len(shards)
    kinds = ["gather"] * n_comm
    comm_start, comm_wait = _hosted_comm(kinds, 8, 3, n_comm, T // tb)

    def body(*refs):
        q_ref, k_ref, v_ref, g_ref, z_ref, wg_ref, bg_ref, gn_ref = refs[:8]
        mix_ref, o_ref, st_ref = refs[8 + n_comm:11 + n_comm]
        state = refs[11 + 2 * n_comm]
        comm_start(refs)

        @pl.when(pl.program_id(0) == 0)
        def _():
            state[...] = jnp.zeros_like(state)

        r_i = lax.broadcasted_iota(jnp.int32, (CHUNK, CHUNK), 0)
        c_i = lax.broadcasted_iota(jnp.int32, (CHUNK, CHUNK), 1)
        causal = r_i >= c_i
        tri = causal.astype(BF16)
        masks = _head_masks()
        wgv, bgv = wg_ref[...], bg_ref[...]

        def chunk(ci, carry):
            rows = pl.ds(pl.multiple_of(ci * CHUNK, CHUNK), CHUNK)
            _, b, b_last, b_mid = _chunk_decay(z_ref[rows, :], wgv, bgv, tri)
            q = q_ref[rows, :].astype(F32) * Q_SCALE
            k = k_ref[rows, :].astype(F32)
            qe0 = q * jnp.exp(b)
            qem = q * jnp.exp(b - b_mid)
            kem = (k * jnp.exp(b_mid - b)).astype(BF16)
            kdec = k * jnp.exp(b_last - b)
            ebl = jnp.exp(b_last)
            for h in range(HEADS):
                hm = masks[h]
                st = state[h]
                st_b = st.astype(BF16)
                st_ref[ci, h] = st_b
                vh = v_ref[rows, h * DV:(h + 1) * DV]
                a = jnp.where(causal, _nt((qem * hm).astype(BF16), kem), 0.0)
                o_h = _nt((qe0 * hm).astype(BF16), st_b) + _nn(a.astype(BF16), vh)
                o_ref[rows, h * DV:(h + 1) * DV] = o_h
                state[h] = st * ebl + _tn(vh, (kdec * hm).astype(BF16))
            return carry

        lax.fori_loop(0, cpb, chunk, 0, unroll=2)

        gnv = gn_ref[...]
        for h in range(HEADS):
            cols = slice(h * DV, (h + 1) * DV)
            oh = o_ref[:, cols]
            r = lax.rsqrt(_rowmean(oh * oh) + EPS)
            gh = g_ref[:, cols].astype(F32)
            mix_ref[:, cols] = (oh * r * gnv * (gh * jax.nn.sigmoid(gh))).astype(BF16)
        comm_wait(refs)

    nc = T // CHUNK
    res = pl.pallas_call(
        body, name="gla_fwd", grid=(T // tb,),
        out_shape=[jax.ShapeDtypeStruct((T, VAL), BF16), jax.ShapeDtypeStruct((T, VAL), F32),
                   jax.ShapeDtypeStruct((nc, HEADS, DV, KEY), BF16)] + _comm_out_shapes(kinds, shards),
        in_specs=[pl.BlockSpec((tb, KEY), lambda i: (i, P_Q // KEY)), pl.BlockSpec((tb, KEY), lambda i: (i, P_K // KEY)),
                  pl.BlockSpec((tb, VAL), lambda i: (i, P_V // VAL)), pl.BlockSpec((tb, VAL), lambda i: (i, P_G // VAL)),
                  pl.BlockSpec((tb, Z_PAD), lambda i: (i, P_Z // Z_PAD)),
                  _const((Z_PAD, KEY)), _const((1, KEY)), _const((1, DV))] + [ANY] * n_comm,
        out_specs=[pl.BlockSpec((tb, VAL), lambda i: (i, 0)), pl.BlockSpec((tb, VAL), lambda i: (i, 0)),
                   pl.BlockSpec((cpb, HEADS, DV, KEY), lambda i: (i, 0, 0, 0))] + [ANY] * n_comm,
        scratch_shapes=[pltpu.VMEM((HEADS, DV, KEY), F32)] + _comm_scratch(n_comm),
        compiler_params=_params(("arbitrary",)),
    )(proj, proj, proj, proj, proj, wg, bg, gn, *shards)
    return res[0], res[1], res[2], res[3:]


def _group_mean(v, gmat):
    return _nn(v.astype(BF16), gmat)


def _shifted_copies(buf, sh, rows):
    for k in range(1, SUBLANES):
        sh[k - 1] = buf[pl.ds(k, rows), :]


def _tap(buf, sh, off, r0):
    k, base = off % SUBLANES, off - off % SUBLANES
    rows = pl.ds(pl.multiple_of(r0 + base, SUBLANES), STRIP)
    return buf[rows, :] if k == 0 else sh[k - 1, rows, :]


def _conv_fwd(proj, conv_w, conv_b, cn_g, cn_b, gmat):
    T = proj.shape[0]
    tm = min(T, 512)
    sh_rows = tm + HALO - SUBLANES

    def body(ci_ref, cg_ref, w_ref, b_ref, g_ref, be_ref, gm_ref, mix_ref, uc_ref, ubuf, ush):
        @pl.when(pl.program_id(0) == 0)
        def _():
            ubuf[0:HALO, :] = jnp.zeros((HALO, CONV), F32)

        ubuf[HALO:, :] = ci_ref[...].astype(F32) * jax.nn.sigmoid(cg_ref[...].astype(F32))
        _shifted_copies(ubuf, ush, sh_rows)

        def strip(s, carry):
            r0 = pl.multiple_of(s * STRIP, STRIP)
            acc = jnp.zeros((STRIP, CONV), F32) + b_ref[...]
            for j in range(CONV_W):
                acc = acc + w_ref[j:j + 1, :] * _tap(ubuf, ush, HALO - (CONV_W - 1) + j, r0)
            uc_ref[pl.ds(r0, STRIP), :] = acc
            return carry

        lax.fori_loop(0, tm // STRIP, strip, 0)
        ubuf[0:HALO, :] = ubuf[tm:tm + HALO, :]
        gm = gm_ref[...]
        ucv = uc_ref[...]
        d = ucv - _group_mean(ucv, gm)
        var = _group_mean(d * d, gm)
        yn = d * lax.rsqrt(var + EPS) * g_ref[...] + be_ref[...]
        mix_ref[...] = (yn * jax.nn.sigmoid(yn)).astype(BF16)

    return pl.pallas_call(
        body, name="conv_fwd", grid=(T // tm,),
        out_shape=[jax.ShapeDtypeStruct((T, CONV), BF16), jax.ShapeDtypeStruct((T, CONV), F32)],
        in_specs=[pl.BlockSpec((tm, CONV), lambda i: (i, P_CI // CONV)), pl.BlockSpec((tm, CONV), lambda i: (i, P_CG // CONV)),
                  _const((HALO, CONV)), _const((1, CONV)), _const((1, CONV)), _const((1, CONV)), _const((CONV, CONV))],
        out_specs=[pl.BlockSpec((tm, CONV), lambda i: (i, 0)), pl.BlockSpec((tm, CONV), lambda i: (i, 0))],
        scratch_shapes=[pltpu.VMEM((tm + HALO, CONV), F32), pltpu.VMEM((SUBLANES - 1, sh_rows, CONV), F32)],
        compiler_params=_params(("arbitrary",)),
    )(proj, proj, conv_w, conv_b, cn_g, cn_b, gmat)


def _rms_bwd(dy, xhat, r, g):
    dyg = dy * g
    return r * (dyg - xhat * _rowmean(dyg * xhat))


def _mlp_fwd_bwd(x, mix_a, mix_c, tgt, w_out, g2, w1, w2, gf):
    T = x.shape[0]
    tm = min(T, 256)
    inv_d = 1.0 / D_MODEL

    def body(x_ref, ma_ref, mc_ref, t_ref, wo_ref, g2_ref, w1_ref, w2_ref, gf_ref,
             dh1_ref, dmix_ref, hn_ref, ff_ref, da_ref, dh2_ref, loss_ref, dgf_ref, dg2_ref):
        @pl.when(pl.program_id(0) == 0)
        def _():
            loss_ref[...] = jnp.zeros_like(loss_ref)
            dgf_ref[...] = jnp.zeros_like(dgf_ref)
            dg2_ref[...] = jnp.zeros_like(dg2_ref)

        g2v, gfv = g2_ref[...], gf_ref[...]
        h1 = x_ref[...] + _nn(ma_ref[...], wo_ref[0:VAL, :]) + _nn(mc_ref[...], wo_ref[VAL:, :])
        r2 = lax.rsqrt(_rowmean(h1 * h1) + EPS)
        h1hat = h1 * r2
        hn = (h1hat * g2v).astype(BF16)
        hn_ref[...] = hn
        relu_a = jnp.maximum(_nn(hn, w1_ref[...]), 0.0)
        ff = (relu_a * relu_a).astype(BF16)
        ff_ref[...] = ff
        h2 = h1 + _nn(ff, w2_ref[...])
        rf = lax.rsqrt(_rowmean(h2 * h2) + EPS)
        h2hat = h2 * rf
        err = h2hat * gfv - t_ref[...]
        loss_ref[...] += (0.5 * inv_d) * _colsum(jnp.sum(err * err, axis=1, keepdims=True))
        dy = err * inv_d
        dgf_ref[...] += _colsum(dy * h2hat)
        dh2 = _rms_bwd(dy, h2hat, rf, gfv)
        dh2_b = dh2.astype(BF16)
        dh2_ref[...] = dh2_b
        da = (_nt(dh2_b, w2_ref[...]) * (2.0 * relu_a)).astype(BF16)
        da_ref[...] = da
        dhn = _nt(da, w1_ref[...])
        dg2_ref[...] += _colsum(dhn * h1hat)
        dh1 = dh2 + _rms_bwd(dhn, h1hat, r2, g2v)
        dh1_ref[...] = dh1
        dmix_ref[...] = _nt(dh1.astype(BF16), wo_ref[...]).astype(BF16)

    tok = lambda w: pl.BlockSpec((tm, w), lambda i: (i, 0))
    return pl.pallas_call(
        body, name="mlp_fwd_bwd", grid=(T // tm,),
        out_shape=[jax.ShapeDtypeStruct((T, D_MODEL), F32), jax.ShapeDtypeStruct((T, D_MODEL), BF16),
                   jax.ShapeDtypeStruct((T, D_MODEL), BF16), jax.ShapeDtypeStruct((T, D_FF), BF16),
                   jax.ShapeDtypeStruct((T, D_FF), BF16), jax.ShapeDtypeStruct((T, D_MODEL), BF16),
                   jax.ShapeDtypeStruct((1, 1), F32), jax.ShapeDtypeStruct((1, D_MODEL), F32),
                   jax.ShapeDtypeStruct((1, D_MODEL), F32)],
        in_specs=[tok(D_MODEL), tok(VAL), tok(CONV), tok(D_MODEL), _const((D_MODEL, D_MODEL)), _const((1, D_MODEL)),
                  _const((D_MODEL, D_FF)), _const((D_FF, D_MODEL)), _const((1, D_MODEL))],
        out_specs=[tok(D_MODEL), tok(D_MODEL), tok(D_MODEL), tok(D_FF), tok(D_FF), tok(D_MODEL),
                   pl.BlockSpec((1, 1), lambda i: (0, 0)), pl.BlockSpec((1, D_MODEL), lambda i: (0, 0)),
                   pl.BlockSpec((1, D_MODEL), lambda i: (0, 0))],
        compiler_params=_params(("arbitrary",)),
    )(x, mix_a, mix_c, tgt, w_out, g2, w1, w2, gf)


def _silu_grad(v, s):
    return s * (1.0 + v * (1.0 - s))


def _conv_bwd(proj, uc, dmix, conv_w, cn_g, cn_b, gmat):
    T = proj.shape[0]
    tm = min(T, 512)
    nt = T // tm
    hb = tm // HALO
    sh_rows = tm + HALO - SUBLANES
    n_strips = tm // STRIP

    def body(ci_ref, cg_ref, cih_ref, cgh_ref, uc_ref, dm_ref, w_ref, g_ref, be_ref, gm_ref,
             dci_ref, dcg_ref, dw_ref, db_ref, dg_ref, dbe_ref, ubuf, ush, dbuf, dsh):
        step = pl.program_id(0)

        @pl.when(step == 0)
        def _():
            dbuf[tm:, :] = jnp.zeros((HALO, CONV), F32)
            dw_ref[...] = jnp.zeros_like(dw_ref)
            db_ref[...] = jnp.zeros_like(db_ref)
            dg_ref[...] = jnp.zeros_like(dg_ref)
            dbe_ref[...] = jnp.zeros_like(dbe_ref)

        ubuf[HALO:, :] = ci_ref[...].astype(F32) * jax.nn.sigmoid(cg_ref[...].astype(F32))
        halo = cih_ref[...].astype(F32) * jax.nn.sigmoid(cgh_ref[...].astype(F32))
        ubuf[0:HALO, :] = jnp.where(step == nt - 1, 0.0, halo)
        _shifted_copies(ubuf, ush, sh_rows)

        gm, gv = gm_ref[...], g_ref[...]
        ucv = uc_ref[...]
        d = ucv - _group_mean(ucv, gm)
        rs = lax.rsqrt(_group_mean(d * d, gm) + EPS)
        yhat = d * rs
        yn = yhat * gv + be_ref[...]
        dyn = dm_ref[...].astype(F32) * _silu_grad(yn, jax.nn.sigmoid(yn))
        dg_ref[...] += _colsum(dyn * yhat)
        dbe_ref[...] += _colsum(dyn)
        dyh = dyn * gv
        duc = rs * (dyh - _group_mean(dyh, gm) - yhat * _group_mean(dyh * yhat, gm))
        db_ref[...] += _colsum(duc)
        dbuf[0:tm, :] = duc
        _shifted_copies(dbuf, dsh, sh_rows)

        def du_strip(s, carry):
            r0 = pl.multiple_of(s * STRIP, STRIP)
            rows = pl.ds(r0, STRIP)
            du = jnp.zeros((STRIP, CONV), F32)
            for j in range(CONV_W):
                du = du + w_ref[j:j + 1, :] * _tap(dbuf, dsh, CONV_W - 1 - j, r0)
            cin = ci_ref[rows, :].astype(F32)
            sg = jax.nn.sigmoid(cg_ref[rows, :].astype(F32))
            dci_ref[rows, :] = (du * sg).astype(BF16)
            dcg_ref[rows, :] = (du * cin * sg * (1.0 - sg)).astype(BF16)
            return carry

        lax.fori_loop(0, n_strips, du_strip, 0)

        for j in range(CONV_W):
            def dw_strip(s, acc, off=HALO - (CONV_W - 1) + j):
                r0 = pl.multiple_of(s * STRIP, STRIP)
                p = dbuf[pl.ds(r0, STRIP), :] * _tap(ubuf, ush, off, r0)
                for q in range(STRIP // SUBLANES):
                    acc = acc + p[q * SUBLANES:(q + 1) * SUBLANES, :]
                return acc

            acc = lax.fori_loop(0, n_strips, dw_strip, jnp.zeros((SUBLANES, CONV), F32))
            dw_ref[j:j + 1, :] += _colsum(acc)
        dbuf[tm:, :] = dbuf[0:HALO, :]

    rev = lambda i: nt - 1 - i
    halo_idx = lambda i: jnp.maximum(rev(i) * hb - 1, 0)
    tile = lambda col: pl.BlockSpec((tm, CONV), lambda i: (rev(i), col))
    acc = lambda rows: pl.BlockSpec((rows, CONV), lambda i: (0, 0))
    return pl.pallas_call(
        body, name="conv_bwd", grid=(nt,),
        out_shape=[jax.ShapeDtypeStruct((T, CONV), BF16), jax.ShapeDtypeStruct((T, CONV), BF16),
                   jax.ShapeDtypeStruct((HALO, CONV), F32), jax.ShapeDtypeStruct((1, CONV), F32),
                   jax.ShapeDtypeStruct((1, CONV), F32), jax.ShapeDtypeStruct((1, CONV), F32)],
        in_specs=[tile(P_CI // CONV), tile(P_CG // CONV),
                  pl.BlockSpec((HALO, CONV), lambda i: (halo_idx(i), P_CI // CONV)),
                  pl.BlockSpec((HALO, CONV), lambda i: (halo_idx(i), P_CG // CONV)),
                  tile(0), tile(1), _const((HALO, CONV)), _const((1, CONV)), _const((1, CONV)), _const((CONV, CONV))],
        out_specs=[tile(0), tile(0), acc(HALO), acc(1), acc(1), acc(1)],
        scratch_shapes=[pltpu.VMEM((tm + HALO, CONV), F32), pltpu.VMEM((SUBLANES - 1, sh_rows, CONV), F32),
                        pltpu.VMEM((tm + HALO, CONV), F32), pltpu.VMEM((SUBLANES - 1, sh_rows, CONV), F32)],
        compiler_params=_params(("arbitrary",)),
    )(proj, proj, proj, proj, uc, dmix, conv_w, cn_g, cn_b, gmat)


def _gla_bwd(proj, o, states, dmix, wg, bg, gn, parts):
    T = proj.shape[0]
    tb = min(T, 512)
    cpb = tb // CHUNK
    nb = T // tb
    n_comm = len(parts)
    kinds = ["exchange"] * n_comm
    comm_start, comm_wait = _hosted_comm(kinds, 11, 8, n_comm, nb)

    def body(*refs):
        q_ref, k_ref, v_ref, g_ref, z_ref, o_ref, st_ref, dm_ref, wg_ref, bg_ref, gn_ref = refs[:11]
        dq_ref, dk_ref, dv_ref, dg_ref, dz_ref, dwg_ref, dbg_ref, dgn_ref = refs[11 + n_comm:19 + n_comm]
        dstate, do_scr = refs[19 + 2 * n_comm:21 + 2 * n_comm]
        comm_start(refs)

        @pl.when(pl.program_id(0) == 0)
        def _():
            dstate[...] = jnp.zeros_like(dstate)
            dwg_ref[...] = jnp.zeros_like(dwg_ref)
            dbg_ref[...] = jnp.zeros_like(dbg_ref)
            dgn_ref[...] = jnp.zeros_like(dgn_ref)

        gnv = gn_ref[...]
        dgn = jnp.zeros((1, DV), F32)
        for h in range(HEADS):
            cols = slice(h * DV, (h + 1) * DV)
            oh = o_ref[:, cols]
            r = lax.rsqrt(_rowmean(oh * oh) + EPS)
            ohat = oh * r
            gh = g_ref[:, cols].astype(F32)
            sg = jax.nn.sigmoid(gh)
            dmx = dm_ref[:, cols].astype(F32)
            don = dmx * (gh * sg)
            dg_ref[:, cols] = (dmx * (ohat * gnv) * _silu_grad(gh, sg)).astype(BF16)
            dgn = dgn + _colsum(don * ohat)
            do_scr[:, cols] = _rms_bwd(don, ohat, r, gnv)
        dgn_ref[...] += dgn

        r_i = lax.broadcasted_iota(jnp.int32, (CHUNK, CHUNK), 0)
        c_i = lax.broadcasted_iota(jnp.int32, (CHUNK, CHUNK), 1)
        causal = r_i >= c_i
        causal_t = r_i <= c_i
        tri = causal.astype(BF16)
        tri_t = causal_t.astype(BF16)
        masks = _head_masks()
        wgv, bgv = wg_ref[...], bg_ref[...]

        def chunk(idx, carry):
            dwg, dbg = carry
            ci = cpb - 1 - idx
            rows = pl.ds(pl.multiple_of(ci * CHUNK, CHUNK), CHUNK)
            zc = z_ref[rows, :]
            al, b, b_last, b_mid = _chunk_decay(zc, wgv, bgv, tri)
            q = q_ref[rows, :].astype(F32) * Q_SCALE
            k = k_ref[rows, :].astype(F32)
            eb = jnp.exp(b)
            ebm = jnp.exp(b - b_mid)
            emb = jnp.exp(b_mid - b)
            elb = jnp.exp(b_last - b)
            ebl = jnp.exp(b_last)
            qe0 = q * eb
            qem_b = (q * ebm).astype(BF16)
            kem_b = (k * emb).astype(BF16)
            kdec = k * elb
            dq = jnp.zeros((CHUNK, KEY), F32)
            dk = jnp.zeros((CHUNK, KEY), F32)
            kst = jnp.zeros((CHUNK, KEY), F32)
            sds = jnp.zeros((1, KEY), F32)
            for h in range(HEADS):
                hm = masks[h]
                cols = slice(h * DV, (h + 1) * DV)
                st_b = st_ref[ci, h]
                ds = dstate[h]
                ds_b = ds.astype(BF16)
                do_b = do_scr[rows, cols].astype(BF16)
                vh = v_ref[rows, cols]
                da = jnp.where(causal, _nt(do_b, vh), 0.0).astype(BF16)
                da_t = jnp.where(causal_t, _nt(vh, do_b), 0.0).astype(BF16)
                a_t = jnp.where(causal_t, _nt(kem_b, (q * ebm * hm).astype(BF16)), 0.0).astype(BF16)
                dk_state = _nn(vh, ds_b) * elb
                dq = dq + hm * (_nn(do_b, st_b) * eb + _nn(da, kem_b) * ebm)
                dk = dk + hm * (_nn(da_t, qem_b) * emb + dk_state)
                kst = kst + hm * (k * dk_state)
                sds = sds + hm * _colsum(st_b.astype(F32) * ds)
                dv_ref[rows, cols] = (_nn(a_t, do_b) + _nt((kdec * hm).astype(BF16), ds_b)).astype(BF16)
                dstate[h] = ds * ebl + _tn(do_b, (qe0 * hm).astype(BF16))
            dq_ref[rows, :] = (dq * Q_SCALE).astype(BF16)
            dk_ref[rows, :] = dk.astype(BF16)
            db = q * dq - k * dk
            hi, lo = _split_bf16(db)
            dla = _nn(tri_t, hi) + _nn(tri_t, lo) + _colsum(kst) + ebl * sds
            dal = (dla * (1.0 / GATE_TAU) * jax.nn.sigmoid(-al))
            dal_b = dal.astype(BF16)
            dz_ref[rows, :] = _nt(dal_b, wgv).astype(BF16)
            return dwg + _tn(zc, dal_b), dbg + _colsum(dal)

        dwg, dbg = lax.fori_loop(0, cpb, chunk, (jnp.zeros((Z_PAD, KEY), F32), jnp.zeros((1, KEY), F32)), unroll=2)
        dwg_ref[...] += dwg
        dbg_ref[...] += dbg
        comm_wait(refs)

    rev = lambda i: nb - 1 - i
    blk = lambda w, col: pl.BlockSpec((tb, w), lambda i: (rev(i), col))
    res = pl.pallas_call(
        body, name="gla_bwd", grid=(nb,),
        out_shape=[jax.ShapeDtypeStruct((T, KEY), BF16), jax.ShapeDtypeStruct((T, KEY), BF16),
                   jax.ShapeDtypeStruct((T, VAL), BF16), jax.ShapeDtypeStruct((T, VAL), BF16),
                   jax.ShapeDtypeStruct((T, Z_PAD), BF16), jax.ShapeDtypeStruct((Z_PAD, KEY), F32),
                   jax.ShapeDtypeStruct((1, KEY), F32), jax.ShapeDtypeStruct((1, DV), F32)]
        + _comm_out_shapes(kinds, parts),
        in_specs=[blk(KEY, P_Q // KEY), blk(KEY, P_K // KEY), blk(VAL, P_V // VAL), blk(VAL, P_G // VAL),
                  blk(Z_PAD, P_Z // Z_PAD), blk(VAL, 0),
                  pl.BlockSpec((cpb, HEADS, DV, KEY), lambda i: (rev(i), 0, 0, 0)), blk(VAL, 0),
                  _const((Z_PAD, KEY)), _const((1, KEY)), _const((1, DV))] + [ANY] * n_comm,
        out_specs=[blk(KEY, 0), blk(KEY, 0), blk(VAL, 0), blk(VAL, 0), blk(Z_PAD, 0),
                   pl.BlockSpec((Z_PAD, KEY), lambda i: (0, 0)), pl.BlockSpec((1, KEY), lambda i: (0, 0)),
                   pl.BlockSpec((1, DV), lambda i: (0, 0))] + [ANY] * n_comm,
        scratch_shapes=[pltpu.VMEM((HEADS, DV, KEY), F32), pltpu.VMEM((tb, VAL), F32)] + _comm_scratch(n_comm),
        compiler_params=_params(("arbitrary",)),
    )(proj, proj, proj, proj, proj, o, states, dmix, wg, bg, gn, *parts)
    return res[:8], res[8:]


def _inproj_bwd(x, g1, w_in_p, dh1, dq, dk, dv, dg, dci, dcg, dz):
    T = x.shape[0]
    tm = min(T, 512)

    def body(x_ref, g_ref, w_ref, dh1_ref, dq_ref, dk_ref, dv_ref, dg_ref, dci_ref, dcg_ref, dz_ref,
             dx_ref, dp_ref, dg1_ref):
        @pl.when(pl.program_id(0) == 0)
        def _():
            dg1_ref[...] = jnp.zeros_like(dg1_ref)

        dp_ref[:, P_Q:P_K] = dq_ref[...]
        dp_ref[:, P_K:P_V] = dk_ref[...]
        dp_ref[:, P_V:P_G] = dv_ref[...]
        dp_ref[:, P_G:P_CI] = dg_ref[...]
        dp_ref[:, P_CI:P_CG] = dci_ref[...]
        dp_ref[:, P_CG:P_Z] = dcg_ref[...]
        dp_ref[:, P_Z:] = dz_ref[...]
        dxn = _nt(dp_ref[...], w_ref[...])
        xv = x_ref[...]
        r = lax.rsqrt(_rowmean(xv * xv) + EPS)
        xhat = xv * r
        dg1_ref[...] += _colsum(dxn * xhat)
        dx_ref[...] = dh1_ref[...] + _rms_bwd(dxn, xhat, r, g_ref[...])

    tok = lambda w: pl.BlockSpec((tm, w), lambda i: (i, 0))
    return pl.pallas_call(
        body, name="inproj_bwd", grid=(T // tm,),
        out_shape=[jax.ShapeDtypeStruct((T, D_MODEL), F32), jax.ShapeDtypeStruct((T, D_INP), BF16),
                   jax.ShapeDtypeStruct((1, D_MODEL), F32)],
        in_specs=[tok(D_MODEL), _const((1, D_MODEL)), _const((D_MODEL, D_INP)), tok(D_MODEL), tok(KEY), tok(KEY),
                  tok(VAL), tok(VAL), tok(CONV), tok(CONV), tok(Z_PAD)],
        out_specs=[tok(D_MODEL), tok(D_INP), pl.BlockSpec((1, D_MODEL), lambda i: (0, 0))],
        compiler_params=_params(("arbitrary",)),
    )(x, g1, w_in_p, dh1, dq, dk, dv, dg, dci, dcg, dz)


def _wgrad(a, b, name, tk, tn, split_n=False):
    T, K = a.shape
    N = b.shape[1]
    tt = min(T, 1024)
    nt = T // tt

    def body(a_ref, b_ref, o_ref, acc):
        @pl.when(pl.program_id(2) == 0)
        def _():
            acc[...] = jnp.zeros_like(acc)

        acc[...] += _tn(a_ref[...], b_ref[...].astype(BF16))

        @pl.when(pl.program_id(2) == nt - 1)
        def _():
            o_ref[...] = acc[...].astype(BF16)

    if split_n:
        assert tk == K
        out_shape = jax.ShapeDtypeStruct((N // tn, K, tn), BF16)
        out_spec = pl.BlockSpec((None, tk, tn), lambda i, j, t: (j, 0, 0))
    else:
        out_shape = jax.ShapeDtypeStruct((K, N), BF16)
        out_spec = pl.BlockSpec((tk, tn), lambda i, j, t: (i, j))
    return pl.pallas_call(
        body, name=name, grid=(K // tk, N // tn, nt), out_shape=out_shape,
        in_specs=[pl.BlockSpec((tt, tk), lambda i, j, t: (t, i)), pl.BlockSpec((tt, tn), lambda i, j, t: (t, j))],
        out_specs=out_spec, scratch_shapes=[pltpu.VMEM((tk, tn), F32)],
        compiler_params=_params(("arbitrary", "arbitrary", "arbitrary")),
    )(a, b)


def _adam_math(w, g, m, v):
    m = ADAM_B1 * m + (1.0 - ADAM_B1) * g
    v = ADAM_B2 * v + (1.0 - ADAM_B2) * (g * g)
    m_hat = m / (1.0 - ADAM_B1 ** ADAM_STEP)
    v_hat = v / (1.0 - ADAM_B2 ** ADAM_STEP)
    delta = -ADAM_LR * (m_hat / (jnp.sqrt(v_hat) + ADAM_EPS) + ADAM_WD * w)
    return delta, m, v


def _sum8(ref):
    g = ref[0].astype(F32)
    for s in range(1, N_DEV):
        g = g + ref[s].astype(F32)
    return g


def _adam_big(parts, w, m, v, name):
    R, C = w.shape
    tr = min(R, 128)

    def body(p_ref, w_ref, m_ref, v_ref, g_ref, d_ref, nm_ref, nv_ref):
        g = _sum8(p_ref)
        g_ref[...] = g
        d_ref[...], nm_ref[...], nv_ref[...] = _adam_math(w_ref[...], g, m_ref[...], v_ref[...])

    row = pl.BlockSpec((tr, C), lambda i: (i, 0))
    return pl.pallas_call(
        body, name=name, grid=(R // tr,), out_shape=[jax.ShapeDtypeStruct((R, C), F32)] * 4,
        in_specs=[pl.BlockSpec((N_DEV, tr, C), lambda i: (0, i, 0)), row, row, row], out_specs=[row] * 4,
        compiler_params=_params(("arbitrary",)),
    )(parts, w, m, v)


def _sum_small(parts):
    def body(p_ref, o_ref):
        o_ref[...] = _sum8(p_ref)

    return pl.pallas_call(body, name="sum_small", out_shape=jax.ShapeDtypeStruct(parts.shape[1:], F32))(parts)


def _adam_small(gs, ws, ms, vs):
    n = len(gs)

    def body(*refs):
        g_refs, w_refs, m_refs, v_refs = refs[:n], refs[n:2 * n], refs[2 * n:3 * n], refs[3 * n:4 * n]
        outs = refs[4 * n:]
        for i in range(n):
            d, nm, nv = _adam_math(w_refs[i][...], g_refs[i][...], m_refs[i][...], v_refs[i][...])
            outs[i][...] = d
            outs[n + i][...] = nm
            outs[2 * n + i][...] = nv

    shapes = [jax.ShapeDtypeStruct(w.shape, F32) for w in ws]
    res = pl.pallas_call(body, name="adam_small", out_shape=shapes * 3)(*gs, *ws, *ms, *vs)
    return res[:n], res[n:2 * n], res[2 * n:]


def _permute_in(w):
    pad = jnp.zeros(w.shape[:-1] + (D_INP - D_IN,), w.dtype)
    return jnp.concatenate([w[..., :1536], w[..., 1552:], w[..., 1536:1552], pad], axis=-1)


def _unpermute_in(w):
    return jnp.concatenate([w[..., :P_CI], w[..., P_Z:P_Z + RANK], w[..., P_CI:P_Z]], axis=-1)


def _group_matrix():
    gi = lax.broadcasted_iota(jnp.int32, (CONV, CONV), 0) // (CONV // GROUPS)
    gj = lax.broadcasted_iota(jnp.int32, (CONV, CONV), 1) // (CONV // GROUPS)
    return jnp.where(gi == gj, GROUPS / CONV, 0.0).astype(BF16)


_SMALL = [("loss", 8), ("dg1", 8), ("dbg", 2), ("dgn", 1), ("dconv_b", 4), ("dcn_g", 4), ("dcn_b", 4), ("dg2", 8),
          ("dgf", 8), ("dwg", 32), ("dconv_w", 124)]


def _pad8(rows):
    return -(-rows // 8) * 8


def kernel(x, norm1_g, w_in, w_gate_up, b_gate, gla_norm_g, conv_w, conv_b, conv_norm_g, conv_norm_b, w_out, norm2_g, w_mlp_in, w_mlp_out, final_norm_g, loss_target, m_norm1_g, m_w_in, m_w_gate_up, m_b_gate, m_gla_norm_g, m_conv_w, m_conv_b, m_conv_norm_g, m_conv_norm_b, m_w_out, m_norm2_g, m_w_mlp_in, m_w_mlp_out, m_final_norm_g, v_norm1_g, v_w_in, v_w_gate_up, v_b_gate, v_gla_norm_g, v_conv_w, v_conv_b, v_conv_norm_g, v_conv_norm_b, v_w_out, v_norm2_g, v_w_mlp_in, v_w_mlp_out, v_final_norm_g):
    x_idx = lax.axis_index("x")
    y_idx = lax.axis_index("y")
    c_idx = lax.axis_index("c")
    me = 4 * x_idx + 2 * y_idx + c_idx
    pad_in = lambda a: jnp.pad(a, ((0, 0), (0, SHARD_IN_PAD - SHARD_IN)))
    xs, tgt = x[0], loss_target[0]
    gf = final_norm_g.reshape(1, D_MODEL)
    gmat = _group_matrix()

    small_shard = jnp.zeros((48, 128), F32)
    small_shard = small_shard.at[0:RANK, 0:KEY // N_DEV].set(w_gate_up[0])
    small_shard = small_shard.at[RANK:RANK + CONV_W, 0:CONV // N_DEV].set(conv_w[0])
    g_in, g_small = _comm(["gather", "gather"], [pad_in(w_in[0]).astype(BF16), small_shard], "gather_w_in")
    w_in_p = _permute_in(jnp.concatenate([g_in[d, :, :SHARD_IN] for d in range(N_DEV)], axis=1))
    wg_full = jnp.concatenate([g_small[d, 0:RANK, 0:KEY // N_DEV] for d in range(N_DEV)], axis=1)
    wg_pad = jnp.pad(wg_full, ((0, Z_PAD - RANK), (0, 0))).astype(BF16)
    conv_w_full = jnp.concatenate([g_small[d, RANK:RANK + CONV_W, 0:CONV // N_DEV] for d in range(N_DEV)], axis=1)
    conv_w_pad = jnp.pad(conv_w_full, ((0, HALO - CONV_W), (0, 0)))

    proj, xn = _inproj_fwd(xs, norm1_g, w_in_p)
    mix_a, o, states, (g_out, g_w1, g_w2) = _gla_fwd(
        proj, wg_pad, b_gate, gla_norm_g,
        [w_out[0].astype(BF16), w_mlp_in[0].astype(BF16), w_mlp_out[0].astype(BF16)])
    w_out_full = g_out.reshape(D_MODEL, D_MODEL)
    w1_full = jnp.concatenate([g_w1[d] for d in range(N_DEV)], axis=1)
    w2_full = g_w2.reshape(D_FF, D_MODEL)
    mix_c, uc = _conv_fwd(proj, conv_w_pad, conv_b, conv_norm_g, conv_norm_b, gmat)
    dh1, dmix, hn, ff, da, dh2, loss, dgf, dg2 = _mlp_fwd_bwd(xs, mix_a, mix_c, tgt, w_out_full, norm2_g, w1_full,
                                                              w2_full, gf)

    dw1 = _wgrad(hn, da, "wgrad_mlp_in", 1024, 512, split_n=True)
    dw2 = _wgrad(ff, dh2, "wgrad_mlp_out", 1024, 1024)
    dw_out = jnp.concatenate([_wgrad(mix_a, dh1, "wgrad_out_a", VAL, 1024),
                              _wgrad(mix_c, dh1, "wgrad_out_c", CONV, 1024)], axis=0)
    dci, dcg, dconv_w, dconv_b, dcn_g, dcn_b = _conv_bwd(proj, uc, dmix, conv_w_pad, conv_norm_g, conv_norm_b, gmat)
    (dq, dk, dv, dg, dz, dwg, dbg, dgn), (p_w1, p_w2, p_out) = _gla_bwd(
        proj, o, states, dmix, wg_pad, b_gate, gla_norm_g,
        [dw1, dw2.reshape(N_DEV, D_FF // N_DEV, D_MODEL), dw_out.reshape(N_DEV, D_MODEL // N_DEV, D_MODEL)])
    dx, dproj, dg1 = _inproj_bwd(xs, norm1_g, w_in_p, dh1, dq, dk, dv, dg, dci, dcg, dz)
    dw_in_p = _wgrad(xn, dproj, "wgrad_in", 1024, 896)

    dw_in = _unpermute_in(dw_in_p).reshape(D_MODEL, N_DEV, SHARD_IN).transpose(1, 0, 2)
    dw_in = jnp.pad(dw_in, ((0, 0), (0, 0), (0, SHARD_IN_PAD - SHARD_IN)))
    small = dict(loss=jnp.zeros((8, 128), F32) + loss, dg1=dg1, dbg=dbg, dgn=dgn, dconv_b=dconv_b, dcn_g=dcn_g,
                 dcn_b=dcn_b, dg2=dg2, dgf=dgf, dwg=dwg[0:RANK], dconv_w=dconv_w[0:CONV_W])
    pack = jnp.concatenate([jnp.pad(small[name].reshape(rows, 128), ((0, _pad8(rows) - rows), (0, 0)))
                            for name, rows in _SMALL], axis=0)
    p_in, g_pack = _comm(["exchange", "gather"], [dw_in, pack], "exchange_w_in")

    gi, di, mi, vi = _adam_big(p_in, pad_in(w_in[0]), pad_in(m_w_in[0]), pad_in(v_w_in[0]), "adam_w_in")
    go, do, mo, vo = _adam_big(p_out, w_out[0], m_w_out[0], v_w_out[0], "adam_w_out")
    ga, da_, ma, va = _adam_big(p_w1, w_mlp_in[0], m_w_mlp_in[0], v_w_mlp_in[0], "adam_w_mlp_in")
    gb, db, mb, vb = _adam_big(p_w2, w_mlp_out[0], m_w_mlp_out[0], v_w_mlp_out[0], "adam_w_mlp_out")
    cut = lambda a: a[:, :SHARD_IN][None]

    summed = _sum_small(g_pack)
    small_g = {}
    at = 0
    for name, rows in _SMALL:
        small_g[name] = summed[at:at + rows]
        at += _pad8(rows)
    loss_out = small_g["loss"][0, 0]
    wg_cols = KEY // N_DEV
    cw_cols = CONV // N_DEV
    g_small_list = [
        small_g["dg1"].reshape(1, D_MODEL),
        lax.dynamic_slice_in_dim(small_g["dwg"].reshape(RANK, KEY), me * wg_cols, wg_cols, axis=1)[None],
        small_g["dbg"].reshape(1, KEY), small_g["dgn"].reshape(1, DV),
        lax.dynamic_slice_in_dim(small_g["dconv_w"].reshape(CONV_W, CONV), me * cw_cols, cw_cols, axis=1)[None],
        small_g["dconv_b"].reshape(1, CONV), small_g["dcn_g"].reshape(1, CONV), small_g["dcn_b"].reshape(1, CONV),
        small_g["dg2"].reshape(1, D_MODEL), small_g["dgf"].reshape(1, D_MODEL),
    ]
    row = lambda a: a.reshape(1, D_MODEL)
    w_small = [norm1_g, w_gate_up, b_gate, gla_norm_g, conv_w, conv_b, conv_norm_g, conv_norm_b, norm2_g,
               row(final_norm_g)]
    m_small = [m_norm1_g, m_w_gate_up, m_b_gate, m_gla_norm_g, m_conv_w, m_conv_b, m_conv_norm_g, m_conv_norm_b,
               m_norm2_g, row(m_final_norm_g)]
    v_small = [v_norm1_g, v_w_gate_up, v_b_gate, v_gla_norm_g, v_conv_w, v_conv_b, v_conv_norm_g, v_conv_norm_b,
               v_norm2_g, row(v_final_norm_g)]
    d_small, nm_small, nv_small = _adam_small(g_small_list, w_small, m_small, v_small)
    flat = lambda lst: list(lst[:-1]) + [lst[-1].reshape(D_MODEL)]
    g_small_list, d_small, nm_small, nv_small = flat(g_small_list), flat(d_small), flat(nm_small), flat(nv_small)

    def order(s, w_in_v, w_out_v, w1_v, w2_v):
        return [s[0], w_in_v, s[1], s[2], s[3], s[4], s[5], s[6], s[7], w_out_v, s[8], w1_v, w2_v, s[9]]

    grads = order(g_small_list, cut(gi), go[None], ga[None], gb[None])
    deltas = order(d_small, cut(di), do[None], da_[None], db[None])
    new_m = order(nm_small, cut(mi), mo[None], ma[None], mb[None])
    new_v = order(nv_small, cut(vi), vo[None], va[None], vb[None])
    return (loss_out, dx[None], *grads, *deltas, *new_m, *new_v)
```

```python
from typing import NamedTuple

import jax
import jax.numpy as jnp
from jax import lax
from jax.experimental import pallas as pl
from jax.experimental.pallas import tpu as pltpu

F32 = jnp.float32
BF16 = jnp.bfloat16

N_DEV = 8
D_MODEL = 1024
HEADS = 4
DK = 64
DV = 128
KEY = HEADS * DK
VAL = HEADS * DV
RANK = 16
CONV = 512
GROUPS = 8
CONV_W = 31
HALO = 32
SUBLANES = 8
STRIP = 32
D_FF = 4096
D_IN = 2576
SHARD_IN = D_IN // N_DEV
SHARD_IN_PAD = 384
CHUNK = 64
SUB = 256
EPS = 1e-6
GATE_TAU = 16.0
Q_SCALE = DK ** -0.5

P_Q, P_K, P_V, P_G, P_CI, P_CG, P_Z = 0, 256, 512, 1024, 1536, 2048, 2560
D_INP = 2688
Z_PAD = D_INP - P_Z

ADAM_LR = 0.001
ADAM_B1 = 0.9
ADAM_B2 = 0.999
ADAM_EPS = 1e-08
ADAM_WD = 0.01
ADAM_STEP = 10

VMEM_LIMIT = 56 * 1024 * 1024

MESH = pl.DeviceIdType.MESH
ANY = pl.BlockSpec(memory_space=pl.ANY)


def _nn(a, b):
    return jnp.dot(a, b, preferred_element_type=F32)


def _nt(a, b):
    return lax.dot_general(a, b, (((1,), (1,)), ((), ())), preferred_element_type=F32)


def _tn(a, b):
    return lax.dot_general(a, b, (((0,), (0,)), ((), ())), preferred_element_type=F32)


def _params(sem=None):
    return pltpu.CompilerParams(dimension_semantics=sem, vmem_limit_bytes=VMEM_LIMIT)


def _const(shape):
    return pl.BlockSpec(shape, lambda *_: (0,) * len(shape), pipeline_mode=pl.Buffered(1))


def _colsum(v):
    return jnp.sum(v, axis=0, keepdims=True)


def _rowmean(v):
    return jnp.mean(v, axis=-1, keepdims=True)


def _split_bf16(v):
    hi = v.astype(BF16)
    return hi, (v - hi.astype(F32)).astype(BF16)


def _my_place():
    return lax.axis_index("x"), lax.axis_index("y"), lax.axis_index("c")


def _peer(j):
    x, y, c = _my_place()
    jx, jy, jc = (j >> 2) & 1, (j >> 1) & 1, j & 1
    px = 1 - x if jx else x
    py = 1 - y if jy else y
    pc = 1 - c if jc else c
    return (px, py, pc), 4 * px + 2 * py + pc


def _comm_plan(kinds, ins, outs, send_sems, recv_sems, local_sems, receives=True):
    x, y, c = _my_place()
    me = 4 * x + 2 * y + c
    own = lambda k, idx: ins[k] if kinds[k] == "gather" else ins[k].at[idx]
    local = [pltpu.make_async_copy(own(k, me), outs[k].at[me], local_sems.at[k]) for k in range(len(kinds))]
    sends, recvs = [], []
    for j in range(1, N_DEV):
        peer, peer_idx = _peer(j)
        for k in range(len(kinds)):
            sems = dict(send_sem=send_sems.at[k, j - 1], recv_sem=recv_sems.at[k, j - 1], device_id=peer,
                        device_id_type=MESH)
            sends.append(pltpu.make_async_remote_copy(src_ref=own(k, peer_idx), dst_ref=outs[k].at[me], **sems))
            if receives:
                recvs.append(pltpu.make_async_remote_copy(src_ref=own(k, me), dst_ref=outs[k].at[peer_idx], **sems))
    return local, sends, recvs


def _comm_start(plan):
    local, sends, _ = plan
    for cp in local + sends:
        cp.start()


def _comm_wait(plan):
    local, sends, recvs = plan
    for cp in recvs:
        cp.wait_recv()
    for cp in sends:
        cp.wait_send()
    for cp in local:
        cp.wait()


def _comm_scratch(n):
    return [pltpu.SemaphoreType.DMA((n, N_DEV - 1)), pltpu.SemaphoreType.DMA((n, N_DEV - 1)),
            pltpu.SemaphoreType.DMA((n,))]


def _comm_out_shapes(kinds, arrays):
    return [jax.ShapeDtypeStruct(((N_DEV,) + a.shape) if kind == "gather" else a.shape, a.dtype)
            for kind, a in zip(kinds, arrays)]


def _comm(kinds, arrays, name):
    n = len(arrays)

    def body(*refs):
        plan = _comm_plan(kinds, refs[:n], refs[n:2 * n], *refs[2 * n:])
        _comm_start(plan)
        _comm_wait(plan)

    return pl.pallas_call(
        body, name=name, out_shape=_comm_out_shapes(kinds, arrays), in_specs=[ANY] * n, out_specs=[ANY] * n,
        scratch_shapes=_comm_scratch(n),
    )(*arrays)


def _hosted_comm(kinds, n_in, n_out, n_comm, n_steps):
    def plan_of(refs, receives):
        ins = refs[n_in:n_in + n_comm]
        outs = refs[n_in + n_comm + n_out:n_in + 2 * n_comm + n_out]
        return _comm_plan(kinds, ins, outs, *refs[-3:], receives=receives)

    def start(refs):
        @pl.when(pl.program_id(0) == 0)
        def _():
            _comm_start(plan_of(refs, False))

    def wait(refs):
        @pl.when(pl.program_id(0) == n_steps - 1)
        def _():
            _comm_wait(plan_of(refs, True))

    return start, wait


def _inproj_fwd(x, g1, w_in_p):
    T = x.shape[0]
    tm = min(T, 512)

    def body(x_ref, g_ref, w_ref, proj_ref, xn_ref):
        xv = x_ref[...]
        r = lax.rsqrt(_rowmean(xv * xv) + EPS)
        xn = (xv * r * g_ref[...]).astype(BF16)
        xn_ref[...] = xn
        proj_ref[...] = _nn(xn, w_ref[...]).astype(BF16)

    return pl.pallas_call(
        body, name="inproj_fwd", grid=(T // tm,),
        out_shape=[jax.ShapeDtypeStruct((T, D_INP), BF16), jax.ShapeDtypeStruct((T, D_MODEL), BF16)],
        in_specs=[pl.BlockSpec((tm, D_MODEL), lambda i: (i, 0)), _const((1, D_MODEL)), _const((D_MODEL, D_INP))],
        out_specs=[pl.BlockSpec((tm, D_INP), lambda i: (i, 0)), pl.BlockSpec((tm, D_MODEL), lambda i: (i, 0))],
        compiler_params=_params(("arbitrary",)),
    )(x, g1, w_in_p)


def _head_masks():
    lane = lax.broadcasted_iota(jnp.int32, (1, KEY), 1)
    return [((lane >= h * DK) & (lane < (h + 1) * DK)).astype(F32) for h in range(HEADS)]


class _Mats(NamedTuple):
    tri: jax.Array
    tri_t: jax.Array
    same: jax.Array
    mid: jax.Array
    causal: jax.Array
    causal_t: jax.Array
    heads: jax.Array


def _chunk_matrices():
    r = lax.broadcasted_iota(jnp.int32, (SUB, SUB), 0)
    c = lax.broadcasted_iota(jnp.int32, (SUB, SUB), 1)
    shift = CHUNK.bit_length() - 1
    same = jnp.right_shift(r, shift) == jnp.right_shift(c, shift)
    causal = same & (r >= c)
    causal_t = same & (r <= c)
    mid = same & ((c & (CHUNK - 1)) < CHUNK // 2)
    hr = jnp.right_shift(lax.broadcasted_iota(jnp.int32, (VAL, KEY), 0), DV.bit_length() - 1)
    hc = jnp.right_shift(lax.broadcasted_iota(jnp.int32, (VAL, KEY), 1), DK.bit_length() - 1)
    return _Mats(tri=causal.astype(BF16), tri_t=causal_t.astype(BF16), same=same.astype(BF16), mid=mid.astype(BF16),
                 causal=causal, causal_t=causal_t, heads=hr == hc)


class _Decay(NamedTuple):
    al: jax.Array
    q: jax.Array
    k: jax.Array
    eb: jax.Array
    ebm: jax.Array
    emb: jax.Array
    elb: jax.Array
    ebl: jax.Array


def _decay_terms(z, q, k, wg, bg, mats):
    al = _nn(z, wg) + bg
    la = (jnp.minimum(al, 0.0) - jnp.log(1.0 + jnp.exp(-jnp.abs(al)))) * (1.0 / GATE_TAU)
    hi, lo = _split_bf16(la)
    cum = lambda m: _nn(m, hi) + _nn(m, lo)
    b, b_last, b_mid = cum(mats.tri), cum(mats.same), cum(mats.mid)
    return _Decay(al=al, q=q.astype(F32) * Q_SCALE, k=k.astype(F32), eb=jnp.exp(b), ebm=jnp.exp(b - b_mid),
                  emb=jnp.exp(b_mid - b), elb=jnp.exp(b_last - b), ebl=jnp.exp(b_last))


def _gla_fwd(proj, wg, bg, gn, shards):
    T = proj.shape[0]
    tb = min(T, 512)
    cpb = tb // CHUNK
    n_comm = len(shards)
    kinds = ["gather"] * n_comm
    comm_start, comm_wait = _hosted_comm(kinds, 8, 3, n_comm, T // tb)

    def body(*refs):
        q_ref, k_ref, v_ref, g_ref, z_ref, wg_ref, bg_ref, gn_ref = refs[:8]
        mix_ref, o_ref, st_ref = refs[8 + n_comm:11 + n_comm]
        state = refs[11 + 2 * n_comm]
        comm_start(refs)

        @pl.when(pl.program_id(0) == 0)
        def _():
            state[...] = jnp.zeros_like(state)

        mats = _chunk_matrices()
        masks = _head_masks()
        wgv, bgv = wg_ref[...], bg_ref[...]

        for sb in range(tb // SUB):
            rows = slice(sb * SUB, (sb + 1) * SUB)
            d = _decay_terms(z_ref[rows, :], q_ref[rows, :], k_ref[rows, :], wgv, bgv, mats)
            kem_b = (d.k * d.emb).astype(BF16)
            qem = d.q * d.ebm
            for h in range(HEADS):
                cols = slice(h * DV, (h + 1) * DV)
                a = jnp.where(mats.causal, _nt((qem * masks[h]).astype(BF16), kem_b), 0.0)
                o_ref[rows, cols] = _nn(a.astype(BF16), v_ref[rows, cols])
            qe0_b = (d.q * d.eb).astype(BF16)
            kdec_b = (d.k * d.elb).astype(BF16)
            for c in range(SUB // CHUNK):
                loc = slice(c * CHUNK, (c + 1) * CHUNK)
                glob = slice(sb * SUB + c * CHUNK, sb * SUB + (c + 1) * CHUNK)
                st = state[...]
                st_b = st.astype(BF16)
                st_ref[sb * (SUB // CHUNK) + c] = st_b
                o_ref[glob, :] += _nt(qe0_b[loc], st_b)
                u = _tn(v_ref[glob, :], kdec_b[loc])
                state[...] = st * d.ebl[c * CHUNK:c * CHUNK + 1] + jnp.where(mats.heads, u, 0.0)

        gnv = gn_ref[...]
        for h in range(HEADS):
            cols = slice(h * DV, (h + 1) * DV)
            oh = o_ref[:, cols]
            r = lax.rsqrt(_rowmean(oh * oh) + EPS)
            gh = g_ref[:, cols].astype(F32)
            mix_ref[:, cols] = (oh * r * gnv * (gh * jax.nn.sigmoid(gh))).astype(BF16)
        comm_wait(refs)

    nc = T // CHUNK
    res = pl.pallas_call(
        body, name="gla_fwd", grid=(T // tb,),
        out_shape=[jax.ShapeDtypeStruct((T, VAL), BF16), jax.ShapeDtypeStruct((T, VAL), F32),
                   jax.ShapeDtypeStruct((nc, VAL, KEY), BF16)] + _comm_out_shapes(kinds, shards),
        in_specs=[pl.BlockSpec((tb, KEY), lambda i: (i, P_Q // KEY)), pl.BlockSpec((tb, KEY), lambda i: (i, P_K // KEY)),
                  pl.BlockSpec((tb, VAL), lambda i: (i, P_V // VAL)), pl.BlockSpec((tb, VAL), lambda i: (i, P_G // VAL)),
                  pl.BlockSpec((tb, Z_PAD), lambda i: (i, P_Z // Z_PAD)),
                  _const((Z_PAD, KEY)), _const((1, KEY)), _const((1, DV))] + [ANY] * n_comm,
        out_specs=[pl.BlockSpec((tb, VAL), lambda i: (i, 0)), pl.BlockSpec((tb, VAL), lambda i: (i, 0)),
                   pl.BlockSpec((cpb, VAL, KEY), lambda i: (i, 0, 0))] + [ANY] * n_comm,
        scratch_shapes=[pltpu.VMEM((VAL, KEY), F32)] + _comm_scratch(n_comm),
        compiler_params=_params(("arbitrary",)),
    )(proj, proj, proj, proj, proj, wg, bg, gn, *shards)
    return res[0], res[1], res[2], res[3:]


def _group_mean(v, gmat):
    return _nn(v.astype(BF16), gmat)


def _shifted_copies(buf, sh, rows):
    for k in range(1, SUBLANES):
        sh[k - 1] = buf[pl.ds(k, rows), :]


def _tap(buf, sh, off, r0):
    k, base = off % SUBLANES, off - off % SUBLANES
    rows = pl.ds(pl.multiple_of(r0 + base, SUBLANES), STRIP)
    return buf[rows, :] if k == 0 else sh[k - 1, rows, :]


def _conv_fwd(proj, conv_w, conv_b, cn_g, cn_b, gmat, shards):
    T = proj.shape[0]
    tm = min(T, 512)
    sh_rows = tm + HALO - SUBLANES
    n_comm = len(shards)
    kinds = ["gather"] * n_comm
    comm_start, comm_wait = _hosted_comm(kinds, 7, 2, n_comm, T // tm)

    def body(*refs):
        ci_ref, cg_ref, w_ref, b_ref, g_ref, be_ref, gm_ref = refs[:7]
        mix_ref, uc_ref = refs[7 + n_comm:9 + n_comm]
        ubuf, ush = refs[9 + 2 * n_comm:11 + 2 * n_comm]
        comm_start(refs)

        @pl.when(pl.program_id(0) == 0)
        def _():
            ubuf[0:HALO, :] = jnp.zeros((HALO, CONV), F32)

        ubuf[HALO:, :] = ci_ref[...].astype(F32) * jax.nn.sigmoid(cg_ref[...].astype(F32))
        _shifted_copies(ubuf, ush, sh_rows)

        def strip(s, carry):
            r0 = pl.multiple_of(s * STRIP, STRIP)
            acc = jnp.zeros((STRIP, CONV), F32) + b_ref[...]
            for j in range(CONV_W):
                acc = acc + w_ref[j:j + 1, :] * _tap(ubuf, ush, HALO - (CONV_W - 1) + j, r0)
            uc_ref[pl.ds(r0, STRIP), :] = acc
            return carry

        lax.fori_loop(0, tm // STRIP, strip, 0)
        ubuf[0:HALO, :] = ubuf[tm:tm + HALO, :]
        gm = gm_ref[...]
        ucv = uc_ref[...]
        d = ucv - _group_mean(ucv, gm)
        var = _group_mean(d * d, gm)
        yn = d * lax.rsqrt(var + EPS) * g_ref[...] + be_ref[...]
        mix_ref[...] = (yn * jax.nn.sigmoid(yn)).astype(BF16)
        comm_wait(refs)

    res = pl.pallas_call(
        body, name="conv_fwd", grid=(T // tm,),
        out_shape=[jax.ShapeDtypeStruct((T, CONV), BF16), jax.ShapeDtypeStruct((T, CONV), F32)]
        + _comm_out_shapes(kinds, shards),
        in_specs=[pl.BlockSpec((tm, CONV), lambda i: (i, P_CI // CONV)), pl.BlockSpec((tm, CONV), lambda i: (i, P_CG // CONV)),
                  _const((HALO, CONV)), _const((1, CONV)), _const((1, CONV)), _const((1, CONV)), _const((CONV, CONV))]
        + [ANY] * n_comm,
        out_specs=[pl.BlockSpec((tm, CONV), lambda i: (i, 0)), pl.BlockSpec((tm, CONV), lambda i: (i, 0))]
        + [ANY] * n_comm,
        scratch_shapes=[pltpu.VMEM((tm + HALO, CONV), F32), pltpu.VMEM((SUBLANES - 1, sh_rows, CONV), F32)]
        + _comm_scratch(n_comm),
        compiler_params=_params(("arbitrary",)),
    )(proj, proj, conv_w, conv_b, cn_g, cn_b, gmat, *shards)
    return res[0], res[1], res[2:]


def _rms_bwd(dy, xhat, r, g):
    dyg = dy * g
    return r * (dyg - xhat * _rowmean(dyg * xhat))


def _mlp_fwd_bwd(x, mix_a, mix_c, tgt, w_out, g2, w1, w2, gf):
    T = x.shape[0]
    tm = min(T, 256)
    inv_d = 1.0 / D_MODEL

    def body(x_ref, ma_ref, mc_ref, t_ref, wo_ref, g2_ref, w1_ref, w2_ref, gf_ref,
             dh1_ref, dmix_ref, hn_ref, ff_ref, da_ref, dh2_ref, loss_ref, dgf_ref, dg2_ref):
        @pl.when(pl.program_id(0) == 0)
        def _():
            loss_ref[...] = jnp.zeros_like(loss_ref)
            dgf_ref[...] = jnp.zeros_like(dgf_ref)
            dg2_ref[...] = jnp.zeros_like(dg2_ref)

        g2v, gfv = g2_ref[...], gf_ref[...]
        h1 = x_ref[...] + _nn(ma_ref[...], wo_ref[0:VAL, :]) + _nn(mc_ref[...], wo_ref[VAL:, :])
        r2 = lax.rsqrt(_rowmean(h1 * h1) + EPS)
        h1hat = h1 * r2
        hn = (h1hat * g2v).astype(BF16)
        hn_ref[...] = hn
        relu_a = jnp.maximum(_nn(hn, w1_ref[...]), 0.0)
        ff = (relu_a * relu_a).astype(BF16)
        ff_ref[...] = ff
        h2 = h1 + _nn(ff, w2_ref[...])
        rf = lax.rsqrt(_rowmean(h2 * h2) + EPS)
        h2hat = h2 * rf
        err = h2hat * gfv - t_ref[...]
        loss_ref[...] += (0.5 * inv_d) * _colsum(jnp.sum(err * err, axis=1, keepdims=True))
        dy = err * inv_d
        dgf_ref[...] += _colsum(dy * h2hat)
        dh2 = _rms_bwd(dy, h2hat, rf, gfv)
        dh2_b = dh2.astype(BF16)
        dh2_ref[...] = dh2_b
        da = (_nt(dh2_b, w2_ref[...]) * (2.0 * relu_a)).astype(BF16)
        da_ref[...] = da
        dhn = _nt(da, w1_ref[...])
        dg2_ref[...] += _colsum(dhn * h1hat)
        dh1 = dh2 + _rms_bwd(dhn, h1hat, r2, g2v)
        dh1_ref[...] = dh1
        dmix_ref[...] = _nt(dh1.astype(BF16), wo_ref[...]).astype(BF16)

    tok = lambda w: pl.BlockSpec((tm, w), lambda i: (i, 0))
    return pl.pallas_call(
        body, name="mlp_fwd_bwd", grid=(T // tm,),
        out_shape=[jax.ShapeDtypeStruct((T, D_MODEL), F32), jax.ShapeDtypeStruct((T, D_MODEL), BF16),
                   jax.ShapeDtypeStruct((T, D_MODEL), BF16), jax.ShapeDtypeStruct((T, D_FF), BF16),
                   jax.ShapeDtypeStruct((T, D_FF), BF16), jax.ShapeDtypeStruct((T, D_MODEL), BF16),
                   jax.ShapeDtypeStruct((1, 1), F32), jax.ShapeDtypeStruct((1, D_MODEL), F32),
                   jax.ShapeDtypeStruct((1, D_MODEL), F32)],
        in_specs=[tok(D_MODEL), tok(VAL), tok(CONV), tok(D_MODEL), _const((D_MODEL, D_MODEL)), _const((1, D_MODEL)),
                  _const((D_MODEL, D_FF)), _const((D_FF, D_MODEL)), _const((1, D_MODEL))],
        out_specs=[tok(D_MODEL), tok(D_MODEL), tok(D_MODEL), tok(D_FF), tok(D_FF), tok(D_MODEL),
                   pl.BlockSpec((1, 1), lambda i: (0, 0)), pl.BlockSpec((1, D_MODEL), lambda i: (0, 0)),
                   pl.BlockSpec((1, D_MODEL), lambda i: (0, 0))],
        compiler_params=_params(("arbitrary",)),
    )(x, mix_a, mix_c, tgt, w_out, g2, w1, w2, gf)


def _silu_grad(v, s):
    return s * (1.0 + v * (1.0 - s))


def _conv_bwd(proj, uc, dmix, conv_w, cn_g, cn_b, gmat):
    T = proj.shape[0]
    tm = min(T, 512)
    nt = T // tm
    hb = tm // HALO
    sh_rows = tm + HALO - SUBLANES
    n_strips = tm // STRIP

    def body(ci_ref, cg_ref, cih_ref, cgh_ref, uc_ref, dm_ref, w_ref, g_ref, be_ref, gm_ref,
             dci_ref, dcg_ref, dw_ref, db_ref, dg_ref, dbe_ref, ubuf, ush, dbuf, dsh):
        step = pl.program_id(0)

        @pl.when(step == 0)
        def _():
            dbuf[tm:, :] = jnp.zeros((HALO, CONV), F32)
            dw_ref[...] = jnp.zeros_like(dw_ref)
            db_ref[...] = jnp.zeros_like(db_ref)
            dg_ref[...] = jnp.zeros_like(dg_ref)
            dbe_ref[...] = jnp.zeros_like(dbe_ref)

        ubuf[HALO:, :] = ci_ref[...].astype(F32) * jax.nn.sigmoid(cg_ref[...].astype(F32))
        halo = cih_ref[...].astype(F32) * jax.nn.sigmoid(cgh_ref[...].astype(F32))
        ubuf[0:HALO, :] = jnp.where(step == nt - 1, 0.0, halo)
        _shifted_copies(ubuf, ush, sh_rows)

        gm, gv = gm_ref[...], g_ref[...]
        ucv = uc_ref[...]
        d = ucv - _group_mean(ucv, gm)
        rs = lax.rsqrt(_group_mean(d * d, gm) + EPS)
        yhat = d * rs
        yn = yhat * gv + be_ref[...]
        dyn = dm_ref[...].astype(F32) * _silu_grad(yn, jax.nn.sigmoid(yn))
        dg_ref[...] += _colsum(dyn * yhat)
        dbe_ref[...] += _colsum(dyn)
        dyh = dyn * gv
        duc = rs * (dyh - _group_mean(dyh, gm) - yhat * _group_mean(dyh * yhat, gm))
        db_ref[...] += _colsum(duc)
        dbuf[0:tm, :] = duc
        _shifted_copies(dbuf, dsh, sh_rows)

        def du_strip(s, carry):
            r0 = pl.multiple_of(s * STRIP, STRIP)
            rows = pl.ds(r0, STRIP)
            du = jnp.zeros((STRIP, CONV), F32)
            for j in range(CONV_W):
                du = du + w_ref[j:j + 1, :] * _tap(dbuf, dsh, CONV_W - 1 - j, r0)
            cin = ci_ref[rows, :].astype(F32)
            sg = jax.nn.sigmoid(cg_ref[rows, :].astype(F32))
            dci_ref[rows, :] = (du * sg).astype(BF16)
            dcg_ref[rows, :] = (du * cin * sg * (1.0 - sg)).astype(BF16)
            return carry

        lax.fori_loop(0, n_strips, du_strip, 0)

        for j in range(CONV_W):
            def dw_strip(s, acc, off=HALO - (CONV_W - 1) + j):
                r0 = pl.multiple_of(s * STRIP, STRIP)
                p = dbuf[pl.ds(r0, STRIP), :] * _tap(ubuf, ush, off, r0)
                for q in range(STRIP // SUBLANES):
                    acc = acc + p[q * SUBLANES:(q + 1) * SUBLANES, :]
                return acc

            acc = lax.fori_loop(0, n_strips, dw_strip, jnp.zeros((SUBLANES, CONV), F32))
            dw_ref[j:j + 1, :] += _colsum(acc)
        dbuf[tm:, :] = dbuf[0:HALO, :]

    rev = lambda i: nt - 1 - i
    halo_idx = lambda i: jnp.maximum(rev(i) * hb - 1, 0)
    tile = lambda col: pl.BlockSpec((tm, CONV), lambda i: (rev(i), col))
    acc = lambda rows: pl.BlockSpec((rows, CONV), lambda i: (0, 0))
    return pl.pallas_call(
        body, name="conv_bwd", grid=(nt,),
        out_shape=[jax.ShapeDtypeStruct((T, CONV), BF16), jax.ShapeDtypeStruct((T, CONV), BF16),
                   jax.ShapeDtypeStruct((HALO, CONV), F32), jax.ShapeDtypeStruct((1, CONV), F32),
                   jax.ShapeDtypeStruct((1, CONV), F32), jax.ShapeDtypeStruct((1, CONV), F32)],
        in_specs=[tile(P_CI // CONV), tile(P_CG // CONV),
                  pl.BlockSpec((HALO, CONV), lambda i: (halo_idx(i), P_CI // CONV)),
                  pl.BlockSpec((HALO, CONV), lambda i: (halo_idx(i), P_CG // CONV)),
                  tile(0), tile(1), _const((HALO, CONV)), _const((1, CONV)), _const((1, CONV)), _const((CONV, CONV))],
        out_specs=[tile(0), tile(0), acc(HALO), acc(1), acc(1), acc(1)],
        scratch_shapes=[pltpu.VMEM((tm + HALO, CONV), F32), pltpu.VMEM((SUBLANES - 1, sh_rows, CONV), F32),
                        pltpu.VMEM((tm + HALO, CONV), F32), pltpu.VMEM((SUBLANES - 1, sh_rows, CONV), F32)],
        compiler_params=_params(("arbitrary",)),
    )(proj, proj, proj, proj, uc, dmix, conv_w, cn_g, cn_b, gmat)


def _gla_bwd(proj, o, states, dmix, wg, bg, gn, parts):
    T = proj.shape[0]
    tb = min(T, 512)
    cpb = tb // CHUNK
    nb = T // tb
    n_comm = len(parts)
    kinds = ["exchange"] * n_comm
    comm_start, comm_wait = _hosted_comm(kinds, 11, 8, n_comm, nb)

    def body(*refs):
        q_ref, k_ref, v_ref, g_ref, z_ref, o_ref, st_ref, dm_ref, wg_ref, bg_ref, gn_ref = refs[:11]
        dq_ref, dk_ref, dv_ref, dg_ref, dz_ref, dwg_ref, dbg_ref, dgn_ref = refs[11 + n_comm:19 + n_comm]
        dstate, do_scr, dv_scr = refs[19 + 2 * n_comm:22 + 2 * n_comm]
        comm_start(refs)

        @pl.when(pl.program_id(0) == 0)
        def _():
            dstate[...] = jnp.zeros_like(dstate)
            dwg_ref[...] = jnp.zeros_like(dwg_ref)
            dbg_ref[...] = jnp.zeros_like(dbg_ref)
            dgn_ref[...] = jnp.zeros_like(dgn_ref)

        gnv = gn_ref[...]
        dgn = jnp.zeros((1, DV), F32)
        for h in range(HEADS):
            cols = slice(h * DV, (h + 1) * DV)
            oh = o_ref[:, cols]
            r = lax.rsqrt(_rowmean(oh * oh) + EPS)
            ohat = oh * r
            gh = g_ref[:, cols].astype(F32)
            sg = jax.nn.sigmoid(gh)
            dmx = dm_ref[:, cols].astype(F32)
            don = dmx * (gh * sg)
            dg_ref[:, cols] = (dmx * (ohat * gnv) * _silu_grad(gh, sg)).astype(BF16)
            dgn = dgn + _colsum(don * ohat)
            do_scr[:, cols] = _rms_bwd(don, ohat, r, gnv)
        dgn_ref[...] += dgn

        mats = _chunk_matrices()
        masks = _head_masks()
        wgv, bgv = wg_ref[...], bg_ref[...]
        n_chunks = SUB // CHUNK

        for sb in reversed(range(tb // SUB)):
            rows = slice(sb * SUB, (sb + 1) * SUB)
            zs = z_ref[rows, :]
            d = _decay_terms(zs, q_ref[rows, :], k_ref[rows, :], wgv, bgv, mats)
            qem = d.q * d.ebm
            qem_b = qem.astype(BF16)
            kem_b = (d.k * d.emb).astype(BF16)
            dq = jnp.zeros((SUB, KEY), F32)
            dk = jnp.zeros((SUB, KEY), F32)
            for h in range(HEADS):
                hm = masks[h]
                cols = slice(h * DV, (h + 1) * DV)
                do_b = do_scr[rows, cols].astype(BF16)
                vh = v_ref[rows, cols]
                da = jnp.where(mats.causal, _nt(do_b, vh), 0.0).astype(BF16)
                da_t = jnp.where(mats.causal_t, _nt(vh, do_b), 0.0).astype(BF16)
                a_t = jnp.where(mats.causal_t, _nt(kem_b, (qem * hm).astype(BF16)), 0.0).astype(BF16)
                dq = dq + hm * _nn(da, kem_b)
                dk = dk + hm * _nn(da_t, qem_b)
                dv_scr[rows, cols] = _nn(a_t, do_b)
            dq = dq * d.ebm
            dk = dk * d.emb

            qe0_b = (d.q * d.eb).astype(BF16)
            kdec_b = (d.k * d.elb).astype(BF16)
            dq_st, dk_st, last = [None] * n_chunks, [None] * n_chunks, [None] * n_chunks
            for c in reversed(range(n_chunks)):
                loc = slice(c * CHUNK, (c + 1) * CHUNK)
                glob = slice(sb * SUB + c * CHUNK, sb * SUB + (c + 1) * CHUNK)
                st_b = st_ref[sb * n_chunks + c]
                ds = dstate[...]
                ds_b = ds.astype(BF16)
                do_c = do_scr[glob, :].astype(BF16)
                ebl_c = d.ebl[c * CHUNK:c * CHUNK + 1]
                dk_c = _nn(v_ref[glob, :], ds_b) * d.elb[loc]
                dq_st[c] = _nn(do_c, st_b) * d.eb[loc]
                dk_st[c] = dk_c
                last_c = _colsum(d.k[loc] * dk_c) + ebl_c * _colsum(st_b.astype(F32) * ds)
                last[c] = jnp.broadcast_to(last_c, (CHUNK, KEY))
                dv_ref[glob, :] = (dv_scr[glob, :] + _nt(kdec_b[loc], ds_b)).astype(BF16)
                dstate[...] = ds * ebl_c + jnp.where(mats.heads, _tn(do_c, qe0_b[loc]), 0.0)
            dq = dq + jnp.concatenate(dq_st, axis=0)
            dk = dk + jnp.concatenate(dk_st, axis=0)
            dq_ref[rows, :] = (dq * Q_SCALE).astype(BF16)
            dk_ref[rows, :] = dk.astype(BF16)
            hi, lo = _split_bf16(d.q * dq - d.k * dk)
            dla = _nn(mats.tri_t, hi) + _nn(mats.tri_t, lo) + jnp.concatenate(last, axis=0)
            dal = dla * (1.0 / GATE_TAU) * jax.nn.sigmoid(-d.al)
            dal_b = dal.astype(BF16)
            dz_ref[rows, :] = _nt(dal_b, wgv).astype(BF16)
            dwg_ref[...] += _tn(zs, dal_b)
            dbg_ref[...] += _colsum(dal)
        comm_wait(refs)

    rev = lambda i: nb - 1 - i
    blk = lambda w, col: pl.BlockSpec((tb, w), lambda i: (rev(i), col))
    res = pl.pallas_call(
        body, name="gla_bwd", grid=(nb,),
        out_shape=[jax.ShapeDtypeStruct((T, KEY), BF16), jax.ShapeDtypeStruct((T, KEY), BF16),
                   jax.ShapeDtypeStruct((T, VAL), BF16), jax.ShapeDtypeStruct((T, VAL), BF16),
                   jax.ShapeDtypeStruct((T, Z_PAD), BF16), jax.ShapeDtypeStruct((Z_PAD, KEY), F32),
                   jax.ShapeDtypeStruct((1, KEY), F32), jax.ShapeDtypeStruct((1, DV), F32)]
        + _comm_out_shapes(kinds, parts),
        in_specs=[blk(KEY, P_Q // KEY), blk(KEY, P_K // KEY), blk(VAL, P_V // VAL), blk(VAL, P_G // VAL),
                  blk(Z_PAD, P_Z // Z_PAD), blk(VAL, 0),
                  pl.BlockSpec((cpb, VAL, KEY), lambda i: (rev(i), 0, 0)), blk(VAL, 0),
                  _const((Z_PAD, KEY)), _const((1, KEY)), _const((1, DV))] + [ANY] * n_comm,
        out_specs=[blk(KEY, 0), blk(KEY, 0), blk(VAL, 0), blk(VAL, 0), blk(Z_PAD, 0),
                   pl.BlockSpec((Z_PAD, KEY), lambda i: (0, 0)), pl.BlockSpec((1, KEY), lambda i: (0, 0)),
                   pl.BlockSpec((1, DV), lambda i: (0, 0))] + [ANY] * n_comm,
        scratch_shapes=[pltpu.VMEM((VAL, KEY), F32), pltpu.VMEM((tb, VAL), F32), pltpu.VMEM((tb, VAL), F32)]
        + _comm_scratch(n_comm),
        compiler_params=_params(("arbitrary",)),
    )(proj, proj, proj, proj, proj, o, states, dmix, wg, bg, gn, *parts)
    return res[:8], res[8:]


def _inproj_bwd(x, g1, w_in_p, dh1, dq, dk, dv, dg, dci, dcg, dz):
    T = x.shape[0]
    tm = min(T, 512)

    def body(x_ref, g_ref, w_ref, dh1_ref, dq_ref, dk_ref, dv_ref, dg_ref, dci_ref, dcg_ref, dz_ref,
             dx_ref, dp_ref, dg1_ref):
        @pl.when(pl.program_id(0) == 0)
        def _():
            dg1_ref[...] = jnp.zeros_like(dg1_ref)

        dp_ref[:, P_Q:P_K] = dq_ref[...]
        dp_ref[:, P_K:P_V] = dk_ref[...]
        dp_ref[:, P_V:P_G] = dv_ref[...]
        dp_ref[:, P_G:P_CI] = dg_ref[...]
        dp_ref[:, P_CI:P_CG] = dci_ref[...]
        dp_ref[:, P_CG:P_Z] = dcg_ref[...]
        dp_ref[:, P_Z:] = dz_ref[...]
        dxn = _nt(dp_ref[...], w_ref[...])
        xv = x_ref[...]
        r = lax.rsqrt(_rowmean(xv * xv) + EPS)
        xhat = xv * r
        dg1_ref[...] += _colsum(dxn * xhat)
        dx_ref[...] = dh1_ref[...] + _rms_bwd(dxn, xhat, r, g_ref[...])

    tok = lambda w: pl.BlockSpec((tm, w), lambda i: (i, 0))
    return pl.pallas_call(
        body, name="inproj_bwd", grid=(T // tm,),
        out_shape=[jax.ShapeDtypeStruct((T, D_MODEL), F32), jax.ShapeDtypeStruct((T, D_INP), BF16),
                   jax.ShapeDtypeStruct((1, D_MODEL), F32)],
        in_specs=[tok(D_MODEL), _const((1, D_MODEL)), _const((D_MODEL, D_INP)), tok(D_MODEL), tok(KEY), tok(KEY),
                  tok(VAL), tok(VAL), tok(CONV), tok(CONV), tok(Z_PAD)],
        out_specs=[tok(D_MODEL), tok(D_INP), pl.BlockSpec((1, D_MODEL), lambda i: (0, 0))],
        compiler_params=_params(("arbitrary",)),
    )(x, g1, w_in_p, dh1, dq, dk, dv, dg, dci, dcg, dz)


def _wgrad(a, b, name, tk, tn, split_n=False):
    T, K = a.shape
    N = b.shape[1]
    tt = min(T, 1024)
    nt = T // tt

    def body(a_ref, b_ref, o_ref, acc):
        @pl.when(pl.program_id(2) == 0)
        def _():
            acc[...] = jnp.zeros_like(acc)

        acc[...] += _tn(a_ref[...], b_ref[...].astype(BF16))

        @pl.when(pl.program_id(2) == nt - 1)
        def _():
            o_ref[...] = acc[...].astype(BF16)

    if split_n:
        assert tk == K
        out_shape = jax.ShapeDtypeStruct((N // tn, K, tn), BF16)
        out_spec = pl.BlockSpec((None, tk, tn), lambda i, j, t: (j, 0, 0))
    else:
        out_shape = jax.ShapeDtypeStruct((K, N), BF16)
        out_spec = pl.BlockSpec((tk, tn), lambda i, j, t: (i, j))
    return pl.pallas_call(
        body, name=name, grid=(K // tk, N // tn, nt), out_shape=out_shape,
        in_specs=[pl.BlockSpec((tt, tk), lambda i, j, t: (t, i)), pl.BlockSpec((tt, tn), lambda i, j, t: (t, j))],
        out_specs=out_spec, scratch_shapes=[pltpu.VMEM((tk, tn), F32)],
        compiler_params=_params(("arbitrary", "arbitrary", "arbitrary")),
    )(a, b)


def _adam_math(w, g, m, v):
    m = ADAM_B1 * m + (1.0 - ADAM_B1) * g
    v = ADAM_B2 * v + (1.0 - ADAM_B2) * (g * g)
    m_hat = m / (1.0 - ADAM_B1 ** ADAM_STEP)
    v_hat = v / (1.0 - ADAM_B2 ** ADAM_STEP)
    delta = -ADAM_LR * (m_hat / (jnp.sqrt(v_hat) + ADAM_EPS) + ADAM_WD * w)
    return delta, m, v


def _sum8(ref):
    g = ref[0].astype(F32)
    for s in range(1, N_DEV):
        g = g + ref[s].astype(F32)
    return g


def _adam_big(parts, w, m, v, name):
    R, C = w.shape
    tr = min(R, 128)

    def body(p_ref, w_ref, m_ref, v_ref, g_ref, d_ref, nm_ref, nv_ref):
        g = _sum8(p_ref)
        g_ref[...] = g
        d_ref[...], nm_ref[...], nv_ref[...] = _adam_math(w_ref[...], g, m_ref[...], v_ref[...])

    row = pl.BlockSpec((tr, C), lambda i: (i, 0))
    return pl.pallas_call(
        body, name=name, grid=(R // tr,), out_shape=[jax.ShapeDtypeStruct((R, C), F32)] * 4,
        in_specs=[pl.BlockSpec((N_DEV, tr, C), lambda i: (0, i, 0)), row, row, row], out_specs=[row] * 4,
        compiler_params=_params(("arbitrary",)),
    )(parts, w, m, v)


def _sum_small(parts):
    def body(p_ref, o_ref):
        o_ref[...] = _sum8(p_ref)

    return pl.pallas_call(body, name="sum_small", out_shape=jax.ShapeDtypeStruct(parts.shape[1:], F32))(parts)


def _adam_small(gs, ws, ms, vs):
    n = len(gs)

    def body(*refs):
        g_refs, w_refs, m_refs, v_refs = refs[:n], refs[n:2 * n], refs[2 * n:3 * n], refs[3 * n:4 * n]
        outs = refs[4 * n:]
        for i in range(n):
            d, nm, nv = _adam_math(w_refs[i][...], g_refs[i][...], m_refs[i][...], v_refs[i][...])
            outs[i][...] = d
            outs[n + i][...] = nm
            outs[2 * n + i][...] = nv

    shapes = [jax.ShapeDtypeStruct(w.shape, F32) for w in ws]
    res = pl.pallas_call(body, name="adam_small", out_shape=shapes * 3)(*gs, *ws, *ms, *vs)
    return res[:n], res[n:2 * n], res[2 * n:]


def _permute_in(w):
    pad = jnp.zeros(w.shape[:-1] + (D_INP - D_IN,), w.dtype)
    return jnp.concatenate([w[..., :1536], w[..., 1552:], w[..., 1536:1552], pad], axis=-1)


def _unpermute_in(w):
    return jnp.concatenate([w[..., :P_CI], w[..., P_Z:P_Z + RANK], w[..., P_CI:P_Z]], axis=-1)


def _group_matrix():
    gi = lax.broadcasted_iota(jnp.int32, (CONV, CONV), 0) // (CONV // GROUPS)
    gj = lax.broadcasted_iota(jnp.int32, (CONV, CONV), 1) // (CONV // GROUPS)
    return jnp.where(gi == gj, GROUPS / CONV, 0.0).astype(BF16)


_SMALL = [("loss", 8), ("dg1", 8), ("dbg", 2), ("dgn", 1), ("dconv_b", 4), ("dcn_g", 4), ("dcn_b", 4), ("dg2", 8),
          ("dgf", 8), ("dwg", 32), ("dconv_w", 124)]


def _pad8(rows):
    return -(-rows // 8) * 8


def kernel(x, norm1_g, w_in, w_gate_up, b_gate, gla_norm_g, conv_w, conv_b, conv_norm_g, conv_norm_b, w_out, norm2_g, w_mlp_in, w_mlp_out, final_norm_g, loss_target, m_norm1_g, m_w_in, m_w_gate_up, m_b_gate, m_gla_norm_g, m_conv_w, m_conv_b, m_conv_norm_g, m_conv_norm_b, m_w_out, m_norm2_g, m_w_mlp_in, m_w_mlp_out, m_final_norm_g, v_norm1_g, v_w_in, v_w_gate_up, v_b_gate, v_gla_norm_g, v_conv_w, v_conv_b, v_conv_norm_g, v_conv_norm_b, v_w_out, v_norm2_g, v_w_mlp_in, v_w_mlp_out, v_final_norm_g):
    x_idx = lax.axis_index("x")
    y_idx = lax.axis_index("y")
    c_idx = lax.axis_index("c")
    me = 4 * x_idx + 2 * y_idx + c_idx
    pad_in = lambda a: jnp.pad(a, ((0, 0), (0, SHARD_IN_PAD - SHARD_IN)))
    xs, tgt = x[0], loss_target[0]
    gf = final_norm_g.reshape(1, D_MODEL)
    gmat = _group_matrix()

    small_shard = jnp.zeros((48, 128), F32)
    small_shard = small_shard.at[0:RANK, 0:KEY // N_DEV].set(w_gate_up[0])
    small_shard = small_shard.at[RANK:RANK + CONV_W, 0:CONV // N_DEV].set(conv_w[0])
    g_in, g_small = _comm(["gather", "gather"], [pad_in(w_in[0]).astype(BF16), small_shard], "gather_w_in")
    w_in_p = _permute_in(jnp.concatenate([g_in[d, :, :SHARD_IN] for d in range(N_DEV)], axis=1))
    wg_full = jnp.concatenate([g_small[d, 0:RANK, 0:KEY // N_DEV] for d in range(N_DEV)], axis=1)
    wg_pad = jnp.pad(wg_full, ((0, Z_PAD - RANK), (0, 0))).astype(BF16)
    conv_w_full = jnp.concatenate([g_small[d, RANK:RANK + CONV_W, 0:CONV // N_DEV] for d in range(N_DEV)], axis=1)
    conv_w_pad = jnp.pad(conv_w_full, ((0, HALO - CONV_W), (0, 0)))

    proj, xn = _inproj_fwd(xs, norm1_g, w_in_p)
    mix_a, o, states, (g_out, g_w1) = _gla_fwd(
        proj, wg_pad, b_gate, gla_norm_g, [w_out[0].astype(BF16), w_mlp_in[0].astype(BF16)])
    mix_c, uc, (g_w2,) = _conv_fwd(proj, conv_w_pad, conv_b, conv_norm_g, conv_norm_b, gmat,
                                   [w_mlp_out[0].astype(BF16)])
    w_out_full = g_out.reshape(D_MODEL, D_MODEL)
    w1_full = jnp.concatenate([g_w1[d] for d in range(N_DEV)], axis=1)
    w2_full = g_w2.reshape(D_FF, D_MODEL)
    dh1, dmix, hn, ff, da, dh2, loss, dgf, dg2 = _mlp_fwd_bwd(xs, mix_a, mix_c, tgt, w_out_full, norm2_g, w1_full,
                                                              w2_full, gf)

    dw1 = _wgrad(hn, da, "wgrad_mlp_in", 1024, 512, split_n=True)
    dw2 = _wgrad(ff, dh2, "wgrad_mlp_out", 1024, 1024)
    dw_out = jnp.concatenate([_wgrad(mix_a, dh1, "wgrad_out_a", VAL, 1024),
                              _wgrad(mix_c, dh1, "wgrad_out_c", CONV, 1024)], axis=0)
    dci, dcg, dconv_w, dconv_b, dcn_g, dcn_b = _conv_bwd(proj, uc, dmix, conv_w_pad, conv_norm_g, conv_norm_b, gmat)
    (dq, dk, dv, dg, dz, dwg, dbg, dgn), (p_w1, p_w2, p_out) = _gla_bwd(
        proj, o, states, dmix, wg_pad, b_gate, gla_norm_g,
        [dw1, dw2.reshape(N_DEV, D_FF // N_DEV, D_MODEL), dw_out.reshape(N_DEV, D_MODEL // N_DEV, D_MODEL)])
    dx, dproj, dg1 = _inproj_bwd(xs, norm1_g, w_in_p, dh1, dq, dk, dv, dg, dci, dcg, dz)
    dw_in_p = _wgrad(xn, dproj, "wgrad_in", 1024, 896)

    dw_in = _unpermute_in(dw_in_p).reshape(D_MODEL, N_DEV, SHARD_IN).transpose(1, 0, 2)
    dw_in = jnp.pad(dw_in, ((0, 0), (0, 0), (0, SHARD_IN_PAD - SHARD_IN)))
    small = dict(loss=jnp.zeros((8, 128), F32) + loss, dg1=dg1, dbg=dbg, dgn=dgn, dconv_b=dconv_b, dcn_g=dcn_g,
                 dcn_b=dcn_b, dg2=dg2, dgf=dgf, dwg=dwg[0:RANK], dconv_w=dconv_w[0:CONV_W])
    pack = jnp.concatenate([jnp.pad(small[name].reshape(rows, 128), ((0, _pad8(rows) - rows), (0, 0)))
                            for name, rows in _SMALL], axis=0)
    p_in, g_pack = _comm(["exchange", "gather"], [dw_in, pack], "exchange_w_in")

    gi, di, mi, vi = _adam_big(p_in, pad_in(w_in[0]), pad_in(m_w_in[0]), pad_in(v_w_in[0]), "adam_w_in")
    go, do, mo, vo = _adam_big(p_out, w_out[0], m_w_out[0], v_w_out[0], "adam_w_out")
    ga, da_, ma, va = _adam_big(p_w1, w_mlp_in[0], m_w_mlp_in[0], v_w_mlp_in[0], "adam_w_mlp_in")
    gb, db, mb, vb = _adam_big(p_w2, w_mlp_out[0], m_w_mlp_out[0], v_w_mlp_out[0], "adam_w_mlp_out")
    cut = lambda a: a[:, :SHARD_IN][None]

    summed = _sum_small(g_pack)
    small_g = {}
    at = 0
    for name, rows in _SMALL:
        small_g[name] = summed[at:at + rows]
        at += _pad8(rows)
    loss_out = small_g["loss"][0, 0]
    wg_cols = KEY // N_DEV
    cw_cols = CONV // N_DEV
    g_small_list = [
        small_g["dg1"].reshape(1, D_MODEL),
        lax.dynamic_slice_in_dim(small_g["dwg"].reshape(RANK, KEY), me * wg_cols, wg_cols, axis=1)[None],
        small_g["dbg"].reshape(1, KEY), small_g["dgn"].reshape(1, DV),
        lax.dynamic_slice_in_dim(small_g["dconv_w"].reshape(CONV_W, CONV), me * cw_cols, cw_cols, axis=1)[None],
        small_g["dconv_b"].reshape(1, CONV), small_g["dcn_g"].reshape(1, CONV), small_g["dcn_b"].reshape(1, CONV),
        small_g["dg2"].reshape(1, D_MODEL), small_g["dgf"].reshape(1, D_MODEL),
    ]
    row = lambda a: a.reshape(1, D_MODEL)
    w_small = [norm1_g, w_gate_up, b_gate, gla_norm_g, conv_w, conv_b, conv_norm_g, conv_norm_b, norm2_g,
               row(final_norm_g)]
    m_small = [m_norm1_g, m_w_gate_up, m_b_gate, m_gla_norm_g, m_conv_w, m_conv_b, m_conv_norm_g, m_conv_norm_b,
               m_norm2_g, row(m_final_norm_g)]
    v_small = [v_norm1_g, v_w_gate_up, v_b_gate, v_gla_norm_g, v_conv_w, v_conv_b, v_conv_norm_g, v_conv_norm_b,
               v_norm2_g, row(v_final_norm_g)]
    d_small, nm_small, nv_small = _adam_small(g_small_list, w_small, m_small, v_small)
    flat = lambda lst: list(lst[:-1]) + [lst[-1].reshape(D_MODEL)]
    g_small_list, d_small, nm_small, nv_small = flat(g_small_list), flat(d_small), flat(nm_small), flat(nv_small)

    def order(s, w_in_v, w_out_v, w1_v, w2_v):
        return [s[0], w_in_v, s[1], s[2], s[3], s[4], s[5], s[6], s[7], w_out_v, s[8], w1_v, w2_v, s[9]]

    grads = order(g_small_list, cut(gi), go[None], ga[None], gb[None])
    deltas = order(d_small, cut(di), do[None], da_[None], db[None])
    new_m = order(nm_small, cut(mi), mo[None], ma[None], mb[None])
    new_v = order(nv_small, cut(vi), vo[None], va[None], vb[None])
    return (loss_out, dx[None], *grads, *deltas, *new_m, *new_v)
```

```python
from typing import NamedTuple

import jax
import jax.numpy as jnp
from jax import lax
from jax.experimental import pallas as pl
from jax.experimental.pallas import tpu as pltpu

F32 = jnp.float32
BF16 = jnp.bfloat16

N_DEV = 8
D_MODEL = 1024
HEADS = 4
DK = 64
DV = 128
KEY = HEADS * DK
VAL = HEADS * DV
RANK = 16
CONV = 512
GROUPS = 8
CONV_W = 31
HALO = 32
SUBLANES = 8
STRIP = 32
D_FF = 4096
D_IN = 2576
SHARD_IN = D_IN // N_DEV
SHARD_IN_PAD = 384
CHUNK = 64
SUB = 256
EPS = 1e-6
GATE_TAU = 16.0
Q_SCALE = DK ** -0.5

P_Q, P_K, P_V, P_G, P_CI, P_CG, P_Z = 0, 256, 512, 1024, 1536, 2048, 2560
D_INP = 2688
Z_PAD = D_INP - P_Z

ADAM_LR = 0.001
ADAM_B1 = 0.9
ADAM_B2 = 0.999
ADAM_EPS = 1e-08
ADAM_WD = 0.01
ADAM_STEP = 10

VMEM_LIMIT = 56 * 1024 * 1024

MESH = pl.DeviceIdType.MESH
ANY = pl.BlockSpec(memory_space=pl.ANY)


def _nn(a, b):
    return jnp.dot(a, b, preferred_element_type=F32)


def _nt(a, b):
    return lax.dot_general(a, b, (((1,), (1,)), ((), ())), preferred_element_type=F32)


def _tn(a, b):
    return lax.dot_general(a, b, (((0,), (0,)), ((), ())), preferred_element_type=F32)


def _params(sem=None):
    return pltpu.CompilerParams(dimension_semantics=sem, vmem_limit_bytes=VMEM_LIMIT)


def _const(shape):
    return pl.BlockSpec(shape, lambda *_: (0,) * len(shape), pipeline_mode=pl.Buffered(1))


def _colsum(v):
    return jnp.sum(v, axis=0, keepdims=True)


def _rowmean(v):
    return jnp.mean(v, axis=-1, keepdims=True)


def _split_bf16(v):
    hi = v.astype(BF16)
    return hi, (v - hi.astype(F32)).astype(BF16)


def _my_place():
    return lax.axis_index("x"), lax.axis_index("y"), lax.axis_index("c")


def _peer(j):
    x, y, c = _my_place()
    jx, jy, jc = (j >> 2) & 1, (j >> 1) & 1, j & 1
    px = 1 - x if jx else x
    py = 1 - y if jy else y
    pc = 1 - c if jc else c
    return (px, py, pc), 4 * px + 2 * py + pc


def _comm_plan(kinds, ins, outs, send_sems, recv_sems, local_sems, receives=True):
    x, y, c = _my_place()
    me = 4 * x + 2 * y + c
    own = lambda k, idx: ins[k] if kinds[k] == "gather" else ins[k].at[idx]
    local = [pltpu.make_async_copy(own(k, me), outs[k].at[me], local_sems.at[k]) for k in range(len(kinds))]
    sends, recvs = [], []
    for j in range(1, N_DEV):
        peer, peer_idx = _peer(j)
        for k in range(len(kinds)):
            sems = dict(send_sem=send_sems.at[k, j - 1], recv_sem=recv_sems.at[k, j - 1], device_id=peer,
                        device_id_type=MESH)
            sends.append(pltpu.make_async_remote_copy(src_ref=own(k, peer_idx), dst_ref=outs[k].at[me], **sems))
            if receives:
                recvs.append(pltpu.make_async_remote_copy(src_ref=own(k, me), dst_ref=outs[k].at[peer_idx], **sems))
    return local, sends, recvs


def _comm_start(plan):
    local, sends, _ = plan
    for cp in local + sends:
        cp.start()


def _comm_wait(plan):
    local, sends, recvs = plan
    for cp in recvs:
        cp.wait_recv()
    for cp in sends:
        cp.wait_send()
    for cp in local:
        cp.wait()


def _comm_scratch(n):
    return [pltpu.SemaphoreType.DMA((n, N_DEV - 1)), pltpu.SemaphoreType.DMA((n, N_DEV - 1)),
            pltpu.SemaphoreType.DMA((n,))]


def _comm_out_shapes(kinds, arrays):
    return [jax.ShapeDtypeStruct(((N_DEV,) + a.shape) if kind == "gather" else a.shape, a.dtype)
            for kind, a in zip(kinds, arrays)]


def _comm(kinds, arrays, name):
    n = len(arrays)

    def body(*refs):
        plan = _comm_plan(kinds, refs[:n], refs[n:2 * n], *refs[2 * n:])
        _comm_start(plan)
        _comm_wait(plan)

    return pl.pallas_call(
        body, name=name, out_shape=_comm_out_shapes(kinds, arrays), in_specs=[ANY] * n, out_specs=[ANY] * n,
        scratch_shapes=_comm_scratch(n),
    )(*arrays)


def _hosted_comm(kinds, n_in, n_out, n_comm, n_steps):
    def plan_of(refs, receives):
        ins = refs[n_in:n_in + n_comm]
        outs = refs[n_in + n_comm + n_out:n_in + 2 * n_comm + n_out]
        return _comm_plan(kinds, ins, outs, *refs[-3:], receives=receives)

    def start(refs):
        @pl.when(pl.program_id(0) == 0)
        def _():
            _comm_start(plan_of(refs, False))

    def wait(refs):
        @pl.when(pl.program_id(0) == n_steps - 1)
        def _():
            _comm_wait(plan_of(refs, True))

    return start, wait


def _inproj_fwd(x, g1, w_in_p):
    T = x.shape[0]
    tm = min(T, 512)

    def body(x_ref, g_ref, w_ref, proj_ref, xn_ref):
        xv = x_ref[...]
        r = lax.rsqrt(_rowmean(xv * xv) + EPS)
        xn = (xv * r * g_ref[...]).astype(BF16)
        xn_ref[...] = xn
        proj_ref[...] = _nn(xn, w_ref[...]).astype(BF16)

    return pl.pallas_call(
        body, name="inproj_fwd", grid=(T // tm,),
        out_shape=[jax.ShapeDtypeStruct((T, D_INP), BF16), jax.ShapeDtypeStruct((T, D_MODEL), BF16)],
        in_specs=[pl.BlockSpec((tm, D_MODEL), lambda i: (i, 0)), _const((1, D_MODEL)), _const((D_MODEL, D_INP))],
        out_specs=[pl.BlockSpec((tm, D_INP), lambda i: (i, 0)), pl.BlockSpec((tm, D_MODEL), lambda i: (i, 0))],
        compiler_params=_params(("arbitrary",)),
    )(x, g1, w_in_p)


def _head_masks():
    lane = lax.broadcasted_iota(jnp.int32, (1, KEY), 1)
    return [((lane >= h * DK) & (lane < (h + 1) * DK)).astype(F32) for h in range(HEADS)]


class _Mats(NamedTuple):
    tri: jax.Array
    tri_t: jax.Array
    same: jax.Array
    mid: jax.Array
    causal: jax.Array
    causal_t: jax.Array
    heads: jax.Array


def _chunk_matrices():
    r = lax.broadcasted_iota(jnp.int32, (SUB, SUB), 0)
    c = lax.broadcasted_iota(jnp.int32, (SUB, SUB), 1)
    shift = CHUNK.bit_length() - 1
    same = jnp.right_shift(r, shift) == jnp.right_shift(c, shift)
    causal = same & (r >= c)
    causal_t = same & (r <= c)
    mid = same & ((c & (CHUNK - 1)) < CHUNK // 2)
    hr = jnp.right_shift(lax.broadcasted_iota(jnp.int32, (VAL, KEY), 0), DV.bit_length() - 1)
    hc = jnp.right_shift(lax.broadcasted_iota(jnp.int32, (VAL, KEY), 1), DK.bit_length() - 1)
    return _Mats(tri=causal.astype(BF16), tri_t=causal_t.astype(BF16), same=same.astype(BF16), mid=mid.astype(BF16),
                 causal=causal, causal_t=causal_t, heads=hr == hc)


class _Decay(NamedTuple):
    al: jax.Array
    q: jax.Array
    k: jax.Array
    eb: jax.Array
    ebm: jax.Array
    emb: jax.Array
    elb: jax.Array
    ebl: jax.Array


def _decay_terms(z, q, k, wg, bg, mats):
    al = _nn(z, wg) + bg
    la = (jnp.minimum(al, 0.0) - jnp.log(1.0 + jnp.exp(-jnp.abs(al)))) * (1.0 / GATE_TAU)
    hi, lo = _split_bf16(la)
    cum = lambda m: _nn(m, hi) + _nn(m, lo)
    b, b_last, b_mid = cum(mats.tri), cum(mats.same), cum(mats.mid)
    return _Decay(al=al, q=q.astype(F32) * Q_SCALE, k=k.astype(F32), eb=jnp.exp(b), ebm=jnp.exp(b - b_mid),
                  emb=jnp.exp(b_mid - b), elb=jnp.exp(b_last - b), ebl=jnp.exp(b_last))


def _gla_fwd(proj, wg, bg, gn, shards):
    T = proj.shape[0]
    tb = min(T, 512)
    cpb = tb // CHUNK
    n_comm = len(shards)
    kinds = ["gather"] * n_comm
    comm_start, comm_wait = _hosted_comm(kinds, 8, 3, n_comm, T // tb)

    def body(*refs):
        q_ref, k_ref, v_ref, g_ref, z_ref, wg_ref, bg_ref, gn_ref = refs[:8]
        mix_ref, o_ref, st_ref = refs[8 + n_comm:11 + n_comm]
        state = refs[11 + 2 * n_comm]
        comm_start(refs)

        @pl.when(pl.program_id(0) == 0)
        def _():
            state[...] = jnp.zeros_like(state)

        mats = _chunk_matrices()
        masks = _head_masks()
        wgv, bgv = wg_ref[...], bg_ref[...]

        for sb in range(tb // SUB):
            rows = slice(sb * SUB, (sb + 1) * SUB)
            d = _decay_terms(z_ref[rows, :], q_ref[rows, :], k_ref[rows, :], wgv, bgv, mats)
            kem_b = (d.k * d.emb).astype(BF16)
            qem = d.q * d.ebm
            for h in range(HEADS):
                cols = slice(h * DV, (h + 1) * DV)
                a = jnp.where(mats.causal, _nt((qem * masks[h]).astype(BF16), kem_b), 0.0)
                o_ref[rows, cols] = _nn(a.astype(BF16), v_ref[rows, cols])
            qe0_b = (d.q * d.eb).astype(BF16)
            kdec_b = (d.k * d.elb).astype(BF16)
            for c in range(SUB // CHUNK):
                loc = slice(c * CHUNK, (c + 1) * CHUNK)
                glob = slice(sb * SUB + c * CHUNK, sb * SUB + (c + 1) * CHUNK)
                st = state[...]
                st_b = st.astype(BF16)
                st_ref[sb * (SUB // CHUNK) + c] = st_b
                o_ref[glob, :] += _nt(qe0_b[loc], st_b)
                u = _tn(v_ref[glob, :], kdec_b[loc])
                state[...] = st * d.ebl[c * CHUNK:c * CHUNK + 1] + jnp.where(mats.heads, u, 0.0)

        gnv = gn_ref[...]
        for h in range(HEADS):
            cols = slice(h * DV, (h + 1) * DV)
            oh = o_ref[:, cols]
            r = lax.rsqrt(_rowmean(oh * oh) + EPS)
            gh = g_ref[:, cols].astype(F32)
            mix_ref[:, cols] = (oh * r * gnv * (gh * jax.nn.sigmoid(gh))).astype(BF16)
        comm_wait(refs)

    nc = T // CHUNK
    res = pl.pallas_call(
        body, name="gla_fwd", grid=(T // tb,),
        out_shape=[jax.ShapeDtypeStruct((T, VAL), BF16), jax.ShapeDtypeStruct((T, VAL), F32),
                   jax.ShapeDtypeStruct((nc, VAL, KEY), BF16)] + _comm_out_shapes(kinds, shards),
        in_specs=[pl.BlockSpec((tb, KEY), lambda i: (i, P_Q // KEY)), pl.BlockSpec((tb, KEY), lambda i: (i, P_K // KEY)),
                  pl.BlockSpec((tb, VAL), lambda i: (i, P_V // VAL)), pl.BlockSpec((tb, VAL), lambda i: (i, P_G // VAL)),
                  pl.BlockSpec((tb, Z_PAD), lambda i: (i, P_Z // Z_PAD)),
                  _const((Z_PAD, KEY)), _const((1, KEY)), _const((1, DV))] + [ANY] * n_comm,
        out_specs=[pl.BlockSpec((tb, VAL), lambda i: (i, 0)), pl.BlockSpec((tb, VAL), lambda i: (i, 0)),
                   pl.BlockSpec((cpb, VAL, KEY), lambda i: (i, 0, 0))] + [ANY] * n_comm,
        scratch_shapes=[pltpu.VMEM((VAL, KEY), F32)] + _comm_scratch(n_comm),
        compiler_params=_params(("arbitrary",)),
    )(proj, proj, proj, proj, proj, wg, bg, gn, *shards)
    return res[0], res[1], res[2], res[3:]


def _group_mean(v, gmat):
    return _nn(v.astype(BF16), gmat)


def _shifted_copies(buf, sh, rows):
    for k in range(1, SUBLANES):
        sh[k - 1] = buf[pl.ds(k, rows), :]


def _tap(buf, sh, off, r0):
    k, base = off % SUBLANES, off - off % SUBLANES
    rows = pl.ds(pl.multiple_of(r0 + base, SUBLANES), STRIP)
    return buf[rows, :] if k == 0 else sh[k - 1, rows, :]


def _conv_fwd(proj, conv_w, conv_b, cn_g, cn_b, gmat, shards):
    T = proj.shape[0]
    tm = min(T, 512)
    sh_rows = tm + HALO - SUBLANES
    n_comm = len(shards)
    kinds = ["gather"] * n_comm
    comm_start, comm_wait = _hosted_comm(kinds, 7, 2, n_comm, T // tm)

    def body(*refs):
        ci_ref, cg_ref, w_ref, b_ref, g_ref, be_ref, gm_ref = refs[:7]
        mix_ref, uc_ref = refs[7 + n_comm:9 + n_comm]
        ubuf, ush = refs[9 + 2 * n_comm:11 + 2 * n_comm]
        comm_start(refs)

        @pl.when(pl.program_id(0) == 0)
        def _():
            ubuf[0:HALO, :] = jnp.zeros((HALO, CONV), F32)

        ubuf[HALO:, :] = ci_ref[...].astype(F32) * jax.nn.sigmoid(cg_ref[...].astype(F32))
        _shifted_copies(ubuf, ush, sh_rows)

        def strip(s, carry):
            r0 = pl.multiple_of(s * STRIP, STRIP)
            acc = jnp.zeros((STRIP, CONV), F32) + b_ref[...]
            for j in range(CONV_W):
                acc = acc + w_ref[j:j + 1, :] * _tap(ubuf, ush, HALO - (CONV_W - 1) + j, r0)
            uc_ref[pl.ds(r0, STRIP), :] = acc
            return carry

        lax.fori_loop(0, tm // STRIP, strip, 0)
        ubuf[0:HALO, :] = ubuf[tm:tm + HALO, :]
        gm = gm_ref[...]
        ucv = uc_ref[...]
        d = ucv - _group_mean(ucv, gm)
        var = _group_mean(d * d, gm)
        yn = d * lax.rsqrt(var + EPS) * g_ref[...] + be_ref[...]
        mix_ref[...] = (yn * jax.nn.sigmoid(yn)).astype(BF16)
        comm_wait(refs)

    res = pl.pallas_call(
        body, name="conv_fwd", grid=(T // tm,),
        out_shape=[jax.ShapeDtypeStruct((T, CONV), BF16), jax.ShapeDtypeStruct((T, CONV), F32)]
        + _comm_out_shapes(kinds, shards),
        in_specs=[pl.BlockSpec((tm, CONV), lambda i: (i, P_CI // CONV)), pl.BlockSpec((tm, CONV), lambda i: (i, P_CG // CONV)),
                  _const((HALO, CONV)), _const((1, CONV)), _const((1, CONV)), _const((1, CONV)), _const((CONV, CONV))]
        + [ANY] * n_comm,
        out_specs=[pl.BlockSpec((tm, CONV), lambda i: (i, 0)), pl.BlockSpec((tm, CONV), lambda i: (i, 0))]
        + [ANY] * n_comm,
        scratch_shapes=[pltpu.VMEM((tm + HALO, CONV), F32), pltpu.VMEM((SUBLANES - 1, sh_rows, CONV), F32)]
        + _comm_scratch(n_comm),
        compiler_params=_params(("arbitrary",)),
    )(proj, proj, conv_w, conv_b, cn_g, cn_b, gmat, *shards)
    return res[0], res[1], res[2:]


def _rms_bwd(dy, xhat, r, g):
    dyg = dy * g
    return r * (dyg - xhat * _rowmean(dyg * xhat))


def _mlp_fwd_bwd(x, mix_a, mix_c, tgt, w_out, g2, w1, w2, gf):
    T = x.shape[0]
    tm = min(T, 256)
    inv_d = 1.0 / D_MODEL

    def body(x_ref, ma_ref, mc_ref, t_ref, wo_ref, g2_ref, w1_ref, w2_ref, gf_ref,
             dh1_ref, dmix_ref, hn_ref, ff_ref, da_ref, dh2_ref, loss_ref, dgf_ref, dg2_ref):
        @pl.when(pl.program_id(0) == 0)
        def _():
            loss_ref[...] = jnp.zeros_like(loss_ref)
            dgf_ref[...] = jnp.zeros_like(dgf_ref)
            dg2_ref[...] = jnp.zeros_like(dg2_ref)

        g2v, gfv = g2_ref[...], gf_ref[...]
        h1 = x_ref[...] + _nn(ma_ref[...], wo_ref[0:VAL, :]) + _nn(mc_ref[...], wo_ref[VAL:, :])
        r2 = lax.rsqrt(_rowmean(h1 * h1) + EPS)
        h1hat = h1 * r2
        hn = (h1hat * g2v).astype(BF16)
        hn_ref[...] = hn
        relu_a = jnp.maximum(_nn(hn, w1_ref[...]), 0.0)
        ff = (relu_a * relu_a).astype(BF16)
        ff_ref[...] = ff
        h2 = h1 + _nn(ff, w2_ref[...])
        rf = lax.rsqrt(_rowmean(h2 * h2) + EPS)
        h2hat = h2 * rf
        err = h2hat * gfv - t_ref[...]
        loss_ref[...] += (0.5 * inv_d) * _colsum(jnp.sum(err * err, axis=1, keepdims=True))
        dy = err * inv_d
        dgf_ref[...] += _colsum(dy * h2hat)
        dh2 = _rms_bwd(dy, h2hat, rf, gfv)
        dh2_b = dh2.astype(BF16)
        dh2_ref[...] = dh2_b
        da = (_nt(dh2_b, w2_ref[...]) * (2.0 * relu_a)).astype(BF16)
        da_ref[...] = da
        dhn = _nt(da, w1_ref[...])
        dg2_ref[...] += _colsum(dhn * h1hat)
        dh1 = dh2 + _rms_bwd(dhn, h1hat, r2, g2v)
        dh1_ref[...] = dh1
        dmix_ref[...] = _nt(dh1.astype(BF16), wo_ref[...]).astype(BF16)

    tok = lambda w: pl.BlockSpec((tm, w), lambda i: (i, 0))
    return pl.pallas_call(
        body, name="mlp_fwd_bwd", grid=(T // tm,),
        out_shape=[jax.ShapeDtypeStruct((T, D_MODEL), F32), jax.ShapeDtypeStruct((T, D_MODEL), BF16),
                   jax.ShapeDtypeStruct((T, D_MODEL), BF16), jax.ShapeDtypeStruct((T, D_FF), BF16),
                   jax.ShapeDtypeStruct((T, D_FF), BF16), jax.ShapeDtypeStruct((T, D_MODEL), BF16),
                   jax.ShapeDtypeStruct((1, 1), F32), jax.ShapeDtypeStruct((1, D_MODEL), F32),
                   jax.ShapeDtypeStruct((1, D_MODEL), F32)],
        in_specs=[tok(D_MODEL), tok(VAL), tok(CONV), tok(D_MODEL), _const((D_MODEL, D_MODEL)), _const((1, D_MODEL)),
                  _const((D_MODEL, D_FF)), _const((D_FF, D_MODEL)), _const((1, D_MODEL))],
        out_specs=[tok(D_MODEL), tok(D_MODEL), tok(D_MODEL), tok(D_FF), tok(D_FF), tok(D_MODEL),
                   pl.BlockSpec((1, 1), lambda i: (0, 0)), pl.BlockSpec((1, D_MODEL), lambda i: (0, 0)),
                   pl.BlockSpec((1, D_MODEL), lambda i: (0, 0))],
        compiler_params=_params(("arbitrary",)),
    )(x, mix_a, mix_c, tgt, w_out, g2, w1, w2, gf)


def _silu_grad(v, s):
    return s * (1.0 + v * (1.0 - s))


def _conv_bwd(proj, uc, dmix, conv_w, cn_g, cn_b, gmat):
    T = proj.shape[0]
    tm = min(T, 512)
    nt = T // tm
    sh_rows = tm + HALO - SUBLANES
    n_strips = tm // STRIP

    def body(ci_ref, cg_ref, uc_ref, dm_ref, w_ref, g_ref, be_ref, gm_ref,
             dci_ref, dcg_ref, dw_ref, db_ref, dg_ref, dbe_ref, dbuf, dsh, dwacc):
        step = pl.program_id(0)

        @pl.when(step == 0)
        def _():
            dbuf[tm:, :] = jnp.zeros((HALO, CONV), F32)
            dwacc[...] = jnp.zeros_like(dwacc)
            db_ref[...] = jnp.zeros_like(db_ref)
            dg_ref[...] = jnp.zeros_like(dg_ref)
            dbe_ref[...] = jnp.zeros_like(dbe_ref)

        gm, gv = gm_ref[...], g_ref[...]
        ucv = uc_ref[...]
        d = ucv - _group_mean(ucv, gm)
        rs = lax.rsqrt(_group_mean(d * d, gm) + EPS)
        yhat = d * rs
        yn = yhat * gv + be_ref[...]
        dyn = dm_ref[...].astype(F32) * _silu_grad(yn, jax.nn.sigmoid(yn))
        dg_ref[...] += _colsum(dyn * yhat)
        dbe_ref[...] += _colsum(dyn)
        dyh = dyn * gv
        duc = rs * (dyh - _group_mean(dyh, gm) - yhat * _group_mean(dyh * yhat, gm))
        db_ref[...] += _colsum(duc)
        dbuf[0:tm, :] = duc
        _shifted_copies(dbuf, dsh, sh_rows)

        def strip(s, carry):
            r0 = pl.multiple_of(s * STRIP, STRIP)
            rows = pl.ds(r0, STRIP)
            cin = ci_ref[rows, :].astype(F32)
            sg = jax.nn.sigmoid(cg_ref[rows, :].astype(F32))
            u = cin * sg
            du = jnp.zeros((STRIP, CONV), F32)
            for j in range(CONV_W):
                dj = _tap(dbuf, dsh, CONV_W - 1 - j, r0)
                du = du + w_ref[j:j + 1, :] * dj
                p = u * dj
                fold = p[0:SUBLANES]
                for q in range(1, STRIP // SUBLANES):
                    fold = fold + p[q * SUBLANES:(q + 1) * SUBLANES, :]
                dwacc[j * SUBLANES:(j + 1) * SUBLANES, :] += fold
            dci_ref[rows, :] = (du * sg).astype(BF16)
            dcg_ref[rows, :] = (du * cin * sg * (1.0 - sg)).astype(BF16)
            return carry

        lax.fori_loop(0, n_strips, strip, 0)
        dbuf[tm:, :] = dbuf[0:HALO, :]

        @pl.when(step == nt - 1)
        def _():
            dw_ref[...] = jnp.zeros_like(dw_ref)
            for j in range(CONV_W):
                dw_ref[j:j + 1, :] = _colsum(dwacc[j * SUBLANES:(j + 1) * SUBLANES, :])

    rev = lambda i: nt - 1 - i
    tile = lambda col: pl.BlockSpec((tm, CONV), lambda i: (rev(i), col))
    acc = lambda rows: pl.BlockSpec((rows, CONV), lambda i: (0, 0))
    return pl.pallas_call(
        body, name="conv_bwd", grid=(nt,),
        out_shape=[jax.ShapeDtypeStruct((T, CONV), BF16), jax.ShapeDtypeStruct((T, CONV), BF16),
                   jax.ShapeDtypeStruct((HALO, CONV), F32), jax.ShapeDtypeStruct((1, CONV), F32),
                   jax.ShapeDtypeStruct((1, CONV), F32), jax.ShapeDtypeStruct((1, CONV), F32)],
        in_specs=[tile(P_CI // CONV), tile(P_CG // CONV), tile(0), tile(1),
                  _const((HALO, CONV)), _const((1, CONV)), _const((1, CONV)), _const((CONV, CONV))],
        out_specs=[tile(0), tile(0), acc(HALO), acc(1), acc(1), acc(1)],
        scratch_shapes=[pltpu.VMEM((tm + HALO, CONV), F32), pltpu.VMEM((SUBLANES - 1, sh_rows, CONV), F32),
                        pltpu.VMEM((HALO * SUBLANES, CONV), F32)],
        compiler_params=_params(("arbitrary",)),
    )(proj, proj, uc, dmix, conv_w, cn_g, cn_b, gmat)


def _gla_bwd(proj, o, states, dmix, wg, bg, gn, parts):
    T = proj.shape[0]
    tb = min(T, 512)
    cpb = tb // CHUNK
    nb = T // tb
    n_comm = len(parts)
    kinds = ["exchange"] * n_comm
    comm_start, comm_wait = _hosted_comm(kinds, 11, 8, n_comm, nb)

    def body(*refs):
        q_ref, k_ref, v_ref, g_ref, z_ref, o_ref, st_ref, dm_ref, wg_ref, bg_ref, gn_ref = refs[:11]
        dq_ref, dk_ref, dv_ref, dg_ref, dz_ref, dwg_ref, dbg_ref, dgn_ref = refs[11 + n_comm:19 + n_comm]
        dstate, do_scr, dv_scr = refs[19 + 2 * n_comm:22 + 2 * n_comm]
        comm_start(refs)

        @pl.when(pl.program_id(0) == 0)
        def _():
            dstate[...] = jnp.zeros_like(dstate)
            dwg_ref[...] = jnp.zeros_like(dwg_ref)
            dbg_ref[...] = jnp.zeros_like(dbg_ref)
            dgn_ref[...] = jnp.zeros_like(dgn_ref)

        gnv = gn_ref[...]
        dgn = jnp.zeros((1, DV), F32)
        for h in range(HEADS):
            cols = slice(h * DV, (h + 1) * DV)
            oh = o_ref[:, cols]
            r = lax.rsqrt(_rowmean(oh * oh) + EPS)
            ohat = oh * r
            gh = g_ref[:, cols].astype(F32)
            sg = jax.nn.sigmoid(gh)
            dmx = dm_ref[:, cols].astype(F32)
            don = dmx * (gh * sg)
            dg_ref[:, cols] = (dmx * (ohat * gnv) * _silu_grad(gh, sg)).astype(BF16)
            dgn = dgn + _colsum(don * ohat)
            do_scr[:, cols] = _rms_bwd(don, ohat, r, gnv)
        dgn_ref[...] += dgn

        mats = _chunk_matrices()
        masks = _head_masks()
        wgv, bgv = wg_ref[...], bg_ref[...]
        n_chunks = SUB // CHUNK

        for sb in reversed(range(tb // SUB)):
            rows = slice(sb * SUB, (sb + 1) * SUB)
            zs = z_ref[rows, :]
            d = _decay_terms(zs, q_ref[rows, :], k_ref[rows, :], wgv, bgv, mats)
            qem = d.q * d.ebm
            qem_b = qem.astype(BF16)
            kem_b = (d.k * d.emb).astype(BF16)
            dq = jnp.zeros((SUB, KEY), F32)
            dk = jnp.zeros((SUB, KEY), F32)
            for h in range(HEADS):
                hm = masks[h]
                cols = slice(h * DV, (h + 1) * DV)
                do_b = do_scr[rows, cols].astype(BF16)
                vh = v_ref[rows, cols]
                da = jnp.where(mats.causal, _nt(do_b, vh), 0.0).astype(BF16)
                da_t = jnp.where(mats.causal_t, _nt(vh, do_b), 0.0).astype(BF16)
                a_t = jnp.where(mats.causal_t, _nt(kem_b, (qem * hm).astype(BF16)), 0.0).astype(BF16)
                dq = dq + hm * _nn(da, kem_b)
                dk = dk + hm * _nn(da_t, qem_b)
                dv_scr[rows, cols] = _nn(a_t, do_b)
            dq = dq * d.ebm
            dk = dk * d.emb

            qe0_b = (d.q * d.eb).astype(BF16)
            kdec_b = (d.k * d.elb).astype(BF16)
            dq_st, dk_st, last = [None] * n_chunks, [None] * n_chunks, [None] * n_chunks
            for c in reversed(range(n_chunks)):
                loc = slice(c * CHUNK, (c + 1) * CHUNK)
                glob = slice(sb * SUB + c * CHUNK, sb * SUB + (c + 1) * CHUNK)
                st_b = st_ref[sb * n_chunks + c]
                ds = dstate[...]
                ds_b = ds.astype(BF16)
                do_c = do_scr[glob, :].astype(BF16)
                ebl_c = d.ebl[c * CHUNK:c * CHUNK + 1]
                dk_c = _nn(v_ref[glob, :], ds_b) * d.elb[loc]
                dq_st[c] = _nn(do_c, st_b) * d.eb[loc]
                dk_st[c] = dk_c
                last_c = _colsum(d.k[loc] * dk_c) + ebl_c * _colsum(st_b.astype(F32) * ds)
                last[c] = jnp.broadcast_to(last_c, (CHUNK, KEY))
                dv_ref[glob, :] = (dv_scr[glob, :] + _nt(kdec_b[loc], ds_b)).astype(BF16)
                dstate[...] = ds * ebl_c + jnp.where(mats.heads, _tn(do_c, qe0_b[loc]), 0.0)
            dq = dq + jnp.concatenate(dq_st, axis=0)
            dk = dk + jnp.concatenate(dk_st, axis=0)
            dq_ref[rows, :] = (dq * Q_SCALE).astype(BF16)
            dk_ref[rows, :] = dk.astype(BF16)
            hi, lo = _split_bf16(d.q * dq - d.k * dk)
            dla = _nn(mats.tri_t, hi) + _nn(mats.tri_t, lo) + jnp.concatenate(last, axis=0)
            dal = dla * (1.0 / GATE_TAU) * jax.nn.sigmoid(-d.al)
            dal_b = dal.astype(BF16)
            dz_ref[rows, :] = _nt(dal_b, wgv).astype(BF16)
            dwg_ref[...] += _tn(zs, dal_b)
            dbg_ref[...] += _colsum(dal)
        comm_wait(refs)

    rev = lambda i: nb - 1 - i
    blk = lambda w, col: pl.BlockSpec((tb, w), lambda i: (rev(i), col))
    res = pl.pallas_call(
        body, name="gla_bwd", grid=(nb,),
        out_shape=[jax.ShapeDtypeStruct((T, KEY), BF16), jax.ShapeDtypeStruct((T, KEY), BF16),
                   jax.ShapeDtypeStruct((T, VAL), BF16), jax.ShapeDtypeStruct((T, VAL), BF16),
                   jax.ShapeDtypeStruct((T, Z_PAD), BF16), jax.ShapeDtypeStruct((Z_PAD, KEY), F32),
                   jax.ShapeDtypeStruct((1, KEY), F32), jax.ShapeDtypeStruct((1, DV), F32)]
        + _comm_out_shapes(kinds, parts),
        in_specs=[blk(KEY, P_Q // KEY), blk(KEY, P_K // KEY), blk(VAL, P_V // VAL), blk(VAL, P_G // VAL),
                  blk(Z_PAD, P_Z // Z_PAD), blk(VAL, 0),
                  pl.BlockSpec((cpb, VAL, KEY), lambda i: (rev(i), 0, 0)), blk(VAL, 0),
                  _const((Z_PAD, KEY)), _const((1, KEY)), _const((1, DV))] + [ANY] * n_comm,
        out_specs=[blk(KEY, 0), blk(KEY, 0), blk(VAL, 0), blk(VAL, 0), blk(Z_PAD, 0),
                   pl.BlockSpec((Z_PAD, KEY), lambda i: (0, 0)), pl.BlockSpec((1, KEY), lambda i: (0, 0)),
                   pl.BlockSpec((1, DV), lambda i: (0, 0))] + [ANY] * n_comm,
        scratch_shapes=[pltpu.VMEM((VAL, KEY), F32), pltpu.VMEM((tb, VAL), F32), pltpu.VMEM((tb, VAL), F32)]
        + _comm_scratch(n_comm),
        compiler_params=_params(("arbitrary",)),
    )(proj, proj, proj, proj, proj, o, states, dmix, wg, bg, gn, *parts)
    return res[:8], res[8:]


def _inproj_bwd(x, g1, w_in_p, dh1, dq, dk, dv, dg, dci, dcg, dz):
    T = x.shape[0]
    tm = min(T, 512)

    def body(x_ref, g_ref, w_ref, dh1_ref, dq_ref, dk_ref, dv_ref, dg_ref, dci_ref, dcg_ref, dz_ref,
             dx_ref, dp_ref, dg1_ref):
        @pl.when(pl.program_id(0) == 0)
        def _():
            dg1_ref[...] = jnp.zeros_like(dg1_ref)

        dp_ref[:, P_Q:P_K] = dq_ref[...]
        dp_ref[:, P_K:P_V] = dk_ref[...]
        dp_ref[:, P_V:P_G] = dv_ref[...]
        dp_ref[:, P_G:P_CI] = dg_ref[...]
        dp_ref[:, P_CI:P_CG] = dci_ref[...]
        dp_ref[:, P_CG:P_Z] = dcg_ref[...]
        dp_ref[:, P_Z:] = dz_ref[...]
        dxn = _nt(dp_ref[...], w_ref[...])
        xv = x_ref[...]
        r = lax.rsqrt(_rowmean(xv * xv) + EPS)
        xhat = xv * r
        dg1_ref[...] += _colsum(dxn * xhat)
        dx_ref[...] = dh1_ref[...] + _rms_bwd(dxn, xhat, r, g_ref[...])

    tok = lambda w: pl.BlockSpec((tm, w), lambda i: (i, 0))
    return pl.pallas_call(
        body, name="inproj_bwd", grid=(T // tm,),
        out_shape=[jax.ShapeDtypeStruct((T, D_MODEL), F32), jax.ShapeDtypeStruct((T, D_INP), BF16),
                   jax.ShapeDtypeStruct((1, D_MODEL), F32)],
        in_specs=[tok(D_MODEL), _const((1, D_MODEL)), _const((D_MODEL, D_INP)), tok(D_MODEL), tok(KEY), tok(KEY),
                  tok(VAL), tok(VAL), tok(CONV), tok(CONV), tok(Z_PAD)],
        out_specs=[tok(D_MODEL), tok(D_INP), pl.BlockSpec((1, D_MODEL), lambda i: (0, 0))],
        compiler_params=_params(("arbitrary",)),
    )(x, g1, w_in_p, dh1, dq, dk, dv, dg, dci, dcg, dz)


def _wgrad(a, b, name, tk, tn, col_block=None):
    T, K = a.shape
    N = b.shape[1]
    tt = min(T, 2048)
    nt = T // tt

    def body(a_ref, b_ref, o_ref, acc):
        @pl.when(pl.program_id(2) == 0)
        def _():
            acc[...] = jnp.zeros_like(acc)

        acc[...] += _tn(a_ref[...], b_ref[...].astype(BF16))

        @pl.when(pl.program_id(2) == nt - 1)
        def _():
            if col_block is None:
                o_ref[...] = acc[...].astype(BF16)
            else:
                for q in range(tn // col_block):
                    o_ref[q] = acc[:, q * col_block:(q + 1) * col_block].astype(BF16)

    if col_block is None:
        out_shape = jax.ShapeDtypeStruct((K, N), BF16)
        out_spec = pl.BlockSpec((tk, tn), lambda i, j, t: (i, j))
    else:
        assert tk == K
        out_shape = jax.ShapeDtypeStruct((N // col_block, K, col_block), BF16)
        out_spec = pl.BlockSpec((tn // col_block, tk, col_block), lambda i, j, t: (j, 0, 0))
    return pl.pallas_call(
        body, name=name, grid=(K // tk, N // tn, nt), out_shape=out_shape,
        in_specs=[pl.BlockSpec((tt, tk), lambda i, j, t: (t, i)), pl.BlockSpec((tt, tn), lambda i, j, t: (t, j))],
        out_specs=out_spec, scratch_shapes=[pltpu.VMEM((tk, tn), F32)],
        compiler_params=_params(("arbitrary", "arbitrary", "arbitrary")),
    )(a, b)


def _adam_math(w, g, m, v):
    m = ADAM_B1 * m + (1.0 - ADAM_B1) * g
    v = ADAM_B2 * v + (1.0 - ADAM_B2) * (g * g)
    m_hat = m / (1.0 - ADAM_B1 ** ADAM_STEP)
    v_hat = v / (1.0 - ADAM_B2 ** ADAM_STEP)
    delta = -ADAM_LR * (m_hat / (jnp.sqrt(v_hat) + ADAM_EPS) + ADAM_WD * w)
    return delta, m, v


def _sum8(ref):
    g = ref[0].astype(F32)
    for s in range(1, N_DEV):
        g = g + ref[s].astype(F32)
    return g


def _adam_big(parts, w, m, v, name):
    R, C = w.shape
    tr = min(R, 128)

    def body(p_ref, w_ref, m_ref, v_ref, g_ref, d_ref, nm_ref, nv_ref):
        g = _sum8(p_ref)
        g_ref[...] = g
        d_ref[...], nm_ref[...], nv_ref[...] = _adam_math(w_ref[...], g, m_ref[...], v_ref[...])

    row = pl.BlockSpec((tr, C), lambda i: (i, 0))
    return pl.pallas_call(
        body, name=name, grid=(R // tr,), out_shape=[jax.ShapeDtypeStruct((R, C), F32)] * 4,
        in_specs=[pl.BlockSpec((N_DEV, tr, C), lambda i: (0, i, 0)), row, row, row], out_specs=[row] * 4,
        compiler_params=_params(("arbitrary",)),
    )(parts, w, m, v)


def _sum_small(parts):
    def body(p_ref, o_ref):
        o_ref[...] = _sum8(p_ref)

    return pl.pallas_call(body, name="sum_small", out_shape=jax.ShapeDtypeStruct(parts.shape[1:], F32))(parts)


def _adam_small(gs, ws, ms, vs):
    n = len(gs)

    def body(*refs):
        g_refs, w_refs, m_refs, v_refs = refs[:n], refs[n:2 * n], refs[2 * n:3 * n], refs[3 * n:4 * n]
        outs = refs[4 * n:]
        for i in range(n):
            d, nm, nv = _adam_math(w_refs[i][...], g_refs[i][...], m_refs[i][...], v_refs[i][...])
            outs[i][...] = d
            outs[n + i][...] = nm
            outs[2 * n + i][...] = nv

    shapes = [jax.ShapeDtypeStruct(w.shape, F32) for w in ws]
    res = pl.pallas_call(body, name="adam_small", out_shape=shapes * 3)(*gs, *ws, *ms, *vs)
    return res[:n], res[n:2 * n], res[2 * n:]


def _permute_in(w):
    pad = jnp.zeros(w.shape[:-1] + (D_INP - D_IN,), w.dtype)
    return jnp.concatenate([w[..., :1536], w[..., 1552:], w[..., 1536:1552], pad], axis=-1)


def _unpermute_in(w):
    return jnp.concatenate([w[..., :P_CI], w[..., P_Z:P_Z + RANK], w[..., P_CI:P_Z]], axis=-1)


def _group_matrix():
    gi = lax.broadcasted_iota(jnp.int32, (CONV, CONV), 0) // (CONV // GROUPS)
    gj = lax.broadcasted_iota(jnp.int32, (CONV, CONV), 1) // (CONV // GROUPS)
    return jnp.where(gi == gj, GROUPS / CONV, 0.0).astype(BF16)


_SMALL = [("loss", 8), ("dg1", 8), ("dbg", 2), ("dgn", 1), ("dconv_b", 4), ("dcn_g", 4), ("dcn_b", 4), ("dg2", 8),
          ("dgf", 8), ("dwg", 32), ("dconv_w", 124)]


def _pad8(rows):
    return -(-rows // 8) * 8


def kernel(x, norm1_g, w_in, w_gate_up, b_gate, gla_norm_g, conv_w, conv_b, conv_norm_g, conv_norm_b, w_out, norm2_g, w_mlp_in, w_mlp_out, final_norm_g, loss_target, m_norm1_g, m_w_in, m_w_gate_up, m_b_gate, m_gla_norm_g, m_conv_w, m_conv_b, m_conv_norm_g, m_conv_norm_b, m_w_out, m_norm2_g, m_w_mlp_in, m_w_mlp_out, m_final_norm_g, v_norm1_g, v_w_in, v_w_gate_up, v_b_gate, v_gla_norm_g, v_conv_w, v_conv_b, v_conv_norm_g, v_conv_norm_b, v_w_out, v_norm2_g, v_w_mlp_in, v_w_mlp_out, v_final_norm_g):
    x_idx = lax.axis_index("x")
    y_idx = lax.axis_index("y")
    c_idx = lax.axis_index("c")
    me = 4 * x_idx + 2 * y_idx + c_idx
    pad_in = lambda a: jnp.pad(a, ((0, 0), (0, SHARD_IN_PAD - SHARD_IN)))
    xs, tgt = x[0], loss_target[0]
    gf = final_norm_g.reshape(1, D_MODEL)
    gmat = _group_matrix()

    small_shard = jnp.zeros((48, 128), F32)
    small_shard = small_shard.at[0:RANK, 0:KEY // N_DEV].set(w_gate_up[0])
    small_shard = small_shard.at[RANK:RANK + CONV_W, 0:CONV // N_DEV].set(conv_w[0])
    g_in, g_small = _comm(["gather", "gather"], [pad_in(w_in[0]).astype(BF16), small_shard], "gather_w_in")
    w_in_p = _permute_in(jnp.concatenate([g_in[d, :, :SHARD_IN] for d in range(N_DEV)], axis=1))
    wg_full = jnp.concatenate([g_small[d, 0:RANK, 0:KEY // N_DEV] for d in range(N_DEV)], axis=1)
    wg_pad = jnp.pad(wg_full, ((0, Z_PAD - RANK), (0, 0))).astype(BF16)
    conv_w_full = jnp.concatenate([g_small[d, RANK:RANK + CONV_W, 0:CONV // N_DEV] for d in range(N_DEV)], axis=1)
    conv_w_pad = jnp.pad(conv_w_full, ((0, HALO - CONV_W), (0, 0)))

    proj, xn = _inproj_fwd(xs, norm1_g, w_in_p)
    mix_a, o, states, (g_out, g_w1) = _gla_fwd(
        proj, wg_pad, b_gate, gla_norm_g, [w_out[0].astype(BF16), w_mlp_in[0].astype(BF16)])
    mix_c, uc, (g_w2,) = _conv_fwd(proj, conv_w_pad, conv_b, conv_norm_g, conv_norm_b, gmat,
                                   [w_mlp_out[0].astype(BF16)])
    w_out_full = g_out.reshape(D_MODEL, D_MODEL)
    w1_full = jnp.concatenate([g_w1[d] for d in range(N_DEV)], axis=1)
    w2_full = g_w2.reshape(D_FF, D_MODEL)
    dh1, dmix, hn, ff, da, dh2, loss, dgf, dg2 = _mlp_fwd_bwd(xs, mix_a, mix_c, tgt, w_out_full, norm2_g, w1_full,
                                                              w2_full, gf)

    dw1 = _wgrad(hn, da, "wgrad_mlp_in", 1024, 1024, col_block=D_FF // N_DEV)
    dw2 = _wgrad(ff, dh2, "wgrad_mlp_out", 1024, 1024)
    dw_out = jnp.concatenate([_wgrad(mix_a, dh1, "wgrad_out_a", VAL, 1024),
                              _wgrad(mix_c, dh1, "wgrad_out_c", CONV, 1024)], axis=0)
    dci, dcg, dconv_w, dconv_b, dcn_g, dcn_b = _conv_bwd(proj, uc, dmix, conv_w_pad, conv_norm_g, conv_norm_b, gmat)
    (dq, dk, dv, dg, dz, dwg, dbg, dgn), (p_w1, p_w2, p_out) = _gla_bwd(
        proj, o, states, dmix, wg_pad, b_gate, gla_norm_g,
        [dw1, dw2.reshape(N_DEV, D_FF // N_DEV, D_MODEL), dw_out.reshape(N_DEV, D_MODEL // N_DEV, D_MODEL)])
    dx, dproj, dg1 = _inproj_bwd(xs, norm1_g, w_in_p, dh1, dq, dk, dv, dg, dci, dcg, dz)
    dw_in_p = _wgrad(xn, dproj, "wgrad_in", 1024, 896)

    dw_in = _unpermute_in(dw_in_p).reshape(D_MODEL, N_DEV, SHARD_IN).transpose(1, 0, 2)
    dw_in = jnp.pad(dw_in, ((0, 0), (0, 0), (0, SHARD_IN_PAD - SHARD_IN)))
    small = dict(loss=jnp.zeros((8, 128), F32) + loss, dg1=dg1, dbg=dbg, dgn=dgn, dconv_b=dconv_b, dcn_g=dcn_g,
                 dcn_b=dcn_b, dg2=dg2, dgf=dgf, dwg=dwg[0:RANK], dconv_w=dconv_w[0:CONV_W])
    pack = jnp.concatenate([jnp.pad(small[name].reshape(rows, 128), ((0, _pad8(rows) - rows), (0, 0)))
                            for name, rows in _SMALL], axis=0)
    p_in, g_pack = _comm(["exchange", "gather"], [dw_in, pack], "exchange_w_in")

    gi, di, mi, vi = _adam_big(p_in, pad_in(w_in[0]), pad_in(m_w_in[0]), pad_in(v_w_in[0]), "adam_w_in")
    go, do, mo, vo = _adam_big(p_out, w_out[0], m_w_out[0], v_w_out[0], "adam_w_out")
    ga, da_, ma, va = _adam_big(p_w1, w_mlp_in[0], m_w_mlp_in[0], v_w_mlp_in[0], "adam_w_mlp_in")
    gb, db, mb, vb = _adam_big(p_w2, w_mlp_out[0], m_w_mlp_out[0], v_w_mlp_out[0], "adam_w_mlp_out")
    cut = lambda a: a[:, :SHARD_IN][None]

    summed = _sum_small(g_pack)
    small_g = {}
    at = 0
    for name, rows in _SMALL:
        small_g[name] = summed[at:at + rows]
        at += _pad8(rows)
    loss_out = small_g["loss"][0, 0]
    wg_cols = KEY // N_DEV
    cw_cols = CONV // N_DEV
    g_small_list = [
        small_g["dg1"].reshape(1, D_MODEL),
        lax.dynamic_slice_in_dim(small_g["dwg"].reshape(RANK, KEY), me * wg_cols, wg_cols, axis=1)[None],
        small_g["dbg"].reshape(1, KEY), small_g["dgn"].reshape(1, DV),
        lax.dynamic_slice_in_dim(small_g["dconv_w"].reshape(CONV_W, CONV), me * cw_cols, cw_cols, axis=1)[None],
        small_g["dconv_b"].reshape(1, CONV), small_g["dcn_g"].reshape(1, CONV), small_g["dcn_b"].reshape(1, CONV),
        small_g["dg2"].reshape(1, D_MODEL), small_g["dgf"].reshape(1, D_MODEL),
    ]
    row = lambda a: a.reshape(1, D_MODEL)
    w_small = [norm1_g, w_gate_up, b_gate, gla_norm_g, conv_w, conv_b, conv_norm_g, conv_norm_b, norm2_g,
               row(final_norm_g)]
    m_small = [m_norm1_g, m_w_gate_up, m_b_gate, m_gla_norm_g, m_conv_w, m_conv_b, m_conv_norm_g, m_conv_norm_b,
               m_norm2_g, row(m_final_norm_g)]
    v_small = [v_norm1_g, v_w_gate_up, v_b_gate, v_gla_norm_g, v_conv_w, v_conv_b, v_conv_norm_g, v_conv_norm_b,
               v_norm2_g, row(v_final_norm_g)]
    d_small, nm_small, nv_small = _adam_small(g_small_list, w_small, m_small, v_small)
    flat = lambda lst: list(lst[:-1]) + [lst[-1].reshape(D_MODEL)]
    g_small_list, d_small, nm_small, nv_small = flat(g_small_list), flat(d_small), flat(nm_small), flat(nv_small)

    def order(s, w_in_v, w_out_v, w1_v, w2_v):
        return [s[0], w_in_v, s[1], s[2], s[3], s[4], s[5], s[6], s[7], w_out_v, s[8], w1_v, w2_v, s[9]]

    grads = order(g_small_list, cut(gi), go[None], ga[None], gb[None])
    deltas = order(d_small, cut(di), do[None], da_[None], db[None])
    new_m = order(nm_small, cut(mi), mo[None], ma[None], mb[None])
    new_v = order(nv_small, cut(vi), vo[None], va[None], vb[None])
    return (loss_out, dx[None], *grads, *deltas, *new_m, *new_v)
```

```python
from typing import NamedTuple

import jax
import jax.numpy as jnp
from jax import lax
from jax.experimental import pallas as pl
from jax.experimental.pallas import tpu as pltpu

F32 = jnp.float32
BF16 = jnp.bfloat16

N_DEV = 8
D_MODEL = 1024
HEADS = 4
DK = 64
DV = 128
KEY = HEADS * DK
VAL = HEADS * DV
RANK = 16
CONV = 512
GROUPS = 8
CONV_W = 31
HALO = 32
SUBLANES = 8
STRIP = 32
D_FF = 4096
D_IN = 2576
SHARD_IN = D_IN // N_DEV
SHARD_IN_PAD = 384
CHUNK = 64
SUB = 256
EPS = 1e-6
GATE_TAU = 16.0
Q_SCALE = DK ** -0.5

P_Q, P_K, P_V, P_G, P_CI, P_CG, P_Z = 0, 256, 512, 1024, 1536, 2048, 2560
D_INP = 2688
Z_PAD = D_INP - P_Z

ADAM_LR = 0.001
ADAM_B1 = 0.9
ADAM_B2 = 0.999
ADAM_EPS = 1e-08
ADAM_WD = 0.01
ADAM_STEP = 10

VMEM_LIMIT = 56 * 1024 * 1024

MESH = pl.DeviceIdType.MESH
ANY = pl.BlockSpec(memory_space=pl.ANY)


def _nn(a, b):
    return jnp.dot(a, b, preferred_element_type=F32)


def _nt(a, b):
    return lax.dot_general(a, b, (((1,), (1,)), ((), ())), preferred_element_type=F32)


def _tn(a, b):
    return lax.dot_general(a, b, (((0,), (0,)), ((), ())), preferred_element_type=F32)


def _params(sem=None):
    return pltpu.CompilerParams(dimension_semantics=sem, vmem_limit_bytes=VMEM_LIMIT)


def _const(shape):
    return pl.BlockSpec(shape, lambda *_: (0,) * len(shape), pipeline_mode=pl.Buffered(1))


def _colsum(v):
    return jnp.sum(v, axis=0, keepdims=True)


def _rowmean(v):
    return jnp.mean(v, axis=-1, keepdims=True)


def _split_bf16(v):
    hi = v.astype(BF16)
    return hi, (v - hi.astype(F32)).astype(BF16)


def _my_place():
    return lax.axis_index("x"), lax.axis_index("y"), lax.axis_index("c")


def _peer(j):
    x, y, c = _my_place()
    jx, jy, jc = (j >> 2) & 1, (j >> 1) & 1, j & 1
    px = 1 - x if jx else x
    py = 1 - y if jy else y
    pc = 1 - c if jc else c
    return (px, py, pc), 4 * px + 2 * py + pc


def _comm_plan(kinds, ins, outs, send_sems, recv_sems, local_sems, receives=True):
    x, y, c = _my_place()
    me = 4 * x + 2 * y + c
    own = lambda k, idx: ins[k] if kinds[k] == "gather" else ins[k].at[idx]
    local = [pltpu.make_async_copy(own(k, me), outs[k].at[me], local_sems.at[k]) for k in range(len(kinds))]
    sends, recvs = [], []
    for j in range(1, N_DEV):
        peer, peer_idx = _peer(j)
        for k in range(len(kinds)):
            sems = dict(send_sem=send_sems.at[k, j - 1], recv_sem=recv_sems.at[k, j - 1], device_id=peer,
                        device_id_type=MESH)
            sends.append(pltpu.make_async_remote_copy(src_ref=own(k, peer_idx), dst_ref=outs[k].at[me], **sems))
            if receives:
                recvs.append(pltpu.make_async_remote_copy(src_ref=own(k, me), dst_ref=outs[k].at[peer_idx], **sems))
    return local, sends, recvs


def _comm_start(plan):
    local, sends, _ = plan
    for cp in local + sends:
        cp.start()


def _comm_wait(plan):
    local, sends, recvs = plan
    for cp in recvs:
        cp.wait_recv()
    for cp in sends:
        cp.wait_send()
    for cp in local:
        cp.wait()


def _comm_scratch(n):
    return [pltpu.SemaphoreType.DMA((n, N_DEV - 1)), pltpu.SemaphoreType.DMA((n, N_DEV - 1)),
            pltpu.SemaphoreType.DMA((n,))]


def _comm_out_shapes(kinds, arrays):
    return [jax.ShapeDtypeStruct(((N_DEV,) + a.shape) if kind == "gather" else a.shape, a.dtype)
            for kind, a in zip(kinds, arrays)]


def _comm(kinds, arrays, name):
    n = len(arrays)

    def body(*refs):
        plan = _comm_plan(kinds, refs[:n], refs[n:2 * n], *refs[2 * n:])
        _comm_start(plan)
        _comm_wait(plan)

    return pl.pallas_call(
        body, name=name, out_shape=_comm_out_shapes(kinds, arrays), in_specs=[ANY] * n, out_specs=[ANY] * n,
        scratch_shapes=_comm_scratch(n),
    )(*arrays)


def _hosted_comm(kinds, n_in, n_out, n_comm, n_steps):
    def plan_of(refs, receives):
        ins = refs[n_in:n_in + n_comm]
        outs = refs[n_in + n_comm + n_out:n_in + 2 * n_comm + n_out]
        return _comm_plan(kinds, ins, outs, *refs[-3:], receives=receives)

    def start(refs):
        @pl.when(pl.program_id(0) == 0)
        def _():
            _comm_start(plan_of(refs, False))

    def wait(refs):
        @pl.when(pl.program_id(0) == n_steps - 1)
        def _():
            _comm_wait(plan_of(refs, True))

    return start, wait


def _inproj_fwd(x, g1, w_in_p):
    T = x.shape[0]
    tm = min(T, 512)

    def body(x_ref, g_ref, w_ref, proj_ref, xn_ref):
        xv = x_ref[...]
        r = lax.rsqrt(_rowmean(xv * xv) + EPS)
        xn = (xv * r * g_ref[...]).astype(BF16)
        xn_ref[...] = xn
        proj_ref[...] = _nn(xn, w_ref[...]).astype(BF16)

    return pl.pallas_call(
        body, name="inproj_fwd", grid=(T // tm,),
        out_shape=[jax.ShapeDtypeStruct((T, D_INP), BF16), jax.ShapeDtypeStruct((T, D_MODEL), BF16)],
        in_specs=[pl.BlockSpec((tm, D_MODEL), lambda i: (i, 0)), _const((1, D_MODEL)), _const((D_MODEL, D_INP))],
        out_specs=[pl.BlockSpec((tm, D_INP), lambda i: (i, 0)), pl.BlockSpec((tm, D_MODEL), lambda i: (i, 0))],
        compiler_params=_params(("arbitrary",)),
    )(x, g1, w_in_p)


def _head_masks():
    lane = lax.broadcasted_iota(jnp.int32, (1, KEY), 1)
    return [((lane >= h * DK) & (lane < (h + 1) * DK)).astype(F32) for h in range(HEADS)]


class _Mats(NamedTuple):
    tri: jax.Array
    tri_t: jax.Array
    same: jax.Array
    mid: jax.Array
    causal: jax.Array
    causal_t: jax.Array
    heads: jax.Array


def _chunk_matrices():
    r = lax.broadcasted_iota(jnp.int32, (SUB, SUB), 0)
    c = lax.broadcasted_iota(jnp.int32, (SUB, SUB), 1)
    shift = CHUNK.bit_length() - 1
    same = jnp.right_shift(r, shift) == jnp.right_shift(c, shift)
    causal = same & (r >= c)
    causal_t = same & (r <= c)
    mid = same & ((c & (CHUNK - 1)) < CHUNK // 2)
    hr = jnp.right_shift(lax.broadcasted_iota(jnp.int32, (VAL, KEY), 0), DV.bit_length() - 1)
    hc = jnp.right_shift(lax.broadcasted_iota(jnp.int32, (VAL, KEY), 1), DK.bit_length() - 1)
    return _Mats(tri=causal.astype(BF16), tri_t=causal_t.astype(BF16), same=same.astype(BF16), mid=mid.astype(BF16),
                 causal=causal, causal_t=causal_t, heads=hr == hc)


class _Decay(NamedTuple):
    al: jax.Array
    q: jax.Array
    k: jax.Array
    eb: jax.Array
    ebm: jax.Array
    emb: jax.Array
    elb: jax.Array
    ebl: jax.Array


def _decay_terms(z, q, k, wg, bg, mats):
    al = _nn(z, wg) + bg
    la = (jnp.minimum(al, 0.0) - jnp.log(1.0 + jnp.exp(-jnp.abs(al)))) * (1.0 / GATE_TAU)
    hi, lo = _split_bf16(la)
    cum = lambda m: _nn(m, hi) + _nn(m, lo)
    b, b_last, b_mid = cum(mats.tri), cum(mats.same), cum(mats.mid)
    return _Decay(al=al, q=q.astype(F32) * Q_SCALE, k=k.astype(F32), eb=jnp.exp(b), ebm=jnp.exp(b - b_mid),
                  emb=jnp.exp(b_mid - b), elb=jnp.exp(b_last - b), ebl=jnp.exp(b_last))


def _gla_fwd(proj, wg, bg, gn, shards):
    T = proj.shape[0]
    tb = min(T, 512)
    cpb = tb // CHUNK
    n_comm = len(shards)
    kinds = ["gather"] * n_comm
    comm_start, comm_wait = _hosted_comm(kinds, 8, 3, n_comm, T // tb)

    def body(*refs):
        q_ref, k_ref, v_ref, g_ref, z_ref, wg_ref, bg_ref, gn_ref = refs[:8]
        mix_ref, o_ref, st_ref = refs[8 + n_comm:11 + n_comm]
        state = refs[11 + 2 * n_comm]
        comm_start(refs)

        @pl.when(pl.program_id(0) == 0)
        def _():
            state[...] = jnp.zeros_like(state)

        mats = _chunk_matrices()
        masks = _head_masks()
        wgv, bgv = wg_ref[...], bg_ref[...]

        for sb in range(tb // SUB):
            rows = slice(sb * SUB, (sb + 1) * SUB)
            d = _decay_terms(z_ref[rows, :], q_ref[rows, :], k_ref[rows, :], wgv, bgv, mats)
            kem_b = (d.k * d.emb).astype(BF16)
            qem = d.q * d.ebm
            for h in range(HEADS):
                cols = slice(h * DV, (h + 1) * DV)
                a = jnp.where(mats.causal, _nt((qem * masks[h]).astype(BF16), kem_b), 0.0)
                o_ref[rows, cols] = _nn(a.astype(BF16), v_ref[rows, cols])
            qe0_b = (d.q * d.eb).astype(BF16)
            kdec_b = (d.k * d.elb).astype(BF16)
            for c in range(SUB // CHUNK):
                loc = slice(c * CHUNK, (c + 1) * CHUNK)
                glob = slice(sb * SUB + c * CHUNK, sb * SUB + (c + 1) * CHUNK)
                st = state[...]
                st_b = st.astype(BF16)
                st_ref[sb * (SUB // CHUNK) + c] = st_b
                o_ref[glob, :] += _nt(qe0_b[loc], st_b)
                u = _tn(v_ref[glob, :], kdec_b[loc])
                state[...] = st * d.ebl[c * CHUNK:c * CHUNK + 1] + jnp.where(mats.heads, u, 0.0)

        gnv = gn_ref[...]
        for h in range(HEADS):
            cols = slice(h * DV, (h + 1) * DV)
            oh = o_ref[:, cols]
            r = lax.rsqrt(_rowmean(oh * oh) + EPS)
            gh = g_ref[:, cols].astype(F32)
            mix_ref[:, cols] = (oh * r * gnv * (gh * jax.nn.sigmoid(gh))).astype(BF16)
        comm_wait(refs)

    nc = T // CHUNK
    res = pl.pallas_call(
        body, name="gla_fwd", grid=(T // tb,),
        out_shape=[jax.ShapeDtypeStruct((T, VAL), BF16), jax.ShapeDtypeStruct((T, VAL), F32),
                   jax.ShapeDtypeStruct((nc, VAL, KEY), BF16)] + _comm_out_shapes(kinds, shards),
        in_specs=[pl.BlockSpec((tb, KEY), lambda i: (i, P_Q // KEY)), pl.BlockSpec((tb, KEY), lambda i: (i, P_K // KEY)),
                  pl.BlockSpec((tb, VAL), lambda i: (i, P_V // VAL)), pl.BlockSpec((tb, VAL), lambda i: (i, P_G // VAL)),
                  pl.BlockSpec((tb, Z_PAD), lambda i: (i, P_Z // Z_PAD)),
                  _const((Z_PAD, KEY)), _const((1, KEY)), _const((1, DV))] + [ANY] * n_comm,
        out_specs=[pl.BlockSpec((tb, VAL), lambda i: (i, 0)), pl.BlockSpec((tb, VAL), lambda i: (i, 0)),
                   pl.BlockSpec((cpb, VAL, KEY), lambda i: (i, 0, 0))] + [ANY] * n_comm,
        scratch_shapes=[pltpu.VMEM((VAL, KEY), F32)] + _comm_scratch(n_comm),
        compiler_params=_params(("arbitrary",)),
    )(proj, proj, proj, proj, proj, wg, bg, gn, *shards)
    return res[0], res[1], res[2], res[3:]


def _group_mean(v, gmat):
    return _nn(v.astype(BF16), gmat)


def _shifted_copies(buf, sh, rows):
    for k in range(1, SUBLANES):
        sh[k - 1] = buf[pl.ds(k, rows), :]


def _tap(buf, sh, off, r0):
    k, base = off % SUBLANES, off - off % SUBLANES
    rows = pl.ds(pl.multiple_of(r0 + base, SUBLANES), STRIP)
    return buf[rows, :] if k == 0 else sh[k - 1, rows, :]


def _conv_fwd(proj, conv_w, conv_b, cn_g, cn_b, gmat, shards):
    T = proj.shape[0]
    tm = min(T, 512)
    sh_rows = tm + HALO - SUBLANES
    n_comm = len(shards)
    kinds = ["gather"] * n_comm
    comm_start, comm_wait = _hosted_comm(kinds, 7, 2, n_comm, T // tm)

    def body(*refs):
        ci_ref, cg_ref, w_ref, b_ref, g_ref, be_ref, gm_ref = refs[:7]
        mix_ref, uc_ref = refs[7 + n_comm:9 + n_comm]
        ubuf, ush = refs[9 + 2 * n_comm:11 + 2 * n_comm]
        comm_start(refs)

        @pl.when(pl.program_id(0) == 0)
        def _():
            ubuf[0:HALO, :] = jnp.zeros((HALO, CONV), F32)

        ubuf[HALO:, :] = ci_ref[...].astype(F32) * jax.nn.sigmoid(cg_ref[...].astype(F32))
        _shifted_copies(ubuf, ush, sh_rows)

        def strip(s, carry):
            r0 = pl.multiple_of(s * STRIP, STRIP)
            acc = jnp.zeros((STRIP, CONV), F32) + b_ref[...]
            for j in range(CONV_W):
                acc = acc + w_ref[j:j + 1, :] * _tap(ubuf, ush, HALO - (CONV_W - 1) + j, r0)
            uc_ref[pl.ds(r0, STRIP), :] = acc
            return carry

        lax.fori_loop(0, tm // STRIP, strip, 0)
        ubuf[0:HALO, :] = ubuf[tm:tm + HALO, :]
        gm = gm_ref[...]
        ucv = uc_ref[...]
        d = ucv - _group_mean(ucv, gm)
        var = _group_mean(d * d, gm)
        yn = d * lax.rsqrt(var + EPS) * g_ref[...] + be_ref[...]
        mix_ref[...] = (yn * jax.nn.sigmoid(yn)).astype(BF16)
        comm_wait(refs)

    res = pl.pallas_call(
        body, name="conv_fwd", grid=(T // tm,),
        out_shape=[jax.ShapeDtypeStruct((T, CONV), BF16), jax.ShapeDtypeStruct((T, CONV), F32)]
        + _comm_out_shapes(kinds, shards),
        in_specs=[pl.BlockSpec((tm, CONV), lambda i: (i, P_CI // CONV)), pl.BlockSpec((tm, CONV), lambda i: (i, P_CG // CONV)),
                  _const((HALO, CONV)), _const((1, CONV)), _const((1, CONV)), _const((1, CONV)), _const((CONV, CONV))]
        + [ANY] * n_comm,
        out_specs=[pl.BlockSpec((tm, CONV), lambda i: (i, 0)), pl.BlockSpec((tm, CONV), lambda i: (i, 0))]
        + [ANY] * n_comm,
        scratch_shapes=[pltpu.VMEM((tm + HALO, CONV), F32), pltpu.VMEM((SUBLANES - 1, sh_rows, CONV), F32)]
        + _comm_scratch(n_comm),
        compiler_params=_params(("arbitrary",)),
    )(proj, proj, conv_w, conv_b, cn_g, cn_b, gmat, *shards)
    return res[0], res[1], res[2:]


def _rms_bwd(dy, xhat, r, g):
    dyg = dy * g
    return r * (dyg - xhat * _rowmean(dyg * xhat))


def _mlp_fwd_bwd(x, mix_a, mix_c, tgt, w_out, g2, w1t, w2, gf):
    T = x.shape[0]
    tm = min(T, 256)
    inv_d = 1.0 / D_MODEL

    def body(x_ref, ma_ref, mc_ref, t_ref, wo_ref, g2_ref, w1_ref, w2_ref, gf_ref,
             dh1_ref, dmix_ref, hn_ref, ff_ref, da_ref, dh2_ref, loss_ref, dgf_ref, dg2_ref):
        @pl.when(pl.program_id(0) == 0)
        def _():
            loss_ref[...] = jnp.zeros_like(loss_ref)
            dgf_ref[...] = jnp.zeros_like(dgf_ref)
            dg2_ref[...] = jnp.zeros_like(dg2_ref)

        g2v, gfv = g2_ref[...], gf_ref[...]
        h1 = x_ref[...] + _nn(ma_ref[...], wo_ref[0:VAL, :]) + _nn(mc_ref[...], wo_ref[VAL:, :])
        r2 = lax.rsqrt(_rowmean(h1 * h1) + EPS)
        h1hat = h1 * r2
        hn = (h1hat * g2v).astype(BF16)
        hn_ref[...] = hn
        relu_a = jnp.maximum(_nt(hn, w1_ref[...]), 0.0)
        ff = (relu_a * relu_a).astype(BF16)
        ff_ref[...] = ff
        h2 = h1 + _nn(ff, w2_ref[...])
        rf = lax.rsqrt(_rowmean(h2 * h2) + EPS)
        h2hat = h2 * rf
        err = h2hat * gfv - t_ref[...]
        loss_ref[...] += (0.5 * inv_d) * _colsum(jnp.sum(err * err, axis=1, keepdims=True))
        dy = err * inv_d
        dgf_ref[...] += _colsum(dy * h2hat)
        dh2 = _rms_bwd(dy, h2hat, rf, gfv)
        dh2_b = dh2.astype(BF16)
        dh2_ref[...] = dh2_b
        da = (_nt(dh2_b, w2_ref[...]) * (2.0 * relu_a)).astype(BF16)
        da_ref[...] = da
        dhn = _nn(da, w1_ref[...])
        dg2_ref[...] += _colsum(dhn * h1hat)
        dh1 = dh2 + _rms_bwd(dhn, h1hat, r2, g2v)
        dh1_ref[...] = dh1
        dmix_ref[...] = _nt(dh1.astype(BF16), wo_ref[...]).astype(BF16)

    tok = lambda w: pl.BlockSpec((tm, w), lambda i: (i, 0))
    return pl.pallas_call(
        body, name="mlp_fwd_bwd", grid=(T // tm,),
        out_shape=[jax.ShapeDtypeStruct((T, D_MODEL), F32), jax.ShapeDtypeStruct((T, D_MODEL), BF16),
                   jax.ShapeDtypeStruct((T, D_MODEL), BF16), jax.ShapeDtypeStruct((T, D_FF), BF16),
                   jax.ShapeDtypeStruct((T, D_FF), BF16), jax.ShapeDtypeStruct((T, D_MODEL), BF16),
                   jax.ShapeDtypeStruct((1, 1), F32), jax.ShapeDtypeStruct((1, D_MODEL), F32),
                   jax.ShapeDtypeStruct((1, D_MODEL), F32)],
        in_specs=[tok(D_MODEL), tok(VAL), tok(CONV), tok(D_MODEL), _const((D_MODEL, D_MODEL)), _const((1, D_MODEL)),
                  _const((D_FF, D_MODEL)), _const((D_FF, D_MODEL)), _const((1, D_MODEL))],
        out_specs=[tok(D_MODEL), tok(D_MODEL), tok(D_MODEL), tok(D_FF), tok(D_FF), tok(D_MODEL),
                   pl.BlockSpec((1, 1), lambda i: (0, 0)), pl.BlockSpec((1, D_MODEL), lambda i: (0, 0)),
                   pl.BlockSpec((1, D_MODEL), lambda i: (0, 0))],
        compiler_params=_params(("arbitrary",)),
    )(x, mix_a, mix_c, tgt, w_out, g2, w1t, w2, gf)


def _silu_grad(v, s):
    return s * (1.0 + v * (1.0 - s))


def _conv_bwd(proj, uc, dmix, conv_w, cn_g, cn_b, gmat):
    T = proj.shape[0]
    tm = min(T, 512)
    nt = T // tm
    sh_rows = tm + HALO - SUBLANES
    n_strips = tm // STRIP

    def body(ci_ref, cg_ref, uc_ref, dm_ref, w_ref, g_ref, be_ref, gm_ref,
             dci_ref, dcg_ref, dw_ref, db_ref, dg_ref, dbe_ref, dbuf, dsh, dwacc):
        step = pl.program_id(0)

        @pl.when(step == 0)
        def _():
            dbuf[tm:, :] = jnp.zeros((HALO, CONV), F32)
            dwacc[...] = jnp.zeros_like(dwacc)
            db_ref[...] = jnp.zeros_like(db_ref)
            dg_ref[...] = jnp.zeros_like(dg_ref)
            dbe_ref[...] = jnp.zeros_like(dbe_ref)

        gm, gv = gm_ref[...], g_ref[...]
        ucv = uc_ref[...]
        d = ucv - _group_mean(ucv, gm)
        rs = lax.rsqrt(_group_mean(d * d, gm) + EPS)
        yhat = d * rs
        yn = yhat * gv + be_ref[...]
        dyn = dm_ref[...].astype(F32) * _silu_grad(yn, jax.nn.sigmoid(yn))
        dg_ref[...] += _colsum(dyn * yhat)
        dbe_ref[...] += _colsum(dyn)
        dyh = dyn * gv
        duc = rs * (dyh - _group_mean(dyh, gm) - yhat * _group_mean(dyh * yhat, gm))
        db_ref[...] += _colsum(duc)
        dbuf[0:tm, :] = duc
        _shifted_copies(dbuf, dsh, sh_rows)

        def strip(s, carry):
            r0 = pl.multiple_of(s * STRIP, STRIP)
            rows = pl.ds(r0, STRIP)
            cin = ci_ref[rows, :].astype(F32)
            sg = jax.nn.sigmoid(cg_ref[rows, :].astype(F32))
            u = cin * sg
            du = jnp.zeros((STRIP, CONV), F32)
            for j in range(CONV_W):
                dj = _tap(dbuf, dsh, CONV_W - 1 - j, r0)
                du = du + w_ref[j:j + 1, :] * dj
                p = u * dj
                fold = p[0:SUBLANES]
                for q in range(1, STRIP // SUBLANES):
                    fold = fold + p[q * SUBLANES:(q + 1) * SUBLANES, :]
                dwacc[j * SUBLANES:(j + 1) * SUBLANES, :] += fold
            dci_ref[rows, :] = (du * sg).astype(BF16)
            dcg_ref[rows, :] = (du * cin * sg * (1.0 - sg)).astype(BF16)
            return carry

        lax.fori_loop(0, n_strips, strip, 0)
        dbuf[tm:, :] = dbuf[0:HALO, :]

        @pl.when(step == nt - 1)
        def _():
            dw_ref[...] = jnp.zeros_like(dw_ref)
            for j in range(CONV_W):
                dw_ref[j:j + 1, :] = _colsum(dwacc[j * SUBLANES:(j + 1) * SUBLANES, :])

    rev = lambda i: nt - 1 - i
    tile = lambda col: pl.BlockSpec((tm, CONV), lambda i: (rev(i), col))
    acc = lambda rows: pl.BlockSpec((rows, CONV), lambda i: (0, 0))
    return pl.pallas_call(
        body, name="conv_bwd", grid=(nt,),
        out_shape=[jax.ShapeDtypeStruct((T, CONV), BF16), jax.ShapeDtypeStruct((T, CONV), BF16),
                   jax.ShapeDtypeStruct((HALO, CONV), F32), jax.ShapeDtypeStruct((1, CONV), F32),
                   jax.ShapeDtypeStruct((1, CONV), F32), jax.ShapeDtypeStruct((1, CONV), F32)],
        in_specs=[tile(P_CI // CONV), tile(P_CG // CONV), tile(0), tile(1),
                  _const((HALO, CONV)), _const((1, CONV)), _const((1, CONV)), _const((CONV, CONV))],
        out_specs=[tile(0), tile(0), acc(HALO), acc(1), acc(1), acc(1)],
        scratch_shapes=[pltpu.VMEM((tm + HALO, CONV), F32), pltpu.VMEM((SUBLANES - 1, sh_rows, CONV), F32),
                        pltpu.VMEM((HALO * SUBLANES, CONV), F32)],
        compiler_params=_params(("arbitrary",)),
    )(proj, proj, uc, dmix, conv_w, cn_g, cn_b, gmat)


def _gla_bwd(proj, o, states, dmix, wg, bg, gn, parts):
    T = proj.shape[0]
    tb = min(T, 512)
    cpb = tb // CHUNK
    nb = T // tb
    n_comm = len(parts)
    kinds = ["exchange"] * n_comm
    comm_start, comm_wait = _hosted_comm(kinds, 11, 8, n_comm, nb)

    def body(*refs):
        q_ref, k_ref, v_ref, g_ref, z_ref, o_ref, st_ref, dm_ref, wg_ref, bg_ref, gn_ref = refs[:11]
        dq_ref, dk_ref, dv_ref, dg_ref, dz_ref, dwg_ref, dbg_ref, dgn_ref = refs[11 + n_comm:19 + n_comm]
        dstate, do_scr, dv_scr = refs[19 + 2 * n_comm:22 + 2 * n_comm]
        comm_start(refs)

        @pl.when(pl.program_id(0) == 0)
        def _():
            dstate[...] = jnp.zeros_like(dstate)
            dwg_ref[...] = jnp.zeros_like(dwg_ref)
            dbg_ref[...] = jnp.zeros_like(dbg_ref)
            dgn_ref[...] = jnp.zeros_like(dgn_ref)

        gnv = gn_ref[...]
        dgn = jnp.zeros((1, DV), F32)
        for h in range(HEADS):
            cols = slice(h * DV, (h + 1) * DV)
            oh = o_ref[:, cols]
            r = lax.rsqrt(_rowmean(oh * oh) + EPS)
            ohat = oh * r
            gh = g_ref[:, cols].astype(F32)
            sg = jax.nn.sigmoid(gh)
            dmx = dm_ref[:, cols].astype(F32)
            don = dmx * (gh * sg)
            dg_ref[:, cols] = (dmx * (ohat * gnv) * _silu_grad(gh, sg)).astype(BF16)
            dgn = dgn + _colsum(don * ohat)
            do_scr[:, cols] = _rms_bwd(don, ohat, r, gnv)
        dgn_ref[...] += dgn

        mats = _chunk_matrices()
        masks = _head_masks()
        wgv, bgv = wg_ref[...], bg_ref[...]
        n_chunks = SUB // CHUNK

        for sb in reversed(range(tb // SUB)):
            rows = slice(sb * SUB, (sb + 1) * SUB)
            zs = z_ref[rows, :]
            d = _decay_terms(zs, q_ref[rows, :], k_ref[rows, :], wgv, bgv, mats)
            qem = d.q * d.ebm
            qem_b = qem.astype(BF16)
            kem_b = (d.k * d.emb).astype(BF16)
            dq = jnp.zeros((SUB, KEY), F32)
            dk = jnp.zeros((SUB, KEY), F32)
            for h in range(HEADS):
                hm = masks[h]
                cols = slice(h * DV, (h + 1) * DV)
                do_b = do_scr[rows, cols].astype(BF16)
                vh = v_ref[rows, cols]
                da = jnp.where(mats.causal, _nt(do_b, vh), 0.0).astype(BF16)
                da_t = jnp.where(mats.causal_t, _nt(vh, do_b), 0.0).astype(BF16)
                a_t = jnp.where(mats.causal_t, _nt(kem_b, (qem * hm).astype(BF16)), 0.0).astype(BF16)
                dq = dq + hm * _nn(da, kem_b)
                dk = dk + hm * _nn(da_t, qem_b)
                dv_scr[rows, cols] = _nn(a_t, do_b)
            dq = dq * d.ebm
            dk = dk * d.emb

            qe0_b = (d.q * d.eb).astype(BF16)
            kdec_b = (d.k * d.elb).astype(BF16)
            dq_st, dk_st, last = [None] * n_chunks, [None] * n_chunks, [None] * n_chunks
            for c in reversed(range(n_chunks)):
                loc = slice(c * CHUNK, (c + 1) * CHUNK)
                glob = slice(sb * SUB + c * CHUNK, sb * SUB + (c + 1) * CHUNK)
                st_b = st_ref[sb * n_chunks + c]
                ds = dstate[...]
                ds_b = ds.astype(BF16)
                do_c = do_scr[glob, :].astype(BF16)
                ebl_c = d.ebl[c * CHUNK:c * CHUNK + 1]
                dk_c = _nn(v_ref[glob, :], ds_b) * d.elb[loc]
                dq_st[c] = _nn(do_c, st_b) * d.eb[loc]
                dk_st[c] = dk_c
                last_c = _colsum(d.k[loc] * dk_c) + ebl_c * _colsum(st_b.astype(F32) * ds)
                last[c] = jnp.broadcast_to(last_c, (CHUNK, KEY))
                dv_ref[glob, :] = (dv_scr[glob, :] + _nt(kdec_b[loc], ds_b)).astype(BF16)
                dstate[...] = ds * ebl_c + jnp.where(mats.heads, _tn(do_c, qe0_b[loc]), 0.0)
            dq = dq + jnp.concatenate(dq_st, axis=0)
            dk = dk + jnp.concatenate(dk_st, axis=0)
            dq_ref[rows, :] = (dq * Q_SCALE).astype(BF16)
            dk_ref[rows, :] = dk.astype(BF16)
            hi, lo = _split_bf16(d.q * dq - d.k * dk)
            dla = _nn(mats.tri_t, hi) + _nn(mats.tri_t, lo) + jnp.concatenate(last, axis=0)
            dal = dla * (1.0 / GATE_TAU) * jax.nn.sigmoid(-d.al)
            dal_b = dal.astype(BF16)
            dz_ref[rows, :] = _nt(dal_b, wgv).astype(BF16)
            dwg_ref[...] += _tn(zs, dal_b)
            dbg_ref[...] += _colsum(dal)
        comm_wait(refs)

    rev = lambda i: nb - 1 - i
    blk = lambda w, col: pl.BlockSpec((tb, w), lambda i: (rev(i), col))
    res = pl.pallas_call(
        body, name="gla_bwd", grid=(nb,),
        out_shape=[jax.ShapeDtypeStruct((T, KEY), BF16), jax.ShapeDtypeStruct((T, KEY), BF16),
                   jax.ShapeDtypeStruct((T, VAL), BF16), jax.ShapeDtypeStruct((T, VAL), BF16),
                   jax.ShapeDtypeStruct((T, Z_PAD), BF16), jax.ShapeDtypeStruct((Z_PAD, KEY), F32),
                   jax.ShapeDtypeStruct((1, KEY), F32), jax.ShapeDtypeStruct((1, DV), F32)]
        + _comm_out_shapes(kinds, parts),
        in_specs=[blk(KEY, P_Q // KEY), blk(KEY, P_K // KEY), blk(VAL, P_V // VAL), blk(VAL, P_G // VAL),
                  blk(Z_PAD, P_Z // Z_PAD), blk(VAL, 0),
                  pl.BlockSpec((cpb, VAL, KEY), lambda i: (rev(i), 0, 0)), blk(VAL, 0),
                  _const((Z_PAD, KEY)), _const((1, KEY)), _const((1, DV))] + [ANY] * n_comm,
        out_specs=[blk(KEY, 0), blk(KEY, 0), blk(VAL, 0), blk(VAL, 0), blk(Z_PAD, 0),
                   pl.BlockSpec((Z_PAD, KEY), lambda i: (0, 0)), pl.BlockSpec((1, KEY), lambda i: (0, 0)),
                   pl.BlockSpec((1, DV), lambda i: (0, 0))] + [ANY] * n_comm,
        scratch_shapes=[pltpu.VMEM((VAL, KEY), F32), pltpu.VMEM((tb, VAL), F32), pltpu.VMEM((tb, VAL), F32)]
        + _comm_scratch(n_comm),
        compiler_params=_params(("arbitrary",)),
    )(proj, proj, proj, proj, proj, o, states, dmix, wg, bg, gn, *parts)
    return res[:8], res[8:]


def _inproj_bwd(x, g1, w_in_p, dh1, dq, dk, dv, dg, dci, dcg, dz, parts):
    T = x.shape[0]
    tm = min(T, 512)
    n_comm = len(parts)
    kinds = ["exchange"] * n_comm
    comm_start, comm_wait = _hosted_comm(kinds, 11, 2, n_comm, T // tm)

    def body(*refs):
        x_ref, g_ref, w_ref, dh1_ref, dq_ref, dk_ref, dv_ref, dg_ref, dci_ref, dcg_ref, dz_ref = refs[:11]
        dx_ref, dg1_ref = refs[11 + n_comm:13 + n_comm]
        dp_ref = refs[13 + 2 * n_comm]
        comm_start(refs)

        @pl.when(pl.program_id(0) == 0)
        def _():
            dg1_ref[...] = jnp.zeros_like(dg1_ref)

        dp_ref[:, P_Q:P_K] = dq_ref[...]
        dp_ref[:, P_K:P_V] = dk_ref[...]
        dp_ref[:, P_V:P_G] = dv_ref[...]
        dp_ref[:, P_G:P_CI] = dg_ref[...]
        dp_ref[:, P_CI:P_CG] = dci_ref[...]
        dp_ref[:, P_CG:P_Z] = dcg_ref[...]
        dp_ref[:, P_Z:] = dz_ref[...]
        dxn = _nt(dp_ref[...], w_ref[...])
        xv = x_ref[...]
        r = lax.rsqrt(_rowmean(xv * xv) + EPS)
        xhat = xv * r
        dg1_ref[...] += _colsum(dxn * xhat)
        dx_ref[...] = dh1_ref[...] + _rms_bwd(dxn, xhat, r, g_ref[...])
        comm_wait(refs)

    tok = lambda w: pl.BlockSpec((tm, w), lambda i: (i, 0))
    res = pl.pallas_call(
        body, name="inproj_bwd", grid=(T // tm,),
        out_shape=[jax.ShapeDtypeStruct((T, D_MODEL), F32), jax.ShapeDtypeStruct((1, D_MODEL), F32)]
        + _comm_out_shapes(kinds, parts),
        in_specs=[tok(D_MODEL), _const((1, D_MODEL)), _const((D_MODEL, D_INP)), tok(D_MODEL), tok(KEY), tok(KEY),
                  tok(VAL), tok(VAL), tok(CONV), tok(CONV), tok(Z_PAD)] + [ANY] * n_comm,
        out_specs=[tok(D_MODEL), pl.BlockSpec((1, D_MODEL), lambda i: (0, 0))] + [ANY] * n_comm,
        scratch_shapes=[pltpu.VMEM((tm, D_INP), BF16)] + _comm_scratch(n_comm),
        compiler_params=_params(("arbitrary",)),
    )(x, g1, w_in_p, dh1, dq, dk, dv, dg, dci, dcg, dz, *parts)
    return res[0], res[1], res[2:]


def _wgrad_in(xn, pieces):
    T = xn.shape[0]
    tt = min(T, 1024)
    nt = T // tt
    widths = [p.shape[1] for p in pieces]
    assert sum(widths) == D_INP

    def body(*refs):
        xn_ref, piece_refs, o_ref, acc = refs[0], refs[1:1 + len(pieces)], refs[-2], refs[-1]

        @pl.when(pl.program_id(0) == 0)
        def _():
            acc[...] = jnp.zeros_like(acc)

        xv = xn_ref[...]
        col = 0
        for ref, w in zip(piece_refs, widths):
            acc[:, col:col + w] += _tn(xv, ref[...])
            col += w

        @pl.when(pl.program_id(0) == nt - 1)
        def _():
            o_ref[...] = acc[...].astype(BF16)

    tok = lambda w: pl.BlockSpec((tt, w), lambda t: (t, 0))
    return pl.pallas_call(
        body, name="wgrad_in", grid=(nt,), out_shape=jax.ShapeDtypeStruct((D_MODEL, D_INP), BF16),
        in_specs=[tok(D_MODEL)] + [tok(w) for w in widths],
        out_specs=pl.BlockSpec((D_MODEL, D_INP), lambda t: (0, 0), pipeline_mode=pl.Buffered(1)),
        scratch_shapes=[pltpu.VMEM((D_MODEL, D_INP), F32)],
        compiler_params=_params(("arbitrary",)),
    )(xn, *pieces)


def _wgrad(a, b, name, tk, tn, col_block=None):
    T, K = a.shape
    N = b.shape[1]
    tt = min(T, 2048)
    nt = T // tt

    def body(a_ref, b_ref, o_ref, acc):
        @pl.when(pl.program_id(2) == 0)
        def _():
            acc[...] = jnp.zeros_like(acc)

        acc[...] += _tn(a_ref[...], b_ref[...].astype(BF16))

        @pl.when(pl.program_id(2) == nt - 1)
        def _():
            if col_block is None:
                o_ref[...] = acc[...].astype(BF16)
            else:
                for q in range(tn // col_block):
                    o_ref[q] = acc[:, q * col_block:(q + 1) * col_block].astype(BF16)

    if col_block is None:
        out_shape = jax.ShapeDtypeStruct((K, N), BF16)
        out_spec = pl.BlockSpec((tk, tn), lambda i, j, t: (i, j))
    else:
        assert tk == K
        out_shape = jax.ShapeDtypeStruct((N // col_block, K, col_block), BF16)
        out_spec = pl.BlockSpec((tn // col_block, tk, col_block), lambda i, j, t: (j, 0, 0))
    return pl.pallas_call(
        body, name=name, grid=(K // tk, N // tn, nt), out_shape=out_shape,
        in_specs=[pl.BlockSpec((tt, tk), lambda i, j, t: (t, i)), pl.BlockSpec((tt, tn), lambda i, j, t: (t, j))],
        out_specs=out_spec, scratch_shapes=[pltpu.VMEM((tk, tn), F32)],
        compiler_params=_params(("arbitrary", "arbitrary", "arbitrary")),
    )(a, b)


def _adam_math(w, g, m, v):
    m = ADAM_B1 * m + (1.0 - ADAM_B1) * g
    v = ADAM_B2 * v + (1.0 - ADAM_B2) * (g * g)
    m_hat = m / (1.0 - ADAM_B1 ** ADAM_STEP)
    v_hat = v / (1.0 - ADAM_B2 ** ADAM_STEP)
    delta = -ADAM_LR * (m_hat / (jnp.sqrt(v_hat) + ADAM_EPS) + ADAM_WD * w)
    return delta, m, v


def _sum8(ref):
    g = ref[0].astype(F32)
    for s in range(1, N_DEV):
        g = g + ref[s].astype(F32)
    return g


def _adam_big(parts, w, m, v, name):
    R, C = w.shape
    tr = min(R, 128)

    def body(p_ref, w_ref, m_ref, v_ref, g_ref, d_ref, nm_ref, nv_ref):
        g = _sum8(p_ref)
        g_ref[...] = g
        d_ref[...], nm_ref[...], nv_ref[...] = _adam_math(w_ref[...], g, m_ref[...], v_ref[...])

    row = pl.BlockSpec((tr, C), lambda i: (i, 0))
    return pl.pallas_call(
        body, name=name, grid=(R // tr,), out_shape=[jax.ShapeDtypeStruct((R, C), F32)] * 4,
        in_specs=[pl.BlockSpec((N_DEV, tr, C), lambda i: (0, i, 0)), row, row, row], out_specs=[row] * 4,
        compiler_params=_params(("arbitrary",)),
    )(parts, w, m, v)


def _sum_small(parts):
    def body(p_ref, o_ref):
        o_ref[...] = _sum8(p_ref)

    return pl.pallas_call(body, name="sum_small", out_shape=jax.ShapeDtypeStruct(parts.shape[1:], F32))(parts)


def _adam_small(gs, ws, ms, vs):
    n = len(gs)

    def body(*refs):
        g_refs, w_refs, m_refs, v_refs = refs[:n], refs[n:2 * n], refs[2 * n:3 * n], refs[3 * n:4 * n]
        outs = refs[4 * n:]
        for i in range(n):
            d, nm, nv = _adam_math(w_refs[i][...], g_refs[i][...], m_refs[i][...], v_refs[i][...])
            outs[i][...] = d
            outs[n + i][...] = nm
            outs[2 * n + i][...] = nv

    shapes = [jax.ShapeDtypeStruct(w.shape, F32) for w in ws]
    res = pl.pallas_call(body, name="adam_small", out_shape=shapes * 3)(*gs, *ws, *ms, *vs)
    return res[:n], res[n:2 * n], res[2 * n:]


def _permute_in(w):
    pad = jnp.zeros(w.shape[:-1] + (D_INP - D_IN,), w.dtype)
    return jnp.concatenate([w[..., :1536], w[..., 1552:], w[..., 1536:1552], pad], axis=-1)


def _unpermute_in(w):
    return jnp.concatenate([w[..., :P_CI], w[..., P_Z:P_Z + RANK], w[..., P_CI:P_Z]], axis=-1)


def _group_matrix():
    gi = lax.broadcasted_iota(jnp.int32, (CONV, CONV), 0) // (CONV // GROUPS)
    gj = lax.broadcasted_iota(jnp.int32, (CONV, CONV), 1) // (CONV // GROUPS)
    return jnp.where(gi == gj, GROUPS / CONV, 0.0).astype(BF16)


_SMALL = [("loss", 8), ("dg1", 8), ("dbg", 2), ("dgn", 1), ("dconv_b", 4), ("dcn_g", 4), ("dcn_b", 4), ("dg2", 8),
          ("dgf", 8), ("dwg", 32), ("dconv_w", 124)]


def _pad8(rows):
    return -(-rows // 8) * 8


def kernel(x, norm1_g, w_in, w_gate_up, b_gate, gla_norm_g, conv_w, conv_b, conv_norm_g, conv_norm_b, w_out, norm2_g, w_mlp_in, w_mlp_out, final_norm_g, loss_target, m_norm1_g, m_w_in, m_w_gate_up, m_b_gate, m_gla_norm_g, m_conv_w, m_conv_b, m_conv_norm_g, m_conv_norm_b, m_w_out, m_norm2_g, m_w_mlp_in, m_w_mlp_out, m_final_norm_g, v_norm1_g, v_w_in, v_w_gate_up, v_b_gate, v_gla_norm_g, v_conv_w, v_conv_b, v_conv_norm_g, v_conv_norm_b, v_w_out, v_norm2_g, v_w_mlp_in, v_w_mlp_out, v_final_norm_g):
    x_idx = lax.axis_index("x")
    y_idx = lax.axis_index("y")
    c_idx = lax.axis_index("c")
    me = 4 * x_idx + 2 * y_idx + c_idx
    pad_in = lambda a: jnp.pad(a, ((0, 0), (0, SHARD_IN_PAD - SHARD_IN)))
    xs, tgt = x[0], loss_target[0]
    gf = final_norm_g.reshape(1, D_MODEL)
    gmat = _group_matrix()

    small_shard = jnp.zeros((48, 128), F32)
    small_shard = small_shard.at[0:RANK, 0:KEY // N_DEV].set(w_gate_up[0])
    small_shard = small_shard.at[RANK:RANK + CONV_W, 0:CONV // N_DEV].set(conv_w[0])
    g_in, g_small = _comm(["gather", "gather"], [pad_in(w_in[0]).astype(BF16), small_shard], "gather_w_in")
    w_in_p = _permute_in(jnp.concatenate([g_in[d, :, :SHARD_IN] for d in range(N_DEV)], axis=1))
    wg_full = jnp.concatenate([g_small[d, 0:RANK, 0:KEY // N_DEV] for d in range(N_DEV)], axis=1)
    wg_pad = jnp.pad(wg_full, ((0, Z_PAD - RANK), (0, 0))).astype(BF16)
    conv_w_full = jnp.concatenate([g_small[d, RANK:RANK + CONV_W, 0:CONV // N_DEV] for d in range(N_DEV)], axis=1)
    conv_w_pad = jnp.pad(conv_w_full, ((0, HALO - CONV_W), (0, 0)))

    proj, xn = _inproj_fwd(xs, norm1_g, w_in_p)
    mix_a, o, states, (g_out, g_w1) = _gla_fwd(
        proj, wg_pad, b_gate, gla_norm_g, [w_out[0].astype(BF16), w_mlp_in[0].T.astype(BF16)])
    mix_c, uc, (g_w2,) = _conv_fwd(proj, conv_w_pad, conv_b, conv_norm_g, conv_norm_b, gmat,
                                   [w_mlp_out[0].astype(BF16)])
    w_out_full = g_out.reshape(D_MODEL, D_MODEL)
    w1t_full = g_w1.reshape(D_FF, D_MODEL)
    w2_full = g_w2.reshape(D_FF, D_MODEL)
    dh1, dmix, hn, ff, da, dh2, loss, dgf, dg2 = _mlp_fwd_bwd(xs, mix_a, mix_c, tgt, w_out_full, norm2_g, w1t_full,
                                                              w2_full, gf)

    dw1 = _wgrad(hn, da, "wgrad_mlp_in", 1024, 1024, col_block=D_FF // N_DEV)
    dw2 = _wgrad(ff, dh2, "wgrad_mlp_out", 1024, 1024)
    dw_out = jnp.concatenate([_wgrad(mix_a, dh1, "wgrad_out_a", VAL, 1024),
                              _wgrad(mix_c, dh1, "wgrad_out_c", CONV, 1024)], axis=0)
    dci, dcg, dconv_w, dconv_b, dcn_g, dcn_b = _conv_bwd(proj, uc, dmix, conv_w_pad, conv_norm_g, conv_norm_b, gmat)
    (dq, dk, dv, dg, dz, dwg, dbg, dgn), (p_w1, p_w2, p_out) = _gla_bwd(
        proj, o, states, dmix, wg_pad, b_gate, gla_norm_g,
        [dw1, dw2.reshape(N_DEV, D_FF // N_DEV, D_MODEL), dw_out.reshape(N_DEV, D_MODEL // N_DEV, D_MODEL)])
    dw_in_p = _wgrad_in(xn, [dq, dk, dv, dg, dci, dcg, dz])
    dw_in = _unpermute_in(dw_in_p).reshape(D_MODEL, N_DEV, SHARD_IN).transpose(1, 0, 2)
    dw_in = jnp.pad(dw_in, ((0, 0), (0, 0), (0, SHARD_IN_PAD - SHARD_IN)))
    dx, dg1, (p_in,) = _inproj_bwd(xs, norm1_g, w_in_p, dh1, dq, dk, dv, dg, dci, dcg, dz, [dw_in])

    small = dict(loss=jnp.zeros((8, 128), F32) + loss, dg1=dg1, dbg=dbg, dgn=dgn, dconv_b=dconv_b, dcn_g=dcn_g,
                 dcn_b=dcn_b, dg2=dg2, dgf=dgf, dwg=dwg[0:RANK], dconv_w=dconv_w[0:CONV_W])
    pack = jnp.concatenate([jnp.pad(small[name].reshape(rows, 128), ((0, _pad8(rows) - rows), (0, 0)))
                            for name, rows in _SMALL], axis=0)
    (g_pack,) = _comm(["gather"], [pack], "gather_small_grads")

    gi, di, mi, vi = _adam_big(p_in, pad_in(w_in[0]), pad_in(m_w_in[0]), pad_in(v_w_in[0]), "adam_w_in")
    go, do, mo, vo = _adam_big(p_out, w_out[0], m_w_out[0], v_w_out[0], "adam_w_out")
    ga, da_, ma, va = _adam_big(p_w1, w_mlp_in[0], m_w_mlp_in[0], v_w_mlp_in[0], "adam_w_mlp_in")
    gb, db, mb, vb = _adam_big(p_w2, w_mlp_out[0], m_w_mlp_out[0], v_w_mlp_out[0], "adam_w_mlp_out")
    cut = lambda a: a[:, :SHARD_IN][None]

    summed = _sum_small(g_pack)
    small_g = {}
    at = 0
    for name, rows in _SMALL:
        small_g[name] = summed[at:at + rows]
        at += _pad8(rows)
    loss_out = small_g["loss"][0, 0]
    wg_cols = KEY // N_DEV
    cw_cols = CONV // N_DEV
    g_small_list = [
        small_g["dg1"].reshape(1, D_MODEL),
        lax.dynamic_slice_in_dim(small_g["dwg"].reshape(RANK, KEY), me * wg_cols, wg_cols, axis=1)[None],
        small_g["dbg"].reshape(1, KEY), small_g["dgn"].reshape(1, DV),
        lax.dynamic_slice_in_dim(small_g["dconv_w"].reshape(CONV_W, CONV), me * cw_cols, cw_cols, axis=1)[None],
        small_g["dconv_b"].reshape(1, CONV), small_g["dcn_g"].reshape(1, CONV), small_g["dcn_b"].reshape(1, CONV),
        small_g["dg2"].reshape(1, D_MODEL), small_g["dgf"].reshape(1, D_MODEL),
    ]
    row = lambda a: a.reshape(1, D_MODEL)
    w_small = [norm1_g, w_gate_up, b_gate, gla_norm_g, conv_w, conv_b, conv_norm_g, conv_norm_b, norm2_g,
               row(final_norm_g)]
    m_small = [m_norm1_g, m_w_gate_up, m_b_gate, m_gla_norm_g, m_conv_w, m_conv_b, m_conv_norm_g, m_conv_norm_b,
               m_norm2_g, row(m_final_norm_g)]
    v_small = [v_norm1_g, v_w_gate_up, v_b_gate, v_gla_norm_g, v_conv_w, v_conv_b, v_conv_norm_g, v_conv_norm_b,
               v_norm2_g, row(v_final_norm_g)]
    d_small, nm_small, nv_small = _adam_small(g_small_list, w_small, m_small, v_small)
    flat = lambda lst: list(lst[:-1]) + [lst[-1].reshape(D_MODEL)]
    g_small_list, d_small, nm_small, nv_small = flat(g_small_list), flat(d_small), flat(nm_small), flat(nv_small)

    def order(s, w_in_v, w_out_v, w1_v, w2_v):
        return [s[0], w_in_v, s[1], s[2], s[3], s[4], s[5], s[6], s[7], w_out_v, s[8], w1_v, w2_v, s[9]]

    grads = order(g_small_list, cut(gi), go[None], ga[None], gb[None])
    deltas = order(d_small, cut(di), do[None], da_[None], db[None])
    new_m = order(nm_small, cut(mi), mo[None], ma[None], mb[None])
    new_v = order(nv_small, cut(vi), vo[None], va[None], vb[None])
    return (loss_out, dx[None], *grads, *deltas, *new_m, *new_v)
```

```python
from typing import NamedTuple

import jax
import jax.numpy as jnp
from jax import lax
from jax.experimental import pallas as pl
from jax.experimental.pallas import tpu as pltpu

F32 = jnp.float32
BF16 = jnp.bfloat16

N_DEV = 8
D_MODEL = 1024
HEADS = 4
DK = 64
DV = 128
KEY = HEADS * DK
VAL = HEADS * DV
RANK = 16
CONV = 512
GROUPS = 8
CONV_W = 31
HALO = 32
SUBLANES = 8
STRIP = 32
D_FF = 4096
D_IN = 2576
SHARD_IN = D_IN // N_DEV
CHUNK = 64
SUB = 256
EPS = 1e-6
GATE_TAU = 16.0
Q_SCALE = DK ** -0.5

P_Q, P_K, P_V, P_G, P_CI, P_CG, P_Z = 0, 256, 512, 1024, 1536, 2048, 2560
D_INP = 2688
Z_PAD = D_INP - P_Z

ADAM_LR = 0.001
ADAM_B1 = 0.9
ADAM_B2 = 0.999
ADAM_EPS = 1e-08
ADAM_WD = 0.01
ADAM_STEP = 10

VMEM_LIMIT = 56 * 1024 * 1024

MESH = pl.DeviceIdType.MESH
ANY = pl.BlockSpec(memory_space=pl.ANY)


def _nn(a, b):
    return jnp.dot(a, b, preferred_element_type=F32)


def _nt(a, b):
    return lax.dot_general(a, b, (((1,), (1,)), ((), ())), preferred_element_type=F32)


def _tn(a, b):
    return lax.dot_general(a, b, (((0,), (0,)), ((), ())), preferred_element_type=F32)


def _params(sem=None):
    return pltpu.CompilerParams(dimension_semantics=sem, vmem_limit_bytes=VMEM_LIMIT)


def _const(shape):
    return pl.BlockSpec(shape, lambda *_: (0,) * len(shape), pipeline_mode=pl.Buffered(1))


def _colsum(v):
    return jnp.sum(v, axis=0, keepdims=True)


def _rowmean(v):
    return jnp.mean(v, axis=-1, keepdims=True)


def _split_bf16(v):
    hi = v.astype(BF16)
    return hi, (v - hi.astype(F32)).astype(BF16)


def _my_place():
    return lax.axis_index("x"), lax.axis_index("y"), lax.axis_index("c")


def _peer(j):
    x, y, c = _my_place()
    jx, jy, jc = (j >> 2) & 1, (j >> 1) & 1, j & 1
    px = 1 - x if jx else x
    py = 1 - y if jy else y
    pc = 1 - c if jc else c
    return (px, py, pc), 4 * px + 2 * py + pc


def _comm_plan(kinds, ins, outs, send_sems, recv_sems, local_sems, receives=True):
    x, y, c = _my_place()
    me = 4 * x + 2 * y + c
    own = lambda k, idx: ins[k] if kinds[k] == "gather" else ins[k].at[idx]
    local = [pltpu.make_async_copy(own(k, me), outs[k].at[me], local_sems.at[k]) for k in range(len(kinds))]
    sends, recvs = [], []
    for j in range(1, N_DEV):
        peer, peer_idx = _peer(j)
        for k in range(len(kinds)):
            sems = dict(send_sem=send_sems.at[k, j - 1], recv_sem=recv_sems.at[k, j - 1], device_id=peer,
                        device_id_type=MESH)
            sends.append(pltpu.make_async_remote_copy(src_ref=own(k, peer_idx), dst_ref=outs[k].at[me], **sems))
            if receives:
                recvs.append(pltpu.make_async_remote_copy(src_ref=own(k, me), dst_ref=outs[k].at[peer_idx], **sems))
    return local, sends, recvs


def _comm_start(plan):
    local, sends, _ = plan
    for cp in local + sends:
        cp.start()


def _comm_wait(plan):
    local, sends, recvs = plan
    for cp in recvs:
        cp.wait_recv()
    for cp in sends:
        cp.wait_send()
    for cp in local:
        cp.wait()


def _comm_scratch(n):
    return [pltpu.SemaphoreType.DMA((n, N_DEV - 1)), pltpu.SemaphoreType.DMA((n, N_DEV - 1)),
            pltpu.SemaphoreType.DMA((n,))]


def _comm_out_shapes(kinds, arrays):
    return [jax.ShapeDtypeStruct(((N_DEV,) + a.shape) if kind == "gather" else a.shape, a.dtype)
            for kind, a in zip(kinds, arrays)]


def _comm(kinds, arrays, name):
    n = len(arrays)

    def body(*refs):
        plan = _comm_plan(kinds, refs[:n], refs[n:2 * n], *refs[2 * n:])
        _comm_start(plan)
        _comm_wait(plan)

    return pl.pallas_call(
        body, name=name, out_shape=_comm_out_shapes(kinds, arrays), in_specs=[ANY] * n, out_specs=[ANY] * n,
        scratch_shapes=_comm_scratch(n),
    )(*arrays)


def _gather_two_level(shards, name):
    n = len(shards)

    def body(*refs):
        ins, outs = refs[:n], refs[n:2 * n]
        send_sems, recv_sems, local_sems = refs[2 * n:]
        x, y, c = _my_place()
        index = lambda px, py, pc: 4 * px + 2 * py + pc
        me, sibling = (x, y, c), (x, y, 1 - c)
        chips = [(1 - x, y), (x, 1 - y), (1 - x, 1 - y)]

        def copy(k, slot, block, to, src=None):
            rows = outs[k].at[index(*block)]
            return pltpu.make_async_remote_copy(
                src_ref=rows if src is None else src, dst_ref=rows, send_sem=send_sems.at[k, slot],
                recv_sem=recv_sems.at[k, slot], device_id=to, device_id_type=MESH)

        local = [pltpu.make_async_copy(ins[k], outs[k].at[index(*me)], local_sems.at[k]) for k in range(n)]
        first = []
        for k in range(n):
            first.append(copy(k, 0, me, sibling, src=ins[k]))
            first += [copy(k, 1 + j, me, (*chip, c), src=ins[k]) for j, chip in enumerate(chips)]
        for cp in local + first:
            cp.start()
        passed = []
        for j, chip in enumerate(chips):
            for k in range(n):
                copy(k, 1 + j, (*chip, c), me).wait_recv()
                cp = copy(k, 4 + j, (*chip, c), sibling)
                cp.start()
                passed.append(cp)
        for k in range(n):
            copy(k, 0, sibling, me).wait_recv()
        for j, chip in enumerate(chips):
            for k in range(n):
                copy(k, 4 + j, (*chip, 1 - c), me).wait_recv()
        for cp in first + passed:
            cp.wait_send()
        for cp in local:
            cp.wait()

    return pl.pallas_call(
        body, name=name, out_shape=_comm_out_shapes(["gather"] * n, shards), in_specs=[ANY] * n, out_specs=[ANY] * n,
        scratch_shapes=_comm_scratch(n),
    )(*shards)


def _hosted_comm(kinds, n_in, n_out, n_comm, n_steps):
    def plan_of(refs, receives):
        ins = refs[n_in:n_in + n_comm]
        outs = refs[n_in + n_comm + n_out:n_in + 2 * n_comm + n_out]
        return _comm_plan(kinds, ins, outs, *refs[-3:], receives=receives)

    def start(refs):
        @pl.when(pl.program_id(0) == 0)
        def _():
            _comm_start(plan_of(refs, False))

    def wait(refs):
        @pl.when(pl.program_id(0) == n_steps - 1)
        def _():
            _comm_wait(plan_of(refs, True))

    return start, wait


def _inproj_fwd(x, g1, w_in_p):
    T = x.shape[0]
    tm = min(T, 512)

    def body(x_ref, g_ref, w_ref, proj_ref, xn_ref):
        xv = x_ref[...]
        r = lax.rsqrt(_rowmean(xv * xv) + EPS)
        xn = (xv * r * g_ref[...]).astype(BF16)
        xn_ref[...] = xn
        proj_ref[...] = _nt(xn, w_ref[...]).astype(BF16)

    return pl.pallas_call(
        body, name="inproj_fwd", grid=(T // tm,),
        out_shape=[jax.ShapeDtypeStruct((T, D_INP), BF16), jax.ShapeDtypeStruct((T, D_MODEL), BF16)],
        in_specs=[pl.BlockSpec((tm, D_MODEL), lambda i: (i, 0)), _const((1, D_MODEL)), _const((D_INP, D_MODEL))],
        out_specs=[pl.BlockSpec((tm, D_INP), lambda i: (i, 0)), pl.BlockSpec((tm, D_MODEL), lambda i: (i, 0))],
        compiler_params=_params(("arbitrary",)),
    )(x, g1, w_in_p)


def _head_masks():
    lane = lax.broadcasted_iota(jnp.int32, (1, KEY), 1)
    return [((lane >= h * DK) & (lane < (h + 1) * DK)).astype(F32) for h in range(HEADS)]


class _Mats(NamedTuple):
    tri: jax.Array
    tri_t: jax.Array
    same: jax.Array
    mid: jax.Array
    causal: jax.Array
    causal_t: jax.Array
    heads: jax.Array


def _chunk_matrices():
    r = lax.broadcasted_iota(jnp.int32, (SUB, SUB), 0)
    c = lax.broadcasted_iota(jnp.int32, (SUB, SUB), 1)
    shift = CHUNK.bit_length() - 1
    same = jnp.right_shift(r, shift) == jnp.right_shift(c, shift)
    causal = same & (r >= c)
    causal_t = same & (r <= c)
    mid = same & ((c & (CHUNK - 1)) < CHUNK // 2)
    hr = jnp.right_shift(lax.broadcasted_iota(jnp.int32, (VAL, KEY), 0), DV.bit_length() - 1)
    hc = jnp.right_shift(lax.broadcasted_iota(jnp.int32, (VAL, KEY), 1), DK.bit_length() - 1)
    return _Mats(tri=causal.astype(BF16), tri_t=causal_t.astype(BF16), same=same.astype(BF16), mid=mid.astype(BF16),
                 causal=causal, causal_t=causal_t, heads=hr == hc)


class _Decay(NamedTuple):
    al: jax.Array
    q: jax.Array
    k: jax.Array
    eb: jax.Array
    ebm: jax.Array
    emb: jax.Array
    elb: jax.Array
    ebl: jax.Array


def _decay_terms(z, q, k, wg, bg, mats):
    al = _nn(z, wg) + bg
    la = (jnp.minimum(al, 0.0) - jnp.log(1.0 + jnp.exp(-jnp.abs(al)))) * (1.0 / GATE_TAU)
    hi, lo = _split_bf16(la)
    cum = lambda m: _nn(m, hi) + _nn(m, lo)
    b, b_last, b_mid = cum(mats.tri), cum(mats.same), cum(mats.mid)
    return _Decay(al=al, q=q.astype(F32) * Q_SCALE, k=k.astype(F32), eb=jnp.exp(b), ebm=jnp.exp(b - b_mid),
                  emb=jnp.exp(b_mid - b), elb=jnp.exp(b_last - b), ebl=jnp.exp(b_last))


def _gla_fwd(proj, wg, bg, gn, shards):
    T = proj.shape[0]
    tb = min(T, 512)
    cpb = tb // CHUNK
    n_comm = len(shards)
    kinds = ["gather"] * n_comm
    comm_start, comm_wait = _hosted_comm(kinds, 8, 3, n_comm, T // tb)

    def body(*refs):
        q_ref, k_ref, v_ref, g_ref, z_ref, wg_ref, bg_ref, gn_ref = refs[:8]
        mix_ref, o_ref, st_ref = refs[8 + n_comm:11 + n_comm]
        state = refs[11 + 2 * n_comm]
        comm_start(refs)

        @pl.when(pl.program_id(0) == 0)
        def _():
            state[...] = jnp.zeros_like(state)

        mats = _chunk_matrices()
        masks = _head_masks()
        wgv, bgv = wg_ref[...], bg_ref[...]

        for sb in range(tb // SUB):
            rows = slice(sb * SUB, (sb + 1) * SUB)
            d = _decay_terms(z_ref[rows, :], q_ref[rows, :], k_ref[rows, :], wgv, bgv, mats)
            kem_b = (d.k * d.emb).astype(BF16)
            qem = d.q * d.ebm
            for h in range(HEADS):
                cols = slice(h * DV, (h + 1) * DV)
                a = jnp.where(mats.causal, _nt((qem * masks[h]).astype(BF16), kem_b), 0.0)
                o_ref[rows, cols] = _nn(a.astype(BF16), v_ref[rows, cols])
            qe0_b = (d.q * d.eb).astype(BF16)
            kdec_b = (d.k * d.elb).astype(BF16)
            for c in range(SUB // CHUNK):
                loc = slice(c * CHUNK, (c + 1) * CHUNK)
                glob = slice(sb * SUB + c * CHUNK, sb * SUB + (c + 1) * CHUNK)
                st = state[...]
                st_b = st.astype(BF16)
                st_ref[sb * (SUB // CHUNK) + c] = st_b
                o_ref[glob, :] += _nt(qe0_b[loc], st_b)
                u = _tn(v_ref[glob, :], kdec_b[loc])
                state[...] = st * d.ebl[c * CHUNK:c * CHUNK + 1] + jnp.where(mats.heads, u, 0.0)

        gnv = gn_ref[...]
        for h in range(HEADS):
            cols = slice(h * DV, (h + 1) * DV)
            oh = o_ref[:, cols]
            r = lax.rsqrt(_rowmean(oh * oh) + EPS)
            gh = g_ref[:, cols].astype(F32)
            mix_ref[:, cols] = (oh * r * gnv * (gh * jax.nn.sigmoid(gh))).astype(BF16)
        comm_wait(refs)

    nc = T // CHUNK
    res = pl.pallas_call(
        body, name="gla_fwd", grid=(T // tb,),
        out_shape=[jax.ShapeDtypeStruct((T, VAL), BF16), jax.ShapeDtypeStruct((T, VAL), F32),
                   jax.ShapeDtypeStruct((nc, VAL, KEY), BF16)] + _comm_out_shapes(kinds, shards),
        in_specs=[pl.BlockSpec((tb, KEY), lambda i: (i, P_Q // KEY)), pl.BlockSpec((tb, KEY), lambda i: (i, P_K // KEY)),
                  pl.BlockSpec((tb, VAL), lambda i: (i, P_V // VAL)), pl.BlockSpec((tb, VAL), lambda i: (i, P_G // VAL)),
                  pl.BlockSpec((tb, Z_PAD), lambda i: (i, P_Z // Z_PAD)),
                  _const((Z_PAD, KEY)), _const((1, KEY)), _const((1, DV))] + [ANY] * n_comm,
        out_specs=[pl.BlockSpec((tb, VAL), lambda i: (i, 0)), pl.BlockSpec((tb, VAL), lambda i: (i, 0)),
                   pl.BlockSpec((cpb, VAL, KEY), lambda i: (i, 0, 0))] + [ANY] * n_comm,
        scratch_shapes=[pltpu.VMEM((VAL, KEY), F32)] + _comm_scratch(n_comm),
        compiler_params=_params(("arbitrary",)),
    )(proj, proj, proj, proj, proj, wg, bg, gn, *shards)
    return res[0], res[1], res[2], res[3:]


def _group_mean(v, gmat):
    return _nn(v.astype(BF16), gmat)


def _shifted_copies(buf, sh, rows):
    for k in range(1, SUBLANES):
        sh[k - 1] = buf[pl.ds(k, rows), :]


def _tap(buf, sh, off, r0):
    k, base = off % SUBLANES, off - off % SUBLANES
    rows = pl.ds(pl.multiple_of(r0 + base, SUBLANES), STRIP)
    return buf[rows, :] if k == 0 else sh[k - 1, rows, :]


def _conv_fwd(proj, conv_w, conv_b, cn_g, cn_b, gmat, shards):
    T = proj.shape[0]
    tm = min(T, 512)
    sh_rows = tm + HALO - SUBLANES
    n_comm = len(shards)
    kinds = ["gather"] * n_comm
    comm_start, comm_wait = _hosted_comm(kinds, 7, 2, n_comm, T // tm)

    def body(*refs):
        ci_ref, cg_ref, w_ref, b_ref, g_ref, be_ref, gm_ref = refs[:7]
        mix_ref, uc_ref = refs[7 + n_comm:9 + n_comm]
        ubuf, ush = refs[9 + 2 * n_comm:11 + 2 * n_comm]
        comm_start(refs)

        @pl.when(pl.program_id(0) == 0)
        def _():
            ubuf[0:HALO, :] = jnp.zeros((HALO, CONV), F32)

        ubuf[HALO:, :] = ci_ref[...].astype(F32) * jax.nn.sigmoid(cg_ref[...].astype(F32))
        _shifted_copies(ubuf, ush, sh_rows)

        def strip(s, carry):
            r0 = pl.multiple_of(s * STRIP, STRIP)
            acc = jnp.zeros((STRIP, CONV), F32) + b_ref[...]
            for j in range(CONV_W):
                acc = acc + w_ref[j:j + 1, :] * _tap(ubuf, ush, HALO - (CONV_W - 1) + j, r0)
            uc_ref[pl.ds(r0, STRIP), :] = acc
            return carry

        lax.fori_loop(0, tm // STRIP, strip, 0)
        ubuf[0:HALO, :] = ubuf[tm:tm + HALO, :]
        gm = gm_ref[...]
        ucv = uc_ref[...]
        d = ucv - _group_mean(ucv, gm)
        var = _group_mean(d * d, gm)
        yn = d * lax.rsqrt(var + EPS) * g_ref[...] + be_ref[...]
        mix_ref[...] = (yn * jax.nn.sigmoid(yn)).astype(BF16)
        comm_wait(refs)

    res = pl.pallas_call(
        body, name="conv_fwd", grid=(T // tm,),
        out_shape=[jax.ShapeDtypeStruct((T, CONV), BF16), jax.ShapeDtypeStruct((T, CONV), F32)]
        + _comm_out_shapes(kinds, shards),
        in_specs=[pl.BlockSpec((tm, CONV), lambda i: (i, P_CI // CONV)), pl.BlockSpec((tm, CONV), lambda i: (i, P_CG // CONV)),
                  _const((HALO, CONV)), _const((1, CONV)), _const((1, CONV)), _const((1, CONV)), _const((CONV, CONV))]
        + [ANY] * n_comm,
        out_specs=[pl.BlockSpec((tm, CONV), lambda i: (i, 0)), pl.BlockSpec((tm, CONV), lambda i: (i, 0))]
        + [ANY] * n_comm,
        scratch_shapes=[pltpu.VMEM((tm + HALO, CONV), F32), pltpu.VMEM((SUBLANES - 1, sh_rows, CONV), F32)]
        + _comm_scratch(n_comm),
        compiler_params=_params(("arbitrary",)),
    )(proj, proj, conv_w, conv_b, cn_g, cn_b, gmat, *shards)
    return res[0], res[1], res[2:]


def _rms_bwd(dy, xhat, r, g):
    dyg = dy * g
    return r * (dyg - xhat * _rowmean(dyg * xhat))


def _mlp_fwd_bwd(x, mix_a, mix_c, tgt, w_out, g2, w1t, w2, gf):
    T = x.shape[0]
    tm = min(T, 256)
    inv_d = 1.0 / D_MODEL

    def body(x_ref, ma_ref, mc_ref, t_ref, wo_ref, g2_ref, w1_ref, w2_ref, gf_ref,
             dh1_ref, dmix_ref, hn_ref, ff_ref, da_ref, dh2_ref, loss_ref, dgf_ref, dg2_ref):
        @pl.when(pl.program_id(0) == 0)
        def _():
            loss_ref[...] = jnp.zeros_like(loss_ref)
            dgf_ref[...] = jnp.zeros_like(dgf_ref)
            dg2_ref[...] = jnp.zeros_like(dg2_ref)

        g2v, gfv = g2_ref[...], gf_ref[...]
        h1 = x_ref[...] + _nn(ma_ref[...], wo_ref[0:VAL, :]) + _nn(mc_ref[...], wo_ref[VAL:, :])
        r2 = lax.rsqrt(_rowmean(h1 * h1) + EPS)
        h1hat = h1 * r2
        hn = (h1hat * g2v).astype(BF16)
        hn_ref[...] = hn
        relu_a = jnp.maximum(_nt(hn, w1_ref[...]), 0.0)
        ff = (relu_a * relu_a).astype(BF16)
        ff_ref[...] = ff
        h2 = h1 + _nn(ff, w2_ref[...])
        rf = lax.rsqrt(_rowmean(h2 * h2) + EPS)
        h2hat = h2 * rf
        err = h2hat * gfv - t_ref[...]
        loss_ref[...] += (0.5 * inv_d) * _colsum(jnp.sum(err * err, axis=1, keepdims=True))
        dy = err * inv_d
        dgf_ref[...] += _colsum(dy * h2hat)
        dh2 = _rms_bwd(dy, h2hat, rf, gfv)
        dh2_b = dh2.astype(BF16)
        dh2_ref[...] = dh2_b
        da = (_nt(dh2_b, w2_ref[...]) * (2.0 * relu_a)).astype(BF16)
        da_ref[...] = da
        dhn = _nn(da, w1_ref[...])
        dg2_ref[...] += _colsum(dhn * h1hat)
        dh1 = dh2 + _rms_bwd(dhn, h1hat, r2, g2v)
        dh1_ref[...] = dh1
        dmix_ref[...] = _nt(dh1.astype(BF16), wo_ref[...]).astype(BF16)

    tok = lambda w: pl.BlockSpec((tm, w), lambda i: (i, 0))
    return pl.pallas_call(
        body, name="mlp_fwd_bwd", grid=(T // tm,),
        out_shape=[jax.ShapeDtypeStruct((T, D_MODEL), F32), jax.ShapeDtypeStruct((T, D_MODEL), BF16),
                   jax.ShapeDtypeStruct((T, D_MODEL), BF16), jax.ShapeDtypeStruct((T, D_FF), BF16),
                   jax.ShapeDtypeStruct((T, D_FF), BF16), jax.ShapeDtypeStruct((T, D_MODEL), BF16),
                   jax.ShapeDtypeStruct((1, 1), F32), jax.ShapeDtypeStruct((1, D_MODEL), F32),
                   jax.ShapeDtypeStruct((1, D_MODEL), F32)],
        in_specs=[tok(D_MODEL), tok(VAL), tok(CONV), tok(D_MODEL), _const((D_MODEL, D_MODEL)), _const((1, D_MODEL)),
                  _const((D_FF, D_MODEL)), _const((D_FF, D_MODEL)), _const((1, D_MODEL))],
        out_specs=[tok(D_MODEL), tok(D_MODEL), tok(D_MODEL), tok(D_FF), tok(D_FF), tok(D_MODEL),
                   pl.BlockSpec((1, 1), lambda i: (0, 0)), pl.BlockSpec((1, D_MODEL), lambda i: (0, 0)),
                   pl.BlockSpec((1, D_MODEL), lambda i: (0, 0))],
        compiler_params=_params(("arbitrary",)),
    )(x, mix_a, mix_c, tgt, w_out, g2, w1t, w2, gf)


def _silu_grad(v, s):
    return s * (1.0 + v * (1.0 - s))


def _conv_bwd(proj, uc, dmix, conv_w, cn_g, cn_b, gmat):
    T = proj.shape[0]
    tm = min(T, 512)
    nt = T // tm
    sh_rows = tm + HALO - SUBLANES
    n_strips = tm // STRIP

    def body(ci_ref, cg_ref, uc_ref, dm_ref, w_ref, g_ref, be_ref, gm_ref,
             dci_ref, dcg_ref, dw_ref, db_ref, dg_ref, dbe_ref, dbuf, dsh, dwacc):
        step = pl.program_id(0)

        @pl.when(step == 0)
        def _():
            dbuf[tm:, :] = jnp.zeros((HALO, CONV), F32)
            dwacc[...] = jnp.zeros_like(dwacc)
            db_ref[...] = jnp.zeros_like(db_ref)
            dg_ref[...] = jnp.zeros_like(dg_ref)
            dbe_ref[...] = jnp.zeros_like(dbe_ref)

        gm, gv = gm_ref[...], g_ref[...]
        ucv = uc_ref[...]
        d = ucv - _group_mean(ucv, gm)
        rs = lax.rsqrt(_group_mean(d * d, gm) + EPS)
        yhat = d * rs
        yn = yhat * gv + be_ref[...]
        dyn = dm_ref[...].astype(F32) * _silu_grad(yn, jax.nn.sigmoid(yn))
        dg_ref[...] += _colsum(dyn * yhat)
        dbe_ref[...] += _colsum(dyn)
        dyh = dyn * gv
        duc = rs * (dyh - _group_mean(dyh, gm) - yhat * _group_mean(dyh * yhat, gm))
        db_ref[...] += _colsum(duc)
        dbuf[0:tm, :] = duc
        _shifted_copies(dbuf, dsh, sh_rows)

        def strip(s, carry):
            r0 = pl.multiple_of(s * STRIP, STRIP)
            rows = pl.ds(r0, STRIP)
            cin = ci_ref[rows, :].astype(F32)
            sg = jax.nn.sigmoid(cg_ref[rows, :].astype(F32))
            u = cin * sg
            du = jnp.zeros((STRIP, CONV), F32)
            for j in range(CONV_W):
                dj = _tap(dbuf, dsh, CONV_W - 1 - j, r0)
                du = du + w_ref[j:j + 1, :] * dj
                p = u * dj
                fold = p[0:SUBLANES]
                for q in range(1, STRIP // SUBLANES):
                    fold = fold + p[q * SUBLANES:(q + 1) * SUBLANES, :]
                dwacc[j * SUBLANES:(j + 1) * SUBLANES, :] += fold
            dci_ref[rows, :] = (du * sg).astype(BF16)
            dcg_ref[rows, :] = (du * cin * sg * (1.0 - sg)).astype(BF16)
            return carry

        lax.fori_loop(0, n_strips, strip, 0)
        dbuf[tm:, :] = dbuf[0:HALO, :]

        @pl.when(step == nt - 1)
        def _():
            dw_ref[...] = jnp.zeros_like(dw_ref)
            for j in range(CONV_W):
                dw_ref[j:j + 1, :] = _colsum(dwacc[j * SUBLANES:(j + 1) * SUBLANES, :])

    rev = lambda i: nt - 1 - i
    tile = lambda col: pl.BlockSpec((tm, CONV), lambda i: (rev(i), col))
    acc = lambda rows: pl.BlockSpec((rows, CONV), lambda i: (0, 0))
    return pl.pallas_call(
        body, name="conv_bwd", grid=(nt,),
        out_shape=[jax.ShapeDtypeStruct((T, CONV), BF16), jax.ShapeDtypeStruct((T, CONV), BF16),
                   jax.ShapeDtypeStruct((HALO, CONV), F32), jax.ShapeDtypeStruct((1, CONV), F32),
                   jax.ShapeDtypeStruct((1, CONV), F32), jax.ShapeDtypeStruct((1, CONV), F32)],
        in_specs=[tile(P_CI // CONV), tile(P_CG // CONV), tile(0), tile(1),
                  _const((HALO, CONV)), _const((1, CONV)), _const((1, CONV)), _const((CONV, CONV))],
        out_specs=[tile(0), tile(0), acc(HALO), acc(1), acc(1), acc(1)],
        scratch_shapes=[pltpu.VMEM((tm + HALO, CONV), F32), pltpu.VMEM((SUBLANES - 1, sh_rows, CONV), F32),
                        pltpu.VMEM((HALO * SUBLANES, CONV), F32)],
        compiler_params=_params(("arbitrary",)),
    )(proj, proj, uc, dmix, conv_w, cn_g, cn_b, gmat)


def _gla_bwd(proj, o, states, dmix, wg, bg, gn, parts):
    T = proj.shape[0]
    tb = min(T, 512)
    cpb = tb // CHUNK
    nb = T // tb
    n_comm = len(parts)
    kinds = ["exchange"] * n_comm
    comm_start, comm_wait = _hosted_comm(kinds, 11, 8, n_comm, nb)

    def body(*refs):
        q_ref, k_ref, v_ref, g_ref, z_ref, o_ref, st_ref, dm_ref, wg_ref, bg_ref, gn_ref = refs[:11]
        dq_ref, dk_ref, dv_ref, dg_ref, dz_ref, dwg_ref, dbg_ref, dgn_ref = refs[11 + n_comm:19 + n_comm]
        dstate, do_scr, dv_scr = refs[19 + 2 * n_comm:22 + 2 * n_comm]
        comm_start(refs)

        @pl.when(pl.program_id(0) == 0)
        def _():
            dstate[...] = jnp.zeros_like(dstate)
            dwg_ref[...] = jnp.zeros_like(dwg_ref)
            dbg_ref[...] = jnp.zeros_like(dbg_ref)
            dgn_ref[...] = jnp.zeros_like(dgn_ref)

        gnv = gn_ref[...]
        dgn = jnp.zeros((1, DV), F32)
        for h in range(HEADS):
            cols = slice(h * DV, (h + 1) * DV)
            oh = o_ref[:, cols]
            r = lax.rsqrt(_rowmean(oh * oh) + EPS)
            ohat = oh * r
            gh = g_ref[:, cols].astype(F32)
            sg = jax.nn.sigmoid(gh)
            dmx = dm_ref[:, cols].astype(F32)
            don = dmx * (gh * sg)
            dg_ref[:, cols] = (dmx * (ohat * gnv) * _silu_grad(gh, sg)).astype(BF16)
            dgn = dgn + _colsum(don * ohat)
            do_scr[:, cols] = _rms_bwd(don, ohat, r, gnv)
        dgn_ref[...] += dgn

        mats = _chunk_matrices()
        masks = _head_masks()
        wgv, bgv = wg_ref[...], bg_ref[...]
        n_chunks = SUB // CHUNK

        for sb in reversed(range(tb // SUB)):
            rows = slice(sb * SUB, (sb + 1) * SUB)
            zs = z_ref[rows, :]
            d = _decay_terms(zs, q_ref[rows, :], k_ref[rows, :], wgv, bgv, mats)
            qem = d.q * d.ebm
            qem_b = qem.astype(BF16)
            kem_b = (d.k * d.emb).astype(BF16)
            dq = jnp.zeros((SUB, KEY), F32)
            dk = jnp.zeros((SUB, KEY), F32)
            for h in range(HEADS):
                hm = masks[h]
                cols = slice(h * DV, (h + 1) * DV)
                do_b = do_scr[rows, cols].astype(BF16)
                vh = v_ref[rows, cols]
                da = jnp.where(mats.causal, _nt(do_b, vh), 0.0).astype(BF16)
                da_t = jnp.where(mats.causal_t, _nt(vh, do_b), 0.0).astype(BF16)
                a_t = jnp.where(mats.causal_t, _nt(kem_b, (qem * hm).astype(BF16)), 0.0).astype(BF16)
                dq = dq + hm * _nn(da, kem_b)
                dk = dk + hm * _nn(da_t, qem_b)
                dv_scr[rows, cols] = _nn(a_t, do_b)
            dq = dq * d.ebm
            dk = dk * d.emb

            qe0_b = (d.q * d.eb).astype(BF16)
            kdec_b = (d.k * d.elb).astype(BF16)
            dq_st, dk_st, last = [None] * n_chunks, [None] * n_chunks, [None] * n_chunks
            for c in reversed(range(n_chunks)):
                loc = slice(c * CHUNK, (c + 1) * CHUNK)
                glob = slice(sb * SUB + c * CHUNK, sb * SUB + (c + 1) * CHUNK)
                st_b = st_ref[sb * n_chunks + c]
                ds = dstate[...]
                ds_b = ds.astype(BF16)
                do_c = do_scr[glob, :].astype(BF16)
                ebl_c = d.ebl[c * CHUNK:c * CHUNK + 1]
                dk_c = _nn(v_ref[glob, :], ds_b) * d.elb[loc]
                dq_st[c] = _nn(do_c, st_b) * d.eb[loc]
                dk_st[c] = dk_c
                last_c = _colsum(d.k[loc] * dk_c) + ebl_c * _colsum(st_b.astype(F32) * ds)
                last[c] = jnp.broadcast_to(last_c, (CHUNK, KEY))
                dv_ref[glob, :] = (dv_scr[glob, :] + _nt(kdec_b[loc], ds_b)).astype(BF16)
                dstate[...] = ds * ebl_c + jnp.where(mats.heads, _tn(do_c, qe0_b[loc]), 0.0)
            dq = dq + jnp.concatenate(dq_st, axis=0)
            dk = dk + jnp.concatenate(dk_st, axis=0)
            dq_ref[rows, :] = (dq * Q_SCALE).astype(BF16)
            dk_ref[rows, :] = dk.astype(BF16)
            hi, lo = _split_bf16(d.q * dq - d.k * dk)
            dla = _nn(mats.tri_t, hi) + _nn(mats.tri_t, lo) + jnp.concatenate(last, axis=0)
            dal = dla * (1.0 / GATE_TAU) * jax.nn.sigmoid(-d.al)
            dal_b = dal.astype(BF16)
            dz_ref[rows, :] = _nt(dal_b, wgv).astype(BF16)
            dwg_ref[...] += _tn(zs, dal_b)
            dbg_ref[...] += _colsum(dal)
        comm_wait(refs)

    rev = lambda i: nb - 1 - i
    blk = lambda w, col: pl.BlockSpec((tb, w), lambda i: (rev(i), col))
    res = pl.pallas_call(
        body, name="gla_bwd", grid=(nb,),
        out_shape=[jax.ShapeDtypeStruct((T, KEY), BF16), jax.ShapeDtypeStruct((T, KEY), BF16),
                   jax.ShapeDtypeStruct((T, VAL), BF16), jax.ShapeDtypeStruct((T, VAL), BF16),
                   jax.ShapeDtypeStruct((T, Z_PAD), BF16), jax.ShapeDtypeStruct((Z_PAD, KEY), F32),
                   jax.ShapeDtypeStruct((1, KEY), F32), jax.ShapeDtypeStruct((1, DV), F32)]
        + _comm_out_shapes(kinds, parts),
        in_specs=[blk(KEY, P_Q // KEY), blk(KEY, P_K // KEY), blk(VAL, P_V // VAL), blk(VAL, P_G // VAL),
                  blk(Z_PAD, P_Z // Z_PAD), blk(VAL, 0),
                  pl.BlockSpec((cpb, VAL, KEY), lambda i: (rev(i), 0, 0)), blk(VAL, 0),
                  _const((Z_PAD, KEY)), _const((1, KEY)), _const((1, DV))] + [ANY] * n_comm,
        out_specs=[blk(KEY, 0), blk(KEY, 0), blk(VAL, 0), blk(VAL, 0), blk(Z_PAD, 0),
                   pl.BlockSpec((Z_PAD, KEY), lambda i: (0, 0)), pl.BlockSpec((1, KEY), lambda i: (0, 0)),
                   pl.BlockSpec((1, DV), lambda i: (0, 0))] + [ANY] * n_comm,
        scratch_shapes=[pltpu.VMEM((VAL, KEY), F32), pltpu.VMEM((tb, VAL), F32), pltpu.VMEM((tb, VAL), F32)]
        + _comm_scratch(n_comm),
        compiler_params=_params(("arbitrary",)),
    )(proj, proj, proj, proj, proj, o, states, dmix, wg, bg, gn, *parts)
    return res[:8], res[8:]


def _inproj_bwd(x, g1, w_in_p, dh1, dq, dk, dv, dg, dci, dcg, dz, parts):
    T = x.shape[0]
    tm = min(T, 512)
    n_comm = len(parts)
    kinds = ["exchange"] * n_comm
    comm_start, comm_wait = _hosted_comm(kinds, 11, 2, n_comm, T // tm)

    def body(*refs):
        x_ref, g_ref, w_ref, dh1_ref, dq_ref, dk_ref, dv_ref, dg_ref, dci_ref, dcg_ref, dz_ref = refs[:11]
        dx_ref, dg1_ref = refs[11 + n_comm:13 + n_comm]
        dp_ref = refs[13 + 2 * n_comm]
        comm_start(refs)

        @pl.when(pl.program_id(0) == 0)
        def _():
            dg1_ref[...] = jnp.zeros_like(dg1_ref)

        dp_ref[:, P_Q:P_K] = dq_ref[...]
        dp_ref[:, P_K:P_V] = dk_ref[...]
        dp_ref[:, P_V:P_G] = dv_ref[...]
        dp_ref[:, P_G:P_CI] = dg_ref[...]
        dp_ref[:, P_CI:P_CG] = dci_ref[...]
        dp_ref[:, P_CG:P_Z] = dcg_ref[...]
        dp_ref[:, P_Z:] = dz_ref[...]
        dxn = _nn(dp_ref[...], w_ref[...])
        xv = x_ref[...]
        r = lax.rsqrt(_rowmean(xv * xv) + EPS)
        xhat = xv * r
        dg1_ref[...] += _colsum(dxn * xhat)
        dx_ref[...] = dh1_ref[...] + _rms_bwd(dxn, xhat, r, g_ref[...])
        comm_wait(refs)

    tok = lambda w: pl.BlockSpec((tm, w), lambda i: (i, 0))
    res = pl.pallas_call(
        body, name="inproj_bwd", grid=(T // tm,),
        out_shape=[jax.ShapeDtypeStruct((T, D_MODEL), F32), jax.ShapeDtypeStruct((1, D_MODEL), F32)]
        + _comm_out_shapes(kinds, parts),
        in_specs=[tok(D_MODEL), _const((1, D_MODEL)), _const((D_INP, D_MODEL)), tok(D_MODEL), tok(KEY), tok(KEY),
                  tok(VAL), tok(VAL), tok(CONV), tok(CONV), tok(Z_PAD)] + [ANY] * n_comm,
        out_specs=[tok(D_MODEL), pl.BlockSpec((1, D_MODEL), lambda i: (0, 0))] + [ANY] * n_comm,
        scratch_shapes=[pltpu.VMEM((tm, D_INP), BF16)] + _comm_scratch(n_comm),
        compiler_params=_params(("arbitrary",)),
    )(x, g1, w_in_p, dh1, dq, dk, dv, dg, dci, dcg, dz, *parts)
    return res[0], res[1], res[2:]


def _wgrad_in(xn, pieces):
    T = xn.shape[0]
    tt = min(T, 1024)
    nt = T // tt
    widths = [p.shape[1] for p in pieces]
    assert sum(widths) == D_INP

    def body(*refs):
        xn_ref, piece_refs, o_ref, acc = refs[0], refs[1:1 + len(pieces)], refs[-2], refs[-1]

        @pl.when(pl.program_id(0) == 0)
        def _():
            acc[...] = jnp.zeros_like(acc)

        xv = xn_ref[...]
        row = 0
        for ref, w in zip(piece_refs, widths):
            acc[row:row + w, :] += _tn(ref[...], xv)
            row += w

        @pl.when(pl.program_id(0) == nt - 1)
        def _():
            o_ref[...] = acc[...].astype(BF16)

    tok = lambda w: pl.BlockSpec((tt, w), lambda t: (t, 0))
    return pl.pallas_call(
        body, name="wgrad_in", grid=(nt,), out_shape=jax.ShapeDtypeStruct((D_INP, D_MODEL), BF16),
        in_specs=[tok(D_MODEL)] + [tok(w) for w in widths],
        out_specs=pl.BlockSpec((D_INP, D_MODEL), lambda t: (0, 0), pipeline_mode=pl.Buffered(1)),
        scratch_shapes=[pltpu.VMEM((D_INP, D_MODEL), F32)],
        compiler_params=_params(("arbitrary",)),
    )(xn, *pieces)


def _wgrad_out(mix_a, mix_c, dh1):
    T = dh1.shape[0]
    tt = min(T, 2048)
    nt = T // tt

    def body(a_ref, c_ref, b_ref, o_ref, acc):
        @pl.when(pl.program_id(0) == 0)
        def _():
            acc[...] = jnp.zeros_like(acc)

        b = b_ref[...].astype(BF16)
        acc[0:VAL, :] += _tn(a_ref[...], b)
        acc[VAL:, :] += _tn(c_ref[...], b)

        @pl.when(pl.program_id(0) == nt - 1)
        def _():
            o_ref[...] = acc[...].astype(BF16)

    tok = lambda w: pl.BlockSpec((tt, w), lambda t: (t, 0))
    return pl.pallas_call(
        body, name="wgrad_out", grid=(nt,), out_shape=jax.ShapeDtypeStruct((D_MODEL, D_MODEL), BF16),
        in_specs=[tok(VAL), tok(CONV), tok(D_MODEL)],
        out_specs=pl.BlockSpec((D_MODEL, D_MODEL), lambda t: (0, 0)),
        scratch_shapes=[pltpu.VMEM((D_MODEL, D_MODEL), F32)],
        compiler_params=_params(("arbitrary",)),
    )(mix_a, mix_c, dh1)


def _wgrad(a, b, name, tk, tn, col_block=None):
    T, K = a.shape
    N = b.shape[1]
    tt = min(T, 2048)
    nt = T // tt

    def body(a_ref, b_ref, o_ref, acc):
        @pl.when(pl.program_id(2) == 0)
        def _():
            acc[...] = jnp.zeros_like(acc)

        acc[...] += _tn(a_ref[...], b_ref[...].astype(BF16))

        @pl.when(pl.program_id(2) == nt - 1)
        def _():
            if col_block is None:
                o_ref[...] = acc[...].astype(BF16)
            else:
                for q in range(tn // col_block):
                    o_ref[q] = acc[:, q * col_block:(q + 1) * col_block].astype(BF16)

    if col_block is None:
        out_shape = jax.ShapeDtypeStruct((K, N), BF16)
        out_spec = pl.BlockSpec((tk, tn), lambda i, j, t: (i, j))
    else:
        assert tk == K
        out_shape = jax.ShapeDtypeStruct((N // col_block, K, col_block), BF16)
        out_spec = pl.BlockSpec((tn // col_block, tk, col_block), lambda i, j, t: (j, 0, 0))
    return pl.pallas_call(
        body, name=name, grid=(K // tk, N // tn, nt), out_shape=out_shape,
        in_specs=[pl.BlockSpec((tt, tk), lambda i, j, t: (t, i)), pl.BlockSpec((tt, tn), lambda i, j, t: (t, j))],
        out_specs=out_spec, scratch_shapes=[pltpu.VMEM((tk, tn), F32)],
        compiler_params=_params(("arbitrary", "arbitrary", "arbitrary")),
    )(a, b)


def _adam_math(w, g, m, v):
    m = ADAM_B1 * m + (1.0 - ADAM_B1) * g
    v = ADAM_B2 * v + (1.0 - ADAM_B2) * (g * g)
    m_hat = m / (1.0 - ADAM_B1 ** ADAM_STEP)
    v_hat = v / (1.0 - ADAM_B2 ** ADAM_STEP)
    delta = -ADAM_LR * (m_hat / (jnp.sqrt(v_hat) + ADAM_EPS) + ADAM_WD * w)
    return delta, m, v


def _sum8(ref):
    g = ref[0].astype(F32)
    for s in range(1, N_DEV):
        g = g + ref[s].astype(F32)
    return g


def _adam_big(parts, w, m, v, name):
    R, C = w.shape
    tr = 128 if R % 128 == 0 else R

    def body(p_ref, w_ref, m_ref, v_ref, g_ref, d_ref, nm_ref, nv_ref):
        g = _sum8(p_ref)
        g_ref[...] = g
        d_ref[...], nm_ref[...], nv_ref[...] = _adam_math(w_ref[...], g, m_ref[...], v_ref[...])

    row = pl.BlockSpec((tr, C), lambda i: (i, 0))
    return pl.pallas_call(
        body, name=name, grid=(R // tr,), out_shape=[jax.ShapeDtypeStruct((R, C), F32)] * 4,
        in_specs=[pl.BlockSpec((N_DEV, tr, C), lambda i: (0, i, 0)), row, row, row], out_specs=[row] * 4,
        compiler_params=_params(("arbitrary",)),
    )(parts, w, m, v)


def _sum_small(parts):
    def body(p_ref, o_ref):
        o_ref[...] = _sum8(p_ref)

    return pl.pallas_call(body, name="sum_small", out_shape=jax.ShapeDtypeStruct(parts.shape[1:], F32))(parts)


def _adam_small(gs, ws, ms, vs):
    n = len(gs)

    def body(*refs):
        g_refs, w_refs, m_refs, v_refs = refs[:n], refs[n:2 * n], refs[2 * n:3 * n], refs[3 * n:4 * n]
        outs = refs[4 * n:]
        for i in range(n):
            d, nm, nv = _adam_math(w_refs[i][...], g_refs[i][...], m_refs[i][...], v_refs[i][...])
            outs[i][...] = d
            outs[n + i][...] = nm
            outs[2 * n + i][...] = nv

    shapes = [jax.ShapeDtypeStruct(w.shape, F32) for w in ws]
    res = pl.pallas_call(body, name="adam_small", out_shape=shapes * 3)(*gs, *ws, *ms, *vs)
    return res[:n], res[n:2 * n], res[2 * n:]


def _permute_in(wt):
    pad = jnp.zeros((D_INP - D_IN, wt.shape[1]), wt.dtype)
    return jnp.concatenate([wt[:1536], wt[1552:], wt[1536:1552], pad], axis=0)


def _unpermute_in(wt):
    return jnp.concatenate([wt[:P_CI], wt[P_Z:P_Z + RANK], wt[P_CI:P_Z]], axis=0)


def _group_matrix():
    gi = lax.broadcasted_iota(jnp.int32, (CONV, CONV), 0) // (CONV // GROUPS)
    gj = lax.broadcasted_iota(jnp.int32, (CONV, CONV), 1) // (CONV // GROUPS)
    return jnp.where(gi == gj, GROUPS / CONV, 0.0).astype(BF16)


_SMALL = [("loss", 8), ("dg1", 8), ("dbg", 2), ("dgn", 1), ("dconv_b", 4), ("dcn_g", 4), ("dcn_b", 4), ("dg2", 8),
          ("dgf", 8), ("dwg", 32), ("dconv_w", 124)]


def _pad8(rows):
    return -(-rows // 8) * 8


def kernel(x, norm1_g, w_in, w_gate_up, b_gate, gla_norm_g, conv_w, conv_b, conv_norm_g, conv_norm_b, w_out, norm2_g, w_mlp_in, w_mlp_out, final_norm_g, loss_target, m_norm1_g, m_w_in, m_w_gate_up, m_b_gate, m_gla_norm_g, m_conv_w, m_conv_b, m_conv_norm_g, m_conv_norm_b, m_w_out, m_norm2_g, m_w_mlp_in, m_w_mlp_out, m_final_norm_g, v_norm1_g, v_w_in, v_w_gate_up, v_b_gate, v_gla_norm_g, v_conv_w, v_conv_b, v_conv_norm_g, v_conv_norm_b, v_w_out, v_norm2_g, v_w_mlp_in, v_w_mlp_out, v_final_norm_g):
    x_idx = lax.axis_index("x")
    y_idx = lax.axis_index("y")
    c_idx = lax.axis_index("c")
    me = 4 * x_idx + 2 * y_idx + c_idx
    xs, tgt = x[0], loss_target[0]
    gf = final_norm_g.reshape(1, D_MODEL)
    gmat = _group_matrix()

    small_shard = jnp.zeros((48, 128), F32)
    small_shard = small_shard.at[0:RANK, 0:KEY // N_DEV].set(w_gate_up[0])
    small_shard = small_shard.at[RANK:RANK + CONV_W, 0:CONV // N_DEV].set(conv_w[0])
    g_in, g_small = _gather_two_level([w_in[0].T.astype(BF16), small_shard], "gather_w_in")
    w_in_p = _permute_in(g_in.reshape(D_IN, D_MODEL))
    wg_full = jnp.concatenate([g_small[d, 0:RANK, 0:KEY // N_DEV] for d in range(N_DEV)], axis=1)
    wg_pad = jnp.pad(wg_full, ((0, Z_PAD - RANK), (0, 0))).astype(BF16)
    conv_w_full = jnp.concatenate([g_small[d, RANK:RANK + CONV_W, 0:CONV // N_DEV] for d in range(N_DEV)], axis=1)
    conv_w_pad = jnp.pad(conv_w_full, ((0, HALO - CONV_W), (0, 0)))

    proj, xn = _inproj_fwd(xs, norm1_g, w_in_p)
    mix_a, o, states, (g_out, g_w1) = _gla_fwd(
        proj, wg_pad, b_gate, gla_norm_g, [w_out[0].astype(BF16), w_mlp_in[0].T.astype(BF16)])
    mix_c, uc, (g_w2,) = _conv_fwd(proj, conv_w_pad, conv_b, conv_norm_g, conv_norm_b, gmat,
                                   [w_mlp_out[0].astype(BF16)])
    w_out_full = g_out.reshape(D_MODEL, D_MODEL)
    w1t_full = g_w1.reshape(D_FF, D_MODEL)
    w2_full = g_w2.reshape(D_FF, D_MODEL)
    dh1, dmix, hn, ff, da, dh2, loss, dgf, dg2 = _mlp_fwd_bwd(xs, mix_a, mix_c, tgt, w_out_full, norm2_g, w1t_full,
                                                              w2_full, gf)

    dw1 = _wgrad(hn, da, "wgrad_mlp_in", 1024, 1024, col_block=D_FF // N_DEV)
    dw2 = _wgrad(ff, dh2, "wgrad_mlp_out", 1024, 1024)
    dw_out = _wgrad_out(mix_a, mix_c, dh1)
    dci, dcg, dconv_w, dconv_b, dcn_g, dcn_b = _conv_bwd(proj, uc, dmix, conv_w_pad, conv_norm_g, conv_norm_b, gmat)
    (dq, dk, dv, dg, dz, dwg, dbg, dgn), (p_w1, p_w2, p_out) = _gla_bwd(
        proj, o, states, dmix, wg_pad, b_gate, gla_norm_g,
        [dw1, dw2.reshape(N_DEV, D_FF // N_DEV, D_MODEL), dw_out.reshape(N_DEV, D_MODEL // N_DEV, D_MODEL)])
    dw_in_p = _wgrad_in(xn, [dq, dk, dv, dg, dci, dcg, dz])
    dw_in = _unpermute_in(dw_in_p).reshape(N_DEV, SHARD_IN, D_MODEL)
    dx, dg1, (p_in,) = _inproj_bwd(xs, norm1_g, w_in_p, dh1, dq, dk, dv, dg, dci, dcg, dz, [dw_in])

    small = dict(loss=jnp.zeros((8, 128), F32) + loss, dg1=dg1, dbg=dbg, dgn=dgn, dconv_b=dconv_b, dcn_g=dcn_g,
                 dcn_b=dcn_b, dg2=dg2, dgf=dgf, dwg=dwg[0:RANK], dconv_w=dconv_w[0:CONV_W])
    pack = jnp.concatenate([jnp.pad(small[name].reshape(rows, 128), ((0, _pad8(rows) - rows), (0, 0)))
                            for name, rows in _SMALL], axis=0)
    (g_pack,) = _comm(["gather"], [pack], "gather_small_grads")

    gi, di, mi, vi = _adam_big(p_in, w_in[0].T, m_w_in[0].T, v_w_in[0].T, "adam_w_in")
    go, do, mo, vo = _adam_big(p_out, w_out[0], m_w_out[0], v_w_out[0], "adam_w_out")
    ga, da_, ma, va = _adam_big(p_w1, w_mlp_in[0], m_w_mlp_in[0], v_w_mlp_in[0], "adam_w_mlp_in")
    gb, db, mb, vb = _adam_big(p_w2, w_mlp_out[0], m_w_mlp_out[0], v_w_mlp_out[0], "adam_w_mlp_out")
    cut = lambda a: a.T[None]

    summed = _sum_small(g_pack)
    small_g = {}
    at = 0
    for name, rows in _SMALL:
        small_g[name] = summed[at:at + rows]
        at += _pad8(rows)
    loss_out = small_g["loss"][0, 0]
    wg_cols = KEY // N_DEV
    cw_cols = CONV // N_DEV
    g_small_list = [
        small_g["dg1"].reshape(1, D_MODEL),
        lax.dynamic_slice_in_dim(small_g["dwg"].reshape(RANK, KEY), me * wg_cols, wg_cols, axis=1)[None],
        small_g["dbg"].reshape(1, KEY), small_g["dgn"].reshape(1, DV),
        lax.dynamic_slice_in_dim(small_g["dconv_w"].reshape(CONV_W, CONV), me * cw_cols, cw_cols, axis=1)[None],
        small_g["dconv_b"].reshape(1, CONV), small_g["dcn_g"].reshape(1, CONV), small_g["dcn_b"].reshape(1, CONV),
        small_g["dg2"].reshape(1, D_MODEL), small_g["dgf"].reshape(1, D_MODEL),
    ]
    row = lambda a: a.reshape(1, D_MODEL)
    w_small = [norm1_g, w_gate_up, b_gate, gla_norm_g, conv_w, conv_b, conv_norm_g, conv_norm_b, norm2_g,
               row(final_norm_g)]
    m_small = [m_norm1_g, m_w_gate_up, m_b_gate, m_gla_norm_g, m_conv_w, m_conv_b, m_conv_norm_g, m_conv_norm_b,
               m_norm2_g, row(m_final_norm_g)]
    v_small = [v_norm1_g, v_w_gate_up, v_b_gate, v_gla_norm_g, v_conv_w, v_conv_b, v_conv_norm_g, v_conv_norm_b,
               v_norm2_g, row(v_final_norm_g)]
    d_small, nm_small, nv_small = _adam_small(g_small_list, w_small, m_small, v_small)
    flat = lambda lst: list(lst[:-1]) + [lst[-1].reshape(D_MODEL)]
    g_small_list, d_small, nm_small, nv_small = flat(g_small_list), flat(d_small), flat(nm_small), flat(nv_small)

    def order(s, w_in_v, w_out_v, w1_v, w2_v):
        return [s[0], w_in_v, s[1], s[2], s[3], s[4], s[5], s[6], s[7], w_out_v, s[8], w1_v, w2_v, s[9]]

    grads = order(g_small_list, cut(gi), go[None], ga[None], gb[None])
    deltas = order(d_small, cut(di), do[None], da_[None], db[None])
    new_m = order(nm_small, cut(mi), mo[None], ma[None], mb[None])
    new_v = order(nv_small, cut(vi), vo[None], va[None], vb[None])
    return (loss_out, dx[None], *grads, *deltas, *new_m, *new_v)
```

```python
from typing import NamedTuple

import jax
import jax.numpy as jnp
from jax import lax
from jax.experimental import pallas as pl
from jax.experimental.pallas import tpu as pltpu

F32 = jnp.float32
BF16 = jnp.bfloat16

N_DEV = 8
D_MODEL = 1024
HEADS = 4
DK = 64
DV = 128
KEY = HEADS * DK
VAL = HEADS * DV
RANK = 16
CONV = 512
GROUPS = 8
CONV_W = 31
HALO = 32
SUBLANES = 8
STRIP = 32
D_FF = 4096
D_IN = 2576
SHARD_IN = D_IN // N_DEV
CHUNK = 64
SUB = 256
EPS = 1e-6
GATE_TAU = 16.0
Q_SCALE = DK ** -0.5

P_Q, P_K, P_V, P_G, P_CI, P_CG, P_Z = 0, 256, 512, 1024, 1536, 2048, 2560
D_INP = 2688
Z_PAD = D_INP - P_Z

ADAM_LR = 0.001
ADAM_B1 = 0.9
ADAM_B2 = 0.999
ADAM_EPS = 1e-08
ADAM_WD = 0.01
ADAM_STEP = 10

VMEM_LIMIT = 56 * 1024 * 1024

MESH = pl.DeviceIdType.MESH
ANY = pl.BlockSpec(memory_space=pl.ANY)


def _nn(a, b):
    return jnp.dot(a, b, preferred_element_type=F32)


def _nt(a, b):
    return lax.dot_general(a, b, (((1,), (1,)), ((), ())), preferred_element_type=F32)


def _tn(a, b):
    return lax.dot_general(a, b, (((0,), (0,)), ((), ())), preferred_element_type=F32)


def _params(sem=None):
    return pltpu.CompilerParams(dimension_semantics=sem, vmem_limit_bytes=VMEM_LIMIT)


def _const(shape):
    return pl.BlockSpec(shape, lambda *_: (0,) * len(shape), pipeline_mode=pl.Buffered(1))


def _colsum(v):
    return jnp.sum(v, axis=0, keepdims=True)


def _rowmean(v):
    return jnp.mean(v, axis=-1, keepdims=True)


def _split_bf16(v):
    hi = v.astype(BF16)
    return hi, (v - hi.astype(F32)).astype(BF16)


def _my_place():
    return lax.axis_index("x"), lax.axis_index("y"), lax.axis_index("c")


def _peer(j):
    x, y, c = _my_place()
    jx, jy, jc = (j >> 2) & 1, (j >> 1) & 1, j & 1
    px = 1 - x if jx else x
    py = 1 - y if jy else y
    pc = 1 - c if jc else c
    return (px, py, pc), 4 * px + 2 * py + pc


def _comm_plan(kinds, ins, outs, send_sems, recv_sems, local_sems, receives=True):
    x, y, c = _my_place()
    me = 4 * x + 2 * y + c
    own = lambda k, idx: ins[k] if kinds[k] == "gather" else ins[k].at[idx]
    local = [pltpu.make_async_copy(own(k, me), outs[k].at[me], local_sems.at[k]) for k in range(len(kinds))]
    sends, recvs = [], []
    for j in range(1, N_DEV):
        peer, peer_idx = _peer(j)
        for k in range(len(kinds)):
            sems = dict(send_sem=send_sems.at[k, j - 1], recv_sem=recv_sems.at[k, j - 1], device_id=peer,
                        device_id_type=MESH)
            sends.append(pltpu.make_async_remote_copy(src_ref=own(k, peer_idx), dst_ref=outs[k].at[me], **sems))
            if receives:
                recvs.append(pltpu.make_async_remote_copy(src_ref=own(k, me), dst_ref=outs[k].at[peer_idx], **sems))
    return local, sends, recvs


def _comm_start(plan):
    local, sends, _ = plan
    for cp in local + sends:
        cp.start()


def _comm_wait(plan):
    local, sends, recvs = plan
    for cp in recvs:
        cp.wait_recv()
    for cp in sends:
        cp.wait_send()
    for cp in local:
        cp.wait()


def _comm_scratch(n):
    return [pltpu.SemaphoreType.DMA((n, N_DEV - 1)), pltpu.SemaphoreType.DMA((n, N_DEV - 1)),
            pltpu.SemaphoreType.DMA((n,))]


def _comm_out_shapes(kinds, arrays):
    return [jax.ShapeDtypeStruct(((N_DEV,) + a.shape) if kind == "gather" else a.shape, a.dtype)
            for kind, a in zip(kinds, arrays)]


def _comm(kinds, arrays, name):
    n = len(arrays)

    def body(*refs):
        plan = _comm_plan(kinds, refs[:n], refs[n:2 * n], *refs[2 * n:])
        _comm_start(plan)
        _comm_wait(plan)

    return pl.pallas_call(
        body, name=name, out_shape=_comm_out_shapes(kinds, arrays), in_specs=[ANY] * n, out_specs=[ANY] * n,
        scratch_shapes=_comm_scratch(n),
    )(*arrays)


def _gather_two_level(shards, name):
    n = len(shards)

    def body(*refs):
        ins, outs = refs[:n], refs[n:2 * n]
        send_sems, recv_sems, local_sems = refs[2 * n:]
        x, y, c = _my_place()
        index = lambda px, py, pc: 4 * px + 2 * py + pc
        me, sibling = (x, y, c), (x, y, 1 - c)
        chips = [(1 - x, y), (x, 1 - y), (1 - x, 1 - y)]

        def copy(k, slot, block, to, src=None):
            rows = outs[k].at[index(*block)]
            return pltpu.make_async_remote_copy(
                src_ref=rows if src is None else src, dst_ref=rows, send_sem=send_sems.at[k, slot],
                recv_sem=recv_sems.at[k, slot], device_id=to, device_id_type=MESH)

        local = [pltpu.make_async_copy(ins[k], outs[k].at[index(*me)], local_sems.at[k]) for k in range(n)]
        first = []
        for k in range(n):
            first.append(copy(k, 0, me, sibling, src=ins[k]))
            first += [copy(k, 1 + j, me, (*chip, c), src=ins[k]) for j, chip in enumerate(chips)]
        for cp in local + first:
            cp.start()
        passed = []
        for j, chip in enumerate(chips):
            for k in range(n):
                copy(k, 1 + j, (*chip, c), me).wait_recv()
                cp = copy(k, 4 + j, (*chip, c), sibling)
                cp.start()
                passed.append(cp)
        for k in range(n):
            copy(k, 0, sibling, me).wait_recv()
        for j, chip in enumerate(chips):
            for k in range(n):
                copy(k, 4 + j, (*chip, 1 - c), me).wait_recv()
        for cp in first + passed:
            cp.wait_send()
        for cp in local:
            cp.wait()

    return pl.pallas_call(
        body, name=name, out_shape=_comm_out_shapes(["gather"] * n, shards), in_specs=[ANY] * n, out_specs=[ANY] * n,
        scratch_shapes=_comm_scratch(n),
    )(*shards)


HBM = pl.BlockSpec(memory_space=pltpu.HBM)
SEM = pl.BlockSpec(memory_space=pltpu.SEMAPHORE)
DATAFLOW = pltpu.SideEffectType.DATAFLOW_SIDE_EFFECTING


def _exchange_start(part, land):
    def body(src_ref, land_ref, send_sems, recv_sems, src_thru, land_thru, token):
        x, y, c = _my_place()
        me = 4 * x + 2 * y + c
        for j in range(1, N_DEV):
            peer, peer_idx = _peer(j)
            pltpu.make_async_remote_copy(
                src_ref=src_ref.at[peer_idx], dst_ref=land_ref.at[me], send_sem=send_sems.at[j - 1],
                recv_sem=recv_sems.at[j - 1], device_id=peer, device_id_type=MESH).start()
        token[...] = jnp.zeros_like(token)

    return pl.pallas_call(
        body, name="exchange_w_in_start",
        out_shape=(pltpu.SemaphoreType.DMA((N_DEV - 1,)), pltpu.SemaphoreType.DMA((N_DEV - 1,)),
                   pltpu.HBM(part.shape, part.dtype), pltpu.HBM(land.shape, land.dtype),
                   jax.ShapeDtypeStruct((8, 128), F32)),
        in_specs=(HBM, HBM), out_specs=(SEM, SEM, HBM, HBM, pl.BlockSpec(memory_space=pltpu.VMEM)),
        input_output_aliases={0: 2, 1: 3},
        compiler_params=pltpu.CompilerParams(has_side_effects=DATAFLOW),
    )(pltpu.with_memory_space_constraint(part, pltpu.HBM), pltpu.with_memory_space_constraint(land, pltpu.HBM))


def _exchange_wait(send_sems, recv_sems, part_thru, land_thru, after):
    def body(src_ref, land_ref, send_sems, recv_sems, after_ref, src_dead, got_ref):
        x, y, c = _my_place()
        me = 4 * x + 2 * y + c
        for j in range(1, N_DEV):
            peer, peer_idx = _peer(j)
            sems = dict(send_sem=send_sems.at[j - 1], recv_sem=recv_sems.at[j - 1], device_id=peer,
                        device_id_type=MESH)
            pltpu.make_async_remote_copy(src_ref=src_ref.at[peer_idx], dst_ref=land_ref.at[me], **sems).wait_send()
            pltpu.make_async_remote_copy(src_ref=src_ref.at[me], dst_ref=land_ref.at[peer_idx], **sems).wait_recv()

    return pl.pallas_call(
        body, name="exchange_w_in_wait",
        out_shape=(pltpu.HBM(part_thru.shape, part_thru.dtype), pltpu.HBM(land_thru.shape, land_thru.dtype)),
        in_specs=(HBM, HBM, SEM, SEM, ANY), out_specs=(HBM, HBM), input_output_aliases={0: 0, 1: 1},
        compiler_params=pltpu.CompilerParams(has_side_effects=DATAFLOW),
    )(part_thru, land_thru, send_sems, recv_sems, after)[1]


def _hosted_comm(kinds, n_in, n_out, n_comm, n_steps):
    def plan_of(refs, receives):
        ins = refs[n_in:n_in + n_comm]
        outs = refs[n_in + n_comm + n_out:n_in + 2 * n_comm + n_out]
        return _comm_plan(kinds, ins, outs, *refs[-3:], receives=receives)

    def start(refs):
        @pl.when(pl.program_id(0) == 0)
        def _():
            _comm_start(plan_of(refs, False))

    def wait(refs):
        @pl.when(pl.program_id(0) == n_steps - 1)
        def _():
            _comm_wait(plan_of(refs, True))

    return start, wait


def _inproj_fwd(x, g1, w_in_p):
    T = x.shape[0]
    tm = min(T, 512)

    def body(x_ref, g_ref, w_ref, proj_ref, xn_ref):
        xv = x_ref[...]
        r = lax.rsqrt(_rowmean(xv * xv) + EPS)
        xn = (xv * r * g_ref[...]).astype(BF16)
        xn_ref[...] = xn
        proj_ref[...] = _nt(xn, w_ref[...]).astype(BF16)

    return pl.pallas_call(
        body, name="inproj_fwd", grid=(T // tm,),
        out_shape=[jax.ShapeDtypeStruct((T, D_INP), BF16), jax.ShapeDtypeStruct((T, D_MODEL), BF16)],
        in_specs=[pl.BlockSpec((tm, D_MODEL), lambda i: (i, 0)), _const((1, D_MODEL)), _const((D_INP, D_MODEL))],
        out_specs=[pl.BlockSpec((tm, D_INP), lambda i: (i, 0)), pl.BlockSpec((tm, D_MODEL), lambda i: (i, 0))],
        compiler_params=_params(("arbitrary",)),
    )(x, g1, w_in_p)


def _head_masks():
    lane = lax.broadcasted_iota(jnp.int32, (1, KEY), 1)
    return [((lane >= h * DK) & (lane < (h + 1) * DK)).astype(F32) for h in range(HEADS)]


class _Mats(NamedTuple):
    tri: jax.Array
    tri_t: jax.Array
    same: jax.Array
    mid: jax.Array
    causal: jax.Array
    causal_t: jax.Array
    heads: jax.Array


def _chunk_matrices():
    r = lax.broadcasted_iota(jnp.int32, (SUB, SUB), 0)
    c = lax.broadcasted_iota(jnp.int32, (SUB, SUB), 1)
    shift = CHUNK.bit_length() - 1
    same = jnp.right_shift(r, shift) == jnp.right_shift(c, shift)
    causal = same & (r >= c)
    causal_t = same & (r <= c)
    mid = same & ((c & (CHUNK - 1)) < CHUNK // 2)
    hr = jnp.right_shift(lax.broadcasted_iota(jnp.int32, (VAL, KEY), 0), DV.bit_length() - 1)
    hc = jnp.right_shift(lax.broadcasted_iota(jnp.int32, (VAL, KEY), 1), DK.bit_length() - 1)
    return _Mats(tri=causal.astype(BF16), tri_t=causal_t.astype(BF16), same=same.astype(BF16), mid=mid.astype(BF16),
                 causal=causal, causal_t=causal_t, heads=hr == hc)


class _Decay(NamedTuple):
    al: jax.Array
    q: jax.Array
    k: jax.Array
    eb: jax.Array
    ebm: jax.Array
    emb: jax.Array
    elb: jax.Array
    ebl: jax.Array


def _decay_terms(z, q, k, wg, bg, mats):
    al = _nn(z, wg) + bg
    la = (jnp.minimum(al, 0.0) - jnp.log(1.0 + jnp.exp(-jnp.abs(al)))) * (1.0 / GATE_TAU)
    hi, lo = _split_bf16(la)
    cum = lambda m: _nn(m, hi) + _nn(m, lo)
    b, b_last, b_mid = cum(mats.tri), cum(mats.same), cum(mats.mid)
    return _Decay(al=al, q=q.astype(F32) * Q_SCALE, k=k.astype(F32), eb=jnp.exp(b), ebm=jnp.exp(b - b_mid),
                  emb=jnp.exp(b_mid - b), elb=jnp.exp(b_last - b), ebl=jnp.exp(b_last))


def _gla_fwd(proj, wg, bg, gn, shards):
    T = proj.shape[0]
    tb = min(T, 512)
    cpb = tb // CHUNK
    n_comm = len(shards)
    kinds = ["gather"] * n_comm
    comm_start, comm_wait = _hosted_comm(kinds, 8, 3, n_comm, T // tb)

    def body(*refs):
        q_ref, k_ref, v_ref, g_ref, z_ref, wg_ref, bg_ref, gn_ref = refs[:8]
        mix_ref, o_ref, st_ref = refs[8 + n_comm:11 + n_comm]
        state = refs[11 + 2 * n_comm]
        comm_start(refs)

        @pl.when(pl.program_id(0) == 0)
        def _():
            state[...] = jnp.zeros_like(state)

        mats = _chunk_matrices()
        masks = _head_masks()
        wgv, bgv = wg_ref[...], bg_ref[...]

        for sb in range(tb // SUB):
            rows = slice(sb * SUB, (sb + 1) * SUB)
            d = _decay_terms(z_ref[rows, :], q_ref[rows, :], k_ref[rows, :], wgv, bgv, mats)
            kem_b = (d.k * d.emb).astype(BF16)
            qem = d.q * d.ebm
            for h in range(HEADS):
                cols = slice(h * DV, (h + 1) * DV)
                a = jnp.where(mats.causal, _nt((qem * masks[h]).astype(BF16), kem_b), 0.0)
                o_ref[rows, cols] = _nn(a.astype(BF16), v_ref[rows, cols])
            qe0_b = (d.q * d.eb).astype(BF16)
            kdec_b = (d.k * d.elb).astype(BF16)
            for c in range(SUB // CHUNK):
                loc = slice(c * CHUNK, (c + 1) * CHUNK)
                glob = slice(sb * SUB + c * CHUNK, sb * SUB + (c + 1) * CHUNK)
                st = state[...]
                st_b = st.astype(BF16)
                st_ref[sb * (SUB // CHUNK) + c] = st_b
                o_ref[glob, :] += _nt(qe0_b[loc], st_b)
                u = _tn(v_ref[glob, :], kdec_b[loc])
                state[...] = st * d.ebl[c * CHUNK:c * CHUNK + 1] + jnp.where(mats.heads, u, 0.0)

        gnv = gn_ref[...]
        for h in range(HEADS):
            cols = slice(h * DV, (h + 1) * DV)
            oh = o_ref[:, cols]
            r = lax.rsqrt(_rowmean(oh * oh) + EPS)
            gh = g_ref[:, cols].astype(F32)
            mix_ref[:, cols] = (oh * r * gnv * (gh * jax.nn.sigmoid(gh))).astype(BF16)
        comm_wait(refs)

    nc = T // CHUNK
    res = pl.pallas_call(
        body, name="gla_fwd", grid=(T // tb,),
        out_shape=[jax.ShapeDtypeStruct((T, VAL), BF16), jax.ShapeDtypeStruct((T, VAL), F32),
                   jax.ShapeDtypeStruct((nc, VAL, KEY), BF16)] + _comm_out_shapes(kinds, shards),
        in_specs=[pl.BlockSpec((tb, KEY), lambda i: (i, P_Q // KEY)), pl.BlockSpec((tb, KEY), lambda i: (i, P_K // KEY)),
                  pl.BlockSpec((tb, VAL), lambda i: (i, P_V // VAL)), pl.BlockSpec((tb, VAL), lambda i: (i, P_G // VAL)),
                  pl.BlockSpec((tb, Z_PAD), lambda i: (i, P_Z // Z_PAD)),
                  _const((Z_PAD, KEY)), _const((1, KEY)), _const((1, DV))] + [ANY] * n_comm,
        out_specs=[pl.BlockSpec((tb, VAL), lambda i: (i, 0)), pl.BlockSpec((tb, VAL), lambda i: (i, 0)),
                   pl.BlockSpec((cpb, VAL, KEY), lambda i: (i, 0, 0))] + [ANY] * n_comm,
        scratch_shapes=[pltpu.VMEM((VAL, KEY), F32)] + _comm_scratch(n_comm),
        compiler_params=_params(("arbitrary",)),
    )(proj, proj, proj, proj, proj, wg, bg, gn, *shards)
    return res[0], res[1], res[2], res[3:]


def _group_mean(v, gmat):
    return _nn(v.astype(BF16), gmat)


def _shifted_copies(buf, sh, rows):
    for k in range(1, SUBLANES):
        sh[k - 1] = buf[pl.ds(k, rows), :]


def _tap(buf, sh, off, r0):
    k, base = off % SUBLANES, off - off % SUBLANES
    rows = pl.ds(pl.multiple_of(r0 + base, SUBLANES), STRIP)
    return buf[rows, :] if k == 0 else sh[k - 1, rows, :]


def _conv_fwd(proj, conv_w, conv_b, cn_g, cn_b, gmat, shards):
    T = proj.shape[0]
    tm = min(T, 512)
    sh_rows = tm + HALO - SUBLANES
    n_comm = len(shards)
    kinds = ["gather"] * n_comm
    comm_start, comm_wait = _hosted_comm(kinds, 7, 2, n_comm, T // tm)

    def body(*refs):
        ci_ref, cg_ref, w_ref, b_ref, g_ref, be_ref, gm_ref = refs[:7]
        mix_ref, uc_ref = refs[7 + n_comm:9 + n_comm]
        ubuf, ush = refs[9 + 2 * n_comm:11 + 2 * n_comm]
        comm_start(refs)

        @pl.when(pl.program_id(0) == 0)
        def _():
            ubuf[0:HALO, :] = jnp.zeros((HALO, CONV), F32)

        ubuf[HALO:, :] = ci_ref[...].astype(F32) * jax.nn.sigmoid(cg_ref[...].astype(F32))
        _shifted_copies(ubuf, ush, sh_rows)

        def strip(s, carry):
            r0 = pl.multiple_of(s * STRIP, STRIP)
            acc = jnp.zeros((STRIP, CONV), F32) + b_ref[...]
            for j in range(CONV_W):
                acc = acc + w_ref[j:j + 1, :] * _tap(ubuf, ush, HALO - (CONV_W - 1) + j, r0)
            uc_ref[pl.ds(r0, STRIP), :] = acc
            return carry

        lax.fori_loop(0, tm // STRIP, strip, 0)
        ubuf[0:HALO, :] = ubuf[tm:tm + HALO, :]
        gm = gm_ref[...]
        ucv = uc_ref[...]
        d = ucv - _group_mean(ucv, gm)
        var = _group_mean(d * d, gm)
        yn = d * lax.rsqrt(var + EPS) * g_ref[...] + be_ref[...]
        mix_ref[...] = (yn * jax.nn.sigmoid(yn)).astype(BF16)
        comm_wait(refs)

    res = pl.pallas_call(
        body, name="conv_fwd", grid=(T // tm,),
        out_shape=[jax.ShapeDtypeStruct((T, CONV), BF16), jax.ShapeDtypeStruct((T, CONV), F32)]
        + _comm_out_shapes(kinds, shards),
        in_specs=[pl.BlockSpec((tm, CONV), lambda i: (i, P_CI // CONV)), pl.BlockSpec((tm, CONV), lambda i: (i, P_CG // CONV)),
                  _const((HALO, CONV)), _const((1, CONV)), _const((1, CONV)), _const((1, CONV)), _const((CONV, CONV))]
        + [ANY] * n_comm,
        out_specs=[pl.BlockSpec((tm, CONV), lambda i: (i, 0)), pl.BlockSpec((tm, CONV), lambda i: (i, 0))]
        + [ANY] * n_comm,
        scratch_shapes=[pltpu.VMEM((tm + HALO, CONV), F32), pltpu.VMEM((SUBLANES - 1, sh_rows, CONV), F32)]
        + _comm_scratch(n_comm),
        compiler_params=_params(("arbitrary",)),
    )(proj, proj, conv_w, conv_b, cn_g, cn_b, gmat, *shards)
    return res[0], res[1], res[2:]


def _rms_bwd(dy, xhat, r, g):
    dyg = dy * g
    return r * (dyg - xhat * _rowmean(dyg * xhat))


def _mlp_fwd_bwd(x, mix_a, mix_c, tgt, w_out, g2, w1t, w2, gf):
    T = x.shape[0]
    tm = min(T, 256)
    inv_d = 1.0 / D_MODEL

    def body(x_ref, ma_ref, mc_ref, t_ref, wo_ref, g2_ref, w1_ref, w2_ref, gf_ref,
             dh1_ref, dmix_ref, hn_ref, ff_ref, da_ref, dh2_ref, loss_ref, dgf_ref, dg2_ref):
        @pl.when(pl.program_id(0) == 0)
        def _():
            loss_ref[...] = jnp.zeros_like(loss_ref)
            dgf_ref[...] = jnp.zeros_like(dgf_ref)
            dg2_ref[...] = jnp.zeros_like(dg2_ref)

        g2v, gfv = g2_ref[...], gf_ref[...]
        h1 = x_ref[...] + _nn(ma_ref[...], wo_ref[0:VAL, :]) + _nn(mc_ref[...], wo_ref[VAL:, :])
        r2 = lax.rsqrt(_rowmean(h1 * h1) + EPS)
        h1hat = h1 * r2
        hn = (h1hat * g2v).astype(BF16)
        hn_ref[...] = hn
        relu_a = jnp.maximum(_nt(hn, w1_ref[...]), 0.0)
        ff = (relu_a * relu_a).astype(BF16)
        ff_ref[...] = ff
        h2 = h1 + _nn(ff, w2_ref[...])
        rf = lax.rsqrt(_rowmean(h2 * h2) + EPS)
        h2hat = h2 * rf
        err = h2hat * gfv - t_ref[...]
        loss_ref[...] += (0.5 * inv_d) * _colsum(jnp.sum(err * err, axis=1, keepdims=True))
        dy = err * inv_d
        dgf_ref[...] += _colsum(dy * h2hat)
        dh2 = _rms_bwd(dy, h2hat, rf, gfv)
        dh2_b = dh2.astype(BF16)
        dh2_ref[...] = dh2_b
        da = (_nt(dh2_b, w2_ref[...]) * (2.0 * relu_a)).astype(BF16)
        da_ref[...] = da
        dhn = _nn(da, w1_ref[...])
        dg2_ref[...] += _colsum(dhn * h1hat)
        dh1 = dh2 + _rms_bwd(dhn, h1hat, r2, g2v)
        dh1_ref[...] = dh1
        dmix_ref[...] = _nt(dh1.astype(BF16), wo_ref[...]).astype(BF16)

    tok = lambda w: pl.BlockSpec((tm, w), lambda i: (i, 0))
    return pl.pallas_call(
        body, name="mlp_fwd_bwd", grid=(T // tm,),
        out_shape=[jax.ShapeDtypeStruct((T, D_MODEL), F32), jax.ShapeDtypeStruct((T, D_MODEL), BF16),
                   jax.ShapeDtypeStruct((T, D_MODEL), BF16), jax.ShapeDtypeStruct((T, D_FF), BF16),
                   jax.ShapeDtypeStruct((T, D_FF), BF16), jax.ShapeDtypeStruct((T, D_MODEL), BF16),
                   jax.ShapeDtypeStruct((1, 1), F32), jax.ShapeDtypeStruct((1, D_MODEL), F32),
                   jax.ShapeDtypeStruct((1, D_MODEL), F32)],
        in_specs=[tok(D_MODEL), tok(VAL), tok(CONV), tok(D_MODEL), _const((D_MODEL, D_MODEL)), _const((1, D_MODEL)),
                  _const((D_FF, D_MODEL)), _const((D_FF, D_MODEL)), _const((1, D_MODEL))],
        out_specs=[tok(D_MODEL), tok(D_MODEL), tok(D_MODEL), tok(D_FF), tok(D_FF), tok(D_MODEL),
                   pl.BlockSpec((1, 1), lambda i: (0, 0)), pl.BlockSpec((1, D_MODEL), lambda i: (0, 0)),
                   pl.BlockSpec((1, D_MODEL), lambda i: (0, 0))],
        compiler_params=_params(("arbitrary",)),
    )(x, mix_a, mix_c, tgt, w_out, g2, w1t, w2, gf)


def _silu_grad(v, s):
    return s * (1.0 + v * (1.0 - s))


def _conv_bwd(proj, uc, dmix, conv_w, cn_g, cn_b, gmat):
    T = proj.shape[0]
    tm = min(T, 512)
    nt = T // tm
    sh_rows = tm + HALO - SUBLANES
    n_strips = tm // STRIP

    def body(ci_ref, cg_ref, uc_ref, dm_ref, w_ref, g_ref, be_ref, gm_ref,
             dci_ref, dcg_ref, dw_ref, db_ref, dg_ref, dbe_ref, dbuf, dsh, dwacc):
        step = pl.program_id(0)

        @pl.when(step == 0)
        def _():
            dbuf[tm:, :] = jnp.zeros((HALO, CONV), F32)
            dwacc[...] = jnp.zeros_like(dwacc)
            db_ref[...] = jnp.zeros_like(db_ref)
            dg_ref[...] = jnp.zeros_like(dg_ref)
            dbe_ref[...] = jnp.zeros_like(dbe_ref)

        gm, gv = gm_ref[...], g_ref[...]
        ucv = uc_ref[...]
        d = ucv - _group_mean(ucv, gm)
        rs = lax.rsqrt(_group_mean(d * d, gm) + EPS)
        yhat = d * rs
        yn = yhat * gv + be_ref[...]
        dyn = dm_ref[...].astype(F32) * _silu_grad(yn, jax.nn.sigmoid(yn))
        dg_ref[...] += _colsum(dyn * yhat)
        dbe_ref[...] += _colsum(dyn)
        dyh = dyn * gv
        duc = rs * (dyh - _group_mean(dyh, gm) - yhat * _group_mean(dyh * yhat, gm))
        db_ref[...] += _colsum(duc)
        dbuf[0:tm, :] = duc
        _shifted_copies(dbuf, dsh, sh_rows)

        def strip(s, carry):
            r0 = pl.multiple_of(s * STRIP, STRIP)
            rows = pl.ds(r0, STRIP)
            cin = ci_ref[rows, :].astype(F32)
            sg = jax.nn.sigmoid(cg_ref[rows, :].astype(F32))
            u = cin * sg
            du = jnp.zeros((STRIP, CONV), F32)
            for j in range(CONV_W):
                dj = _tap(dbuf, dsh, CONV_W - 1 - j, r0)
                du = du + w_ref[j:j + 1, :] * dj
                p = u * dj
                fold = p[0:SUBLANES]
                for q in range(1, STRIP // SUBLANES):
                    fold = fold + p[q * SUBLANES:(q + 1) * SUBLANES, :]
                dwacc[j * SUBLANES:(j + 1) * SUBLANES, :] += fold
            dci_ref[rows, :] = (du * sg).astype(BF16)
            dcg_ref[rows, :] = (du * cin * sg * (1.0 - sg)).astype(BF16)
            return carry

        lax.fori_loop(0, n_strips, strip, 0)
        dbuf[tm:, :] = dbuf[0:HALO, :]

        @pl.when(step == nt - 1)
        def _():
            dw_ref[...] = jnp.zeros_like(dw_ref)
            for j in range(CONV_W):
                dw_ref[j:j + 1, :] = _colsum(dwacc[j * SUBLANES:(j + 1) * SUBLANES, :])

    rev = lambda i: nt - 1 - i
    tile = lambda col: pl.BlockSpec((tm, CONV), lambda i: (rev(i), col))
    acc = lambda rows: pl.BlockSpec((rows, CONV), lambda i: (0, 0))
    return pl.pallas_call(
        body, name="conv_bwd", grid=(nt,),
        out_shape=[jax.ShapeDtypeStruct((T, CONV), BF16), jax.ShapeDtypeStruct((T, CONV), BF16),
                   jax.ShapeDtypeStruct((HALO, CONV), F32), jax.ShapeDtypeStruct((1, CONV), F32),
                   jax.ShapeDtypeStruct((1, CONV), F32), jax.ShapeDtypeStruct((1, CONV), F32)],
        in_specs=[tile(P_CI // CONV), tile(P_CG // CONV), tile(0), tile(1),
                  _const((HALO, CONV)), _const((1, CONV)), _const((1, CONV)), _const((CONV, CONV))],
        out_specs=[tile(0), tile(0), acc(HALO), acc(1), acc(1), acc(1)],
        scratch_shapes=[pltpu.VMEM((tm + HALO, CONV), F32), pltpu.VMEM((SUBLANES - 1, sh_rows, CONV), F32),
                        pltpu.VMEM((HALO * SUBLANES, CONV), F32)],
        compiler_params=_params(("arbitrary",)),
    )(proj, proj, uc, dmix, conv_w, cn_g, cn_b, gmat)


def _gla_bwd(proj, o, states, dmix, wg, bg, gn, parts):
    T = proj.shape[0]
    tb = min(T, 512)
    cpb = tb // CHUNK
    nb = T // tb
    n_comm = len(parts)
    kinds = ["exchange"] * n_comm
    comm_start, comm_wait = _hosted_comm(kinds, 11, 8, n_comm, nb)

    def body(*refs):
        q_ref, k_ref, v_ref, g_ref, z_ref, o_ref, st_ref, dm_ref, wg_ref, bg_ref, gn_ref = refs[:11]
        dq_ref, dk_ref, dv_ref, dg_ref, dz_ref, dwg_ref, dbg_ref, dgn_ref = refs[11 + n_comm:19 + n_comm]
        dstate, do_scr, dv_scr = refs[19 + 2 * n_comm:22 + 2 * n_comm]
        comm_start(refs)

        @pl.when(pl.program_id(0) == 0)
        def _():
            dstate[...] = jnp.zeros_like(dstate)
            dwg_ref[...] = jnp.zeros_like(dwg_ref)
            dbg_ref[...] = jnp.zeros_like(dbg_ref)
            dgn_ref[...] = jnp.zeros_like(dgn_ref)

        gnv = gn_ref[...]
        dgn = jnp.zeros((1, DV), F32)
        for h in range(HEADS):
            cols = slice(h * DV, (h + 1) * DV)
            oh = o_ref[:, cols]
            r = lax.rsqrt(_rowmean(oh * oh) + EPS)
            ohat = oh * r
            gh = g_ref[:, cols].astype(F32)
            sg = jax.nn.sigmoid(gh)
            dmx = dm_ref[:, cols].astype(F32)
            don = dmx * (gh * sg)
            dg_ref[:, cols] = (dmx * (ohat * gnv) * _silu_grad(gh, sg)).astype(BF16)
            dgn = dgn + _colsum(don * ohat)
            do_scr[:, cols] = _rms_bwd(don, ohat, r, gnv)
        dgn_ref[...] += dgn

        mats = _chunk_matrices()
        masks = _head_masks()
        wgv, bgv = wg_ref[...], bg_ref[...]
        n_chunks = SUB // CHUNK

        for sb in reversed(range(tb // SUB)):
            rows = slice(sb * SUB, (sb + 1) * SUB)
            zs = z_ref[rows, :]
            d = _decay_terms(zs, q_ref[rows, :], k_ref[rows, :], wgv, bgv, mats)
            qem = d.q * d.ebm
            qem_b = qem.astype(BF16)
            kem_b = (d.k * d.emb).astype(BF16)
            dq = jnp.zeros((SUB, KEY), F32)
            dk = jnp.zeros((SUB, KEY), F32)
            for h in range(HEADS):
                hm = masks[h]
                cols = slice(h * DV, (h + 1) * DV)
                do_b = do_scr[rows, cols].astype(BF16)
                vh = v_ref[rows, cols]
                da = jnp.where(mats.causal, _nt(do_b, vh), 0.0).astype(BF16)
                da_t = jnp.where(mats.causal_t, _nt(vh, do_b), 0.0).astype(BF16)
                a_t = jnp.where(mats.causal_t, _nt(kem_b, (qem * hm).astype(BF16)), 0.0).astype(BF16)
                dq = dq + hm * _nn(da, kem_b)
                dk = dk + hm * _nn(da_t, qem_b)
                dv_scr[rows, cols] = _nn(a_t, do_b)
            dq = dq * d.ebm
            dk = dk * d.emb

            qe0_b = (d.q * d.eb).astype(BF16)
            kdec_b = (d.k * d.elb).astype(BF16)
            dq_st, dk_st, last = [None] * n_chunks, [None] * n_chunks, [None] * n_chunks
            for c in reversed(range(n_chunks)):
                loc = slice(c * CHUNK, (c + 1) * CHUNK)
                glob = slice(sb * SUB + c * CHUNK, sb * SUB + (c + 1) * CHUNK)
                st_b = st_ref[sb * n_chunks + c]
                ds = dstate[...]
                ds_b = ds.astype(BF16)
                do_c = do_scr[glob, :].astype(BF16)
                ebl_c = d.ebl[c * CHUNK:c * CHUNK + 1]
                dk_c = _nn(v_ref[glob, :], ds_b) * d.elb[loc]
                dq_st[c] = _nn(do_c, st_b) * d.eb[loc]
                dk_st[c] = dk_c
                last_c = _colsum(d.k[loc] * dk_c) + ebl_c * _colsum(st_b.astype(F32) * ds)
                last[c] = jnp.broadcast_to(last_c, (CHUNK, KEY))
                dv_ref[glob, :] = (dv_scr[glob, :] + _nt(kdec_b[loc], ds_b)).astype(BF16)
                dstate[...] = ds * ebl_c + jnp.where(mats.heads, _tn(do_c, qe0_b[loc]), 0.0)
            dq = dq + jnp.concatenate(dq_st, axis=0)
            dk = dk + jnp.concatenate(dk_st, axis=0)
            dq_ref[rows, :] = (dq * Q_SCALE).astype(BF16)
            dk_ref[rows, :] = dk.astype(BF16)
            hi, lo = _split_bf16(d.q * dq - d.k * dk)
            dla = _nn(mats.tri_t, hi) + _nn(mats.tri_t, lo) + jnp.concatenate(last, axis=0)
            dal = dla * (1.0 / GATE_TAU) * jax.nn.sigmoid(-d.al)
            dal_b = dal.astype(BF16)
            dz_ref[rows, :] = _nt(dal_b, wgv).astype(BF16)
            dwg_ref[...] += _tn(zs, dal_b)
            dbg_ref[...] += _colsum(dal)
        comm_wait(refs)

    rev = lambda i: nb - 1 - i
    blk = lambda w, col: pl.BlockSpec((tb, w), lambda i: (rev(i), col))
    res = pl.pallas_call(
        body, name="gla_bwd", grid=(nb,),
        out_shape=[jax.ShapeDtypeStruct((T, KEY), BF16), jax.ShapeDtypeStruct((T, KEY), BF16),
                   jax.ShapeDtypeStruct((T, VAL), BF16), jax.ShapeDtypeStruct((T, VAL), BF16),
                   jax.ShapeDtypeStruct((T, Z_PAD), BF16), jax.ShapeDtypeStruct((Z_PAD, KEY), F32),
                   jax.ShapeDtypeStruct((1, KEY), F32), jax.ShapeDtypeStruct((1, DV), F32)]
        + _comm_out_shapes(kinds, parts),
        in_specs=[blk(KEY, P_Q // KEY), blk(KEY, P_K // KEY), blk(VAL, P_V // VAL), blk(VAL, P_G // VAL),
                  blk(Z_PAD, P_Z // Z_PAD), blk(VAL, 0),
                  pl.BlockSpec((cpb, VAL, KEY), lambda i: (rev(i), 0, 0)), blk(VAL, 0),
                  _const((Z_PAD, KEY)), _const((1, KEY)), _const((1, DV))] + [ANY] * n_comm,
        out_specs=[blk(KEY, 0), blk(KEY, 0), blk(VAL, 0), blk(VAL, 0), blk(Z_PAD, 0),
                   pl.BlockSpec((Z_PAD, KEY), lambda i: (0, 0)), pl.BlockSpec((1, KEY), lambda i: (0, 0)),
                   pl.BlockSpec((1, DV), lambda i: (0, 0))] + [ANY] * n_comm,
        scratch_shapes=[pltpu.VMEM((VAL, KEY), F32), pltpu.VMEM((tb, VAL), F32), pltpu.VMEM((tb, VAL), F32)]
        + _comm_scratch(n_comm),
        compiler_params=_params(("arbitrary",)),
    )(proj, proj, proj, proj, proj, o, states, dmix, wg, bg, gn, *parts)
    return res[:8], res[8:]


def _inproj_bwd(x, g1, w_in_p, dh1, dq, dk, dv, dg, dci, dcg, dz):
    T = x.shape[0]
    tm = min(T, 512)

    def body(x_ref, g_ref, w_ref, dh1_ref, dq_ref, dk_ref, dv_ref, dg_ref, dci_ref, dcg_ref, dz_ref,
             dx_ref, dg1_ref, dp_ref):
        @pl.when(pl.program_id(0) == 0)
        def _():
            dg1_ref[...] = jnp.zeros_like(dg1_ref)

        dp_ref[:, P_Q:P_K] = dq_ref[...]
        dp_ref[:, P_K:P_V] = dk_ref[...]
        dp_ref[:, P_V:P_G] = dv_ref[...]
        dp_ref[:, P_G:P_CI] = dg_ref[...]
        dp_ref[:, P_CI:P_CG] = dci_ref[...]
        dp_ref[:, P_CG:P_Z] = dcg_ref[...]
        dp_ref[:, P_Z:] = dz_ref[...]
        dxn = _nn(dp_ref[...], w_ref[...])
        xv = x_ref[...]
        r = lax.rsqrt(_rowmean(xv * xv) + EPS)
        xhat = xv * r
        dg1_ref[...] += _colsum(dxn * xhat)
        dx_ref[...] = dh1_ref[...] + _rms_bwd(dxn, xhat, r, g_ref[...])

    tok = lambda w: pl.BlockSpec((tm, w), lambda i: (i, 0))
    return pl.pallas_call(
        body, name="inproj_bwd", grid=(T // tm,),
        out_shape=[jax.ShapeDtypeStruct((T, D_MODEL), F32), jax.ShapeDtypeStruct((1, D_MODEL), F32)],
        in_specs=[tok(D_MODEL), _const((1, D_MODEL)), _const((D_INP, D_MODEL)), tok(D_MODEL), tok(KEY), tok(KEY),
                  tok(VAL), tok(VAL), tok(CONV), tok(CONV), tok(Z_PAD)],
        out_specs=[tok(D_MODEL), pl.BlockSpec((1, D_MODEL), lambda i: (0, 0))],
        scratch_shapes=[pltpu.VMEM((tm, D_INP), BF16)],
        compiler_params=_params(("arbitrary",)),
    )(x, g1, w_in_p, dh1, dq, dk, dv, dg, dci, dcg, dz)


def _wgrad_in(xn, pieces):
    T = xn.shape[0]
    tt = min(T, 1024)
    nt = T // tt
    widths = [p.shape[1] for p in pieces]
    assert sum(widths) == D_INP

    def body(*refs):
        xn_ref, piece_refs, o_ref, acc = refs[0], refs[1:1 + len(pieces)], refs[-2], refs[-1]

        @pl.when(pl.program_id(0) == 0)
        def _():
            acc[...] = jnp.zeros_like(acc)

        xv = xn_ref[...]
        row = 0
        for ref, w in zip(piece_refs, widths):
            acc[row:row + w, :] += _tn(ref[...], xv)
            row += w

        @pl.when(pl.program_id(0) == nt - 1)
        def _():
            o_ref[...] = acc[...].astype(BF16)

    tok = lambda w: pl.BlockSpec((tt, w), lambda t: (t, 0))
    return pl.pallas_call(
        body, name="wgrad_in", grid=(nt,), out_shape=jax.ShapeDtypeStruct((D_INP, D_MODEL), BF16),
        in_specs=[tok(D_MODEL)] + [tok(w) for w in widths],
        out_specs=pl.BlockSpec((D_INP, D_MODEL), lambda t: (0, 0), pipeline_mode=pl.Buffered(1)),
        scratch_shapes=[pltpu.VMEM((D_INP, D_MODEL), F32)],
        compiler_params=_params(("arbitrary",)),
    )(xn, *pieces)


def _wgrad_out(mix_a, mix_c, dh1):
    T = dh1.shape[0]
    tt = min(T, 2048)
    nt = T // tt

    def body(a_ref, c_ref, b_ref, o_ref, acc):
        @pl.when(pl.program_id(0) == 0)
        def _():
            acc[...] = jnp.zeros_like(acc)

        b = b_ref[...].astype(BF16)
        acc[0:VAL, :] += _tn(a_ref[...], b)
        acc[VAL:, :] += _tn(c_ref[...], b)

        @pl.when(pl.program_id(0) == nt - 1)
        def _():
            o_ref[...] = acc[...].astype(BF16)

    tok = lambda w: pl.BlockSpec((tt, w), lambda t: (t, 0))
    return pl.pallas_call(
        body, name="wgrad_out", grid=(nt,), out_shape=jax.ShapeDtypeStruct((D_MODEL, D_MODEL), BF16),
        in_specs=[tok(VAL), tok(CONV), tok(D_MODEL)],
        out_specs=pl.BlockSpec((D_MODEL, D_MODEL), lambda t: (0, 0)),
        scratch_shapes=[pltpu.VMEM((D_MODEL, D_MODEL), F32)],
        compiler_params=_params(("arbitrary",)),
    )(mix_a, mix_c, dh1)


def _wgrad(a, b, name, tk, tn, col_block=None):
    T, K = a.shape
    N = b.shape[1]
    tt = min(T, 2048)
    nt = T // tt

    def body(a_ref, b_ref, o_ref, acc):
        @pl.when(pl.program_id(2) == 0)
        def _():
            acc[...] = jnp.zeros_like(acc)

        acc[...] += _tn(a_ref[...], b_ref[...].astype(BF16))

        @pl.when(pl.program_id(2) == nt - 1)
        def _():
            if col_block is None:
                o_ref[...] = acc[...].astype(BF16)
            else:
                for q in range(tn // col_block):
                    o_ref[q] = acc[:, q * col_block:(q + 1) * col_block].astype(BF16)

    if col_block is None:
        out_shape = jax.ShapeDtypeStruct((K, N), BF16)
        out_spec = pl.BlockSpec((tk, tn), lambda i, j, t: (i, j))
    else:
        assert tk == K
        out_shape = jax.ShapeDtypeStruct((N // col_block, K, col_block), BF16)
        out_spec = pl.BlockSpec((tn // col_block, tk, col_block), lambda i, j, t: (j, 0, 0))
    return pl.pallas_call(
        body, name=name, grid=(K // tk, N // tn, nt), out_shape=out_shape,
        in_specs=[pl.BlockSpec((tt, tk), lambda i, j, t: (t, i)), pl.BlockSpec((tt, tn), lambda i, j, t: (t, j))],
        out_specs=out_spec, scratch_shapes=[pltpu.VMEM((tk, tn), F32)],
        compiler_params=_params(("arbitrary", "arbitrary", "arbitrary")),
    )(a, b)


def _adam_math(w, g, m, v):
    m = ADAM_B1 * m + (1.0 - ADAM_B1) * g
    v = ADAM_B2 * v + (1.0 - ADAM_B2) * (g * g)
    m_hat = m / (1.0 - ADAM_B1 ** ADAM_STEP)
    v_hat = v / (1.0 - ADAM_B2 ** ADAM_STEP)
    delta = -ADAM_LR * (m_hat / (jnp.sqrt(v_hat) + ADAM_EPS) + ADAM_WD * w)
    return delta, m, v


def _sum8(ref):
    g = ref[0].astype(F32)
    for s in range(1, N_DEV):
        g = g + ref[s].astype(F32)
    return g


def _adam_big(parts, w, m, v, name):
    R, C = w.shape
    tr = 128 if R % 128 == 0 else R

    def body(p_ref, w_ref, m_ref, v_ref, g_ref, d_ref, nm_ref, nv_ref):
        g = _sum8(p_ref)
        g_ref[...] = g
        d_ref[...], nm_ref[...], nv_ref[...] = _adam_math(w_ref[...], g, m_ref[...], v_ref[...])

    row = pl.BlockSpec((tr, C), lambda i: (i, 0))
    return pl.pallas_call(
        body, name=name, grid=(R // tr,), out_shape=[jax.ShapeDtypeStruct((R, C), F32)] * 4,
        in_specs=[pl.BlockSpec((N_DEV, tr, C), lambda i: (0, i, 0)), row, row, row], out_specs=[row] * 4,
        compiler_params=_params(("arbitrary",)),
    )(parts, w, m, v)


def _sum_small(parts):
    def body(p_ref, o_ref):
        o_ref[...] = _sum8(p_ref)

    return pl.pallas_call(body, name="sum_small", out_shape=jax.ShapeDtypeStruct(parts.shape[1:], F32))(parts)


def _adam_small(gs, ws, ms, vs):
    n = len(gs)

    def body(*refs):
        g_refs, w_refs, m_refs, v_refs = refs[:n], refs[n:2 * n], refs[2 * n:3 * n], refs[3 * n:4 * n]
        outs = refs[4 * n:]
        for i in range(n):
            d, nm, nv = _adam_math(w_refs[i][...], g_refs[i][...], m_refs[i][...], v_refs[i][...])
            outs[i][...] = d
            outs[n + i][...] = nm
            outs[2 * n + i][...] = nv

    shapes = [jax.ShapeDtypeStruct(w.shape, F32) for w in ws]
    res = pl.pallas_call(body, name="adam_small", out_shape=shapes * 3)(*gs, *ws, *ms, *vs)
    return res[:n], res[n:2 * n], res[2 * n:]


def _permute_in(wt):
    pad = jnp.zeros((D_INP - D_IN, wt.shape[1]), wt.dtype)
    return jnp.concatenate([wt[:1536], wt[1552:], wt[1536:1552], pad], axis=0)


def _unpermute_in(wt):
    return jnp.concatenate([wt[:P_CI], wt[P_Z:P_Z + RANK], wt[P_CI:P_Z]], axis=0)


def _group_matrix():
    gi = lax.broadcasted_iota(jnp.int32, (CONV, CONV), 0) // (CONV // GROUPS)
    gj = lax.broadcasted_iota(jnp.int32, (CONV, CONV), 1) // (CONV // GROUPS)
    return jnp.where(gi == gj, GROUPS / CONV, 0.0).astype(BF16)


_SMALL = [("loss", 8), ("dg1", 8), ("dbg", 2), ("dgn", 1), ("dconv_b", 4), ("dcn_g", 4), ("dcn_b", 4), ("dg2", 8),
          ("dgf", 8), ("dwg", 32), ("dconv_w", 124)]


def _pad8(rows):
    return -(-rows // 8) * 8


def kernel(x, norm1_g, w_in, w_gate_up, b_gate, gla_norm_g, conv_w, conv_b, conv_norm_g, conv_norm_b, w_out, norm2_g, w_mlp_in, w_mlp_out, final_norm_g, loss_target, m_norm1_g, m_w_in, m_w_gate_up, m_b_gate, m_gla_norm_g, m_conv_w, m_conv_b, m_conv_norm_g, m_conv_norm_b, m_w_out, m_norm2_g, m_w_mlp_in, m_w_mlp_out, m_final_norm_g, v_norm1_g, v_w_in, v_w_gate_up, v_b_gate, v_gla_norm_g, v_conv_w, v_conv_b, v_conv_norm_g, v_conv_norm_b, v_w_out, v_norm2_g, v_w_mlp_in, v_w_mlp_out, v_final_norm_g):
    x_idx = lax.axis_index("x")
    y_idx = lax.axis_index("y")
    c_idx = lax.axis_index("c")
    me = 4 * x_idx + 2 * y_idx + c_idx
    xs, tgt = x[0], loss_target[0]
    gf = final_norm_g.reshape(1, D_MODEL)
    gmat = _group_matrix()

    small_shard = jnp.zeros((48, 128), F32)
    small_shard = small_shard.at[0:RANK, 0:KEY // N_DEV].set(w_gate_up[0])
    small_shard = small_shard.at[RANK:RANK + CONV_W, 0:CONV // N_DEV].set(conv_w[0])
    g_in, g_small = _gather_two_level([w_in[0].T.astype(BF16), small_shard], "gather_w_in")
    w_in_p = _permute_in(g_in.reshape(D_IN, D_MODEL))
    wg_full = jnp.concatenate([g_small[d, 0:RANK, 0:KEY // N_DEV] for d in range(N_DEV)], axis=1)
    wg_pad = jnp.pad(wg_full, ((0, Z_PAD - RANK), (0, 0))).astype(BF16)
    conv_w_full = jnp.concatenate([g_small[d, RANK:RANK + CONV_W, 0:CONV // N_DEV] for d in range(N_DEV)], axis=1)
    conv_w_pad = jnp.pad(conv_w_full, ((0, HALO - CONV_W), (0, 0)))

    proj, xn = _inproj_fwd(xs, norm1_g, w_in_p)
    mix_a, o, states, (g_out, g_w1) = _gla_fwd(
        proj, wg_pad, b_gate, gla_norm_g, [w_out[0].astype(BF16), w_mlp_in[0].T.astype(BF16)])
    mix_c, uc, (g_w2,) = _conv_fwd(proj, conv_w_pad, conv_b, conv_norm_g, conv_norm_b, gmat,
                                   [w_mlp_out[0].astype(BF16)])
    w_out_full = g_out.reshape(D_MODEL, D_MODEL)
    w1t_full = g_w1.reshape(D_FF, D_MODEL)
    w2_full = g_w2.reshape(D_FF, D_MODEL)
    dh1, dmix, hn, ff, da, dh2, loss, dgf, dg2 = _mlp_fwd_bwd(xs, mix_a, mix_c, tgt, w_out_full, norm2_g, w1t_full,
                                                              w2_full, gf)

    dw1 = _wgrad(hn, da, "wgrad_mlp_in", 1024, 1024, col_block=D_FF // N_DEV)
    dw2 = _wgrad(ff, dh2, "wgrad_mlp_out", 1024, 1024)
    dw_out = _wgrad_out(mix_a, mix_c, dh1)
    dci, dcg, dconv_w, dconv_b, dcn_g, dcn_b = _conv_bwd(proj, uc, dmix, conv_w_pad, conv_norm_g, conv_norm_b, gmat)
    (dq, dk, dv, dg, dz, dwg, dbg, dgn), (p_w1, p_w2, p_out) = _gla_bwd(
        proj, o, states, dmix, wg_pad, b_gate, gla_norm_g,
        [dw1, dw2.reshape(N_DEV, D_FF // N_DEV, D_MODEL), dw_out.reshape(N_DEV, D_MODEL // N_DEV, D_MODEL)])
    dw_in_p = _wgrad_in(xn, [dq, dk, dv, dg, dci, dcg, dz])
    dw_in = _unpermute_in(dw_in_p).reshape(N_DEV, SHARD_IN, D_MODEL)
    send_sems, recv_sems, dw_in_thru, land, token = _exchange_start(dw_in, jnp.copy(dw_in))
    dx, dg1 = _inproj_bwd(xs, norm1_g + token[0:1, 0:1], w_in_p, dh1, dq, dk, dv, dg, dci, dcg, dz)
    p_in = _exchange_wait(send_sems, recv_sems, dw_in_thru, land, dg1)

    small = dict(loss=jnp.zeros((8, 128), F32) + loss, dg1=dg1, dbg=dbg, dgn=dgn, dconv_b=dconv_b, dcn_g=dcn_g,
                 dcn_b=dcn_b, dg2=dg2, dgf=dgf, dwg=dwg[0:RANK], dconv_w=dconv_w[0:CONV_W])
    pack = jnp.concatenate([jnp.pad(small[name].reshape(rows, 128), ((0, _pad8(rows) - rows), (0, 0)))
                            for name, rows in _SMALL], axis=0)
    (g_pack,) = _comm(["gather"], [pack], "gather_small_grads")

    gi, di, mi, vi = _adam_big(p_in, w_in[0].T, m_w_in[0].T, v_w_in[0].T, "adam_w_in")
    go, do, mo, vo = _adam_big(p_out, w_out[0], m_w_out[0], v_w_out[0], "adam_w_out")
    ga, da_, ma, va = _adam_big(p_w1, w_mlp_in[0], m_w_mlp_in[0], v_w_mlp_in[0], "adam_w_mlp_in")
    gb, db, mb, vb = _adam_big(p_w2, w_mlp_out[0], m_w_mlp_out[0], v_w_mlp_out[0], "adam_w_mlp_out")
    cut = lambda a: a.T[None]

    summed = _sum_small(g_pack)
    small_g = {}
    at = 0
    for name, rows in _SMALL:
        small_g[name] = summed[at:at + rows]
        at += _pad8(rows)
    loss_out = small_g["loss"][0, 0]
    wg_cols = KEY // N_DEV
    cw_cols = CONV // N_DEV
    g_small_list = [
        small_g["dg1"].reshape(1, D_MODEL),
        lax.dynamic_slice_in_dim(small_g["dwg"].reshape(RANK, KEY), me * wg_cols, wg_cols, axis=1)[None],
        small_g["dbg"].reshape(1, KEY), small_g["dgn"].reshape(1, DV),
        lax.dynamic_slice_in_dim(small_g["dconv_w"].reshape(CONV_W, CONV), me * cw_cols, cw_cols, axis=1)[None],
        small_g["dconv_b"].reshape(1, CONV), small_g["dcn_g"].reshape(1, CONV), small_g["dcn_b"].reshape(1, CONV),
        small_g["dg2"].reshape(1, D_MODEL), small_g["dgf"].reshape(1, D_MODEL),
    ]
    row = lambda a: a.reshape(1, D_MODEL)
    w_small = [norm1_g, w_gate_up, b_gate, gla_norm_g, conv_w, conv_b, conv_norm_g, conv_norm_b, norm2_g,
               row(final_norm_g)]
    m_small = [m_norm1_g, m_w_gate_up, m_b_gate, m_gla_norm_g, m_conv_w, m_conv_b, m_conv_norm_g, m_conv_norm_b,
               m_norm2_g, row(m_final_norm_g)]
    v_small = [v_norm1_g, v_w_gate_up, v_b_gate, v_gla_norm_g, v_conv_w, v_conv_b, v_conv_norm_g, v_conv_norm_b,
               v_norm2_g, row(v_final_norm_g)]
    d_small, nm_small, nv_small = _adam_small(g_small_list, w_small, m_small, v_small)
    flat = lambda lst: list(lst[:-1]) + [lst[-1].reshape(D_MODEL)]
    g_small_list, d_small, nm_small, nv_small = flat(g_small_list), flat(d_small), flat(nm_small), flat(nv_small)

    def order(s, w_in_v, w_out_v, w1_v, w2_v):
        return [s[0], w_in_v, s[1], s[2], s[3], s[4], s[5], s[6], s[7], w_out_v, s[8], w1_v, w2_v, s[9]]

    grads = order(g_small_list, cut(gi), go[None], ga[None], gb[None])
    deltas = order(d_small, cut(di), do[None], da_[None], db[None])
    new_m = order(nm_small, cut(mi), mo[None], ma[None], mb[None])
    new_v = order(nv_small, cut(vi), vo[None], va[None], vb[None])
    return (loss_out, dx[None], *grads, *deltas, *new_m, *new_v)
```

```python
from typing import NamedTuple

import jax
import jax.numpy as jnp
from jax import lax
from jax.experimental import pallas as pl
from jax.experimental.pallas import tpu as pltpu

F32 = jnp.float32
BF16 = jnp.bfloat16

N_DEV = 8
D_MODEL = 1024
HEADS = 4
DK = 64
DV = 128
KEY = HEADS * DK
VAL = HEADS * DV
RANK = 16
CONV = 512
GROUPS = 8
CONV_W = 31
HALO = 32
SUBLANES = 8
STRIP = 32
D_FF = 4096
D_IN = 2576
SHARD_IN = D_IN // N_DEV
CHUNK = 64
SUB = 256
EPS = 1e-6
GATE_TAU = 16.0
Q_SCALE = DK ** -0.5

P_Q, P_K, P_V, P_G, P_CI, P_CG, P_Z = 0, 256, 512, 1024, 1536, 2048, 2560
D_INP = 2688
Z_PAD = D_INP - P_Z
OFF_Z = 1536
OFF_C = OFF_Z + RANK

ADAM_LR = 0.001
ADAM_B1 = 0.9
ADAM_B2 = 0.999
ADAM_EPS = 1e-08
ADAM_WD = 0.01
ADAM_STEP = 10

VMEM_LIMIT = 56 * 1024 * 1024

MESH = pl.DeviceIdType.MESH
ANY = pl.BlockSpec(memory_space=pl.ANY)


def _nn(a, b):
    return jnp.dot(a, b, preferred_element_type=F32)


def _nt(a, b):
    return lax.dot_general(a, b, (((1,), (1,)), ((), ())), preferred_element_type=F32)


def _tn(a, b):
    return lax.dot_general(a, b, (((0,), (0,)), ((), ())), preferred_element_type=F32)


def _params(sem=None):
    return pltpu.CompilerParams(dimension_semantics=sem, vmem_limit_bytes=VMEM_LIMIT)


def _const(shape):
    return pl.BlockSpec(shape, lambda *_: (0,) * len(shape), pipeline_mode=pl.Buffered(1))


def _colsum(v):
    return jnp.sum(v, axis=0, keepdims=True)


def _rowmean(v):
    return jnp.mean(v, axis=-1, keepdims=True)


def _split_bf16(v):
    hi = v.astype(BF16)
    return hi, (v - hi.astype(F32)).astype(BF16)


def _my_place():
    return lax.axis_index("x"), lax.axis_index("y"), lax.axis_index("c")


def _peer(j):
    x, y, c = _my_place()
    jx, jy, jc = (j >> 2) & 1, (j >> 1) & 1, j & 1
    px = 1 - x if jx else x
    py = 1 - y if jy else y
    pc = 1 - c if jc else c
    return (px, py, pc), 4 * px + 2 * py + pc


def _comm_plan(kinds, ins, outs, send_sems, recv_sems, local_sems, receives=True):
    x, y, c = _my_place()
    me = 4 * x + 2 * y + c
    own = lambda k, idx: ins[k] if kinds[k] == "gather" else ins[k].at[idx]
    local = [pltpu.make_async_copy(own(k, me), outs[k].at[me], local_sems.at[k]) for k in range(len(kinds))]
    sends, recvs = [], []
    for j in range(1, N_DEV):
        peer, peer_idx = _peer(j)
        for k in range(len(kinds)):
            sems = dict(send_sem=send_sems.at[k, j - 1], recv_sem=recv_sems.at[k, j - 1], device_id=peer,
                        device_id_type=MESH)
            sends.append(pltpu.make_async_remote_copy(src_ref=own(k, peer_idx), dst_ref=outs[k].at[me], **sems))
            if receives:
                recvs.append(pltpu.make_async_remote_copy(src_ref=own(k, me), dst_ref=outs[k].at[peer_idx], **sems))
    return local, sends, recvs


def _comm_start(plan):
    local, sends, _ = plan
    for cp in local + sends:
        cp.start()


def _comm_wait(plan):
    local, sends, recvs = plan
    for cp in recvs:
        cp.wait_recv()
    for cp in sends:
        cp.wait_send()
    for cp in local:
        cp.wait()


def _comm_scratch(n):
    return [pltpu.SemaphoreType.DMA((n, N_DEV - 1)), pltpu.SemaphoreType.DMA((n, N_DEV - 1)),
            pltpu.SemaphoreType.DMA((n,))]


def _comm_out_shapes(kinds, arrays):
    return [jax.ShapeDtypeStruct(((N_DEV,) + a.shape) if kind == "gather" else a.shape, a.dtype)
            for kind, a in zip(kinds, arrays)]


def _comm(kinds, arrays, name):
    n = len(arrays)

    def body(*refs):
        plan = _comm_plan(kinds, refs[:n], refs[n:2 * n], *refs[2 * n:])
        _comm_start(plan)
        _comm_wait(plan)

    return pl.pallas_call(
        body, name=name, out_shape=_comm_out_shapes(kinds, arrays), in_specs=[ANY] * n, out_specs=[ANY] * n,
        scratch_shapes=_comm_scratch(n),
    )(*arrays)


def _gather_two_level(shards, name):
    n = len(shards)

    def body(*refs):
        ins, outs = refs[:n], refs[n:2 * n]
        send_sems, recv_sems, local_sems = refs[2 * n:]
        x, y, c = _my_place()
        index = lambda px, py, pc: 4 * px + 2 * py + pc
        me, sibling = (x, y, c), (x, y, 1 - c)
        chips = [(1 - x, y), (x, 1 - y), (1 - x, 1 - y)]

        def copy(k, slot, block, to, src=None):
            rows = outs[k].at[index(*block)]
            return pltpu.make_async_remote_copy(
                src_ref=rows if src is None else src, dst_ref=rows, send_sem=send_sems.at[k, slot],
                recv_sem=recv_sems.at[k, slot], device_id=to, device_id_type=MESH)

        local = [pltpu.make_async_copy(ins[k], outs[k].at[index(*me)], local_sems.at[k]) for k in range(n)]
        first = []
        for k in range(n):
            first.append(copy(k, 0, me, sibling, src=ins[k]))
            first += [copy(k, 1 + j, me, (*chip, c), src=ins[k]) for j, chip in enumerate(chips)]
        for cp in local + first:
            cp.start()
        passed = []
        for j, chip in enumerate(chips):
            for k in range(n):
                copy(k, 1 + j, (*chip, c), me).wait_recv()
                cp = copy(k, 4 + j, (*chip, c), sibling)
                cp.start()
                passed.append(cp)
        for k in range(n):
            copy(k, 0, sibling, me).wait_recv()
        for j, chip in enumerate(chips):
            for k in range(n):
                copy(k, 4 + j, (*chip, 1 - c), me).wait_recv()
        for cp in first + passed:
            cp.wait_send()
        for cp in local:
            cp.wait()

    return pl.pallas_call(
        body, name=name, out_shape=_comm_out_shapes(["gather"] * n, shards), in_specs=[ANY] * n, out_specs=[ANY] * n,
        scratch_shapes=_comm_scratch(n),
    )(*shards)


HBM = pl.BlockSpec(memory_space=pltpu.HBM)
SEM = pl.BlockSpec(memory_space=pltpu.SEMAPHORE)
DATAFLOW = pltpu.SideEffectType.DATAFLOW_SIDE_EFFECTING


def _split_start(kind, part, land, name):
    def body(src_ref, land_ref, send_sems, recv_sems, src_thru, land_thru, token):
        x, y, c = _my_place()
        me = 4 * x + 2 * y + c
        for j in range(1, N_DEV):
            peer, peer_idx = _peer(j)
            pltpu.make_async_remote_copy(
                src_ref=src_ref.at[peer_idx] if kind == "exchange" else src_ref, dst_ref=land_ref.at[me],
                send_sem=send_sems.at[j - 1], recv_sem=recv_sems.at[j - 1], device_id=peer,
                device_id_type=MESH).start()
        token[...] = jnp.zeros_like(token)

    return pl.pallas_call(
        body, name=name,
        out_shape=(pltpu.SemaphoreType.DMA((N_DEV - 1,)), pltpu.SemaphoreType.DMA((N_DEV - 1,)),
                   pltpu.HBM(part.shape, part.dtype), pltpu.HBM(land.shape, land.dtype),
                   jax.ShapeDtypeStruct((8, 128), F32)),
        in_specs=(HBM, HBM), out_specs=(SEM, SEM, HBM, HBM, pl.BlockSpec(memory_space=pltpu.VMEM)),
        input_output_aliases={0: 2, 1: 3},
        compiler_params=pltpu.CompilerParams(has_side_effects=DATAFLOW),
    )(pltpu.with_memory_space_constraint(part, pltpu.HBM), pltpu.with_memory_space_constraint(land, pltpu.HBM))


def _split_wait(kind, send_sems, recv_sems, part_thru, land_thru, after, name):
    def body(src_ref, land_ref, send_sems, recv_sems, after_ref, src_dead, got_ref):
        x, y, c = _my_place()
        me = 4 * x + 2 * y + c
        own = lambda idx: src_ref.at[idx] if kind == "exchange" else src_ref
        for j in range(1, N_DEV):
            peer, peer_idx = _peer(j)
            sems = dict(send_sem=send_sems.at[j - 1], recv_sem=recv_sems.at[j - 1], device_id=peer,
                        device_id_type=MESH)
            pltpu.make_async_remote_copy(src_ref=own(peer_idx), dst_ref=land_ref.at[me], **sems).wait_send()
            pltpu.make_async_remote_copy(src_ref=own(me), dst_ref=land_ref.at[peer_idx], **sems).wait_recv()

    return pl.pallas_call(
        body, name=name,
        out_shape=(pltpu.HBM(part_thru.shape, part_thru.dtype), pltpu.HBM(land_thru.shape, land_thru.dtype)),
        in_specs=(HBM, HBM, SEM, SEM, ANY), out_specs=(HBM, HBM), input_output_aliases={0: 0, 1: 1},
        compiler_params=pltpu.CompilerParams(has_side_effects=DATAFLOW),
    )(part_thru, land_thru, send_sems, recv_sems, after)[1]


def _hosted_comm(kinds, n_in, n_out, n_comm, n_steps):
    def plan_of(refs, receives):
        ins = refs[n_in:n_in + n_comm]
        outs = refs[n_in + n_comm + n_out:n_in + 2 * n_comm + n_out]
        return _comm_plan(kinds, ins, outs, *refs[-3:], receives=receives)

    def start(refs):
        @pl.when(pl.program_id(0) == 0)
        def _():
            _comm_start(plan_of(refs, False))

    def wait(refs):
        @pl.when(pl.program_id(0) == n_steps - 1)
        def _():
            _comm_wait(plan_of(refs, True))

    return start, wait


def _z_lanes():
    return lax.broadcasted_iota(jnp.int32, (1, Z_PAD), 1) < RANK


def _inproj_fwd(x, g1, w_in_t):
    T = x.shape[0]
    tm = min(T, 512)

    def body(x_ref, g_ref, w_ref, proj_ref, xn_ref):
        xv = x_ref[...]
        r = lax.rsqrt(_rowmean(xv * xv) + EPS)
        xn = (xv * r * g_ref[...]).astype(BF16)
        xn_ref[...] = xn
        proj_ref[:, 0:P_CI] = _nt(xn, w_ref[0:OFF_Z, :]).astype(BF16)
        proj_ref[:, P_CI:P_Z] = _nt(xn, w_ref[OFF_C:D_IN, :]).astype(BF16)
        proj_ref[:, P_Z:] = jnp.where(_z_lanes(), _nt(xn, w_ref[OFF_Z:OFF_Z + Z_PAD, :]), 0.0).astype(BF16)

    return pl.pallas_call(
        body, name="inproj_fwd", grid=(T // tm,),
        out_shape=[jax.ShapeDtypeStruct((T, D_INP), BF16), jax.ShapeDtypeStruct((T, D_MODEL), BF16)],
        in_specs=[pl.BlockSpec((tm, D_MODEL), lambda i: (i, 0)), _const((1, D_MODEL)), _const((D_IN, D_MODEL))],
        out_specs=[pl.BlockSpec((tm, D_INP), lambda i: (i, 0)), pl.BlockSpec((tm, D_MODEL), lambda i: (i, 0))],
        compiler_params=_params(("arbitrary",)),
    )(x, g1, w_in_t)


def _head_masks():
    lane = lax.broadcasted_iota(jnp.int32, (1, KEY), 1)
    return [((lane >= h * DK) & (lane < (h + 1) * DK)).astype(F32) for h in range(HEADS)]


class _Mats(NamedTuple):
    tri: jax.Array
    tri_t: jax.Array
    same: jax.Array
    mid: jax.Array
    causal: jax.Array
    causal_t: jax.Array
    heads: jax.Array


def _chunk_matrices():
    r = lax.broadcasted_iota(jnp.int32, (SUB, SUB), 0)
    c = lax.broadcasted_iota(jnp.int32, (SUB, SUB), 1)
    shift = CHUNK.bit_length() - 1
    same = jnp.right_shift(r, shift) == jnp.right_shift(c, shift)
    causal = same & (r >= c)
    causal_t = same & (r <= c)
    mid = same & ((c & (CHUNK - 1)) < CHUNK // 2)
    hr = jnp.right_shift(lax.broadcasted_iota(jnp.int32, (VAL, KEY), 0), DV.bit_length() - 1)
    hc = jnp.right_shift(lax.broadcasted_iota(jnp.int32, (VAL, KEY), 1), DK.bit_length() - 1)
    return _Mats(tri=causal.astype(BF16), tri_t=causal_t.astype(BF16), same=same.astype(BF16), mid=mid.astype(BF16),
                 causal=causal, causal_t=causal_t, heads=hr == hc)


class _Decay(NamedTuple):
    al: jax.Array
    q: jax.Array
    k: jax.Array
    eb: jax.Array
    ebm: jax.Array
    emb: jax.Array
    elb: jax.Array
    ebl: jax.Array


def _decay_terms(z, q, k, wg, bg, mats):
    al = _nn(z, wg) + bg
    la = (jnp.minimum(al, 0.0) - jnp.log(1.0 + jnp.exp(-jnp.abs(al)))) * (1.0 / GATE_TAU)
    hi, lo = _split_bf16(la)
    cum = lambda m: _nn(m, hi) + _nn(m, lo)
    b, b_last, b_mid = cum(mats.tri), cum(mats.same), cum(mats.mid)
    return _Decay(al=al, q=q.astype(F32) * Q_SCALE, k=k.astype(F32), eb=jnp.exp(b), ebm=jnp.exp(b - b_mid),
                  emb=jnp.exp(b_mid - b), elb=jnp.exp(b_last - b), ebl=jnp.exp(b_last))


def _gla_fwd(proj, wg, bg, gn, shards):
    T = proj.shape[0]
    tb = min(T, 512)
    cpb = tb // CHUNK
    n_comm = len(shards)
    kinds = ["gather"] * n_comm
    comm_start, comm_wait = _hosted_comm(kinds, 8, 3, n_comm, T // tb)

    def body(*refs):
        q_ref, k_ref, v_ref, g_ref, z_ref, wg_ref, bg_ref, gn_ref = refs[:8]
        mix_ref, o_ref, st_ref = refs[8 + n_comm:11 + n_comm]
        state = refs[11 + 2 * n_comm]
        comm_start(refs)

        @pl.when(pl.program_id(0) == 0)
        def _():
            state[...] = jnp.zeros_like(state)

        mats = _chunk_matrices()
        masks = _head_masks()
        wgv, bgv = wg_ref[...], bg_ref[...]

        for sb in range(tb // SUB):
            rows = slice(sb * SUB, (sb + 1) * SUB)
            d = _decay_terms(z_ref[rows, :], q_ref[rows, :], k_ref[rows, :], wgv, bgv, mats)
            kem_b = (d.k * d.emb).astype(BF16)
            qem = d.q * d.ebm
            for h in range(HEADS):
                cols = slice(h * DV, (h + 1) * DV)
                a = jnp.where(mats.causal, _nt((qem * masks[h]).astype(BF16), kem_b), 0.0)
                o_ref[rows, cols] = _nn(a.astype(BF16), v_ref[rows, cols])
            qe0_b = (d.q * d.eb).astype(BF16)
            kdec_b = (d.k * d.elb).astype(BF16)
            for c in range(SUB // CHUNK):
                loc = slice(c * CHUNK, (c + 1) * CHUNK)
                glob = slice(sb * SUB + c * CHUNK, sb * SUB + (c + 1) * CHUNK)
                st = state[...]
                st_b = st.astype(BF16)
                st_ref[sb * (SUB // CHUNK) + c] = st_b
                o_ref[glob, :] += _nt(qe0_b[loc], st_b)
                u = _tn(v_ref[glob, :], kdec_b[loc])
                state[...] = st * d.ebl[c * CHUNK:c * CHUNK + 1] + jnp.where(mats.heads, u, 0.0)

        gnv = gn_ref[...]
        for h in range(HEADS):
            cols = slice(h * DV, (h + 1) * DV)
            oh = o_ref[:, cols]
            r = lax.rsqrt(_rowmean(oh * oh) + EPS)
            gh = g_ref[:, cols].astype(F32)
            mix_ref[:, cols] = (oh * r * gnv * (gh * jax.nn.sigmoid(gh))).astype(BF16)
        comm_wait(refs)

    nc = T // CHUNK
    res = pl.pallas_call(
        body, name="gla_fwd", grid=(T // tb,),
        out_shape=[jax.ShapeDtypeStruct((T, VAL), BF16), jax.ShapeDtypeStruct((T, VAL), F32),
                   jax.ShapeDtypeStruct((nc, VAL, KEY), BF16)] + _comm_out_shapes(kinds, shards),
        in_specs=[pl.BlockSpec((tb, KEY), lambda i: (i, P_Q // KEY)), pl.BlockSpec((tb, KEY), lambda i: (i, P_K // KEY)),
                  pl.BlockSpec((tb, VAL), lambda i: (i, P_V // VAL)), pl.BlockSpec((tb, VAL), lambda i: (i, P_G // VAL)),
                  pl.BlockSpec((tb, Z_PAD), lambda i: (i, P_Z // Z_PAD)),
                  _const((Z_PAD, KEY)), _const((1, KEY)), _const((1, DV))] + [ANY] * n_comm,
        out_specs=[pl.BlockSpec((tb, VAL), lambda i: (i, 0)), pl.BlockSpec((tb, VAL), lambda i: (i, 0)),
                   pl.BlockSpec((cpb, VAL, KEY), lambda i: (i, 0, 0))] + [ANY] * n_comm,
        scratch_shapes=[pltpu.VMEM((VAL, KEY), F32)] + _comm_scratch(n_comm),
        compiler_params=_params(("arbitrary",)),
    )(proj, proj, proj, proj, proj, wg, bg, gn, *shards)
    return res[0], res[1], res[2], res[3:]


def _group_mean(v, gmat):
    return _nn(v.astype(BF16), gmat)


def _shifted_copies(buf, sh, rows):
    for k in range(1, SUBLANES):
        sh[k - 1] = buf[pl.ds(k, rows), :]


def _tap(buf, sh, off, r0):
    k, base = off % SUBLANES, off - off % SUBLANES
    rows = pl.ds(pl.multiple_of(r0 + base, SUBLANES), STRIP)
    return buf[rows, :] if k == 0 else sh[k - 1, rows, :]


def _conv_fwd(proj, conv_w, conv_b, cn_g, cn_b, gmat, shards):
    T = proj.shape[0]
    tm = min(T, 512)
    sh_rows = tm + HALO - SUBLANES
    n_comm = len(shards)
    kinds = ["gather"] * n_comm
    comm_start, comm_wait = _hosted_comm(kinds, 7, 2, n_comm, T // tm)

    def body(*refs):
        ci_ref, cg_ref, w_ref, b_ref, g_ref, be_ref, gm_ref = refs[:7]
        mix_ref, uc_ref = refs[7 + n_comm:9 + n_comm]
        ubuf, ush = refs[9 + 2 * n_comm:11 + 2 * n_comm]
        comm_start(refs)

        @pl.when(pl.program_id(0) == 0)
        def _():
            ubuf[0:HALO, :] = jnp.zeros((HALO, CONV), F32)

        ubuf[HALO:, :] = ci_ref[...].astype(F32) * jax.nn.sigmoid(cg_ref[...].astype(F32))
        _shifted_copies(ubuf, ush, sh_rows)

        def strip(s, carry):
            r0 = pl.multiple_of(s * STRIP, STRIP)
            acc = jnp.zeros((STRIP, CONV), F32) + b_ref[...]
            for j in range(CONV_W):
                acc = acc + w_ref[j:j + 1, :] * _tap(ubuf, ush, HALO - (CONV_W - 1) + j, r0)
            uc_ref[pl.ds(r0, STRIP), :] = acc
            return carry

        lax.fori_loop(0, tm // STRIP, strip, 0)
        ubuf[0:HALO, :] = ubuf[tm:tm + HALO, :]
        gm = gm_ref[...]
        ucv = uc_ref[...]
        d = ucv - _group_mean(ucv, gm)
        var = _group_mean(d * d, gm)
        yn = d * lax.rsqrt(var + EPS) * g_ref[...] + be_ref[...]
        mix_ref[...] = (yn * jax.nn.sigmoid(yn)).astype(BF16)
        comm_wait(refs)

    res = pl.pallas_call(
        body, name="conv_fwd", grid=(T // tm,),
        out_shape=[jax.ShapeDtypeStruct((T, CONV), BF16), jax.ShapeDtypeStruct((T, CONV), F32)]
        + _comm_out_shapes(kinds, shards),
        in_specs=[pl.BlockSpec((tm, CONV), lambda i: (i, P_CI // CONV)), pl.BlockSpec((tm, CONV), lambda i: (i, P_CG // CONV)),
                  _const((HALO, CONV)), _const((1, CONV)), _const((1, CONV)), _const((1, CONV)), _const((CONV, CONV))]
        + [ANY] * n_comm,
        out_specs=[pl.BlockSpec((tm, CONV), lambda i: (i, 0)), pl.BlockSpec((tm, CONV), lambda i: (i, 0))]
        + [ANY] * n_comm,
        scratch_shapes=[pltpu.VMEM((tm + HALO, CONV), F32), pltpu.VMEM((SUBLANES - 1, sh_rows, CONV), F32)]
        + _comm_scratch(n_comm),
        compiler_params=_params(("arbitrary",)),
    )(proj, proj, conv_w, conv_b, cn_g, cn_b, gmat, *shards)
    return res[0], res[1], res[2:]


def _rms_bwd(dy, xhat, r, g):
    dyg = dy * g
    return r * (dyg - xhat * _rowmean(dyg * xhat))


def _mlp_fwd_bwd(x, mix_a, mix_c, tgt, w_out, g2, w1t, w2, gf):
    T = x.shape[0]
    tm = min(T, 256)
    inv_d = 1.0 / D_MODEL

    def body(x_ref, ma_ref, mc_ref, t_ref, wo_ref, g2_ref, w1_ref, w2_ref, gf_ref,
             dh1_ref, dmix_ref, hn_ref, ff_ref, da_ref, dh2_ref, loss_ref, dgf_ref, dg2_ref):
        @pl.when(pl.program_id(0) == 0)
        def _():
            loss_ref[...] = jnp.zeros_like(loss_ref)
            dgf_ref[...] = jnp.zeros_like(dgf_ref)
            dg2_ref[...] = jnp.zeros_like(dg2_ref)

        g2v, gfv = g2_ref[...], gf_ref[...]
        h1 = x_ref[...] + _nn(ma_ref[...], wo_ref[0:VAL, :]) + _nn(mc_ref[...], wo_ref[VAL:, :])
        r2 = lax.rsqrt(_rowmean(h1 * h1) + EPS)
        h1hat = h1 * r2
        hn = (h1hat * g2v).astype(BF16)
        hn_ref[...] = hn
        relu_a = jnp.maximum(_nt(hn, w1_ref[...]), 0.0)
        ff = (relu_a * relu_a).astype(BF16)
        ff_ref[...] = ff
        h2 = h1 + _nn(ff, w2_ref[...])
        rf = lax.rsqrt(_rowmean(h2 * h2) + EPS)
        h2hat = h2 * rf
        err = h2hat * gfv - t_ref[...]
        loss_ref[...] += (0.5 * inv_d) * _colsum(jnp.sum(err * err, axis=1, keepdims=True))
        dy = err * inv_d
        dgf_ref[...] += _colsum(dy * h2hat)
        dh2 = _rms_bwd(dy, h2hat, rf, gfv)
        dh2_b = dh2.astype(BF16)
        dh2_ref[...] = dh2_b
        da = (_nt(dh2_b, w2_ref[...]) * (2.0 * relu_a)).astype(BF16)
        da_ref[...] = da
        dhn = _nn(da, w1_ref[...])
        dg2_ref[...] += _colsum(dhn * h1hat)
        dh1 = dh2 + _rms_bwd(dhn, h1hat, r2, g2v)
        dh1_ref[...] = dh1
        dmix_ref[...] = _nt(dh1.astype(BF16), wo_ref[...]).astype(BF16)

    tok = lambda w: pl.BlockSpec((tm, w), lambda i: (i, 0))
    return pl.pallas_call(
        body, name="mlp_fwd_bwd", grid=(T // tm,),
        out_shape=[jax.ShapeDtypeStruct((T, D_MODEL), F32), jax.ShapeDtypeStruct((T, D_MODEL), BF16),
                   jax.ShapeDtypeStruct((T, D_MODEL), BF16), jax.ShapeDtypeStruct((T, D_FF), BF16),
                   jax.ShapeDtypeStruct((T, D_FF), BF16), jax.ShapeDtypeStruct((T, D_MODEL), BF16),
                   jax.ShapeDtypeStruct((1, 1), F32), jax.ShapeDtypeStruct((1, D_MODEL), F32),
                   jax.ShapeDtypeStruct((1, D_MODEL), F32)],
        in_specs=[tok(D_MODEL), tok(VAL), tok(CONV), tok(D_MODEL), _const((D_MODEL, D_MODEL)), _const((1, D_MODEL)),
                  _const((D_FF, D_MODEL)), _const((D_FF, D_MODEL)), _const((1, D_MODEL))],
        out_specs=[tok(D_MODEL), tok(D_MODEL), tok(D_MODEL), tok(D_FF), tok(D_FF), tok(D_MODEL),
                   pl.BlockSpec((1, 1), lambda i: (0, 0)), pl.BlockSpec((1, D_MODEL), lambda i: (0, 0)),
                   pl.BlockSpec((1, D_MODEL), lambda i: (0, 0))],
        compiler_params=_params(("arbitrary",)),
    )(x, mix_a, mix_c, tgt, w_out, g2, w1t, w2, gf)


def _silu_grad(v, s):
    return s * (1.0 + v * (1.0 - s))


def _conv_bwd(proj, uc, dmix, conv_w, cn_g, cn_b, gmat):
    T = proj.shape[0]
    tm = min(T, 512)
    nt = T // tm
    sh_rows = tm + HALO - SUBLANES
    n_strips = tm // STRIP

    def body(ci_ref, cg_ref, uc_ref, dm_ref, w_ref, g_ref, be_ref, gm_ref,
             dci_ref, dcg_ref, dw_ref, db_ref, dg_ref, dbe_ref, dbuf, dsh, dwacc):
        step = pl.program_id(0)

        @pl.when(step == 0)
        def _():
            dbuf[tm:, :] = jnp.zeros((HALO, CONV), F32)
            dwacc[...] = jnp.zeros_like(dwacc)
            db_ref[...] = jnp.zeros_like(db_ref)
            dg_ref[...] = jnp.zeros_like(dg_ref)
            dbe_ref[...] = jnp.zeros_like(dbe_ref)

        gm, gv = gm_ref[...], g_ref[...]
        ucv = uc_ref[...]
        d = ucv - _group_mean(ucv, gm)
        rs = lax.rsqrt(_group_mean(d * d, gm) + EPS)
        yhat = d * rs
        yn = yhat * gv + be_ref[...]
        dyn = dm_ref[...].astype(F32) * _silu_grad(yn, jax.nn.sigmoid(yn))
        dg_ref[...] += _colsum(dyn * yhat)
        dbe_ref[...] += _colsum(dyn)
        dyh = dyn * gv
        duc = rs * (dyh - _group_mean(dyh, gm) - yhat * _group_mean(dyh * yhat, gm))
        db_ref[...] += _colsum(duc)
        dbuf[0:tm, :] = duc
        _shifted_copies(dbuf, dsh, sh_rows)

        def strip(s, carry):
            r0 = pl.multiple_of(s * STRIP, STRIP)
            rows = pl.ds(r0, STRIP)
            cin = ci_ref[rows, :].astype(F32)
            sg = jax.nn.sigmoid(cg_ref[rows, :].astype(F32))
            u = cin * sg
            du = jnp.zeros((STRIP, CONV), F32)
            for j in range(CONV_W):
                dj = _tap(dbuf, dsh, CONV_W - 1 - j, r0)
                du = du + w_ref[j:j + 1, :] * dj
                p = u * dj
                fold = p[0:SUBLANES]
                for q in range(1, STRIP // SUBLANES):
                    fold = fold + p[q * SUBLANES:(q + 1) * SUBLANES, :]
                dwacc[j * SUBLANES:(j + 1) * SUBLANES, :] += fold
            dci_ref[rows, :] = (du * sg).astype(BF16)
            dcg_ref[rows, :] = (du * cin * sg * (1.0 - sg)).astype(BF16)
            return carry

        lax.fori_loop(0, n_strips, strip, 0)
        dbuf[tm:, :] = dbuf[0:HALO, :]

        @pl.when(step == nt - 1)
        def _():
            dw_ref[...] = jnp.zeros_like(dw_ref)
            for j in range(CONV_W):
                dw_ref[j:j + 1, :] = _colsum(dwacc[j * SUBLANES:(j + 1) * SUBLANES, :])

    rev = lambda i: nt - 1 - i
    tile = lambda col: pl.BlockSpec((tm, CONV), lambda i: (rev(i), col))
    acc = lambda rows: pl.BlockSpec((rows, CONV), lambda i: (0, 0))
    return pl.pallas_call(
        body, name="conv_bwd", grid=(nt,),
        out_shape=[jax.ShapeDtypeStruct((T, CONV), BF16), jax.ShapeDtypeStruct((T, CONV), BF16),
                   jax.ShapeDtypeStruct((HALO, CONV), F32), jax.ShapeDtypeStruct((1, CONV), F32),
                   jax.ShapeDtypeStruct((1, CONV), F32), jax.ShapeDtypeStruct((1, CONV), F32)],
        in_specs=[tile(P_CI // CONV), tile(P_CG // CONV), tile(0), tile(1),
                  _const((HALO, CONV)), _const((1, CONV)), _const((1, CONV)), _const((CONV, CONV))],
        out_specs=[tile(0), tile(0), acc(HALO), acc(1), acc(1), acc(1)],
        scratch_shapes=[pltpu.VMEM((tm + HALO, CONV), F32), pltpu.VMEM((SUBLANES - 1, sh_rows, CONV), F32),
                        pltpu.VMEM((HALO * SUBLANES, CONV), F32)],
        compiler_params=_params(("arbitrary",)),
    )(proj, proj, uc, dmix, conv_w, cn_g, cn_b, gmat)


def _gla_bwd(proj, o, states, dmix, wg, bg, gn, parts):
    T = proj.shape[0]
    tb = min(T, 512)
    cpb = tb // CHUNK
    nb = T // tb
    n_comm = len(parts)
    kinds = ["exchange"] * n_comm
    comm_start, comm_wait = _hosted_comm(kinds, 11, 8, n_comm, nb)

    def body(*refs):
        q_ref, k_ref, v_ref, g_ref, z_ref, o_ref, st_ref, dm_ref, wg_ref, bg_ref, gn_ref = refs[:11]
        dq_ref, dk_ref, dv_ref, dg_ref, dz_ref, dwg_ref, dbg_ref, dgn_ref = refs[11 + n_comm:19 + n_comm]
        dstate, do_scr, dv_scr = refs[19 + 2 * n_comm:22 + 2 * n_comm]
        comm_start(refs)

        @pl.when(pl.program_id(0) == 0)
        def _():
            dstate[...] = jnp.zeros_like(dstate)
            dwg_ref[...] = jnp.zeros_like(dwg_ref)
            dbg_ref[...] = jnp.zeros_like(dbg_ref)
            dgn_ref[...] = jnp.zeros_like(dgn_ref)

        gnv = gn_ref[...]
        dgn = jnp.zeros((1, DV), F32)
        for h in range(HEADS):
            cols = slice(h * DV, (h + 1) * DV)
            oh = o_ref[:, cols]
            r = lax.rsqrt(_rowmean(oh * oh) + EPS)
            ohat = oh * r
            gh = g_ref[:, cols].astype(F32)
            sg = jax.nn.sigmoid(gh)
            dmx = dm_ref[:, cols].astype(F32)
            don = dmx * (gh * sg)
            dg_ref[:, cols] = (dmx * (ohat * gnv) * _silu_grad(gh, sg)).astype(BF16)
            dgn = dgn + _colsum(don * ohat)
            do_scr[:, cols] = _rms_bwd(don, ohat, r, gnv)
        dgn_ref[...] += dgn

        mats = _chunk_matrices()
        masks = _head_masks()
        wgv, bgv = wg_ref[...], bg_ref[...]
        n_chunks = SUB // CHUNK

        for sb in reversed(range(tb // SUB)):
            rows = slice(sb * SUB, (sb + 1) * SUB)
            zs = z_ref[rows, :]
            d = _decay_terms(zs, q_ref[rows, :], k_ref[rows, :], wgv, bgv, mats)
            qem = d.q * d.ebm
            qem_b = qem.astype(BF16)
            kem_b = (d.k * d.emb).astype(BF16)
            dq = jnp.zeros((SUB, KEY), F32)
            dk = jnp.zeros((SUB, KEY), F32)
            for h in range(HEADS):
                hm = masks[h]
                cols = slice(h * DV, (h + 1) * DV)
                do_b = do_scr[rows, cols].astype(BF16)
                vh = v_ref[rows, cols]
                da = jnp.where(mats.causal, _nt(do_b, vh), 0.0).astype(BF16)
                da_t = jnp.where(mats.causal_t, _nt(vh, do_b), 0.0).astype(BF16)
                a_t = jnp.where(mats.causal_t, _nt(kem_b, (qem * hm).astype(BF16)), 0.0).astype(BF16)
                dq = dq + hm * _nn(da, kem_b)
                dk = dk + hm * _nn(da_t, qem_b)
                dv_scr[rows, cols] = _nn(a_t, do_b)
            dq = dq * d.ebm
            dk = dk * d.emb

            qe0_b = (d.q * d.eb).astype(BF16)
            kdec_b = (d.k * d.elb).astype(BF16)
            dq_st, dk_st, last = [None] * n_chunks, [None] * n_chunks, [None] * n_chunks
            for c in reversed(range(n_chunks)):
                loc = slice(c * CHUNK, (c + 1) * CHUNK)
                glob = slice(sb * SUB + c * CHUNK, sb * SUB + (c + 1) * CHUNK)
                st_b = st_ref[sb * n_chunks + c]
                ds = dstate[...]
                ds_b = ds.astype(BF16)
                do_c = do_scr[glob, :].astype(BF16)
                ebl_c = d.ebl[c * CHUNK:c * CHUNK + 1]
                dk_c = _nn(v_ref[glob, :], ds_b) * d.elb[loc]
                dq_st[c] = _nn(do_c, st_b) * d.eb[loc]
                dk_st[c] = dk_c
                last_c = _colsum(d.k[loc] * dk_c) + ebl_c * _colsum(st_b.astype(F32) * ds)
                last[c] = jnp.broadcast_to(last_c, (CHUNK, KEY))
                dv_ref[glob, :] = (dv_scr[glob, :] + _nt(kdec_b[loc], ds_b)).astype(BF16)
                dstate[...] = ds * ebl_c + jnp.where(mats.heads, _tn(do_c, qe0_b[loc]), 0.0)
            dq = dq + jnp.concatenate(dq_st, axis=0)
            dk = dk + jnp.concatenate(dk_st, axis=0)
            dq_ref[rows, :] = (dq * Q_SCALE).astype(BF16)
            dk_ref[rows, :] = dk.astype(BF16)
            hi, lo = _split_bf16(d.q * dq - d.k * dk)
            dla = _nn(mats.tri_t, hi) + _nn(mats.tri_t, lo) + jnp.concatenate(last, axis=0)
            dal = dla * (1.0 / GATE_TAU) * jax.nn.sigmoid(-d.al)
            dal_b = dal.astype(BF16)
            dz_ref[rows, :] = _nt(dal_b, wgv).astype(BF16)
            dwg_ref[...] += _tn(zs, dal_b)
            dbg_ref[...] += _colsum(dal)
        comm_wait(refs)

    rev = lambda i: nb - 1 - i
    blk = lambda w, col: pl.BlockSpec((tb, w), lambda i: (rev(i), col))
    res = pl.pallas_call(
        body, name="gla_bwd", grid=(nb,),
        out_shape=[jax.ShapeDtypeStruct((T, KEY), BF16), jax.ShapeDtypeStruct((T, KEY), BF16),
                   jax.ShapeDtypeStruct((T, VAL), BF16), jax.ShapeDtypeStruct((T, VAL), BF16),
                   jax.ShapeDtypeStruct((T, Z_PAD), BF16), jax.ShapeDtypeStruct((Z_PAD, KEY), F32),
                   jax.ShapeDtypeStruct((1, KEY), F32), jax.ShapeDtypeStruct((1, DV), F32)]
        + _comm_out_shapes(kinds, parts),
        in_specs=[blk(KEY, P_Q // KEY), blk(KEY, P_K // KEY), blk(VAL, P_V // VAL), blk(VAL, P_G // VAL),
                  blk(Z_PAD, P_Z // Z_PAD), blk(VAL, 0),
                  pl.BlockSpec((cpb, VAL, KEY), lambda i: (rev(i), 0, 0)), blk(VAL, 0),
                  _const((Z_PAD, KEY)), _const((1, KEY)), _const((1, DV))] + [ANY] * n_comm,
        out_specs=[blk(KEY, 0), blk(KEY, 0), blk(VAL, 0), blk(VAL, 0), blk(Z_PAD, 0),
                   pl.BlockSpec((Z_PAD, KEY), lambda i: (0, 0)), pl.BlockSpec((1, KEY), lambda i: (0, 0)),
                   pl.BlockSpec((1, DV), lambda i: (0, 0))] + [ANY] * n_comm,
        scratch_shapes=[pltpu.VMEM((VAL, KEY), F32), pltpu.VMEM((tb, VAL), F32), pltpu.VMEM((tb, VAL), F32)]
        + _comm_scratch(n_comm),
        compiler_params=_params(("arbitrary",)),
    )(proj, proj, proj, proj, proj, o, states, dmix, wg, bg, gn, *parts)
    return res[:8], res[8:]


def _inproj_bwd(x, g1, w_in_t, dh1, dq, dk, dv, dg, dci, dcg, dz):
    T = x.shape[0]
    tm = min(T, 512)

    def body(x_ref, g_ref, w_ref, dh1_ref, dq_ref, dk_ref, dv_ref, dg_ref, dci_ref, dcg_ref, dz_ref,
             dx_ref, dg1_ref, dp_ref):
        @pl.when(pl.program_id(0) == 0)
        def _():
            dg1_ref[...] = jnp.zeros_like(dg1_ref)

        dp_ref[:, P_Q:P_K] = dq_ref[...]
        dp_ref[:, P_K:P_V] = dk_ref[...]
        dp_ref[:, P_V:P_G] = dv_ref[...]
        dp_ref[:, P_G:P_CI] = dg_ref[...]
        dp_ref[:, P_CI:P_CG] = dci_ref[...]
        dp_ref[:, P_CG:P_Z] = dcg_ref[...]
        dp_ref[:, P_Z:] = dz_ref[...]
        dxn = (_nn(dp_ref[:, 0:P_CI], w_ref[0:OFF_Z, :]) + _nn(dp_ref[:, P_CI:P_Z], w_ref[OFF_C:D_IN, :])
               + _nn(dp_ref[:, P_Z:], w_ref[OFF_Z:OFF_Z + Z_PAD, :]))
        xv = x_ref[...]
        r = lax.rsqrt(_rowmean(xv * xv) + EPS)
        xhat = xv * r
        dg1_ref[...] += _colsum(dxn * xhat)
        dx_ref[...] = dh1_ref[...] + _rms_bwd(dxn, xhat, r, g_ref[...])

    tok = lambda w: pl.BlockSpec((tm, w), lambda i: (i, 0))
    return pl.pallas_call(
        body, name="inproj_bwd", grid=(T // tm,),
        out_shape=[jax.ShapeDtypeStruct((T, D_MODEL), F32), jax.ShapeDtypeStruct((1, D_MODEL), F32)],
        in_specs=[tok(D_MODEL), _const((1, D_MODEL)), _const((D_IN, D_MODEL)), tok(D_MODEL), tok(KEY), tok(KEY),
                  tok(VAL), tok(VAL), tok(CONV), tok(CONV), tok(Z_PAD)],
        out_specs=[tok(D_MODEL), pl.BlockSpec((1, D_MODEL), lambda i: (0, 0))],
        scratch_shapes=[pltpu.VMEM((tm, D_INP), BF16)],
        compiler_params=_params(("arbitrary",)),
    )(x, g1, w_in_t, dh1, dq, dk, dv, dg, dci, dcg, dz)


def _wgrad_in(xn, dq, dk, dv, dg, dz, dci, dcg):
    T = xn.shape[0]
    tt = min(T, 1024)
    nt = T // tt
    pieces = [dq, dk, dv, dg, dz, dci, dcg]
    rows = [KEY, KEY, VAL, VAL, RANK, CONV, CONV]
    assert sum(rows) == D_IN

    def body(*refs):
        xn_ref, piece_refs, o_ref, acc = refs[0], refs[1:1 + len(pieces)], refs[-2], refs[-1]

        @pl.when(pl.program_id(0) == 0)
        def _():
            acc[...] = jnp.zeros_like(acc)

        xv = xn_ref[...]
        row = 0
        for ref, n in zip(piece_refs, rows):
            acc[row:row + n, :] += _tn(ref[...], xv)[0:n]
            row += n

        @pl.when(pl.program_id(0) == nt - 1)
        def _():
            o_ref[...] = acc[...].astype(BF16)

    tok = lambda w: pl.BlockSpec((tt, w), lambda t: (t, 0))
    return pl.pallas_call(
        body, name="wgrad_in", grid=(nt,), out_shape=jax.ShapeDtypeStruct((D_IN, D_MODEL), BF16),
        in_specs=[tok(D_MODEL)] + [tok(p.shape[1]) for p in pieces],
        out_specs=pl.BlockSpec((D_IN, D_MODEL), lambda t: (0, 0), pipeline_mode=pl.Buffered(1)),
        scratch_shapes=[pltpu.VMEM((D_IN, D_MODEL), F32)],
        compiler_params=_params(("arbitrary",)),
    )(xn, *pieces)


def _wgrad_out(mix_a, mix_c, dh1):
    T = dh1.shape[0]
    tt = min(T, 2048)
    nt = T // tt

    def body(a_ref, c_ref, b_ref, o_ref, acc):
        @pl.when(pl.program_id(0) == 0)
        def _():
            acc[...] = jnp.zeros_like(acc)

        b = b_ref[...].astype(BF16)
        acc[0:VAL, :] += _tn(a_ref[...], b)
        acc[VAL:, :] += _tn(c_ref[...], b)

        @pl.when(pl.program_id(0) == nt - 1)
        def _():
            o_ref[...] = acc[...].astype(BF16)

    tok = lambda w: pl.BlockSpec((tt, w), lambda t: (t, 0))
    return pl.pallas_call(
        body, name="wgrad_out", grid=(nt,), out_shape=jax.ShapeDtypeStruct((D_MODEL, D_MODEL), BF16),
        in_specs=[tok(VAL), tok(CONV), tok(D_MODEL)],
        out_specs=pl.BlockSpec((D_MODEL, D_MODEL), lambda t: (0, 0)),
        scratch_shapes=[pltpu.VMEM((D_MODEL, D_MODEL), F32)],
        compiler_params=_params(("arbitrary",)),
    )(mix_a, mix_c, dh1)


def _wgrad(a, b, name, tk, tn, col_block=None):
    T, K = a.shape
    N = b.shape[1]
    tt = min(T, 2048)
    nt = T // tt

    def body(a_ref, b_ref, o_ref, acc):
        @pl.when(pl.program_id(2) == 0)
        def _():
            acc[...] = jnp.zeros_like(acc)

        acc[...] += _tn(a_ref[...], b_ref[...].astype(BF16))

        @pl.when(pl.program_id(2) == nt - 1)
        def _():
            if col_block is None:
                o_ref[...] = acc[...].astype(BF16)
            else:
                for q in range(tn // col_block):
                    o_ref[q] = acc[:, q * col_block:(q + 1) * col_block].astype(BF16)

    if col_block is None:
        out_shape = jax.ShapeDtypeStruct((K, N), BF16)
        out_spec = pl.BlockSpec((tk, tn), lambda i, j, t: (i, j))
    else:
        assert tk == K
        out_shape = jax.ShapeDtypeStruct((N // col_block, K, col_block), BF16)
        out_spec = pl.BlockSpec((tn // col_block, tk, col_block), lambda i, j, t: (j, 0, 0))
    return pl.pallas_call(
        body, name=name, grid=(K // tk, N // tn, nt), out_shape=out_shape,
        in_specs=[pl.BlockSpec((tt, tk), lambda i, j, t: (t, i)), pl.BlockSpec((tt, tn), lambda i, j, t: (t, j))],
        out_specs=out_spec, scratch_shapes=[pltpu.VMEM((tk, tn), F32)],
        compiler_params=_params(("arbitrary", "arbitrary", "arbitrary")),
    )(a, b)


def _adam_math(w, g, m, v):
    m = ADAM_B1 * m + (1.0 - ADAM_B1) * g
    v = ADAM_B2 * v + (1.0 - ADAM_B2) * (g * g)
    m_hat = m / (1.0 - ADAM_B1 ** ADAM_STEP)
    v_hat = v / (1.0 - ADAM_B2 ** ADAM_STEP)
    delta = -ADAM_LR * (m_hat / (jnp.sqrt(v_hat) + ADAM_EPS) + ADAM_WD * w)
    return delta, m, v


def _sum8(ref):
    g = ref[0].astype(F32)
    for s in range(1, N_DEV):
        g = g + ref[s].astype(F32)
    return g


def _adam_big(parts, w, m, v, name):
    R, C = w.shape
    tr = 128 if R % 128 == 0 else R

    def body(p_ref, w_ref, m_ref, v_ref, g_ref, d_ref, nm_ref, nv_ref):
        g = _sum8(p_ref)
        g_ref[...] = g
        d_ref[...], nm_ref[...], nv_ref[...] = _adam_math(w_ref[...], g, m_ref[...], v_ref[...])

    row = pl.BlockSpec((tr, C), lambda i: (i, 0))
    return pl.pallas_call(
        body, name=name, grid=(R // tr,), out_shape=[jax.ShapeDtypeStruct((R, C), F32)] * 4,
        in_specs=[pl.BlockSpec((N_DEV, tr, C), lambda i: (0, i, 0)), row, row, row], out_specs=[row] * 4,
        compiler_params=_params(("arbitrary",)),
    )(parts, w, m, v)


def _sum_small(parts):
    def body(p_ref, o_ref):
        o_ref[...] = _sum8(p_ref)

    return pl.pallas_call(body, name="sum_small", out_shape=jax.ShapeDtypeStruct(parts.shape[1:], F32))(parts)


def _adam_small(gs, ws, ms, vs):
    n = len(gs)

    def body(*refs):
        g_refs, w_refs, m_refs, v_refs = refs[:n], refs[n:2 * n], refs[2 * n:3 * n], refs[3 * n:4 * n]
        outs = refs[4 * n:]
        for i in range(n):
            d, nm, nv = _adam_math(w_refs[i][...], g_refs[i][...], m_refs[i][...], v_refs[i][...])
            outs[i][...] = d
            outs[n + i][...] = nm
            outs[2 * n + i][...] = nv

    shapes = [jax.ShapeDtypeStruct(w.shape, F32) for w in ws]
    res = pl.pallas_call(body, name="adam_small", out_shape=shapes * 3)(*gs, *ws, *ms, *vs)
    return res[:n], res[n:2 * n], res[2 * n:]


def _group_matrix():
    gi = lax.broadcasted_iota(jnp.int32, (CONV, CONV), 0) // (CONV // GROUPS)
    gj = lax.broadcasted_iota(jnp.int32, (CONV, CONV), 1) // (CONV // GROUPS)
    return jnp.where(gi == gj, GROUPS / CONV, 0.0).astype(BF16)


_SMALL = [("loss", 8), ("dg1", 8), ("dbg", 2), ("dgn", 1), ("dconv_b", 4), ("dcn_g", 4), ("dcn_b", 4), ("dg2", 8),
          ("dgf", 8), ("dwg", 32), ("dconv_w", 124)]


def _pad8(rows):
    return -(-rows // 8) * 8


def kernel(x, norm1_g, w_in, w_gate_up, b_gate, gla_norm_g, conv_w, conv_b, conv_norm_g, conv_norm_b, w_out, norm2_g, w_mlp_in, w_mlp_out, final_norm_g, loss_target, m_norm1_g, m_w_in, m_w_gate_up, m_b_gate, m_gla_norm_g, m_conv_w, m_conv_b, m_conv_norm_g, m_conv_norm_b, m_w_out, m_norm2_g, m_w_mlp_in, m_w_mlp_out, m_final_norm_g, v_norm1_g, v_w_in, v_w_gate_up, v_b_gate, v_gla_norm_g, v_conv_w, v_conv_b, v_conv_norm_g, v_conv_norm_b, v_w_out, v_norm2_g, v_w_mlp_in, v_w_mlp_out, v_final_norm_g):
    x_idx = lax.axis_index("x")
    y_idx = lax.axis_index("y")
    c_idx = lax.axis_index("c")
    me = 4 * x_idx + 2 * y_idx + c_idx
    xs, tgt = x[0], loss_target[0]
    gf = final_norm_g.reshape(1, D_MODEL)
    gmat = _group_matrix()

    small_shard = jnp.zeros((48, 128), F32)
    small_shard = small_shard.at[0:RANK, 0:KEY // N_DEV].set(w_gate_up[0])
    small_shard = small_shard.at[RANK:RANK + CONV_W, 0:CONV // N_DEV].set(conv_w[0])
    g_in, g_small = _gather_two_level([w_in[0].T.astype(BF16), small_shard], "gather_w_in")
    w_in_t = g_in.reshape(D_IN, D_MODEL)
    wg_full = jnp.concatenate([g_small[d, 0:RANK, 0:KEY // N_DEV] for d in range(N_DEV)], axis=1)
    wg_pad = jnp.pad(wg_full, ((0, Z_PAD - RANK), (0, 0))).astype(BF16)
    conv_w_full = jnp.concatenate([g_small[d, RANK:RANK + CONV_W, 0:CONV // N_DEV] for d in range(N_DEV)], axis=1)
    conv_w_pad = jnp.pad(conv_w_full, ((0, HALO - CONV_W), (0, 0)))

    proj, xn = _inproj_fwd(xs, norm1_g, w_in_t)
    mix_a, o, states, (g_out, g_w1) = _gla_fwd(
        proj, wg_pad, b_gate, gla_norm_g, [w_out[0].astype(BF16), w_mlp_in[0].T.astype(BF16)])
    mix_c, uc, (g_w2,) = _conv_fwd(proj, conv_w_pad, conv_b, conv_norm_g, conv_norm_b, gmat,
                                   [w_mlp_out[0].astype(BF16)])
    w_out_full = g_out.reshape(D_MODEL, D_MODEL)
    w1t_full = g_w1.reshape(D_FF, D_MODEL)
    w2_full = g_w2.reshape(D_FF, D_MODEL)
    dh1, dmix, hn, ff, da, dh2, loss, dgf, dg2 = _mlp_fwd_bwd(xs, mix_a, mix_c, tgt, w_out_full, norm2_g, w1t_full,
                                                              w2_full, gf)

    dw1 = _wgrad(hn, da, "wgrad_mlp_in", 1024, 1024, col_block=D_FF // N_DEV)
    dw2 = _wgrad(ff, dh2, "wgrad_mlp_out", 1024, 1024)
    dw_out = _wgrad_out(mix_a, mix_c, dh1)
    dci, dcg, dconv_w, dconv_b, dcn_g, dcn_b = _conv_bwd(proj, uc, dmix, conv_w_pad, conv_norm_g, conv_norm_b, gmat)
    (dq, dk, dv, dg, dz, dwg, dbg, dgn), (p_w1, p_w2, p_out) = _gla_bwd(
        proj, o, states, dmix, wg_pad, b_gate, gla_norm_g,
        [dw1, dw2.reshape(N_DEV, D_FF // N_DEV, D_MODEL), dw_out.reshape(N_DEV, D_MODEL // N_DEV, D_MODEL)])
    dw_in = _wgrad_in(xn, dq, dk, dv, dg, dz, dci, dcg).reshape(N_DEV, SHARD_IN, D_MODEL)
    send_sems, recv_sems, dw_in_thru, land, token = _split_start("exchange", dw_in, jnp.copy(dw_in),
                                                                 "exchange_w_in_start")
    dx, dg1 = _inproj_bwd(xs, norm1_g + token[0:1, 0:1], w_in_t, dh1, dq, dk, dv, dg, dci, dcg, dz)
    p_in = _split_wait("exchange", send_sems, recv_sems, dw_in_thru, land, dg1, "exchange_w_in_wait")

    small = dict(loss=jnp.zeros((8, 128), F32) + loss, dg1=dg1, dbg=dbg, dgn=dgn, dconv_b=dconv_b, dcn_g=dcn_g,
                 dcn_b=dcn_b, dg2=dg2, dgf=dgf, dwg=dwg[0:RANK], dconv_w=dconv_w[0:CONV_W])
    pack = jnp.concatenate([jnp.pad(small[name].reshape(rows, 128), ((0, _pad8(rows) - rows), (0, 0)))
                            for name, rows in _SMALL], axis=0)
    s_send, s_recv, pack_thru, pack_land, s_token = _split_start(
        "gather", pack, jnp.broadcast_to(pack, (N_DEV,) + pack.shape) + 0.0, "gather_small_start")

    gi, di, mi, vi = _adam_big(p_in, w_in[0].T, m_w_in[0].T, v_w_in[0].T, "adam_w_in")
    go, do, mo, vo = _adam_big(p_out, w_out[0] + s_token[0:1, 0:1], m_w_out[0], v_w_out[0], "adam_w_out")
    ga, da_, ma, va = _adam_big(p_w1, w_mlp_in[0], m_w_mlp_in[0], v_w_mlp_in[0], "adam_w_mlp_in")
    gb, db, mb, vb = _adam_big(p_w2, w_mlp_out[0], m_w_mlp_out[0], v_w_mlp_out[0], "adam_w_mlp_out")
    cut = lambda a: a.T[None]

    g_pack = _split_wait("gather", s_send, s_recv, pack_thru, pack_land, go[0:8, 0:128] + ga[0:8, 0:128]
                         + gb[0:8, 0:128], "gather_small_wait")
    summed = _sum_small(g_pack)
    small_g = {}
    at = 0
    for name, rows in _SMALL:
        small_g[name] = summed[at:at + rows]
        at += _pad8(rows)
    loss_out = small_g["loss"][0, 0]
    wg_cols = KEY // N_DEV
    cw_cols = CONV // N_DEV
    g_small_list = [
        small_g["dg1"].reshape(1, D_MODEL),
        lax.dynamic_slice_in_dim(small_g["dwg"].reshape(RANK, KEY), me * wg_cols, wg_cols, axis=1)[None],
        small_g["dbg"].reshape(1, KEY), small_g["dgn"].reshape(1, DV),
        lax.dynamic_slice_in_dim(small_g["dconv_w"].reshape(CONV_W, CONV), me * cw_cols, cw_cols, axis=1)[None],
        small_g["dconv_b"].reshape(1, CONV), small_g["dcn_g"].reshape(1, CONV), small_g["dcn_b"].reshape(1, CONV),
        small_g["dg2"].reshape(1, D_MODEL), small_g["dgf"].reshape(1, D_MODEL),
    ]
    row = lambda a: a.reshape(1, D_MODEL)
    w_small = [norm1_g, w_gate_up, b_gate, gla_norm_g, conv_w, conv_b, conv_norm_g, conv_norm_b, norm2_g,
               row(final_norm_g)]
    m_small = [m_norm1_g, m_w_gate_up, m_b_gate, m_gla_norm_g, m_conv_w, m_conv_b, m_conv_norm_g, m_conv_norm_b,
               m_norm2_g, row(m_final_norm_g)]
    v_small = [v_norm1_g, v_w_gate_up, v_b_gate, v_gla_norm_g, v_conv_w, v_conv_b, v_conv_norm_g, v_conv_norm_b,
               v_norm2_g, row(v_final_norm_g)]
    d_small, nm_small, nv_small = _adam_small(g_small_list, w_small, m_small, v_small)
    flat = lambda lst: list(lst[:-1]) + [lst[-1].reshape(D_MODEL)]
    g_small_list, d_small, nm_small, nv_small = flat(g_small_list), flat(d_small), flat(nm_small), flat(nv_small)

    def order(s, w_in_v, w_out_v, w1_v, w2_v):
        return [s[0], w_in_v, s[1], s[2], s[3], s[4], s[5], s[6], s[7], w_out_v, s[8], w1_v, w2_v, s[9]]

    grads = order(g_small_list, cut(gi), go[None], ga[None], gb[None])
    deltas = order(d_small, cut(di), do[None], da_[None], db[None])
    new_m = order(nm_small, cut(mi), mo[None], ma[None], mb[None])
    new_v = order(nv_small, cut(vi), vo[None], va[None], vb[None])
    return (loss_out, dx[None], *grads, *deltas, *new_m, *new_v)
```

```python
from typing import NamedTuple

import jax
import jax.numpy as jnp
from jax import lax
from jax.experimental import pallas as pl
from jax.experimental.pallas import tpu as pltpu

F32 = jnp.float32
BF16 = jnp.bfloat16

N_DEV = 8
D_MODEL = 1024
HEADS = 4
DK = 64
DV = 128
KEY = HEADS * DK
VAL = HEADS * DV
RANK = 16
CONV = 512
GROUPS = 8
CONV_W = 31
HALO = 32
SUBLANES = 8
STRIP = 32
D_FF = 4096
D_IN = 2576
SHARD_IN = D_IN // N_DEV
CHUNK = 64
SUB = 256
EPS = 1e-6
GATE_TAU = 16.0
Q_SCALE = DK ** -0.5

P_Q, P_K, P_V, P_G, P_CI, P_CG, P_Z = 0, 256, 512, 1024, 1536, 2048, 2560
D_INP = 2688
Z_PAD = D_INP - P_Z
OFF_Z = 1536
OFF_C = OFF_Z + RANK

ADAM_LR = 0.001
ADAM_B1 = 0.9
ADAM_B2 = 0.999
ADAM_EPS = 1e-08
ADAM_WD = 0.01
ADAM_STEP = 10

VMEM_LIMIT = 56 * 1024 * 1024

MESH = pl.DeviceIdType.MESH
ANY = pl.BlockSpec(memory_space=pl.ANY)


def _nn(a, b):
    return jnp.dot(a, b, preferred_element_type=F32)


def _nt(a, b):
    return lax.dot_general(a, b, (((1,), (1,)), ((), ())), preferred_element_type=F32)


def _tn(a, b):
    return lax.dot_general(a, b, (((0,), (0,)), ((), ())), preferred_element_type=F32)


def _params(sem=None):
    return pltpu.CompilerParams(dimension_semantics=sem, vmem_limit_bytes=VMEM_LIMIT)


def _const(shape):
    return pl.BlockSpec(shape, lambda *_: (0,) * len(shape), pipeline_mode=pl.Buffered(1))


def _colsum(v):
    return jnp.sum(v, axis=0, keepdims=True)


def _rowmean(v):
    return jnp.mean(v, axis=-1, keepdims=True)


def _split_bf16(v):
    hi = v.astype(BF16)
    return hi, (v - hi.astype(F32)).astype(BF16)


def _my_place():
    return lax.axis_index("x"), lax.axis_index("y"), lax.axis_index("c")


def _peer(j):
    x, y, c = _my_place()
    jx, jy, jc = (j >> 2) & 1, (j >> 1) & 1, j & 1
    px = 1 - x if jx else x
    py = 1 - y if jy else y
    pc = 1 - c if jc else c
    return (px, py, pc), 4 * px + 2 * py + pc


def _comm_plan(kinds, ins, outs, send_sems, recv_sems, local_sems, receives=True):
    x, y, c = _my_place()
    me = 4 * x + 2 * y + c
    own = lambda k, idx: ins[k] if kinds[k] == "gather" else ins[k].at[idx]
    local = [pltpu.make_async_copy(own(k, me), outs[k].at[me], local_sems.at[k]) for k in range(len(kinds))]
    sends, recvs = [], []
    for j in range(1, N_DEV):
        peer, peer_idx = _peer(j)
        for k in range(len(kinds)):
            sems = dict(send_sem=send_sems.at[k, j - 1], recv_sem=recv_sems.at[k, j - 1], device_id=peer,
                        device_id_type=MESH)
            sends.append(pltpu.make_async_remote_copy(src_ref=own(k, peer_idx), dst_ref=outs[k].at[me], **sems))
            if receives:
                recvs.append(pltpu.make_async_remote_copy(src_ref=own(k, me), dst_ref=outs[k].at[peer_idx], **sems))
    return local, sends, recvs


def _comm_start(plan):
    local, sends, _ = plan
    for cp in local + sends:
        cp.start()


def _comm_wait(plan):
    local, sends, recvs = plan
    for cp in recvs:
        cp.wait_recv()
    for cp in sends:
        cp.wait_send()
    for cp in local:
        cp.wait()


def _comm_scratch(n):
    return [pltpu.SemaphoreType.DMA((n, N_DEV - 1)), pltpu.SemaphoreType.DMA((n, N_DEV - 1)),
            pltpu.SemaphoreType.DMA((n,))]


def _comm_out_shapes(kinds, arrays):
    return [jax.ShapeDtypeStruct(((N_DEV,) + a.shape) if kind == "gather" else a.shape, a.dtype)
            for kind, a in zip(kinds, arrays)]


def _comm(kinds, arrays, name):
    n = len(arrays)

    def body(*refs):
        plan = _comm_plan(kinds, refs[:n], refs[n:2 * n], *refs[2 * n:])
        _comm_start(plan)
        _comm_wait(plan)

    return pl.pallas_call(
        body, name=name, out_shape=_comm_out_shapes(kinds, arrays), in_specs=[ANY] * n, out_specs=[ANY] * n,
        scratch_shapes=_comm_scratch(n),
    )(*arrays)


def _gather_two_level(shards, name):
    n = len(shards)

    def body(*refs):
        ins, outs = refs[:n], refs[n:2 * n]
        send_sems, recv_sems, local_sems = refs[2 * n:]
        x, y, c = _my_place()
        index = lambda px, py, pc: 4 * px + 2 * py + pc
        me, sibling = (x, y, c), (x, y, 1 - c)
        chips = [(1 - x, y), (x, 1 - y), (1 - x, 1 - y)]

        def copy(k, slot, block, to, src=None):
            rows = outs[k].at[index(*block)]
            return pltpu.make_async_remote_copy(
                src_ref=rows if src is None else src, dst_ref=rows, send_sem=send_sems.at[k, slot],
                recv_sem=recv_sems.at[k, slot], device_id=to, device_id_type=MESH)

        local = [pltpu.make_async_copy(ins[k], outs[k].at[index(*me)], local_sems.at[k]) for k in range(n)]
        first = []
        for k in range(n):
            first.append(copy(k, 0, me, sibling, src=ins[k]))
            first += [copy(k, 1 + j, me, (*chip, c), src=ins[k]) for j, chip in enumerate(chips)]
        for cp in local + first:
            cp.start()
        passed = []
        for j, chip in enumerate(chips):
            for k in range(n):
                copy(k, 1 + j, (*chip, c), me).wait_recv()
                cp = copy(k, 4 + j, (*chip, c), sibling)
                cp.start()
                passed.append(cp)
        for k in range(n):
            copy(k, 0, sibling, me).wait_recv()
        for j, chip in enumerate(chips):
            for k in range(n):
                copy(k, 4 + j, (*chip, 1 - c), me).wait_recv()
        for cp in first + passed:
            cp.wait_send()
        for cp in local:
            cp.wait()

    return pl.pallas_call(
        body, name=name, out_shape=_comm_out_shapes(["gather"] * n, shards), in_specs=[ANY] * n, out_specs=[ANY] * n,
        scratch_shapes=_comm_scratch(n),
    )(*shards)


HBM = pl.BlockSpec(memory_space=pltpu.HBM)
SEM = pl.BlockSpec(memory_space=pltpu.SEMAPHORE)
DATAFLOW = pltpu.SideEffectType.DATAFLOW_SIDE_EFFECTING


def _split_start(kind, part, land, name):
    def body(src_ref, land_ref, send_sems, recv_sems, src_thru, land_thru, token):
        x, y, c = _my_place()
        me = 4 * x + 2 * y + c
        for j in range(1, N_DEV):
            peer, peer_idx = _peer(j)
            pltpu.make_async_remote_copy(
                src_ref=src_ref.at[peer_idx] if kind == "exchange" else src_ref, dst_ref=land_ref.at[me],
                send_sem=send_sems.at[j - 1], recv_sem=recv_sems.at[j - 1], device_id=peer,
                device_id_type=MESH).start()
        token[...] = jnp.zeros_like(token)

    return pl.pallas_call(
        body, name=name,
        out_shape=(pltpu.SemaphoreType.DMA((N_DEV - 1,)), pltpu.SemaphoreType.DMA((N_DEV - 1,)),
                   pltpu.HBM(part.shape, part.dtype), pltpu.HBM(land.shape, land.dtype),
                   jax.ShapeDtypeStruct((8, 128), F32)),
        in_specs=(HBM, HBM), out_specs=(SEM, SEM, HBM, HBM, pl.BlockSpec(memory_space=pltpu.VMEM)),
        input_output_aliases={0: 2, 1: 3},
        compiler_params=pltpu.CompilerParams(has_side_effects=DATAFLOW),
    )(pltpu.with_memory_space_constraint(part, pltpu.HBM), pltpu.with_memory_space_constraint(land, pltpu.HBM))


def _split_wait(kind, send_sems, recv_sems, part_thru, land_thru, after, name):
    def body(src_ref, land_ref, send_sems, recv_sems, after_ref, src_dead, got_ref):
        x, y, c = _my_place()
        me = 4 * x + 2 * y + c
        own = lambda idx: src_ref.at[idx] if kind == "exchange" else src_ref
        for j in range(1, N_DEV):
            peer, peer_idx = _peer(j)
            sems = dict(send_sem=send_sems.at[j - 1], recv_sem=recv_sems.at[j - 1], device_id=peer,
                        device_id_type=MESH)
            pltpu.make_async_remote_copy(src_ref=own(peer_idx), dst_ref=land_ref.at[me], **sems).wait_send()
            pltpu.make_async_remote_copy(src_ref=own(me), dst_ref=land_ref.at[peer_idx], **sems).wait_recv()

    return pl.pallas_call(
        body, name=name,
        out_shape=(pltpu.HBM(part_thru.shape, part_thru.dtype), pltpu.HBM(land_thru.shape, land_thru.dtype)),
        in_specs=(HBM, HBM, SEM, SEM, ANY), out_specs=(HBM, HBM), input_output_aliases={0: 0, 1: 1},
        compiler_params=pltpu.CompilerParams(has_side_effects=DATAFLOW),
    )(part_thru, land_thru, send_sems, recv_sems, after)[1]


def _hosted_comm(kinds, n_in, n_out, n_comm, n_steps):
    def plan_of(refs, receives):
        ins = refs[n_in:n_in + n_comm]
        outs = refs[n_in + n_comm + n_out:n_in + 2 * n_comm + n_out]
        return _comm_plan(kinds, ins, outs, *refs[-3:], receives=receives)

    def start(refs):
        @pl.when(pl.program_id(0) == 0)
        def _():
            _comm_start(plan_of(refs, False))

    def wait(refs):
        @pl.when(pl.program_id(0) == n_steps - 1)
        def _():
            _comm_wait(plan_of(refs, True))

    return start, wait


def _z_lanes():
    return lax.broadcasted_iota(jnp.int32, (1, Z_PAD), 1) < RANK


def _inproj_fwd(x, g1, w_in_t, shards):
    T = x.shape[0]
    tm = min(T, 512)
    n_comm = len(shards)
    kinds = ["gather"] * n_comm
    comm_start, comm_wait = _hosted_comm(kinds, 3, 2, n_comm, T // tm)

    def body(*refs):
        x_ref, g_ref, w_ref = refs[:3]
        proj_ref, xn_ref = refs[3 + n_comm:5 + n_comm]
        comm_start(refs)
        xv = x_ref[...]
        r = lax.rsqrt(_rowmean(xv * xv) + EPS)
        xn = (xv * r * g_ref[...]).astype(BF16)
        xn_ref[...] = xn
        proj_ref[:, 0:P_CI] = _nt(xn, w_ref[0:OFF_Z, :]).astype(BF16)
        proj_ref[:, P_CI:P_Z] = _nt(xn, w_ref[OFF_C:D_IN, :]).astype(BF16)
        proj_ref[:, P_Z:] = jnp.where(_z_lanes(), _nt(xn, w_ref[OFF_Z:OFF_Z + Z_PAD, :]), 0.0).astype(BF16)
        comm_wait(refs)

    res = pl.pallas_call(
        body, name="inproj_fwd", grid=(T // tm,),
        out_shape=[jax.ShapeDtypeStruct((T, D_INP), BF16), jax.ShapeDtypeStruct((T, D_MODEL), BF16)]
        + _comm_out_shapes(kinds, shards),
        in_specs=[pl.BlockSpec((tm, D_MODEL), lambda i: (i, 0)), _const((1, D_MODEL)), _const((D_IN, D_MODEL))]
        + [ANY] * n_comm,
        out_specs=[pl.BlockSpec((tm, D_INP), lambda i: (i, 0)), pl.BlockSpec((tm, D_MODEL), lambda i: (i, 0))]
        + [ANY] * n_comm,
        scratch_shapes=_comm_scratch(n_comm),
        compiler_params=_params(("arbitrary",)),
    )(x, g1, w_in_t, *shards)
    return res[0], res[1], res[2:]


def _head_masks():
    lane = lax.broadcasted_iota(jnp.int32, (1, KEY), 1)
    return [((lane >= h * DK) & (lane < (h + 1) * DK)).astype(F32) for h in range(HEADS)]


class _Mats(NamedTuple):
    tri: jax.Array
    tri_t: jax.Array
    same: jax.Array
    mid: jax.Array
    causal: jax.Array
    causal_t: jax.Array
    heads: jax.Array


def _chunk_matrices():
    r = lax.broadcasted_iota(jnp.int32, (SUB, SUB), 0)
    c = lax.broadcasted_iota(jnp.int32, (SUB, SUB), 1)
    shift = CHUNK.bit_length() - 1
    same = jnp.right_shift(r, shift) == jnp.right_shift(c, shift)
    causal = same & (r >= c)
    causal_t = same & (r <= c)
    mid = same & ((c & (CHUNK - 1)) < CHUNK // 2)
    hr = jnp.right_shift(lax.broadcasted_iota(jnp.int32, (VAL, KEY), 0), DV.bit_length() - 1)
    hc = jnp.right_shift(lax.broadcasted_iota(jnp.int32, (VAL, KEY), 1), DK.bit_length() - 1)
    return _Mats(tri=causal.astype(BF16), tri_t=causal_t.astype(BF16), same=same.astype(BF16), mid=mid.astype(BF16),
                 causal=causal, causal_t=causal_t, heads=hr == hc)


class _Decay(NamedTuple):
    al: jax.Array
    q: jax.Array
    k: jax.Array
    eb: jax.Array
    ebm: jax.Array
    emb: jax.Array
    elb: jax.Array
    ebl: jax.Array


def _decay_terms(z, q, k, wg, bg, mats):
    al = _nn(z, wg) + bg
    la = (jnp.minimum(al, 0.0) - jnp.log(1.0 + jnp.exp(-jnp.abs(al)))) * (1.0 / GATE_TAU)
    hi, lo = _split_bf16(la)
    cum = lambda m: _nn(m, hi) + _nn(m, lo)
    b, b_last, b_mid = cum(mats.tri), cum(mats.same), cum(mats.mid)
    return _Decay(al=al, q=q.astype(F32) * Q_SCALE, k=k.astype(F32), eb=jnp.exp(b), ebm=jnp.exp(b - b_mid),
                  emb=jnp.exp(b_mid - b), elb=jnp.exp(b_last - b), ebl=jnp.exp(b_last))


def _gla_fwd_tile(q_ref, k_ref, v_ref, g_ref, z_ref, wg_ref, bg_ref, gn_ref, mix_ref, o_ref, st_ref, state, tb):
    @pl.when(pl.program_id(0) == 0)
    def _():
        state[...] = jnp.zeros_like(state)

    mats = _chunk_matrices()
    masks = _head_masks()
    wgv, bgv = wg_ref[...], bg_ref[...]

    for sb in range(tb // SUB):
        rows = slice(sb * SUB, (sb + 1) * SUB)
        d = _decay_terms(z_ref[rows, :], q_ref[rows, :], k_ref[rows, :], wgv, bgv, mats)
        kem_b = (d.k * d.emb).astype(BF16)
        qem = d.q * d.ebm
        for h in range(HEADS):
            cols = slice(h * DV, (h + 1) * DV)
            a = jnp.where(mats.causal, _nt((qem * masks[h]).astype(BF16), kem_b), 0.0)
            o_ref[rows, cols] = _nn(a.astype(BF16), v_ref[rows, cols])
        qe0_b = (d.q * d.eb).astype(BF16)
        kdec_b = (d.k * d.elb).astype(BF16)
        for c in range(SUB // CHUNK):
            loc = slice(c * CHUNK, (c + 1) * CHUNK)
            glob = slice(sb * SUB + c * CHUNK, sb * SUB + (c + 1) * CHUNK)
            st = state[...]
            st_b = st.astype(BF16)
            st_ref[sb * (SUB // CHUNK) + c] = st_b
            o_ref[glob, :] += _nt(qe0_b[loc], st_b)
            u = _tn(v_ref[glob, :], kdec_b[loc])
            state[...] = st * d.ebl[c * CHUNK:c * CHUNK + 1] + jnp.where(mats.heads, u, 0.0)

    gnv = gn_ref[...]
    for h in range(HEADS):
        cols = slice(h * DV, (h + 1) * DV)
        oh = o_ref[:, cols]
        r = lax.rsqrt(_rowmean(oh * oh) + EPS)
        gh = g_ref[:, cols].astype(F32)
        mix_ref[:, cols] = (oh * r * gnv * (gh * jax.nn.sigmoid(gh))).astype(BF16)


def _group_mean(v, gmat):
    return _nn(v.astype(BF16), gmat)


def _shifted_copies(buf, sh, rows):
    for k in range(1, SUBLANES):
        sh[k - 1] = buf[pl.ds(k, rows), :]


def _tap(buf, sh, off, r0):
    k, base = off % SUBLANES, off - off % SUBLANES
    rows = pl.ds(r0 + base if isinstance(r0, int) else pl.multiple_of(r0 + base, SUBLANES), STRIP)
    return buf[rows, :] if k == 0 else sh[k - 1, rows, :]


def _conv_fwd_tile(ci_ref, cg_ref, w_ref, b_ref, g_ref, be_ref, gm_ref, mix_ref, uc_ref, ubuf, ush, tm):
    sh_rows = tm + HALO - SUBLANES

    @pl.when(pl.program_id(0) == 0)
    def _():
        ubuf[0:HALO, :] = jnp.zeros((HALO, CONV), F32)

    ubuf[HALO:, :] = ci_ref[...].astype(F32) * jax.nn.sigmoid(cg_ref[...].astype(F32))
    _shifted_copies(ubuf, ush, sh_rows)
    for s in range(tm // STRIP):
        acc = jnp.zeros((STRIP, CONV), F32) + b_ref[...]
        for j in range(CONV_W):
            acc = acc + w_ref[j:j + 1, :] * _tap(ubuf, ush, HALO - (CONV_W - 1) + j, s * STRIP)
        uc_ref[s * STRIP:(s + 1) * STRIP, :] = acc
    ubuf[0:HALO, :] = ubuf[tm:tm + HALO, :]
    gm = gm_ref[...]
    ucv = uc_ref[...]
    d = ucv - _group_mean(ucv, gm)
    var = _group_mean(d * d, gm)
    yn = d * lax.rsqrt(var + EPS) * g_ref[...] + be_ref[...]
    mix_ref[...] = (yn * jax.nn.sigmoid(yn)).astype(BF16)


def _mix_fwd(proj, wg, bg, gn, conv_w, conv_b, cn_g, cn_b, gmat, shards):
    T = proj.shape[0]
    tb = min(T, 512)
    cpb = tb // CHUNK
    n_comm = len(shards)
    kinds = ["gather"] * n_comm
    comm_start, comm_wait = _hosted_comm(kinds, 15, 5, n_comm, T // tb)

    def body(*refs):
        gla_in, conv_in = refs[:8], refs[8:15]
        gla_out, conv_out = refs[15 + n_comm:18 + n_comm], refs[18 + n_comm:20 + n_comm]
        state, ubuf, ush = refs[20 + 2 * n_comm:23 + 2 * n_comm]
        comm_start(refs)
        _gla_fwd_tile(*gla_in, *gla_out, state, tb)
        _conv_fwd_tile(*conv_in, *conv_out, ubuf, ush, tb)
        comm_wait(refs)

    nc = T // CHUNK
    tok = lambda w, col: pl.BlockSpec((tb, w), lambda i: (i, col))
    res = pl.pallas_call(
        body, name="mix_fwd", grid=(T // tb,),
        out_shape=[jax.ShapeDtypeStruct((T, VAL), BF16), jax.ShapeDtypeStruct((T, VAL), F32),
                   jax.ShapeDtypeStruct((nc, VAL, KEY), BF16), jax.ShapeDtypeStruct((T, CONV), BF16),
                   jax.ShapeDtypeStruct((T, CONV), F32)] + _comm_out_shapes(kinds, shards),
        in_specs=[tok(KEY, P_Q // KEY), tok(KEY, P_K // KEY), tok(VAL, P_V // VAL), tok(VAL, P_G // VAL),
                  tok(Z_PAD, P_Z // Z_PAD), _const((Z_PAD, KEY)), _const((1, KEY)), _const((1, DV)),
                  tok(CONV, P_CI // CONV), tok(CONV, P_CG // CONV), _const((HALO, CONV)), _const((1, CONV)),
                  _const((1, CONV)), _const((1, CONV)), _const((CONV, CONV))] + [ANY] * n_comm,
        out_specs=[tok(VAL, 0), tok(VAL, 0), pl.BlockSpec((cpb, VAL, KEY), lambda i: (i, 0, 0)), tok(CONV, 0),
                   tok(CONV, 0)] + [ANY] * n_comm,
        scratch_shapes=[pltpu.VMEM((VAL, KEY), F32), pltpu.VMEM((tb + HALO, CONV), F32),
                        pltpu.VMEM((SUBLANES - 1, tb + HALO - SUBLANES, CONV), F32)] + _comm_scratch(n_comm),
        compiler_params=_params(("arbitrary",)),
    )(proj, proj, proj, proj, proj, wg, bg, gn, proj, proj, conv_w, conv_b, cn_g, cn_b, gmat, *shards)
    return res[0], res[1], res[2], res[3], res[4], res[5:]


def _rms_bwd(dy, xhat, r, g):
    dyg = dy * g
    return r * (dyg - xhat * _rowmean(dyg * xhat))


def _mlp_fwd_bwd(x, mix_a, mix_c, tgt, w_out, g2, w1t, w2, gf):
    T = x.shape[0]
    tm = min(T, 256)
    inv_d = 1.0 / D_MODEL

    def body(x_ref, ma_ref, mc_ref, t_ref, wo_ref, g2_ref, w1_ref, w2_ref, gf_ref,
             dh1_ref, dmix_ref, hn_ref, ff_ref, da_ref, dh2_ref, loss_ref, dgf_ref, dg2_ref):
        @pl.when(pl.program_id(0) == 0)
        def _():
            loss_ref[...] = jnp.zeros_like(loss_ref)
            dgf_ref[...] = jnp.zeros_like(dgf_ref)
            dg2_ref[...] = jnp.zeros_like(dg2_ref)

        g2v, gfv = g2_ref[...], gf_ref[...]
        h1 = x_ref[...] + _nn(ma_ref[...], wo_ref[0:VAL, :]) + _nn(mc_ref[...], wo_ref[VAL:, :])
        r2 = lax.rsqrt(_rowmean(h1 * h1) + EPS)
        h1hat = h1 * r2
        hn = (h1hat * g2v).astype(BF16)
        hn_ref[...] = hn
        relu_a = jnp.maximum(_nt(hn, w1_ref[...]), 0.0)
        ff = (relu_a * relu_a).astype(BF16)
        ff_ref[...] = ff
        h2 = h1 + _nn(ff, w2_ref[...])
        rf = lax.rsqrt(_rowmean(h2 * h2) + EPS)
        h2hat = h2 * rf
        err = h2hat * gfv - t_ref[...]
        loss_ref[...] += (0.5 * inv_d) * _colsum(jnp.sum(err * err, axis=1, keepdims=True))
        dy = err * inv_d
        dgf_ref[...] += _colsum(dy * h2hat)
        dh2 = _rms_bwd(dy, h2hat, rf, gfv)
        dh2_b = dh2.astype(BF16)
        dh2_ref[...] = dh2_b
        da = (_nt(dh2_b, w2_ref[...]) * (2.0 * relu_a)).astype(BF16)
        da_ref[...] = da
        dhn = _nn(da, w1_ref[...])
        dg2_ref[...] += _colsum(dhn * h1hat)
        dh1 = dh2 + _rms_bwd(dhn, h1hat, r2, g2v)
        dh1_ref[...] = dh1
        dmix_ref[...] = _nt(dh1.astype(BF16), wo_ref[...]).astype(BF16)

    tok = lambda w: pl.BlockSpec((tm, w), lambda i: (i, 0))
    return pl.pallas_call(
        body, name="mlp_fwd_bwd", grid=(T // tm,),
        out_shape=[jax.ShapeDtypeStruct((T, D_MODEL), F32), jax.ShapeDtypeStruct((T, D_MODEL), BF16),
                   jax.ShapeDtypeStruct((T, D_MODEL), BF16), jax.ShapeDtypeStruct((T, D_FF), BF16),
                   jax.ShapeDtypeStruct((T, D_FF), BF16), jax.ShapeDtypeStruct((T, D_MODEL), BF16),
                   jax.ShapeDtypeStruct((1, 1), F32), jax.ShapeDtypeStruct((1, D_MODEL), F32),
                   jax.ShapeDtypeStruct((1, D_MODEL), F32)],
        in_specs=[tok(D_MODEL), tok(VAL), tok(CONV), tok(D_MODEL), _const((D_MODEL, D_MODEL)), _const((1, D_MODEL)),
                  _const((D_FF, D_MODEL)), _const((D_FF, D_MODEL)), _const((1, D_MODEL))],
        out_specs=[tok(D_MODEL), tok(D_MODEL), tok(D_MODEL), tok(D_FF), tok(D_FF), tok(D_MODEL),
                   pl.BlockSpec((1, 1), lambda i: (0, 0)), pl.BlockSpec((1, D_MODEL), lambda i: (0, 0)),
                   pl.BlockSpec((1, D_MODEL), lambda i: (0, 0))],
        compiler_params=_params(("arbitrary",)),
    )(x, mix_a, mix_c, tgt, w_out, g2, w1t, w2, gf)


def _silu_grad(v, s):
    return s * (1.0 + v * (1.0 - s))


def _conv_bwd_tile(ci_ref, cg_ref, uc_ref, dm_ref, w_ref, g_ref, be_ref, gm_ref,
                   dci_ref, dcg_ref, dw_ref, db_ref, dg_ref, dbe_ref, dbuf, dsh, dwacc, tm, nt):
    step = pl.program_id(0)
    sh_rows = tm + HALO - SUBLANES

    @pl.when(step == 0)
    def _():
        dbuf[tm:, :] = jnp.zeros((HALO, CONV), F32)
        dwacc[...] = jnp.zeros_like(dwacc)
        db_ref[...] = jnp.zeros_like(db_ref)
        dg_ref[...] = jnp.zeros_like(dg_ref)
        dbe_ref[...] = jnp.zeros_like(dbe_ref)

    gm, gv = gm_ref[...], g_ref[...]
    ucv = uc_ref[...]
    d = ucv - _group_mean(ucv, gm)
    rs = lax.rsqrt(_group_mean(d * d, gm) + EPS)
    yhat = d * rs
    yn = yhat * gv + be_ref[...]
    dyn = dm_ref[...].astype(F32) * _silu_grad(yn, jax.nn.sigmoid(yn))
    dg_ref[...] += _colsum(dyn * yhat)
    dbe_ref[...] += _colsum(dyn)
    dyh = dyn * gv
    duc = rs * (dyh - _group_mean(dyh, gm) - yhat * _group_mean(dyh * yhat, gm))
    db_ref[...] += _colsum(duc)
    dbuf[0:tm, :] = duc
    _shifted_copies(dbuf, dsh, sh_rows)

    def strip(s, carry):
        r0 = pl.multiple_of(s * STRIP, STRIP)
        rows = pl.ds(r0, STRIP)
        cin = ci_ref[rows, :].astype(F32)
        sg = jax.nn.sigmoid(cg_ref[rows, :].astype(F32))
        u = cin * sg
        du = jnp.zeros((STRIP, CONV), F32)
        for j in range(CONV_W):
            dj = _tap(dbuf, dsh, CONV_W - 1 - j, r0)
            du = du + w_ref[j:j + 1, :] * dj
            p = u * dj
            fold = p[0:SUBLANES]
            for q in range(1, STRIP // SUBLANES):
                fold = fold + p[q * SUBLANES:(q + 1) * SUBLANES, :]
            dwacc[j * SUBLANES:(j + 1) * SUBLANES, :] += fold
        dci_ref[rows, :] = (du * sg).astype(BF16)
        dcg_ref[rows, :] = (du * cin * sg * (1.0 - sg)).astype(BF16)
        return carry

    lax.fori_loop(0, tm // STRIP, strip, 0)
    dbuf[tm:, :] = dbuf[0:HALO, :]

    @pl.when(step == nt - 1)
    def _():
        dw_ref[...] = jnp.zeros_like(dw_ref)
        for j in range(CONV_W):
            dw_ref[j:j + 1, :] = _colsum(dwacc[j * SUBLANES:(j + 1) * SUBLANES, :])


def _conv_bwd(proj, uc, dmix, conv_w, cn_g, cn_b, gmat):
    T = proj.shape[0]
    tm = min(T, 512)
    nt = T // tm
    sh_rows = tm + HALO - SUBLANES

    def body(*refs):
        _conv_bwd_tile(*refs, tm, nt)

    rev = lambda i: nt - 1 - i
    tile = lambda col: pl.BlockSpec((tm, CONV), lambda i: (rev(i), col))
    acc = lambda rows: pl.BlockSpec((rows, CONV), lambda i: (0, 0))
    return pl.pallas_call(
        body, name="conv_bwd", grid=(nt,),
        out_shape=[jax.ShapeDtypeStruct((T, CONV), BF16), jax.ShapeDtypeStruct((T, CONV), BF16),
                   jax.ShapeDtypeStruct((HALO, CONV), F32), jax.ShapeDtypeStruct((1, CONV), F32),
                   jax.ShapeDtypeStruct((1, CONV), F32), jax.ShapeDtypeStruct((1, CONV), F32)],
        in_specs=[tile(P_CI // CONV), tile(P_CG // CONV), tile(0), tile(1),
                  _const((HALO, CONV)), _const((1, CONV)), _const((1, CONV)), _const((CONV, CONV))],
        out_specs=[tile(0), tile(0), acc(HALO), acc(1), acc(1), acc(1)],
        scratch_shapes=[pltpu.VMEM((tm + HALO, CONV), F32), pltpu.VMEM((SUBLANES - 1, sh_rows, CONV), F32),
                        pltpu.VMEM((HALO * SUBLANES, CONV), F32)],
        compiler_params=_params(("arbitrary",)),
    )(proj, proj, uc, dmix, conv_w, cn_g, cn_b, gmat)


def _gla_bwd_tile(q_ref, k_ref, v_ref, g_ref, z_ref, o_ref, st_ref, dm_ref, wg_ref, bg_ref, gn_ref,
                  dq_ref, dk_ref, dv_ref, dg_ref, dz_ref, dwg_ref, dbg_ref, dgn_ref, dstate, do_scr, dv_scr, tb):
    @pl.when(pl.program_id(0) == 0)
    def _():
        dstate[...] = jnp.zeros_like(dstate)
        dwg_ref[...] = jnp.zeros_like(dwg_ref)
        dbg_ref[...] = jnp.zeros_like(dbg_ref)
        dgn_ref[...] = jnp.zeros_like(dgn_ref)

    gnv = gn_ref[...]
    dgn = jnp.zeros((1, DV), F32)
    for h in range(HEADS):
        cols = slice(h * DV, (h + 1) * DV)
        oh = o_ref[:, cols]
        r = lax.rsqrt(_rowmean(oh * oh) + EPS)
        ohat = oh * r
        gh = g_ref[:, cols].astype(F32)
        sg = jax.nn.sigmoid(gh)
        dmx = dm_ref[:, cols].astype(F32)
        don = dmx * (gh * sg)
        dg_ref[:, cols] = (dmx * (ohat * gnv) * _silu_grad(gh, sg)).astype(BF16)
        dgn = dgn + _colsum(don * ohat)
        do_scr[:, cols] = _rms_bwd(don, ohat, r, gnv)
    dgn_ref[...] += dgn

    mats = _chunk_matrices()
    masks = _head_masks()
    wgv, bgv = wg_ref[...], bg_ref[...]
    n_chunks = SUB // CHUNK

    for sb in reversed(range(tb // SUB)):
        rows = slice(sb * SUB, (sb + 1) * SUB)
        zs = z_ref[rows, :]
        d = _decay_terms(zs, q_ref[rows, :], k_ref[rows, :], wgv, bgv, mats)
        qem = d.q * d.ebm
        qem_b = qem.astype(BF16)
        kem_b = (d.k * d.emb).astype(BF16)
        dq = jnp.zeros((SUB, KEY), F32)
        dk = jnp.zeros((SUB, KEY), F32)
        for h in range(HEADS):
            hm = masks[h]
            cols = slice(h * DV, (h + 1) * DV)
            do_b = do_scr[rows, cols].astype(BF16)
            vh = v_ref[rows, cols]
            da = jnp.where(mats.causal, _nt(do_b, vh), 0.0).astype(BF16)
            da_t = jnp.where(mats.causal_t, _nt(vh, do_b), 0.0).astype(BF16)
            a_t = jnp.where(mats.causal_t, _nt(kem_b, (qem * hm).astype(BF16)), 0.0).astype(BF16)
            dq = dq + hm * _nn(da, kem_b)
            dk = dk + hm * _nn(da_t, qem_b)
            dv_scr[rows, cols] = _nn(a_t, do_b)
        dq = dq * d.ebm
        dk = dk * d.emb

        qe0_b = (d.q * d.eb).astype(BF16)
        kdec_b = (d.k * d.elb).astype(BF16)
        dq_st, dk_st, last = [None] * n_chunks, [None] * n_chunks, [None] * n_chunks
        for c in reversed(range(n_chunks)):
            loc = slice(c * CHUNK, (c + 1) * CHUNK)
            glob = slice(sb * SUB + c * CHUNK, sb * SUB + (c + 1) * CHUNK)
            st_b = st_ref[sb * n_chunks + c]
            ds = dstate[...]
            ds_b = ds.astype(BF16)
            do_c = do_scr[glob, :].astype(BF16)
            ebl_c = d.ebl[c * CHUNK:c * CHUNK + 1]
            dk_c = _nn(v_ref[glob, :], ds_b) * d.elb[loc]
            dq_st[c] = _nn(do_c, st_b) * d.eb[loc]
            dk_st[c] = dk_c
            last_c = _colsum(d.k[loc] * dk_c) + ebl_c * _colsum(st_b.astype(F32) * ds)
            last[c] = jnp.broadcast_to(last_c, (CHUNK, KEY))
            dv_ref[glob, :] = (dv_scr[glob, :] + _nt(kdec_b[loc], ds_b)).astype(BF16)
            dstate[...] = ds * ebl_c + jnp.where(mats.heads, _tn(do_c, qe0_b[loc]), 0.0)
        dq = dq + jnp.concatenate(dq_st, axis=0)
        dk = dk + jnp.concatenate(dk_st, axis=0)
        dq_ref[rows, :] = (dq * Q_SCALE).astype(BF16)
        dk_ref[rows, :] = dk.astype(BF16)
        hi, lo = _split_bf16(d.q * dq - d.k * dk)
        dla = _nn(mats.tri_t, hi) + _nn(mats.tri_t, lo) + jnp.concatenate(last, axis=0)
        dal = dla * (1.0 / GATE_TAU) * jax.nn.sigmoid(-d.al)
        dal_b = dal.astype(BF16)
        dz_ref[rows, :] = _nt(dal_b, wgv).astype(BF16)
        dwg_ref[...] += _tn(zs, dal_b)
        dbg_ref[...] += _colsum(dal)


def _gla_bwd(proj, o, states, dmix, wg, bg, gn, parts):
    T = proj.shape[0]
    tb = min(T, 512)
    cpb = tb // CHUNK
    nb = T // tb
    n_comm = len(parts)
    kinds = ["exchange"] * n_comm
    comm_start, comm_wait = _hosted_comm(kinds, 11, 8, n_comm, nb)

    def body(*refs):
        comm_start(refs)
        _gla_bwd_tile(*refs[:11], *refs[11 + n_comm:19 + n_comm], *refs[19 + 2 * n_comm:22 + 2 * n_comm], tb)
        comm_wait(refs)

    rev = lambda i: nb - 1 - i
    blk = lambda w, col: pl.BlockSpec((tb, w), lambda i: (rev(i), col))
    res = pl.pallas_call(
        body, name="gla_bwd", grid=(nb,),
        out_shape=[jax.ShapeDtypeStruct((T, KEY), BF16), jax.ShapeDtypeStruct((T, KEY), BF16),
                   jax.ShapeDtypeStruct((T, VAL), BF16), jax.ShapeDtypeStruct((T, VAL), BF16),
                   jax.ShapeDtypeStruct((T, Z_PAD), BF16), jax.ShapeDtypeStruct((Z_PAD, KEY), F32),
                   jax.ShapeDtypeStruct((1, KEY), F32), jax.ShapeDtypeStruct((1, DV), F32)]
        + _comm_out_shapes(kinds, parts),
        in_specs=[blk(KEY, P_Q // KEY), blk(KEY, P_K // KEY), blk(VAL, P_V // VAL), blk(VAL, P_G // VAL),
                  blk(Z_PAD, P_Z // Z_PAD), blk(VAL, 0),
                  pl.BlockSpec((cpb, VAL, KEY), lambda i: (rev(i), 0, 0)), blk(VAL, 0),
                  _const((Z_PAD, KEY)), _const((1, KEY)), _const((1, DV))] + [ANY] * n_comm,
        out_specs=[blk(KEY, 0), blk(KEY, 0), blk(VAL, 0), blk(VAL, 0), blk(Z_PAD, 0),
                   pl.BlockSpec((Z_PAD, KEY), lambda i: (0, 0)), pl.BlockSpec((1, KEY), lambda i: (0, 0)),
                   pl.BlockSpec((1, DV), lambda i: (0, 0))] + [ANY] * n_comm,
        scratch_shapes=[pltpu.VMEM((VAL, KEY), F32), pltpu.VMEM((tb, VAL), F32), pltpu.VMEM((tb, VAL), F32)]
        + _comm_scratch(n_comm),
        compiler_params=_params(("arbitrary",)),
    )(proj, proj, proj, proj, proj, o, states, dmix, wg, bg, gn, *parts)
    return res[:8], res[8:]


def _inproj_bwd(x, g1, w_in_t, dh1, dq, dk, dv, dg, dci, dcg, dz):
    T = x.shape[0]
    tm = min(T, 512)

    def body(x_ref, g_ref, w_ref, dh1_ref, dq_ref, dk_ref, dv_ref, dg_ref, dci_ref, dcg_ref, dz_ref,
             dx_ref, dg1_ref, dp_ref):
        @pl.when(pl.program_id(0) == 0)
        def _():
            dg1_ref[...] = jnp.zeros_like(dg1_ref)

        dp_ref[:, P_Q:P_K] = dq_ref[...]
        dp_ref[:, P_K:P_V] = dk_ref[...]
        dp_ref[:, P_V:P_G] = dv_ref[...]
        dp_ref[:, P_G:P_CI] = dg_ref[...]
        dp_ref[:, P_CI:P_CG] = dci_ref[...]
        dp_ref[:, P_CG:P_Z] = dcg_ref[...]
        dp_ref[:, P_Z:] = dz_ref[...]
        dxn = (_nn(dp_ref[:, 0:P_CI], w_ref[0:OFF_Z, :]) + _nn(dp_ref[:, P_CI:P_Z], w_ref[OFF_C:D_IN, :])
               + _nn(dp_ref[:, P_Z:], w_ref[OFF_Z:OFF_Z + Z_PAD, :]))
        xv = x_ref[...]
        r = lax.rsqrt(_rowmean(xv * xv) + EPS)
        xhat = xv * r
        dg1_ref[...] += _colsum(dxn * xhat)
        dx_ref[...] = dh1_ref[...] + _rms_bwd(dxn, xhat, r, g_ref[...])

    tok = lambda w: pl.BlockSpec((tm, w), lambda i: (i, 0))
    return pl.pallas_call(
        body, name="inproj_bwd", grid=(T // tm,),
        out_shape=[jax.ShapeDtypeStruct((T, D_MODEL), F32), jax.ShapeDtypeStruct((1, D_MODEL), F32)],
        in_specs=[tok(D_MODEL), _const((1, D_MODEL)), _const((D_IN, D_MODEL)), tok(D_MODEL), tok(KEY), tok(KEY),
                  tok(VAL), tok(VAL), tok(CONV), tok(CONV), tok(Z_PAD)],
        out_specs=[tok(D_MODEL), pl.BlockSpec((1, D_MODEL), lambda i: (0, 0))],
        scratch_shapes=[pltpu.VMEM((tm, D_INP), BF16)],
        compiler_params=_params(("arbitrary",)),
    )(x, g1, w_in_t, dh1, dq, dk, dv, dg, dci, dcg, dz)


def _wgrad_in(xn, dq, dk, dv, dg, dz, dci, dcg):
    T = xn.shape[0]
    tt = min(T, 1024)
    nt = T // tt
    pieces = [dq, dk, dv, dg, dz, dci, dcg]
    rows = [KEY, KEY, VAL, VAL, RANK, CONV, CONV]
    assert sum(rows) == D_IN

    def body(*refs):
        xn_ref, piece_refs, o_ref, acc = refs[0], refs[1:1 + len(pieces)], refs[-2], refs[-1]

        @pl.when(pl.program_id(0) == 0)
        def _():
            acc[...] = jnp.zeros_like(acc)

        xv = xn_ref[...]
        row = 0
        for ref, n in zip(piece_refs, rows):
            acc[row:row + n, :] += _tn(ref[...], xv)[0:n]
            row += n

        @pl.when(pl.program_id(0) == nt - 1)
        def _():
            o_ref[...] = acc[...].astype(BF16)

    tok = lambda w: pl.BlockSpec((tt, w), lambda t: (t, 0))
    return pl.pallas_call(
        body, name="wgrad_in", grid=(nt,), out_shape=jax.ShapeDtypeStruct((D_IN, D_MODEL), BF16),
        in_specs=[tok(D_MODEL)] + [tok(p.shape[1]) for p in pieces],
        out_specs=pl.BlockSpec((D_IN, D_MODEL), lambda t: (0, 0), pipeline_mode=pl.Buffered(1)),
        scratch_shapes=[pltpu.VMEM((D_IN, D_MODEL), F32)],
        compiler_params=_params(("arbitrary",)),
    )(xn, *pieces)


def _wgrad_out(mix_a, mix_c, dh1):
    T = dh1.shape[0]
    tt = min(T, 2048)
    nt = T // tt

    def body(a_ref, c_ref, b_ref, o_ref, acc):
        @pl.when(pl.program_id(0) == 0)
        def _():
            acc[...] = jnp.zeros_like(acc)

        b = b_ref[...].astype(BF16)
        acc[0:VAL, :] += _tn(a_ref[...], b)
        acc[VAL:, :] += _tn(c_ref[...], b)

        @pl.when(pl.program_id(0) == nt - 1)
        def _():
            o_ref[...] = acc[...].astype(BF16)

    tok = lambda w: pl.BlockSpec((tt, w), lambda t: (t, 0))
    return pl.pallas_call(
        body, name="wgrad_out", grid=(nt,), out_shape=jax.ShapeDtypeStruct((D_MODEL, D_MODEL), BF16),
        in_specs=[tok(VAL), tok(CONV), tok(D_MODEL)],
        out_specs=pl.BlockSpec((D_MODEL, D_MODEL), lambda t: (0, 0)),
        scratch_shapes=[pltpu.VMEM((D_MODEL, D_MODEL), F32)],
        compiler_params=_params(("arbitrary",)),
    )(mix_a, mix_c, dh1)


def _wgrad(a, b, name, tk, tn, col_block=None):
    T, K = a.shape
    N = b.shape[1]
    tt = min(T, 2048)
    nt = T // tt

    def body(a_ref, b_ref, o_ref, acc):
        @pl.when(pl.program_id(2) == 0)
        def _():
            acc[...] = jnp.zeros_like(acc)

        acc[...] += _tn(a_ref[...], b_ref[...].astype(BF16))

        @pl.when(pl.program_id(2) == nt - 1)
        def _():
            if col_block is None:
                o_ref[...] = acc[...].astype(BF16)
            else:
                for q in range(tn // col_block):
                    o_ref[q] = acc[:, q * col_block:(q + 1) * col_block].astype(BF16)

    if col_block is None:
        out_shape = jax.ShapeDtypeStruct((K, N), BF16)
        out_spec = pl.BlockSpec((tk, tn), lambda i, j, t: (i, j))
    else:
        assert tk == K
        out_shape = jax.ShapeDtypeStruct((N // col_block, K, col_block), BF16)
        out_spec = pl.BlockSpec((tn // col_block, tk, col_block), lambda i, j, t: (j, 0, 0))
    return pl.pallas_call(
        body, name=name, grid=(K // tk, N // tn, nt), out_shape=out_shape,
        in_specs=[pl.BlockSpec((tt, tk), lambda i, j, t: (t, i)), pl.BlockSpec((tt, tn), lambda i, j, t: (t, j))],
        out_specs=out_spec, scratch_shapes=[pltpu.VMEM((tk, tn), F32)],
        compiler_params=_params(("arbitrary", "arbitrary", "arbitrary")),
    )(a, b)


def _adam_math(w, g, m, v):
    m = ADAM_B1 * m + (1.0 - ADAM_B1) * g
    v = ADAM_B2 * v + (1.0 - ADAM_B2) * (g * g)
    m_hat = m / (1.0 - ADAM_B1 ** ADAM_STEP)
    v_hat = v / (1.0 - ADAM_B2 ** ADAM_STEP)
    delta = -ADAM_LR * (m_hat / (jnp.sqrt(v_hat) + ADAM_EPS) + ADAM_WD * w)
    return delta, m, v


def _sum8(ref):
    g = ref[0].astype(F32)
    for s in range(1, N_DEV):
        g = g + ref[s].astype(F32)
    return g


def _adam_big(parts, w, m, v, name):
    R, C = w.shape
    tr = 128 if R % 128 == 0 else R

    def body(p_ref, w_ref, m_ref, v_ref, g_ref, d_ref, nm_ref, nv_ref):
        g = _sum8(p_ref)
        g_ref[...] = g
        d_ref[...], nm_ref[...], nv_ref[...] = _adam_math(w_ref[...], g, m_ref[...], v_ref[...])

    row = pl.BlockSpec((tr, C), lambda i: (i, 0))
    return pl.pallas_call(
        body, name=name, grid=(R // tr,), out_shape=[jax.ShapeDtypeStruct((R, C), F32)] * 4,
        in_specs=[pl.BlockSpec((N_DEV, tr, C), lambda i: (0, i, 0)), row, row, row], out_specs=[row] * 4,
        compiler_params=_params(("arbitrary",)),
    )(parts, w, m, v)


def _sum_small(parts):
    def body(p_ref, o_ref):
        o_ref[...] = _sum8(p_ref)

    return pl.pallas_call(body, name="sum_small", out_shape=jax.ShapeDtypeStruct(parts.shape[1:], F32))(parts)


def _adam_small(gs, ws, ms, vs):
    n = len(gs)

    def body(*refs):
        g_refs, w_refs, m_refs, v_refs = refs[:n], refs[n:2 * n], refs[2 * n:3 * n], refs[3 * n:4 * n]
        outs = refs[4 * n:]
        for i in range(n):
            d, nm, nv = _adam_math(w_refs[i][...], g_refs[i][...], m_refs[i][...], v_refs[i][...])
            outs[i][...] = d
            outs[n + i][...] = nm
            outs[2 * n + i][...] = nv

    shapes = [jax.ShapeDtypeStruct(w.shape, F32) for w in ws]
    res = pl.pallas_call(body, name="adam_small", out_shape=shapes * 3)(*gs, *ws, *ms, *vs)
    return res[:n], res[n:2 * n], res[2 * n:]


def _group_matrix():
    gi = lax.broadcasted_iota(jnp.int32, (CONV, CONV), 0) // (CONV // GROUPS)
    gj = lax.broadcasted_iota(jnp.int32, (CONV, CONV), 1) // (CONV // GROUPS)
    return jnp.where(gi == gj, GROUPS / CONV, 0.0).astype(BF16)


_SMALL = [("loss", 8), ("dg1", 8), ("dbg", 2), ("dgn", 1), ("dconv_b", 4), ("dcn_g", 4), ("dcn_b", 4), ("dg2", 8),
          ("dgf", 8), ("dwg", 32), ("dconv_w", 124)]


def _pad8(rows):
    return -(-rows // 8) * 8


def kernel(x, norm1_g, w_in, w_gate_up, b_gate, gla_norm_g, conv_w, conv_b, conv_norm_g, conv_norm_b, w_out, norm2_g, w_mlp_in, w_mlp_out, final_norm_g, loss_target, m_norm1_g, m_w_in, m_w_gate_up, m_b_gate, m_gla_norm_g, m_conv_w, m_conv_b, m_conv_norm_g, m_conv_norm_b, m_w_out, m_norm2_g, m_w_mlp_in, m_w_mlp_out, m_final_norm_g, v_norm1_g, v_w_in, v_w_gate_up, v_b_gate, v_gla_norm_g, v_conv_w, v_conv_b, v_conv_norm_g, v_conv_norm_b, v_w_out, v_norm2_g, v_w_mlp_in, v_w_mlp_out, v_final_norm_g):
    x_idx = lax.axis_index("x")
    y_idx = lax.axis_index("y")
    c_idx = lax.axis_index("c")
    me = 4 * x_idx + 2 * y_idx + c_idx
    xs, tgt = x[0], loss_target[0]
    gf = final_norm_g.reshape(1, D_MODEL)
    gmat = _group_matrix()

    small_shard = jnp.zeros((48, 128), F32)
    small_shard = small_shard.at[0:RANK, 0:KEY // N_DEV].set(w_gate_up[0])
    small_shard = small_shard.at[RANK:RANK + CONV_W, 0:CONV // N_DEV].set(conv_w[0])
    g_in, g_small = _gather_two_level([w_in[0].T.astype(BF16), small_shard], "gather_w_in")
    w_in_t = g_in.reshape(D_IN, D_MODEL)
    wg_full = jnp.concatenate([g_small[d, 0:RANK, 0:KEY // N_DEV] for d in range(N_DEV)], axis=1)
    wg_pad = jnp.pad(wg_full, ((0, Z_PAD - RANK), (0, 0))).astype(BF16)
    conv_w_full = jnp.concatenate([g_small[d, RANK:RANK + CONV_W, 0:CONV // N_DEV] for d in range(N_DEV)], axis=1)
    conv_w_pad = jnp.pad(conv_w_full, ((0, HALO - CONV_W), (0, 0)))

    proj, xn, (g_w2,) = _inproj_fwd(xs, norm1_g, w_in_t, [w_mlp_out[0].astype(BF16)])
    mix_a, o, states, mix_c, uc, (g_out, g_w1) = _mix_fwd(
        proj, wg_pad, b_gate, gla_norm_g, conv_w_pad, conv_b, conv_norm_g, conv_norm_b, gmat,
        [w_out[0].astype(BF16), w_mlp_in[0].T.astype(BF16)])
    w_out_full = g_out.reshape(D_MODEL, D_MODEL)
    w1t_full = g_w1.reshape(D_FF, D_MODEL)
    w2_full = g_w2.reshape(D_FF, D_MODEL)
    dh1, dmix, hn, ff, da, dh2, loss, dgf, dg2 = _mlp_fwd_bwd(xs, mix_a, mix_c, tgt, w_out_full, norm2_g, w1t_full,
                                                              w2_full, gf)

    dw1 = _wgrad(hn, da, "wgrad_mlp_in", 1024, 1024, col_block=D_FF // N_DEV)
    dw2 = _wgrad(ff, dh2, "wgrad_mlp_out", 1024, 1024)
    dw_out = _wgrad_out(mix_a, mix_c, dh1)
    dci, dcg, dconv_w, dconv_b, dcn_g, dcn_b = _conv_bwd(proj, uc, dmix, conv_w_pad, conv_norm_g, conv_norm_b, gmat)
    (dq, dk, dv, dg, dz, dwg, dbg, dgn), (p_w1, p_w2, p_out) = _gla_bwd(
        proj, o, states, dmix, wg_pad, b_gate, gla_norm_g,
        [dw1, dw2.reshape(N_DEV, D_FF // N_DEV, D_MODEL), dw_out.reshape(N_DEV, D_MODEL // N_DEV, D_MODEL)])
    dw_in = _wgrad_in(xn, dq, dk, dv, dg, dz, dci, dcg).reshape(N_DEV, SHARD_IN, D_MODEL)
    send_sems, recv_sems, dw_in_thru, land, token = _split_start("exchange", dw_in, jnp.copy(dw_in),
                                                                 "exchange_w_in_start")
    dx, dg1 = _inproj_bwd(xs, norm1_g + token[0:1, 0:1], w_in_t, dh1, dq, dk, dv, dg, dci, dcg, dz)
    p_in = _split_wait("exchange", send_sems, recv_sems, dw_in_thru, land, dg1, "exchange_w_in_wait")

    small = dict(loss=jnp.zeros((8, 128), F32) + loss, dg1=dg1, dbg=dbg, dgn=dgn, dconv_b=dconv_b, dcn_g=dcn_g,
                 dcn_b=dcn_b, dg2=dg2, dgf=dgf, dwg=dwg[0:RANK], dconv_w=dconv_w[0:CONV_W])
    pack = jnp.concatenate([jnp.pad(small[name].reshape(rows, 128), ((0, _pad8(rows) - rows), (0, 0)))
                            for name, rows in _SMALL], axis=0)
    s_send, s_recv, pack_thru, pack_land, s_token = _split_start(
        "gather", pack, jnp.broadcast_to(pack, (N_DEV,) + pack.shape) + 0.0, "gather_small_start")

    gi, di, mi, vi = _adam_big(p_in, w_in[0].T, m_w_in[0].T, v_w_in[0].T, "adam_w_in")
    go, do, mo, vo = _adam_big(p_out, w_out[0] + s_token[0:1, 0:1], m_w_out[0], v_w_out[0], "adam_w_out")
    ga, da_, ma, va = _adam_big(p_w1, w_mlp_in[0], m_w_mlp_in[0], v_w_mlp_in[0], "adam_w_mlp_in")
    gb, db, mb, vb = _adam_big(p_w2, w_mlp_out[0], m_w_mlp_out[0], v_w_mlp_out[0], "adam_w_mlp_out")
    cut = lambda a: a.T[None]

    g_pack = _split_wait("gather", s_send, s_recv, pack_thru, pack_land, go[0:8, 0:128] + ga[0:8, 0:128]
                         + gb[0:8, 0:128], "gather_small_wait")
    summed = _sum_small(g_pack)
    small_g = {}
    at = 0
    for name, rows in _SMALL:
        small_g[name] = summed[at:at + rows]
        at += _pad8(rows)
    loss_out = small_g["loss"][0, 0]
    wg_cols = KEY // N_DEV
    cw_cols = CONV // N_DEV
    g_small_list = [
        small_g["dg1"].reshape(1, D_MODEL),
        lax.dynamic_slice_in_dim(small_g["dwg"].reshape(RANK, KEY), me * wg_cols, wg_cols, axis=1)[None],
        small_g["dbg"].reshape(1, KEY), small_g["dgn"].reshape(1, DV),
        lax.dynamic_slice_in_dim(small_g["dconv_w"].reshape(CONV_W, CONV), me * cw_cols, cw_cols, axis=1)[None],
        small_g["dconv_b"].reshape(1, CONV), small_g["dcn_g"].reshape(1, CONV), small_g["dcn_b"].reshape(1, CONV),
        small_g["dg2"].reshape(1, D_MODEL), small_g["dgf"].reshape(1, D_MODEL),
    ]
    row = lambda a: a.reshape(1, D_MODEL)
    w_small = [norm1_g, w_gate_up, b_gate, gla_norm_g, conv_w, conv_b, conv_norm_g, conv_norm_b, norm2_g,
               row(final_norm_g)]
    m_small = [m_norm1_g, m_w_gate_up, m_b_gate, m_gla_norm_g, m_conv_w, m_conv_b, m_conv_norm_g, m_conv_norm_b,
               m_norm2_g, row(m_final_norm_g)]
    v_small = [v_norm1_g, v_w_gate_up, v_b_gate, v_gla_norm_g, v_conv_w, v_conv_b, v_conv_norm_g, v_conv_norm_b,
               v_norm2_g, row(v_final_norm_g)]
    d_small, nm_small, nv_small = _adam_small(g_small_list, w_small, m_small, v_small)
    flat = lambda lst: list(lst[:-1]) + [lst[-1].reshape(D_MODEL)]
    g_small_list, d_small, nm_small, nv_small = flat(g_small_list), flat(d_small), flat(nm_small), flat(nv_small)

    def order(s, w_in_v, w_out_v, w1_v, w2_v):
        return [s[0], w_in_v, s[1], s[2], s[3], s[4], s[5], s[6], s[7], w_out_v, s[8], w1_v, w2_v, s[9]]

    grads = order(g_small_list, cut(gi), go[None], ga[None], gb[None])
    deltas = order(d_small, cut(di), do[None], da_[None], db[None])
    new_m = order(nm_small, cut(mi), mo[None], ma[None], mb[None])
    new_v = order(nv_small, cut(vi), vo[None], va[None], vb[None])
    return (loss_out, dx[None], *grads, *deltas, *new_m, *new_v)
```

```python
from typing import NamedTuple

import jax
import jax.numpy as jnp
from jax import lax
from jax.experimental import pallas as pl
from jax.experimental.pallas import tpu as pltpu

F32 = jnp.float32
BF16 = jnp.bfloat16

N_DEV = 8
D_MODEL = 1024
HEADS = 4
DK = 64
DV = 128
KEY = HEADS * DK
VAL = HEADS * DV
RANK = 16
CONV = 512
GROUPS = 8
CONV_W = 31
HALO = 32
SUBLANES = 8
STRIP = 32
D_FF = 4096
D_IN = 2576
SHARD_IN = D_IN // N_DEV
CHUNK = 64
SUB = 256
EPS = 1e-6
GATE_TAU = 16.0
Q_SCALE = DK ** -0.5

P_Q, P_K, P_V, P_G, P_CI, P_CG, P_Z = 0, 256, 512, 1024, 1536, 2048, 2560
D_INP = 2688
Z_PAD = D_INP - P_Z
OFF_Z = 1536
OFF_C = OFF_Z + RANK

ADAM_LR = 0.001
ADAM_B1 = 0.9
ADAM_B2 = 0.999
ADAM_EPS = 1e-08
ADAM_WD = 0.01
ADAM_STEP = 10

VMEM_LIMIT = 56 * 1024 * 1024

MESH = pl.DeviceIdType.MESH
ANY = pl.BlockSpec(memory_space=pl.ANY)


def _nn(a, b):
    return jnp.dot(a, b, preferred_element_type=F32)


def _nt(a, b):
    return lax.dot_general(a, b, (((1,), (1,)), ((), ())), preferred_element_type=F32)


def _tn(a, b):
    return lax.dot_general(a, b, (((0,), (0,)), ((), ())), preferred_element_type=F32)


def _params(sem=None):
    return pltpu.CompilerParams(dimension_semantics=sem, vmem_limit_bytes=VMEM_LIMIT)


def _const(shape):
    return pl.BlockSpec(shape, lambda *_: (0,) * len(shape), pipeline_mode=pl.Buffered(1))


def _colsum(v):
    return jnp.sum(v, axis=0, keepdims=True)


def _rowmean(v):
    return jnp.mean(v, axis=-1, keepdims=True)


def _split_bf16(v):
    hi = v.astype(BF16)
    return hi, (v - hi.astype(F32)).astype(BF16)


def _my_place():
    return lax.axis_index("x"), lax.axis_index("y"), lax.axis_index("c")


def _peer(j):
    x, y, c = _my_place()
    jx, jy, jc = (j >> 2) & 1, (j >> 1) & 1, j & 1
    px = 1 - x if jx else x
    py = 1 - y if jy else y
    pc = 1 - c if jc else c
    return (px, py, pc), 4 * px + 2 * py + pc


def _comm_plan(kinds, ins, outs, send_sems, recv_sems, local_sems, receives=True):
    x, y, c = _my_place()
    me = 4 * x + 2 * y + c
    own = lambda k, idx: ins[k] if kinds[k] == "gather" else ins[k].at[idx]
    local = [pltpu.make_async_copy(own(k, me), outs[k].at[me], local_sems.at[k]) for k in range(len(kinds))]
    sends, recvs = [], []
    for j in range(1, N_DEV):
        peer, peer_idx = _peer(j)
        for k in range(len(kinds)):
            sems = dict(send_sem=send_sems.at[k, j - 1], recv_sem=recv_sems.at[k, j - 1], device_id=peer,
                        device_id_type=MESH)
            sends.append(pltpu.make_async_remote_copy(src_ref=own(k, peer_idx), dst_ref=outs[k].at[me], **sems))
            if receives:
                recvs.append(pltpu.make_async_remote_copy(src_ref=own(k, me), dst_ref=outs[k].at[peer_idx], **sems))
    return local, sends, recvs


def _comm_start(plan):
    local, sends, _ = plan
    for cp in local + sends:
        cp.start()


def _comm_wait(plan):
    local, sends, recvs = plan
    for cp in recvs:
        cp.wait_recv()
    for cp in sends:
        cp.wait_send()
    for cp in local:
        cp.wait()


def _comm_scratch(n):
    return [pltpu.SemaphoreType.DMA((n, N_DEV - 1)), pltpu.SemaphoreType.DMA((n, N_DEV - 1)),
            pltpu.SemaphoreType.DMA((n,))]


def _comm_out_shapes(kinds, arrays):
    return [jax.ShapeDtypeStruct(((N_DEV,) + a.shape) if kind == "gather" else a.shape, a.dtype)
            for kind, a in zip(kinds, arrays)]


def _comm(kinds, arrays, name):
    n = len(arrays)

    def body(*refs):
        plan = _comm_plan(kinds, refs[:n], refs[n:2 * n], *refs[2 * n:])
        _comm_start(plan)
        _comm_wait(plan)

    return pl.pallas_call(
        body, name=name, out_shape=_comm_out_shapes(kinds, arrays), in_specs=[ANY] * n, out_specs=[ANY] * n,
        scratch_shapes=_comm_scratch(n),
    )(*arrays)


def _gather_two_level(shards, name):
    n = len(shards)

    def body(*refs):
        ins, outs = refs[:n], refs[n:2 * n]
        send_sems, recv_sems, local_sems = refs[2 * n:]
        x, y, c = _my_place()
        index = lambda px, py, pc: 4 * px + 2 * py + pc
        me, sibling = (x, y, c), (x, y, 1 - c)
        chips = [(1 - x, y), (x, 1 - y), (1 - x, 1 - y)]

        def copy(k, slot, block, to, src=None):
            rows = outs[k].at[index(*block)]
            return pltpu.make_async_remote_copy(
                src_ref=rows if src is None else src, dst_ref=rows, send_sem=send_sems.at[k, slot],
                recv_sem=recv_sems.at[k, slot], device_id=to, device_id_type=MESH)

        local = [pltpu.make_async_copy(ins[k], outs[k].at[index(*me)], local_sems.at[k]) for k in range(n)]
        first = []
        for k in range(n):
            first.append(copy(k, 0, me, sibling, src=ins[k]))
            first += [copy(k, 1 + j, me, (*chip, c), src=ins[k]) for j, chip in enumerate(chips)]
        for cp in local + first:
            cp.start()
        passed = []
        for j, chip in enumerate(chips):
            for k in range(n):
                copy(k, 1 + j, (*chip, c), me).wait_recv()
                cp = copy(k, 4 + j, (*chip, c), sibling)
                cp.start()
                passed.append(cp)
        for k in range(n):
            copy(k, 0, sibling, me).wait_recv()
        for j, chip in enumerate(chips):
            for k in range(n):
                copy(k, 4 + j, (*chip, 1 - c), me).wait_recv()
        for cp in first + passed:
            cp.wait_send()
        for cp in local:
            cp.wait()

    return pl.pallas_call(
        body, name=name, out_shape=_comm_out_shapes(["gather"] * n, shards), in_specs=[ANY] * n, out_specs=[ANY] * n,
        scratch_shapes=_comm_scratch(n),
    )(*shards)


HBM = pl.BlockSpec(memory_space=pltpu.HBM)
SEM = pl.BlockSpec(memory_space=pltpu.SEMAPHORE)
DATAFLOW = pltpu.SideEffectType.DATAFLOW_SIDE_EFFECTING


def _split_start(kind, part, land, name):
    def body(src_ref, land_ref, send_sems, recv_sems, src_thru, land_thru, token):
        x, y, c = _my_place()
        me = 4 * x + 2 * y + c
        for j in range(1, N_DEV):
            peer, peer_idx = _peer(j)
            pltpu.make_async_remote_copy(
                src_ref=src_ref.at[peer_idx] if kind == "exchange" else src_ref, dst_ref=land_ref.at[me],
                send_sem=send_sems.at[j - 1], recv_sem=recv_sems.at[j - 1], device_id=peer,
                device_id_type=MESH).start()
        token[...] = jnp.zeros_like(token)

    return pl.pallas_call(
        body, name=name,
        out_shape=(pltpu.SemaphoreType.DMA((N_DEV - 1,)), pltpu.SemaphoreType.DMA((N_DEV - 1,)),
                   pltpu.HBM(part.shape, part.dtype), pltpu.HBM(land.shape, land.dtype),
                   jax.ShapeDtypeStruct((8, 128), F32)),
        in_specs=(HBM, HBM), out_specs=(SEM, SEM, HBM, HBM, pl.BlockSpec(memory_space=pltpu.VMEM)),
        input_output_aliases={0: 2, 1: 3},
        compiler_params=pltpu.CompilerParams(has_side_effects=DATAFLOW),
    )(pltpu.with_memory_space_constraint(part, pltpu.HBM), pltpu.with_memory_space_constraint(land, pltpu.HBM))


def _split_wait(kind, send_sems, recv_sems, part_thru, land_thru, after, name):
    def body(src_ref, land_ref, send_sems, recv_sems, after_ref, src_dead, got_ref):
        x, y, c = _my_place()
        me = 4 * x + 2 * y + c
        own = lambda idx: src_ref.at[idx] if kind == "exchange" else src_ref
        for j in range(1, N_DEV):
            peer, peer_idx = _peer(j)
            sems = dict(send_sem=send_sems.at[j - 1], recv_sem=recv_sems.at[j - 1], device_id=peer,
                        device_id_type=MESH)
            pltpu.make_async_remote_copy(src_ref=own(peer_idx), dst_ref=land_ref.at[me], **sems).wait_send()
            pltpu.make_async_remote_copy(src_ref=own(me), dst_ref=land_ref.at[peer_idx], **sems).wait_recv()

    return pl.pallas_call(
        body, name=name,
        out_shape=(pltpu.HBM(part_thru.shape, part_thru.dtype), pltpu.HBM(land_thru.shape, land_thru.dtype)),
        in_specs=(HBM, HBM, SEM, SEM, ANY), out_specs=(HBM, HBM), input_output_aliases={0: 0, 1: 1},
        compiler_params=pltpu.CompilerParams(has_side_effects=DATAFLOW),
    )(part_thru, land_thru, send_sems, recv_sems, after)[1]


def _hosted_comm(kinds, n_in, n_out, n_comm, n_steps):
    def plan_of(refs, receives):
        ins = refs[n_in:n_in + n_comm]
        outs = refs[n_in + n_comm + n_out:n_in + 2 * n_comm + n_out]
        return _comm_plan(kinds, ins, outs, *refs[-3:], receives=receives)

    def start(refs):
        @pl.when(pl.program_id(0) == 0)
        def _():
            _comm_start(plan_of(refs, False))

    def wait(refs):
        @pl.when(pl.program_id(0) == n_steps - 1)
        def _():
            _comm_wait(plan_of(refs, True))

    return start, wait


def _z_lanes():
    return lax.broadcasted_iota(jnp.int32, (1, Z_PAD), 1) < RANK


def _inproj_fwd(x, g1, w_in_t, shards):
    T = x.shape[0]
    tm = min(T, 512)
    n_comm = len(shards)
    kinds = ["gather"] * n_comm
    comm_start, comm_wait = _hosted_comm(kinds, 3, 2, n_comm, T // tm)

    def body(*refs):
        x_ref, g_ref, w_ref = refs[:3]
        proj_ref, xn_ref = refs[3 + n_comm:5 + n_comm]
        comm_start(refs)
        xv = x_ref[...]
        r = lax.rsqrt(_rowmean(xv * xv) + EPS)
        xn = (xv * r * g_ref[...]).astype(BF16)
        xn_ref[...] = xn
        proj_ref[:, 0:P_CI] = _nt(xn, w_ref[0:OFF_Z, :]).astype(BF16)
        proj_ref[:, P_CI:P_Z] = _nt(xn, w_ref[OFF_C:D_IN, :]).astype(BF16)
        proj_ref[:, P_Z:] = jnp.where(_z_lanes(), _nt(xn, w_ref[OFF_Z:OFF_Z + Z_PAD, :]), 0.0).astype(BF16)
        comm_wait(refs)

    res = pl.pallas_call(
        body, name="inproj_fwd", grid=(T // tm,),
        out_shape=[jax.ShapeDtypeStruct((T, D_INP), BF16), jax.ShapeDtypeStruct((T, D_MODEL), BF16)]
        + _comm_out_shapes(kinds, shards),
        in_specs=[pl.BlockSpec((tm, D_MODEL), lambda i: (i, 0)), _const((1, D_MODEL)), _const((D_IN, D_MODEL))]
        + [ANY] * n_comm,
        out_specs=[pl.BlockSpec((tm, D_INP), lambda i: (i, 0)), pl.BlockSpec((tm, D_MODEL), lambda i: (i, 0))]
        + [ANY] * n_comm,
        scratch_shapes=_comm_scratch(n_comm),
        compiler_params=_params(("arbitrary",)),
    )(x, g1, w_in_t, *shards)
    return res[0], res[1], res[2:]


def _head_masks():
    lane = lax.broadcasted_iota(jnp.int32, (1, KEY), 1)
    return [((lane >= h * DK) & (lane < (h + 1) * DK)).astype(F32) for h in range(HEADS)]


class _Mats(NamedTuple):
    tri: jax.Array
    tri_t: jax.Array
    same: jax.Array
    mid: jax.Array
    causal: jax.Array
    causal_t: jax.Array
    heads: jax.Array


def _chunk_matrices():
    r = lax.broadcasted_iota(jnp.int32, (SUB, SUB), 0)
    c = lax.broadcasted_iota(jnp.int32, (SUB, SUB), 1)
    shift = CHUNK.bit_length() - 1
    same = jnp.right_shift(r, shift) == jnp.right_shift(c, shift)
    causal = same & (r >= c)
    causal_t = same & (r <= c)
    mid = same & ((c & (CHUNK - 1)) < CHUNK // 2)
    hr = jnp.right_shift(lax.broadcasted_iota(jnp.int32, (VAL, KEY), 0), DV.bit_length() - 1)
    hc = jnp.right_shift(lax.broadcasted_iota(jnp.int32, (VAL, KEY), 1), DK.bit_length() - 1)
    return _Mats(tri=causal.astype(BF16), tri_t=causal_t.astype(BF16), same=same.astype(BF16), mid=mid.astype(BF16),
                 causal=causal, causal_t=causal_t, heads=hr == hc)


class _Decay(NamedTuple):
    al: jax.Array
    q: jax.Array
    k: jax.Array
    eb: jax.Array
    ebm: jax.Array
    emb: jax.Array
    elb: jax.Array
    ebl: jax.Array


def _decay_terms(z, q, k, wg, bg, mats):
    al = _nn(z, wg) + bg
    la = (jnp.minimum(al, 0.0) - jnp.log(1.0 + jnp.exp(-jnp.abs(al)))) * (1.0 / GATE_TAU)
    hi, lo = _split_bf16(la)
    cum = lambda m: _nn(m, hi) + _nn(m, lo)
    b, b_last, b_mid = cum(mats.tri), cum(mats.same), cum(mats.mid)
    return _Decay(al=al, q=q.astype(F32) * Q_SCALE, k=k.astype(F32), eb=jnp.exp(b), ebm=jnp.exp(b - b_mid),
                  emb=jnp.exp(b_mid - b), elb=jnp.exp(b_last - b), ebl=jnp.exp(b_last))


def _gla_fwd_tile(q_ref, k_ref, v_ref, g_ref, z_ref, wg_ref, bg_ref, gn_ref, mix_ref, o_ref, st_ref, state, tb):
    @pl.when(pl.program_id(0) == 0)
    def _():
        state[...] = jnp.zeros_like(state)

    mats = _chunk_matrices()
    masks = _head_masks()
    wgv, bgv = wg_ref[...], bg_ref[...]

    for sb in range(tb // SUB):
        rows = slice(sb * SUB, (sb + 1) * SUB)
        d = _decay_terms(z_ref[rows, :], q_ref[rows, :], k_ref[rows, :], wgv, bgv, mats)
        kem_b = (d.k * d.emb).astype(BF16)
        qem = d.q * d.ebm
        for h in range(HEADS):
            cols = slice(h * DV, (h + 1) * DV)
            a = jnp.where(mats.causal, _nt((qem * masks[h]).astype(BF16), kem_b), 0.0)
            o_ref[rows, cols] = _nn(a.astype(BF16), v_ref[rows, cols])
        qe0_b = (d.q * d.eb).astype(BF16)
        kdec_b = (d.k * d.elb).astype(BF16)
        for c in range(SUB // CHUNK):
            loc = slice(c * CHUNK, (c + 1) * CHUNK)
            glob = slice(sb * SUB + c * CHUNK, sb * SUB + (c + 1) * CHUNK)
            st = state[...]
            st_b = st.astype(BF16)
            st_ref[sb * (SUB // CHUNK) + c] = st_b
            o_ref[glob, :] += _nt(qe0_b[loc], st_b)
            u = _tn(v_ref[glob, :], kdec_b[loc])
            state[...] = st * d.ebl[c * CHUNK:c * CHUNK + 1] + jnp.where(mats.heads, u, 0.0)

    gnv = gn_ref[...]
    for h in range(HEADS):
        cols = slice(h * DV, (h + 1) * DV)
        oh = o_ref[:, cols]
        r = lax.rsqrt(_rowmean(oh * oh) + EPS)
        gh = g_ref[:, cols].astype(F32)
        mix_ref[:, cols] = (oh * r * gnv * (gh * jax.nn.sigmoid(gh))).astype(BF16)


def _group_mean(v, gmat):
    return _nn(v.astype(BF16), gmat)


def _shifted_copies(buf, sh, rows):
    for k in range(1, SUBLANES):
        sh[k - 1] = buf[pl.ds(k, rows), :]


def _tap(buf, sh, off, r0):
    k, base = off % SUBLANES, off - off % SUBLANES
    rows = pl.ds(r0 + base if isinstance(r0, int) else pl.multiple_of(r0 + base, SUBLANES), STRIP)
    return buf[rows, :] if k == 0 else sh[k - 1, rows, :]


def _conv_fwd_tile(ci_ref, cg_ref, w_ref, b_ref, g_ref, be_ref, gm_ref, mix_ref, uc_ref, ubuf, ush, tm):
    sh_rows = tm + HALO - SUBLANES

    @pl.when(pl.program_id(0) == 0)
    def _():
        ubuf[0:HALO, :] = jnp.zeros((HALO, CONV), F32)

    ubuf[HALO:, :] = ci_ref[...].astype(F32) * jax.nn.sigmoid(cg_ref[...].astype(F32))
    _shifted_copies(ubuf, ush, sh_rows)
    for s in range(tm // STRIP):
        acc = jnp.zeros((STRIP, CONV), F32) + b_ref[...]
        for j in range(CONV_W):
            acc = acc + w_ref[j:j + 1, :] * _tap(ubuf, ush, HALO - (CONV_W - 1) + j, s * STRIP)
        uc_ref[s * STRIP:(s + 1) * STRIP, :] = acc
    ubuf[0:HALO, :] = ubuf[tm:tm + HALO, :]
    gm = gm_ref[...]
    ucv = uc_ref[...]
    d = ucv - _group_mean(ucv, gm)
    var = _group_mean(d * d, gm)
    yn = d * lax.rsqrt(var + EPS) * g_ref[...] + be_ref[...]
    mix_ref[...] = (yn * jax.nn.sigmoid(yn)).astype(BF16)


def _mix_fwd(proj, wg, bg, gn, conv_w, conv_b, cn_g, cn_b, gmat, shards):
    T = proj.shape[0]
    tb = min(T, 512)
    cpb = tb // CHUNK
    n_comm = len(shards)
    kinds = ["gather"] * n_comm
    comm_start, comm_wait = _hosted_comm(kinds, 15, 5, n_comm, T // tb)

    def body(*refs):
        gla_in, conv_in = refs[:8], refs[8:15]
        gla_out, conv_out = refs[15 + n_comm:18 + n_comm], refs[18 + n_comm:20 + n_comm]
        state, ubuf, ush = refs[20 + 2 * n_comm:23 + 2 * n_comm]
        comm_start(refs)
        _gla_fwd_tile(*gla_in, *gla_out, state, tb)
        _conv_fwd_tile(*conv_in, *conv_out, ubuf, ush, tb)
        comm_wait(refs)

    nc = T // CHUNK
    tok = lambda w, col: pl.BlockSpec((tb, w), lambda i: (i, col))
    res = pl.pallas_call(
        body, name="mix_fwd", grid=(T // tb,),
        out_shape=[jax.ShapeDtypeStruct((T, VAL), BF16), jax.ShapeDtypeStruct((T, VAL), F32),
                   jax.ShapeDtypeStruct((nc, VAL, KEY), BF16), jax.ShapeDtypeStruct((T, CONV), BF16),
                   jax.ShapeDtypeStruct((T, CONV), F32)] + _comm_out_shapes(kinds, shards),
        in_specs=[tok(KEY, P_Q // KEY), tok(KEY, P_K // KEY), tok(VAL, P_V // VAL), tok(VAL, P_G // VAL),
                  tok(Z_PAD, P_Z // Z_PAD), _const((Z_PAD, KEY)), _const((1, KEY)), _const((1, DV)),
                  tok(CONV, P_CI // CONV), tok(CONV, P_CG // CONV), _const((HALO, CONV)), _const((1, CONV)),
                  _const((1, CONV)), _const((1, CONV)), _const((CONV, CONV))] + [ANY] * n_comm,
        out_specs=[tok(VAL, 0), tok(VAL, 0), pl.BlockSpec((cpb, VAL, KEY), lambda i: (i, 0, 0)), tok(CONV, 0),
                   tok(CONV, 0)] + [ANY] * n_comm,
        scratch_shapes=[pltpu.VMEM((VAL, KEY), F32), pltpu.VMEM((tb + HALO, CONV), F32),
                        pltpu.VMEM((SUBLANES - 1, tb + HALO - SUBLANES, CONV), F32)] + _comm_scratch(n_comm),
        compiler_params=_params(("arbitrary",)),
    )(proj, proj, proj, proj, proj, wg, bg, gn, proj, proj, conv_w, conv_b, cn_g, cn_b, gmat, *shards)
    return res[0], res[1], res[2], res[3], res[4], res[5:]


def _rms_bwd(dy, xhat, r, g):
    dyg = dy * g
    return r * (dyg - xhat * _rowmean(dyg * xhat))


def _mlp_fwd_bwd(x, mix_a, mix_c, tgt, w_out, g2, w1t, w2, gf):
    T = x.shape[0]
    tm = min(T, 256)
    inv_d = 1.0 / D_MODEL

    def body(x_ref, ma_ref, mc_ref, t_ref, wo_ref, g2_ref, w1_ref, w2_ref, gf_ref,
             dh1_ref, dmix_ref, hn_ref, ff_ref, da_ref, dh2_ref, loss_ref, dgf_ref, dg2_ref):
        @pl.when(pl.program_id(0) == 0)
        def _():
            loss_ref[...] = jnp.zeros_like(loss_ref)
            dgf_ref[...] = jnp.zeros_like(dgf_ref)
            dg2_ref[...] = jnp.zeros_like(dg2_ref)

        g2v, gfv = g2_ref[...], gf_ref[...]
        h1 = x_ref[...] + _nn(ma_ref[...], wo_ref[0:VAL, :]) + _nn(mc_ref[...], wo_ref[VAL:, :])
        r2 = lax.rsqrt(_rowmean(h1 * h1) + EPS)
        h1hat = h1 * r2
        hn = (h1hat * g2v).astype(BF16)
        hn_ref[...] = hn
        relu_a = jnp.maximum(_nt(hn, w1_ref[...]), 0.0)
        ff = (relu_a * relu_a).astype(BF16)
        ff_ref[...] = ff
        h2 = h1 + _nn(ff, w2_ref[...])
        rf = lax.rsqrt(_rowmean(h2 * h2) + EPS)
        h2hat = h2 * rf
        err = h2hat * gfv - t_ref[...]
        loss_ref[...] += (0.5 * inv_d) * _colsum(jnp.sum(err * err, axis=1, keepdims=True))
        dy = err * inv_d
        dgf_ref[...] += _colsum(dy * h2hat)
        dh2 = _rms_bwd(dy, h2hat, rf, gfv)
        dh2_b = dh2.astype(BF16)
        dh2_ref[...] = dh2_b
        da = (_nt(dh2_b, w2_ref[...]) * (2.0 * relu_a)).astype(BF16)
        da_ref[...] = da
        dhn = _nn(da, w1_ref[...])
        dg2_ref[...] += _colsum(dhn * h1hat)
        dh1 = dh2 + _rms_bwd(dhn, h1hat, r2, g2v)
        dh1_ref[...] = dh1
        dmix_ref[...] = _nt(dh1.astype(BF16), wo_ref[...]).astype(BF16)

    tok = lambda w: pl.BlockSpec((tm, w), lambda i: (i, 0))
    return pl.pallas_call(
        body, name="mlp_fwd_bwd", grid=(T // tm,),
        out_shape=[jax.ShapeDtypeStruct((T, D_MODEL), F32), jax.ShapeDtypeStruct((T, D_MODEL), BF16),
                   jax.ShapeDtypeStruct((T, D_MODEL), BF16), jax.ShapeDtypeStruct((T, D_FF), BF16),
                   jax.ShapeDtypeStruct((T, D_FF), BF16), jax.ShapeDtypeStruct((T, D_MODEL), BF16),
                   jax.ShapeDtypeStruct((1, 1), F32), jax.ShapeDtypeStruct((1, D_MODEL), F32),
                   jax.ShapeDtypeStruct((1, D_MODEL), F32)],
        in_specs=[tok(D_MODEL), tok(VAL), tok(CONV), tok(D_MODEL), _const((D_MODEL, D_MODEL)), _const((1, D_MODEL)),
                  _const((D_FF, D_MODEL)), _const((D_FF, D_MODEL)), _const((1, D_MODEL))],
        out_specs=[tok(D_MODEL), tok(D_MODEL), tok(D_MODEL), tok(D_FF), tok(D_FF), tok(D_MODEL),
                   pl.BlockSpec((1, 1), lambda i: (0, 0)), pl.BlockSpec((1, D_MODEL), lambda i: (0, 0)),
                   pl.BlockSpec((1, D_MODEL), lambda i: (0, 0))],
        compiler_params=_params(("arbitrary",)),
    )(x, mix_a, mix_c, tgt, w_out, g2, w1t, w2, gf)


def _silu_grad(v, s):
    return s * (1.0 + v * (1.0 - s))


def _conv_bwd_tile(ci_ref, cg_ref, uc_ref, dm_ref, w_ref, g_ref, be_ref, gm_ref,
                   dci_ref, dcg_ref, dw_ref, db_ref, dg_ref, dbe_ref, dbuf, dsh, dwacc, tm, nt):
    step = pl.program_id(0)
    sh_rows = tm + HALO - SUBLANES

    @pl.when(step == 0)
    def _():
        dbuf[tm:, :] = jnp.zeros((HALO, CONV), F32)
        dwacc[...] = jnp.zeros_like(dwacc)
        db_ref[...] = jnp.zeros_like(db_ref)
        dg_ref[...] = jnp.zeros_like(dg_ref)
        dbe_ref[...] = jnp.zeros_like(dbe_ref)

    gm, gv = gm_ref[...], g_ref[...]
    ucv = uc_ref[...]
    d = ucv - _group_mean(ucv, gm)
    rs = lax.rsqrt(_group_mean(d * d, gm) + EPS)
    yhat = d * rs
    yn = yhat * gv + be_ref[...]
    dyn = dm_ref[...].astype(F32) * _silu_grad(yn, jax.nn.sigmoid(yn))
    dg_ref[...] += _colsum(dyn * yhat)
    dbe_ref[...] += _colsum(dyn)
    dyh = dyn * gv
    duc = rs * (dyh - _group_mean(dyh, gm) - yhat * _group_mean(dyh * yhat, gm))
    db_ref[...] += _colsum(duc)
    dbuf[0:tm, :] = duc
    _shifted_copies(dbuf, dsh, sh_rows)

    def strip(s, carry):
        r0 = pl.multiple_of(s * STRIP, STRIP)
        rows = pl.ds(r0, STRIP)
        cin = ci_ref[rows, :].astype(F32)
        sg = jax.nn.sigmoid(cg_ref[rows, :].astype(F32))
        u = cin * sg
        du = jnp.zeros((STRIP, CONV), F32)
        for j in range(CONV_W):
            dj = _tap(dbuf, dsh, CONV_W - 1 - j, r0)
            du = du + w_ref[j:j + 1, :] * dj
            p = u * dj
            fold = p[0:SUBLANES]
            for q in range(1, STRIP // SUBLANES):
                fold = fold + p[q * SUBLANES:(q + 1) * SUBLANES, :]
            dwacc[j * SUBLANES:(j + 1) * SUBLANES, :] += fold
        dci_ref[rows, :] = (du * sg).astype(BF16)
        dcg_ref[rows, :] = (du * cin * sg * (1.0 - sg)).astype(BF16)
        return carry

    lax.fori_loop(0, tm // STRIP, strip, 0)
    dbuf[tm:, :] = dbuf[0:HALO, :]

    @pl.when(step == nt - 1)
    def _():
        dw_ref[...] = jnp.zeros_like(dw_ref)
        for j in range(CONV_W):
            dw_ref[j:j + 1, :] = _colsum(dwacc[j * SUBLANES:(j + 1) * SUBLANES, :])


def _conv_bwd(proj, uc, dmix, conv_w, cn_g, cn_b, gmat):
    T = proj.shape[0]
    tm = min(T, 512)
    nt = T // tm
    sh_rows = tm + HALO - SUBLANES

    def body(*refs):
        _conv_bwd_tile(*refs, tm, nt)

    rev = lambda i: nt - 1 - i
    tile = lambda col: pl.BlockSpec((tm, CONV), lambda i: (rev(i), col))
    acc = lambda rows: pl.BlockSpec((rows, CONV), lambda i: (0, 0))
    return pl.pallas_call(
        body, name="conv_bwd", grid=(nt,),
        out_shape=[jax.ShapeDtypeStruct((T, CONV), BF16), jax.ShapeDtypeStruct((T, CONV), BF16),
                   jax.ShapeDtypeStruct((HALO, CONV), F32), jax.ShapeDtypeStruct((1, CONV), F32),
                   jax.ShapeDtypeStruct((1, CONV), F32), jax.ShapeDtypeStruct((1, CONV), F32)],
        in_specs=[tile(P_CI // CONV), tile(P_CG // CONV), tile(0), tile(1),
                  _const((HALO, CONV)), _const((1, CONV)), _const((1, CONV)), _const((CONV, CONV))],
        out_specs=[tile(0), tile(0), acc(HALO), acc(1), acc(1), acc(1)],
        scratch_shapes=[pltpu.VMEM((tm + HALO, CONV), F32), pltpu.VMEM((SUBLANES - 1, sh_rows, CONV), F32),
                        pltpu.VMEM((HALO * SUBLANES, CONV), F32)],
        compiler_params=_params(("arbitrary",)),
    )(proj, proj, uc, dmix, conv_w, cn_g, cn_b, gmat)


def _gla_bwd_tile(q_ref, k_ref, v_ref, g_ref, z_ref, o_ref, st_ref, dm_ref, wg_ref, bg_ref, gn_ref,
                  dq_ref, dk_ref, dv_ref, dg_ref, dz_ref, dwg_ref, dbg_ref, dgn_ref, dstate, do_scr, dv_scr, tb):
    @pl.when(pl.program_id(0) == 0)
    def _():
        dstate[...] = jnp.zeros_like(dstate)
        dwg_ref[...] = jnp.zeros_like(dwg_ref)
        dbg_ref[...] = jnp.zeros_like(dbg_ref)
        dgn_ref[...] = jnp.zeros_like(dgn_ref)

    gnv = gn_ref[...]
    dgn = jnp.zeros((1, DV), F32)
    for h in range(HEADS):
        cols = slice(h * DV, (h + 1) * DV)
        oh = o_ref[:, cols]
        r = lax.rsqrt(_rowmean(oh * oh) + EPS)
        ohat = oh * r
        gh = g_ref[:, cols].astype(F32)
        sg = jax.nn.sigmoid(gh)
        dmx = dm_ref[:, cols].astype(F32)
        don = dmx * (gh * sg)
        dg_ref[:, cols] = (dmx * (ohat * gnv) * _silu_grad(gh, sg)).astype(BF16)
        dgn = dgn + _colsum(don * ohat)
        do_scr[:, cols] = _rms_bwd(don, ohat, r, gnv)
    dgn_ref[...] += dgn

    mats = _chunk_matrices()
    masks = _head_masks()
    wgv, bgv = wg_ref[...], bg_ref[...]
    n_chunks = SUB // CHUNK

    for sb in reversed(range(tb // SUB)):
        rows = slice(sb * SUB, (sb + 1) * SUB)
        zs = z_ref[rows, :]
        d = _decay_terms(zs, q_ref[rows, :], k_ref[rows, :], wgv, bgv, mats)
        qem = d.q * d.ebm
        qem_b = qem.astype(BF16)
        kem_b = (d.k * d.emb).astype(BF16)
        dq = jnp.zeros((SUB, KEY), F32)
        dk = jnp.zeros((SUB, KEY), F32)
        for h in range(HEADS):
            hm = masks[h]
            cols = slice(h * DV, (h + 1) * DV)
            do_b = do_scr[rows, cols].astype(BF16)
            vh = v_ref[rows, cols]
            da = jnp.where(mats.causal, _nt(do_b, vh), 0.0).astype(BF16)
            da_t = jnp.where(mats.causal_t, _nt(vh, do_b), 0.0).astype(BF16)
            a_t = jnp.where(mats.causal_t, _nt(kem_b, (qem * hm).astype(BF16)), 0.0).astype(BF16)
            dq = dq + hm * _nn(da, kem_b)
            dk = dk + hm * _nn(da_t, qem_b)
            dv_scr[rows, cols] = _nn(a_t, do_b)
        dq = dq * d.ebm
        dk = dk * d.emb

        qe0_b = (d.q * d.eb).astype(BF16)
        kdec_b = (d.k * d.elb).astype(BF16)
        dq_st, dk_st, last = [None] * n_chunks, [None] * n_chunks, [None] * n_chunks
        for c in reversed(range(n_chunks)):
            loc = slice(c * CHUNK, (c + 1) * CHUNK)
            glob = slice(sb * SUB + c * CHUNK, sb * SUB + (c + 1) * CHUNK)
            st_b = st_ref[sb * n_chunks + c]
            ds = dstate[...]
            ds_b = ds.astype(BF16)
            do_c = do_scr[glob, :].astype(BF16)
            ebl_c = d.ebl[c * CHUNK:c * CHUNK + 1]
            dk_c = _nn(v_ref[glob, :], ds_b) * d.elb[loc]
            dq_st[c] = _nn(do_c, st_b) * d.eb[loc]
            dk_st[c] = dk_c
            last_c = _colsum(d.k[loc] * dk_c) + ebl_c * _colsum(st_b.astype(F32) * ds)
            last[c] = jnp.broadcast_to(last_c, (CHUNK, KEY))
            dv_ref[glob, :] = (dv_scr[glob, :] + _nt(kdec_b[loc], ds_b)).astype(BF16)
            dstate[...] = ds * ebl_c + jnp.where(mats.heads, _tn(do_c, qe0_b[loc]), 0.0)
        dq = dq + jnp.concatenate(dq_st, axis=0)
        dk = dk + jnp.concatenate(dk_st, axis=0)
        dq_ref[rows, :] = (dq * Q_SCALE).astype(BF16)
        dk_ref[rows, :] = dk.astype(BF16)
        hi, lo = _split_bf16(d.q * dq - d.k * dk)
        dla = _nn(mats.tri_t, hi) + _nn(mats.tri_t, lo) + jnp.concatenate(last, axis=0)
        dal = dla * (1.0 / GATE_TAU) * jax.nn.sigmoid(-d.al)
        dal_b = dal.astype(BF16)
        dz_ref[rows, :] = _nt(dal_b, wgv).astype(BF16)
        dwg_ref[...] += _tn(zs, dal_b)
        dbg_ref[...] += _colsum(dal)


def _gla_bwd(proj, o, states, dmix, wg, bg, gn, parts):
    T = proj.shape[0]
    tb = min(T, 512)
    cpb = tb // CHUNK
    nb = T // tb
    n_comm = len(parts)
    kinds = ["exchange"] * n_comm
    comm_start, comm_wait = _hosted_comm(kinds, 11, 8, n_comm, nb)

    def body(*refs):
        comm_start(refs)
        _gla_bwd_tile(*refs[:11], *refs[11 + n_comm:19 + n_comm], *refs[19 + 2 * n_comm:22 + 2 * n_comm], tb)
        comm_wait(refs)

    rev = lambda i: nb - 1 - i
    blk = lambda w, col: pl.BlockSpec((tb, w), lambda i: (rev(i), col))
    res = pl.pallas_call(
        body, name="gla_bwd", grid=(nb,),
        out_shape=[jax.ShapeDtypeStruct((T, KEY), BF16), jax.ShapeDtypeStruct((T, KEY), BF16),
                   jax.ShapeDtypeStruct((T, VAL), BF16), jax.ShapeDtypeStruct((T, VAL), BF16),
                   jax.ShapeDtypeStruct((T, Z_PAD), BF16), jax.ShapeDtypeStruct((Z_PAD, KEY), F32),
                   jax.ShapeDtypeStruct((1, KEY), F32), jax.ShapeDtypeStruct((1, DV), F32)]
        + _comm_out_shapes(kinds, parts),
        in_specs=[blk(KEY, P_Q // KEY), blk(KEY, P_K // KEY), blk(VAL, P_V // VAL), blk(VAL, P_G // VAL),
                  blk(Z_PAD, P_Z // Z_PAD), blk(VAL, 0),
                  pl.BlockSpec((cpb, VAL, KEY), lambda i: (rev(i), 0, 0)), blk(VAL, 0),
                  _const((Z_PAD, KEY)), _const((1, KEY)), _const((1, DV))] + [ANY] * n_comm,
        out_specs=[blk(KEY, 0), blk(KEY, 0), blk(VAL, 0), blk(VAL, 0), blk(Z_PAD, 0),
                   pl.BlockSpec((Z_PAD, KEY), lambda i: (0, 0)), pl.BlockSpec((1, KEY), lambda i: (0, 0)),
                   pl.BlockSpec((1, DV), lambda i: (0, 0))] + [ANY] * n_comm,
        scratch_shapes=[pltpu.VMEM((VAL, KEY), F32), pltpu.VMEM((tb, VAL), F32), pltpu.VMEM((tb, VAL), F32)]
        + _comm_scratch(n_comm),
        compiler_params=_params(("arbitrary",)),
    )(proj, proj, proj, proj, proj, o, states, dmix, wg, bg, gn, *parts)
    return res[:8], res[8:]


def _inproj_bwd(x, g1, w_in_t, dh1, dq, dk, dv, dg, dci, dcg, dz):
    T = x.shape[0]
    tm = min(T, 512)

    def body(x_ref, g_ref, w_ref, dh1_ref, dq_ref, dk_ref, dv_ref, dg_ref, dci_ref, dcg_ref, dz_ref,
             dx_ref, dg1_ref, dp_ref):
        @pl.when(pl.program_id(0) == 0)
        def _():
            dg1_ref[...] = jnp.zeros_like(dg1_ref)

        dp_ref[:, P_Q:P_K] = dq_ref[...]
        dp_ref[:, P_K:P_V] = dk_ref[...]
        dp_ref[:, P_V:P_G] = dv_ref[...]
        dp_ref[:, P_G:P_CI] = dg_ref[...]
        dp_ref[:, P_CI:P_CG] = dci_ref[...]
        dp_ref[:, P_CG:P_Z] = dcg_ref[...]
        dp_ref[:, P_Z:] = dz_ref[...]
        dxn = (_nn(dp_ref[:, 0:P_CI], w_ref[0:OFF_Z, :]) + _nn(dp_ref[:, P_CI:P_Z], w_ref[OFF_C:D_IN, :])
               + _nn(dp_ref[:, P_Z:], w_ref[OFF_Z:OFF_Z + Z_PAD, :]))
        xv = x_ref[...]
        r = lax.rsqrt(_rowmean(xv * xv) + EPS)
        xhat = xv * r
        dg1_ref[...] += _colsum(dxn * xhat)
        dx_ref[...] = dh1_ref[...] + _rms_bwd(dxn, xhat, r, g_ref[...])

    tok = lambda w: pl.BlockSpec((tm, w), lambda i: (i, 0))
    return pl.pallas_call(
        body, name="inproj_bwd", grid=(T // tm,),
        out_shape=[jax.ShapeDtypeStruct((T, D_MODEL), F32), jax.ShapeDtypeStruct((1, D_MODEL), F32)],
        in_specs=[tok(D_MODEL), _const((1, D_MODEL)), _const((D_IN, D_MODEL)), tok(D_MODEL), tok(KEY), tok(KEY),
                  tok(VAL), tok(VAL), tok(CONV), tok(CONV), tok(Z_PAD)],
        out_specs=[tok(D_MODEL), pl.BlockSpec((1, D_MODEL), lambda i: (0, 0))],
        scratch_shapes=[pltpu.VMEM((tm, D_INP), BF16)],
        compiler_params=_params(("arbitrary",)),
    )(x, g1, w_in_t, dh1, dq, dk, dv, dg, dci, dcg, dz)


def _wgrad_in(xn, dq, dk, dv, dg, dz, dci, dcg):
    T = xn.shape[0]
    tt = min(T, 2048)
    nt = T // tt
    pieces = [dq, dk, dv, dg, dz, dci, dcg]
    rows = [KEY, KEY, VAL, VAL, RANK, CONV, CONV]
    assert sum(rows) == D_IN

    def body(*refs):
        xn_ref, piece_refs, o_ref, acc = refs[0], refs[1:1 + len(pieces)], refs[-2], refs[-1]

        @pl.when(pl.program_id(0) == 0)
        def _():
            acc[...] = jnp.zeros_like(acc)

        xv = xn_ref[...]
        row = 0
        for ref, n in zip(piece_refs, rows):
            acc[row:row + n, :] += _tn(ref[...], xv)[0:n]
            row += n

        @pl.when(pl.program_id(0) == nt - 1)
        def _():
            o_ref[...] = acc[...].astype(BF16)

    tok = lambda w: pl.BlockSpec((tt, w), lambda t: (t, 0))
    return pl.pallas_call(
        body, name="wgrad_in", grid=(nt,), out_shape=jax.ShapeDtypeStruct((D_IN, D_MODEL), BF16),
        in_specs=[tok(D_MODEL)] + [tok(p.shape[1]) for p in pieces],
        out_specs=pl.BlockSpec((D_IN, D_MODEL), lambda t: (0, 0), pipeline_mode=pl.Buffered(1)),
        scratch_shapes=[pltpu.VMEM((D_IN, D_MODEL), F32)],
        compiler_params=_params(("arbitrary",)),
    )(xn, *pieces)


def _wgrad_out(mix_a, mix_c, dh1):
    T = dh1.shape[0]
    tt = min(T, 2048)
    nt = T // tt

    def body(a_ref, c_ref, b_ref, o_ref, acc):
        @pl.when(pl.program_id(0) == 0)
        def _():
            acc[...] = jnp.zeros_like(acc)

        b = b_ref[...].astype(BF16)
        acc[0:VAL, :] += _tn(a_ref[...], b)
        acc[VAL:, :] += _tn(c_ref[...], b)

        @pl.when(pl.program_id(0) == nt - 1)
        def _():
            o_ref[...] = acc[...].astype(BF16)

    tok = lambda w: pl.BlockSpec((tt, w), lambda t: (t, 0))
    return pl.pallas_call(
        body, name="wgrad_out", grid=(nt,), out_shape=jax.ShapeDtypeStruct((D_MODEL, D_MODEL), BF16),
        in_specs=[tok(VAL), tok(CONV), tok(D_MODEL)],
        out_specs=pl.BlockSpec((D_MODEL, D_MODEL), lambda t: (0, 0)),
        scratch_shapes=[pltpu.VMEM((D_MODEL, D_MODEL), F32)],
        compiler_params=_params(("arbitrary",)),
    )(mix_a, mix_c, dh1)


def _wgrad(a, b, name, tk, tn, col_block=None):
    T, K = a.shape
    N = b.shape[1]
    tt = min(T, 4096)
    nt = T // tt

    def body(a_ref, b_ref, o_ref, acc):
        @pl.when(pl.program_id(2) == 0)
        def _():
            acc[...] = jnp.zeros_like(acc)

        acc[...] += _tn(a_ref[...], b_ref[...].astype(BF16))

        @pl.when(pl.program_id(2) == nt - 1)
        def _():
            if col_block is None:
                o_ref[...] = acc[...].astype(BF16)
            else:
                for q in range(tn // col_block):
                    o_ref[q] = acc[:, q * col_block:(q + 1) * col_block].astype(BF16)

    if col_block is None:
        out_shape = jax.ShapeDtypeStruct((K, N), BF16)
        out_spec = pl.BlockSpec((tk, tn), lambda i, j, t: (i, j))
    else:
        assert tk == K
        out_shape = jax.ShapeDtypeStruct((N // col_block, K, col_block), BF16)
        out_spec = pl.BlockSpec((tn // col_block, tk, col_block), lambda i, j, t: (j, 0, 0))
    return pl.pallas_call(
        body, name=name, grid=(K // tk, N // tn, nt), out_shape=out_shape,
        in_specs=[pl.BlockSpec((tt, tk), lambda i, j, t: (t, i)), pl.BlockSpec((tt, tn), lambda i, j, t: (t, j))],
        out_specs=out_spec, scratch_shapes=[pltpu.VMEM((tk, tn), F32)],
        compiler_params=_params(("arbitrary", "arbitrary", "arbitrary")),
    )(a, b)


def _adam_math(w, g, m, v):
    m = ADAM_B1 * m + (1.0 - ADAM_B1) * g
    v = ADAM_B2 * v + (1.0 - ADAM_B2) * (g * g)
    m_hat = m / (1.0 - ADAM_B1 ** ADAM_STEP)
    v_hat = v / (1.0 - ADAM_B2 ** ADAM_STEP)
    delta = -ADAM_LR * (m_hat / (jnp.sqrt(v_hat) + ADAM_EPS) + ADAM_WD * w)
    return delta, m, v


def _sum8(ref):
    g = ref[0].astype(F32)
    for s in range(1, N_DEV):
        g = g + ref[s].astype(F32)
    return g


def _adam_big(parts, w, m, v, name):
    R, C = w.shape
    tr = 128 if R % 128 == 0 else R

    def body(p_ref, w_ref, m_ref, v_ref, g_ref, d_ref, nm_ref, nv_ref):
        g = _sum8(p_ref)
        g_ref[...] = g
        d_ref[...], nm_ref[...], nv_ref[...] = _adam_math(w_ref[...], g, m_ref[...], v_ref[...])

    row = pl.BlockSpec((tr, C), lambda i: (i, 0))
    return pl.pallas_call(
        body, name=name, grid=(R // tr,), out_shape=[jax.ShapeDtypeStruct((R, C), F32)] * 4,
        in_specs=[pl.BlockSpec((N_DEV, tr, C), lambda i: (0, i, 0)), row, row, row], out_specs=[row] * 4,
        compiler_params=_params(("arbitrary",)),
    )(parts, w, m, v)


def _sum_small(parts):
    def body(p_ref, o_ref):
        o_ref[...] = _sum8(p_ref)

    return pl.pallas_call(body, name="sum_small", out_shape=jax.ShapeDtypeStruct(parts.shape[1:], F32))(parts)


def _adam_small(gs, ws, ms, vs):
    n = len(gs)

    def body(*refs):
        g_refs, w_refs, m_refs, v_refs = refs[:n], refs[n:2 * n], refs[2 * n:3 * n], refs[3 * n:4 * n]
        outs = refs[4 * n:]
        for i in range(n):
            d, nm, nv = _adam_math(w_refs[i][...], g_refs[i][...], m_refs[i][...], v_refs[i][...])
            outs[i][...] = d
            outs[n + i][...] = nm
            outs[2 * n + i][...] = nv

    shapes = [jax.ShapeDtypeStruct(w.shape, F32) for w in ws]
    res = pl.pallas_call(body, name="adam_small", out_shape=shapes * 3)(*gs, *ws, *ms, *vs)
    return res[:n], res[n:2 * n], res[2 * n:]


def _group_matrix():
    gi = lax.broadcasted_iota(jnp.int32, (CONV, CONV), 0) // (CONV // GROUPS)
    gj = lax.broadcasted_iota(jnp.int32, (CONV, CONV), 1) // (CONV // GROUPS)
    return jnp.where(gi == gj, GROUPS / CONV, 0.0).astype(BF16)


_SMALL = [("loss", 8), ("dg1", 8), ("dbg", 2), ("dgn", 1), ("dconv_b", 4), ("dcn_g", 4), ("dcn_b", 4), ("dg2", 8),
          ("dgf", 8), ("dwg", 32), ("dconv_w", 124)]


def _pad8(rows):
    return -(-rows // 8) * 8


def kernel(x, norm1_g, w_in, w_gate_up, b_gate, gla_norm_g, conv_w, conv_b, conv_norm_g, conv_norm_b, w_out, norm2_g, w_mlp_in, w_mlp_out, final_norm_g, loss_target, m_norm1_g, m_w_in, m_w_gate_up, m_b_gate, m_gla_norm_g, m_conv_w, m_conv_b, m_conv_norm_g, m_conv_norm_b, m_w_out, m_norm2_g, m_w_mlp_in, m_w_mlp_out, m_final_norm_g, v_norm1_g, v_w_in, v_w_gate_up, v_b_gate, v_gla_norm_g, v_conv_w, v_conv_b, v_conv_norm_g, v_conv_norm_b, v_w_out, v_norm2_g, v_w_mlp_in, v_w_mlp_out, v_final_norm_g):
    x_idx = lax.axis_index("x")
    y_idx = lax.axis_index("y")
    c_idx = lax.axis_index("c")
    me = 4 * x_idx + 2 * y_idx + c_idx
    xs, tgt = x[0], loss_target[0]
    gf = final_norm_g.reshape(1, D_MODEL)
    gmat = _group_matrix()

    small_shard = jnp.zeros((48, 128), F32)
    small_shard = small_shard.at[0:RANK, 0:KEY // N_DEV].set(w_gate_up[0])
    small_shard = small_shard.at[RANK:RANK + CONV_W, 0:CONV // N_DEV].set(conv_w[0])
    g_in, g_small = _gather_two_level([w_in[0].T.astype(BF16), small_shard], "gather_w_in")
    w_in_t = g_in.reshape(D_IN, D_MODEL)
    wg_full = jnp.concatenate([g_small[d, 0:RANK, 0:KEY // N_DEV] for d in range(N_DEV)], axis=1)
    wg_pad = jnp.pad(wg_full, ((0, Z_PAD - RANK), (0, 0))).astype(BF16)
    conv_w_full = jnp.concatenate([g_small[d, RANK:RANK + CONV_W, 0:CONV // N_DEV] for d in range(N_DEV)], axis=1)
    conv_w_pad = jnp.pad(conv_w_full, ((0, HALO - CONV_W), (0, 0)))

    proj, xn, (g_w2,) = _inproj_fwd(xs, norm1_g, w_in_t, [w_mlp_out[0].astype(BF16)])
    mix_a, o, states, mix_c, uc, (g_out, g_w1) = _mix_fwd(
        proj, wg_pad, b_gate, gla_norm_g, conv_w_pad, conv_b, conv_norm_g, conv_norm_b, gmat,
        [w_out[0].astype(BF16), w_mlp_in[0].T.astype(BF16)])
    w_out_full = g_out.reshape(D_MODEL, D_MODEL)
    w1t_full = g_w1.reshape(D_FF, D_MODEL)
    w2_full = g_w2.reshape(D_FF, D_MODEL)
    dh1, dmix, hn, ff, da, dh2, loss, dgf, dg2 = _mlp_fwd_bwd(xs, mix_a, mix_c, tgt, w_out_full, norm2_g, w1t_full,
                                                              w2_full, gf)

    dw1 = _wgrad(hn, da, "wgrad_mlp_in", 1024, 1024, col_block=D_FF // N_DEV)
    dw2 = _wgrad(ff, dh2, "wgrad_mlp_out", 1024, 1024)
    dw_out = _wgrad_out(mix_a, mix_c, dh1)
    dci, dcg, dconv_w, dconv_b, dcn_g, dcn_b = _conv_bwd(proj, uc, dmix, conv_w_pad, conv_norm_g, conv_norm_b, gmat)
    (dq, dk, dv, dg, dz, dwg, dbg, dgn), (p_w1, p_w2, p_out) = _gla_bwd(
        proj, o, states, dmix, wg_pad, b_gate, gla_norm_g,
        [dw1, dw2.reshape(N_DEV, D_FF // N_DEV, D_MODEL), dw_out.reshape(N_DEV, D_MODEL // N_DEV, D_MODEL)])
    dw_in = _wgrad_in(xn, dq, dk, dv, dg, dz, dci, dcg).reshape(N_DEV, SHARD_IN, D_MODEL)
    send_sems, recv_sems, dw_in_thru, land, token = _split_start("exchange", dw_in, jnp.copy(dw_in),
                                                                 "exchange_w_in_start")
    dx, dg1 = _inproj_bwd(xs, norm1_g + token[0:1, 0:1], w_in_t, dh1, dq, dk, dv, dg, dci, dcg, dz)
    p_in = _split_wait("exchange", send_sems, recv_sems, dw_in_thru, land, dg1, "exchange_w_in_wait")

    small = dict(loss=jnp.zeros((8, 128), F32) + loss, dg1=dg1, dbg=dbg, dgn=dgn, dconv_b=dconv_b, dcn_g=dcn_g,
                 dcn_b=dcn_b, dg2=dg2, dgf=dgf, dwg=dwg[0:RANK], dconv_w=dconv_w[0:CONV_W])
    pack = jnp.concatenate([jnp.pad(small[name].reshape(rows, 128), ((0, _pad8(rows) - rows), (0, 0)))
                            for name, rows in _SMALL], axis=0)
    s_send, s_recv, pack_thru, pack_land, s_token = _split_start(
        "gather", pack, jnp.broadcast_to(pack, (N_DEV,) + pack.shape) + 0.0, "gather_small_start")

    gi, di, mi, vi = _adam_big(p_in, w_in[0].T, m_w_in[0].T, v_w_in[0].T, "adam_w_in")
    go, do, mo, vo = _adam_big(p_out, w_out[0] + s_token[0:1, 0:1], m_w_out[0], v_w_out[0], "adam_w_out")
    ga, da_, ma, va = _adam_big(p_w1, w_mlp_in[0], m_w_mlp_in[0], v_w_mlp_in[0], "adam_w_mlp_in")
    gb, db, mb, vb = _adam_big(p_w2, w_mlp_out[0], m_w_mlp_out[0], v_w_mlp_out[0], "adam_w_mlp_out")
    cut = lambda a: a.T[None]

    g_pack = _split_wait("gather", s_send, s_recv, pack_thru, pack_land, go[0:8, 0:128] + ga[0:8, 0:128]
                         + gb[0:8, 0:128], "gather_small_wait")
    summed = _sum_small(g_pack)
    small_g = {}
    at = 0
    for name, rows in _SMALL:
        small_g[name] = summed[at:at + rows]
        at += _pad8(rows)
    loss_out = small_g["loss"][0, 0]
    wg_cols = KEY // N_DEV
    cw_cols = CONV // N_DEV
    g_small_list = [
        small_g["dg1"].reshape(1, D_MODEL),
        lax.dynamic_slice_in_dim(small_g["dwg"].reshape(RANK, KEY), me * wg_cols, wg_cols, axis=1)[None],
        small_g["dbg"].reshape(1, KEY), small_g["dgn"].reshape(1, DV),
        lax.dynamic_slice_in_dim(small_g["dconv_w"].reshape(CONV_W, CONV), me * cw_cols, cw_cols, axis=1)[None],
        small_g["dconv_b"].reshape(1, CONV), small_g["dcn_g"].reshape(1, CONV), small_g["dcn_b"].reshape(1, CONV),
        small_g["dg2"].reshape(1, D_MODEL), small_g["dgf"].reshape(1, D_MODEL),
    ]
    row = lambda a: a.reshape(1, D_MODEL)
    w_small = [norm1_g, w_gate_up, b_gate, gla_norm_g, conv_w, conv_b, conv_norm_g, conv_norm_b, norm2_g,
               row(final_norm_g)]
    m_small = [m_norm1_g, m_w_gate_up, m_b_gate, m_gla_norm_g, m_conv_w, m_conv_b, m_conv_norm_g, m_conv_norm_b,
               m_norm2_g, row(m_final_norm_g)]
    v_small = [v_norm1_g, v_w_gate_up, v_b_gate, v_gla_norm_g, v_conv_w, v_conv_b, v_conv_norm_g, v_conv_norm_b,
               v_norm2_g, row(v_final_norm_g)]
    d_small, nm_small, nv_small = _adam_small(g_small_list, w_small, m_small, v_small)
    flat = lambda lst: list(lst[:-1]) + [lst[-1].reshape(D_MODEL)]
    g_small_list, d_small, nm_small, nv_small = flat(g_small_list), flat(d_small), flat(nm_small), flat(nv_small)

    def order(s, w_in_v, w_out_v, w1_v, w2_v):
        return [s[0], w_in_v, s[1], s[2], s[3], s[4], s[5], s[6], s[7], w_out_v, s[8], w1_v, w2_v, s[9]]

    grads = order(g_small_list, cut(gi), go[None], ga[None], gb[None])
    deltas = order(d_small, cut(di), do[None], da_[None], db[None])
    new_m = order(nm_small, cut(mi), mo[None], ma[None], mb[None])
    new_v = order(nv_small, cut(vi), vo[None], va[None], vb[None])
    return (loss_out, dx[None], *grads, *deltas, *new_m, *new_v)
```

```python
from typing import NamedTuple

import jax
import jax.numpy as jnp
from jax import lax
from jax.experimental import pallas as pl
from jax.experimental.pallas import tpu as pltpu

F32 = jnp.float32
BF16 = jnp.bfloat16

N_DEV = 8
D_MODEL = 1024
HEADS = 4
DK = 64
DV = 128
KEY = HEADS * DK
VAL = HEADS * DV
RANK = 16
CONV = 512
GROUPS = 8
CONV_W = 31
HALO = 32
SUBLANES = 8
LANES = 128
STRIP = 32
FWD_STRIP = 16
TOKEN_TILE = 512
MLP_TILE = 256
WGRAD_TILE = 4096
WGRAD_BLOCK = 1024
ADAM_ROWS = 128
D_FF = 4096
D_IN = 2576
SHARD_IN = D_IN // N_DEV
CHUNK = 64
SUB = 256
EPS = 1e-6
GATE_TAU = 16.0
Q_SCALE = DK ** -0.5

P_Q, P_K, P_V, P_G, P_CI, P_CG, P_Z = 0, 256, 512, 1024, 1536, 2048, 2560
D_INP = 2688
Z_PAD = D_INP - P_Z
OFF_Z = 1536
OFF_C = OFF_Z + RANK

ADAM_LR = 0.001
ADAM_B1 = 0.9
ADAM_B2 = 0.999
ADAM_EPS = 1e-08
ADAM_WD = 0.01
ADAM_STEP = 10

VMEM_LIMIT = 56 * 1024 * 1024

MESH = pl.DeviceIdType.MESH
ANY = pl.BlockSpec(memory_space=pl.ANY)


def _nn(a, b):
    return jnp.dot(a, b, preferred_element_type=F32)


def _nt(a, b):
    return lax.dot_general(a, b, (((1,), (1,)), ((), ())), preferred_element_type=F32)


def _tn(a, b):
    return lax.dot_general(a, b, (((0,), (0,)), ((), ())), preferred_element_type=F32)


def _params(sem=None):
    return pltpu.CompilerParams(dimension_semantics=sem, vmem_limit_bytes=VMEM_LIMIT)


def _const(shape):
    return pl.BlockSpec(shape, lambda *_: (0,) * len(shape), pipeline_mode=pl.Buffered(1))


def _colsum(v):
    return jnp.sum(v, axis=0, keepdims=True)


def _rowmean(v):
    return jnp.mean(v, axis=-1, keepdims=True)


def _split_bf16(v):
    hi = v.astype(BF16)
    return hi, (v - hi.astype(F32)).astype(BF16)


def _my_place():
    return lax.axis_index("x"), lax.axis_index("y"), lax.axis_index("c")


def _peer(j):
    x, y, c = _my_place()
    jx, jy, jc = (j >> 2) & 1, (j >> 1) & 1, j & 1
    px = 1 - x if jx else x
    py = 1 - y if jy else y
    pc = 1 - c if jc else c
    return (px, py, pc), 4 * px + 2 * py + pc


def _comm_plan(kinds, ins, outs, send_sems, recv_sems, local_sems, receives=True):
    x, y, c = _my_place()
    me = 4 * x + 2 * y + c
    own = lambda k, idx: ins[k] if kinds[k] == "gather" else ins[k].at[idx]
    local = [pltpu.make_async_copy(own(k, me), outs[k].at[me], local_sems.at[k]) for k in range(len(kinds))]
    sends, recvs = [], []
    for j in range(1, N_DEV):
        peer, peer_idx = _peer(j)
        for k in range(len(kinds)):
            sems = dict(send_sem=send_sems.at[k, j - 1], recv_sem=recv_sems.at[k, j - 1], device_id=peer,
                        device_id_type=MESH)
            sends.append(pltpu.make_async_remote_copy(src_ref=own(k, peer_idx), dst_ref=outs[k].at[me], **sems))
            if receives:
                recvs.append(pltpu.make_async_remote_copy(src_ref=own(k, me), dst_ref=outs[k].at[peer_idx], **sems))
    return local, sends, recvs


def _comm_start(plan):
    local, sends, _ = plan
    for cp in local + sends:
        cp.start()


def _comm_wait(plan):
    local, sends, recvs = plan
    for cp in recvs:
        cp.wait_recv()
    for cp in sends:
        cp.wait_send()
    for cp in local:
        cp.wait()


def _comm_scratch(n):
    return [pltpu.SemaphoreType.DMA((n, N_DEV - 1)), pltpu.SemaphoreType.DMA((n, N_DEV - 1)),
            pltpu.SemaphoreType.DMA((n,))]


def _comm_out_shapes(kinds, arrays):
    return [jax.ShapeDtypeStruct(((N_DEV,) + a.shape) if kind == "gather" else a.shape, a.dtype)
            for kind, a in zip(kinds, arrays)]


def _gather_two_level(shards, name):
    n = len(shards)

    def body(*refs):
        ins, outs = refs[:n], refs[n:2 * n]
        send_sems, recv_sems, local_sems = refs[2 * n:]
        x, y, c = _my_place()
        index = lambda px, py, pc: 4 * px + 2 * py + pc
        me, sibling = (x, y, c), (x, y, 1 - c)
        chips = [(1 - x, y), (x, 1 - y), (1 - x, 1 - y)]

        def copy(k, slot, block, to, src=None):
            rows = outs[k].at[index(*block)]
            return pltpu.make_async_remote_copy(
                src_ref=rows if src is None else src, dst_ref=rows, send_sem=send_sems.at[k, slot],
                recv_sem=recv_sems.at[k, slot], device_id=to, device_id_type=MESH)

        local = [pltpu.make_async_copy(ins[k], outs[k].at[index(*me)], local_sems.at[k]) for k in range(n)]
        first = []
        for k in range(n):
            first.append(copy(k, 0, me, sibling, src=ins[k]))
            first += [copy(k, 1 + j, me, (*chip, c), src=ins[k]) for j, chip in enumerate(chips)]
        for cp in local + first:
            cp.start()
        passed = []
        for j, chip in enumerate(chips):
            for k in range(n):
                copy(k, 1 + j, (*chip, c), me).wait_recv()
                cp = copy(k, 4 + j, (*chip, c), sibling)
                cp.start()
                passed.append(cp)
        for k in range(n):
            copy(k, 0, sibling, me).wait_recv()
        for j, chip in enumerate(chips):
            for k in range(n):
                copy(k, 4 + j, (*chip, 1 - c), me).wait_recv()
        for cp in first + passed:
            cp.wait_send()
        for cp in local:
            cp.wait()

    return pl.pallas_call(
        body, name=name, out_shape=_comm_out_shapes(["gather"] * n, shards), in_specs=[ANY] * n, out_specs=[ANY] * n,
        scratch_shapes=_comm_scratch(n),
    )(*shards)


HBM = pl.BlockSpec(memory_space=pltpu.HBM)
SEM = pl.BlockSpec(memory_space=pltpu.SEMAPHORE)
DATAFLOW = pltpu.SideEffectType.DATAFLOW_SIDE_EFFECTING


def _split_start(kind, part, land, name):
    def body(src_ref, land_ref, send_sems, recv_sems, src_thru, land_thru, token):
        x, y, c = _my_place()
        me = 4 * x + 2 * y + c
        for j in range(1, N_DEV):
            peer, peer_idx = _peer(j)
            pltpu.make_async_remote_copy(
                src_ref=src_ref.at[peer_idx] if kind == "exchange" else src_ref, dst_ref=land_ref.at[me],
                send_sem=send_sems.at[j - 1], recv_sem=recv_sems.at[j - 1], device_id=peer,
                device_id_type=MESH).start()
        token[...] = jnp.zeros_like(token)

    return pl.pallas_call(
        body, name=name,
        out_shape=(pltpu.SemaphoreType.DMA((N_DEV - 1,)), pltpu.SemaphoreType.DMA((N_DEV - 1,)),
                   pltpu.HBM(part.shape, part.dtype), pltpu.HBM(land.shape, land.dtype),
                   jax.ShapeDtypeStruct((SUBLANES, LANES), F32)),
        in_specs=(HBM, HBM), out_specs=(SEM, SEM, HBM, HBM, pl.BlockSpec(memory_space=pltpu.VMEM)),
        input_output_aliases={0: 2, 1: 3},
        compiler_params=pltpu.CompilerParams(has_side_effects=DATAFLOW),
    )(pltpu.with_memory_space_constraint(part, pltpu.HBM), pltpu.with_memory_space_constraint(land, pltpu.HBM))


def _split_wait(kind, send_sems, recv_sems, part_thru, land_thru, after, name):
    def body(src_ref, land_ref, send_sems, recv_sems, after_ref, src_dead, got_ref):
        x, y, c = _my_place()
        me = 4 * x + 2 * y + c
        own = lambda idx: src_ref.at[idx] if kind == "exchange" else src_ref
        for j in range(1, N_DEV):
            peer, peer_idx = _peer(j)
            sems = dict(send_sem=send_sems.at[j - 1], recv_sem=recv_sems.at[j - 1], device_id=peer,
                        device_id_type=MESH)
            pltpu.make_async_remote_copy(src_ref=own(peer_idx), dst_ref=land_ref.at[me], **sems).wait_send()
            pltpu.make_async_remote_copy(src_ref=own(me), dst_ref=land_ref.at[peer_idx], **sems).wait_recv()

    return pl.pallas_call(
        body, name=name,
        out_shape=(pltpu.HBM(part_thru.shape, part_thru.dtype), pltpu.HBM(land_thru.shape, land_thru.dtype)),
        in_specs=(HBM, HBM, SEM, SEM, ANY), out_specs=(HBM, HBM), input_output_aliases={0: 0, 1: 1},
        compiler_params=pltpu.CompilerParams(has_side_effects=DATAFLOW),
    )(part_thru, land_thru, send_sems, recv_sems, after)[1]


def _hosted_comm(kinds, n_in, n_out, n_comm, n_steps):
    def plan_of(refs, receives):
        ins = refs[n_in:n_in + n_comm]
        outs = refs[n_in + n_comm + n_out:n_in + 2 * n_comm + n_out]
        return _comm_plan(kinds, ins, outs, *refs[-3:], receives=receives)

    def start(refs):
        @pl.when(pl.program_id(0) == 0)
        def _():
            _comm_start(plan_of(refs, False))

    def wait(refs):
        @pl.when(pl.program_id(0) == n_steps - 1)
        def _():
            _comm_wait(plan_of(refs, True))

    return start, wait


def _z_lanes():
    return lax.broadcasted_iota(jnp.int32, (1, Z_PAD), 1) < RANK


def _inproj_fwd(x, g1, w_in_t, shards):
    T = x.shape[0]
    tm = min(T, TOKEN_TILE)
    n_comm = len(shards)
    kinds = ["gather"] * n_comm
    comm_start, comm_wait = _hosted_comm(kinds, 3, 2, n_comm, T // tm)

    def body(*refs):
        x_ref, g_ref, w_ref = refs[:3]
        proj_ref, xn_ref = refs[3 + n_comm:5 + n_comm]
        comm_start(refs)
        xv = x_ref[...]
        r = lax.rsqrt(_rowmean(xv * xv) + EPS)
        xn = (xv * r * g_ref[...]).astype(BF16)
        xn_ref[...] = xn
        proj_ref[:, 0:P_CI] = _nt(xn, w_ref[0:OFF_Z, :]).astype(BF16)
        proj_ref[:, P_CI:P_Z] = _nt(xn, w_ref[OFF_C:D_IN, :]).astype(BF16)
        proj_ref[:, P_Z:] = jnp.where(_z_lanes(), _nt(xn, w_ref[OFF_Z:OFF_Z + Z_PAD, :]), 0.0).astype(BF16)
        comm_wait(refs)

    res = pl.pallas_call(
        body, name="inproj_fwd", grid=(T // tm,),
        out_shape=[jax.ShapeDtypeStruct((T, D_INP), BF16), jax.ShapeDtypeStruct((T, D_MODEL), BF16)]
        + _comm_out_shapes(kinds, shards),
        in_specs=[pl.BlockSpec((tm, D_MODEL), lambda i: (i, 0)), _const((1, D_MODEL)), _const((D_IN, D_MODEL))]
        + [ANY] * n_comm,
        out_specs=[pl.BlockSpec((tm, D_INP), lambda i: (i, 0)), pl.BlockSpec((tm, D_MODEL), lambda i: (i, 0))]
        + [ANY] * n_comm,
        scratch_shapes=_comm_scratch(n_comm),
        compiler_params=_params(("arbitrary",)),
    )(x, g1, w_in_t, *shards)
    return res[0], res[1], res[2:]


def _head_masks():
    lane = lax.broadcasted_iota(jnp.int32, (1, KEY), 1)
    return [((lane >= h * DK) & (lane < (h + 1) * DK)).astype(F32) for h in range(HEADS)]


class _Mats(NamedTuple):
    tri: jax.Array
    tri_t: jax.Array
    same: jax.Array
    mid: jax.Array
    causal: jax.Array
    causal_t: jax.Array
    heads: jax.Array


def _chunk_matrices():
    r = lax.broadcasted_iota(jnp.int32, (SUB, SUB), 0)
    c = lax.broadcasted_iota(jnp.int32, (SUB, SUB), 1)
    shift = CHUNK.bit_length() - 1
    same = jnp.right_shift(r, shift) == jnp.right_shift(c, shift)
    causal = same & (r >= c)
    causal_t = same & (r <= c)
    mid = same & ((c & (CHUNK - 1)) < CHUNK // 2)
    hr = jnp.right_shift(lax.broadcasted_iota(jnp.int32, (VAL, KEY), 0), DV.bit_length() - 1)
    hc = jnp.right_shift(lax.broadcasted_iota(jnp.int32, (VAL, KEY), 1), DK.bit_length() - 1)
    return _Mats(tri=causal.astype(BF16), tri_t=causal_t.astype(BF16), same=same.astype(BF16), mid=mid.astype(BF16),
                 causal=causal, causal_t=causal_t, heads=hr == hc)


class _Decay(NamedTuple):
    al: jax.Array
    q: jax.Array
    k: jax.Array
    eb: jax.Array
    ebm: jax.Array
    emb: jax.Array
    elb: jax.Array
    ebl: jax.Array


def _decay_terms(z, q, k, wg, bg, mats):
    al = _nn(z, wg) + bg
    la = (jnp.minimum(al, 0.0) - jnp.log(1.0 + jnp.exp(-jnp.abs(al)))) * (1.0 / GATE_TAU)
    hi, lo = _split_bf16(la)
    cum = lambda m: _nn(m, hi) + _nn(m, lo)
    b, b_last, b_mid = cum(mats.tri), cum(mats.same), cum(mats.mid)
    return _Decay(al=al, q=q.astype(F32) * Q_SCALE, k=k.astype(F32), eb=jnp.exp(b), ebm=jnp.exp(b - b_mid),
                  emb=jnp.exp(b_mid - b), elb=jnp.exp(b_last - b), ebl=jnp.exp(b_last))


def _gla_fwd_tile(q_ref, k_ref, v_ref, g_ref, z_ref, wg_ref, bg_ref, gn_ref, mix_ref, o_ref, st_ref, state, tb):
    @pl.when(pl.program_id(0) == 0)
    def _():
        state[...] = jnp.zeros_like(state)

    mats = _chunk_matrices()
    masks = _head_masks()
    wgv, bgv = wg_ref[...], bg_ref[...]

    for sb in range(tb // SUB):
        rows = slice(sb * SUB, (sb + 1) * SUB)
        d = _decay_terms(z_ref[rows, :], q_ref[rows, :], k_ref[rows, :], wgv, bgv, mats)
        kem_b = (d.k * d.emb).astype(BF16)
        qem = d.q * d.ebm
        for h in range(HEADS):
            cols = slice(h * DV, (h + 1) * DV)
            a = jnp.where(mats.causal, _nt((qem * masks[h]).astype(BF16), kem_b), 0.0)
            o_ref[rows, cols] = _nn(a.astype(BF16), v_ref[rows, cols])
        qe0_b = (d.q * d.eb).astype(BF16)
        kdec_b = (d.k * d.elb).astype(BF16)
        for c in range(SUB // CHUNK):
            loc = slice(c * CHUNK, (c + 1) * CHUNK)
            glob = slice(sb * SUB + c * CHUNK, sb * SUB + (c + 1) * CHUNK)
            st = state[...]
            st_b = st.astype(BF16)
            st_ref[sb * (SUB // CHUNK) + c] = st_b
            o_ref[glob, :] += _nt(qe0_b[loc], st_b)
            u = _tn(v_ref[glob, :], kdec_b[loc])
            state[...] = st * d.ebl[c * CHUNK:c * CHUNK + 1] + jnp.where(mats.heads, u, 0.0)

    gnv = gn_ref[...]
    for h in range(HEADS):
        cols = slice(h * DV, (h + 1) * DV)
        oh = o_ref[:, cols]
        r = lax.rsqrt(_rowmean(oh * oh) + EPS)
        gh = g_ref[:, cols].astype(F32)
        mix_ref[:, cols] = (oh * r * gnv * (gh * jax.nn.sigmoid(gh))).astype(BF16)


def _group_mean(v, gmat):
    return _nn(v.astype(BF16), gmat)


def _shifted_copies(buf, sh, rows):
    for k in range(1, SUBLANES):
        sh[k - 1] = buf[pl.ds(k, rows), :]


def _tap(buf, sh, off, r0, n):
    k, base = off % SUBLANES, off - off % SUBLANES
    rows = pl.ds(r0 + base if isinstance(r0, int) else pl.multiple_of(r0 + base, SUBLANES), n)
    return buf[rows, :] if k == 0 else sh[k - 1, rows, :]


def _conv_fwd_tile(ci_ref, cg_ref, w_ref, b_ref, g_ref, be_ref, gm_ref, mix_ref, uc_ref, ubuf, ush, tm):
    sh_rows = tm + HALO - SUBLANES

    @pl.when(pl.program_id(0) == 0)
    def _():
        ubuf[0:HALO, :] = jnp.zeros((HALO, CONV), F32)

    ubuf[HALO:, :] = ci_ref[...].astype(F32) * jax.nn.sigmoid(cg_ref[...].astype(F32))
    _shifted_copies(ubuf, ush, sh_rows)
    for s in range(tm // FWD_STRIP):
        acc = jnp.zeros((FWD_STRIP, CONV), F32) + b_ref[...]
        for j in range(CONV_W):
            acc = acc + w_ref[j:j + 1, :] * _tap(ubuf, ush, HALO - (CONV_W - 1) + j, s * FWD_STRIP, FWD_STRIP)
        uc_ref[s * FWD_STRIP:(s + 1) * FWD_STRIP, :] = acc
    ubuf[0:HALO, :] = ubuf[tm:tm + HALO, :]
    gm = gm_ref[...]
    ucv = uc_ref[...]
    d = ucv - _group_mean(ucv, gm)
    var = _group_mean(d * d, gm)
    yn = d * lax.rsqrt(var + EPS) * g_ref[...] + be_ref[...]
    mix_ref[...] = (yn * jax.nn.sigmoid(yn)).astype(BF16)


def _mix_fwd(proj, wg, bg, gn, conv_w, conv_b, cn_g, cn_b, gmat, shards):
    T = proj.shape[0]
    tb = min(T, TOKEN_TILE)
    cpb = tb // CHUNK
    n_comm = len(shards)
    kinds = ["gather"] * n_comm
    comm_start, comm_wait = _hosted_comm(kinds, 15, 5, n_comm, T // tb)

    def body(*refs):
        gla_in, conv_in = refs[:8], refs[8:15]
        gla_out, conv_out = refs[15 + n_comm:18 + n_comm], refs[18 + n_comm:20 + n_comm]
        state, ubuf, ush = refs[20 + 2 * n_comm:23 + 2 * n_comm]
        comm_start(refs)
        _gla_fwd_tile(*gla_in, *gla_out, state, tb)
        _conv_fwd_tile(*conv_in, *conv_out, ubuf, ush, tb)
        comm_wait(refs)

    nc = T // CHUNK
    tok = lambda w, col: pl.BlockSpec((tb, w), lambda i: (i, col))
    res = pl.pallas_call(
        body, name="mix_fwd", grid=(T // tb,),
        out_shape=[jax.ShapeDtypeStruct((T, VAL), BF16), jax.ShapeDtypeStruct((T, VAL), F32),
                   jax.ShapeDtypeStruct((nc, VAL, KEY), BF16), jax.ShapeDtypeStruct((T, CONV), BF16),
                   jax.ShapeDtypeStruct((T, CONV), F32)] + _comm_out_shapes(kinds, shards),
        in_specs=[tok(KEY, P_Q // KEY), tok(KEY, P_K // KEY), tok(VAL, P_V // VAL), tok(VAL, P_G // VAL),
                  tok(Z_PAD, P_Z // Z_PAD), _const((Z_PAD, KEY)), _const((1, KEY)), _const((1, DV)),
                  tok(CONV, P_CI // CONV), tok(CONV, P_CG // CONV), _const((HALO, CONV)), _const((1, CONV)),
                  _const((1, CONV)), _const((1, CONV)), _const((CONV, CONV))] + [ANY] * n_comm,
        out_specs=[tok(VAL, 0), tok(VAL, 0), pl.BlockSpec((cpb, VAL, KEY), lambda i: (i, 0, 0)), tok(CONV, 0),
                   tok(CONV, 0)] + [ANY] * n_comm,
        scratch_shapes=[pltpu.VMEM((VAL, KEY), F32), pltpu.VMEM((tb + HALO, CONV), F32),
                        pltpu.VMEM((SUBLANES - 1, tb + HALO - SUBLANES, CONV), F32)] + _comm_scratch(n_comm),
        compiler_params=_params(("arbitrary",)),
    )(proj, proj, proj, proj, proj, wg, bg, gn, proj, proj, conv_w, conv_b, cn_g, cn_b, gmat, *shards)
    return res[0], res[1], res[2], res[3], res[4], res[5:]


def _rms_bwd(dy, xhat, r, g):
    dyg = dy * g
    return r * (dyg - xhat * _rowmean(dyg * xhat))


def _mlp_fwd_bwd(x, mix_a, mix_c, tgt, w_out, g2, w1t, w2, gf):
    T = x.shape[0]
    tm = min(T, MLP_TILE)
    inv_d = 1.0 / D_MODEL

    def body(x_ref, ma_ref, mc_ref, t_ref, wo_ref, g2_ref, w1_ref, w2_ref, gf_ref,
             dh1_ref, dmix_ref, hn_ref, ff_ref, da_ref, dh2_ref, loss_ref, dgf_ref, dg2_ref):
        @pl.when(pl.program_id(0) == 0)
        def _():
            loss_ref[...] = jnp.zeros_like(loss_ref)
            dgf_ref[...] = jnp.zeros_like(dgf_ref)
            dg2_ref[...] = jnp.zeros_like(dg2_ref)

        g2v, gfv = g2_ref[...], gf_ref[...]
        h1 = x_ref[...] + _nn(ma_ref[...], wo_ref[0:VAL, :]) + _nn(mc_ref[...], wo_ref[VAL:, :])
        r2 = lax.rsqrt(_rowmean(h1 * h1) + EPS)
        h1hat = h1 * r2
        hn = (h1hat * g2v).astype(BF16)
        hn_ref[...] = hn
        relu_a = jnp.maximum(_nt(hn, w1_ref[...]), 0.0)
        ff = (relu_a * relu_a).astype(BF16)
        ff_ref[...] = ff
        h2 = h1 + _nn(ff, w2_ref[...])
        rf = lax.rsqrt(_rowmean(h2 * h2) + EPS)
        h2hat = h2 * rf
        err = h2hat * gfv - t_ref[...]
        loss_ref[...] += (0.5 * inv_d) * _colsum(jnp.sum(err * err, axis=1, keepdims=True))
        dy = err * inv_d
        dgf_ref[...] += _colsum(dy * h2hat)
        dh2 = _rms_bwd(dy, h2hat, rf, gfv)
        dh2_b = dh2.astype(BF16)
        dh2_ref[...] = dh2_b
        da = (_nt(dh2_b, w2_ref[...]) * (2.0 * relu_a)).astype(BF16)
        da_ref[...] = da
        dhn = _nn(da, w1_ref[...])
        dg2_ref[...] += _colsum(dhn * h1hat)
        dh1 = dh2 + _rms_bwd(dhn, h1hat, r2, g2v)
        dh1_ref[...] = dh1
        dmix_ref[...] = _nt(dh1.astype(BF16), wo_ref[...]).astype(BF16)

    tok = lambda w: pl.BlockSpec((tm, w), lambda i: (i, 0))
    return pl.pallas_call(
        body, name="mlp_fwd_bwd", grid=(T // tm,),
        out_shape=[jax.ShapeDtypeStruct((T, D_MODEL), F32), jax.ShapeDtypeStruct((T, D_MODEL), BF16),
                   jax.ShapeDtypeStruct((T, D_MODEL), BF16), jax.ShapeDtypeStruct((T, D_FF), BF16),
                   jax.ShapeDtypeStruct((T, D_FF), BF16), jax.ShapeDtypeStruct((T, D_MODEL), BF16),
                   jax.ShapeDtypeStruct((1, 1), F32), jax.ShapeDtypeStruct((1, D_MODEL), F32),
                   jax.ShapeDtypeStruct((1, D_MODEL), F32)],
        in_specs=[tok(D_MODEL), tok(VAL), tok(CONV), tok(D_MODEL), _const((D_MODEL, D_MODEL)), _const((1, D_MODEL)),
                  _const((D_FF, D_MODEL)), _const((D_FF, D_MODEL)), _const((1, D_MODEL))],
        out_specs=[tok(D_MODEL), tok(D_MODEL), tok(D_MODEL), tok(D_FF), tok(D_FF), tok(D_MODEL),
                   pl.BlockSpec((1, 1), lambda i: (0, 0)), pl.BlockSpec((1, D_MODEL), lambda i: (0, 0)),
                   pl.BlockSpec((1, D_MODEL), lambda i: (0, 0))],
        compiler_params=_params(("arbitrary",)),
    )(x, mix_a, mix_c, tgt, w_out, g2, w1t, w2, gf)


def _silu_grad(v, s):
    return s * (1.0 + v * (1.0 - s))


def _conv_bwd_tile(ci_ref, cg_ref, uc_ref, dm_ref, w_ref, g_ref, be_ref, gm_ref,
                   dci_ref, dcg_ref, dw_ref, db_ref, dg_ref, dbe_ref, dbuf, dsh, dwacc, tm, nt):
    step = pl.program_id(0)
    sh_rows = tm + HALO - SUBLANES

    @pl.when(step == 0)
    def _():
        dbuf[tm:, :] = jnp.zeros((HALO, CONV), F32)
        dwacc[...] = jnp.zeros_like(dwacc)
        db_ref[...] = jnp.zeros_like(db_ref)
        dg_ref[...] = jnp.zeros_like(dg_ref)
        dbe_ref[...] = jnp.zeros_like(dbe_ref)

    gm, gv = gm_ref[...], g_ref[...]
    ucv = uc_ref[...]
    d = ucv - _group_mean(ucv, gm)
    rs = lax.rsqrt(_group_mean(d * d, gm) + EPS)
    yhat = d * rs
    yn = yhat * gv + be_ref[...]
    dyn = dm_ref[...].astype(F32) * _silu_grad(yn, jax.nn.sigmoid(yn))
    dg_ref[...] += _colsum(dyn * yhat)
    dbe_ref[...] += _colsum(dyn)
    dyh = dyn * gv
    duc = rs * (dyh - _group_mean(dyh, gm) - yhat * _group_mean(dyh * yhat, gm))
    db_ref[...] += _colsum(duc)
    dbuf[0:tm, :] = duc
    _shifted_copies(dbuf, dsh, sh_rows)

    def strip(s, carry):
        r0 = pl.multiple_of(s * STRIP, STRIP)
        rows = pl.ds(r0, STRIP)
        cin = ci_ref[rows, :].astype(F32)
        sg = jax.nn.sigmoid(cg_ref[rows, :].astype(F32))
        u = cin * sg
        du = jnp.zeros((STRIP, CONV), F32)
        for j in range(CONV_W):
            dj = _tap(dbuf, dsh, CONV_W - 1 - j, r0, STRIP)
            du = du + w_ref[j:j + 1, :] * dj
            p = u * dj
            fold = p[0:SUBLANES]
            for q in range(1, STRIP // SUBLANES):
                fold = fold + p[q * SUBLANES:(q + 1) * SUBLANES, :]
            dwacc[j * SUBLANES:(j + 1) * SUBLANES, :] += fold
        dci_ref[rows, :] = (du * sg).astype(BF16)
        dcg_ref[rows, :] = (du * cin * sg * (1.0 - sg)).astype(BF16)
        return carry

    lax.fori_loop(0, tm // STRIP, strip, 0)
    dbuf[tm:, :] = dbuf[0:HALO, :]

    @pl.when(step == nt - 1)
    def _():
        dw_ref[...] = jnp.zeros_like(dw_ref)
        for j in range(CONV_W):
            dw_ref[j:j + 1, :] = _colsum(dwacc[j * SUBLANES:(j + 1) * SUBLANES, :])


def _conv_bwd(proj, uc, dmix, conv_w, cn_g, cn_b, gmat):
    T = proj.shape[0]
    tm = min(T, TOKEN_TILE)
    nt = T // tm
    sh_rows = tm + HALO - SUBLANES

    def body(*refs):
        _conv_bwd_tile(*refs, tm, nt)

    rev = lambda i: nt - 1 - i
    tile = lambda col: pl.BlockSpec((tm, CONV), lambda i: (rev(i), col))
    acc = lambda rows: pl.BlockSpec((rows, CONV), lambda i: (0, 0))
    return pl.pallas_call(
        body, name="conv_bwd", grid=(nt,),
        out_shape=[jax.ShapeDtypeStruct((T, CONV), BF16), jax.ShapeDtypeStruct((T, CONV), BF16),
                   jax.ShapeDtypeStruct((HALO, CONV), F32), jax.ShapeDtypeStruct((1, CONV), F32),
                   jax.ShapeDtypeStruct((1, CONV), F32), jax.ShapeDtypeStruct((1, CONV), F32)],
        in_specs=[tile(P_CI // CONV), tile(P_CG // CONV), tile(0), tile(1),
                  _const((HALO, CONV)), _const((1, CONV)), _const((1, CONV)), _const((CONV, CONV))],
        out_specs=[tile(0), tile(0), acc(HALO), acc(1), acc(1), acc(1)],
        scratch_shapes=[pltpu.VMEM((tm + HALO, CONV), F32), pltpu.VMEM((SUBLANES - 1, sh_rows, CONV), F32),
                        pltpu.VMEM((HALO * SUBLANES, CONV), F32)],
        compiler_params=_params(("arbitrary",)),
    )(proj, proj, uc, dmix, conv_w, cn_g, cn_b, gmat)


def _gla_bwd_tile(q_ref, k_ref, v_ref, g_ref, z_ref, o_ref, st_ref, dm_ref, wg_ref, bg_ref, gn_ref,
                  dq_ref, dk_ref, dv_ref, dg_ref, dz_ref, dwg_ref, dbg_ref, dgn_ref, dstate, do_scr, dv_scr, tb):
    @pl.when(pl.program_id(0) == 0)
    def _():
        dstate[...] = jnp.zeros_like(dstate)
        dwg_ref[...] = jnp.zeros_like(dwg_ref)
        dbg_ref[...] = jnp.zeros_like(dbg_ref)
        dgn_ref[...] = jnp.zeros_like(dgn_ref)

    gnv = gn_ref[...]
    dgn = jnp.zeros((1, DV), F32)
    for h in range(HEADS):
        cols = slice(h * DV, (h + 1) * DV)
        oh = o_ref[:, cols]
        r = lax.rsqrt(_rowmean(oh * oh) + EPS)
        ohat = oh * r
        gh = g_ref[:, cols].astype(F32)
        sg = jax.nn.sigmoid(gh)
        dmx = dm_ref[:, cols].astype(F32)
        don = dmx * (gh * sg)
        dg_ref[:, cols] = (dmx * (ohat * gnv) * _silu_grad(gh, sg)).astype(BF16)
        dgn = dgn + _colsum(don * ohat)
        do_scr[:, cols] = _rms_bwd(don, ohat, r, gnv)
    dgn_ref[...] += dgn

    mats = _chunk_matrices()
    masks = _head_masks()
    wgv, bgv = wg_ref[...], bg_ref[...]
    n_chunks = SUB // CHUNK

    for sb in reversed(range(tb // SUB)):
        rows = slice(sb * SUB, (sb + 1) * SUB)
        zs = z_ref[rows, :]
        d = _decay_terms(zs, q_ref[rows, :], k_ref[rows, :], wgv, bgv, mats)
        qem = d.q * d.ebm
        qem_b = qem.astype(BF16)
        kem_b = (d.k * d.emb).astype(BF16)
        dq = jnp.zeros((SUB, KEY), F32)
        dk = jnp.zeros((SUB, KEY), F32)
        for h in range(HEADS):
            hm = masks[h]
            cols = slice(h * DV, (h + 1) * DV)
            do_b = do_scr[rows, cols].astype(BF16)
            vh = v_ref[rows, cols]
            da = jnp.where(mats.causal, _nt(do_b, vh), 0.0).astype(BF16)
            da_t = jnp.where(mats.causal_t, _nt(vh, do_b), 0.0).astype(BF16)
            a_t = jnp.where(mats.causal_t, _nt(kem_b, (qem * hm).astype(BF16)), 0.0).astype(BF16)
            dq = dq + hm * _nn(da, kem_b)
            dk = dk + hm * _nn(da_t, qem_b)
            dv_scr[rows, cols] = _nn(a_t, do_b)
        dq = dq * d.ebm
        dk = dk * d.emb

        qe0_b = (d.q * d.eb).astype(BF16)
        kdec_b = (d.k * d.elb).astype(BF16)
        dq_st, dk_st, last = [None] * n_chunks, [None] * n_chunks, [None] * n_chunks
        for c in reversed(range(n_chunks)):
            loc = slice(c * CHUNK, (c + 1) * CHUNK)
            glob = slice(sb * SUB + c * CHUNK, sb * SUB + (c + 1) * CHUNK)
            st_b = st_ref[sb * n_chunks + c]
            ds = dstate[...]
            ds_b = ds.astype(BF16)
            do_c = do_scr[glob, :].astype(BF16)
            ebl_c = d.ebl[c * CHUNK:c * CHUNK + 1]
            dk_c = _nn(v_ref[glob, :], ds_b) * d.elb[loc]
            dq_st[c] = _nn(do_c, st_b) * d.eb[loc]
            dk_st[c] = dk_c
            last_c = _colsum(d.k[loc] * dk_c) + ebl_c * _colsum(st_b.astype(F32) * ds)
            last[c] = jnp.broadcast_to(last_c, (CHUNK, KEY))
            dv_ref[glob, :] = (dv_scr[glob, :] + _nt(kdec_b[loc], ds_b)).astype(BF16)
            dstate[...] = ds * ebl_c + jnp.where(mats.heads, _tn(do_c, qe0_b[loc]), 0.0)
        dq = dq + jnp.concatenate(dq_st, axis=0)
        dk = dk + jnp.concatenate(dk_st, axis=0)
        dq_ref[rows, :] = (dq * Q_SCALE).astype(BF16)
        dk_ref[rows, :] = dk.astype(BF16)
        hi, lo = _split_bf16(d.q * dq - d.k * dk)
        dla = _nn(mats.tri_t, hi) + _nn(mats.tri_t, lo) + jnp.concatenate(last, axis=0)
        dal = dla * (1.0 / GATE_TAU) * jax.nn.sigmoid(-d.al)
        dal_b = dal.astype(BF16)
        dz_ref[rows, :] = _nt(dal_b, wgv).astype(BF16)
        dwg_ref[...] += _tn(zs, dal_b)
        dbg_ref[...] += _colsum(dal)


def _gla_bwd(proj, o, states, dmix, wg, bg, gn, parts):
    T = proj.shape[0]
    tb = min(T, TOKEN_TILE)
    cpb = tb // CHUNK
    nb = T // tb
    n_comm = len(parts)
    kinds = ["exchange"] * n_comm
    comm_start, comm_wait = _hosted_comm(kinds, 11, 8, n_comm, nb)

    def body(*refs):
        comm_start(refs)
        _gla_bwd_tile(*refs[:11], *refs[11 + n_comm:19 + n_comm], *refs[19 + 2 * n_comm:22 + 2 * n_comm], tb)
        comm_wait(refs)

    rev = lambda i: nb - 1 - i
    blk = lambda w, col: pl.BlockSpec((tb, w), lambda i: (rev(i), col))
    res = pl.pallas_call(
        body, name="gla_bwd", grid=(nb,),
        out_shape=[jax.ShapeDtypeStruct((T, KEY), BF16), jax.ShapeDtypeStruct((T, KEY), BF16),
                   jax.ShapeDtypeStruct((T, VAL), BF16), jax.ShapeDtypeStruct((T, VAL), BF16),
                   jax.ShapeDtypeStruct((T, Z_PAD), BF16), jax.ShapeDtypeStruct((Z_PAD, KEY), F32),
                   jax.ShapeDtypeStruct((1, KEY), F32), jax.ShapeDtypeStruct((1, DV), F32)]
        + _comm_out_shapes(kinds, parts),
        in_specs=[blk(KEY, P_Q // KEY), blk(KEY, P_K // KEY), blk(VAL, P_V // VAL), blk(VAL, P_G // VAL),
                  blk(Z_PAD, P_Z // Z_PAD), blk(VAL, 0),
                  pl.BlockSpec((cpb, VAL, KEY), lambda i: (rev(i), 0, 0)), blk(VAL, 0),
                  _const((Z_PAD, KEY)), _const((1, KEY)), _const((1, DV))] + [ANY] * n_comm,
        out_specs=[blk(KEY, 0), blk(KEY, 0), blk(VAL, 0), blk(VAL, 0), blk(Z_PAD, 0),
                   pl.BlockSpec((Z_PAD, KEY), lambda i: (0, 0)), pl.BlockSpec((1, KEY), lambda i: (0, 0)),
                   pl.BlockSpec((1, DV), lambda i: (0, 0))] + [ANY] * n_comm,
        scratch_shapes=[pltpu.VMEM((VAL, KEY), F32), pltpu.VMEM((tb, VAL), F32), pltpu.VMEM((tb, VAL), F32)]
        + _comm_scratch(n_comm),
        compiler_params=_params(("arbitrary",)),
    )(proj, proj, proj, proj, proj, o, states, dmix, wg, bg, gn, *parts)
    return res[:8], res[8:]


def _inproj_bwd(x, g1, w_in_t, dh1, dq, dk, dv, dg, dci, dcg, dz):
    T = x.shape[0]
    tm = min(T, TOKEN_TILE)

    def body(x_ref, g_ref, w_ref, dh1_ref, dq_ref, dk_ref, dv_ref, dg_ref, dci_ref, dcg_ref, dz_ref,
             dx_ref, dg1_ref, dp_ref):
        @pl.when(pl.program_id(0) == 0)
        def _():
            dg1_ref[...] = jnp.zeros_like(dg1_ref)

        dp_ref[:, P_Q:P_K] = dq_ref[...]
        dp_ref[:, P_K:P_V] = dk_ref[...]
        dp_ref[:, P_V:P_G] = dv_ref[...]
        dp_ref[:, P_G:P_CI] = dg_ref[...]
        dp_ref[:, P_CI:P_CG] = dci_ref[...]
        dp_ref[:, P_CG:P_Z] = dcg_ref[...]
        dp_ref[:, P_Z:] = dz_ref[...]
        dxn = (_nn(dp_ref[:, 0:P_CI], w_ref[0:OFF_Z, :]) + _nn(dp_ref[:, P_CI:P_Z], w_ref[OFF_C:D_IN, :])
               + _nn(dp_ref[:, P_Z:], w_ref[OFF_Z:OFF_Z + Z_PAD, :]))
        xv = x_ref[...]
        r = lax.rsqrt(_rowmean(xv * xv) + EPS)
        xhat = xv * r
        dg1_ref[...] += _colsum(dxn * xhat)
        dx_ref[...] = dh1_ref[...] + _rms_bwd(dxn, xhat, r, g_ref[...])

    tok = lambda w: pl.BlockSpec((tm, w), lambda i: (i, 0))
    return pl.pallas_call(
        body, name="inproj_bwd", grid=(T // tm,),
        out_shape=[jax.ShapeDtypeStruct((T, D_MODEL), F32), jax.ShapeDtypeStruct((1, D_MODEL), F32)],
        in_specs=[tok(D_MODEL), _const((1, D_MODEL)), _const((D_IN, D_MODEL)), tok(D_MODEL), tok(KEY), tok(KEY),
                  tok(VAL), tok(VAL), tok(CONV), tok(CONV), tok(Z_PAD)],
        out_specs=[tok(D_MODEL), pl.BlockSpec((1, D_MODEL), lambda i: (0, 0))],
        scratch_shapes=[pltpu.VMEM((tm, D_INP), BF16)],
        compiler_params=_params(("arbitrary",)),
    )(x, g1, w_in_t, dh1, dq, dk, dv, dg, dci, dcg, dz)


def _wgrad_in(xn, dq, dk, dv, dg, dz, dci, dcg):
    T = xn.shape[0]
    tt = min(T, WGRAD_TILE // 2)
    nt = T // tt
    pieces = [dq, dk, dv, dg, dz, dci, dcg]
    rows = [KEY, KEY, VAL, VAL, RANK, CONV, CONV]
    assert sum(rows) == D_IN

    def body(*refs):
        xn_ref, piece_refs, o_ref, acc = refs[0], refs[1:1 + len(pieces)], refs[-2], refs[-1]

        @pl.when(pl.program_id(0) == 0)
        def _():
            acc[...] = jnp.zeros_like(acc)

        xv = xn_ref[...]
        row = 0
        for ref, n in zip(piece_refs, rows):
            acc[row:row + n, :] += _tn(ref[...], xv)[0:n]
            row += n

        @pl.when(pl.program_id(0) == nt - 1)
        def _():
            o_ref[...] = acc[...].astype(BF16)

    tok = lambda w: pl.BlockSpec((tt, w), lambda t: (t, 0))
    return pl.pallas_call(
        body, name="wgrad_in", grid=(nt,), out_shape=jax.ShapeDtypeStruct((D_IN, D_MODEL), BF16),
        in_specs=[tok(D_MODEL)] + [tok(p.shape[1]) for p in pieces],
        out_specs=pl.BlockSpec((D_IN, D_MODEL), lambda t: (0, 0), pipeline_mode=pl.Buffered(1)),
        scratch_shapes=[pltpu.VMEM((D_IN, D_MODEL), F32)],
        compiler_params=_params(("arbitrary",)),
    )(xn, *pieces)


def _wgrad_out(mix_a, mix_c, dh1):
    T = dh1.shape[0]
    tt = min(T, WGRAD_TILE // 2)
    nt = T // tt

    def body(a_ref, c_ref, b_ref, o_ref, acc):
        @pl.when(pl.program_id(0) == 0)
        def _():
            acc[...] = jnp.zeros_like(acc)

        b = b_ref[...].astype(BF16)
        acc[0:VAL, :] += _tn(a_ref[...], b)
        acc[VAL:, :] += _tn(c_ref[...], b)

        @pl.when(pl.program_id(0) == nt - 1)
        def _():
            o_ref[...] = acc[...].astype(BF16)

    tok = lambda w: pl.BlockSpec((tt, w), lambda t: (t, 0))
    return pl.pallas_call(
        body, name="wgrad_out", grid=(nt,), out_shape=jax.ShapeDtypeStruct((D_MODEL, D_MODEL), BF16),
        in_specs=[tok(VAL), tok(CONV), tok(D_MODEL)],
        out_specs=pl.BlockSpec((D_MODEL, D_MODEL), lambda t: (0, 0)),
        scratch_shapes=[pltpu.VMEM((D_MODEL, D_MODEL), F32)],
        compiler_params=_params(("arbitrary",)),
    )(mix_a, mix_c, dh1)


def _wgrad(a, b, name, tk, tn, col_block=None):
    T, K = a.shape
    N = b.shape[1]
    tt = min(T, WGRAD_TILE)
    nt = T // tt

    def body(a_ref, b_ref, o_ref, acc):
        @pl.when(pl.program_id(2) == 0)
        def _():
            acc[...] = jnp.zeros_like(acc)

        acc[...] += _tn(a_ref[...], b_ref[...].astype(BF16))

        @pl.when(pl.program_id(2) == nt - 1)
        def _():
            if col_block is None:
                o_ref[...] = acc[...].astype(BF16)
            else:
                for q in range(tn // col_block):
                    o_ref[q] = acc[:, q * col_block:(q + 1) * col_block].astype(BF16)

    if col_block is None:
        out_shape = jax.ShapeDtypeStruct((K, N), BF16)
        out_spec = pl.BlockSpec((tk, tn), lambda i, j, t: (i, j))
    else:
        assert tk == K
        out_shape = jax.ShapeDtypeStruct((N // col_block, K, col_block), BF16)
        out_spec = pl.BlockSpec((tn // col_block, tk, col_block), lambda i, j, t: (j, 0, 0))
    return pl.pallas_call(
        body, name=name, grid=(K // tk, N // tn, nt), out_shape=out_shape,
        in_specs=[pl.BlockSpec((tt, tk), lambda i, j, t: (t, i)), pl.BlockSpec((tt, tn), lambda i, j, t: (t, j))],
        out_specs=out_spec, scratch_shapes=[pltpu.VMEM((tk, tn), F32)],
        compiler_params=_params(("arbitrary", "arbitrary", "arbitrary")),
    )(a, b)


def _adam_math(w, g, m, v):
    m = ADAM_B1 * m + (1.0 - ADAM_B1) * g
    v = ADAM_B2 * v + (1.0 - ADAM_B2) * (g * g)
    m_hat = m / (1.0 - ADAM_B1 ** ADAM_STEP)
    v_hat = v / (1.0 - ADAM_B2 ** ADAM_STEP)
    delta = -ADAM_LR * (m_hat / (jnp.sqrt(v_hat) + ADAM_EPS) + ADAM_WD * w)
    return delta, m, v


def _sum8(ref):
    g = ref[0].astype(F32)
    for s in range(1, N_DEV):
        g = g + ref[s].astype(F32)
    return g


def _adam_big(parts, w, m, v, name):
    R, C = w.shape
    tr = ADAM_ROWS if R % ADAM_ROWS == 0 else R

    def body(p_ref, w_ref, m_ref, v_ref, g_ref, d_ref, nm_ref, nv_ref):
        g = _sum8(p_ref)
        g_ref[...] = g
        d_ref[...], nm_ref[...], nv_ref[...] = _adam_math(w_ref[...], g, m_ref[...], v_ref[...])

    row = pl.BlockSpec((tr, C), lambda i: (i, 0))
    return pl.pallas_call(
        body, name=name, grid=(R // tr,), out_shape=[jax.ShapeDtypeStruct((R, C), F32)] * 4,
        in_specs=[pl.BlockSpec((N_DEV, tr, C), lambda i: (0, i, 0)), row, row, row], out_specs=[row] * 4,
        compiler_params=_params(("arbitrary",)),
    )(parts, w, m, v)


def _sum_small(parts):
    def body(p_ref, o_ref):
        o_ref[...] = _sum8(p_ref)

    return pl.pallas_call(body, name="sum_small", out_shape=jax.ShapeDtypeStruct(parts.shape[1:], F32))(parts)


def _adam_small(gs, ws, ms, vs):
    n = len(gs)

    def body(*refs):
        g_refs, w_refs, m_refs, v_refs = refs[:n], refs[n:2 * n], refs[2 * n:3 * n], refs[3 * n:4 * n]
        outs = refs[4 * n:]
        for i in range(n):
            d, nm, nv = _adam_math(w_refs[i][...], g_refs[i][...], m_refs[i][...], v_refs[i][...])
            outs[i][...] = d
            outs[n + i][...] = nm
            outs[2 * n + i][...] = nv

    shapes = [jax.ShapeDtypeStruct(w.shape, F32) for w in ws]
    res = pl.pallas_call(body, name="adam_small", out_shape=shapes * 3)(*gs, *ws, *ms, *vs)
    return res[:n], res[n:2 * n], res[2 * n:]


def _group_matrix():
    gi = lax.broadcasted_iota(jnp.int32, (CONV, CONV), 0) // (CONV // GROUPS)
    gj = lax.broadcasted_iota(jnp.int32, (CONV, CONV), 1) // (CONV // GROUPS)
    return jnp.where(gi == gj, GROUPS / CONV, 0.0).astype(BF16)


_SMALL = [("loss", 8), ("dg1", 8), ("dbg", 2), ("dgn", 1), ("dconv_b", 4), ("dcn_g", 4), ("dcn_b", 4), ("dg2", 8),
          ("dgf", 8), ("dwg", 32), ("dconv_w", 124)]


def _pad8(rows):
    return -(-rows // 8) * 8


def kernel(x, norm1_g, w_in, w_gate_up, b_gate, gla_norm_g, conv_w, conv_b, conv_norm_g, conv_norm_b, w_out, norm2_g, w_mlp_in, w_mlp_out, final_norm_g, loss_target, m_norm1_g, m_w_in, m_w_gate_up, m_b_gate, m_gla_norm_g, m_conv_w, m_conv_b, m_conv_norm_g, m_conv_norm_b, m_w_out, m_norm2_g, m_w_mlp_in, m_w_mlp_out, m_final_norm_g, v_norm1_g, v_w_in, v_w_gate_up, v_b_gate, v_gla_norm_g, v_conv_w, v_conv_b, v_conv_norm_g, v_conv_norm_b, v_w_out, v_norm2_g, v_w_mlp_in, v_w_mlp_out, v_final_norm_g):
    x_idx = lax.axis_index("x")
    y_idx = lax.axis_index("y")
    c_idx = lax.axis_index("c")
    me = 4 * x_idx + 2 * y_idx + c_idx
    xs, tgt = x[0], loss_target[0]
    gf = final_norm_g.reshape(1, D_MODEL)
    gmat = _group_matrix()

    small_shard = jnp.zeros((_pad8(RANK + CONV_W), LANES), F32)
    small_shard = small_shard.at[0:RANK, 0:KEY // N_DEV].set(w_gate_up[0])
    small_shard = small_shard.at[RANK:RANK + CONV_W, 0:CONV // N_DEV].set(conv_w[0])
    g_in, g_small = _gather_two_level([w_in[0].T.astype(BF16), small_shard], "gather_w_in")
    w_in_t = g_in.reshape(D_IN, D_MODEL)
    wg_full = jnp.concatenate([g_small[d, 0:RANK, 0:KEY // N_DEV] for d in range(N_DEV)], axis=1)
    wg_pad = jnp.pad(wg_full, ((0, Z_PAD - RANK), (0, 0))).astype(BF16)
    conv_w_full = jnp.concatenate([g_small[d, RANK:RANK + CONV_W, 0:CONV // N_DEV] for d in range(N_DEV)], axis=1)
    conv_w_pad = jnp.pad(conv_w_full, ((0, HALO - CONV_W), (0, 0)))

    proj, xn, (g_w2,) = _inproj_fwd(xs, norm1_g, w_in_t, [w_mlp_out[0].astype(BF16)])
    mix_a, o, states, mix_c, uc, (g_out, g_w1) = _mix_fwd(
        proj, wg_pad, b_gate, gla_norm_g, conv_w_pad, conv_b, conv_norm_g, conv_norm_b, gmat,
        [w_out[0].astype(BF16), w_mlp_in[0].T.astype(BF16)])
    w_out_full = g_out.reshape(D_MODEL, D_MODEL)
    w1t_full = g_w1.reshape(D_FF, D_MODEL)
    w2_full = g_w2.reshape(D_FF, D_MODEL)
    dh1, dmix, hn, ff, da, dh2, loss, dgf, dg2 = _mlp_fwd_bwd(xs, mix_a, mix_c, tgt, w_out_full, norm2_g, w1t_full,
                                                              w2_full, gf)

    dw1 = _wgrad(hn, da, "wgrad_mlp_in", WGRAD_BLOCK, WGRAD_BLOCK, col_block=D_FF // N_DEV)
    dw2 = _wgrad(ff, dh2, "wgrad_mlp_out", WGRAD_BLOCK, WGRAD_BLOCK)
    dw_out = _wgrad_out(mix_a, mix_c, dh1)
    dci, dcg, dconv_w, dconv_b, dcn_g, dcn_b = _conv_bwd(proj, uc, dmix, conv_w_pad, conv_norm_g, conv_norm_b, gmat)
    (dq, dk, dv, dg, dz, dwg, dbg, dgn), (p_w1, p_w2, p_out) = _gla_bwd(
        proj, o, states, dmix, wg_pad, b_gate, gla_norm_g,
        [dw1, dw2.reshape(N_DEV, D_FF // N_DEV, D_MODEL), dw_out.reshape(N_DEV, D_MODEL // N_DEV, D_MODEL)])
    dw_in = _wgrad_in(xn, dq, dk, dv, dg, dz, dci, dcg).reshape(N_DEV, SHARD_IN, D_MODEL)
    send_sems, recv_sems, dw_in_thru, land, token = _split_start("exchange", dw_in, jnp.copy(dw_in),
                                                                 "exchange_w_in_start")
    dx, dg1 = _inproj_bwd(xs, norm1_g + token[0:1, 0:1], w_in_t, dh1, dq, dk, dv, dg, dci, dcg, dz)
    p_in = _split_wait("exchange", send_sems, recv_sems, dw_in_thru, land, dg1, "exchange_w_in_wait")

    small = dict(loss=jnp.zeros((SUBLANES, LANES), F32) + loss, dg1=dg1, dbg=dbg, dgn=dgn, dconv_b=dconv_b, dcn_g=dcn_g,
                 dcn_b=dcn_b, dg2=dg2, dgf=dgf, dwg=dwg[0:RANK], dconv_w=dconv_w[0:CONV_W])
    pack = jnp.concatenate([jnp.pad(small[name].reshape(rows, LANES), ((0, _pad8(rows) - rows), (0, 0)))
                            for name, rows in _SMALL], axis=0)
    s_send, s_recv, pack_thru, pack_land, s_token = _split_start(
        "gather", pack, jnp.broadcast_to(pack, (N_DEV,) + pack.shape) + 0.0, "gather_small_start")

    gi, di, mi, vi = _adam_big(p_in, w_in[0].T, m_w_in[0].T, v_w_in[0].T, "adam_w_in")
    go, do, mo, vo = _adam_big(p_out, w_out[0] + s_token[0:1, 0:1], m_w_out[0], v_w_out[0], "adam_w_out")
    ga, da_, ma, va = _adam_big(p_w1, w_mlp_in[0], m_w_mlp_in[0], v_w_mlp_in[0], "adam_w_mlp_in")
    gb, db, mb, vb = _adam_big(p_w2, w_mlp_out[0], m_w_mlp_out[0], v_w_mlp_out[0], "adam_w_mlp_out")
    cut = lambda a: a.T[None]

    g_pack = _split_wait("gather", s_send, s_recv, pack_thru, pack_land, go[0:8, 0:128] + ga[0:8, 0:128]
                         + gb[0:8, 0:128], "gather_small_wait")
    summed = _sum_small(g_pack)
    small_g = {}
    at = 0
    for name, rows in _SMALL:
        small_g[name] = summed[at:at + rows]
        at += _pad8(rows)
    loss_out = small_g["loss"][0, 0]
    wg_cols = KEY // N_DEV
    cw_cols = CONV // N_DEV
    g_small_list = [
        small_g["dg1"].reshape(1, D_MODEL),
        lax.dynamic_slice_in_dim(small_g["dwg"].reshape(RANK, KEY), me * wg_cols, wg_cols, axis=1)[None],
        small_g["dbg"].reshape(1, KEY), small_g["dgn"].reshape(1, DV),
        lax.dynamic_slice_in_dim(small_g["dconv_w"].reshape(CONV_W, CONV), me * cw_cols, cw_cols, axis=1)[None],
        small_g["dconv_b"].reshape(1, CONV), small_g["dcn_g"].reshape(1, CONV), small_g["dcn_b"].reshape(1, CONV),
        small_g["dg2"].reshape(1, D_MODEL), small_g["dgf"].reshape(1, D_MODEL),
    ]
    row = lambda a: a.reshape(1, D_MODEL)
    w_small = [norm1_g, w_gate_up, b_gate, gla_norm_g, conv_w, conv_b, conv_norm_g, conv_norm_b, norm2_g,
               row(final_norm_g)]
    m_small = [m_norm1_g, m_w_gate_up, m_b_gate, m_gla_norm_g, m_conv_w, m_conv_b, m_conv_norm_g, m_conv_norm_b,
               m_norm2_g, row(m_final_norm_g)]
    v_small = [v_norm1_g, v_w_gate_up, v_b_gate, v_gla_norm_g, v_conv_w, v_conv_b, v_conv_norm_g, v_conv_norm_b,
               v_norm2_g, row(v_final_norm_g)]
    d_small, nm_small, nv_small = _adam_small(g_small_list, w_small, m_small, v_small)
    flat = lambda lst: list(lst[:-1]) + [lst[-1].reshape(D_MODEL)]
    g_small_list, d_small, nm_small, nv_small = flat(g_small_list), flat(d_small), flat(nm_small), flat(nv_small)

    def order(s, w_in_v, w_out_v, w1_v, w2_v):
        return [s[0], w_in_v, s[1], s[2], s[3], s[4], s[5], s[6], s[7], w_out_v, s[8], w1_v, w2_v, s[9]]

    grads = order(g_small_list, cut(gi), go[None], ga[None], gb[None])
    deltas = order(d_small, cut(di), do[None], da_[None], db[None])
    new_m = order(nm_small, cut(mi), mo[None], ma[None], mb[None])
    new_v = order(nv_small, cut(vi), vo[None], va[None], vb[None])
    return (loss_out, dx[None], *grads, *deltas, *new_m, *new_v)
```

```python
from typing import NamedTuple

import jax
import jax.numpy as jnp
from jax import lax
from jax.experimental import pallas as pl
from jax.experimental.pallas import tpu as pltpu

F32 = jnp.float32
BF16 = jnp.bfloat16

N_DEV = 8
D_MODEL = 1024
HEADS = 4
DK = 64
DV = 128
KEY = HEADS * DK
VAL = HEADS * DV
RANK = 16
CONV = 512
GROUPS = 8
CONV_W = 31
HALO = 32
SUBLANES = 8
LANES = 128
STRIP = 32
FWD_STRIP = 16
TOKEN_TILE = 512
MLP_TILE = 256
WGRAD_TILE = 4096
WGRAD_BLOCK = 1024
ADAM_ROWS = 128
D_FF = 4096
D_IN = 2576
SHARD_IN = D_IN // N_DEV
CHUNK = 64
SUB = 256
EPS = 1e-6
GATE_TAU = 16.0
Q_SCALE = DK ** -0.5

P_Q, P_K, P_V, P_G, P_CI, P_CG, P_Z = 0, 256, 512, 1024, 1536, 2048, 2560
D_INP = 2688
Z_PAD = D_INP - P_Z
OFF_Z = 1536
OFF_C = OFF_Z + RANK
D_GLA = OFF_Z + Z_PAD

ADAM_LR = 0.001
ADAM_B1 = 0.9
ADAM_B2 = 0.999
ADAM_EPS = 1e-08
ADAM_WD = 0.01
ADAM_STEP = 10

V7X_VMEM_BYTES = 64 * 1024 * 1024
VMEM_LIMIT = V7X_VMEM_BYTES * 7 // 8

MESH = pl.DeviceIdType.MESH
ANY = pl.BlockSpec(memory_space=pl.ANY)


def _nn(a, b):
    return jnp.dot(a, b, preferred_element_type=F32)


def _nt(a, b):
    return lax.dot_general(a, b, (((1,), (1,)), ((), ())), preferred_element_type=F32)


def _tn(a, b):
    return lax.dot_general(a, b, (((0,), (0,)), ((), ())), preferred_element_type=F32)


def _params(sem=None):
    return pltpu.CompilerParams(dimension_semantics=sem, vmem_limit_bytes=VMEM_LIMIT)


def _const(shape):
    return pl.BlockSpec(shape, lambda *_: (0,) * len(shape), pipeline_mode=pl.Buffered(1))


def _colsum(v):
    return jnp.sum(v, axis=0, keepdims=True)


def _rowmean(v):
    return jnp.mean(v, axis=-1, keepdims=True)


def _split_bf16(v):
    hi = v.astype(BF16)
    return hi, (v - hi.astype(F32)).astype(BF16)


def _my_place():
    return lax.axis_index("x"), lax.axis_index("y"), lax.axis_index("c")


def _peer(j):
    x, y, c = _my_place()
    jx, jy, jc = (j >> 2) & 1, (j >> 1) & 1, j & 1
    px = 1 - x if jx else x
    py = 1 - y if jy else y
    pc = 1 - c if jc else c
    return (px, py, pc), 4 * px + 2 * py + pc


def _comm_plan(kinds, ins, outs, send_sems, recv_sems, local_sems, receives=True):
    x, y, c = _my_place()
    me = 4 * x + 2 * y + c
    own = lambda k, idx: ins[k] if kinds[k] == "gather" else ins[k].at[idx]
    local = [pltpu.make_async_copy(own(k, me), outs[k].at[me], local_sems.at[k]) for k in range(len(kinds))]
    sends, recvs = [], []
    for j in range(1, N_DEV):
        peer, peer_idx = _peer(j)
        for k in range(len(kinds)):
            sems = dict(send_sem=send_sems.at[k, j - 1], recv_sem=recv_sems.at[k, j - 1], device_id=peer,
                        device_id_type=MESH)
            sends.append(pltpu.make_async_remote_copy(src_ref=own(k, peer_idx), dst_ref=outs[k].at[me], **sems))
            if receives:
                recvs.append(pltpu.make_async_remote_copy(src_ref=own(k, me), dst_ref=outs[k].at[peer_idx], **sems))
    return local, sends, recvs


def _comm_start(plan):
    local, sends, _ = plan
    for cp in local + sends:
        cp.start()


def _comm_wait(plan):
    local, sends, recvs = plan
    for cp in recvs:
        cp.wait_recv()
    for cp in sends:
        cp.wait_send()
    for cp in local:
        cp.wait()


def _comm_scratch(n):
    return [pltpu.SemaphoreType.DMA((n, N_DEV - 1)), pltpu.SemaphoreType.DMA((n, N_DEV - 1)),
            pltpu.SemaphoreType.DMA((n,))]


def _comm_out_shapes(kinds, arrays):
    return [jax.ShapeDtypeStruct(((N_DEV,) + a.shape) if kind == "gather" else a.shape, a.dtype)
            for kind, a in zip(kinds, arrays)]


def _gather_two_level(shards, name):
    n = len(shards)

    def body(*refs):
        ins, outs = refs[:n], refs[n:2 * n]
        send_sems, recv_sems, local_sems = refs[2 * n:]
        x, y, c = _my_place()
        index = lambda px, py, pc: 4 * px + 2 * py + pc
        me, sibling = (x, y, c), (x, y, 1 - c)
        chips = [(1 - x, y), (x, 1 - y), (1 - x, 1 - y)]

        def copy(k, slot, block, to, src=None):
            rows = outs[k].at[index(*block)]
            return pltpu.make_async_remote_copy(
                src_ref=rows if src is None else src, dst_ref=rows, send_sem=send_sems.at[k, slot],
                recv_sem=recv_sems.at[k, slot], device_id=to, device_id_type=MESH)

        local = [pltpu.make_async_copy(ins[k], outs[k].at[index(*me)], local_sems.at[k]) for k in range(n)]
        first = []
        for k in range(n):
            first.append(copy(k, 0, me, sibling, src=ins[k]))
            first += [copy(k, 1 + j, me, (*chip, c), src=ins[k]) for j, chip in enumerate(chips)]
        for cp in local + first:
            cp.start()
        passed = []
        for j, chip in enumerate(chips):
            for k in range(n):
                copy(k, 1 + j, (*chip, c), me).wait_recv()
                cp = copy(k, 4 + j, (*chip, c), sibling)
                cp.start()
                passed.append(cp)
        for k in range(n):
            copy(k, 0, sibling, me).wait_recv()
        for j, chip in enumerate(chips):
            for k in range(n):
                copy(k, 4 + j, (*chip, 1 - c), me).wait_recv()
        for cp in first + passed:
            cp.wait_send()
        for cp in local:
            cp.wait()

    return pl.pallas_call(
        body, name=name, out_shape=_comm_out_shapes(["gather"] * n, shards), in_specs=[ANY] * n, out_specs=[ANY] * n,
        scratch_shapes=_comm_scratch(n),
    )(*shards)


HBM = pl.BlockSpec(memory_space=pltpu.HBM)
SEM = pl.BlockSpec(memory_space=pltpu.SEMAPHORE)
DATAFLOW = pltpu.SideEffectType.DATAFLOW_SIDE_EFFECTING


def _split_start(kind, part, land, name):
    def body(src_ref, land_ref, send_sems, recv_sems, src_thru, land_thru, token):
        x, y, c = _my_place()
        me = 4 * x + 2 * y + c
        for j in range(1, N_DEV):
            peer, peer_idx = _peer(j)
            pltpu.make_async_remote_copy(
                src_ref=src_ref.at[peer_idx] if kind == "exchange" else src_ref, dst_ref=land_ref.at[me],
                send_sem=send_sems.at[j - 1], recv_sem=recv_sems.at[j - 1], device_id=peer,
                device_id_type=MESH).start()
        token[...] = jnp.zeros_like(token)

    return pl.pallas_call(
        body, name=name,
        out_shape=(pltpu.SemaphoreType.DMA((N_DEV - 1,)), pltpu.SemaphoreType.DMA((N_DEV - 1,)),
                   pltpu.HBM(part.shape, part.dtype), pltpu.HBM(land.shape, land.dtype),
                   jax.ShapeDtypeStruct((SUBLANES, LANES), F32)),
        in_specs=(HBM, HBM), out_specs=(SEM, SEM, HBM, HBM, pl.BlockSpec(memory_space=pltpu.VMEM)),
        input_output_aliases={0: 2, 1: 3},
        compiler_params=pltpu.CompilerParams(has_side_effects=DATAFLOW),
    )(pltpu.with_memory_space_constraint(part, pltpu.HBM), pltpu.with_memory_space_constraint(land, pltpu.HBM))


def _split_wait(kind, send_sems, recv_sems, part_thru, land_thru, after, name):
    def body(src_ref, land_ref, send_sems, recv_sems, after_ref, src_dead, got_ref):
        x, y, c = _my_place()
        me = 4 * x + 2 * y + c
        own = lambda idx: src_ref.at[idx] if kind == "exchange" else src_ref
        for j in range(1, N_DEV):
            peer, peer_idx = _peer(j)
            sems = dict(send_sem=send_sems.at[j - 1], recv_sem=recv_sems.at[j - 1], device_id=peer,
                        device_id_type=MESH)
            pltpu.make_async_remote_copy(src_ref=own(peer_idx), dst_ref=land_ref.at[me], **sems).wait_send()
            pltpu.make_async_remote_copy(src_ref=own(me), dst_ref=land_ref.at[peer_idx], **sems).wait_recv()

    return pl.pallas_call(
        body, name=name,
        out_shape=(pltpu.HBM(part_thru.shape, part_thru.dtype), pltpu.HBM(land_thru.shape, land_thru.dtype)),
        in_specs=(HBM, HBM, SEM, SEM, ANY), out_specs=(HBM, HBM), input_output_aliases={0: 0, 1: 1},
        compiler_params=pltpu.CompilerParams(has_side_effects=DATAFLOW),
    )(part_thru, land_thru, send_sems, recv_sems, after)[1]


def _hosted_comm(kinds, n_in, n_out, n_comm, n_steps):
    def plan_of(refs, receives):
        ins = refs[n_in:n_in + n_comm]
        outs = refs[n_in + n_comm + n_out:n_in + 2 * n_comm + n_out]
        return _comm_plan(kinds, ins, outs, *refs[-3:], receives=receives)

    def start(refs):
        @pl.when(pl.program_id(0) == 0)
        def _():
            _comm_start(plan_of(refs, False))

    def wait(refs):
        @pl.when(pl.program_id(0) == n_steps - 1)
        def _():
            _comm_wait(plan_of(refs, True))

    return start, wait


def _z_lanes():
    return lax.broadcasted_iota(jnp.int32, (1, Z_PAD), 1) < RANK


def _inproj_fwd(x, g1, w_in_t, shards):
    T = x.shape[0]
    tm = min(T, TOKEN_TILE)
    n_comm = len(shards)
    kinds = ["gather"] * n_comm
    comm_start, comm_wait = _hosted_comm(kinds, 3, 2, n_comm, T // tm)

    def body(*refs):
        x_ref, g_ref, w_ref = refs[:3]
        proj_ref, xn_ref = refs[3 + n_comm:5 + n_comm]
        comm_start(refs)
        xv = x_ref[...]
        r = lax.rsqrt(_rowmean(xv * xv) + EPS)
        xn = (xv * r * g_ref[...]).astype(BF16)
        xn_ref[...] = xn
        proj_ref[:, 0:P_CI] = _nt(xn, w_ref[0:OFF_Z, :]).astype(BF16)
        proj_ref[:, P_CI:P_Z] = _nt(xn, w_ref[OFF_C:D_IN, :]).astype(BF16)
        proj_ref[:, P_Z:] = jnp.where(_z_lanes(), _nt(xn, w_ref[OFF_Z:OFF_Z + Z_PAD, :]), 0.0).astype(BF16)
        comm_wait(refs)

    res = pl.pallas_call(
        body, name="inproj_fwd", grid=(T // tm,),
        out_shape=[jax.ShapeDtypeStruct((T, D_INP), BF16), jax.ShapeDtypeStruct((T, D_MODEL), BF16)]
        + _comm_out_shapes(kinds, shards),
        in_specs=[pl.BlockSpec((tm, D_MODEL), lambda i: (i, 0)), _const((1, D_MODEL)), _const((D_IN, D_MODEL))]
        + [ANY] * n_comm,
        out_specs=[pl.BlockSpec((tm, D_INP), lambda i: (i, 0)), pl.BlockSpec((tm, D_MODEL), lambda i: (i, 0))]
        + [ANY] * n_comm,
        scratch_shapes=_comm_scratch(n_comm),
        compiler_params=_params(("arbitrary",)),
    )(x, g1, w_in_t, *shards)
    return res[0], res[1], res[2:]


def _head_masks():
    lane = lax.broadcasted_iota(jnp.int32, (1, KEY), 1)
    return [((lane >= h * DK) & (lane < (h + 1) * DK)).astype(F32) for h in range(HEADS)]


class _Mats(NamedTuple):
    tri: jax.Array
    tri_t: jax.Array
    same: jax.Array
    mid: jax.Array
    causal: jax.Array
    causal_t: jax.Array
    heads: jax.Array


def _chunk_matrices():
    r = lax.broadcasted_iota(jnp.int32, (SUB, SUB), 0)
    c = lax.broadcasted_iota(jnp.int32, (SUB, SUB), 1)
    shift = CHUNK.bit_length() - 1
    same = jnp.right_shift(r, shift) == jnp.right_shift(c, shift)
    causal = same & (r >= c)
    causal_t = same & (r <= c)
    mid = same & ((c & (CHUNK - 1)) < CHUNK // 2)
    hr = jnp.right_shift(lax.broadcasted_iota(jnp.int32, (VAL, KEY), 0), DV.bit_length() - 1)
    hc = jnp.right_shift(lax.broadcasted_iota(jnp.int32, (VAL, KEY), 1), DK.bit_length() - 1)
    return _Mats(tri=causal.astype(BF16), tri_t=causal_t.astype(BF16), same=same.astype(BF16), mid=mid.astype(BF16),
                 causal=causal, causal_t=causal_t, heads=hr == hc)


class _Decay(NamedTuple):
    al: jax.Array
    q: jax.Array
    k: jax.Array
    eb: jax.Array
    ebm: jax.Array
    emb: jax.Array
    elb: jax.Array
    ebl: jax.Array


def _decay_terms(z, q, k, wg, bg, mats):
    al = _nn(z, wg) + bg
    la = (jnp.minimum(al, 0.0) - jnp.log(1.0 + jnp.exp(-jnp.abs(al)))) * (1.0 / GATE_TAU)
    hi, lo = _split_bf16(la)
    cum = lambda m: _nn(m, hi) + _nn(m, lo)
    b, b_last, b_mid = cum(mats.tri), cum(mats.same), cum(mats.mid)
    return _Decay(al=al, q=q.astype(F32) * Q_SCALE, k=k.astype(F32), eb=jnp.exp(b), ebm=jnp.exp(b - b_mid),
                  emb=jnp.exp(b_mid - b), elb=jnp.exp(b_last - b), ebl=jnp.exp(b_last))


def _gla_fwd_tile(q_ref, k_ref, v_ref, g_ref, z_ref, wg_ref, bg_ref, gn_ref, mix_ref, o_ref, st_ref, state, tb):
    @pl.when(pl.program_id(0) == 0)
    def _():
        state[...] = jnp.zeros_like(state)

    mats = _chunk_matrices()
    masks = _head_masks()
    wgv, bgv = wg_ref[...], bg_ref[...]

    for sb in range(tb // SUB):
        rows = slice(sb * SUB, (sb + 1) * SUB)
        d = _decay_terms(z_ref[rows, :], q_ref[rows, :], k_ref[rows, :], wgv, bgv, mats)
        kem_b = (d.k * d.emb).astype(BF16)
        qem = d.q * d.ebm
        for h in range(HEADS):
            cols = slice(h * DV, (h + 1) * DV)
            a = jnp.where(mats.causal, _nt((qem * masks[h]).astype(BF16), kem_b), 0.0)
            o_ref[rows, cols] = _nn(a.astype(BF16), v_ref[rows, cols])
        qe0_b = (d.q * d.eb).astype(BF16)
        kdec_b = (d.k * d.elb).astype(BF16)
        for c in range(SUB // CHUNK):
            loc = slice(c * CHUNK, (c + 1) * CHUNK)
            glob = slice(sb * SUB + c * CHUNK, sb * SUB + (c + 1) * CHUNK)
            st = state[...]
            st_b = st.astype(BF16)
            st_ref[sb * (SUB // CHUNK) + c] = st_b
            o_ref[glob, :] += _nt(qe0_b[loc], st_b)
            u = _tn(v_ref[glob, :], kdec_b[loc])
            state[...] = st * d.ebl[c * CHUNK:c * CHUNK + 1] + jnp.where(mats.heads, u, 0.0)

    gnv = gn_ref[...]
    for h in range(HEADS):
        cols = slice(h * DV, (h + 1) * DV)
        oh = o_ref[:, cols]
        r = lax.rsqrt(_rowmean(oh * oh) + EPS)
        gh = g_ref[:, cols].astype(F32)
        mix_ref[:, cols] = (oh * r * gnv * (gh * jax.nn.sigmoid(gh))).astype(BF16)


def _group_mean(v, gmat):
    return _nn(v.astype(BF16), gmat)


def _shifted_copies(buf, sh, rows):
    for k in range(1, SUBLANES):
        sh[k - 1] = buf[pl.ds(k, rows), :]


def _tap(buf, sh, off, r0, n):
    k, base = off % SUBLANES, off - off % SUBLANES
    rows = pl.ds(r0 + base if isinstance(r0, int) else pl.multiple_of(r0 + base, SUBLANES), n)
    return buf[rows, :] if k == 0 else sh[k - 1, rows, :]


def _conv_fwd_tile(ci_ref, cg_ref, w_ref, b_ref, g_ref, be_ref, gm_ref, mix_ref, uc_ref, ubuf, ush, tm):
    sh_rows = tm + HALO - SUBLANES

    @pl.when(pl.program_id(0) == 0)
    def _():
        ubuf[0:HALO, :] = jnp.zeros((HALO, CONV), F32)

    ubuf[HALO:, :] = ci_ref[...].astype(F32) * jax.nn.sigmoid(cg_ref[...].astype(F32))
    _shifted_copies(ubuf, ush, sh_rows)
    for s in range(tm // FWD_STRIP):
        acc = jnp.zeros((FWD_STRIP, CONV), F32) + b_ref[...]
        for j in range(CONV_W):
            acc = acc + w_ref[j:j + 1, :] * _tap(ubuf, ush, HALO - (CONV_W - 1) + j, s * FWD_STRIP, FWD_STRIP)
        uc_ref[s * FWD_STRIP:(s + 1) * FWD_STRIP, :] = acc
    ubuf[0:HALO, :] = ubuf[tm:tm + HALO, :]
    gm = gm_ref[...]
    ucv = uc_ref[...]
    d = ucv - _group_mean(ucv, gm)
    var = _group_mean(d * d, gm)
    yn = d * lax.rsqrt(var + EPS) * g_ref[...] + be_ref[...]
    mix_ref[...] = (yn * jax.nn.sigmoid(yn)).astype(BF16)


def _mix_fwd(proj, wg, bg, gn, conv_w, conv_b, cn_g, cn_b, gmat, shards):
    T = proj.shape[0]
    tb = min(T, TOKEN_TILE)
    cpb = tb // CHUNK
    n_comm = len(shards)
    kinds = ["gather"] * n_comm
    comm_start, comm_wait = _hosted_comm(kinds, 15, 5, n_comm, T // tb)

    def body(*refs):
        gla_in, conv_in = refs[:8], refs[8:15]
        gla_out, conv_out = refs[15 + n_comm:18 + n_comm], refs[18 + n_comm:20 + n_comm]
        state, ubuf, ush = refs[20 + 2 * n_comm:23 + 2 * n_comm]
        comm_start(refs)
        _gla_fwd_tile(*gla_in, *gla_out, state, tb)
        _conv_fwd_tile(*conv_in, *conv_out, ubuf, ush, tb)
        comm_wait(refs)

    nc = T // CHUNK
    tok = lambda w, col: pl.BlockSpec((tb, w), lambda i: (i, col))
    res = pl.pallas_call(
        body, name="mix_fwd", grid=(T // tb,),
        out_shape=[jax.ShapeDtypeStruct((T, VAL), BF16), jax.ShapeDtypeStruct((T, VAL), F32),
                   jax.ShapeDtypeStruct((nc, VAL, KEY), BF16), jax.ShapeDtypeStruct((T, CONV), BF16),
                   jax.ShapeDtypeStruct((T, CONV), F32)] + _comm_out_shapes(kinds, shards),
        in_specs=[tok(KEY, P_Q // KEY), tok(KEY, P_K // KEY), tok(VAL, P_V // VAL), tok(VAL, P_G // VAL),
                  tok(Z_PAD, P_Z // Z_PAD), _const((Z_PAD, KEY)), _const((1, KEY)), _const((1, DV)),
                  tok(CONV, P_CI // CONV), tok(CONV, P_CG // CONV), _const((HALO, CONV)), _const((1, CONV)),
                  _const((1, CONV)), _const((1, CONV)), _const((CONV, CONV))] + [ANY] * n_comm,
        out_specs=[tok(VAL, 0), tok(VAL, 0), pl.BlockSpec((cpb, VAL, KEY), lambda i: (i, 0, 0)), tok(CONV, 0),
                   tok(CONV, 0)] + [ANY] * n_comm,
        scratch_shapes=[pltpu.VMEM((VAL, KEY), F32), pltpu.VMEM((tb + HALO, CONV), F32),
                        pltpu.VMEM((SUBLANES - 1, tb + HALO - SUBLANES, CONV), F32)] + _comm_scratch(n_comm),
        compiler_params=_params(("arbitrary",)),
    )(proj, proj, proj, proj, proj, wg, bg, gn, proj, proj, conv_w, conv_b, cn_g, cn_b, gmat, *shards)
    return res[0], res[1], res[2], res[3], res[4], res[5:]


def _rms_bwd(dy, xhat, r, g):
    dyg = dy * g
    return r * (dyg - xhat * _rowmean(dyg * xhat))


def _mlp_fwd_bwd(x, mix_a, mix_c, tgt, w_out, g2, w1t, w2, gf):
    T = x.shape[0]
    tm = min(T, MLP_TILE)
    inv_d = 1.0 / D_MODEL

    def body(x_ref, ma_ref, mc_ref, t_ref, wo_ref, g2_ref, w1_ref, w2_ref, gf_ref,
             dh1_ref, dmix_ref, hn_ref, ff_ref, da_ref, dh2_ref, loss_ref, dgf_ref, dg2_ref):
        @pl.when(pl.program_id(0) == 0)
        def _():
            loss_ref[...] = jnp.zeros_like(loss_ref)
            dgf_ref[...] = jnp.zeros_like(dgf_ref)
            dg2_ref[...] = jnp.zeros_like(dg2_ref)

        g2v, gfv = g2_ref[...], gf_ref[...]
        h1 = x_ref[...] + _nn(ma_ref[...], wo_ref[0:VAL, :]) + _nn(mc_ref[...], wo_ref[VAL:, :])
        r2 = lax.rsqrt(_rowmean(h1 * h1) + EPS)
        h1hat = h1 * r2
        hn = (h1hat * g2v).astype(BF16)
        hn_ref[...] = hn
        relu_a = jnp.maximum(_nt(hn, w1_ref[...]), 0.0)
        ff = (relu_a * relu_a).astype(BF16)
        ff_ref[...] = ff
        h2 = h1 + _nn(ff, w2_ref[...])
        rf = lax.rsqrt(_rowmean(h2 * h2) + EPS)
        h2hat = h2 * rf
        err = h2hat * gfv - t_ref[...]
        loss_ref[...] += (0.5 * inv_d) * _colsum(jnp.sum(err * err, axis=1, keepdims=True))
        dy = err * inv_d
        dgf_ref[...] += _colsum(dy * h2hat)
        dh2 = _rms_bwd(dy, h2hat, rf, gfv)
        dh2_b = dh2.astype(BF16)
        dh2_ref[...] = dh2_b
        da = (_nt(dh2_b, w2_ref[...]) * (2.0 * relu_a)).astype(BF16)
        da_ref[...] = da
        dhn = _nn(da, w1_ref[...])
        dg2_ref[...] += _colsum(dhn * h1hat)
        dh1 = dh2 + _rms_bwd(dhn, h1hat, r2, g2v)
        dh1_ref[...] = dh1
        dmix_ref[...] = _nt(dh1.astype(BF16), wo_ref[...]).astype(BF16)

    tok = lambda w: pl.BlockSpec((tm, w), lambda i: (i, 0))
    return pl.pallas_call(
        body, name="mlp_fwd_bwd", grid=(T // tm,),
        out_shape=[jax.ShapeDtypeStruct((T, D_MODEL), F32), jax.ShapeDtypeStruct((T, D_MODEL), BF16),
                   jax.ShapeDtypeStruct((T, D_MODEL), BF16), jax.ShapeDtypeStruct((T, D_FF), BF16),
                   jax.ShapeDtypeStruct((T, D_FF), BF16), jax.ShapeDtypeStruct((T, D_MODEL), BF16),
                   jax.ShapeDtypeStruct((1, 1), F32), jax.ShapeDtypeStruct((1, D_MODEL), F32),
                   jax.ShapeDtypeStruct((1, D_MODEL), F32)],
        in_specs=[tok(D_MODEL), tok(VAL), tok(CONV), tok(D_MODEL), _const((D_MODEL, D_MODEL)), _const((1, D_MODEL)),
                  _const((D_FF, D_MODEL)), _const((D_FF, D_MODEL)), _const((1, D_MODEL))],
        out_specs=[tok(D_MODEL), tok(D_MODEL), tok(D_MODEL), tok(D_FF), tok(D_FF), tok(D_MODEL),
                   pl.BlockSpec((1, 1), lambda i: (0, 0)), pl.BlockSpec((1, D_MODEL), lambda i: (0, 0)),
                   pl.BlockSpec((1, D_MODEL), lambda i: (0, 0))],
        compiler_params=_params(("arbitrary",)),
    )(x, mix_a, mix_c, tgt, w_out, g2, w1t, w2, gf)


def _silu_grad(v, s):
    return s * (1.0 + v * (1.0 - s))


def _conv_bwd_tile(ci_ref, cg_ref, uc_ref, dm_ref, w_ref, g_ref, be_ref, gm_ref,
                   dpc_ref, dw_ref, db_ref, dg_ref, dbe_ref, dbuf, dsh, dwacc, tm, nt):
    step = pl.program_id(0)
    sh_rows = tm + HALO - SUBLANES

    @pl.when(step == 0)
    def _():
        dbuf[tm:, :] = jnp.zeros((HALO, CONV), F32)
        dwacc[...] = jnp.zeros_like(dwacc)
        db_ref[...] = jnp.zeros_like(db_ref)
        dg_ref[...] = jnp.zeros_like(dg_ref)
        dbe_ref[...] = jnp.zeros_like(dbe_ref)

    gm, gv = gm_ref[...], g_ref[...]
    ucv = uc_ref[...]
    d = ucv - _group_mean(ucv, gm)
    rs = lax.rsqrt(_group_mean(d * d, gm) + EPS)
    yhat = d * rs
    yn = yhat * gv + be_ref[...]
    dyn = dm_ref[...].astype(F32) * _silu_grad(yn, jax.nn.sigmoid(yn))
    dg_ref[...] += _colsum(dyn * yhat)
    dbe_ref[...] += _colsum(dyn)
    dyh = dyn * gv
    duc = rs * (dyh - _group_mean(dyh, gm) - yhat * _group_mean(dyh * yhat, gm))
    db_ref[...] += _colsum(duc)
    dbuf[0:tm, :] = duc
    _shifted_copies(dbuf, dsh, sh_rows)

    def strip(s, carry):
        r0 = pl.multiple_of(s * STRIP, STRIP)
        rows = pl.ds(r0, STRIP)
        cin = ci_ref[rows, :].astype(F32)
        sg = jax.nn.sigmoid(cg_ref[rows, :].astype(F32))
        u = cin * sg
        du = jnp.zeros((STRIP, CONV), F32)
        for j in range(CONV_W):
            dj = _tap(dbuf, dsh, CONV_W - 1 - j, r0, STRIP)
            du = du + w_ref[j:j + 1, :] * dj
            p = u * dj
            fold = p[0:SUBLANES]
            for q in range(1, STRIP // SUBLANES):
                fold = fold + p[q * SUBLANES:(q + 1) * SUBLANES, :]
            dwacc[j * SUBLANES:(j + 1) * SUBLANES, :] += fold
        dpc_ref[rows, 0:CONV] = (du * sg).astype(BF16)
        dpc_ref[rows, CONV:] = (du * cin * sg * (1.0 - sg)).astype(BF16)
        return carry

    lax.fori_loop(0, tm // STRIP, strip, 0)
    dbuf[tm:, :] = dbuf[0:HALO, :]

    @pl.when(step == nt - 1)
    def _():
        dw_ref[...] = jnp.zeros_like(dw_ref)
        for j in range(CONV_W):
            dw_ref[j:j + 1, :] = _colsum(dwacc[j * SUBLANES:(j + 1) * SUBLANES, :])


def _conv_bwd(proj, uc, dmix, conv_w, cn_g, cn_b, gmat):
    T = proj.shape[0]
    tm = min(T, TOKEN_TILE)
    nt = T // tm
    sh_rows = tm + HALO - SUBLANES

    def body(*refs):
        _conv_bwd_tile(*refs, tm, nt)

    rev = lambda i: nt - 1 - i
    tile = lambda col: pl.BlockSpec((tm, CONV), lambda i: (rev(i), col))
    acc = lambda rows: pl.BlockSpec((rows, CONV), lambda i: (0, 0))
    return pl.pallas_call(
        body, name="conv_bwd", grid=(nt,),
        out_shape=[jax.ShapeDtypeStruct((T, 2 * CONV), BF16),
                   jax.ShapeDtypeStruct((HALO, CONV), F32), jax.ShapeDtypeStruct((1, CONV), F32),
                   jax.ShapeDtypeStruct((1, CONV), F32), jax.ShapeDtypeStruct((1, CONV), F32)],
        in_specs=[tile(P_CI // CONV), tile(P_CG // CONV), tile(0), tile(1),
                  _const((HALO, CONV)), _const((1, CONV)), _const((1, CONV)), _const((CONV, CONV))],
        out_specs=[pl.BlockSpec((tm, 2 * CONV), lambda i: (rev(i), 0)), acc(HALO), acc(1), acc(1), acc(1)],
        scratch_shapes=[pltpu.VMEM((tm + HALO, CONV), F32), pltpu.VMEM((SUBLANES - 1, sh_rows, CONV), F32),
                        pltpu.VMEM((HALO * SUBLANES, CONV), F32)],
        compiler_params=_params(("arbitrary",)),
    )(proj, proj, uc, dmix, conv_w, cn_g, cn_b, gmat)


def _gla_bwd_tile(q_ref, k_ref, v_ref, g_ref, z_ref, o_ref, st_ref, dm_ref, wg_ref, bg_ref, gn_ref,
                  dpg_ref, dwg_ref, dbg_ref, dgn_ref, dstate, do_scr, dv_scr, tb):
    @pl.when(pl.program_id(0) == 0)
    def _():
        dstate[...] = jnp.zeros_like(dstate)
        dwg_ref[...] = jnp.zeros_like(dwg_ref)
        dbg_ref[...] = jnp.zeros_like(dbg_ref)
        dgn_ref[...] = jnp.zeros_like(dgn_ref)

    gnv = gn_ref[...]
    dgn = jnp.zeros((1, DV), F32)
    for h in range(HEADS):
        cols = slice(h * DV, (h + 1) * DV)
        oh = o_ref[:, cols]
        r = lax.rsqrt(_rowmean(oh * oh) + EPS)
        ohat = oh * r
        gh = g_ref[:, cols].astype(F32)
        sg = jax.nn.sigmoid(gh)
        dmx = dm_ref[:, cols].astype(F32)
        don = dmx * (gh * sg)
        dpg_ref[:, P_G + h * DV:P_G + (h + 1) * DV] = (dmx * (ohat * gnv) * _silu_grad(gh, sg)).astype(BF16)
        dgn = dgn + _colsum(don * ohat)
        do_scr[:, cols] = _rms_bwd(don, ohat, r, gnv)
    dgn_ref[...] += dgn

    mats = _chunk_matrices()
    masks = _head_masks()
    wgv, bgv = wg_ref[...], bg_ref[...]
    n_chunks = SUB // CHUNK

    for sb in reversed(range(tb // SUB)):
        rows = slice(sb * SUB, (sb + 1) * SUB)
        zs = z_ref[rows, :]
        d = _decay_terms(zs, q_ref[rows, :], k_ref[rows, :], wgv, bgv, mats)
        qem = d.q * d.ebm
        qem_b = qem.astype(BF16)
        kem_b = (d.k * d.emb).astype(BF16)
        dq = jnp.zeros((SUB, KEY), F32)
        dk = jnp.zeros((SUB, KEY), F32)
        for h in range(HEADS):
            hm = masks[h]
            cols = slice(h * DV, (h + 1) * DV)
            do_b = do_scr[rows, cols].astype(BF16)
            vh = v_ref[rows, cols]
            da = jnp.where(mats.causal, _nt(do_b, vh), 0.0).astype(BF16)
            da_t = jnp.where(mats.causal_t, _nt(vh, do_b), 0.0).astype(BF16)
            a_t = jnp.where(mats.causal_t, _nt(kem_b, (qem * hm).astype(BF16)), 0.0).astype(BF16)
            dq = dq + hm * _nn(da, kem_b)
            dk = dk + hm * _nn(da_t, qem_b)
            dv_scr[rows, cols] = _nn(a_t, do_b)
        dq = dq * d.ebm
        dk = dk * d.emb

        qe0_b = (d.q * d.eb).astype(BF16)
        kdec_b = (d.k * d.elb).astype(BF16)
        dq_st, dk_st, last = [None] * n_chunks, [None] * n_chunks, [None] * n_chunks
        for c in reversed(range(n_chunks)):
            loc = slice(c * CHUNK, (c + 1) * CHUNK)
            glob = slice(sb * SUB + c * CHUNK, sb * SUB + (c + 1) * CHUNK)
            st_b = st_ref[sb * n_chunks + c]
            ds = dstate[...]
            ds_b = ds.astype(BF16)
            do_c = do_scr[glob, :].astype(BF16)
            ebl_c = d.ebl[c * CHUNK:c * CHUNK + 1]
            dk_c = _nn(v_ref[glob, :], ds_b) * d.elb[loc]
            dq_st[c] = _nn(do_c, st_b) * d.eb[loc]
            dk_st[c] = dk_c
            last_c = _colsum(d.k[loc] * dk_c) + ebl_c * _colsum(st_b.astype(F32) * ds)
            last[c] = jnp.broadcast_to(last_c, (CHUNK, KEY))
            dpg_ref[glob, P_V:P_G] = (dv_scr[glob, :] + _nt(kdec_b[loc], ds_b)).astype(BF16)
            dstate[...] = ds * ebl_c + jnp.where(mats.heads, _tn(do_c, qe0_b[loc]), 0.0)
        dq = dq + jnp.concatenate(dq_st, axis=0)
        dk = dk + jnp.concatenate(dk_st, axis=0)
        dpg_ref[rows, P_Q:P_K] = (dq * Q_SCALE).astype(BF16)
        dpg_ref[rows, P_K:P_V] = dk.astype(BF16)
        hi, lo = _split_bf16(d.q * dq - d.k * dk)
        dla = _nn(mats.tri_t, hi) + _nn(mats.tri_t, lo) + jnp.concatenate(last, axis=0)
        dal = dla * (1.0 / GATE_TAU) * jax.nn.sigmoid(-d.al)
        dal_b = dal.astype(BF16)
        dpg_ref[rows, OFF_Z:] = _nt(dal_b, wgv).astype(BF16)
        dwg_ref[...] += _tn(zs, dal_b)
        dbg_ref[...] += _colsum(dal)


def _gla_bwd(proj, o, states, dmix, wg, bg, gn, parts):
    T = proj.shape[0]
    tb = min(T, TOKEN_TILE)
    cpb = tb // CHUNK
    nb = T // tb
    n_comm = len(parts)
    kinds = ["exchange"] * n_comm
    comm_start, comm_wait = _hosted_comm(kinds, 11, 4, n_comm, nb)

    def body(*refs):
        comm_start(refs)
        _gla_bwd_tile(*refs[:11], *refs[11 + n_comm:15 + n_comm], *refs[15 + 2 * n_comm:18 + 2 * n_comm], tb)
        comm_wait(refs)

    rev = lambda i: nb - 1 - i
    blk = lambda w, col: pl.BlockSpec((tb, w), lambda i: (rev(i), col))
    res = pl.pallas_call(
        body, name="gla_bwd", grid=(nb,),
        out_shape=[jax.ShapeDtypeStruct((T, D_GLA), BF16), jax.ShapeDtypeStruct((Z_PAD, KEY), F32),
                   jax.ShapeDtypeStruct((1, KEY), F32), jax.ShapeDtypeStruct((1, DV), F32)]
        + _comm_out_shapes(kinds, parts),
        in_specs=[blk(KEY, P_Q // KEY), blk(KEY, P_K // KEY), blk(VAL, P_V // VAL), blk(VAL, P_G // VAL),
                  blk(Z_PAD, P_Z // Z_PAD), blk(VAL, 0),
                  pl.BlockSpec((cpb, VAL, KEY), lambda i: (rev(i), 0, 0)), blk(VAL, 0),
                  _const((Z_PAD, KEY)), _const((1, KEY)), _const((1, DV))] + [ANY] * n_comm,
        out_specs=[blk(D_GLA, 0),
                   pl.BlockSpec((Z_PAD, KEY), lambda i: (0, 0)), pl.BlockSpec((1, KEY), lambda i: (0, 0)),
                   pl.BlockSpec((1, DV), lambda i: (0, 0))] + [ANY] * n_comm,
        scratch_shapes=[pltpu.VMEM((VAL, KEY), F32), pltpu.VMEM((tb, VAL), F32), pltpu.VMEM((tb, VAL), F32)]
        + _comm_scratch(n_comm),
        compiler_params=_params(("arbitrary",)),
    )(proj, proj, proj, proj, proj, o, states, dmix, wg, bg, gn, *parts)
    return res[:4], res[4:]


def _inproj_bwd(x, g1, w_in_t, dh1, dp_gla, dp_conv):
    T = x.shape[0]
    tm = min(T, TOKEN_TILE)

    def body(x_ref, g_ref, w_ref, dh1_ref, dpg_ref, dpc_ref, dx_ref, dg1_ref):
        @pl.when(pl.program_id(0) == 0)
        def _():
            dg1_ref[...] = jnp.zeros_like(dg1_ref)

        dxn = _nn(dpg_ref[...], w_ref[0:D_GLA, :]) + _nn(dpc_ref[...], w_ref[OFF_C:D_IN, :])
        xv = x_ref[...]
        r = lax.rsqrt(_rowmean(xv * xv) + EPS)
        xhat = xv * r
        dg1_ref[...] += _colsum(dxn * xhat)
        dx_ref[...] = dh1_ref[...] + _rms_bwd(dxn, xhat, r, g_ref[...])

    tok = lambda w: pl.BlockSpec((tm, w), lambda i: (i, 0))
    return pl.pallas_call(
        body, name="inproj_bwd", grid=(T // tm,),
        out_shape=[jax.ShapeDtypeStruct((T, D_MODEL), F32), jax.ShapeDtypeStruct((1, D_MODEL), F32)],
        in_specs=[tok(D_MODEL), _const((1, D_MODEL)), _const((D_IN, D_MODEL)), tok(D_MODEL), tok(D_GLA),
                  tok(2 * CONV)],
        out_specs=[tok(D_MODEL), pl.BlockSpec((1, D_MODEL), lambda i: (0, 0))],
        compiler_params=_params(("arbitrary",)),
    )(x, g1, w_in_t, dh1, dp_gla, dp_conv)


def _wgrad_in(xn, dp_gla, dp_conv):
    T = xn.shape[0]
    tt = min(T, WGRAD_TILE // 2)
    nt = T // tt

    def body(xn_ref, dpg_ref, dpc_ref, o_ref, acc):
        @pl.when(pl.program_id(0) == 0)
        def _():
            acc[...] = jnp.zeros_like(acc)

        xv = xn_ref[...]
        acc[0:OFF_C, :] += _tn(dpg_ref[...], xv)[0:OFF_C]
        acc[OFF_C:, :] += _tn(dpc_ref[...], xv)

        @pl.when(pl.program_id(0) == nt - 1)
        def _():
            o_ref[...] = acc[...].astype(BF16)

    tok = lambda w: pl.BlockSpec((tt, w), lambda t: (t, 0))
    return pl.pallas_call(
        body, name="wgrad_in", grid=(nt,), out_shape=jax.ShapeDtypeStruct((D_IN, D_MODEL), BF16),
        in_specs=[tok(D_MODEL), tok(D_GLA), tok(2 * CONV)],
        out_specs=pl.BlockSpec((D_IN, D_MODEL), lambda t: (0, 0), pipeline_mode=pl.Buffered(1)),
        scratch_shapes=[pltpu.VMEM((D_IN, D_MODEL), F32)],
        compiler_params=_params(("arbitrary",)),
    )(xn, dp_gla, dp_conv)


def _wgrad_out(mix_a, mix_c, dh1):
    T = dh1.shape[0]
    tt = min(T, WGRAD_TILE // 2)
    nt = T // tt

    def body(a_ref, c_ref, b_ref, o_ref, acc):
        @pl.when(pl.program_id(0) == 0)
        def _():
            acc[...] = jnp.zeros_like(acc)

        b = b_ref[...].astype(BF16)
        acc[0:VAL, :] += _tn(a_ref[...], b)
        acc[VAL:, :] += _tn(c_ref[...], b)

        @pl.when(pl.program_id(0) == nt - 1)
        def _():
            o_ref[...] = acc[...].astype(BF16)

    tok = lambda w: pl.BlockSpec((tt, w), lambda t: (t, 0))
    return pl.pallas_call(
        body, name="wgrad_out", grid=(nt,), out_shape=jax.ShapeDtypeStruct((D_MODEL, D_MODEL), BF16),
        in_specs=[tok(VAL), tok(CONV), tok(D_MODEL)],
        out_specs=pl.BlockSpec((D_MODEL, D_MODEL), lambda t: (0, 0)),
        scratch_shapes=[pltpu.VMEM((D_MODEL, D_MODEL), F32)],
        compiler_params=_params(("arbitrary",)),
    )(mix_a, mix_c, dh1)


def _wgrad(a, b, name, tk, tn, col_block=None):
    T, K = a.shape
    N = b.shape[1]
    tt = min(T, WGRAD_TILE)
    nt = T // tt

    def body(a_ref, b_ref, o_ref, acc):
        @pl.when(pl.program_id(2) == 0)
        def _():
            acc[...] = jnp.zeros_like(acc)

        acc[...] += _tn(a_ref[...], b_ref[...].astype(BF16))

        @pl.when(pl.program_id(2) == nt - 1)
        def _():
            if col_block is None:
                o_ref[...] = acc[...].astype(BF16)
            else:
                for q in range(tn // col_block):
                    o_ref[q] = acc[:, q * col_block:(q + 1) * col_block].astype(BF16)

    if col_block is None:
        out_shape = jax.ShapeDtypeStruct((K, N), BF16)
        out_spec = pl.BlockSpec((tk, tn), lambda i, j, t: (i, j))
    else:
        assert tk == K
        out_shape = jax.ShapeDtypeStruct((N // col_block, K, col_block), BF16)
        out_spec = pl.BlockSpec((tn // col_block, tk, col_block), lambda i, j, t: (j, 0, 0))
    return pl.pallas_call(
        body, name=name, grid=(K // tk, N // tn, nt), out_shape=out_shape,
        in_specs=[pl.BlockSpec((tt, tk), lambda i, j, t: (t, i)), pl.BlockSpec((tt, tn), lambda i, j, t: (t, j))],
        out_specs=out_spec, scratch_shapes=[pltpu.VMEM((tk, tn), F32)],
        compiler_params=_params(("arbitrary", "arbitrary", "arbitrary")),
    )(a, b)


def _adam_math(w, g, m, v):
    m = ADAM_B1 * m + (1.0 - ADAM_B1) * g
    v = ADAM_B2 * v + (1.0 - ADAM_B2) * (g * g)
    m_hat = m / (1.0 - ADAM_B1 ** ADAM_STEP)
    v_hat = v / (1.0 - ADAM_B2 ** ADAM_STEP)
    delta = -ADAM_LR * (m_hat / (jnp.sqrt(v_hat) + ADAM_EPS) + ADAM_WD * w)
    return delta, m, v


def _sum8(ref):
    g = ref[0].astype(F32)
    for s in range(1, N_DEV):
        g = g + ref[s].astype(F32)
    return g


def _adam_big(parts, w, m, v, name):
    R, C = w.shape
    tr = ADAM_ROWS if R % ADAM_ROWS == 0 else R

    def body(p_ref, w_ref, m_ref, v_ref, g_ref, d_ref, nm_ref, nv_ref):
        g = _sum8(p_ref)
        g_ref[...] = g
        d_ref[...], nm_ref[...], nv_ref[...] = _adam_math(w_ref[...], g, m_ref[...], v_ref[...])

    row = pl.BlockSpec((tr, C), lambda i: (i, 0))
    return pl.pallas_call(
        body, name=name, grid=(R // tr,), out_shape=[jax.ShapeDtypeStruct((R, C), F32)] * 4,
        in_specs=[pl.BlockSpec((N_DEV, tr, C), lambda i: (0, i, 0)), row, row, row], out_specs=[row] * 4,
        compiler_params=_params(("arbitrary",)),
    )(parts, w, m, v)


def _sum_small(parts):
    def body(p_ref, o_ref):
        o_ref[...] = _sum8(p_ref)

    return pl.pallas_call(body, name="sum_small", out_shape=jax.ShapeDtypeStruct(parts.shape[1:], F32))(parts)


def _adam_small(gs, ws, ms, vs):
    n = len(gs)

    def body(*refs):
        g_refs, w_refs, m_refs, v_refs = refs[:n], refs[n:2 * n], refs[2 * n:3 * n], refs[3 * n:4 * n]
        outs = refs[4 * n:]
        for i in range(n):
            d, nm, nv = _adam_math(w_refs[i][...], g_refs[i][...], m_refs[i][...], v_refs[i][...])
            outs[i][...] = d
            outs[n + i][...] = nm
            outs[2 * n + i][...] = nv

    shapes = [jax.ShapeDtypeStruct(w.shape, F32) for w in ws]
    res = pl.pallas_call(body, name="adam_small", out_shape=shapes * 3)(*gs, *ws, *ms, *vs)
    return res[:n], res[n:2 * n], res[2 * n:]


def _group_matrix():
    gi = lax.broadcasted_iota(jnp.int32, (CONV, CONV), 0) // (CONV // GROUPS)
    gj = lax.broadcasted_iota(jnp.int32, (CONV, CONV), 1) // (CONV // GROUPS)
    return jnp.where(gi == gj, GROUPS / CONV, 0.0).astype(BF16)


_SMALL = [("loss", 8), ("dg1", 8), ("dbg", 2), ("dgn", 1), ("dconv_b", 4), ("dcn_g", 4), ("dcn_b", 4), ("dg2", 8),
          ("dgf", 8), ("dwg", 32), ("dconv_w", 124)]


def _pad8(rows):
    return -(-rows // 8) * 8


def kernel(x, norm1_g, w_in, w_gate_up, b_gate, gla_norm_g, conv_w, conv_b, conv_norm_g, conv_norm_b, w_out, norm2_g, w_mlp_in, w_mlp_out, final_norm_g, loss_target, m_norm1_g, m_w_in, m_w_gate_up, m_b_gate, m_gla_norm_g, m_conv_w, m_conv_b, m_conv_norm_g, m_conv_norm_b, m_w_out, m_norm2_g, m_w_mlp_in, m_w_mlp_out, m_final_norm_g, v_norm1_g, v_w_in, v_w_gate_up, v_b_gate, v_gla_norm_g, v_conv_w, v_conv_b, v_conv_norm_g, v_conv_norm_b, v_w_out, v_norm2_g, v_w_mlp_in, v_w_mlp_out, v_final_norm_g):
    x_idx = lax.axis_index("x")
    y_idx = lax.axis_index("y")
    c_idx = lax.axis_index("c")
    me = 4 * x_idx + 2 * y_idx + c_idx
    xs, tgt = x[0], loss_target[0]
    gf = final_norm_g.reshape(1, D_MODEL)
    gmat = _group_matrix()

    small_shard = jnp.zeros((_pad8(RANK + CONV_W), LANES), F32)
    small_shard = small_shard.at[0:RANK, 0:KEY // N_DEV].set(w_gate_up[0])
    small_shard = small_shard.at[RANK:RANK + CONV_W, 0:CONV // N_DEV].set(conv_w[0])
    g_in, g_small = _gather_two_level([w_in[0].T.astype(BF16), small_shard], "gather_w_in")
    w_in_t = g_in.reshape(D_IN, D_MODEL)
    wg_full = jnp.concatenate([g_small[d, 0:RANK, 0:KEY // N_DEV] for d in range(N_DEV)], axis=1)
    wg_pad = jnp.pad(wg_full, ((0, Z_PAD - RANK), (0, 0))).astype(BF16)
    conv_w_full = jnp.concatenate([g_small[d, RANK:RANK + CONV_W, 0:CONV // N_DEV] for d in range(N_DEV)], axis=1)
    conv_w_pad = jnp.pad(conv_w_full, ((0, HALO - CONV_W), (0, 0)))

    proj, xn, (g_w2,) = _inproj_fwd(xs, norm1_g, w_in_t, [w_mlp_out[0].astype(BF16)])
    mix_a, o, states, mix_c, uc, (g_out, g_w1) = _mix_fwd(
        proj, wg_pad, b_gate, gla_norm_g, conv_w_pad, conv_b, conv_norm_g, conv_norm_b, gmat,
        [w_out[0].astype(BF16), w_mlp_in[0].T.astype(BF16)])
    w_out_full = g_out.reshape(D_MODEL, D_MODEL)
    w1t_full = g_w1.reshape(D_FF, D_MODEL)
    w2_full = g_w2.reshape(D_FF, D_MODEL)
    dh1, dmix, hn, ff, da, dh2, loss, dgf, dg2 = _mlp_fwd_bwd(xs, mix_a, mix_c, tgt, w_out_full, norm2_g, w1t_full,
                                                              w2_full, gf)

    dw1 = _wgrad(hn, da, "wgrad_mlp_in", WGRAD_BLOCK, WGRAD_BLOCK, col_block=D_FF // N_DEV)
    dw2 = _wgrad(ff, dh2, "wgrad_mlp_out", WGRAD_BLOCK, WGRAD_BLOCK)
    dw_out = _wgrad_out(mix_a, mix_c, dh1)
    dp_conv, dconv_w, dconv_b, dcn_g, dcn_b = _conv_bwd(proj, uc, dmix, conv_w_pad, conv_norm_g, conv_norm_b, gmat)
    (dp_gla, dwg, dbg, dgn), (p_w1, p_w2, p_out) = _gla_bwd(
        proj, o, states, dmix, wg_pad, b_gate, gla_norm_g,
        [dw1, dw2.reshape(N_DEV, D_FF // N_DEV, D_MODEL), dw_out.reshape(N_DEV, D_MODEL // N_DEV, D_MODEL)])
    dw_in = _wgrad_in(xn, dp_gla, dp_conv).reshape(N_DEV, SHARD_IN, D_MODEL)
    send_sems, recv_sems, dw_in_thru, land, token = _split_start("exchange", dw_in, jnp.copy(dw_in),
                                                                 "exchange_w_in_start")
    dx, dg1 = _inproj_bwd(xs, norm1_g + token[0:1, 0:1], w_in_t, dh1, dp_gla, dp_conv)
    p_in = _split_wait("exchange", send_sems, recv_sems, dw_in_thru, land, dg1, "exchange_w_in_wait")

    small = dict(loss=jnp.zeros((SUBLANES, LANES), F32) + loss, dg1=dg1, dbg=dbg, dgn=dgn, dconv_b=dconv_b, dcn_g=dcn_g,
                 dcn_b=dcn_b, dg2=dg2, dgf=dgf, dwg=dwg[0:RANK], dconv_w=dconv_w[0:CONV_W])
    pack = jnp.concatenate([jnp.pad(small[name].reshape(rows, LANES), ((0, _pad8(rows) - rows), (0, 0)))
                            for name, rows in _SMALL], axis=0)
    s_send, s_recv, pack_thru, pack_land, s_token = _split_start(
        "gather", pack, jnp.broadcast_to(pack, (N_DEV,) + pack.shape) + 0.0, "gather_small_start")

    gi, di, mi, vi = _adam_big(p_in, w_in[0].T, m_w_in[0].T, v_w_in[0].T, "adam_w_in")
    go, do, mo, vo = _adam_big(p_out, w_out[0] + s_token[0:1, 0:1], m_w_out[0], v_w_out[0], "adam_w_out")
    ga, da_, ma, va = _adam_big(p_w1, w_mlp_in[0], m_w_mlp_in[0], v_w_mlp_in[0], "adam_w_mlp_in")
    gb, db, mb, vb = _adam_big(p_w2, w_mlp_out[0], m_w_mlp_out[0], v_w_mlp_out[0], "adam_w_mlp_out")
    cut = lambda a: a.T[None]

    g_pack = _split_wait("gather", s_send, s_recv, pack_thru, pack_land, go[0:8, 0:128] + ga[0:8, 0:128]
                         + gb[0:8, 0:128], "gather_small_wait")
    summed = _sum_small(g_pack)
    small_g = {}
    at = 0
    for name, rows in _SMALL:
        small_g[name] = summed[at:at + rows]
        at += _pad8(rows)
    loss_out = small_g["loss"][0, 0]
    wg_cols = KEY // N_DEV
    cw_cols = CONV // N_DEV
    g_small_list = [
        small_g["dg1"].reshape(1, D_MODEL),
        lax.dynamic_slice_in_dim(small_g["dwg"].reshape(RANK, KEY), me * wg_cols, wg_cols, axis=1)[None],
        small_g["dbg"].reshape(1, KEY), small_g["dgn"].reshape(1, DV),
        lax.dynamic_slice_in_dim(small_g["dconv_w"].reshape(CONV_W, CONV), me * cw_cols, cw_cols, axis=1)[None],
        small_g["dconv_b"].reshape(1, CONV), small_g["dcn_g"].reshape(1, CONV), small_g["dcn_b"].reshape(1, CONV),
        small_g["dg2"].reshape(1, D_MODEL), small_g["dgf"].reshape(1, D_MODEL),
    ]
    row = lambda a: a.reshape(1, D_MODEL)
    w_small = [norm1_g, w_gate_up, b_gate, gla_norm_g, conv_w, conv_b, conv_norm_g, conv_norm_b, norm2_g,
               row(final_norm_g)]
    m_small = [m_norm1_g, m_w_gate_up, m_b_gate, m_gla_norm_g, m_conv_w, m_conv_b, m_conv_norm_g, m_conv_norm_b,
               m_norm2_g, row(m_final_norm_g)]
    v_small = [v_norm1_g, v_w_gate_up, v_b_gate, v_gla_norm_g, v_conv_w, v_conv_b, v_conv_norm_g, v_conv_norm_b,
               v_norm2_g, row(v_final_norm_g)]
    d_small, nm_small, nv_small = _adam_small(g_small_list, w_small, m_small, v_small)
    flat = lambda lst: list(lst[:-1]) + [lst[-1].reshape(D_MODEL)]
    g_small_list, d_small, nm_small, nv_small = flat(g_small_list), flat(d_small), flat(nm_small), flat(nv_small)

    def order(s, w_in_v, w_out_v, w1_v, w2_v):
        return [s[0], w_in_v, s[1], s[2], s[3], s[4], s[5], s[6], s[7], w_out_v, s[8], w1_v, w2_v, s[9]]

    grads = order(g_small_list, cut(gi), go[None], ga[None], gb[None])
    deltas = order(d_small, cut(di), do[None], da_[None], db[None])
    new_m = order(nm_small, cut(mi), mo[None], ma[None], mb[None])
    new_v = order(nv_small, cut(vi), vo[None], va[None], vb[None])
    return (loss_out, dx[None], *grads, *deltas, *new_m, *new_v)
```

```python
from typing import NamedTuple

import jax
import jax.numpy as jnp
from jax import lax
from jax.experimental import pallas as pl
from jax.experimental.pallas import tpu as pltpu

F32 = jnp.float32
BF16 = jnp.bfloat16

N_DEV = 8
D_MODEL = 1024
HEADS = 4
DK = 64
DV = 128
KEY = HEADS * DK
VAL = HEADS * DV
RANK = 16
CONV = 512
GROUPS = 8
CONV_W = 31
HALO = 32
SUBLANES = 8
LANES = 128
STRIP = 32
FWD_STRIP = 16
TOKEN_TILE = 512
MLP_TILE = 256
WGRAD_TILE = 4096
WGRAD_BLOCK = 1024
ADAM_ROWS = 128
D_FF = 4096
D_IN = 2576
SHARD_IN = D_IN // N_DEV
CHUNK = 64
SUB = 256
EPS = 1e-6
GATE_TAU = 16.0
Q_SCALE = DK ** -0.5

P_Q, P_K, P_V, P_G, P_CI, P_CG, P_Z = 0, 256, 512, 1024, 1536, 2048, 2560
D_INP = 2688
Z_PAD = D_INP - P_Z
OFF_Z = 1536
OFF_C = OFF_Z + RANK
D_GLA = OFF_Z + Z_PAD

ADAM_LR = 0.001
ADAM_B1 = 0.9
ADAM_B2 = 0.999
ADAM_EPS = 1e-08
ADAM_WD = 0.01
ADAM_STEP = 10

V7X_VMEM_BYTES = 64 * 1024 * 1024
VMEM_LIMIT = V7X_VMEM_BYTES * 7 // 8

MESH = pl.DeviceIdType.MESH
ANY = pl.BlockSpec(memory_space=pl.ANY)


def _nn(a, b):
    return jnp.dot(a, b, preferred_element_type=F32)


def _nt(a, b):
    return lax.dot_general(a, b, (((1,), (1,)), ((), ())), preferred_element_type=F32)


def _tn(a, b):
    return lax.dot_general(a, b, (((0,), (0,)), ((), ())), preferred_element_type=F32)


def _params(sem=None):
    return pltpu.CompilerParams(dimension_semantics=sem, vmem_limit_bytes=VMEM_LIMIT)


def _const(shape):
    return pl.BlockSpec(shape, lambda *_: (0,) * len(shape), pipeline_mode=pl.Buffered(1))


def _colsum(v):
    return jnp.sum(v, axis=0, keepdims=True)


def _rowmean(v):
    return jnp.mean(v, axis=-1, keepdims=True)


def _split_bf16(v):
    hi = v.astype(BF16)
    return hi, (v - hi.astype(F32)).astype(BF16)


def _my_place():
    return lax.axis_index("x"), lax.axis_index("y"), lax.axis_index("c")


def _peer(j):
    x, y, c = _my_place()
    jx, jy, jc = (j >> 2) & 1, (j >> 1) & 1, j & 1
    px = 1 - x if jx else x
    py = 1 - y if jy else y
    pc = 1 - c if jc else c
    return (px, py, pc), 4 * px + 2 * py + pc


def _comm_plan(kinds, ins, outs, send_sems, recv_sems, local_sems, receives=True):
    x, y, c = _my_place()
    me = 4 * x + 2 * y + c
    own = lambda k, idx: ins[k] if kinds[k] == "gather" else ins[k].at[idx]
    local = [pltpu.make_async_copy(own(k, me), outs[k].at[me], local_sems.at[k]) for k in range(len(kinds))]
    sends, recvs = [], []
    for j in range(1, N_DEV):
        peer, peer_idx = _peer(j)
        for k in range(len(kinds)):
            sems = dict(send_sem=send_sems.at[k, j - 1], recv_sem=recv_sems.at[k, j - 1], device_id=peer,
                        device_id_type=MESH)
            sends.append(pltpu.make_async_remote_copy(src_ref=own(k, peer_idx), dst_ref=outs[k].at[me], **sems))
            if receives:
                recvs.append(pltpu.make_async_remote_copy(src_ref=own(k, me), dst_ref=outs[k].at[peer_idx], **sems))
    return local, sends, recvs


def _comm_start(plan):
    local, sends, _ = plan
    for cp in local + sends:
        cp.start()


def _comm_wait(plan):
    local, sends, recvs = plan
    for cp in recvs:
        cp.wait_recv()
    for cp in sends:
        cp.wait_send()
    for cp in local:
        cp.wait()


def _comm_scratch(n):
    return [pltpu.SemaphoreType.DMA((n, N_DEV - 1)), pltpu.SemaphoreType.DMA((n, N_DEV - 1)),
            pltpu.SemaphoreType.DMA((n,))]


def _comm_out_shapes(kinds, arrays):
    return [jax.ShapeDtypeStruct(((N_DEV,) + a.shape) if kind == "gather" else a.shape, a.dtype)
            for kind, a in zip(kinds, arrays)]


def _gather_two_level(shards, name):
    n = len(shards)

    def body(*refs):
        ins, outs = refs[:n], refs[n:2 * n]
        send_sems, recv_sems, local_sems = refs[2 * n:]
        x, y, c = _my_place()
        index = lambda px, py, pc: 4 * px + 2 * py + pc
        me, sibling = (x, y, c), (x, y, 1 - c)
        chips = [(1 - x, y), (x, 1 - y), (1 - x, 1 - y)]

        def copy(k, slot, block, to, src=None):
            rows = outs[k].at[index(*block)]
            return pltpu.make_async_remote_copy(
                src_ref=rows if src is None else src, dst_ref=rows, send_sem=send_sems.at[k, slot],
                recv_sem=recv_sems.at[k, slot], device_id=to, device_id_type=MESH)

        local = [pltpu.make_async_copy(ins[k], outs[k].at[index(*me)], local_sems.at[k]) for k in range(n)]
        first = []
        for k in range(n):
            first.append(copy(k, 0, me, sibling, src=ins[k]))
            first += [copy(k, 1 + j, me, (*chip, c), src=ins[k]) for j, chip in enumerate(chips)]
        for cp in local + first:
            cp.start()
        passed = []
        for j, chip in enumerate(chips):
            for k in range(n):
                copy(k, 1 + j, (*chip, c), me).wait_recv()
                cp = copy(k, 4 + j, (*chip, c), sibling)
                cp.start()
                passed.append(cp)
        for k in range(n):
            copy(k, 0, sibling, me).wait_recv()
        for j, chip in enumerate(chips):
            for k in range(n):
                copy(k, 4 + j, (*chip, 1 - c), me).wait_recv()
        for cp in first + passed:
            cp.wait_send()
        for cp in local:
            cp.wait()

    return pl.pallas_call(
        body, name=name, out_shape=_comm_out_shapes(["gather"] * n, shards), in_specs=[ANY] * n, out_specs=[ANY] * n,
        scratch_shapes=_comm_scratch(n),
    )(*shards)


HBM = pl.BlockSpec(memory_space=pltpu.HBM)
SEM = pl.BlockSpec(memory_space=pltpu.SEMAPHORE)
DATAFLOW = pltpu.SideEffectType.DATAFLOW_SIDE_EFFECTING


def _split_start(kind, part, land, name):
    def body(src_ref, land_ref, send_sems, recv_sems, src_thru, land_thru, token):
        x, y, c = _my_place()
        me = 4 * x + 2 * y + c
        for j in range(1, N_DEV):
            peer, peer_idx = _peer(j)
            pltpu.make_async_remote_copy(
                src_ref=src_ref.at[peer_idx] if kind == "exchange" else src_ref, dst_ref=land_ref.at[me],
                send_sem=send_sems.at[j - 1], recv_sem=recv_sems.at[j - 1], device_id=peer,
                device_id_type=MESH).start()
        token[...] = jnp.zeros_like(token)

    return pl.pallas_call(
        body, name=name,
        out_shape=(pltpu.SemaphoreType.DMA((N_DEV - 1,)), pltpu.SemaphoreType.DMA((N_DEV - 1,)),
                   pltpu.HBM(part.shape, part.dtype), pltpu.HBM(land.shape, land.dtype),
                   jax.ShapeDtypeStruct((SUBLANES, LANES), F32)),
        in_specs=(HBM, HBM), out_specs=(SEM, SEM, HBM, HBM, pl.BlockSpec(memory_space=pltpu.VMEM)),
        input_output_aliases={0: 2, 1: 3},
        compiler_params=pltpu.CompilerParams(has_side_effects=DATAFLOW),
    )(pltpu.with_memory_space_constraint(part, pltpu.HBM), pltpu.with_memory_space_constraint(land, pltpu.HBM))


def _split_wait(kind, send_sems, recv_sems, part_thru, land_thru, after, name):
    def body(src_ref, land_ref, send_sems, recv_sems, after_ref, src_dead, got_ref):
        x, y, c = _my_place()
        me = 4 * x + 2 * y + c
        own = lambda idx: src_ref.at[idx] if kind == "exchange" else src_ref
        for j in range(1, N_DEV):
            peer, peer_idx = _peer(j)
            sems = dict(send_sem=send_sems.at[j - 1], recv_sem=recv_sems.at[j - 1], device_id=peer,
                        device_id_type=MESH)
            pltpu.make_async_remote_copy(src_ref=own(peer_idx), dst_ref=land_ref.at[me], **sems).wait_send()
            pltpu.make_async_remote_copy(src_ref=own(me), dst_ref=land_ref.at[peer_idx], **sems).wait_recv()

    return pl.pallas_call(
        body, name=name,
        out_shape=(pltpu.HBM(part_thru.shape, part_thru.dtype), pltpu.HBM(land_thru.shape, land_thru.dtype)),
        in_specs=(HBM, HBM, SEM, SEM, ANY), out_specs=(HBM, HBM), input_output_aliases={0: 0, 1: 1},
        compiler_params=pltpu.CompilerParams(has_side_effects=DATAFLOW),
    )(part_thru, land_thru, send_sems, recv_sems, after)[1]


def _hosted_comm(kinds, n_in, n_out, n_comm, n_steps):
    def plan_of(refs, receives):
        ins = refs[n_in:n_in + n_comm]
        outs = refs[n_in + n_comm + n_out:n_in + 2 * n_comm + n_out]
        return _comm_plan(kinds, ins, outs, *refs[-3:], receives=receives)

    def start(refs):
        @pl.when(pl.program_id(0) == 0)
        def _():
            _comm_start(plan_of(refs, False))

    def wait(refs):
        @pl.when(pl.program_id(0) == n_steps - 1)
        def _():
            _comm_wait(plan_of(refs, True))

    return start, wait


def _z_lanes():
    return lax.broadcasted_iota(jnp.int32, (1, Z_PAD), 1) < RANK


def _inproj_fwd(x, g1, w_in_t, shards):
    T = x.shape[0]
    tm = min(T, TOKEN_TILE)
    n_comm = len(shards)
    kinds = ["gather"] * n_comm
    comm_start, comm_wait = _hosted_comm(kinds, 3, 2, n_comm, T // tm)

    def body(*refs):
        x_ref, g_ref, w_ref = refs[:3]
        proj_ref, xn_ref = refs[3 + n_comm:5 + n_comm]
        comm_start(refs)
        xv = x_ref[...]
        r = lax.rsqrt(_rowmean(xv * xv) + EPS)
        xn = (xv * r * g_ref[...]).astype(BF16)
        xn_ref[...] = xn
        proj_ref[:, 0:P_CI] = _nt(xn, w_ref[0:OFF_Z, :]).astype(BF16)
        proj_ref[:, P_CI:P_Z] = _nt(xn, w_ref[OFF_C:D_IN, :]).astype(BF16)
        proj_ref[:, P_Z:] = jnp.where(_z_lanes(), _nt(xn, w_ref[OFF_Z:OFF_Z + Z_PAD, :]), 0.0).astype(BF16)
        comm_wait(refs)

    res = pl.pallas_call(
        body, name="inproj_fwd", grid=(T // tm,),
        out_shape=[jax.ShapeDtypeStruct((T, D_INP), BF16), jax.ShapeDtypeStruct((T, D_MODEL), BF16)]
        + _comm_out_shapes(kinds, shards),
        in_specs=[pl.BlockSpec((tm, D_MODEL), lambda i: (i, 0)), _const((1, D_MODEL)), _const((D_IN, D_MODEL))]
        + [ANY] * n_comm,
        out_specs=[pl.BlockSpec((tm, D_INP), lambda i: (i, 0)), pl.BlockSpec((tm, D_MODEL), lambda i: (i, 0))]
        + [ANY] * n_comm,
        scratch_shapes=_comm_scratch(n_comm),
        compiler_params=_params(("arbitrary",)),
    )(x, g1, w_in_t, *shards)
    return res[0], res[1], res[2:]


def _head_masks():
    lane = lax.broadcasted_iota(jnp.int32, (1, KEY), 1)
    return [((lane >= h * DK) & (lane < (h + 1) * DK)).astype(F32) for h in range(HEADS)]


class _Mats(NamedTuple):
    tri: jax.Array
    tri_t: jax.Array
    same: jax.Array
    mid: jax.Array
    causal: jax.Array
    causal_t: jax.Array
    heads: jax.Array


def _chunk_matrices():
    r = lax.broadcasted_iota(jnp.int32, (SUB, SUB), 0)
    c = lax.broadcasted_iota(jnp.int32, (SUB, SUB), 1)
    shift = CHUNK.bit_length() - 1
    same = jnp.right_shift(r, shift) == jnp.right_shift(c, shift)
    causal = same & (r >= c)
    causal_t = same & (r <= c)
    mid = same & ((c & (CHUNK - 1)) < CHUNK // 2)
    hr = jnp.right_shift(lax.broadcasted_iota(jnp.int32, (VAL, KEY), 0), DV.bit_length() - 1)
    hc = jnp.right_shift(lax.broadcasted_iota(jnp.int32, (VAL, KEY), 1), DK.bit_length() - 1)
    return _Mats(tri=causal.astype(BF16), tri_t=causal_t.astype(BF16), same=same.astype(BF16), mid=mid.astype(BF16),
                 causal=causal, causal_t=causal_t, heads=hr == hc)


class _Decay(NamedTuple):
    al: jax.Array
    q: jax.Array
    k: jax.Array
    eb: jax.Array
    ebm: jax.Array
    emb: jax.Array
    elb: jax.Array
    ebl: jax.Array


def _decay_terms(z, q, k, wg, bg, mats):
    al = _nn(z, wg) + bg
    la = (jnp.minimum(al, 0.0) - jnp.log(1.0 + jnp.exp(-jnp.abs(al)))) * (1.0 / GATE_TAU)
    hi, lo = _split_bf16(la)
    cum = lambda m: _nn(m, hi) + _nn(m, lo)
    b, b_last, b_mid = cum(mats.tri), cum(mats.same), cum(mats.mid)
    return _Decay(al=al, q=q.astype(F32) * Q_SCALE, k=k.astype(F32), eb=jnp.exp(b), ebm=jnp.exp(b - b_mid),
                  emb=jnp.exp(b_mid - b), elb=jnp.exp(b_last - b), ebl=jnp.exp(b_last))


def _gla_fwd_tile(q_ref, k_ref, v_ref, g_ref, z_ref, wg_ref, bg_ref, gn_ref, mix_ref, o_ref, st_ref, state, tb):
    @pl.when(pl.program_id(0) == 0)
    def _():
        state[...] = jnp.zeros_like(state)

    mats = _chunk_matrices()
    masks = _head_masks()
    wgv, bgv = wg_ref[...], bg_ref[...]

    for sb in range(tb // SUB):
        rows = slice(sb * SUB, (sb + 1) * SUB)
        d = _decay_terms(z_ref[rows, :], q_ref[rows, :], k_ref[rows, :], wgv, bgv, mats)
        kem_b = (d.k * d.emb).astype(BF16)
        qem = d.q * d.ebm
        for h in range(HEADS):
            cols = slice(h * DV, (h + 1) * DV)
            a = jnp.where(mats.causal, _nt((qem * masks[h]).astype(BF16), kem_b), 0.0)
            o_ref[rows, cols] = _nn(a.astype(BF16), v_ref[rows, cols])
        qe0_b = (d.q * d.eb).astype(BF16)
        kdec_b = (d.k * d.elb).astype(BF16)
        for c in range(SUB // CHUNK):
            loc = slice(c * CHUNK, (c + 1) * CHUNK)
            glob = slice(sb * SUB + c * CHUNK, sb * SUB + (c + 1) * CHUNK)
            st = state[...]
            st_b = st.astype(BF16)
            st_ref[sb * (SUB // CHUNK) + c] = st_b
            o_ref[glob, :] += _nt(qe0_b[loc], st_b)
            u = _tn(v_ref[glob, :], kdec_b[loc])
            state[...] = st * d.ebl[c * CHUNK:c * CHUNK + 1] + jnp.where(mats.heads, u, 0.0)

    gnv = gn_ref[...]
    for h in range(HEADS):
        cols = slice(h * DV, (h + 1) * DV)
        oh = o_ref[:, cols]
        r = lax.rsqrt(_rowmean(oh * oh) + EPS)
        gh = g_ref[:, cols].astype(F32)
        mix_ref[:, cols] = (oh * r * gnv * (gh * jax.nn.sigmoid(gh))).astype(BF16)


def _group_mean(v, gmat):
    return _nn(v.astype(BF16), gmat)


def _shifted_copies(buf, sh, rows):
    for k in range(1, SUBLANES):
        sh[k - 1] = buf[pl.ds(k, rows), :]


def _tap(buf, sh, off, r0, n):
    k, base = off % SUBLANES, off - off % SUBLANES
    rows = pl.ds(r0 + base if isinstance(r0, int) else pl.multiple_of(r0 + base, SUBLANES), n)
    return buf[rows, :] if k == 0 else sh[k - 1, rows, :]


def _conv_fwd_tile(ci_ref, cg_ref, w_ref, b_ref, g_ref, be_ref, gm_ref, mix_ref, uc_ref, ubuf, ush, tm):
    sh_rows = tm + HALO - SUBLANES

    @pl.when(pl.program_id(0) == 0)
    def _():
        ubuf[0:HALO, :] = jnp.zeros((HALO, CONV), F32)

    ubuf[HALO:, :] = ci_ref[...].astype(F32) * jax.nn.sigmoid(cg_ref[...].astype(F32))
    _shifted_copies(ubuf, ush, sh_rows)
    for s in range(tm // FWD_STRIP):
        acc = jnp.zeros((FWD_STRIP, CONV), F32) + b_ref[...]
        for j in range(CONV_W):
            acc = acc + w_ref[j:j + 1, :] * _tap(ubuf, ush, HALO - (CONV_W - 1) + j, s * FWD_STRIP, FWD_STRIP)
        uc_ref[s * FWD_STRIP:(s + 1) * FWD_STRIP, :] = acc
    ubuf[0:HALO, :] = ubuf[tm:tm + HALO, :]
    gm = gm_ref[...]
    ucv = uc_ref[...]
    d = ucv - _group_mean(ucv, gm)
    var = _group_mean(d * d, gm)
    yn = d * lax.rsqrt(var + EPS) * g_ref[...] + be_ref[...]
    mix_ref[...] = (yn * jax.nn.sigmoid(yn)).astype(BF16)


def _mix_fwd(proj, wg, bg, gn, conv_w, conv_b, cn_g, cn_b, gmat, shards):
    T = proj.shape[0]
    tb = min(T, TOKEN_TILE)
    cpb = tb // CHUNK
    n_comm = len(shards)
    kinds = ["gather"] * n_comm
    comm_start, comm_wait = _hosted_comm(kinds, 15, 5, n_comm, T // tb)

    def body(*refs):
        gla_in, conv_in = refs[:8], refs[8:15]
        gla_out, conv_out = refs[15 + n_comm:18 + n_comm], refs[18 + n_comm:20 + n_comm]
        state, ubuf, ush = refs[20 + 2 * n_comm:23 + 2 * n_comm]
        comm_start(refs)
        _gla_fwd_tile(*gla_in, *gla_out, state, tb)
        _conv_fwd_tile(*conv_in, *conv_out, ubuf, ush, tb)
        comm_wait(refs)

    nc = T // CHUNK
    tok = lambda w, col: pl.BlockSpec((tb, w), lambda i: (i, col))
    res = pl.pallas_call(
        body, name="mix_fwd", grid=(T // tb,),
        out_shape=[jax.ShapeDtypeStruct((T, VAL), BF16), jax.ShapeDtypeStruct((T, VAL), F32),
                   jax.ShapeDtypeStruct((nc, VAL, KEY), BF16), jax.ShapeDtypeStruct((T, CONV), BF16),
                   jax.ShapeDtypeStruct((T, CONV), F32)] + _comm_out_shapes(kinds, shards),
        in_specs=[tok(KEY, P_Q // KEY), tok(KEY, P_K // KEY), tok(VAL, P_V // VAL), tok(VAL, P_G // VAL),
                  tok(Z_PAD, P_Z // Z_PAD), _const((Z_PAD, KEY)), _const((1, KEY)), _const((1, DV)),
                  tok(CONV, P_CI // CONV), tok(CONV, P_CG // CONV), _const((HALO, CONV)), _const((1, CONV)),
                  _const((1, CONV)), _const((1, CONV)), _const((CONV, CONV))] + [ANY] * n_comm,
        out_specs=[tok(VAL, 0), tok(VAL, 0), pl.BlockSpec((cpb, VAL, KEY), lambda i: (i, 0, 0)), tok(CONV, 0),
                   tok(CONV, 0)] + [ANY] * n_comm,
        scratch_shapes=[pltpu.VMEM((VAL, KEY), F32), pltpu.VMEM((tb + HALO, CONV), F32),
                        pltpu.VMEM((SUBLANES - 1, tb + HALO - SUBLANES, CONV), F32)] + _comm_scratch(n_comm),
        compiler_params=_params(("arbitrary",)),
    )(proj, proj, proj, proj, proj, wg, bg, gn, proj, proj, conv_w, conv_b, cn_g, cn_b, gmat, *shards)
    return res[0], res[1], res[2], res[3], res[4], res[5:]


def _rms_bwd(dy, xhat, r, g):
    dyg = dy * g
    return r * (dyg - xhat * _rowmean(dyg * xhat))


def _mlp_fwd_bwd(x, mix_a, mix_c, tgt, w_out, g2, w1t, w2, gf):
    T = x.shape[0]
    tm = min(T, MLP_TILE)
    inv_d = 1.0 / D_MODEL

    def body(x_ref, ma_ref, mc_ref, t_ref, wo_ref, g2_ref, w1_ref, w2_ref, gf_ref,
             dh1_ref, dh1b_ref, dmix_ref, hn_ref, ff_ref, da_ref, dh2_ref, loss_ref, dgf_ref, dg2_ref):
        @pl.when(pl.program_id(0) == 0)
        def _():
            loss_ref[...] = jnp.zeros_like(loss_ref)
            dgf_ref[...] = jnp.zeros_like(dgf_ref)
            dg2_ref[...] = jnp.zeros_like(dg2_ref)

        g2v, gfv = g2_ref[...], gf_ref[...]
        h1 = x_ref[...] + _nn(ma_ref[...], wo_ref[0:VAL, :]) + _nn(mc_ref[...], wo_ref[VAL:, :])
        r2 = lax.rsqrt(_rowmean(h1 * h1) + EPS)
        h1hat = h1 * r2
        hn = (h1hat * g2v).astype(BF16)
        hn_ref[...] = hn
        relu_a = jnp.maximum(_nt(hn, w1_ref[...]), 0.0)
        ff = (relu_a * relu_a).astype(BF16)
        ff_ref[...] = ff
        h2 = h1 + _nn(ff, w2_ref[...])
        rf = lax.rsqrt(_rowmean(h2 * h2) + EPS)
        h2hat = h2 * rf
        err = h2hat * gfv - t_ref[...]
        loss_ref[...] += (0.5 * inv_d) * _colsum(jnp.sum(err * err, axis=1, keepdims=True))
        dy = err * inv_d
        dgf_ref[...] += _colsum(dy * h2hat)
        dh2 = _rms_bwd(dy, h2hat, rf, gfv)
        dh2_b = dh2.astype(BF16)
        dh2_ref[...] = dh2_b
        da = (_nt(dh2_b, w2_ref[...]) * (2.0 * relu_a)).astype(BF16)
        da_ref[...] = da
        dhn = _nn(da, w1_ref[...])
        dg2_ref[...] += _colsum(dhn * h1hat)
        dh1 = dh2 + _rms_bwd(dhn, h1hat, r2, g2v)
        dh1_ref[...] = dh1
        dh1_b = dh1.astype(BF16)
        dh1b_ref[...] = dh1_b
        dmix_ref[...] = _nt(dh1_b, wo_ref[...]).astype(BF16)

    tok = lambda w: pl.BlockSpec((tm, w), lambda i: (i, 0))
    return pl.pallas_call(
        body, name="mlp_fwd_bwd", grid=(T // tm,),
        out_shape=[jax.ShapeDtypeStruct((T, D_MODEL), F32), jax.ShapeDtypeStruct((T, D_MODEL), BF16),
                   jax.ShapeDtypeStruct((T, D_MODEL), BF16),
                   jax.ShapeDtypeStruct((T, D_MODEL), BF16), jax.ShapeDtypeStruct((T, D_FF), BF16),
                   jax.ShapeDtypeStruct((T, D_FF), BF16), jax.ShapeDtypeStruct((T, D_MODEL), BF16),
                   jax.ShapeDtypeStruct((1, 1), F32), jax.ShapeDtypeStruct((1, D_MODEL), F32),
                   jax.ShapeDtypeStruct((1, D_MODEL), F32)],
        in_specs=[tok(D_MODEL), tok(VAL), tok(CONV), tok(D_MODEL), _const((D_MODEL, D_MODEL)), _const((1, D_MODEL)),
                  _const((D_FF, D_MODEL)), _const((D_FF, D_MODEL)), _const((1, D_MODEL))],
        out_specs=[tok(D_MODEL), tok(D_MODEL), tok(D_MODEL), tok(D_MODEL), tok(D_FF), tok(D_FF), tok(D_MODEL),
                   pl.BlockSpec((1, 1), lambda i: (0, 0)), pl.BlockSpec((1, D_MODEL), lambda i: (0, 0)),
                   pl.BlockSpec((1, D_MODEL), lambda i: (0, 0))],
        compiler_params=_params(("arbitrary",)),
    )(x, mix_a, mix_c, tgt, w_out, g2, w1t, w2, gf)


def _silu_grad(v, s):
    return s * (1.0 + v * (1.0 - s))


def _conv_bwd_tile(ci_ref, cg_ref, uc_ref, dm_ref, w_ref, g_ref, be_ref, gm_ref,
                   dpc_ref, dw_ref, db_ref, dg_ref, dbe_ref, dbuf, dsh, dwacc, tm, nt):
    step = pl.program_id(0)
    sh_rows = tm + HALO - SUBLANES

    @pl.when(step == 0)
    def _():
        dbuf[tm:, :] = jnp.zeros((HALO, CONV), F32)
        dwacc[...] = jnp.zeros_like(dwacc)
        db_ref[...] = jnp.zeros_like(db_ref)
        dg_ref[...] = jnp.zeros_like(dg_ref)
        dbe_ref[...] = jnp.zeros_like(dbe_ref)

    gm, gv = gm_ref[...], g_ref[...]
    ucv = uc_ref[...]
    d = ucv - _group_mean(ucv, gm)
    rs = lax.rsqrt(_group_mean(d * d, gm) + EPS)
    yhat = d * rs
    yn = yhat * gv + be_ref[...]
    dyn = dm_ref[...].astype(F32) * _silu_grad(yn, jax.nn.sigmoid(yn))
    dg_ref[...] += _colsum(dyn * yhat)
    dbe_ref[...] += _colsum(dyn)
    dyh = dyn * gv
    duc = rs * (dyh - _group_mean(dyh, gm) - yhat * _group_mean(dyh * yhat, gm))
    db_ref[...] += _colsum(duc)
    dbuf[0:tm, :] = duc
    _shifted_copies(dbuf, dsh, sh_rows)

    def strip(s, carry):
        r0 = pl.multiple_of(s * STRIP, STRIP)
        rows = pl.ds(r0, STRIP)
        cin = ci_ref[rows, :].astype(F32)
        sg = jax.nn.sigmoid(cg_ref[rows, :].astype(F32))
        u = cin * sg
        du = jnp.zeros((STRIP, CONV), F32)
        for j in range(CONV_W):
            dj = _tap(dbuf, dsh, CONV_W - 1 - j, r0, STRIP)
            du = du + w_ref[j:j + 1, :] * dj
            p = u * dj
            fold = p[0:SUBLANES]
            for q in range(1, STRIP // SUBLANES):
                fold = fold + p[q * SUBLANES:(q + 1) * SUBLANES, :]
            dwacc[j * SUBLANES:(j + 1) * SUBLANES, :] += fold
        dpc_ref[rows, 0:CONV] = (du * sg).astype(BF16)
        dpc_ref[rows, CONV:] = (du * cin * sg * (1.0 - sg)).astype(BF16)
        return carry

    lax.fori_loop(0, tm // STRIP, strip, 0)
    dbuf[tm:, :] = dbuf[0:HALO, :]

    @pl.when(step == nt - 1)
    def _():
        dw_ref[...] = jnp.zeros_like(dw_ref)
        for j in range(CONV_W):
            dw_ref[j:j + 1, :] = _colsum(dwacc[j * SUBLANES:(j + 1) * SUBLANES, :])


def _conv_bwd(proj, uc, dmix, conv_w, cn_g, cn_b, gmat):
    T = proj.shape[0]
    tm = min(T, TOKEN_TILE)
    nt = T // tm
    sh_rows = tm + HALO - SUBLANES

    def body(*refs):
        _conv_bwd_tile(*refs, tm, nt)

    rev = lambda i: nt - 1 - i
    tile = lambda col: pl.BlockSpec((tm, CONV), lambda i: (rev(i), col))
    acc = lambda rows: pl.BlockSpec((rows, CONV), lambda i: (0, 0))
    return pl.pallas_call(
        body, name="conv_bwd", grid=(nt,),
        out_shape=[jax.ShapeDtypeStruct((T, 2 * CONV), BF16),
                   jax.ShapeDtypeStruct((HALO, CONV), F32), jax.ShapeDtypeStruct((1, CONV), F32),
                   jax.ShapeDtypeStruct((1, CONV), F32), jax.ShapeDtypeStruct((1, CONV), F32)],
        in_specs=[tile(P_CI // CONV), tile(P_CG // CONV), tile(0), tile(1),
                  _const((HALO, CONV)), _const((1, CONV)), _const((1, CONV)), _const((CONV, CONV))],
        out_specs=[pl.BlockSpec((tm, 2 * CONV), lambda i: (rev(i), 0)), acc(HALO), acc(1), acc(1), acc(1)],
        scratch_shapes=[pltpu.VMEM((tm + HALO, CONV), F32), pltpu.VMEM((SUBLANES - 1, sh_rows, CONV), F32),
                        pltpu.VMEM((HALO * SUBLANES, CONV), F32)],
        compiler_params=_params(("arbitrary",)),
    )(proj, proj, uc, dmix, conv_w, cn_g, cn_b, gmat)


def _gla_bwd_tile(q_ref, k_ref, v_ref, g_ref, z_ref, o_ref, st_ref, dm_ref, wg_ref, bg_ref, gn_ref,
                  dpg_ref, dwg_ref, dbg_ref, dgn_ref, dstate, do_scr, dv_scr, tb):
    @pl.when(pl.program_id(0) == 0)
    def _():
        dstate[...] = jnp.zeros_like(dstate)
        dwg_ref[...] = jnp.zeros_like(dwg_ref)
        dbg_ref[...] = jnp.zeros_like(dbg_ref)
        dgn_ref[...] = jnp.zeros_like(dgn_ref)

    gnv = gn_ref[...]
    dgn = jnp.zeros((1, DV), F32)
    for h in range(HEADS):
        cols = slice(h * DV, (h + 1) * DV)
        oh = o_ref[:, cols]
        r = lax.rsqrt(_rowmean(oh * oh) + EPS)
        ohat = oh * r
        gh = g_ref[:, cols].astype(F32)
        sg = jax.nn.sigmoid(gh)
        dmx = dm_ref[:, cols].astype(F32)
        don = dmx * (gh * sg)
        dpg_ref[:, P_G + h * DV:P_G + (h + 1) * DV] = (dmx * (ohat * gnv) * _silu_grad(gh, sg)).astype(BF16)
        dgn = dgn + _colsum(don * ohat)
        do_scr[:, cols] = _rms_bwd(don, ohat, r, gnv)
    dgn_ref[...] += dgn

    mats = _chunk_matrices()
    masks = _head_masks()
    wgv, bgv = wg_ref[...], bg_ref[...]
    n_chunks = SUB // CHUNK

    for sb in reversed(range(tb // SUB)):
        rows = slice(sb * SUB, (sb + 1) * SUB)
        zs = z_ref[rows, :]
        d = _decay_terms(zs, q_ref[rows, :], k_ref[rows, :], wgv, bgv, mats)
        qem = d.q * d.ebm
        qem_b = qem.astype(BF16)
        kem_b = (d.k * d.emb).astype(BF16)
        dq = jnp.zeros((SUB, KEY), F32)
        dk = jnp.zeros((SUB, KEY), F32)
        for h in range(HEADS):
            hm = masks[h]
            cols = slice(h * DV, (h + 1) * DV)
            do_b = do_scr[rows, cols].astype(BF16)
            vh = v_ref[rows, cols]
            da = jnp.where(mats.causal, _nt(do_b, vh), 0.0).astype(BF16)
            da_t = jnp.where(mats.causal_t, _nt(vh, do_b), 0.0).astype(BF16)
            a_t = jnp.where(mats.causal_t, _nt(kem_b, (qem * hm).astype(BF16)), 0.0).astype(BF16)
            dq = dq + hm * _nn(da, kem_b)
            dk = dk + hm * _nn(da_t, qem_b)
            dv_scr[rows, cols] = _nn(a_t, do_b)
        dq = dq * d.ebm
        dk = dk * d.emb

        qe0_b = (d.q * d.eb).astype(BF16)
        kdec_b = (d.k * d.elb).astype(BF16)
        dq_st, dk_st, last = [None] * n_chunks, [None] * n_chunks, [None] * n_chunks
        for c in reversed(range(n_chunks)):
            loc = slice(c * CHUNK, (c + 1) * CHUNK)
            glob = slice(sb * SUB + c * CHUNK, sb * SUB + (c + 1) * CHUNK)
            st_b = st_ref[sb * n_chunks + c]
            ds = dstate[...]
            ds_b = ds.astype(BF16)
            do_c = do_scr[glob, :].astype(BF16)
            ebl_c = d.ebl[c * CHUNK:c * CHUNK + 1]
            dk_c = _nn(v_ref[glob, :], ds_b) * d.elb[loc]
            dq_st[c] = _nn(do_c, st_b) * d.eb[loc]
            dk_st[c] = dk_c
            last_c = _colsum(d.k[loc] * dk_c) + ebl_c * _colsum(st_b.astype(F32) * ds)
            last[c] = jnp.broadcast_to(last_c, (CHUNK, KEY))
            dpg_ref[glob, P_V:P_G] = (dv_scr[glob, :] + _nt(kdec_b[loc], ds_b)).astype(BF16)
            dstate[...] = ds * ebl_c + jnp.where(mats.heads, _tn(do_c, qe0_b[loc]), 0.0)
        dq = dq + jnp.concatenate(dq_st, axis=0)
        dk = dk + jnp.concatenate(dk_st, axis=0)
        dpg_ref[rows, P_Q:P_K] = (dq * Q_SCALE).astype(BF16)
        dpg_ref[rows, P_K:P_V] = dk.astype(BF16)
        hi, lo = _split_bf16(d.q * dq - d.k * dk)
        dla = _nn(mats.tri_t, hi) + _nn(mats.tri_t, lo) + jnp.concatenate(last, axis=0)
        dal = dla * (1.0 / GATE_TAU) * jax.nn.sigmoid(-d.al)
        dal_b = dal.astype(BF16)
        dpg_ref[rows, OFF_Z:] = _nt(dal_b, wgv).astype(BF16)
        dwg_ref[...] += _tn(zs, dal_b)
        dbg_ref[...] += _colsum(dal)


def _gla_bwd(proj, o, states, dmix, wg, bg, gn, parts):
    T = proj.shape[0]
    tb = min(T, TOKEN_TILE)
    cpb = tb // CHUNK
    nb = T // tb
    n_comm = len(parts)
    kinds = ["exchange"] * n_comm
    comm_start, comm_wait = _hosted_comm(kinds, 11, 4, n_comm, nb)

    def body(*refs):
        comm_start(refs)
        _gla_bwd_tile(*refs[:11], *refs[11 + n_comm:15 + n_comm], *refs[15 + 2 * n_comm:18 + 2 * n_comm], tb)
        comm_wait(refs)

    rev = lambda i: nb - 1 - i
    blk = lambda w, col: pl.BlockSpec((tb, w), lambda i: (rev(i), col))
    res = pl.pallas_call(
        body, name="gla_bwd", grid=(nb,),
        out_shape=[jax.ShapeDtypeStruct((T, D_GLA), BF16), jax.ShapeDtypeStruct((Z_PAD, KEY), F32),
                   jax.ShapeDtypeStruct((1, KEY), F32), jax.ShapeDtypeStruct((1, DV), F32)]
        + _comm_out_shapes(kinds, parts),
        in_specs=[blk(KEY, P_Q // KEY), blk(KEY, P_K // KEY), blk(VAL, P_V // VAL), blk(VAL, P_G // VAL),
                  blk(Z_PAD, P_Z // Z_PAD), blk(VAL, 0),
                  pl.BlockSpec((cpb, VAL, KEY), lambda i: (rev(i), 0, 0)), blk(VAL, 0),
                  _const((Z_PAD, KEY)), _const((1, KEY)), _const((1, DV))] + [ANY] * n_comm,
        out_specs=[blk(D_GLA, 0),
                   pl.BlockSpec((Z_PAD, KEY), lambda i: (0, 0)), pl.BlockSpec((1, KEY), lambda i: (0, 0)),
                   pl.BlockSpec((1, DV), lambda i: (0, 0))] + [ANY] * n_comm,
        scratch_shapes=[pltpu.VMEM((VAL, KEY), F32), pltpu.VMEM((tb, VAL), F32), pltpu.VMEM((tb, VAL), F32)]
        + _comm_scratch(n_comm),
        compiler_params=_params(("arbitrary",)),
    )(proj, proj, proj, proj, proj, o, states, dmix, wg, bg, gn, *parts)
    return res[:4], res[4:]


def _inproj_bwd(x, g1, w_in_t, dh1, dp_gla, dp_conv):
    T = x.shape[0]
    tm = min(T, TOKEN_TILE)

    def body(x_ref, g_ref, w_ref, dh1_ref, dpg_ref, dpc_ref, dx_ref, dg1_ref):
        @pl.when(pl.program_id(0) == 0)
        def _():
            dg1_ref[...] = jnp.zeros_like(dg1_ref)

        dxn = _nn(dpg_ref[...], w_ref[0:D_GLA, :]) + _nn(dpc_ref[...], w_ref[OFF_C:D_IN, :])
        xv = x_ref[...]
        r = lax.rsqrt(_rowmean(xv * xv) + EPS)
        xhat = xv * r
        dg1_ref[...] += _colsum(dxn * xhat)
        dx_ref[...] = dh1_ref[...] + _rms_bwd(dxn, xhat, r, g_ref[...])

    tok = lambda w: pl.BlockSpec((tm, w), lambda i: (i, 0))
    return pl.pallas_call(
        body, name="inproj_bwd", grid=(T // tm,),
        out_shape=[jax.ShapeDtypeStruct((T, D_MODEL), F32), jax.ShapeDtypeStruct((1, D_MODEL), F32)],
        in_specs=[tok(D_MODEL), _const((1, D_MODEL)), _const((D_IN, D_MODEL)), tok(D_MODEL), tok(D_GLA),
                  tok(2 * CONV)],
        out_specs=[tok(D_MODEL), pl.BlockSpec((1, D_MODEL), lambda i: (0, 0))],
        compiler_params=_params(("arbitrary",)),
    )(x, g1, w_in_t, dh1, dp_gla, dp_conv)


def _wgrad_in(xn, dp_gla, dp_conv):
    T = xn.shape[0]
    tt = min(T, WGRAD_TILE // 2)
    nt = T // tt

    def body(xn_ref, dpg_ref, dpc_ref, o_ref, acc):
        @pl.when(pl.program_id(0) == 0)
        def _():
            acc[...] = jnp.zeros_like(acc)

        xv = xn_ref[...]
        acc[0:OFF_C, :] += _tn(dpg_ref[...], xv)[0:OFF_C]
        acc[OFF_C:, :] += _tn(dpc_ref[...], xv)

        @pl.when(pl.program_id(0) == nt - 1)
        def _():
            o_ref[...] = acc[...].astype(BF16)

    tok = lambda w: pl.BlockSpec((tt, w), lambda t: (t, 0))
    return pl.pallas_call(
        body, name="wgrad_in", grid=(nt,), out_shape=jax.ShapeDtypeStruct((D_IN, D_MODEL), BF16),
        in_specs=[tok(D_MODEL), tok(D_GLA), tok(2 * CONV)],
        out_specs=pl.BlockSpec((D_IN, D_MODEL), lambda t: (0, 0), pipeline_mode=pl.Buffered(1)),
        scratch_shapes=[pltpu.VMEM((D_IN, D_MODEL), F32)],
        compiler_params=_params(("arbitrary",)),
    )(xn, dp_gla, dp_conv)


def _wgrad_out(mix_a, mix_c, dh1):
    T = dh1.shape[0]
    tt = min(T, WGRAD_TILE)
    nt = T // tt

    def body(a_ref, c_ref, b_ref, o_ref, acc):
        @pl.when(pl.program_id(0) == 0)
        def _():
            acc[...] = jnp.zeros_like(acc)

        b = b_ref[...]
        acc[0:VAL, :] += _tn(a_ref[...], b)
        acc[VAL:, :] += _tn(c_ref[...], b)

        @pl.when(pl.program_id(0) == nt - 1)
        def _():
            o_ref[...] = acc[...].astype(BF16)

    tok = lambda w: pl.BlockSpec((tt, w), lambda t: (t, 0))
    return pl.pallas_call(
        body, name="wgrad_out", grid=(nt,), out_shape=jax.ShapeDtypeStruct((D_MODEL, D_MODEL), BF16),
        in_specs=[tok(VAL), tok(CONV), tok(D_MODEL)],
        out_specs=pl.BlockSpec((D_MODEL, D_MODEL), lambda t: (0, 0)),
        scratch_shapes=[pltpu.VMEM((D_MODEL, D_MODEL), F32)],
        compiler_params=_params(("arbitrary",)),
    )(mix_a, mix_c, dh1)


def _wgrad(a, b, name, tk, tn, col_block=None):
    T, K = a.shape
    N = b.shape[1]
    tt = min(T, WGRAD_TILE)
    nt = T // tt

    def body(a_ref, b_ref, o_ref, acc):
        @pl.when(pl.program_id(2) == 0)
        def _():
            acc[...] = jnp.zeros_like(acc)

        acc[...] += _tn(a_ref[...], b_ref[...])

        @pl.when(pl.program_id(2) == nt - 1)
        def _():
            if col_block is None:
                o_ref[...] = acc[...].astype(BF16)
            else:
                for q in range(tn // col_block):
                    o_ref[q] = acc[:, q * col_block:(q + 1) * col_block].astype(BF16)

    if col_block is None:
        out_shape = jax.ShapeDtypeStruct((K, N), BF16)
        out_spec = pl.BlockSpec((tk, tn), lambda i, j, t: (i, j))
    else:
        assert tk == K
        out_shape = jax.ShapeDtypeStruct((N // col_block, K, col_block), BF16)
        out_spec = pl.BlockSpec((tn // col_block, tk, col_block), lambda i, j, t: (j, 0, 0))
    return pl.pallas_call(
        body, name=name, grid=(K // tk, N // tn, nt), out_shape=out_shape,
        in_specs=[pl.BlockSpec((tt, tk), lambda i, j, t: (t, i)), pl.BlockSpec((tt, tn), lambda i, j, t: (t, j))],
        out_specs=out_spec, scratch_shapes=[pltpu.VMEM((tk, tn), F32)],
        compiler_params=_params(("arbitrary", "arbitrary", "arbitrary")),
    )(a, b)


def _adam_math(w, g, m, v):
    m = ADAM_B1 * m + (1.0 - ADAM_B1) * g
    v = ADAM_B2 * v + (1.0 - ADAM_B2) * (g * g)
    m_hat = m / (1.0 - ADAM_B1 ** ADAM_STEP)
    v_hat = v / (1.0 - ADAM_B2 ** ADAM_STEP)
    delta = -ADAM_LR * (m_hat / (jnp.sqrt(v_hat) + ADAM_EPS) + ADAM_WD * w)
    return delta, m, v


def _sum8(ref):
    g = ref[0].astype(F32)
    for s in range(1, N_DEV):
        g = g + ref[s].astype(F32)
    return g


def _adam_big(parts, w, m, v, name):
    R, C = w.shape
    tr = ADAM_ROWS if R % ADAM_ROWS == 0 else R

    def body(p_ref, w_ref, m_ref, v_ref, g_ref, d_ref, nm_ref, nv_ref):
        g = _sum8(p_ref)
        g_ref[...] = g
        d_ref[...], nm_ref[...], nv_ref[...] = _adam_math(w_ref[...], g, m_ref[...], v_ref[...])

    row = pl.BlockSpec((tr, C), lambda i: (i, 0))
    return pl.pallas_call(
        body, name=name, grid=(R // tr,), out_shape=[jax.ShapeDtypeStruct((R, C), F32)] * 4,
        in_specs=[pl.BlockSpec((N_DEV, tr, C), lambda i: (0, i, 0)), row, row, row], out_specs=[row] * 4,
        compiler_params=_params(("arbitrary",)),
    )(parts, w, m, v)


def _sum_small(parts):
    def body(p_ref, o_ref):
        o_ref[...] = _sum8(p_ref)

    return pl.pallas_call(body, name="sum_small", out_shape=jax.ShapeDtypeStruct(parts.shape[1:], F32))(parts)


def _adam_small(gs, ws, ms, vs):
    n = len(gs)

    def body(*refs):
        g_refs, w_refs, m_refs, v_refs = refs[:n], refs[n:2 * n], refs[2 * n:3 * n], refs[3 * n:4 * n]
        outs = refs[4 * n:]
        for i in range(n):
            d, nm, nv = _adam_math(w_refs[i][...], g_refs[i][...], m_refs[i][...], v_refs[i][...])
            outs[i][...] = d
            outs[n + i][...] = nm
            outs[2 * n + i][...] = nv

    shapes = [jax.ShapeDtypeStruct(w.shape, F32) for w in ws]
    res = pl.pallas_call(body, name="adam_small", out_shape=shapes * 3)(*gs, *ws, *ms, *vs)
    return res[:n], res[n:2 * n], res[2 * n:]


def _group_matrix():
    gi = lax.broadcasted_iota(jnp.int32, (CONV, CONV), 0) // (CONV // GROUPS)
    gj = lax.broadcasted_iota(jnp.int32, (CONV, CONV), 1) // (CONV // GROUPS)
    return jnp.where(gi == gj, GROUPS / CONV, 0.0).astype(BF16)


_SMALL = [("loss", 8), ("dg1", 8), ("dbg", 2), ("dgn", 1), ("dconv_b", 4), ("dcn_g", 4), ("dcn_b", 4), ("dg2", 8),
          ("dgf", 8), ("dwg", 32), ("dconv_w", 124)]


def _pad8(rows):
    return -(-rows // 8) * 8


def kernel(x, norm1_g, w_in, w_gate_up, b_gate, gla_norm_g, conv_w, conv_b, conv_norm_g, conv_norm_b, w_out, norm2_g, w_mlp_in, w_mlp_out, final_norm_g, loss_target, m_norm1_g, m_w_in, m_w_gate_up, m_b_gate, m_gla_norm_g, m_conv_w, m_conv_b, m_conv_norm_g, m_conv_norm_b, m_w_out, m_norm2_g, m_w_mlp_in, m_w_mlp_out, m_final_norm_g, v_norm1_g, v_w_in, v_w_gate_up, v_b_gate, v_gla_norm_g, v_conv_w, v_conv_b, v_conv_norm_g, v_conv_norm_b, v_w_out, v_norm2_g, v_w_mlp_in, v_w_mlp_out, v_final_norm_g):
    x_idx = lax.axis_index("x")
    y_idx = lax.axis_index("y")
    c_idx = lax.axis_index("c")
    me = 4 * x_idx + 2 * y_idx + c_idx
    xs, tgt = x[0], loss_target[0]
    gf = final_norm_g.reshape(1, D_MODEL)
    gmat = _group_matrix()

    small_shard = jnp.zeros((_pad8(RANK + CONV_W), LANES), F32)
    small_shard = small_shard.at[0:RANK, 0:KEY // N_DEV].set(w_gate_up[0])
    small_shard = small_shard.at[RANK:RANK + CONV_W, 0:CONV // N_DEV].set(conv_w[0])
    g_in, g_small = _gather_two_level([w_in[0].T.astype(BF16), small_shard], "gather_w_in")
    w_in_t = g_in.reshape(D_IN, D_MODEL)
    wg_full = jnp.concatenate([g_small[d, 0:RANK, 0:KEY // N_DEV] for d in range(N_DEV)], axis=1)
    wg_pad = jnp.pad(wg_full, ((0, Z_PAD - RANK), (0, 0))).astype(BF16)
    conv_w_full = jnp.concatenate([g_small[d, RANK:RANK + CONV_W, 0:CONV // N_DEV] for d in range(N_DEV)], axis=1)
    conv_w_pad = jnp.pad(conv_w_full, ((0, HALO - CONV_W), (0, 0)))

    proj, xn, (g_w2,) = _inproj_fwd(xs, norm1_g, w_in_t, [w_mlp_out[0].astype(BF16)])
    mix_a, o, states, mix_c, uc, (g_out, g_w1) = _mix_fwd(
        proj, wg_pad, b_gate, gla_norm_g, conv_w_pad, conv_b, conv_norm_g, conv_norm_b, gmat,
        [w_out[0].astype(BF16), w_mlp_in[0].T.astype(BF16)])
    w_out_full = g_out.reshape(D_MODEL, D_MODEL)
    w1t_full = g_w1.reshape(D_FF, D_MODEL)
    w2_full = g_w2.reshape(D_FF, D_MODEL)
    dh1, dh1_b, dmix, hn, ff, da, dh2, loss, dgf, dg2 = _mlp_fwd_bwd(xs, mix_a, mix_c, tgt, w_out_full, norm2_g,
                                                                     w1t_full, w2_full, gf)

    dw1 = _wgrad(hn, da, "wgrad_mlp_in", WGRAD_BLOCK, WGRAD_BLOCK, col_block=D_FF // N_DEV)
    dw2 = _wgrad(ff, dh2, "wgrad_mlp_out", WGRAD_BLOCK, WGRAD_BLOCK)
    dw_out = _wgrad_out(mix_a, mix_c, dh1_b)
    dp_conv, dconv_w, dconv_b, dcn_g, dcn_b = _conv_bwd(proj, uc, dmix, conv_w_pad, conv_norm_g, conv_norm_b, gmat)
    (dp_gla, dwg, dbg, dgn), (p_w1, p_w2, p_out) = _gla_bwd(
        proj, o, states, dmix, wg_pad, b_gate, gla_norm_g,
        [dw1, dw2.reshape(N_DEV, D_FF // N_DEV, D_MODEL), dw_out.reshape(N_DEV, D_MODEL // N_DEV, D_MODEL)])
    dw_in = _wgrad_in(xn, dp_gla, dp_conv).reshape(N_DEV, SHARD_IN, D_MODEL)
    send_sems, recv_sems, dw_in_thru, land, token = _split_start("exchange", dw_in, jnp.copy(dw_in),
                                                                 "exchange_w_in_start")
    dx, dg1 = _inproj_bwd(xs, norm1_g + token[0:1, 0:1], w_in_t, dh1, dp_gla, dp_conv)
    p_in = _split_wait("exchange", send_sems, recv_sems, dw_in_thru, land, dg1, "exchange_w_in_wait")

    small = dict(loss=jnp.zeros((SUBLANES, LANES), F32) + loss, dg1=dg1, dbg=dbg, dgn=dgn, dconv_b=dconv_b, dcn_g=dcn_g,
                 dcn_b=dcn_b, dg2=dg2, dgf=dgf, dwg=dwg[0:RANK], dconv_w=dconv_w[0:CONV_W])
    pack = jnp.concatenate([jnp.pad(small[name].reshape(rows, LANES), ((0, _pad8(rows) - rows), (0, 0)))
                            for name, rows in _SMALL], axis=0)
    s_send, s_recv, pack_thru, pack_land, s_token = _split_start(
        "gather", pack, jnp.broadcast_to(pack, (N_DEV,) + pack.shape) + 0.0, "gather_small_start")

    gi, di, mi, vi = _adam_big(p_in, w_in[0].T, m_w_in[0].T, v_w_in[0].T, "adam_w_in")
    go, do, mo, vo = _adam_big(p_out, w_out[0] + s_token[0:1, 0:1], m_w_out[0], v_w_out[0], "adam_w_out")
    ga, da_, ma, va = _adam_big(p_w1, w_mlp_in[0], m_w_mlp_in[0], v_w_mlp_in[0], "adam_w_mlp_in")
    gb, db, mb, vb = _adam_big(p_w2, w_mlp_out[0], m_w_mlp_out[0], v_w_mlp_out[0], "adam_w_mlp_out")
    cut = lambda a: a.T[None]

    g_pack = _split_wait("gather", s_send, s_recv, pack_thru, pack_land, go[0:8, 0:128] + ga[0:8, 0:128]
                         + gb[0:8, 0:128], "gather_small_wait")
    summed = _sum_small(g_pack)
    small_g = {}
    at = 0
    for name, rows in _SMALL:
        small_g[name] = summed[at:at + rows]
        at += _pad8(rows)
    loss_out = small_g["loss"][0, 0]
    wg_cols = KEY // N_DEV
    cw_cols = CONV // N_DEV
    g_small_list = [
        small_g["dg1"].reshape(1, D_MODEL),
        lax.dynamic_slice_in_dim(small_g["dwg"].reshape(RANK, KEY), me * wg_cols, wg_cols, axis=1)[None],
        small_g["dbg"].reshape(1, KEY), small_g["dgn"].reshape(1, DV),
        lax.dynamic_slice_in_dim(small_g["dconv_w"].reshape(CONV_W, CONV), me * cw_cols, cw_cols, axis=1)[None],
        small_g["dconv_b"].reshape(1, CONV), small_g["dcn_g"].reshape(1, CONV), small_g["dcn_b"].reshape(1, CONV),
        small_g["dg2"].reshape(1, D_MODEL), small_g["dgf"].reshape(1, D_MODEL),
    ]
    row = lambda a: a.reshape(1, D_MODEL)
    w_small = [norm1_g, w_gate_up, b_gate, gla_norm_g, conv_w, conv_b, conv_norm_g, conv_norm_b, norm2_g,
               row(final_norm_g)]
    m_small = [m_norm1_g, m_w_gate_up, m_b_gate, m_gla_norm_g, m_conv_w, m_conv_b, m_conv_norm_g, m_conv_norm_b,
               m_norm2_g, row(m_final_norm_g)]
    v_small = [v_norm1_g, v_w_gate_up, v_b_gate, v_gla_norm_g, v_conv_w, v_conv_b, v_conv_norm_g, v_conv_norm_b,
               v_norm2_g, row(v_final_norm_g)]
    d_small, nm_small, nv_small = _adam_small(g_small_list, w_small, m_small, v_small)
    flat = lambda lst: list(lst[:-1]) + [lst[-1].reshape(D_MODEL)]
    g_small_list, d_small, nm_small, nv_small = flat(g_small_list), flat(d_small), flat(nm_small), flat(nv_small)

    def order(s, w_in_v, w_out_v, w1_v, w2_v):
        return [s[0], w_in_v, s[1], s[2], s[3], s[4], s[5], s[6], s[7], w_out_v, s[8], w1_v, w2_v, s[9]]

    grads = order(g_small_list, cut(gi), go[None], ga[None], gb[None])
    deltas = order(d_small, cut(di), do[None], da_[None], db[None])
    new_m = order(nm_small, cut(mi), mo[None], ma[None], mb[None])
    new_v = order(nv_small, cut(vi), vo[None], va[None], vb[None])
    return (loss_out, dx[None], *grads, *deltas, *new_m, *new_v)
```

```python
from typing import NamedTuple

import jax
import jax.numpy as jnp
from jax import lax
from jax.experimental import pallas as pl
from jax.experimental.pallas import tpu as pltpu

F32 = jnp.float32
BF16 = jnp.bfloat16

N_DEV = 8
D_MODEL = 1024
HEADS = 4
DK = 64
DV = 128
KEY = HEADS * DK
VAL = HEADS * DV
RANK = 16
CONV = 512
GROUPS = 8
CONV_W = 31
HALO = 32
SUBLANES = 8
LANES = 128
STRIP = 32
FWD_STRIP = 16
TOKEN_TILE = 512
MLP_TILE = 256
WGRAD_TILE = 4096
WGRAD_BLOCK = 1024
ADAM_ROWS = 128
D_FF = 4096
D_IN = 2576
SHARD_IN = D_IN // N_DEV
CHUNK = 64
SUB = 256
EPS = 1e-6
GATE_TAU = 16.0
Q_SCALE = DK ** -0.5

P_Q, P_K, P_V, P_G, P_CI, P_CG, P_Z = 0, 256, 512, 1024, 1536, 2048, 2560
D_INP = 2688
Z_PAD = D_INP - P_Z
OFF_Z = 1536
OFF_C = OFF_Z + RANK
D_GLA = OFF_Z + Z_PAD

ADAM_LR = 0.001
ADAM_B1 = 0.9
ADAM_B2 = 0.999
ADAM_EPS = 1e-08
ADAM_WD = 0.01
ADAM_STEP = 10

V7X_VMEM_BYTES = 64 * 1024 * 1024
VMEM_LIMIT = V7X_VMEM_BYTES * 7 // 8

MESH = pl.DeviceIdType.MESH
ANY = pl.BlockSpec(memory_space=pl.ANY)


def _nn(a, b):
    return jnp.dot(a, b, preferred_element_type=F32)


def _nt(a, b):
    return lax.dot_general(a, b, (((1,), (1,)), ((), ())), preferred_element_type=F32)


def _tn(a, b):
    return lax.dot_general(a, b, (((0,), (0,)), ((), ())), preferred_element_type=F32)


def _params(sem=None):
    return pltpu.CompilerParams(dimension_semantics=sem, vmem_limit_bytes=VMEM_LIMIT)


def _const(shape):
    return pl.BlockSpec(shape, lambda *_: (0,) * len(shape), pipeline_mode=pl.Buffered(1))


def _colsum(v):
    return jnp.sum(v, axis=0, keepdims=True)


def _rowmean(v):
    return jnp.mean(v, axis=-1, keepdims=True)


def _split_bf16(v):
    hi = v.astype(BF16)
    return hi, (v - hi.astype(F32)).astype(BF16)


def _my_place():
    return lax.axis_index("x"), lax.axis_index("y"), lax.axis_index("c")


def _peer(j):
    x, y, c = _my_place()
    jx, jy, jc = (j >> 2) & 1, (j >> 1) & 1, j & 1
    px = 1 - x if jx else x
    py = 1 - y if jy else y
    pc = 1 - c if jc else c
    return (px, py, pc), 4 * px + 2 * py + pc


def _comm_plan(kinds, ins, outs, send_sems, recv_sems, local_sems, receives=True):
    x, y, c = _my_place()
    me = 4 * x + 2 * y + c
    own = lambda k, idx: ins[k] if kinds[k] == "gather" else ins[k].at[idx]
    local = [pltpu.make_async_copy(own(k, me), outs[k].at[me], local_sems.at[k]) for k in range(len(kinds))]
    sends, recvs = [], []
    for j in range(1, N_DEV):
        peer, peer_idx = _peer(j)
        for k in range(len(kinds)):
            sems = dict(send_sem=send_sems.at[k, j - 1], recv_sem=recv_sems.at[k, j - 1], device_id=peer,
                        device_id_type=MESH)
            sends.append(pltpu.make_async_remote_copy(src_ref=own(k, peer_idx), dst_ref=outs[k].at[me], **sems))
            if receives:
                recvs.append(pltpu.make_async_remote_copy(src_ref=own(k, me), dst_ref=outs[k].at[peer_idx], **sems))
    return local, sends, recvs


def _comm_start(plan):
    local, sends, _ = plan
    for cp in local + sends:
        cp.start()


def _comm_wait(plan):
    local, sends, recvs = plan
    for cp in recvs:
        cp.wait_recv()
    for cp in sends:
        cp.wait_send()
    for cp in local:
        cp.wait()


def _comm_scratch(n):
    return [pltpu.SemaphoreType.DMA((n, N_DEV - 1)), pltpu.SemaphoreType.DMA((n, N_DEV - 1)),
            pltpu.SemaphoreType.DMA((n,))]


def _comm_out_shapes(kinds, arrays):
    return [jax.ShapeDtypeStruct(((N_DEV,) + a.shape) if kind == "gather" else a.shape, a.dtype)
            for kind, a in zip(kinds, arrays)]


def _norm_gather(x, g1, shards):
    T = x.shape[0]
    tm = min(T, TOKEN_TILE)
    nt = T // tm
    n = len(shards)
    forward_step = (nt * 5) // 8

    def body(*refs):
        x_ref, g_ref = refs[:2]
        ins = refs[2:2 + n]
        xn_ref = refs[2 + n]
        outs = refs[3 + n:3 + 2 * n]
        send_sems, recv_sems, local_sems = refs[3 + 2 * n:]
        px, py, pc = _my_place()
        index = lambda ax, ay, ac: 4 * ax + 2 * ay + ac
        me, sibling = (px, py, pc), (px, py, 1 - pc)
        chips = [(1 - px, py), (px, 1 - py), (1 - px, 1 - py)]
        step = pl.program_id(0)

        def copy(k, slot, block, to, src=None):
            rows = outs[k].at[index(*block)]
            return pltpu.make_async_remote_copy(
                src_ref=rows if src is None else src, dst_ref=rows, send_sem=send_sems.at[k, slot],
                recv_sem=recv_sems.at[k, slot], device_id=to, device_id_type=MESH)

        def local(k):
            return pltpu.make_async_copy(ins[k], outs[k].at[index(*me)], local_sems.at[k])

        def first(k):
            return [copy(k, 0, me, sibling, src=ins[k])] + [copy(k, 1 + j, me, (*chip, pc), src=ins[k])
                                                           for j, chip in enumerate(chips)]

        @pl.when(step == 0)
        def _():
            for k in range(n):
                for cp in [local(k)] + first(k):
                    cp.start()

        xv = x_ref[...]
        r = lax.rsqrt(_rowmean(xv * xv) + EPS)
        xn_ref[...] = (xv * r * g_ref[...]).astype(BF16)

        @pl.when(step == forward_step)
        def _():
            for j, chip in enumerate(chips):
                for k in range(n):
                    copy(k, 1 + j, (*chip, pc), me).wait_recv()
                    copy(k, 4 + j, (*chip, pc), sibling).start()

        @pl.when(step == nt - 1)
        def _():
            for k in range(n):
                copy(k, 0, sibling, me).wait_recv()
            for j, chip in enumerate(chips):
                for k in range(n):
                    copy(k, 4 + j, (*chip, 1 - pc), me).wait_recv()
            for k in range(n):
                for cp in first(k) + [copy(k, 4 + j, (*chip, pc), sibling) for j, chip in enumerate(chips)]:
                    cp.wait_send()
                local(k).wait()

    res = pl.pallas_call(
        body, name="norm_gather", grid=(nt,),
        out_shape=[jax.ShapeDtypeStruct((T, D_MODEL), BF16)] + _comm_out_shapes(["gather"] * n, shards),
        in_specs=[pl.BlockSpec((tm, D_MODEL), lambda i: (i, 0)), _const((1, D_MODEL))] + [ANY] * n,
        out_specs=[pl.BlockSpec((tm, D_MODEL), lambda i: (i, 0))] + [ANY] * n,
        scratch_shapes=_comm_scratch(n),
        compiler_params=_params(("arbitrary",)),
    )(x, g1, *shards)
    return res[0], res[1:]


HBM = pl.BlockSpec(memory_space=pltpu.HBM)
SEM = pl.BlockSpec(memory_space=pltpu.SEMAPHORE)
DATAFLOW = pltpu.SideEffectType.DATAFLOW_SIDE_EFFECTING


def _split_start(kind, part, land, name):
    def body(src_ref, land_ref, send_sems, recv_sems, src_thru, land_thru, token):
        x, y, c = _my_place()
        me = 4 * x + 2 * y + c
        for j in range(1, N_DEV):
            peer, peer_idx = _peer(j)
            pltpu.make_async_remote_copy(
                src_ref=src_ref.at[peer_idx] if kind == "exchange" else src_ref, dst_ref=land_ref.at[me],
                send_sem=send_sems.at[j - 1], recv_sem=recv_sems.at[j - 1], device_id=peer,
                device_id_type=MESH).start()
        token[...] = jnp.zeros_like(token)

    return pl.pallas_call(
        body, name=name,
        out_shape=(pltpu.SemaphoreType.DMA((N_DEV - 1,)), pltpu.SemaphoreType.DMA((N_DEV - 1,)),
                   pltpu.HBM(part.shape, part.dtype), pltpu.HBM(land.shape, land.dtype),
                   jax.ShapeDtypeStruct((SUBLANES, LANES), F32)),
        in_specs=(HBM, HBM), out_specs=(SEM, SEM, HBM, HBM, pl.BlockSpec(memory_space=pltpu.VMEM)),
        input_output_aliases={0: 2, 1: 3},
        compiler_params=pltpu.CompilerParams(has_side_effects=DATAFLOW),
    )(pltpu.with_memory_space_constraint(part, pltpu.HBM), pltpu.with_memory_space_constraint(land, pltpu.HBM))


def _split_wait(kind, send_sems, recv_sems, part_thru, land_thru, after, name):
    def body(src_ref, land_ref, send_sems, recv_sems, after_ref, src_dead, got_ref):
        x, y, c = _my_place()
        me = 4 * x + 2 * y + c
        own = lambda idx: src_ref.at[idx] if kind == "exchange" else src_ref
        for j in range(1, N_DEV):
            peer, peer_idx = _peer(j)
            sems = dict(send_sem=send_sems.at[j - 1], recv_sem=recv_sems.at[j - 1], device_id=peer,
                        device_id_type=MESH)
            pltpu.make_async_remote_copy(src_ref=own(peer_idx), dst_ref=land_ref.at[me], **sems).wait_send()
            pltpu.make_async_remote_copy(src_ref=own(me), dst_ref=land_ref.at[peer_idx], **sems).wait_recv()

    return pl.pallas_call(
        body, name=name,
        out_shape=(pltpu.HBM(part_thru.shape, part_thru.dtype), pltpu.HBM(land_thru.shape, land_thru.dtype)),
        in_specs=(HBM, HBM, SEM, SEM, ANY), out_specs=(HBM, HBM), input_output_aliases={0: 0, 1: 1},
        compiler_params=pltpu.CompilerParams(has_side_effects=DATAFLOW),
    )(part_thru, land_thru, send_sems, recv_sems, after)[1]


def _hosted_comm(kinds, n_in, n_out, n_comm, n_steps):
    def plan_of(refs, receives):
        ins = refs[n_in:n_in + n_comm]
        outs = refs[n_in + n_comm + n_out:n_in + 2 * n_comm + n_out]
        return _comm_plan(kinds, ins, outs, *refs[-3:], receives=receives)

    def start(refs):
        @pl.when(pl.program_id(0) == 0)
        def _():
            _comm_start(plan_of(refs, False))

    def wait(refs):
        @pl.when(pl.program_id(0) == n_steps - 1)
        def _():
            _comm_wait(plan_of(refs, True))

    return start, wait


def _z_lanes():
    return lax.broadcasted_iota(jnp.int32, (1, Z_PAD), 1) < RANK


def _inproj_fwd(xn, w_in_t, shards):
    T = xn.shape[0]
    tm = min(T, TOKEN_TILE)
    n_comm = len(shards)
    kinds = ["gather"] * n_comm
    comm_start, comm_wait = _hosted_comm(kinds, 2, 1, n_comm, T // tm)

    def body(*refs):
        xn_ref, w_ref = refs[:2]
        proj_ref = refs[2 + n_comm]
        comm_start(refs)
        xn = xn_ref[...]
        proj_ref[:, 0:P_CI] = _nt(xn, w_ref[0:OFF_Z, :]).astype(BF16)
        proj_ref[:, P_CI:P_Z] = _nt(xn, w_ref[OFF_C:D_IN, :]).astype(BF16)
        proj_ref[:, P_Z:] = jnp.where(_z_lanes(), _nt(xn, w_ref[OFF_Z:OFF_Z + Z_PAD, :]), 0.0).astype(BF16)
        comm_wait(refs)

    res = pl.pallas_call(
        body, name="inproj_fwd", grid=(T // tm,),
        out_shape=[jax.ShapeDtypeStruct((T, D_INP), BF16)] + _comm_out_shapes(kinds, shards),
        in_specs=[pl.BlockSpec((tm, D_MODEL), lambda i: (i, 0)), _const((D_IN, D_MODEL))] + [ANY] * n_comm,
        out_specs=[pl.BlockSpec((tm, D_INP), lambda i: (i, 0))] + [ANY] * n_comm,
        scratch_shapes=_comm_scratch(n_comm),
        compiler_params=_params(("arbitrary",)),
    )(xn, w_in_t, *shards)
    return res[0], res[1:]


def _head_masks():
    lane = lax.broadcasted_iota(jnp.int32, (1, KEY), 1)
    return [((lane >= h * DK) & (lane < (h + 1) * DK)).astype(F32) for h in range(HEADS)]


class _Mats(NamedTuple):
    tri: jax.Array
    tri_t: jax.Array
    same: jax.Array
    mid: jax.Array
    causal: jax.Array
    causal_t: jax.Array
    heads: jax.Array


def _chunk_matrices():
    r = lax.broadcasted_iota(jnp.int32, (SUB, SUB), 0)
    c = lax.broadcasted_iota(jnp.int32, (SUB, SUB), 1)
    shift = CHUNK.bit_length() - 1
    same = jnp.right_shift(r, shift) == jnp.right_shift(c, shift)
    causal = same & (r >= c)
    causal_t = same & (r <= c)
    mid = same & ((c & (CHUNK - 1)) < CHUNK // 2)
    hr = jnp.right_shift(lax.broadcasted_iota(jnp.int32, (VAL, KEY), 0), DV.bit_length() - 1)
    hc = jnp.right_shift(lax.broadcasted_iota(jnp.int32, (VAL, KEY), 1), DK.bit_length() - 1)
    return _Mats(tri=causal.astype(BF16), tri_t=causal_t.astype(BF16), same=same.astype(BF16), mid=mid.astype(BF16),
                 causal=causal, causal_t=causal_t, heads=hr == hc)


class _Decay(NamedTuple):
    al: jax.Array
    q: jax.Array
    k: jax.Array
    eb: jax.Array
    ebm: jax.Array
    emb: jax.Array
    elb: jax.Array
    ebl: jax.Array


def _decay_terms(z, q, k, wg, bg, mats):
    al = _nn(z, wg) + bg
    la = (jnp.minimum(al, 0.0) - jnp.log(1.0 + jnp.exp(-jnp.abs(al)))) * (1.0 / GATE_TAU)
    hi, lo = _split_bf16(la)
    cum = lambda m: _nn(m, hi) + _nn(m, lo)
    b, b_last, b_mid = cum(mats.tri), cum(mats.same), cum(mats.mid)
    return _Decay(al=al, q=q.astype(F32) * Q_SCALE, k=k.astype(F32), eb=jnp.exp(b), ebm=jnp.exp(b - b_mid),
                  emb=jnp.exp(b_mid - b), elb=jnp.exp(b_last - b), ebl=jnp.exp(b_last))


def _gla_fwd_tile(q_ref, k_ref, v_ref, g_ref, z_ref, wg_ref, bg_ref, gn_ref, mix_ref, o_ref, st_ref, state, tb):
    @pl.when(pl.program_id(0) == 0)
    def _():
        state[...] = jnp.zeros_like(state)

    mats = _chunk_matrices()
    masks = _head_masks()
    wgv, bgv = wg_ref[...], bg_ref[...]

    for sb in range(tb // SUB):
        rows = slice(sb * SUB, (sb + 1) * SUB)
        d = _decay_terms(z_ref[rows, :], q_ref[rows, :], k_ref[rows, :], wgv, bgv, mats)
        kem_b = (d.k * d.emb).astype(BF16)
        qem = d.q * d.ebm
        for h in range(HEADS):
            cols = slice(h * DV, (h + 1) * DV)
            a = jnp.where(mats.causal, _nt((qem * masks[h]).astype(BF16), kem_b), 0.0)
            o_ref[rows, cols] = _nn(a.astype(BF16), v_ref[rows, cols])
        qe0_b = (d.q * d.eb).astype(BF16)
        kdec_b = (d.k * d.elb).astype(BF16)
        for c in range(SUB // CHUNK):
            loc = slice(c * CHUNK, (c + 1) * CHUNK)
            glob = slice(sb * SUB + c * CHUNK, sb * SUB + (c + 1) * CHUNK)
            st = state[...]
            st_b = st.astype(BF16)
            st_ref[sb * (SUB // CHUNK) + c] = st_b
            o_ref[glob, :] += _nt(qe0_b[loc], st_b)
            u = _tn(v_ref[glob, :], kdec_b[loc])
            state[...] = st * d.ebl[c * CHUNK:c * CHUNK + 1] + jnp.where(mats.heads, u, 0.0)

    gnv = gn_ref[...]
    for h in range(HEADS):
        cols = slice(h * DV, (h + 1) * DV)
        oh = o_ref[:, cols]
        r = lax.rsqrt(_rowmean(oh * oh) + EPS)
        gh = g_ref[:, cols].astype(F32)
        mix_ref[:, cols] = (oh * r * gnv * (gh * jax.nn.sigmoid(gh))).astype(BF16)


def _group_mean(v, gmat):
    return _nn(v.astype(BF16), gmat)


def _shifted_copies(buf, sh, rows):
    for k in range(1, SUBLANES):
        sh[k - 1] = buf[pl.ds(k, rows), :]


def _tap(buf, sh, off, r0, n):
    k, base = off % SUBLANES, off - off % SUBLANES
    rows = pl.ds(r0 + base if isinstance(r0, int) else pl.multiple_of(r0 + base, SUBLANES), n)
    return buf[rows, :] if k == 0 else sh[k - 1, rows, :]


def _conv_fwd_tile(ci_ref, cg_ref, w_ref, b_ref, g_ref, be_ref, gm_ref, mix_ref, uc_ref, ubuf, ush, tm):
    sh_rows = tm + HALO - SUBLANES

    @pl.when(pl.program_id(0) == 0)
    def _():
        ubuf[0:HALO, :] = jnp.zeros((HALO, CONV), F32)

    ubuf[HALO:, :] = ci_ref[...].astype(F32) * jax.nn.sigmoid(cg_ref[...].astype(F32))
    _shifted_copies(ubuf, ush, sh_rows)
    for s in range(tm // FWD_STRIP):
        acc = jnp.zeros((FWD_STRIP, CONV), F32) + b_ref[...]
        for j in range(CONV_W):
            acc = acc + w_ref[j:j + 1, :] * _tap(ubuf, ush, HALO - (CONV_W - 1) + j, s * FWD_STRIP, FWD_STRIP)
        uc_ref[s * FWD_STRIP:(s + 1) * FWD_STRIP, :] = acc
    ubuf[0:HALO, :] = ubuf[tm:tm + HALO, :]
    gm = gm_ref[...]
    ucv = uc_ref[...]
    d = ucv - _group_mean(ucv, gm)
    var = _group_mean(d * d, gm)
    yn = d * lax.rsqrt(var + EPS) * g_ref[...] + be_ref[...]
    mix_ref[...] = (yn * jax.nn.sigmoid(yn)).astype(BF16)


def _mix_fwd(proj, wg, bg, gn, conv_w, conv_b, cn_g, cn_b, gmat, shards):
    T = proj.shape[0]
    tb = min(T, TOKEN_TILE)
    cpb = tb // CHUNK
    n_comm = len(shards)
    kinds = ["gather"] * n_comm
    comm_start, comm_wait = _hosted_comm(kinds, 15, 5, n_comm, T // tb)

    def body(*refs):
        gla_in, conv_in = refs[:8], refs[8:15]
        gla_out, conv_out = refs[15 + n_comm:18 + n_comm], refs[18 + n_comm:20 + n_comm]
        state, ubuf, ush = refs[20 + 2 * n_comm:23 + 2 * n_comm]
        comm_start(refs)
        _gla_fwd_tile(*gla_in, *gla_out, state, tb)
        _conv_fwd_tile(*conv_in, *conv_out, ubuf, ush, tb)
        comm_wait(refs)

    nc = T // CHUNK
    tok = lambda w, col: pl.BlockSpec((tb, w), lambda i: (i, col))
    res = pl.pallas_call(
        body, name="mix_fwd", grid=(T // tb,),
        out_shape=[jax.ShapeDtypeStruct((T, VAL), BF16), jax.ShapeDtypeStruct((T, VAL), F32),
                   jax.ShapeDtypeStruct((nc, VAL, KEY), BF16), jax.ShapeDtypeStruct((T, CONV), BF16),
                   jax.ShapeDtypeStruct((T, CONV), F32)] + _comm_out_shapes(kinds, shards),
        in_specs=[tok(KEY, P_Q // KEY), tok(KEY, P_K // KEY), tok(VAL, P_V // VAL), tok(VAL, P_G // VAL),
                  tok(Z_PAD, P_Z // Z_PAD), _const((Z_PAD, KEY)), _const((1, KEY)), _const((1, DV)),
                  tok(CONV, P_CI // CONV), tok(CONV, P_CG // CONV), _const((HALO, CONV)), _const((1, CONV)),
                  _const((1, CONV)), _const((1, CONV)), _const((CONV, CONV))] + [ANY] * n_comm,
        out_specs=[tok(VAL, 0), tok(VAL, 0), pl.BlockSpec((cpb, VAL, KEY), lambda i: (i, 0, 0)), tok(CONV, 0),
                   tok(CONV, 0)] + [ANY] * n_comm,
        scratch_shapes=[pltpu.VMEM((VAL, KEY), F32), pltpu.VMEM((tb + HALO, CONV), F32),
                        pltpu.VMEM((SUBLANES - 1, tb + HALO - SUBLANES, CONV), F32)] + _comm_scratch(n_comm),
        compiler_params=_params(("arbitrary",)),
    )(proj, proj, proj, proj, proj, wg, bg, gn, proj, proj, conv_w, conv_b, cn_g, cn_b, gmat, *shards)
    return res[0], res[1], res[2], res[3], res[4], res[5:]


def _rms_bwd(dy, xhat, r, g):
    dyg = dy * g
    return r * (dyg - xhat * _rowmean(dyg * xhat))


def _mlp_fwd_bwd(x, mix_a, mix_c, tgt, w_out, g2, w1t, w2, gf):
    T = x.shape[0]
    tm = min(T, MLP_TILE)
    inv_d = 1.0 / D_MODEL

    def body(x_ref, ma_ref, mc_ref, t_ref, wo_ref, g2_ref, w1_ref, w2_ref, gf_ref,
             dh1_ref, dmix_ref, hn_ref, ff_ref, da_ref, dh2_ref, loss_ref, dgf_ref, dg2_ref):
        @pl.when(pl.program_id(0) == 0)
        def _():
            loss_ref[...] = jnp.zeros_like(loss_ref)
            dgf_ref[...] = jnp.zeros_like(dgf_ref)
            dg2_ref[...] = jnp.zeros_like(dg2_ref)

        g2v, gfv = g2_ref[...], gf_ref[...]
        h1 = x_ref[...] + _nn(ma_ref[...], wo_ref[0:VAL, :]) + _nn(mc_ref[...], wo_ref[VAL:, :])
        r2 = lax.rsqrt(_rowmean(h1 * h1) + EPS)
        h1hat = h1 * r2
        hn = (h1hat * g2v).astype(BF16)
        hn_ref[...] = hn
        relu_a = jnp.maximum(_nt(hn, w1_ref[...]), 0.0)
        ff = (relu_a * relu_a).astype(BF16)
        ff_ref[...] = ff
        h2 = h1 + _nn(ff, w2_ref[...])
        rf = lax.rsqrt(_rowmean(h2 * h2) + EPS)
        h2hat = h2 * rf
        err = h2hat * gfv - t_ref[...]
        loss_ref[...] += (0.5 * inv_d) * _colsum(jnp.sum(err * err, axis=1, keepdims=True))
        dy = err * inv_d
        dgf_ref[...] += _colsum(dy * h2hat)
        dh2 = _rms_bwd(dy, h2hat, rf, gfv)
        dh2_b = dh2.astype(BF16)
        dh2_ref[...] = dh2_b
        da = (_nt(dh2_b, w2_ref[...]) * (2.0 * relu_a)).astype(BF16)
        da_ref[...] = da
        dhn = _nn(da, w1_ref[...])
        dg2_ref[...] += _colsum(dhn * h1hat)
        dh1 = dh2 + _rms_bwd(dhn, h1hat, r2, g2v)
        dh1_ref[...] = dh1
        dmix_ref[...] = _nt(dh1.astype(BF16), wo_ref[...]).astype(BF16)

    tok = lambda w: pl.BlockSpec((tm, w), lambda i: (i, 0))
    return pl.pallas_call(
        body, name="mlp_fwd_bwd", grid=(T // tm,),
        out_shape=[jax.ShapeDtypeStruct((T, D_MODEL), F32), jax.ShapeDtypeStruct((T, D_MODEL), BF16),
                   jax.ShapeDtypeStruct((T, D_MODEL), BF16), jax.ShapeDtypeStruct((T, D_FF), BF16),
                   jax.ShapeDtypeStruct((T, D_FF), BF16), jax.ShapeDtypeStruct((T, D_MODEL), BF16),
                   jax.ShapeDtypeStruct((1, 1), F32), jax.ShapeDtypeStruct((1, D_MODEL), F32),
                   jax.ShapeDtypeStruct((1, D_MODEL), F32)],
        in_specs=[tok(D_MODEL), tok(VAL), tok(CONV), tok(D_MODEL), _const((D_MODEL, D_MODEL)), _const((1, D_MODEL)),
                  _const((D_FF, D_MODEL)), _const((D_FF, D_MODEL)), _const((1, D_MODEL))],
        out_specs=[tok(D_MODEL), tok(D_MODEL), tok(D_MODEL), tok(D_FF), tok(D_FF), tok(D_MODEL),
                   pl.BlockSpec((1, 1), lambda i: (0, 0)), pl.BlockSpec((1, D_MODEL), lambda i: (0, 0)),
                   pl.BlockSpec((1, D_MODEL), lambda i: (0, 0))],
        compiler_params=_params(("arbitrary",)),
    )(x, mix_a, mix_c, tgt, w_out, g2, w1t, w2, gf)


def _silu_grad(v, s):
    return s * (1.0 + v * (1.0 - s))


def _conv_bwd_tile(ci_ref, cg_ref, uc_ref, dm_ref, w_ref, g_ref, be_ref, gm_ref,
                   dpc_ref, dw_ref, db_ref, dg_ref, dbe_ref, dbuf, dsh, dwacc, tm, nt):
    step = pl.program_id(0)
    sh_rows = tm + HALO - SUBLANES

    @pl.when(step == 0)
    def _():
        dbuf[tm:, :] = jnp.zeros((HALO, CONV), F32)
        dwacc[...] = jnp.zeros_like(dwacc)
        db_ref[...] = jnp.zeros_like(db_ref)
        dg_ref[...] = jnp.zeros_like(dg_ref)
        dbe_ref[...] = jnp.zeros_like(dbe_ref)

    gm, gv = gm_ref[...], g_ref[...]
    ucv = uc_ref[...]
    d = ucv - _group_mean(ucv, gm)
    rs = lax.rsqrt(_group_mean(d * d, gm) + EPS)
    yhat = d * rs
    yn = yhat * gv + be_ref[...]
    dyn = dm_ref[...].astype(F32) * _silu_grad(yn, jax.nn.sigmoid(yn))
    dg_ref[...] += _colsum(dyn * yhat)
    dbe_ref[...] += _colsum(dyn)
    dyh = dyn * gv
    duc = rs * (dyh - _group_mean(dyh, gm) - yhat * _group_mean(dyh * yhat, gm))
    db_ref[...] += _colsum(duc)
    dbuf[0:tm, :] = duc
    _shifted_copies(dbuf, dsh, sh_rows)

    def strip(s, carry):
        r0 = pl.multiple_of(s * STRIP, STRIP)
        rows = pl.ds(r0, STRIP)
        cin = ci_ref[rows, :].astype(F32)
        sg = jax.nn.sigmoid(cg_ref[rows, :].astype(F32))
        u = cin * sg
        du = jnp.zeros((STRIP, CONV), F32)
        for j in range(CONV_W):
            dj = _tap(dbuf, dsh, CONV_W - 1 - j, r0, STRIP)
            du = du + w_ref[j:j + 1, :] * dj
            p = u * dj
            fold = p[0:SUBLANES]
            for q in range(1, STRIP // SUBLANES):
                fold = fold + p[q * SUBLANES:(q + 1) * SUBLANES, :]
            dwacc[j * SUBLANES:(j + 1) * SUBLANES, :] += fold
        dpc_ref[rows, 0:CONV] = (du * sg).astype(BF16)
        dpc_ref[rows, CONV:] = (du * cin * sg * (1.0 - sg)).astype(BF16)
        return carry

    lax.fori_loop(0, tm // STRIP, strip, 0)
    dbuf[tm:, :] = dbuf[0:HALO, :]

    @pl.when(step == nt - 1)
    def _():
        dw_ref[...] = jnp.zeros_like(dw_ref)
        for j in range(CONV_W):
            dw_ref[j:j + 1, :] = _colsum(dwacc[j * SUBLANES:(j + 1) * SUBLANES, :])


def _conv_bwd(proj, uc, dmix, conv_w, cn_g, cn_b, gmat):
    T = proj.shape[0]
    tm = min(T, TOKEN_TILE)
    nt = T // tm
    sh_rows = tm + HALO - SUBLANES

    def body(*refs):
        _conv_bwd_tile(*refs, tm, nt)

    rev = lambda i: nt - 1 - i
    tile = lambda col: pl.BlockSpec((tm, CONV), lambda i: (rev(i), col))
    acc = lambda rows: pl.BlockSpec((rows, CONV), lambda i: (0, 0))
    return pl.pallas_call(
        body, name="conv_bwd", grid=(nt,),
        out_shape=[jax.ShapeDtypeStruct((T, 2 * CONV), BF16),
                   jax.ShapeDtypeStruct((HALO, CONV), F32), jax.ShapeDtypeStruct((1, CONV), F32),
                   jax.ShapeDtypeStruct((1, CONV), F32), jax.ShapeDtypeStruct((1, CONV), F32)],
        in_specs=[tile(P_CI // CONV), tile(P_CG // CONV), tile(0), tile(1),
                  _const((HALO, CONV)), _const((1, CONV)), _const((1, CONV)), _const((CONV, CONV))],
        out_specs=[pl.BlockSpec((tm, 2 * CONV), lambda i: (rev(i), 0)), acc(HALO), acc(1), acc(1), acc(1)],
        scratch_shapes=[pltpu.VMEM((tm + HALO, CONV), F32), pltpu.VMEM((SUBLANES - 1, sh_rows, CONV), F32),
                        pltpu.VMEM((HALO * SUBLANES, CONV), F32)],
        compiler_params=_params(("arbitrary",)),
    )(proj, proj, uc, dmix, conv_w, cn_g, cn_b, gmat)


def _gla_bwd_tile(q_ref, k_ref, v_ref, g_ref, z_ref, o_ref, st_ref, dm_ref, wg_ref, bg_ref, gn_ref,
                  dpg_ref, dwg_ref, dbg_ref, dgn_ref, dstate, do_scr, dv_scr, tb):
    @pl.when(pl.program_id(0) == 0)
    def _():
        dstate[...] = jnp.zeros_like(dstate)
        dwg_ref[...] = jnp.zeros_like(dwg_ref)
        dbg_ref[...] = jnp.zeros_like(dbg_ref)
        dgn_ref[...] = jnp.zeros_like(dgn_ref)

    gnv = gn_ref[...]
    dgn = jnp.zeros((1, DV), F32)
    for h in range(HEADS):
        cols = slice(h * DV, (h + 1) * DV)
        oh = o_ref[:, cols]
        r = lax.rsqrt(_rowmean(oh * oh) + EPS)
        ohat = oh * r
        gh = g_ref[:, cols].astype(F32)
        sg = jax.nn.sigmoid(gh)
        dmx = dm_ref[:, cols].astype(F32)
        don = dmx * (gh * sg)
        dpg_ref[:, P_G + h * DV:P_G + (h + 1) * DV] = (dmx * (ohat * gnv) * _silu_grad(gh, sg)).astype(BF16)
        dgn = dgn + _colsum(don * ohat)
        do_scr[:, cols] = _rms_bwd(don, ohat, r, gnv)
    dgn_ref[...] += dgn

    mats = _chunk_matrices()
    masks = _head_masks()
    wgv, bgv = wg_ref[...], bg_ref[...]
    n_chunks = SUB // CHUNK

    for sb in reversed(range(tb // SUB)):
        rows = slice(sb * SUB, (sb + 1) * SUB)
        zs = z_ref[rows, :]
        d = _decay_terms(zs, q_ref[rows, :], k_ref[rows, :], wgv, bgv, mats)
        qem = d.q * d.ebm
        qem_b = qem.astype(BF16)
        kem_b = (d.k * d.emb).astype(BF16)
        dq = jnp.zeros((SUB, KEY), F32)
        dk = jnp.zeros((SUB, KEY), F32)
        for h in range(HEADS):
            hm = masks[h]
            cols = slice(h * DV, (h + 1) * DV)
            do_b = do_scr[rows, cols].astype(BF16)
            vh = v_ref[rows, cols]
            da = jnp.where(mats.causal, _nt(do_b, vh), 0.0).astype(BF16)
            da_t = jnp.where(mats.causal_t, _nt(vh, do_b), 0.0).astype(BF16)
            a_t = jnp.where(mats.causal_t, _nt(kem_b, (qem * hm).astype(BF16)), 0.0).astype(BF16)
            dq = dq + hm * _nn(da, kem_b)
            dk = dk + hm * _nn(da_t, qem_b)
            dv_scr[rows, cols] = _nn(a_t, do_b)
        dq = dq * d.ebm
        dk = dk * d.emb

        qe0_b = (d.q * d.eb).astype(BF16)
        kdec_b = (d.k * d.elb).astype(BF16)
        dq_st, dk_st, last = [None] * n_chunks, [None] * n_chunks, [None] * n_chunks
        for c in reversed(range(n_chunks)):
            loc = slice(c * CHUNK, (c + 1) * CHUNK)
            glob = slice(sb * SUB + c * CHUNK, sb * SUB + (c + 1) * CHUNK)
            st_b = st_ref[sb * n_chunks + c]
            ds = dstate[...]
            ds_b = ds.astype(BF16)
            do_c = do_scr[glob, :].astype(BF16)
            ebl_c = d.ebl[c * CHUNK:c * CHUNK + 1]
            dk_c = _nn(v_ref[glob, :], ds_b) * d.elb[loc]
            dq_st[c] = _nn(do_c, st_b) * d.eb[loc]
            dk_st[c] = dk_c
            last_c = _colsum(d.k[loc] * dk_c) + ebl_c * _colsum(st_b.astype(F32) * ds)
            last[c] = jnp.broadcast_to(last_c, (CHUNK, KEY))
            dpg_ref[glob, P_V:P_G] = (dv_scr[glob, :] + _nt(kdec_b[loc], ds_b)).astype(BF16)
            dstate[...] = ds * ebl_c + jnp.where(mats.heads, _tn(do_c, qe0_b[loc]), 0.0)
        dq = dq + jnp.concatenate(dq_st, axis=0)
        dk = dk + jnp.concatenate(dk_st, axis=0)
        dpg_ref[rows, P_Q:P_K] = (dq * Q_SCALE).astype(BF16)
        dpg_ref[rows, P_K:P_V] = dk.astype(BF16)
        hi, lo = _split_bf16(d.q * dq - d.k * dk)
        dla = _nn(mats.tri_t, hi) + _nn(mats.tri_t, lo) + jnp.concatenate(last, axis=0)
        dal = dla * (1.0 / GATE_TAU) * jax.nn.sigmoid(-d.al)
        dal_b = dal.astype(BF16)
        dpg_ref[rows, OFF_Z:] = _nt(dal_b, wgv).astype(BF16)
        dwg_ref[...] += _tn(zs, dal_b)
        dbg_ref[...] += _colsum(dal)


def _gla_bwd(proj, o, states, dmix, wg, bg, gn, parts):
    T = proj.shape[0]
    tb = min(T, TOKEN_TILE)
    cpb = tb // CHUNK
    nb = T // tb
    n_comm = len(parts)
    kinds = ["exchange"] * n_comm
    comm_start, comm_wait = _hosted_comm(kinds, 11, 4, n_comm, nb)

    def body(*refs):
        comm_start(refs)
        _gla_bwd_tile(*refs[:11], *refs[11 + n_comm:15 + n_comm], *refs[15 + 2 * n_comm:18 + 2 * n_comm], tb)
        comm_wait(refs)

    rev = lambda i: nb - 1 - i
    blk = lambda w, col: pl.BlockSpec((tb, w), lambda i: (rev(i), col))
    res = pl.pallas_call(
        body, name="gla_bwd", grid=(nb,),
        out_shape=[jax.ShapeDtypeStruct((T, D_GLA), BF16), jax.ShapeDtypeStruct((Z_PAD, KEY), F32),
                   jax.ShapeDtypeStruct((1, KEY), F32), jax.ShapeDtypeStruct((1, DV), F32)]
        + _comm_out_shapes(kinds, parts),
        in_specs=[blk(KEY, P_Q // KEY), blk(KEY, P_K // KEY), blk(VAL, P_V // VAL), blk(VAL, P_G // VAL),
                  blk(Z_PAD, P_Z // Z_PAD), blk(VAL, 0),
                  pl.BlockSpec((cpb, VAL, KEY), lambda i: (rev(i), 0, 0)), blk(VAL, 0),
                  _const((Z_PAD, KEY)), _const((1, KEY)), _const((1, DV))] + [ANY] * n_comm,
        out_specs=[blk(D_GLA, 0),
                   pl.BlockSpec((Z_PAD, KEY), lambda i: (0, 0)), pl.BlockSpec((1, KEY), lambda i: (0, 0)),
                   pl.BlockSpec((1, DV), lambda i: (0, 0))] + [ANY] * n_comm,
        scratch_shapes=[pltpu.VMEM((VAL, KEY), F32), pltpu.VMEM((tb, VAL), F32), pltpu.VMEM((tb, VAL), F32)]
        + _comm_scratch(n_comm),
        compiler_params=_params(("arbitrary",)),
    )(proj, proj, proj, proj, proj, o, states, dmix, wg, bg, gn, *parts)
    return res[:4], res[4:]


def _inproj_bwd(x, g1, w_in_t, dh1, dp_gla, dp_conv):
    T = x.shape[0]
    tm = min(T, TOKEN_TILE)

    def body(x_ref, g_ref, w_ref, dh1_ref, dpg_ref, dpc_ref, dx_ref, dg1_ref):
        @pl.when(pl.program_id(0) == 0)
        def _():
            dg1_ref[...] = jnp.zeros_like(dg1_ref)

        dxn = _nn(dpg_ref[...], w_ref[0:D_GLA, :]) + _nn(dpc_ref[...], w_ref[OFF_C:D_IN, :])
        xv = x_ref[...]
        r = lax.rsqrt(_rowmean(xv * xv) + EPS)
        xhat = xv * r
        dg1_ref[...] += _colsum(dxn * xhat)
        dx_ref[...] = dh1_ref[...] + _rms_bwd(dxn, xhat, r, g_ref[...])

    tok = lambda w: pl.BlockSpec((tm, w), lambda i: (i, 0))
    return pl.pallas_call(
        body, name="inproj_bwd", grid=(T // tm,),
        out_shape=[jax.ShapeDtypeStruct((T, D_MODEL), F32), jax.ShapeDtypeStruct((1, D_MODEL), F32)],
        in_specs=[tok(D_MODEL), _const((1, D_MODEL)), _const((D_IN, D_MODEL)), tok(D_MODEL), tok(D_GLA),
                  tok(2 * CONV)],
        out_specs=[tok(D_MODEL), pl.BlockSpec((1, D_MODEL), lambda i: (0, 0))],
        compiler_params=_params(("arbitrary",)),
    )(x, g1, w_in_t, dh1, dp_gla, dp_conv)


def _wgrad_in(xn, dp_gla, dp_conv):
    T = xn.shape[0]
    tt = min(T, WGRAD_TILE // 2)
    nt = T // tt

    def body(xn_ref, dpg_ref, dpc_ref, o_ref, acc):
        @pl.when(pl.program_id(0) == 0)
        def _():
            acc[...] = jnp.zeros_like(acc)

        xv = xn_ref[...]
        acc[0:OFF_C, :] += _tn(dpg_ref[...], xv)[0:OFF_C]
        acc[OFF_C:, :] += _tn(dpc_ref[...], xv)

        @pl.when(pl.program_id(0) == nt - 1)
        def _():
            o_ref[...] = acc[...].astype(BF16)

    tok = lambda w: pl.BlockSpec((tt, w), lambda t: (t, 0))
    return pl.pallas_call(
        body, name="wgrad_in", grid=(nt,), out_shape=jax.ShapeDtypeStruct((D_IN, D_MODEL), BF16),
        in_specs=[tok(D_MODEL), tok(D_GLA), tok(2 * CONV)],
        out_specs=pl.BlockSpec((D_IN, D_MODEL), lambda t: (0, 0), pipeline_mode=pl.Buffered(1)),
        scratch_shapes=[pltpu.VMEM((D_IN, D_MODEL), F32)],
        compiler_params=_params(("arbitrary",)),
    )(xn, dp_gla, dp_conv)


def _wgrad_out(mix_a, mix_c, dh1):
    T = dh1.shape[0]
    tt = min(T, WGRAD_TILE // 2)
    nt = T // tt

    def body(a_ref, c_ref, b_ref, o_ref, acc):
        @pl.when(pl.program_id(0) == 0)
        def _():
            acc[...] = jnp.zeros_like(acc)

        b = b_ref[...].astype(BF16)
        acc[0:VAL, :] += _tn(a_ref[...], b)
        acc[VAL:, :] += _tn(c_ref[...], b)

        @pl.when(pl.program_id(0) == nt - 1)
        def _():
            o_ref[...] = acc[...].astype(BF16)

    tok = lambda w: pl.BlockSpec((tt, w), lambda t: (t, 0))
    return pl.pallas_call(
        body, name="wgrad_out", grid=(nt,), out_shape=jax.ShapeDtypeStruct((D_MODEL, D_MODEL), BF16),
        in_specs=[tok(VAL), tok(CONV), tok(D_MODEL)],
        out_specs=pl.BlockSpec((D_MODEL, D_MODEL), lambda t: (0, 0)),
        scratch_shapes=[pltpu.VMEM((D_MODEL, D_MODEL), F32)],
        compiler_params=_params(("arbitrary",)),
    )(mix_a, mix_c, dh1)


def _wgrad(a, b, name, tk, tn, col_block=None):
    T, K = a.shape
    N = b.shape[1]
    tt = min(T, WGRAD_TILE)
    nt = T // tt

    def body(a_ref, b_ref, o_ref, acc):
        @pl.when(pl.program_id(2) == 0)
        def _():
            acc[...] = jnp.zeros_like(acc)

        acc[...] += _tn(a_ref[...], b_ref[...].astype(BF16))

        @pl.when(pl.program_id(2) == nt - 1)
        def _():
            if col_block is None:
                o_ref[...] = acc[...].astype(BF16)
            else:
                for q in range(tn // col_block):
                    o_ref[q] = acc[:, q * col_block:(q + 1) * col_block].astype(BF16)

    if col_block is None:
        out_shape = jax.ShapeDtypeStruct((K, N), BF16)
        out_spec = pl.BlockSpec((tk, tn), lambda i, j, t: (i, j))
    else:
        assert tk == K
        out_shape = jax.ShapeDtypeStruct((N // col_block, K, col_block), BF16)
        out_spec = pl.BlockSpec((tn // col_block, tk, col_block), lambda i, j, t: (j, 0, 0))
    return pl.pallas_call(
        body, name=name, grid=(K // tk, N // tn, nt), out_shape=out_shape,
        in_specs=[pl.BlockSpec((tt, tk), lambda i, j, t: (t, i)), pl.BlockSpec((tt, tn), lambda i, j, t: (t, j))],
        out_specs=out_spec, scratch_shapes=[pltpu.VMEM((tk, tn), F32)],
        compiler_params=_params(("arbitrary", "arbitrary", "arbitrary")),
    )(a, b)


def _adam_math(w, g, m, v):
    m = ADAM_B1 * m + (1.0 - ADAM_B1) * g
    v = ADAM_B2 * v + (1.0 - ADAM_B2) * (g * g)
    m_hat = m / (1.0 - ADAM_B1 ** ADAM_STEP)
    v_hat = v / (1.0 - ADAM_B2 ** ADAM_STEP)
    delta = -ADAM_LR * (m_hat / (jnp.sqrt(v_hat) + ADAM_EPS) + ADAM_WD * w)
    return delta, m, v


def _sum8(ref):
    g = ref[0].astype(F32)
    for s in range(1, N_DEV):
        g = g + ref[s].astype(F32)
    return g


def _adam_big(parts, w, m, v, name):
    R, C = w.shape
    tr = ADAM_ROWS if R % ADAM_ROWS == 0 else R

    def body(p_ref, w_ref, m_ref, v_ref, g_ref, d_ref, nm_ref, nv_ref):
        g = _sum8(p_ref)
        g_ref[...] = g
        d_ref[...], nm_ref[...], nv_ref[...] = _adam_math(w_ref[...], g, m_ref[...], v_ref[...])

    row = pl.BlockSpec((tr, C), lambda i: (i, 0))
    return pl.pallas_call(
        body, name=name, grid=(R // tr,), out_shape=[jax.ShapeDtypeStruct((R, C), F32)] * 4,
        in_specs=[pl.BlockSpec((N_DEV, tr, C), lambda i: (0, i, 0)), row, row, row], out_specs=[row] * 4,
        compiler_params=_params(("arbitrary",)),
    )(parts, w, m, v)


def _sum_small(parts):
    def body(p_ref, o_ref):
        o_ref[...] = _sum8(p_ref)

    return pl.pallas_call(body, name="sum_small", out_shape=jax.ShapeDtypeStruct(parts.shape[1:], F32))(parts)


def _adam_small(gs, ws, ms, vs):
    n = len(gs)

    def body(*refs):
        g_refs, w_refs, m_refs, v_refs = refs[:n], refs[n:2 * n], refs[2 * n:3 * n], refs[3 * n:4 * n]
        outs = refs[4 * n:]
        for i in range(n):
            d, nm, nv = _adam_math(w_refs[i][...], g_refs[i][...], m_refs[i][...], v_refs[i][...])
            outs[i][...] = d
            outs[n + i][...] = nm
            outs[2 * n + i][...] = nv

    shapes = [jax.ShapeDtypeStruct(w.shape, F32) for w in ws]
    res = pl.pallas_call(body, name="adam_small", out_shape=shapes * 3)(*gs, *ws, *ms, *vs)
    return res[:n], res[n:2 * n], res[2 * n:]


def _group_matrix():
    gi = lax.broadcasted_iota(jnp.int32, (CONV, CONV), 0) // (CONV // GROUPS)
    gj = lax.broadcasted_iota(jnp.int32, (CONV, CONV), 1) // (CONV // GROUPS)
    return jnp.where(gi == gj, GROUPS / CONV, 0.0).astype(BF16)


_SMALL = [("loss", 8), ("dg1", 8), ("dbg", 2), ("dgn", 1), ("dconv_b", 4), ("dcn_g", 4), ("dcn_b", 4), ("dg2", 8),
          ("dgf", 8), ("dwg", 32), ("dconv_w", 124)]


def _pad8(rows):
    return -(-rows // 8) * 8


def kernel(x, norm1_g, w_in, w_gate_up, b_gate, gla_norm_g, conv_w, conv_b, conv_norm_g, conv_norm_b, w_out, norm2_g, w_mlp_in, w_mlp_out, final_norm_g, loss_target, m_norm1_g, m_w_in, m_w_gate_up, m_b_gate, m_gla_norm_g, m_conv_w, m_conv_b, m_conv_norm_g, m_conv_norm_b, m_w_out, m_norm2_g, m_w_mlp_in, m_w_mlp_out, m_final_norm_g, v_norm1_g, v_w_in, v_w_gate_up, v_b_gate, v_gla_norm_g, v_conv_w, v_conv_b, v_conv_norm_g, v_conv_norm_b, v_w_out, v_norm2_g, v_w_mlp_in, v_w_mlp_out, v_final_norm_g):
    x_idx = lax.axis_index("x")
    y_idx = lax.axis_index("y")
    c_idx = lax.axis_index("c")
    me = 4 * x_idx + 2 * y_idx + c_idx
    xs, tgt = x[0], loss_target[0]
    gf = final_norm_g.reshape(1, D_MODEL)
    gmat = _group_matrix()

    small_shard = jnp.zeros((_pad8(RANK + CONV_W), LANES), F32)
    small_shard = small_shard.at[0:RANK, 0:KEY // N_DEV].set(w_gate_up[0])
    small_shard = small_shard.at[RANK:RANK + CONV_W, 0:CONV // N_DEV].set(conv_w[0])
    xn, (g_in, g_small) = _norm_gather(xs, norm1_g, [w_in[0].T.astype(BF16), small_shard])
    w_in_t = g_in.reshape(D_IN, D_MODEL)
    wg_full = jnp.concatenate([g_small[d, 0:RANK, 0:KEY // N_DEV] for d in range(N_DEV)], axis=1)
    wg_pad = jnp.pad(wg_full, ((0, Z_PAD - RANK), (0, 0))).astype(BF16)
    conv_w_full = jnp.concatenate([g_small[d, RANK:RANK + CONV_W, 0:CONV // N_DEV] for d in range(N_DEV)], axis=1)
    conv_w_pad = jnp.pad(conv_w_full, ((0, HALO - CONV_W), (0, 0)))

    proj, (g_w2,) = _inproj_fwd(xn, w_in_t, [w_mlp_out[0].astype(BF16)])
    mix_a, o, states, mix_c, uc, (g_out, g_w1) = _mix_fwd(
        proj, wg_pad, b_gate, gla_norm_g, conv_w_pad, conv_b, conv_norm_g, conv_norm_b, gmat,
        [w_out[0].astype(BF16), w_mlp_in[0].T.astype(BF16)])
    w_out_full = g_out.reshape(D_MODEL, D_MODEL)
    w1t_full = g_w1.reshape(D_FF, D_MODEL)
    w2_full = g_w2.reshape(D_FF, D_MODEL)
    dh1, dmix, hn, ff, da, dh2, loss, dgf, dg2 = _mlp_fwd_bwd(xs, mix_a, mix_c, tgt, w_out_full, norm2_g, w1t_full,
                                                              w2_full, gf)

    dw1 = _wgrad(hn, da, "wgrad_mlp_in", WGRAD_BLOCK, WGRAD_BLOCK, col_block=D_FF // N_DEV)
    dw2 = _wgrad(ff, dh2, "wgrad_mlp_out", WGRAD_BLOCK, WGRAD_BLOCK)
    dw_out = _wgrad_out(mix_a, mix_c, dh1)
    dp_conv, dconv_w, dconv_b, dcn_g, dcn_b = _conv_bwd(proj, uc, dmix, conv_w_pad, conv_norm_g, conv_norm_b, gmat)
    (dp_gla, dwg, dbg, dgn), (p_w1, p_w2, p_out) = _gla_bwd(
        proj, o, states, dmix, wg_pad, b_gate, gla_norm_g,
        [dw1, dw2.reshape(N_DEV, D_FF // N_DEV, D_MODEL), dw_out.reshape(N_DEV, D_MODEL // N_DEV, D_MODEL)])
    dw_in = _wgrad_in(xn, dp_gla, dp_conv).reshape(N_DEV, SHARD_IN, D_MODEL)
    send_sems, recv_sems, dw_in_thru, land, token = _split_start("exchange", dw_in, jnp.copy(dw_in),
                                                                 "exchange_w_in_start")
    dx, dg1 = _inproj_bwd(xs, norm1_g + token[0:1, 0:1], w_in_t, dh1, dp_gla, dp_conv)
    p_in = _split_wait("exchange", send_sems, recv_sems, dw_in_thru, land, dg1, "exchange_w_in_wait")

    small = dict(loss=jnp.zeros((SUBLANES, LANES), F32) + loss, dg1=dg1, dbg=dbg, dgn=dgn, dconv_b=dconv_b, dcn_g=dcn_g,
                 dcn_b=dcn_b, dg2=dg2, dgf=dgf, dwg=dwg[0:RANK], dconv_w=dconv_w[0:CONV_W])
    pack = jnp.concatenate([jnp.pad(small[name].reshape(rows, LANES), ((0, _pad8(rows) - rows), (0, 0)))
                            for name, rows in _SMALL], axis=0)
    s_send, s_recv, pack_thru, pack_land, s_token = _split_start(
        "gather", pack, jnp.broadcast_to(pack, (N_DEV,) + pack.shape) + 0.0, "gather_small_start")

    gi, di, mi, vi = _adam_big(p_in, w_in[0].T, m_w_in[0].T, v_w_in[0].T, "adam_w_in")
    go, do, mo, vo = _adam_big(p_out, w_out[0] + s_token[0:1, 0:1], m_w_out[0], v_w_out[0], "adam_w_out")
    ga, da_, ma, va = _adam_big(p_w1, w_mlp_in[0], m_w_mlp_in[0], v_w_mlp_in[0], "adam_w_mlp_in")
    gb, db, mb, vb = _adam_big(p_w2, w_mlp_out[0], m_w_mlp_out[0], v_w_mlp_out[0], "adam_w_mlp_out")
    cut = lambda a: a.T[None]

    g_pack = _split_wait("gather", s_send, s_recv, pack_thru, pack_land, go[0:8, 0:128] + ga[0:8, 0:128]
                         + gb[0:8, 0:128], "gather_small_wait")
    summed = _sum_small(g_pack)
    small_g = {}
    at = 0
    for name, rows in _SMALL:
        small_g[name] = summed[at:at + rows]
        at += _pad8(rows)
    loss_out = small_g["loss"][0, 0]
    wg_cols = KEY // N_DEV
    cw_cols = CONV // N_DEV
    g_small_list = [
        small_g["dg1"].reshape(1, D_MODEL),
        lax.dynamic_slice_in_dim(small_g["dwg"].reshape(RANK, KEY), me * wg_cols, wg_cols, axis=1)[None],
        small_g["dbg"].reshape(1, KEY), small_g["dgn"].reshape(1, DV),
        lax.dynamic_slice_in_dim(small_g["dconv_w"].reshape(CONV_W, CONV), me * cw_cols, cw_cols, axis=1)[None],
        small_g["dconv_b"].reshape(1, CONV), small_g["dcn_g"].reshape(1, CONV), small_g["dcn_b"].reshape(1, CONV),
        small_g["dg2"].reshape(1, D_MODEL), small_g["dgf"].reshape(1, D_MODEL),
    ]
    row = lambda a: a.reshape(1, D_MODEL)
    w_small = [norm1_g, w_gate_up, b_gate, gla_norm_g, conv_w, conv_b, conv_norm_g, conv_norm_b, norm2_g,
               row(final_norm_g)]
    m_small = [m_norm1_g, m_w_gate_up, m_b_gate, m_gla_norm_g, m_conv_w, m_conv_b, m_conv_norm_g, m_conv_norm_b,
               m_norm2_g, row(m_final_norm_g)]
    v_small = [v_norm1_g, v_w_gate_up, v_b_gate, v_gla_norm_g, v_conv_w, v_conv_b, v_conv_norm_g, v_conv_norm_b,
               v_norm2_g, row(v_final_norm_g)]
    d_small, nm_small, nv_small = _adam_small(g_small_list, w_small, m_small, v_small)
    flat = lambda lst: list(lst[:-1]) + [lst[-1].reshape(D_MODEL)]
    g_small_list, d_small, nm_small, nv_small = flat(g_small_list), flat(d_small), flat(nm_small), flat(nv_small)

    def order(s, w_in_v, w_out_v, w1_v, w2_v):
        return [s[0], w_in_v, s[1], s[2], s[3], s[4], s[5], s[6], s[7], w_out_v, s[8], w1_v, w2_v, s[9]]

    grads = order(g_small_list, cut(gi), go[None], ga[None], gb[None])
    deltas = order(d_small, cut(di), do[None], da_[None], db[None])
    new_m = order(nm_small, cut(mi), mo[None], ma[None], mb[None])
    new_v = order(nv_small, cut(vi), vo[None], va[None], vb[None])
    return (loss_out, dx[None], *grads, *deltas, *new_m, *new_v)
```

```python
from typing import NamedTuple

import jax
import jax.numpy as jnp
from jax import lax
from jax.experimental import pallas as pl
from jax.experimental.pallas import tpu as pltpu

F32 = jnp.float32
BF16 = jnp.bfloat16

N_DEV = 8
D_MODEL = 1024
HEADS = 4
DK = 64
DV = 128
KEY = HEADS * DK
VAL = HEADS * DV
RANK = 16
CONV = 512
GROUPS = 8
CONV_W = 31
HALO = 32
SUBLANES = 8
LANES = 128
STRIP = 32
FWD_STRIP = 16
TOKEN_TILE = 512
MLP_TILE = 256
WGRAD_TILE = 4096
WGRAD_BLOCK = 1024
ADAM_ROWS = 128
RING = 3
D_FF = 4096
D_IN = 2576
SHARD_IN = D_IN // N_DEV
CHUNK = 64
SUB = 256
EPS = 1e-6
GATE_TAU = 16.0
Q_SCALE = DK ** -0.5

P_Q, P_K, P_V, P_G, P_CI, P_CG, P_Z = 0, 256, 512, 1024, 1536, 2048, 2560
D_INP = 2688
Z_PAD = D_INP - P_Z
OFF_Z = 1536
OFF_C = OFF_Z + RANK
D_GLA = OFF_Z + Z_PAD

ADAM_LR = 0.001
ADAM_B1 = 0.9
ADAM_B2 = 0.999
ADAM_EPS = 1e-08
ADAM_WD = 0.01
ADAM_STEP = 10

V7X_VMEM_BYTES = 64 * 1024 * 1024
VMEM_LIMIT = V7X_VMEM_BYTES * 7 // 8

MESH = pl.DeviceIdType.MESH
ANY = pl.BlockSpec(memory_space=pl.ANY)


def _nn(a, b):
    return jnp.dot(a, b, preferred_element_type=F32)


def _nt(a, b):
    return lax.dot_general(a, b, (((1,), (1,)), ((), ())), preferred_element_type=F32)


def _tn(a, b):
    return lax.dot_general(a, b, (((0,), (0,)), ((), ())), preferred_element_type=F32)


def _params(sem=None):
    return pltpu.CompilerParams(dimension_semantics=sem, vmem_limit_bytes=VMEM_LIMIT)


def _const(shape):
    return pl.BlockSpec(shape, lambda *_: (0,) * len(shape), pipeline_mode=pl.Buffered(1))


def _colsum(v):
    return jnp.sum(v, axis=0, keepdims=True)


def _rowmean(v):
    return jnp.mean(v, axis=-1, keepdims=True)


def _split_bf16(v):
    hi = v.astype(BF16)
    return hi, (v - hi.astype(F32)).astype(BF16)


def _my_place():
    return lax.axis_index("x"), lax.axis_index("y"), lax.axis_index("c")


def _peer(j):
    x, y, c = _my_place()
    jx, jy, jc = (j >> 2) & 1, (j >> 1) & 1, j & 1
    px = 1 - x if jx else x
    py = 1 - y if jy else y
    pc = 1 - c if jc else c
    return (px, py, pc), 4 * px + 2 * py + pc


def _comm_plan(kinds, ins, outs, send_sems, recv_sems, local_sems, receives=True):
    x, y, c = _my_place()
    me = 4 * x + 2 * y + c
    own = lambda k, idx: ins[k] if kinds[k] == "gather" else ins[k].at[idx]
    local = [pltpu.make_async_copy(own(k, me), outs[k].at[me], local_sems.at[k]) for k in range(len(kinds))]
    sends, recvs = [], []
    for j in range(1, N_DEV):
        peer, peer_idx = _peer(j)
        for k in range(len(kinds)):
            sems = dict(send_sem=send_sems.at[k, j - 1], recv_sem=recv_sems.at[k, j - 1], device_id=peer,
                        device_id_type=MESH)
            sends.append(pltpu.make_async_remote_copy(src_ref=own(k, peer_idx), dst_ref=outs[k].at[me], **sems))
            if receives:
                recvs.append(pltpu.make_async_remote_copy(src_ref=own(k, me), dst_ref=outs[k].at[peer_idx], **sems))
    return local, sends, recvs


def _comm_start(plan):
    local, sends, _ = plan
    for cp in local + sends:
        cp.start()


def _comm_wait(plan):
    local, sends, recvs = plan
    for cp in recvs:
        cp.wait_recv()
    for cp in sends:
        cp.wait_send()
    for cp in local:
        cp.wait()


def _comm_scratch(n):
    return [pltpu.SemaphoreType.DMA((n, N_DEV - 1)), pltpu.SemaphoreType.DMA((n, N_DEV - 1)),
            pltpu.SemaphoreType.DMA((n,))]


def _comm_out_shapes(kinds, arrays):
    return [jax.ShapeDtypeStruct(((N_DEV,) + a.shape) if kind == "gather" else a.shape, a.dtype)
            for kind, a in zip(kinds, arrays)]


def _gather_two_level(shards, name):
    n = len(shards)

    def body(*refs):
        ins, outs = refs[:n], refs[n:2 * n]
        send_sems, recv_sems, local_sems = refs[2 * n:]
        x, y, c = _my_place()
        index = lambda px, py, pc: 4 * px + 2 * py + pc
        me, sibling = (x, y, c), (x, y, 1 - c)
        chips = [(1 - x, y), (x, 1 - y), (1 - x, 1 - y)]

        def copy(k, slot, block, to, src=None):
            rows = outs[k].at[index(*block)]
            return pltpu.make_async_remote_copy(
                src_ref=rows if src is None else src, dst_ref=rows, send_sem=send_sems.at[k, slot],
                recv_sem=recv_sems.at[k, slot], device_id=to, device_id_type=MESH)

        local = [pltpu.make_async_copy(ins[k], outs[k].at[index(*me)], local_sems.at[k]) for k in range(n)]
        first = []
        for k in range(n):
            first.append(copy(k, 0, me, sibling, src=ins[k]))
            first += [copy(k, 1 + j, me, (*chip, c), src=ins[k]) for j, chip in enumerate(chips)]
        for cp in local + first:
            cp.start()
        passed = []
        for j, chip in enumerate(chips):
            for k in range(n):
                copy(k, 1 + j, (*chip, c), me).wait_recv()
                cp = copy(k, 4 + j, (*chip, c), sibling)
                cp.start()
                passed.append(cp)
        for k in range(n):
            copy(k, 0, sibling, me).wait_recv()
        for j, chip in enumerate(chips):
            for k in range(n):
                copy(k, 4 + j, (*chip, 1 - c), me).wait_recv()
        for cp in first + passed:
            cp.wait_send()
        for cp in local:
            cp.wait()

    return pl.pallas_call(
        body, name=name, out_shape=_comm_out_shapes(["gather"] * n, shards), in_specs=[ANY] * n, out_specs=[ANY] * n,
        scratch_shapes=_comm_scratch(n),
    )(*shards)


HBM = pl.BlockSpec(memory_space=pltpu.HBM)
SEM = pl.BlockSpec(memory_space=pltpu.SEMAPHORE)
DATAFLOW = pltpu.SideEffectType.DATAFLOW_SIDE_EFFECTING


def _split_start(kind, part, land, name):
    def body(src_ref, land_ref, send_sems, recv_sems, src_thru, land_thru, token):
        x, y, c = _my_place()
        me = 4 * x + 2 * y + c
        for j in range(1, N_DEV):
            peer, peer_idx = _peer(j)
            pltpu.make_async_remote_copy(
                src_ref=src_ref.at[peer_idx] if kind == "exchange" else src_ref, dst_ref=land_ref.at[me],
                send_sem=send_sems.at[j - 1], recv_sem=recv_sems.at[j - 1], device_id=peer,
                device_id_type=MESH).start()
        token[...] = jnp.zeros_like(token)

    return pl.pallas_call(
        body, name=name,
        out_shape=(pltpu.SemaphoreType.DMA((N_DEV - 1,)), pltpu.SemaphoreType.DMA((N_DEV - 1,)),
                   pltpu.HBM(part.shape, part.dtype), pltpu.HBM(land.shape, land.dtype),
                   jax.ShapeDtypeStruct((SUBLANES, LANES), F32)),
        in_specs=(HBM, HBM), out_specs=(SEM, SEM, HBM, HBM, pl.BlockSpec(memory_space=pltpu.VMEM)),
        input_output_aliases={0: 2, 1: 3},
        compiler_params=pltpu.CompilerParams(has_side_effects=DATAFLOW),
    )(pltpu.with_memory_space_constraint(part, pltpu.HBM), pltpu.with_memory_space_constraint(land, pltpu.HBM))


def _split_wait(kind, send_sems, recv_sems, part_thru, land_thru, after, name):
    def body(src_ref, land_ref, send_sems, recv_sems, after_ref, src_dead, got_ref):
        x, y, c = _my_place()
        me = 4 * x + 2 * y + c
        own = lambda idx: src_ref.at[idx] if kind == "exchange" else src_ref
        for j in range(1, N_DEV):
            peer, peer_idx = _peer(j)
            sems = dict(send_sem=send_sems.at[j - 1], recv_sem=recv_sems.at[j - 1], device_id=peer,
                        device_id_type=MESH)
            pltpu.make_async_remote_copy(src_ref=own(peer_idx), dst_ref=land_ref.at[me], **sems).wait_send()
            pltpu.make_async_remote_copy(src_ref=own(me), dst_ref=land_ref.at[peer_idx], **sems).wait_recv()

    return pl.pallas_call(
        body, name=name,
        out_shape=(pltpu.HBM(part_thru.shape, part_thru.dtype), pltpu.HBM(land_thru.shape, land_thru.dtype)),
        in_specs=(HBM, HBM, SEM, SEM, ANY), out_specs=(HBM, HBM), input_output_aliases={0: 0, 1: 1},
        compiler_params=pltpu.CompilerParams(has_side_effects=DATAFLOW),
    )(part_thru, land_thru, send_sems, recv_sems, after)[1]


def _hosted_comm(kinds, n_in, n_out, n_comm, n_steps):
    def plan_of(refs, receives):
        ins = refs[n_in:n_in + n_comm]
        outs = refs[n_in + n_comm + n_out:n_in + 2 * n_comm + n_out]
        return _comm_plan(kinds, ins, outs, *refs[-3:], receives=receives)

    def start(refs):
        @pl.when(pl.program_id(0) == 0)
        def _():
            _comm_start(plan_of(refs, False))

    def wait(refs):
        @pl.when(pl.program_id(0) == n_steps - 1)
        def _():
            _comm_wait(plan_of(refs, True))

    return start, wait


def _z_lanes():
    return lax.broadcasted_iota(jnp.int32, (1, Z_PAD), 1) < RANK


def _inproj_fwd(x, g1, w_in_t, shards):
    T = x.shape[0]
    tm = min(T, TOKEN_TILE)
    n_comm = len(shards)
    kinds = ["gather"] * n_comm
    comm_start, comm_wait = _hosted_comm(kinds, 3, 2, n_comm, T // tm)

    def body(*refs):
        x_ref, g_ref, w_ref = refs[:3]
        proj_ref, xn_ref = refs[3 + n_comm:5 + n_comm]
        comm_start(refs)
        xv = x_ref[...]
        r = lax.rsqrt(_rowmean(xv * xv) + EPS)
        xn = (xv * r * g_ref[...]).astype(BF16)
        xn_ref[...] = xn
        proj_ref[:, 0:P_CI] = _nt(xn, w_ref[0:OFF_Z, :]).astype(BF16)
        proj_ref[:, P_CI:P_Z] = _nt(xn, w_ref[OFF_C:D_IN, :]).astype(BF16)
        proj_ref[:, P_Z:] = jnp.where(_z_lanes(), _nt(xn, w_ref[OFF_Z:OFF_Z + Z_PAD, :]), 0.0).astype(BF16)
        comm_wait(refs)

    res = pl.pallas_call(
        body, name="inproj_fwd", grid=(T // tm,),
        out_shape=[jax.ShapeDtypeStruct((T, D_INP), BF16), jax.ShapeDtypeStruct((T, D_MODEL), BF16)]
        + _comm_out_shapes(kinds, shards),
        in_specs=[pl.BlockSpec((tm, D_MODEL), lambda i: (i, 0)), _const((1, D_MODEL)), _const((D_IN, D_MODEL))]
        + [ANY] * n_comm,
        out_specs=[pl.BlockSpec((tm, D_INP), lambda i: (i, 0)), pl.BlockSpec((tm, D_MODEL), lambda i: (i, 0))]
        + [ANY] * n_comm,
        scratch_shapes=_comm_scratch(n_comm),
        compiler_params=_params(("arbitrary",)),
    )(x, g1, w_in_t, *shards)
    return res[0], res[1], res[2:]


def _head_masks():
    lane = lax.broadcasted_iota(jnp.int32, (1, KEY), 1)
    return [((lane >= h * DK) & (lane < (h + 1) * DK)).astype(F32) for h in range(HEADS)]


class _Mats(NamedTuple):
    tri: jax.Array
    tri_t: jax.Array
    same: jax.Array
    mid: jax.Array
    causal: jax.Array
    causal_t: jax.Array
    heads: jax.Array


def _chunk_matrices():
    r = lax.broadcasted_iota(jnp.int32, (SUB, SUB), 0)
    c = lax.broadcasted_iota(jnp.int32, (SUB, SUB), 1)
    shift = CHUNK.bit_length() - 1
    same = jnp.right_shift(r, shift) == jnp.right_shift(c, shift)
    causal = same & (r >= c)
    causal_t = same & (r <= c)
    mid = same & ((c & (CHUNK - 1)) < CHUNK // 2)
    hr = jnp.right_shift(lax.broadcasted_iota(jnp.int32, (VAL, KEY), 0), DV.bit_length() - 1)
    hc = jnp.right_shift(lax.broadcasted_iota(jnp.int32, (VAL, KEY), 1), DK.bit_length() - 1)
    return _Mats(tri=causal.astype(BF16), tri_t=causal_t.astype(BF16), same=same.astype(BF16), mid=mid.astype(BF16),
                 causal=causal, causal_t=causal_t, heads=hr == hc)


class _Decay(NamedTuple):
    al: jax.Array
    q: jax.Array
    k: jax.Array
    eb: jax.Array
    ebm: jax.Array
    emb: jax.Array
    elb: jax.Array
    ebl: jax.Array


def _decay_terms(z, q, k, wg, bg, mats):
    al = _nn(z, wg) + bg
    la = (jnp.minimum(al, 0.0) - jnp.log(1.0 + jnp.exp(-jnp.abs(al)))) * (1.0 / GATE_TAU)
    hi, lo = _split_bf16(la)
    cum = lambda m: _nn(m, hi) + _nn(m, lo)
    b, b_last, b_mid = cum(mats.tri), cum(mats.same), cum(mats.mid)
    return _Decay(al=al, q=q.astype(F32) * Q_SCALE, k=k.astype(F32), eb=jnp.exp(b), ebm=jnp.exp(b - b_mid),
                  emb=jnp.exp(b_mid - b), elb=jnp.exp(b_last - b), ebl=jnp.exp(b_last))


def _gla_fwd_tile(q_ref, k_ref, v_ref, g_ref, z_ref, wg_ref, bg_ref, gn_ref, mix_ref, o_ref, st_ref, state, tb):
    @pl.when(pl.program_id(0) == 0)
    def _():
        state[...] = jnp.zeros_like(state)

    mats = _chunk_matrices()
    masks = _head_masks()
    wgv, bgv = wg_ref[...], bg_ref[...]

    for sb in range(tb // SUB):
        rows = slice(sb * SUB, (sb + 1) * SUB)
        d = _decay_terms(z_ref[rows, :], q_ref[rows, :], k_ref[rows, :], wgv, bgv, mats)
        kem_b = (d.k * d.emb).astype(BF16)
        qem = d.q * d.ebm
        for h in range(HEADS):
            cols = slice(h * DV, (h + 1) * DV)
            a = jnp.where(mats.causal, _nt((qem * masks[h]).astype(BF16), kem_b), 0.0)
            o_ref[rows, cols] = _nn(a.astype(BF16), v_ref[rows, cols])
        qe0_b = (d.q * d.eb).astype(BF16)
        kdec_b = (d.k * d.elb).astype(BF16)
        for c in range(SUB // CHUNK):
            loc = slice(c * CHUNK, (c + 1) * CHUNK)
            glob = slice(sb * SUB + c * CHUNK, sb * SUB + (c + 1) * CHUNK)
            st = state[...]
            st_b = st.astype(BF16)
            st_ref[sb * (SUB // CHUNK) + c] = st_b
            o_ref[glob, :] += _nt(qe0_b[loc], st_b)
            u = _tn(v_ref[glob, :], kdec_b[loc])
            state[...] = st * d.ebl[c * CHUNK:c * CHUNK + 1] + jnp.where(mats.heads, u, 0.0)

    gnv = gn_ref[...]
    for h in range(HEADS):
        cols = slice(h * DV, (h + 1) * DV)
        oh = o_ref[:, cols]
        r = lax.rsqrt(_rowmean(oh * oh) + EPS)
        gh = g_ref[:, cols].astype(F32)
        mix_ref[:, cols] = (oh * r * gnv * (gh * jax.nn.sigmoid(gh))).astype(BF16)


def _group_mean(v, gmat):
    return _nn(v.astype(BF16), gmat)


def _shifted_copies(buf, sh, rows):
    for k in range(1, SUBLANES):
        sh[k - 1] = buf[pl.ds(k, rows), :]


def _tap(buf, sh, off, r0, n):
    k, base = off % SUBLANES, off - off % SUBLANES
    rows = pl.ds(r0 + base if isinstance(r0, int) else pl.multiple_of(r0 + base, SUBLANES), n)
    return buf[rows, :] if k == 0 else sh[k - 1, rows, :]


def _conv_fwd_tile(ci_ref, cg_ref, w_ref, b_ref, g_ref, be_ref, gm_ref, mix_ref, uc_ref, ubuf, ush, tm):
    sh_rows = tm + HALO - SUBLANES

    @pl.when(pl.program_id(0) == 0)
    def _():
        ubuf[0:HALO, :] = jnp.zeros((HALO, CONV), F32)

    ubuf[HALO:, :] = ci_ref[...].astype(F32) * jax.nn.sigmoid(cg_ref[...].astype(F32))
    _shifted_copies(ubuf, ush, sh_rows)
    for s in range(tm // FWD_STRIP):
        acc = jnp.zeros((FWD_STRIP, CONV), F32) + b_ref[...]
        for j in range(CONV_W):
            acc = acc + w_ref[j:j + 1, :] * _tap(ubuf, ush, HALO - (CONV_W - 1) + j, s * FWD_STRIP, FWD_STRIP)
        uc_ref[s * FWD_STRIP:(s + 1) * FWD_STRIP, :] = acc
    ubuf[0:HALO, :] = ubuf[tm:tm + HALO, :]
    gm = gm_ref[...]
    ucv = uc_ref[...]
    d = ucv - _group_mean(ucv, gm)
    var = _group_mean(d * d, gm)
    yn = d * lax.rsqrt(var + EPS) * g_ref[...] + be_ref[...]
    mix_ref[...] = (yn * jax.nn.sigmoid(yn)).astype(BF16)


def _mix_fwd(proj, wg, bg, gn, conv_w, conv_b, cn_g, cn_b, gmat, shards):
    T = proj.shape[0]
    tb = min(T, TOKEN_TILE)
    cpb = tb // CHUNK
    n_comm = len(shards)
    kinds = ["gather"] * n_comm
    comm_start, comm_wait = _hosted_comm(kinds, 15, 5, n_comm, T // tb)

    def body(*refs):
        gla_in, conv_in = refs[:8], refs[8:15]
        gla_out, conv_out = refs[15 + n_comm:18 + n_comm], refs[18 + n_comm:20 + n_comm]
        state, ubuf, ush = refs[20 + 2 * n_comm:23 + 2 * n_comm]
        comm_start(refs)
        _gla_fwd_tile(*gla_in, *gla_out, state, tb)
        _conv_fwd_tile(*conv_in, *conv_out, ubuf, ush, tb)
        comm_wait(refs)

    nc = T // CHUNK
    tok = lambda w, col: pl.BlockSpec((tb, w), lambda i: (i, col))
    res = pl.pallas_call(
        body, name="mix_fwd", grid=(T // tb,),
        out_shape=[jax.ShapeDtypeStruct((T, VAL), BF16), jax.ShapeDtypeStruct((T, VAL), F32),
                   jax.ShapeDtypeStruct((nc, VAL, KEY), BF16), jax.ShapeDtypeStruct((T, CONV), BF16),
                   jax.ShapeDtypeStruct((T, CONV), F32)] + _comm_out_shapes(kinds, shards),
        in_specs=[tok(KEY, P_Q // KEY), tok(KEY, P_K // KEY), tok(VAL, P_V // VAL), tok(VAL, P_G // VAL),
                  tok(Z_PAD, P_Z // Z_PAD), _const((Z_PAD, KEY)), _const((1, KEY)), _const((1, DV)),
                  tok(CONV, P_CI // CONV), tok(CONV, P_CG // CONV), _const((HALO, CONV)), _const((1, CONV)),
                  _const((1, CONV)), _const((1, CONV)), _const((CONV, CONV))] + [ANY] * n_comm,
        out_specs=[tok(VAL, 0), tok(VAL, 0), pl.BlockSpec((cpb, VAL, KEY), lambda i: (i, 0, 0)), tok(CONV, 0),
                   tok(CONV, 0)] + [ANY] * n_comm,
        scratch_shapes=[pltpu.VMEM((VAL, KEY), F32), pltpu.VMEM((tb + HALO, CONV), F32),
                        pltpu.VMEM((SUBLANES - 1, tb + HALO - SUBLANES, CONV), F32)] + _comm_scratch(n_comm),
        compiler_params=_params(("arbitrary",)),
    )(proj, proj, proj, proj, proj, wg, bg, gn, proj, proj, conv_w, conv_b, cn_g, cn_b, gmat, *shards)
    return res[0], res[1], res[2], res[3], res[4], res[5:]


def _rms_bwd(dy, xhat, r, g):
    dyg = dy * g
    return r * (dyg - xhat * _rowmean(dyg * xhat))


def _mlp_fwd_bwd(x, mix_a, mix_c, tgt, w_out, g2, w1t, w2, gf):
    T = x.shape[0]
    tm = min(T, MLP_TILE)
    inv_d = 1.0 / D_MODEL

    def body(x_ref, ma_ref, mc_ref, t_ref, wo_ref, g2_ref, w1_ref, w2_ref, gf_ref,
             dh1_ref, dmix_ref, hn_ref, ff_ref, da_ref, dh2_ref, loss_ref, dgf_ref, dg2_ref):
        @pl.when(pl.program_id(0) == 0)
        def _():
            loss_ref[...] = jnp.zeros_like(loss_ref)
            dgf_ref[...] = jnp.zeros_like(dgf_ref)
            dg2_ref[...] = jnp.zeros_like(dg2_ref)

        g2v, gfv = g2_ref[...], gf_ref[...]
        h1 = x_ref[...] + _nn(ma_ref[...], wo_ref[0:VAL, :]) + _nn(mc_ref[...], wo_ref[VAL:, :])
        r2 = lax.rsqrt(_rowmean(h1 * h1) + EPS)
        h1hat = h1 * r2
        hn = (h1hat * g2v).astype(BF16)
        hn_ref[...] = hn
        relu_a = jnp.maximum(_nt(hn, w1_ref[...]), 0.0)
        ff = (relu_a * relu_a).astype(BF16)
        ff_ref[...] = ff
        h2 = h1 + _nn(ff, w2_ref[...])
        rf = lax.rsqrt(_rowmean(h2 * h2) + EPS)
        h2hat = h2 * rf
        err = h2hat * gfv - t_ref[...]
        loss_ref[...] += (0.5 * inv_d) * _colsum(jnp.sum(err * err, axis=1, keepdims=True))
        dy = err * inv_d
        dgf_ref[...] += _colsum(dy * h2hat)
        dh2 = _rms_bwd(dy, h2hat, rf, gfv)
        dh2_b = dh2.astype(BF16)
        dh2_ref[...] = dh2_b
        da = (_nt(dh2_b, w2_ref[...]) * (2.0 * relu_a)).astype(BF16)
        da_ref[...] = da
        dhn = _nn(da, w1_ref[...])
        dg2_ref[...] += _colsum(dhn * h1hat)
        dh1 = dh2 + _rms_bwd(dhn, h1hat, r2, g2v)
        dh1_ref[...] = dh1
        dmix_ref[...] = _nt(dh1.astype(BF16), wo_ref[...]).astype(BF16)

    tok = lambda w: pl.BlockSpec((tm, w), lambda i: (i, 0))
    return pl.pallas_call(
        body, name="mlp_fwd_bwd", grid=(T // tm,),
        out_shape=[jax.ShapeDtypeStruct((T, D_MODEL), F32), jax.ShapeDtypeStruct((T, D_MODEL), BF16),
                   jax.ShapeDtypeStruct((T, D_MODEL), BF16), jax.ShapeDtypeStruct((T, D_FF), BF16),
                   jax.ShapeDtypeStruct((T, D_FF), BF16), jax.ShapeDtypeStruct((T, D_MODEL), BF16),
                   jax.ShapeDtypeStruct((1, 1), F32), jax.ShapeDtypeStruct((1, D_MODEL), F32),
                   jax.ShapeDtypeStruct((1, D_MODEL), F32)],
        in_specs=[tok(D_MODEL), tok(VAL), tok(CONV), tok(D_MODEL), _const((D_MODEL, D_MODEL)), _const((1, D_MODEL)),
                  _const((D_FF, D_MODEL)), _const((D_FF, D_MODEL)), _const((1, D_MODEL))],
        out_specs=[tok(D_MODEL), tok(D_MODEL), tok(D_MODEL), tok(D_FF), tok(D_FF), tok(D_MODEL),
                   pl.BlockSpec((1, 1), lambda i: (0, 0)), pl.BlockSpec((1, D_MODEL), lambda i: (0, 0)),
                   pl.BlockSpec((1, D_MODEL), lambda i: (0, 0))],
        compiler_params=_params(("arbitrary",)),
    )(x, mix_a, mix_c, tgt, w_out, g2, w1t, w2, gf)


def _silu_grad(v, s):
    return s * (1.0 + v * (1.0 - s))


def _conv_bwd_tile(ci_ref, cg_ref, uc_ref, dm_ref, w_ref, g_ref, be_ref, gm_ref,
                   dpc_ref, dw_ref, db_ref, dg_ref, dbe_ref, dbuf, dsh, dwacc, tm, nt):
    step = pl.program_id(0)
    sh_rows = tm + HALO - SUBLANES

    @pl.when(step == 0)
    def _():
        dbuf[tm:, :] = jnp.zeros((HALO, CONV), F32)
        dwacc[...] = jnp.zeros_like(dwacc)
        db_ref[...] = jnp.zeros_like(db_ref)
        dg_ref[...] = jnp.zeros_like(dg_ref)
        dbe_ref[...] = jnp.zeros_like(dbe_ref)

    gm, gv = gm_ref[...], g_ref[...]
    ucv = uc_ref[...]
    d = ucv - _group_mean(ucv, gm)
    rs = lax.rsqrt(_group_mean(d * d, gm) + EPS)
    yhat = d * rs
    yn = yhat * gv + be_ref[...]
    dyn = dm_ref[...].astype(F32) * _silu_grad(yn, jax.nn.sigmoid(yn))
    dg_ref[...] += _colsum(dyn * yhat)
    dbe_ref[...] += _colsum(dyn)
    dyh = dyn * gv
    duc = rs * (dyh - _group_mean(dyh, gm) - yhat * _group_mean(dyh * yhat, gm))
    db_ref[...] += _colsum(duc)
    dbuf[0:tm, :] = duc
    _shifted_copies(dbuf, dsh, sh_rows)

    def strip(s, carry):
        r0 = pl.multiple_of(s * STRIP, STRIP)
        rows = pl.ds(r0, STRIP)
        cin = ci_ref[rows, :].astype(F32)
        sg = jax.nn.sigmoid(cg_ref[rows, :].astype(F32))
        u = cin * sg
        du = jnp.zeros((STRIP, CONV), F32)
        for j in range(CONV_W):
            dj = _tap(dbuf, dsh, CONV_W - 1 - j, r0, STRIP)
            du = du + w_ref[j:j + 1, :] * dj
            p = u * dj
            fold = p[0:SUBLANES]
            for q in range(1, STRIP // SUBLANES):
                fold = fold + p[q * SUBLANES:(q + 1) * SUBLANES, :]
            dwacc[j * SUBLANES:(j + 1) * SUBLANES, :] += fold
        dpc_ref[rows, 0:CONV] = (du * sg).astype(BF16)
        dpc_ref[rows, CONV:] = (du * cin * sg * (1.0 - sg)).astype(BF16)
        return carry

    lax.fori_loop(0, tm // STRIP, strip, 0)
    dbuf[tm:, :] = dbuf[0:HALO, :]

    @pl.when(step == nt - 1)
    def _():
        dw_ref[...] = jnp.zeros_like(dw_ref)
        for j in range(CONV_W):
            dw_ref[j:j + 1, :] = _colsum(dwacc[j * SUBLANES:(j + 1) * SUBLANES, :])


def _conv_bwd(proj, uc, dmix, conv_w, cn_g, cn_b, gmat):
    T = proj.shape[0]
    tm = min(T, TOKEN_TILE)
    nt = T // tm
    sh_rows = tm + HALO - SUBLANES

    def body(*refs):
        _conv_bwd_tile(*refs, tm, nt)

    rev = lambda i: nt - 1 - i
    tile = lambda col: pl.BlockSpec((tm, CONV), lambda i: (rev(i), col))
    acc = lambda rows: pl.BlockSpec((rows, CONV), lambda i: (0, 0))
    return pl.pallas_call(
        body, name="conv_bwd", grid=(nt,),
        out_shape=[jax.ShapeDtypeStruct((T, 2 * CONV), BF16),
                   jax.ShapeDtypeStruct((HALO, CONV), F32), jax.ShapeDtypeStruct((1, CONV), F32),
                   jax.ShapeDtypeStruct((1, CONV), F32), jax.ShapeDtypeStruct((1, CONV), F32)],
        in_specs=[tile(P_CI // CONV), tile(P_CG // CONV), tile(0), tile(1),
                  _const((HALO, CONV)), _const((1, CONV)), _const((1, CONV)), _const((CONV, CONV))],
        out_specs=[pl.BlockSpec((tm, 2 * CONV), lambda i: (rev(i), 0)), acc(HALO), acc(1), acc(1), acc(1)],
        scratch_shapes=[pltpu.VMEM((tm + HALO, CONV), F32), pltpu.VMEM((SUBLANES - 1, sh_rows, CONV), F32),
                        pltpu.VMEM((HALO * SUBLANES, CONV), F32)],
        compiler_params=_params(("arbitrary",)),
    )(proj, proj, uc, dmix, conv_w, cn_g, cn_b, gmat)


def _gla_bwd_tile(q_ref, k_ref, v_ref, g_ref, z_ref, o_ref, st_ref, dm_ref, wg_ref, bg_ref, gn_ref,
                  dpg_ref, dwg_ref, dbg_ref, dgn_ref, dstate, do_scr, dv_scr, tb):
    @pl.when(pl.program_id(0) == 0)
    def _():
        dstate[...] = jnp.zeros_like(dstate)
        dwg_ref[...] = jnp.zeros_like(dwg_ref)
        dbg_ref[...] = jnp.zeros_like(dbg_ref)
        dgn_ref[...] = jnp.zeros_like(dgn_ref)

    gnv = gn_ref[...]
    dgn = jnp.zeros((1, DV), F32)
    for h in range(HEADS):
        cols = slice(h * DV, (h + 1) * DV)
        oh = o_ref[:, cols]
        r = lax.rsqrt(_rowmean(oh * oh) + EPS)
        ohat = oh * r
        gh = g_ref[:, cols].astype(F32)
        sg = jax.nn.sigmoid(gh)
        dmx = dm_ref[:, cols].astype(F32)
        don = dmx * (gh * sg)
        dpg_ref[:, P_G + h * DV:P_G + (h + 1) * DV] = (dmx * (ohat * gnv) * _silu_grad(gh, sg)).astype(BF16)
        dgn = dgn + _colsum(don * ohat)
        do_scr[:, cols] = _rms_bwd(don, ohat, r, gnv)
    dgn_ref[...] += dgn

    mats = _chunk_matrices()
    masks = _head_masks()
    wgv, bgv = wg_ref[...], bg_ref[...]
    n_chunks = SUB // CHUNK

    for sb in reversed(range(tb // SUB)):
        rows = slice(sb * SUB, (sb + 1) * SUB)
        zs = z_ref[rows, :]
        d = _decay_terms(zs, q_ref[rows, :], k_ref[rows, :], wgv, bgv, mats)
        qem = d.q * d.ebm
        qem_b = qem.astype(BF16)
        kem_b = (d.k * d.emb).astype(BF16)
        dq = jnp.zeros((SUB, KEY), F32)
        dk = jnp.zeros((SUB, KEY), F32)
        for h in range(HEADS):
            hm = masks[h]
            cols = slice(h * DV, (h + 1) * DV)
            do_b = do_scr[rows, cols].astype(BF16)
            vh = v_ref[rows, cols]
            da = jnp.where(mats.causal, _nt(do_b, vh), 0.0).astype(BF16)
            da_t = jnp.where(mats.causal_t, _nt(vh, do_b), 0.0).astype(BF16)
            a_t = jnp.where(mats.causal_t, _nt(kem_b, (qem * hm).astype(BF16)), 0.0).astype(BF16)
            dq = dq + hm * _nn(da, kem_b)
            dk = dk + hm * _nn(da_t, qem_b)
            dv_scr[rows, cols] = _nn(a_t, do_b)
        dq = dq * d.ebm
        dk = dk * d.emb

        qe0_b = (d.q * d.eb).astype(BF16)
        kdec_b = (d.k * d.elb).astype(BF16)
        dq_st, dk_st, last = [None] * n_chunks, [None] * n_chunks, [None] * n_chunks
        for c in reversed(range(n_chunks)):
            loc = slice(c * CHUNK, (c + 1) * CHUNK)
            glob = slice(sb * SUB + c * CHUNK, sb * SUB + (c + 1) * CHUNK)
            st_b = st_ref[sb * n_chunks + c]
            ds = dstate[...]
            ds_b = ds.astype(BF16)
            do_c = do_scr[glob, :].astype(BF16)
            ebl_c = d.ebl[c * CHUNK:c * CHUNK + 1]
            dk_c = _nn(v_ref[glob, :], ds_b) * d.elb[loc]
            dq_st[c] = _nn(do_c, st_b) * d.eb[loc]
            dk_st[c] = dk_c
            last_c = _colsum(d.k[loc] * dk_c) + ebl_c * _colsum(st_b.astype(F32) * ds)
            last[c] = jnp.broadcast_to(last_c, (CHUNK, KEY))
            dpg_ref[glob, P_V:P_G] = (dv_scr[glob, :] + _nt(kdec_b[loc], ds_b)).astype(BF16)
            dstate[...] = ds * ebl_c + jnp.where(mats.heads, _tn(do_c, qe0_b[loc]), 0.0)
        dq = dq + jnp.concatenate(dq_st, axis=0)
        dk = dk + jnp.concatenate(dk_st, axis=0)
        dpg_ref[rows, P_Q:P_K] = (dq * Q_SCALE).astype(BF16)
        dpg_ref[rows, P_K:P_V] = dk.astype(BF16)
        hi, lo = _split_bf16(d.q * dq - d.k * dk)
        dla = _nn(mats.tri_t, hi) + _nn(mats.tri_t, lo) + jnp.concatenate(last, axis=0)
        dal = dla * (1.0 / GATE_TAU) * jax.nn.sigmoid(-d.al)
        dal_b = dal.astype(BF16)
        dpg_ref[rows, OFF_Z:] = _nt(dal_b, wgv).astype(BF16)
        dwg_ref[...] += _tn(zs, dal_b)
        dbg_ref[...] += _colsum(dal)


def _gla_bwd(proj, o, states, dmix, wg, bg, gn, parts):
    T = proj.shape[0]
    tb = min(T, TOKEN_TILE)
    cpb = tb // CHUNK
    nb = T // tb
    n_comm = len(parts)
    kinds = ["exchange"] * n_comm
    comm_start, comm_wait = _hosted_comm(kinds, 11, 4, n_comm, nb)

    def body(*refs):
        comm_start(refs)
        _gla_bwd_tile(*refs[:11], *refs[11 + n_comm:15 + n_comm], *refs[15 + 2 * n_comm:18 + 2 * n_comm], tb)
        comm_wait(refs)

    rev = lambda i: nb - 1 - i
    blk = lambda w, col: pl.BlockSpec((tb, w), lambda i: (rev(i), col))
    res = pl.pallas_call(
        body, name="gla_bwd", grid=(nb,),
        out_shape=[jax.ShapeDtypeStruct((T, D_GLA), BF16), jax.ShapeDtypeStruct((Z_PAD, KEY), F32),
                   jax.ShapeDtypeStruct((1, KEY), F32), jax.ShapeDtypeStruct((1, DV), F32)]
        + _comm_out_shapes(kinds, parts),
        in_specs=[blk(KEY, P_Q // KEY), blk(KEY, P_K // KEY), blk(VAL, P_V // VAL), blk(VAL, P_G // VAL),
                  blk(Z_PAD, P_Z // Z_PAD), blk(VAL, 0),
                  pl.BlockSpec((cpb, VAL, KEY), lambda i: (rev(i), 0, 0)), blk(VAL, 0),
                  _const((Z_PAD, KEY)), _const((1, KEY)), _const((1, DV))] + [ANY] * n_comm,
        out_specs=[blk(D_GLA, 0),
                   pl.BlockSpec((Z_PAD, KEY), lambda i: (0, 0)), pl.BlockSpec((1, KEY), lambda i: (0, 0)),
                   pl.BlockSpec((1, DV), lambda i: (0, 0))] + [ANY] * n_comm,
        scratch_shapes=[pltpu.VMEM((VAL, KEY), F32), pltpu.VMEM((tb, VAL), F32), pltpu.VMEM((tb, VAL), F32)]
        + _comm_scratch(n_comm),
        compiler_params=_params(("arbitrary",)),
    )(proj, proj, proj, proj, proj, o, states, dmix, wg, bg, gn, *parts)
    return res[:4], res[4:]


def _inproj_bwd(x, g1, w_in_t, dh1, dp_gla, dp_conv):
    T = x.shape[0]
    tm = min(T, TOKEN_TILE)
    nt = T // tm
    streams = [x, dh1, dp_gla, dp_conv]

    def body(g_ref, w_ref, x_hbm, dh1_hbm, dpg_hbm, dpc_hbm, dx_ref, dg1_ref, x_buf, dh1_buf, dpg_buf, dpc_buf, sems):
        step = pl.program_id(0)
        hbm, bufs = [x_hbm, dh1_hbm, dpg_hbm, dpc_hbm], [x_buf, dh1_buf, dpg_buf, dpc_buf]

        def fetch(at, k):
            start = at * tm if isinstance(at, int) else pl.multiple_of(at * tm, tm)
            return pltpu.make_async_copy(hbm[k].at[pl.ds(start, tm), :], bufs[k].at[at % RING], sems.at[k, at % RING])

        @pl.when(step == 0)
        def _():
            dg1_ref[...] = jnp.zeros_like(dg1_ref)
            for at in range(min(RING - 1, nt)):
                for k in range(len(streams)):
                    fetch(at, k).start()

        @pl.when(step + (RING - 1) < nt)
        def _():
            for k in range(len(streams)):
                fetch(step + (RING - 1), k).start()

        for k in range(len(streams)):
            fetch(step, k).wait()
        slot = step % RING
        dxn = _nn(dpg_buf[slot], w_ref[0:D_GLA, :]) + _nn(dpc_buf[slot], w_ref[OFF_C:D_IN, :])
        xv = x_buf[slot]
        r = lax.rsqrt(_rowmean(xv * xv) + EPS)
        xhat = xv * r
        dg1_ref[...] += _colsum(dxn * xhat)
        dx_ref[...] = dh1_buf[slot] + _rms_bwd(dxn, xhat, r, g_ref[...])

    return pl.pallas_call(
        body, name="inproj_bwd", grid=(nt,),
        out_shape=[jax.ShapeDtypeStruct((T, D_MODEL), F32), jax.ShapeDtypeStruct((1, D_MODEL), F32)],
        in_specs=[_const((1, D_MODEL)), _const((D_IN, D_MODEL))] + [ANY] * len(streams),
        out_specs=[pl.BlockSpec((tm, D_MODEL), lambda i: (i, 0)), pl.BlockSpec((1, D_MODEL), lambda i: (0, 0))],
        scratch_shapes=[pltpu.VMEM((RING, tm, s.shape[1]), s.dtype) for s in streams]
        + [pltpu.SemaphoreType.DMA((len(streams), RING))],
        compiler_params=_params(("arbitrary",)),
    )(g1, w_in_t, x, dh1, dp_gla, dp_conv)


def _wgrad_in(xn, dp_gla, dp_conv):
    T = xn.shape[0]
    tt = min(T, WGRAD_TILE // 2)
    nt = T // tt

    def body(xn_ref, dpg_ref, dpc_ref, o_ref, acc):
        @pl.when(pl.program_id(0) == 0)
        def _():
            acc[...] = jnp.zeros_like(acc)

        xv = xn_ref[...]
        acc[0:OFF_C, :] += _tn(dpg_ref[...], xv)[0:OFF_C]
        acc[OFF_C:, :] += _tn(dpc_ref[...], xv)

        @pl.when(pl.program_id(0) == nt - 1)
        def _():
            o_ref[...] = acc[...].astype(BF16)

    tok = lambda w: pl.BlockSpec((tt, w), lambda t: (t, 0))
    return pl.pallas_call(
        body, name="wgrad_in", grid=(nt,), out_shape=jax.ShapeDtypeStruct((D_IN, D_MODEL), BF16),
        in_specs=[tok(D_MODEL), tok(D_GLA), tok(2 * CONV)],
        out_specs=pl.BlockSpec((D_IN, D_MODEL), lambda t: (0, 0), pipeline_mode=pl.Buffered(1)),
        scratch_shapes=[pltpu.VMEM((D_IN, D_MODEL), F32)],
        compiler_params=_params(("arbitrary",)),
    )(xn, dp_gla, dp_conv)


def _wgrad_out(mix_a, mix_c, dh1):
    T = dh1.shape[0]
    tt = min(T, WGRAD_TILE // 2)
    nt = T // tt

    def body(a_ref, c_ref, b_ref, o_ref, acc):
        @pl.when(pl.program_id(0) == 0)
        def _():
            acc[...] = jnp.zeros_like(acc)

        b = b_ref[...].astype(BF16)
        acc[0:VAL, :] += _tn(a_ref[...], b)
        acc[VAL:, :] += _tn(c_ref[...], b)

        @pl.when(pl.program_id(0) == nt - 1)
        def _():
            o_ref[...] = acc[...].astype(BF16)

    tok = lambda w: pl.BlockSpec((tt, w), lambda t: (t, 0))
    return pl.pallas_call(
        body, name="wgrad_out", grid=(nt,), out_shape=jax.ShapeDtypeStruct((D_MODEL, D_MODEL), BF16),
        in_specs=[tok(VAL), tok(CONV), tok(D_MODEL)],
        out_specs=pl.BlockSpec((D_MODEL, D_MODEL), lambda t: (0, 0)),
        scratch_shapes=[pltpu.VMEM((D_MODEL, D_MODEL), F32)],
        compiler_params=_params(("arbitrary",)),
    )(mix_a, mix_c, dh1)


def _wgrad(a, b, name, tk, tn, col_block=None):
    T, K = a.shape
    N = b.shape[1]
    tt = min(T, WGRAD_TILE)
    nt = T // tt

    def body(a_ref, b_ref, o_ref, acc):
        @pl.when(pl.program_id(2) == 0)
        def _():
            acc[...] = jnp.zeros_like(acc)

        acc[...] += _tn(a_ref[...], b_ref[...].astype(BF16))

        @pl.when(pl.program_id(2) == nt - 1)
        def _():
            if col_block is None:
                o_ref[...] = acc[...].astype(BF16)
            else:
                for q in range(tn // col_block):
                    o_ref[q] = acc[:, q * col_block:(q + 1) * col_block].astype(BF16)

    if col_block is None:
        out_shape = jax.ShapeDtypeStruct((K, N), BF16)
        out_spec = pl.BlockSpec((tk, tn), lambda i, j, t: (i, j))
    else:
        assert tk == K
        out_shape = jax.ShapeDtypeStruct((N // col_block, K, col_block), BF16)
        out_spec = pl.BlockSpec((tn // col_block, tk, col_block), lambda i, j, t: (j, 0, 0))
    return pl.pallas_call(
        body, name=name, grid=(K // tk, N // tn, nt), out_shape=out_shape,
        in_specs=[pl.BlockSpec((tt, tk), lambda i, j, t: (t, i)), pl.BlockSpec((tt, tn), lambda i, j, t: (t, j))],
        out_specs=out_spec, scratch_shapes=[pltpu.VMEM((tk, tn), F32)],
        compiler_params=_params(("arbitrary", "arbitrary", "arbitrary")),
    )(a, b)


def _adam_math(w, g, m, v):
    m = ADAM_B1 * m + (1.0 - ADAM_B1) * g
    v = ADAM_B2 * v + (1.0 - ADAM_B2) * (g * g)
    m_hat = m / (1.0 - ADAM_B1 ** ADAM_STEP)
    v_hat = v / (1.0 - ADAM_B2 ** ADAM_STEP)
    delta = -ADAM_LR * (m_hat / (jnp.sqrt(v_hat) + ADAM_EPS) + ADAM_WD * w)
    return delta, m, v


def _sum8(ref):
    g = ref[0].astype(F32)
    for s in range(1, N_DEV):
        g = g + ref[s].astype(F32)
    return g


def _adam_big(parts, w, m, v, name):
    R, C = w.shape
    tr = ADAM_ROWS if R % ADAM_ROWS == 0 else R

    def body(p_ref, w_ref, m_ref, v_ref, g_ref, d_ref, nm_ref, nv_ref):
        g = _sum8(p_ref)
        g_ref[...] = g
        d_ref[...], nm_ref[...], nv_ref[...] = _adam_math(w_ref[...], g, m_ref[...], v_ref[...])

    row = pl.BlockSpec((tr, C), lambda i: (i, 0))
    return pl.pallas_call(
        body, name=name, grid=(R // tr,), out_shape=[jax.ShapeDtypeStruct((R, C), F32)] * 4,
        in_specs=[pl.BlockSpec((N_DEV, tr, C), lambda i: (0, i, 0)), row, row, row], out_specs=[row] * 4,
        compiler_params=_params(("arbitrary",)),
    )(parts, w, m, v)


def _sum_small(parts):
    def body(p_ref, o_ref):
        o_ref[...] = _sum8(p_ref)

    return pl.pallas_call(body, name="sum_small", out_shape=jax.ShapeDtypeStruct(parts.shape[1:], F32))(parts)


def _adam_small(gs, ws, ms, vs):
    n = len(gs)

    def body(*refs):
        g_refs, w_refs, m_refs, v_refs = refs[:n], refs[n:2 * n], refs[2 * n:3 * n], refs[3 * n:4 * n]
        outs = refs[4 * n:]
        for i in range(n):
            d, nm, nv = _adam_math(w_refs[i][...], g_refs[i][...], m_refs[i][...], v_refs[i][...])
            outs[i][...] = d
            outs[n + i][...] = nm
            outs[2 * n + i][...] = nv

    shapes = [jax.ShapeDtypeStruct(w.shape, F32) for w in ws]
    res = pl.pallas_call(body, name="adam_small", out_shape=shapes * 3)(*gs, *ws, *ms, *vs)
    return res[:n], res[n:2 * n], res[2 * n:]


def _group_matrix():
    gi = lax.broadcasted_iota(jnp.int32, (CONV, CONV), 0) // (CONV // GROUPS)
    gj = lax.broadcasted_iota(jnp.int32, (CONV, CONV), 1) // (CONV // GROUPS)
    return jnp.where(gi == gj, GROUPS / CONV, 0.0).astype(BF16)


_SMALL = [("loss", 8), ("dg1", 8), ("dbg", 2), ("dgn", 1), ("dconv_b", 4), ("dcn_g", 4), ("dcn_b", 4), ("dg2", 8),
          ("dgf", 8), ("dwg", 32), ("dconv_w", 124)]


def _pad8(rows):
    return -(-rows // 8) * 8


def kernel(x, norm1_g, w_in, w_gate_up, b_gate, gla_norm_g, conv_w, conv_b, conv_norm_g, conv_norm_b, w_out, norm2_g, w_mlp_in, w_mlp_out, final_norm_g, loss_target, m_norm1_g, m_w_in, m_w_gate_up, m_b_gate, m_gla_norm_g, m_conv_w, m_conv_b, m_conv_norm_g, m_conv_norm_b, m_w_out, m_norm2_g, m_w_mlp_in, m_w_mlp_out, m_final_norm_g, v_norm1_g, v_w_in, v_w_gate_up, v_b_gate, v_gla_norm_g, v_conv_w, v_conv_b, v_conv_norm_g, v_conv_norm_b, v_w_out, v_norm2_g, v_w_mlp_in, v_w_mlp_out, v_final_norm_g):
    x_idx = lax.axis_index("x")
    y_idx = lax.axis_index("y")
    c_idx = lax.axis_index("c")
    me = 4 * x_idx + 2 * y_idx + c_idx
    xs, tgt = x[0], loss_target[0]
    gf = final_norm_g.reshape(1, D_MODEL)
    gmat = _group_matrix()

    small_shard = jnp.zeros((_pad8(RANK + CONV_W), LANES), F32)
    small_shard = small_shard.at[0:RANK, 0:KEY // N_DEV].set(w_gate_up[0])
    small_shard = small_shard.at[RANK:RANK + CONV_W, 0:CONV // N_DEV].set(conv_w[0])
    g_in, g_small = _gather_two_level([w_in[0].T.astype(BF16), small_shard], "gather_w_in")
    w_in_t = g_in.reshape(D_IN, D_MODEL)
    wg_full = jnp.concatenate([g_small[d, 0:RANK, 0:KEY // N_DEV] for d in range(N_DEV)], axis=1)
    wg_pad = jnp.pad(wg_full, ((0, Z_PAD - RANK), (0, 0))).astype(BF16)
    conv_w_full = jnp.concatenate([g_small[d, RANK:RANK + CONV_W, 0:CONV // N_DEV] for d in range(N_DEV)], axis=1)
    conv_w_pad = jnp.pad(conv_w_full, ((0, HALO - CONV_W), (0, 0)))

    proj, xn, (g_w2,) = _inproj_fwd(xs, norm1_g, w_in_t, [w_mlp_out[0].astype(BF16)])
    mix_a, o, states, mix_c, uc, (g_out, g_w1) = _mix_fwd(
        proj, wg_pad, b_gate, gla_norm_g, conv_w_pad, conv_b, conv_norm_g, conv_norm_b, gmat,
        [w_out[0].astype(BF16), w_mlp_in[0].T.astype(BF16)])
    w_out_full = g_out.reshape(D_MODEL, D_MODEL)
    w1t_full = g_w1.reshape(D_FF, D_MODEL)
    w2_full = g_w2.reshape(D_FF, D_MODEL)
    dh1, dmix, hn, ff, da, dh2, loss, dgf, dg2 = _mlp_fwd_bwd(xs, mix_a, mix_c, tgt, w_out_full, norm2_g, w1t_full,
                                                              w2_full, gf)

    dw1 = _wgrad(hn, da, "wgrad_mlp_in", WGRAD_BLOCK, WGRAD_BLOCK, col_block=D_FF // N_DEV)
    dw2 = _wgrad(ff, dh2, "wgrad_mlp_out", WGRAD_BLOCK, WGRAD_BLOCK)
    dw_out = _wgrad_out(mix_a, mix_c, dh1)
    dp_conv, dconv_w, dconv_b, dcn_g, dcn_b = _conv_bwd(proj, uc, dmix, conv_w_pad, conv_norm_g, conv_norm_b, gmat)
    (dp_gla, dwg, dbg, dgn), (p_w1, p_w2, p_out) = _gla_bwd(
        proj, o, states, dmix, wg_pad, b_gate, gla_norm_g,
        [dw1, dw2.reshape(N_DEV, D_FF // N_DEV, D_MODEL), dw_out.reshape(N_DEV, D_MODEL // N_DEV, D_MODEL)])
    dw_in = _wgrad_in(xn, dp_gla, dp_conv).reshape(N_DEV, SHARD_IN, D_MODEL)
    send_sems, recv_sems, dw_in_thru, land, token = _split_start("exchange", dw_in, jnp.copy(dw_in),
                                                                 "exchange_w_in_start")
    dx, dg1 = _inproj_bwd(xs, norm1_g + token[0:1, 0:1], w_in_t, dh1, dp_gla, dp_conv)
    p_in = _split_wait("exchange", send_sems, recv_sems, dw_in_thru, land, dg1, "exchange_w_in_wait")

    small = dict(loss=jnp.zeros((SUBLANES, LANES), F32) + loss, dg1=dg1, dbg=dbg, dgn=dgn, dconv_b=dconv_b, dcn_g=dcn_g,
                 dcn_b=dcn_b, dg2=dg2, dgf=dgf, dwg=dwg[0:RANK], dconv_w=dconv_w[0:CONV_W])
    pack = jnp.concatenate([jnp.pad(small[name].reshape(rows, LANES), ((0, _pad8(rows) - rows), (0, 0)))
                            for name, rows in _SMALL], axis=0)
    s_send, s_recv, pack_thru, pack_land, s_token = _split_start(
        "gather", pack, jnp.broadcast_to(pack, (N_DEV,) + pack.shape) + 0.0, "gather_small_start")

    gi, di, mi, vi = _adam_big(p_in, w_in[0].T, m_w_in[0].T, v_w_in[0].T, "adam_w_in")
    go, do, mo, vo = _adam_big(p_out, w_out[0] + s_token[0:1, 0:1], m_w_out[0], v_w_out[0], "adam_w_out")
    ga, da_, ma, va = _adam_big(p_w1, w_mlp_in[0], m_w_mlp_in[0], v_w_mlp_in[0], "adam_w_mlp_in")
    gb, db, mb, vb = _adam_big(p_w2, w_mlp_out[0], m_w_mlp_out[0], v_w_mlp_out[0], "adam_w_mlp_out")
    cut = lambda a: a.T[None]

    g_pack = _split_wait("gather", s_send, s_recv, pack_thru, pack_land, go[0:8, 0:128] + ga[0:8, 0:128]
                         + gb[0:8, 0:128], "gather_small_wait")
    summed = _sum_small(g_pack)
    small_g = {}
    at = 0
    for name, rows in _SMALL:
        small_g[name] = summed[at:at + rows]
        at += _pad8(rows)
    loss_out = small_g["loss"][0, 0]
    wg_cols = KEY // N_DEV
    cw_cols = CONV // N_DEV
    g_small_list = [
        small_g["dg1"].reshape(1, D_MODEL),
        lax.dynamic_slice_in_dim(small_g["dwg"].reshape(RANK, KEY), me * wg_cols, wg_cols, axis=1)[None],
        small_g["dbg"].reshape(1, KEY), small_g["dgn"].reshape(1, DV),
        lax.dynamic_slice_in_dim(small_g["dconv_w"].reshape(CONV_W, CONV), me * cw_cols, cw_cols, axis=1)[None],
        small_g["dconv_b"].reshape(1, CONV), small_g["dcn_g"].reshape(1, CONV), small_g["dcn_b"].reshape(1, CONV),
        small_g["dg2"].reshape(1, D_MODEL), small_g["dgf"].reshape(1, D_MODEL),
    ]
    row = lambda a: a.reshape(1, D_MODEL)
    w_small = [norm1_g, w_gate_up, b_gate, gla_norm_g, conv_w, conv_b, conv_norm_g, conv_norm_b, norm2_g,
               row(final_norm_g)]
    m_small = [m_norm1_g, m_w_gate_up, m_b_gate, m_gla_norm_g, m_conv_w, m_conv_b, m_conv_norm_g, m_conv_norm_b,
               m_norm2_g, row(m_final_norm_g)]
    v_small = [v_norm1_g, v_w_gate_up, v_b_gate, v_gla_norm_g, v_conv_w, v_conv_b, v_conv_norm_g, v_conv_norm_b,
               v_norm2_g, row(v_final_norm_g)]
    d_small, nm_small, nv_small = _adam_small(g_small_list, w_small, m_small, v_small)
    flat = lambda lst: list(lst[:-1]) + [lst[-1].reshape(D_MODEL)]
    g_small_list, d_small, nm_small, nv_small = flat(g_small_list), flat(d_small), flat(nm_small), flat(nv_small)

    def order(s, w_in_v, w_out_v, w1_v, w2_v):
        return [s[0], w_in_v, s[1], s[2], s[3], s[4], s[5], s[6], s[7], w_out_v, s[8], w1_v, w2_v, s[9]]

    grads = order(g_small_list, cut(gi), go[None], ga[None], gb[None])
    deltas = order(d_small, cut(di), do[None], da_[None], db[None])
    new_m = order(nm_small, cut(mi), mo[None], ma[None], mb[None])
    new_v = order(nv_small, cut(vi), vo[None], va[None], vb[None])
    return (loss_out, dx[None], *grads, *deltas, *new_m, *new_v)
```

```python
from typing import NamedTuple

import jax
import jax.numpy as jnp
from jax import lax
from jax.experimental import pallas as pl
from jax.experimental.pallas import tpu as pltpu

F32 = jnp.float32
BF16 = jnp.bfloat16

N_DEV = 8
D_MODEL = 1024
HEADS = 4
DK = 64
DV = 128
KEY = HEADS * DK
VAL = HEADS * DV
RANK = 16
CONV = 512
GROUPS = 8
CONV_W = 31
HALO = 32
SUBLANES = 8
LANES = 128
STRIP = 32
FWD_STRIP = 16
TOKEN_TILE = 512
MLP_TILE = 256
WGRAD_TILE = 4096
WGRAD_BLOCK = 1024
ADAM_ROWS = 128
RING = 3
D_FF = 4096
D_IN = 2576
SHARD_IN = D_IN // N_DEV
CHUNK = 64
SUB = 256
EPS = 1e-6
GATE_TAU = 16.0
Q_SCALE = DK ** -0.5

P_Q, P_K, P_V, P_G, P_CI, P_CG, P_Z = 0, 256, 512, 1024, 1536, 2048, 2560
D_INP = 2688
Z_PAD = D_INP - P_Z
OFF_Z = 1536
OFF_C = OFF_Z + RANK
D_GLA = OFF_Z + Z_PAD

ADAM_LR = 0.001
ADAM_B1 = 0.9
ADAM_B2 = 0.999
ADAM_EPS = 1e-08
ADAM_WD = 0.01
ADAM_STEP = 10

V7X_VMEM_BYTES = 64 * 1024 * 1024
VMEM_LIMIT = V7X_VMEM_BYTES * 7 // 8

MESH = pl.DeviceIdType.MESH
ANY = pl.BlockSpec(memory_space=pl.ANY)


def _nn(a, b):
    return jnp.dot(a, b, preferred_element_type=F32)


def _nt(a, b):
    return lax.dot_general(a, b, (((1,), (1,)), ((), ())), preferred_element_type=F32)


def _tn(a, b):
    return lax.dot_general(a, b, (((0,), (0,)), ((), ())), preferred_element_type=F32)


def _params(sem=None):
    return pltpu.CompilerParams(dimension_semantics=sem, vmem_limit_bytes=VMEM_LIMIT)


def _const(shape):
    return pl.BlockSpec(shape, lambda *_: (0,) * len(shape), pipeline_mode=pl.Buffered(1))


def _colsum(v):
    return jnp.sum(v, axis=0, keepdims=True)


def _rowmean(v):
    return jnp.mean(v, axis=-1, keepdims=True)


def _split_bf16(v):
    hi = v.astype(BF16)
    return hi, (v - hi.astype(F32)).astype(BF16)


def _my_place():
    return lax.axis_index("x"), lax.axis_index("y"), lax.axis_index("c")


def _peer(j):
    x, y, c = _my_place()
    jx, jy, jc = (j >> 2) & 1, (j >> 1) & 1, j & 1
    px = 1 - x if jx else x
    py = 1 - y if jy else y
    pc = 1 - c if jc else c
    return (px, py, pc), 4 * px + 2 * py + pc


def _comm_plan(kinds, ins, outs, send_sems, recv_sems, local_sems, receives=True):
    x, y, c = _my_place()
    me = 4 * x + 2 * y + c
    own = lambda k, idx: ins[k] if kinds[k] == "gather" else ins[k].at[idx]
    local = [pltpu.make_async_copy(own(k, me), outs[k].at[me], local_sems.at[k]) for k in range(len(kinds))]
    sends, recvs = [], []
    for j in range(1, N_DEV):
        peer, peer_idx = _peer(j)
        for k in range(len(kinds)):
            sems = dict(send_sem=send_sems.at[k, j - 1], recv_sem=recv_sems.at[k, j - 1], device_id=peer,
                        device_id_type=MESH)
            sends.append(pltpu.make_async_remote_copy(src_ref=own(k, peer_idx), dst_ref=outs[k].at[me], **sems))
            if receives:
                recvs.append(pltpu.make_async_remote_copy(src_ref=own(k, me), dst_ref=outs[k].at[peer_idx], **sems))
    return local, sends, recvs


def _comm_start(plan):
    local, sends, _ = plan
    for cp in local + sends:
        cp.start()


def _comm_wait(plan):
    local, sends, recvs = plan
    for cp in recvs:
        cp.wait_recv()
    for cp in sends:
        cp.wait_send()
    for cp in local:
        cp.wait()


def _comm_scratch(n):
    return [pltpu.SemaphoreType.DMA((n, N_DEV - 1)), pltpu.SemaphoreType.DMA((n, N_DEV - 1)),
            pltpu.SemaphoreType.DMA((n,))]


def _comm_out_shapes(kinds, arrays):
    return [jax.ShapeDtypeStruct(((N_DEV,) + a.shape) if kind == "gather" else a.shape, a.dtype)
            for kind, a in zip(kinds, arrays)]


def _gather_two_level(shards, name):
    n = len(shards)

    def body(*refs):
        ins, outs = refs[:n], refs[n:2 * n]
        send_sems, recv_sems, local_sems = refs[2 * n:]
        x, y, c = _my_place()
        index = lambda px, py, pc: 4 * px + 2 * py + pc
        me, sibling = (x, y, c), (x, y, 1 - c)
        chips = [(1 - x, y), (x, 1 - y), (1 - x, 1 - y)]

        def copy(k, slot, block, to, src=None):
            rows = outs[k].at[index(*block)]
            return pltpu.make_async_remote_copy(
                src_ref=rows if src is None else src, dst_ref=rows, send_sem=send_sems.at[k, slot],
                recv_sem=recv_sems.at[k, slot], device_id=to, device_id_type=MESH)

        local = [pltpu.make_async_copy(ins[k], outs[k].at[index(*me)], local_sems.at[k]) for k in range(n)]
        first = []
        for k in range(n):
            first.append(copy(k, 0, me, sibling, src=ins[k]))
            first += [copy(k, 1 + j, me, (*chip, c), src=ins[k]) for j, chip in enumerate(chips)]
        for cp in local + first:
            cp.start()
        passed = []
        for j, chip in enumerate(chips):
            for k in range(n):
                copy(k, 1 + j, (*chip, c), me).wait_recv()
                cp = copy(k, 4 + j, (*chip, c), sibling)
                cp.start()
                passed.append(cp)
        for k in range(n):
            copy(k, 0, sibling, me).wait_recv()
        for j, chip in enumerate(chips):
            for k in range(n):
                copy(k, 4 + j, (*chip, 1 - c), me).wait_recv()
        for cp in first + passed:
            cp.wait_send()
        for cp in local:
            cp.wait()

    return pl.pallas_call(
        body, name=name, out_shape=_comm_out_shapes(["gather"] * n, shards), in_specs=[ANY] * n, out_specs=[ANY] * n,
        scratch_shapes=_comm_scratch(n),
    )(*shards)


HBM = pl.BlockSpec(memory_space=pltpu.HBM)
SEM = pl.BlockSpec(memory_space=pltpu.SEMAPHORE)
DATAFLOW = pltpu.SideEffectType.DATAFLOW_SIDE_EFFECTING


def _split_start(kind, part, land, name):
    def body(src_ref, land_ref, send_sems, recv_sems, src_thru, land_thru, token):
        x, y, c = _my_place()
        me = 4 * x + 2 * y + c
        for j in range(1, N_DEV):
            peer, peer_idx = _peer(j)
            pltpu.make_async_remote_copy(
                src_ref=src_ref.at[peer_idx] if kind == "exchange" else src_ref, dst_ref=land_ref.at[me],
                send_sem=send_sems.at[j - 1], recv_sem=recv_sems.at[j - 1], device_id=peer,
                device_id_type=MESH).start()
        token[...] = jnp.zeros_like(token)

    return pl.pallas_call(
        body, name=name,
        out_shape=(pltpu.SemaphoreType.DMA((N_DEV - 1,)), pltpu.SemaphoreType.DMA((N_DEV - 1,)),
                   pltpu.HBM(part.shape, part.dtype), pltpu.HBM(land.shape, land.dtype),
                   jax.ShapeDtypeStruct((SUBLANES, LANES), F32)),
        in_specs=(HBM, HBM), out_specs=(SEM, SEM, HBM, HBM, pl.BlockSpec(memory_space=pltpu.VMEM)),
        input_output_aliases={0: 2, 1: 3},
        compiler_params=pltpu.CompilerParams(has_side_effects=DATAFLOW),
    )(pltpu.with_memory_space_constraint(part, pltpu.HBM), pltpu.with_memory_space_constraint(land, pltpu.HBM))


def _split_wait(kind, send_sems, recv_sems, part_thru, land_thru, after, name):
    def body(src_ref, land_ref, send_sems, recv_sems, after_ref, src_dead, got_ref):
        x, y, c = _my_place()
        me = 4 * x + 2 * y + c
        own = lambda idx: src_ref.at[idx] if kind == "exchange" else src_ref
        for j in range(1, N_DEV):
            peer, peer_idx = _peer(j)
            sems = dict(send_sem=send_sems.at[j - 1], recv_sem=recv_sems.at[j - 1], device_id=peer,
                        device_id_type=MESH)
            pltpu.make_async_remote_copy(src_ref=own(peer_idx), dst_ref=land_ref.at[me], **sems).wait_send()
            pltpu.make_async_remote_copy(src_ref=own(me), dst_ref=land_ref.at[peer_idx], **sems).wait_recv()

    return pl.pallas_call(
        body, name=name,
        out_shape=(pltpu.HBM(part_thru.shape, part_thru.dtype), pltpu.HBM(land_thru.shape, land_thru.dtype)),
        in_specs=(HBM, HBM, SEM, SEM, ANY), out_specs=(HBM, HBM), input_output_aliases={0: 0, 1: 1},
        compiler_params=pltpu.CompilerParams(has_side_effects=DATAFLOW),
    )(part_thru, land_thru, send_sems, recv_sems, after)[1]


def _hosted_comm(kinds, n_in, n_out, n_comm, n_steps):
    def plan_of(refs, receives):
        ins = refs[n_in:n_in + n_comm]
        outs = refs[n_in + n_comm + n_out:n_in + 2 * n_comm + n_out]
        return _comm_plan(kinds, ins, outs, *refs[-3:], receives=receives)

    def start(refs):
        @pl.when(pl.program_id(0) == 0)
        def _():
            _comm_start(plan_of(refs, False))

    def wait(refs):
        @pl.when(pl.program_id(0) == n_steps - 1)
        def _():
            _comm_wait(plan_of(refs, True))

    return start, wait


def _z_lanes():
    return lax.broadcasted_iota(jnp.int32, (1, Z_PAD), 1) < RANK


def _inproj_fwd(x, g1, w_in_t):
    T = x.shape[0]
    tm = min(T, TOKEN_TILE)

    def body(x_ref, g_ref, w_ref, proj_ref, xn_ref):
        xv = x_ref[...]
        r = lax.rsqrt(_rowmean(xv * xv) + EPS)
        xn = (xv * r * g_ref[...]).astype(BF16)
        xn_ref[...] = xn
        proj_ref[:, 0:P_CI] = _nt(xn, w_ref[0:OFF_Z, :]).astype(BF16)
        proj_ref[:, P_CI:P_Z] = _nt(xn, w_ref[OFF_C:D_IN, :]).astype(BF16)
        proj_ref[:, P_Z:] = jnp.where(_z_lanes(), _nt(xn, w_ref[OFF_Z:OFF_Z + Z_PAD, :]), 0.0).astype(BF16)

    return pl.pallas_call(
        body, name="inproj_fwd", grid=(T // tm,),
        out_shape=[jax.ShapeDtypeStruct((T, D_INP), BF16), jax.ShapeDtypeStruct((T, D_MODEL), BF16)],
        in_specs=[pl.BlockSpec((tm, D_MODEL), lambda i: (i, 0)), _const((1, D_MODEL)), _const((D_IN, D_MODEL))],
        out_specs=[pl.BlockSpec((tm, D_INP), lambda i: (i, 0)), pl.BlockSpec((tm, D_MODEL), lambda i: (i, 0))],
        compiler_params=_params(("arbitrary",)),
    )(x, g1, w_in_t)


def _head_masks():
    lane = lax.broadcasted_iota(jnp.int32, (1, KEY), 1)
    return [((lane >= h * DK) & (lane < (h + 1) * DK)).astype(F32) for h in range(HEADS)]


class _Mats(NamedTuple):
    tri: jax.Array
    tri_t: jax.Array
    same: jax.Array
    mid: jax.Array
    causal: jax.Array
    causal_t: jax.Array
    heads: jax.Array


def _chunk_matrices():
    r = lax.broadcasted_iota(jnp.int32, (SUB, SUB), 0)
    c = lax.broadcasted_iota(jnp.int32, (SUB, SUB), 1)
    shift = CHUNK.bit_length() - 1
    same = jnp.right_shift(r, shift) == jnp.right_shift(c, shift)
    causal = same & (r >= c)
    causal_t = same & (r <= c)
    mid = same & ((c & (CHUNK - 1)) < CHUNK // 2)
    hr = jnp.right_shift(lax.broadcasted_iota(jnp.int32, (VAL, KEY), 0), DV.bit_length() - 1)
    hc = jnp.right_shift(lax.broadcasted_iota(jnp.int32, (VAL, KEY), 1), DK.bit_length() - 1)
    return _Mats(tri=causal.astype(BF16), tri_t=causal_t.astype(BF16), same=same.astype(BF16), mid=mid.astype(BF16),
                 causal=causal, causal_t=causal_t, heads=hr == hc)


class _Decay(NamedTuple):
    al: jax.Array
    q: jax.Array
    k: jax.Array
    eb: jax.Array
    ebm: jax.Array
    emb: jax.Array
    elb: jax.Array
    ebl: jax.Array


def _decay_terms(z, q, k, wg, bg, mats):
    al = _nn(z, wg) + bg
    la = (jnp.minimum(al, 0.0) - jnp.log(1.0 + jnp.exp(-jnp.abs(al)))) * (1.0 / GATE_TAU)
    hi, lo = _split_bf16(la)
    cum = lambda m: _nn(m, hi) + _nn(m, lo)
    b, b_last, b_mid = cum(mats.tri), cum(mats.same), cum(mats.mid)
    return _Decay(al=al, q=q.astype(F32) * Q_SCALE, k=k.astype(F32), eb=jnp.exp(b), ebm=jnp.exp(b - b_mid),
                  emb=jnp.exp(b_mid - b), elb=jnp.exp(b_last - b), ebl=jnp.exp(b_last))


def _gla_fwd_tile(q_ref, k_ref, v_ref, g_ref, z_ref, wg_ref, bg_ref, gn_ref, mix_ref, o_ref, st_ref, state, tb):
    @pl.when(pl.program_id(0) == 0)
    def _():
        state[...] = jnp.zeros_like(state)

    mats = _chunk_matrices()
    masks = _head_masks()
    wgv, bgv = wg_ref[...], bg_ref[...]

    for sb in range(tb // SUB):
        rows = slice(sb * SUB, (sb + 1) * SUB)
        d = _decay_terms(z_ref[rows, :], q_ref[rows, :], k_ref[rows, :], wgv, bgv, mats)
        kem_b = (d.k * d.emb).astype(BF16)
        qem = d.q * d.ebm
        for h in range(HEADS):
            cols = slice(h * DV, (h + 1) * DV)
            a = jnp.where(mats.causal, _nt((qem * masks[h]).astype(BF16), kem_b), 0.0)
            o_ref[rows, cols] = _nn(a.astype(BF16), v_ref[rows, cols])
        qe0_b = (d.q * d.eb).astype(BF16)
        kdec_b = (d.k * d.elb).astype(BF16)
        for c in range(SUB // CHUNK):
            loc = slice(c * CHUNK, (c + 1) * CHUNK)
            glob = slice(sb * SUB + c * CHUNK, sb * SUB + (c + 1) * CHUNK)
            st = state[...]
            st_b = st.astype(BF16)
            st_ref[sb * (SUB // CHUNK) + c] = st_b
            o_ref[glob, :] += _nt(qe0_b[loc], st_b)
            u = _tn(v_ref[glob, :], kdec_b[loc])
            state[...] = st * d.ebl[c * CHUNK:c * CHUNK + 1] + jnp.where(mats.heads, u, 0.0)

    gnv = gn_ref[...]
    for h in range(HEADS):
        cols = slice(h * DV, (h + 1) * DV)
        oh = o_ref[:, cols]
        r = lax.rsqrt(_rowmean(oh * oh) + EPS)
        gh = g_ref[:, cols].astype(F32)
        mix_ref[:, cols] = (oh * r * gnv * (gh * jax.nn.sigmoid(gh))).astype(BF16)


def _group_mean(v, gmat):
    return _nn(v.astype(BF16), gmat)


def _shifted_copies(buf, sh, rows):
    for k in range(1, SUBLANES):
        sh[k - 1] = buf[pl.ds(k, rows), :]


def _tap(buf, sh, off, r0, n):
    k, base = off % SUBLANES, off - off % SUBLANES
    rows = pl.ds(r0 + base if isinstance(r0, int) else pl.multiple_of(r0 + base, SUBLANES), n)
    return buf[rows, :] if k == 0 else sh[k - 1, rows, :]


def _conv_fwd_tile(ci_ref, cg_ref, w_ref, b_ref, g_ref, be_ref, gm_ref, mix_ref, uc_ref, ubuf, ush, tm):
    sh_rows = tm + HALO - SUBLANES

    @pl.when(pl.program_id(0) == 0)
    def _():
        ubuf[0:HALO, :] = jnp.zeros((HALO, CONV), F32)

    ubuf[HALO:, :] = ci_ref[...].astype(F32) * jax.nn.sigmoid(cg_ref[...].astype(F32))
    _shifted_copies(ubuf, ush, sh_rows)
    for s in range(tm // FWD_STRIP):
        acc = jnp.zeros((FWD_STRIP, CONV), F32) + b_ref[...]
        for j in range(CONV_W):
            acc = acc + w_ref[j:j + 1, :] * _tap(ubuf, ush, HALO - (CONV_W - 1) + j, s * FWD_STRIP, FWD_STRIP)
        uc_ref[s * FWD_STRIP:(s + 1) * FWD_STRIP, :] = acc
    ubuf[0:HALO, :] = ubuf[tm:tm + HALO, :]
    gm = gm_ref[...]
    ucv = uc_ref[...]
    d = ucv - _group_mean(ucv, gm)
    var = _group_mean(d * d, gm)
    yn = d * lax.rsqrt(var + EPS) * g_ref[...] + be_ref[...]
    mix_ref[...] = (yn * jax.nn.sigmoid(yn)).astype(BF16)


def _mix_fwd(proj, wg, bg, gn, conv_w, conv_b, cn_g, cn_b, gmat, shards):
    T = proj.shape[0]
    tb = min(T, TOKEN_TILE)
    cpb = tb // CHUNK
    n_comm = len(shards)
    kinds = ["gather"] * n_comm
    comm_start, comm_wait = _hosted_comm(kinds, 15, 5, n_comm, T // tb)

    def body(*refs):
        gla_in, conv_in = refs[:8], refs[8:15]
        gla_out, conv_out = refs[15 + n_comm:18 + n_comm], refs[18 + n_comm:20 + n_comm]
        state, ubuf, ush = refs[20 + 2 * n_comm:23 + 2 * n_comm]
        comm_start(refs)
        _gla_fwd_tile(*gla_in, *gla_out, state, tb)
        _conv_fwd_tile(*conv_in, *conv_out, ubuf, ush, tb)
        comm_wait(refs)

    nc = T // CHUNK
    tok = lambda w, col: pl.BlockSpec((tb, w), lambda i: (i, col))
    res = pl.pallas_call(
        body, name="mix_fwd", grid=(T // tb,),
        out_shape=[jax.ShapeDtypeStruct((T, VAL), BF16), jax.ShapeDtypeStruct((T, VAL), F32),
                   jax.ShapeDtypeStruct((nc, VAL, KEY), BF16), jax.ShapeDtypeStruct((T, CONV), BF16),
                   jax.ShapeDtypeStruct((T, CONV), F32)] + _comm_out_shapes(kinds, shards),
        in_specs=[tok(KEY, P_Q // KEY), tok(KEY, P_K // KEY), tok(VAL, P_V // VAL), tok(VAL, P_G // VAL),
                  tok(Z_PAD, P_Z // Z_PAD), _const((Z_PAD, KEY)), _const((1, KEY)), _const((1, DV)),
                  tok(CONV, P_CI // CONV), tok(CONV, P_CG // CONV), _const((HALO, CONV)), _const((1, CONV)),
                  _const((1, CONV)), _const((1, CONV)), _const((CONV, CONV))] + [ANY] * n_comm,
        out_specs=[tok(VAL, 0), tok(VAL, 0), pl.BlockSpec((cpb, VAL, KEY), lambda i: (i, 0, 0)), tok(CONV, 0),
                   tok(CONV, 0)] + [ANY] * n_comm,
        scratch_shapes=[pltpu.VMEM((VAL, KEY), F32), pltpu.VMEM((tb + HALO, CONV), F32),
                        pltpu.VMEM((SUBLANES - 1, tb + HALO - SUBLANES, CONV), F32)] + _comm_scratch(n_comm),
        compiler_params=_params(("arbitrary",)),
    )(proj, proj, proj, proj, proj, wg, bg, gn, proj, proj, conv_w, conv_b, cn_g, cn_b, gmat, *shards)
    return res[0], res[1], res[2], res[3], res[4], res[5:]


def _rms_bwd(dy, xhat, r, g):
    dyg = dy * g
    return r * (dyg - xhat * _rowmean(dyg * xhat))


def _mlp_fwd_bwd(x, mix_a, mix_c, tgt, w_out, g2, w1t, w2, gf):
    T = x.shape[0]
    tm = min(T, MLP_TILE)
    inv_d = 1.0 / D_MODEL

    def body(x_ref, ma_ref, mc_ref, t_ref, wo_ref, g2_ref, w1_ref, w2_ref, gf_ref,
             dh1_ref, dmix_ref, hn_ref, ff_ref, da_ref, dh2_ref, loss_ref, dgf_ref, dg2_ref):
        @pl.when(pl.program_id(0) == 0)
        def _():
            loss_ref[...] = jnp.zeros_like(loss_ref)
            dgf_ref[...] = jnp.zeros_like(dgf_ref)
            dg2_ref[...] = jnp.zeros_like(dg2_ref)

        g2v, gfv = g2_ref[...], gf_ref[...]
        h1 = x_ref[...] + _nn(ma_ref[...], wo_ref[0:VAL, :]) + _nn(mc_ref[...], wo_ref[VAL:, :])
        r2 = lax.rsqrt(_rowmean(h1 * h1) + EPS)
        h1hat = h1 * r2
        hn = (h1hat * g2v).astype(BF16)
        hn_ref[...] = hn
        relu_a = jnp.maximum(_nt(hn, w1_ref[...]), 0.0)
        ff = (relu_a * relu_a).astype(BF16)
        ff_ref[...] = ff
        h2 = h1 + _nn(ff, w2_ref[...])
        rf = lax.rsqrt(_rowmean(h2 * h2) + EPS)
        h2hat = h2 * rf
        err = h2hat * gfv - t_ref[...]
        loss_ref[...] += (0.5 * inv_d) * _colsum(jnp.sum(err * err, axis=1, keepdims=True))
        dy = err * inv_d
        dgf_ref[...] += _colsum(dy * h2hat)
        dh2 = _rms_bwd(dy, h2hat, rf, gfv)
        dh2_b = dh2.astype(BF16)
        dh2_ref[...] = dh2_b
        da = (_nt(dh2_b, w2_ref[...]) * (2.0 * relu_a)).astype(BF16)
        da_ref[...] = da
        dhn = _nn(da, w1_ref[...])
        dg2_ref[...] += _colsum(dhn * h1hat)
        dh1 = dh2 + _rms_bwd(dhn, h1hat, r2, g2v)
        dh1_ref[...] = dh1
        dmix_ref[...] = _nt(dh1.astype(BF16), wo_ref[...]).astype(BF16)

    tok = lambda w: pl.BlockSpec((tm, w), lambda i: (i, 0))
    return pl.pallas_call(
        body, name="mlp_fwd_bwd", grid=(T // tm,),
        out_shape=[jax.ShapeDtypeStruct((T, D_MODEL), F32), jax.ShapeDtypeStruct((T, D_MODEL), BF16),
                   jax.ShapeDtypeStruct((T, D_MODEL), BF16), jax.ShapeDtypeStruct((T, D_FF), BF16),
                   jax.ShapeDtypeStruct((T, D_FF), BF16), jax.ShapeDtypeStruct((T, D_MODEL), BF16),
                   jax.ShapeDtypeStruct((1, 1), F32), jax.ShapeDtypeStruct((1, D_MODEL), F32),
                   jax.ShapeDtypeStruct((1, D_MODEL), F32)],
        in_specs=[tok(D_MODEL), tok(VAL), tok(CONV), tok(D_MODEL), _const((D_MODEL, D_MODEL)), _const((1, D_MODEL)),
                  _const((D_FF, D_MODEL)), _const((D_FF, D_MODEL)), _const((1, D_MODEL))],
        out_specs=[tok(D_MODEL), tok(D_MODEL), tok(D_MODEL), tok(D_FF), tok(D_FF), tok(D_MODEL),
                   pl.BlockSpec((1, 1), lambda i: (0, 0)), pl.BlockSpec((1, D_MODEL), lambda i: (0, 0)),
                   pl.BlockSpec((1, D_MODEL), lambda i: (0, 0))],
        compiler_params=_params(("arbitrary",)),
    )(x, mix_a, mix_c, tgt, w_out, g2, w1t, w2, gf)


def _silu_grad(v, s):
    return s * (1.0 + v * (1.0 - s))


def _conv_bwd_tile(ci_ref, cg_ref, uc_ref, dm_ref, w_ref, g_ref, be_ref, gm_ref,
                   dpc_ref, dw_ref, db_ref, dg_ref, dbe_ref, dbuf, dsh, dwacc, tm, nt):
    step = pl.program_id(0)
    sh_rows = tm + HALO - SUBLANES

    @pl.when(step == 0)
    def _():
        dbuf[tm:, :] = jnp.zeros((HALO, CONV), F32)
        dwacc[...] = jnp.zeros_like(dwacc)
        db_ref[...] = jnp.zeros_like(db_ref)
        dg_ref[...] = jnp.zeros_like(dg_ref)
        dbe_ref[...] = jnp.zeros_like(dbe_ref)

    gm, gv = gm_ref[...], g_ref[...]
    ucv = uc_ref[...]
    d = ucv - _group_mean(ucv, gm)
    rs = lax.rsqrt(_group_mean(d * d, gm) + EPS)
    yhat = d * rs
    yn = yhat * gv + be_ref[...]
    dyn = dm_ref[...].astype(F32) * _silu_grad(yn, jax.nn.sigmoid(yn))
    dg_ref[...] += _colsum(dyn * yhat)
    dbe_ref[...] += _colsum(dyn)
    dyh = dyn * gv
    duc = rs * (dyh - _group_mean(dyh, gm) - yhat * _group_mean(dyh * yhat, gm))
    db_ref[...] += _colsum(duc)
    dbuf[0:tm, :] = duc
    _shifted_copies(dbuf, dsh, sh_rows)

    def strip(s, carry):
        r0 = pl.multiple_of(s * STRIP, STRIP)
        rows = pl.ds(r0, STRIP)
        cin = ci_ref[rows, :].astype(F32)
        sg = jax.nn.sigmoid(cg_ref[rows, :].astype(F32))
        u = cin * sg
        du = jnp.zeros((STRIP, CONV), F32)
        for j in range(CONV_W):
            dj = _tap(dbuf, dsh, CONV_W - 1 - j, r0, STRIP)
            du = du + w_ref[j:j + 1, :] * dj
            p = u * dj
            fold = p[0:SUBLANES]
            for q in range(1, STRIP // SUBLANES):
                fold = fold + p[q * SUBLANES:(q + 1) * SUBLANES, :]
            dwacc[j * SUBLANES:(j + 1) * SUBLANES, :] += fold
        dpc_ref[rows, 0:CONV] = (du * sg).astype(BF16)
        dpc_ref[rows, CONV:] = (du * cin * sg * (1.0 - sg)).astype(BF16)
        return carry

    lax.fori_loop(0, tm // STRIP, strip, 0)
    dbuf[tm:, :] = dbuf[0:HALO, :]

    @pl.when(step == nt - 1)
    def _():
        dw_ref[...] = jnp.zeros_like(dw_ref)
        for j in range(CONV_W):
            dw_ref[j:j + 1, :] = _colsum(dwacc[j * SUBLANES:(j + 1) * SUBLANES, :])


def _conv_bwd(proj, uc, dmix, conv_w, cn_g, cn_b, gmat):
    T = proj.shape[0]
    tm = min(T, TOKEN_TILE)
    nt = T // tm
    sh_rows = tm + HALO - SUBLANES

    def body(*refs):
        _conv_bwd_tile(*refs, tm, nt)

    rev = lambda i: nt - 1 - i
    tile = lambda col: pl.BlockSpec((tm, CONV), lambda i: (rev(i), col))
    acc = lambda rows: pl.BlockSpec((rows, CONV), lambda i: (0, 0))
    return pl.pallas_call(
        body, name="conv_bwd", grid=(nt,),
        out_shape=[jax.ShapeDtypeStruct((T, 2 * CONV), BF16),
                   jax.ShapeDtypeStruct((HALO, CONV), F32), jax.ShapeDtypeStruct((1, CONV), F32),
                   jax.ShapeDtypeStruct((1, CONV), F32), jax.ShapeDtypeStruct((1, CONV), F32)],
        in_specs=[tile(P_CI // CONV), tile(P_CG // CONV), tile(0), tile(1),
                  _const((HALO, CONV)), _const((1, CONV)), _const((1, CONV)), _const((CONV, CONV))],
        out_specs=[pl.BlockSpec((tm, 2 * CONV), lambda i: (rev(i), 0)), acc(HALO), acc(1), acc(1), acc(1)],
        scratch_shapes=[pltpu.VMEM((tm + HALO, CONV), F32), pltpu.VMEM((SUBLANES - 1, sh_rows, CONV), F32),
                        pltpu.VMEM((HALO * SUBLANES, CONV), F32)],
        compiler_params=_params(("arbitrary",)),
    )(proj, proj, uc, dmix, conv_w, cn_g, cn_b, gmat)


def _gla_bwd_tile(q_ref, k_ref, v_ref, g_ref, z_ref, o_ref, st_ref, dm_ref, wg_ref, bg_ref, gn_ref,
                  dpg_ref, dwg_ref, dbg_ref, dgn_ref, dstate, do_scr, dv_scr, tb):
    @pl.when(pl.program_id(0) == 0)
    def _():
        dstate[...] = jnp.zeros_like(dstate)
        dwg_ref[...] = jnp.zeros_like(dwg_ref)
        dbg_ref[...] = jnp.zeros_like(dbg_ref)
        dgn_ref[...] = jnp.zeros_like(dgn_ref)

    gnv = gn_ref[...]
    dgn = jnp.zeros((1, DV), F32)
    for h in range(HEADS):
        cols = slice(h * DV, (h + 1) * DV)
        oh = o_ref[:, cols]
        r = lax.rsqrt(_rowmean(oh * oh) + EPS)
        ohat = oh * r
        gh = g_ref[:, cols].astype(F32)
        sg = jax.nn.sigmoid(gh)
        dmx = dm_ref[:, cols].astype(F32)
        don = dmx * (gh * sg)
        dpg_ref[:, P_G + h * DV:P_G + (h + 1) * DV] = (dmx * (ohat * gnv) * _silu_grad(gh, sg)).astype(BF16)
        dgn = dgn + _colsum(don * ohat)
        do_scr[:, cols] = _rms_bwd(don, ohat, r, gnv)
    dgn_ref[...] += dgn

    mats = _chunk_matrices()
    masks = _head_masks()
    wgv, bgv = wg_ref[...], bg_ref[...]
    n_chunks = SUB // CHUNK

    for sb in reversed(range(tb // SUB)):
        rows = slice(sb * SUB, (sb + 1) * SUB)
        zs = z_ref[rows, :]
        d = _decay_terms(zs, q_ref[rows, :], k_ref[rows, :], wgv, bgv, mats)
        qem = d.q * d.ebm
        qem_b = qem.astype(BF16)
        kem_b = (d.k * d.emb).astype(BF16)
        dq = jnp.zeros((SUB, KEY), F32)
        dk = jnp.zeros((SUB, KEY), F32)
        for h in range(HEADS):
            hm = masks[h]
            cols = slice(h * DV, (h + 1) * DV)
            do_b = do_scr[rows, cols].astype(BF16)
            vh = v_ref[rows, cols]
            da = jnp.where(mats.causal, _nt(do_b, vh), 0.0).astype(BF16)
            da_t = jnp.where(mats.causal_t, _nt(vh, do_b), 0.0).astype(BF16)
            a_t = jnp.where(mats.causal_t, _nt(kem_b, (qem * hm).astype(BF16)), 0.0).astype(BF16)
            dq = dq + hm * _nn(da, kem_b)
            dk = dk + hm * _nn(da_t, qem_b)
            dv_scr[rows, cols] = _nn(a_t, do_b)
        dq = dq * d.ebm
        dk = dk * d.emb

        qe0_b = (d.q * d.eb).astype(BF16)
        kdec_b = (d.k * d.elb).astype(BF16)
        dq_st, dk_st, last = [None] * n_chunks, [None] * n_chunks, [None] * n_chunks
        for c in reversed(range(n_chunks)):
            loc = slice(c * CHUNK, (c + 1) * CHUNK)
            glob = slice(sb * SUB + c * CHUNK, sb * SUB + (c + 1) * CHUNK)
            st_b = st_ref[sb * n_chunks + c]
            ds = dstate[...]
            ds_b = ds.astype(BF16)
            do_c = do_scr[glob, :].astype(BF16)
            ebl_c = d.ebl[c * CHUNK:c * CHUNK + 1]
            dk_c = _nn(v_ref[glob, :], ds_b) * d.elb[loc]
            dq_st[c] = _nn(do_c, st_b) * d.eb[loc]
            dk_st[c] = dk_c
            last_c = _colsum(d.k[loc] * dk_c) + ebl_c * _colsum(st_b.astype(F32) * ds)
            last[c] = jnp.broadcast_to(last_c, (CHUNK, KEY))
            dpg_ref[glob, P_V:P_G] = (dv_scr[glob, :] + _nt(kdec_b[loc], ds_b)).astype(BF16)
            dstate[...] = ds * ebl_c + jnp.where(mats.heads, _tn(do_c, qe0_b[loc]), 0.0)
        dq = dq + jnp.concatenate(dq_st, axis=0)
        dk = dk + jnp.concatenate(dk_st, axis=0)
        dpg_ref[rows, P_Q:P_K] = (dq * Q_SCALE).astype(BF16)
        dpg_ref[rows, P_K:P_V] = dk.astype(BF16)
        hi, lo = _split_bf16(d.q * dq - d.k * dk)
        dla = _nn(mats.tri_t, hi) + _nn(mats.tri_t, lo) + jnp.concatenate(last, axis=0)
        dal = dla * (1.0 / GATE_TAU) * jax.nn.sigmoid(-d.al)
        dal_b = dal.astype(BF16)
        dpg_ref[rows, OFF_Z:] = _nt(dal_b, wgv).astype(BF16)
        dwg_ref[...] += _tn(zs, dal_b)
        dbg_ref[...] += _colsum(dal)


def _gla_bwd(proj, o, states, dmix, wg, bg, gn, parts):
    T = proj.shape[0]
    tb = min(T, TOKEN_TILE)
    cpb = tb // CHUNK
    nb = T // tb
    n_comm = len(parts)
    kinds = ["exchange"] * n_comm
    comm_start, comm_wait = _hosted_comm(kinds, 11, 4, n_comm, nb)

    def body(*refs):
        comm_start(refs)
        _gla_bwd_tile(*refs[:11], *refs[11 + n_comm:15 + n_comm], *refs[15 + 2 * n_comm:18 + 2 * n_comm], tb)
        comm_wait(refs)

    rev = lambda i: nb - 1 - i
    blk = lambda w, col: pl.BlockSpec((tb, w), lambda i: (rev(i), col))
    res = pl.pallas_call(
        body, name="gla_bwd", grid=(nb,),
        out_shape=[jax.ShapeDtypeStruct((T, D_GLA), BF16), jax.ShapeDtypeStruct((Z_PAD, KEY), F32),
                   jax.ShapeDtypeStruct((1, KEY), F32), jax.ShapeDtypeStruct((1, DV), F32)]
        + _comm_out_shapes(kinds, parts),
        in_specs=[blk(KEY, P_Q // KEY), blk(KEY, P_K // KEY), blk(VAL, P_V // VAL), blk(VAL, P_G // VAL),
                  blk(Z_PAD, P_Z // Z_PAD), blk(VAL, 0),
                  pl.BlockSpec((cpb, VAL, KEY), lambda i: (rev(i), 0, 0)), blk(VAL, 0),
                  _const((Z_PAD, KEY)), _const((1, KEY)), _const((1, DV))] + [ANY] * n_comm,
        out_specs=[blk(D_GLA, 0),
                   pl.BlockSpec((Z_PAD, KEY), lambda i: (0, 0)), pl.BlockSpec((1, KEY), lambda i: (0, 0)),
                   pl.BlockSpec((1, DV), lambda i: (0, 0))] + [ANY] * n_comm,
        scratch_shapes=[pltpu.VMEM((VAL, KEY), F32), pltpu.VMEM((tb, VAL), F32), pltpu.VMEM((tb, VAL), F32)]
        + _comm_scratch(n_comm),
        compiler_params=_params(("arbitrary",)),
    )(proj, proj, proj, proj, proj, o, states, dmix, wg, bg, gn, *parts)
    return res[:4], res[4:]


def _inproj_bwd(x, g1, w_in_t, dh1, dp_gla, dp_conv):
    T = x.shape[0]
    tm = min(T, TOKEN_TILE)
    nt = T // tm
    streams = [x, dh1, dp_gla, dp_conv]

    def body(g_ref, w_ref, x_hbm, dh1_hbm, dpg_hbm, dpc_hbm, dx_ref, dg1_ref, x_buf, dh1_buf, dpg_buf, dpc_buf, sems):
        step = pl.program_id(0)
        hbm, bufs = [x_hbm, dh1_hbm, dpg_hbm, dpc_hbm], [x_buf, dh1_buf, dpg_buf, dpc_buf]

        def fetch(at, k):
            start = at * tm if isinstance(at, int) else pl.multiple_of(at * tm, tm)
            return pltpu.make_async_copy(hbm[k].at[pl.ds(start, tm), :], bufs[k].at[at % RING], sems.at[k, at % RING])

        @pl.when(step == 0)
        def _():
            dg1_ref[...] = jnp.zeros_like(dg1_ref)
            for at in range(min(RING - 1, nt)):
                for k in range(len(streams)):
                    fetch(at, k).start()

        @pl.when(step + (RING - 1) < nt)
        def _():
            for k in range(len(streams)):
                fetch(step + (RING - 1), k).start()

        for k in range(len(streams)):
            fetch(step, k).wait()
        slot = step % RING
        dxn = _nn(dpg_buf[slot], w_ref[0:D_GLA, :]) + _nn(dpc_buf[slot], w_ref[OFF_C:D_IN, :])
        xv = x_buf[slot]
        r = lax.rsqrt(_rowmean(xv * xv) + EPS)
        xhat = xv * r
        dg1_ref[...] += _colsum(dxn * xhat)
        dx_ref[...] = dh1_buf[slot] + _rms_bwd(dxn, xhat, r, g_ref[...])

    return pl.pallas_call(
        body, name="inproj_bwd", grid=(nt,),
        out_shape=[jax.ShapeDtypeStruct((T, D_MODEL), F32), jax.ShapeDtypeStruct((1, D_MODEL), F32)],
        in_specs=[_const((1, D_MODEL)), _const((D_IN, D_MODEL))] + [ANY] * len(streams),
        out_specs=[pl.BlockSpec((tm, D_MODEL), lambda i: (i, 0)), pl.BlockSpec((1, D_MODEL), lambda i: (0, 0))],
        scratch_shapes=[pltpu.VMEM((RING, tm, s.shape[1]), s.dtype) for s in streams]
        + [pltpu.SemaphoreType.DMA((len(streams), RING))],
        compiler_params=_params(("arbitrary",)),
    )(g1, w_in_t, x, dh1, dp_gla, dp_conv)


def _wgrad_in(xn, dp_gla, dp_conv):
    T = xn.shape[0]
    tt = min(T, WGRAD_TILE // 2)
    nt = T // tt

    def body(xn_ref, dpg_ref, dpc_ref, o_ref, acc):
        @pl.when(pl.program_id(0) == 0)
        def _():
            acc[...] = jnp.zeros_like(acc)

        xv = xn_ref[...]
        acc[0:OFF_C, :] += _tn(dpg_ref[...], xv)[0:OFF_C]
        acc[OFF_C:, :] += _tn(dpc_ref[...], xv)

        @pl.when(pl.program_id(0) == nt - 1)
        def _():
            o_ref[...] = acc[...].astype(BF16)

    tok = lambda w: pl.BlockSpec((tt, w), lambda t: (t, 0))
    return pl.pallas_call(
        body, name="wgrad_in", grid=(nt,), out_shape=jax.ShapeDtypeStruct((D_IN, D_MODEL), BF16),
        in_specs=[tok(D_MODEL), tok(D_GLA), tok(2 * CONV)],
        out_specs=pl.BlockSpec((D_IN, D_MODEL), lambda t: (0, 0), pipeline_mode=pl.Buffered(1)),
        scratch_shapes=[pltpu.VMEM((D_IN, D_MODEL), F32)],
        compiler_params=_params(("arbitrary",)),
    )(xn, dp_gla, dp_conv)


def _wgrad_out(mix_a, mix_c, dh1):
    T = dh1.shape[0]
    tt = min(T, WGRAD_TILE // 2)
    nt = T // tt

    def body(a_ref, c_ref, b_ref, o_ref, acc):
        @pl.when(pl.program_id(0) == 0)
        def _():
            acc[...] = jnp.zeros_like(acc)

        b = b_ref[...].astype(BF16)
        acc[0:VAL, :] += _tn(a_ref[...], b)
        acc[VAL:, :] += _tn(c_ref[...], b)

        @pl.when(pl.program_id(0) == nt - 1)
        def _():
            o_ref[...] = acc[...].astype(BF16)

    tok = lambda w: pl.BlockSpec((tt, w), lambda t: (t, 0))
    return pl.pallas_call(
        body, name="wgrad_out", grid=(nt,), out_shape=jax.ShapeDtypeStruct((D_MODEL, D_MODEL), BF16),
        in_specs=[tok(VAL), tok(CONV), tok(D_MODEL)],
        out_specs=pl.BlockSpec((D_MODEL, D_MODEL), lambda t: (0, 0)),
        scratch_shapes=[pltpu.VMEM((D_MODEL, D_MODEL), F32)],
        compiler_params=_params(("arbitrary",)),
    )(mix_a, mix_c, dh1)


def _wgrad(a, b, name, tk, tn, col_block=None):
    T, K = a.shape
    N = b.shape[1]
    tt = min(T, WGRAD_TILE)
    nt = T // tt

    def body(a_ref, b_ref, o_ref, acc):
        @pl.when(pl.program_id(2) == 0)
        def _():
            acc[...] = jnp.zeros_like(acc)

        acc[...] += _tn(a_ref[...], b_ref[...].astype(BF16))

        @pl.when(pl.program_id(2) == nt - 1)
        def _():
            if col_block is None:
                o_ref[...] = acc[...].astype(BF16)
            else:
                for q in range(tn // col_block):
                    o_ref[q] = acc[:, q * col_block:(q + 1) * col_block].astype(BF16)

    if col_block is None:
        out_shape = jax.ShapeDtypeStruct((K, N), BF16)
        out_spec = pl.BlockSpec((tk, tn), lambda i, j, t: (i, j))
    else:
        assert tk == K
        out_shape = jax.ShapeDtypeStruct((N // col_block, K, col_block), BF16)
        out_spec = pl.BlockSpec((tn // col_block, tk, col_block), lambda i, j, t: (j, 0, 0))
    return pl.pallas_call(
        body, name=name, grid=(K // tk, N // tn, nt), out_shape=out_shape,
        in_specs=[pl.BlockSpec((tt, tk), lambda i, j, t: (t, i)), pl.BlockSpec((tt, tn), lambda i, j, t: (t, j))],
        out_specs=out_spec, scratch_shapes=[pltpu.VMEM((tk, tn), F32)],
        compiler_params=_params(("arbitrary", "arbitrary", "arbitrary")),
    )(a, b)


def _adam_math(w, g, m, v):
    m = ADAM_B1 * m + (1.0 - ADAM_B1) * g
    v = ADAM_B2 * v + (1.0 - ADAM_B2) * (g * g)
    m_hat = m / (1.0 - ADAM_B1 ** ADAM_STEP)
    v_hat = v / (1.0 - ADAM_B2 ** ADAM_STEP)
    delta = -ADAM_LR * (m_hat / (jnp.sqrt(v_hat) + ADAM_EPS) + ADAM_WD * w)
    return delta, m, v


def _sum8(ref):
    g = ref[0].astype(F32)
    for s in range(1, N_DEV):
        g = g + ref[s].astype(F32)
    return g


def _adam_big(parts, w, m, v, name):
    R, C = w.shape
    tr = ADAM_ROWS if R % ADAM_ROWS == 0 else R

    def body(p_ref, w_ref, m_ref, v_ref, g_ref, d_ref, nm_ref, nv_ref):
        g = _sum8(p_ref)
        g_ref[...] = g
        d_ref[...], nm_ref[...], nv_ref[...] = _adam_math(w_ref[...], g, m_ref[...], v_ref[...])

    row = pl.BlockSpec((tr, C), lambda i: (i, 0))
    return pl.pallas_call(
        body, name=name, grid=(R // tr,), out_shape=[jax.ShapeDtypeStruct((R, C), F32)] * 4,
        in_specs=[pl.BlockSpec((N_DEV, tr, C), lambda i: (0, i, 0)), row, row, row], out_specs=[row] * 4,
        compiler_params=_params(("arbitrary",)),
    )(parts, w, m, v)


def _sum_small(parts):
    def body(p_ref, o_ref):
        o_ref[...] = _sum8(p_ref)

    return pl.pallas_call(body, name="sum_small", out_shape=jax.ShapeDtypeStruct(parts.shape[1:], F32))(parts)


def _adam_small(gs, ws, ms, vs):
    n = len(gs)

    def body(*refs):
        g_refs, w_refs, m_refs, v_refs = refs[:n], refs[n:2 * n], refs[2 * n:3 * n], refs[3 * n:4 * n]
        outs = refs[4 * n:]
        for i in range(n):
            d, nm, nv = _adam_math(w_refs[i][...], g_refs[i][...], m_refs[i][...], v_refs[i][...])
            outs[i][...] = d
            outs[n + i][...] = nm
            outs[2 * n + i][...] = nv

    shapes = [jax.ShapeDtypeStruct(w.shape, F32) for w in ws]
    res = pl.pallas_call(body, name="adam_small", out_shape=shapes * 3)(*gs, *ws, *ms, *vs)
    return res[:n], res[n:2 * n], res[2 * n:]


def _group_matrix():
    gi = lax.broadcasted_iota(jnp.int32, (CONV, CONV), 0) // (CONV // GROUPS)
    gj = lax.broadcasted_iota(jnp.int32, (CONV, CONV), 1) // (CONV // GROUPS)
    return jnp.where(gi == gj, GROUPS / CONV, 0.0).astype(BF16)


_SMALL = [("loss", 8), ("dg1", 8), ("dbg", 2), ("dgn", 1), ("dconv_b", 4), ("dcn_g", 4), ("dcn_b", 4), ("dg2", 8),
          ("dgf", 8), ("dwg", 32), ("dconv_w", 124)]


def _pad8(rows):
    return -(-rows // 8) * 8


def kernel(x, norm1_g, w_in, w_gate_up, b_gate, gla_norm_g, conv_w, conv_b, conv_norm_g, conv_norm_b, w_out, norm2_g, w_mlp_in, w_mlp_out, final_norm_g, loss_target, m_norm1_g, m_w_in, m_w_gate_up, m_b_gate, m_gla_norm_g, m_conv_w, m_conv_b, m_conv_norm_g, m_conv_norm_b, m_w_out, m_norm2_g, m_w_mlp_in, m_w_mlp_out, m_final_norm_g, v_norm1_g, v_w_in, v_w_gate_up, v_b_gate, v_gla_norm_g, v_conv_w, v_conv_b, v_conv_norm_g, v_conv_norm_b, v_w_out, v_norm2_g, v_w_mlp_in, v_w_mlp_out, v_final_norm_g):
    x_idx = lax.axis_index("x")
    y_idx = lax.axis_index("y")
    c_idx = lax.axis_index("c")
    me = 4 * x_idx + 2 * y_idx + c_idx
    xs, tgt = x[0], loss_target[0]
    gf = final_norm_g.reshape(1, D_MODEL)
    gmat = _group_matrix()

    small_shard = jnp.zeros((_pad8(RANK + CONV_W), LANES), F32)
    small_shard = small_shard.at[0:RANK, 0:KEY // N_DEV].set(w_gate_up[0])
    small_shard = small_shard.at[RANK:RANK + CONV_W, 0:CONV // N_DEV].set(conv_w[0])
    g_in, g_small = _gather_two_level([w_in[0].T.astype(BF16), small_shard], "gather_w_in")
    w_in_t = g_in.reshape(D_IN, D_MODEL)
    wg_full = jnp.concatenate([g_small[d, 0:RANK, 0:KEY // N_DEV] for d in range(N_DEV)], axis=1)
    wg_pad = jnp.pad(wg_full, ((0, Z_PAD - RANK), (0, 0))).astype(BF16)
    conv_w_full = jnp.concatenate([g_small[d, RANK:RANK + CONV_W, 0:CONV // N_DEV] for d in range(N_DEV)], axis=1)
    conv_w_pad = jnp.pad(conv_w_full, ((0, HALO - CONV_W), (0, 0)))

    proj, xn = _inproj_fwd(xs, norm1_g, w_in_t)
    mix_a, o, states, mix_c, uc, (g_out, g_w1, g_w2) = _mix_fwd(
        proj, wg_pad, b_gate, gla_norm_g, conv_w_pad, conv_b, conv_norm_g, conv_norm_b, gmat,
        [w_out[0].astype(BF16), w_mlp_in[0].T.astype(BF16), w_mlp_out[0].astype(BF16)])
    w_out_full = g_out.reshape(D_MODEL, D_MODEL)
    w1t_full = g_w1.reshape(D_FF, D_MODEL)
    w2_full = g_w2.reshape(D_FF, D_MODEL)
    dh1, dmix, hn, ff, da, dh2, loss, dgf, dg2 = _mlp_fwd_bwd(xs, mix_a, mix_c, tgt, w_out_full, norm2_g, w1t_full,
                                                              w2_full, gf)

    dw1 = _wgrad(hn, da, "wgrad_mlp_in", WGRAD_BLOCK, WGRAD_BLOCK, col_block=D_FF // N_DEV)
    dw2 = _wgrad(ff, dh2, "wgrad_mlp_out", WGRAD_BLOCK, WGRAD_BLOCK)
    dw_out = _wgrad_out(mix_a, mix_c, dh1)
    dp_conv, dconv_w, dconv_b, dcn_g, dcn_b = _conv_bwd(proj, uc, dmix, conv_w_pad, conv_norm_g, conv_norm_b, gmat)
    (dp_gla, dwg, dbg, dgn), (p_w1, p_w2, p_out) = _gla_bwd(
        proj, o, states, dmix, wg_pad, b_gate, gla_norm_g,
        [dw1, dw2.reshape(N_DEV, D_FF // N_DEV, D_MODEL), dw_out.reshape(N_DEV, D_MODEL // N_DEV, D_MODEL)])
    dw_in = _wgrad_in(xn, dp_gla, dp_conv).reshape(N_DEV, SHARD_IN, D_MODEL)
    send_sems, recv_sems, dw_in_thru, land, token = _split_start("exchange", dw_in, jnp.copy(dw_in),
                                                                 "exchange_w_in_start")
    dx, dg1 = _inproj_bwd(xs, norm1_g + token[0:1, 0:1], w_in_t, dh1, dp_gla, dp_conv)
    p_in = _split_wait("exchange", send_sems, recv_sems, dw_in_thru, land, dg1, "exchange_w_in_wait")

    small = dict(loss=jnp.zeros((SUBLANES, LANES), F32) + loss, dg1=dg1, dbg=dbg, dgn=dgn, dconv_b=dconv_b, dcn_g=dcn_g,
                 dcn_b=dcn_b, dg2=dg2, dgf=dgf, dwg=dwg[0:RANK], dconv_w=dconv_w[0:CONV_W])
    pack = jnp.concatenate([jnp.pad(small[name].reshape(rows, LANES), ((0, _pad8(rows) - rows), (0, 0)))
                            for name, rows in _SMALL], axis=0)
    s_send, s_recv, pack_thru, pack_land, s_token = _split_start(
        "gather", pack, jnp.broadcast_to(pack, (N_DEV,) + pack.shape) + 0.0, "gather_small_start")

    gi, di, mi, vi = _adam_big(p_in, w_in[0].T, m_w_in[0].T, v_w_in[0].T, "adam_w_in")
    go, do, mo, vo = _adam_big(p_out, w_out[0] + s_token[0:1, 0:1], m_w_out[0], v_w_out[0], "adam_w_out")
    ga, da_, ma, va = _adam_big(p_w1, w_mlp_in[0], m_w_mlp_in[0], v_w_mlp_in[0], "adam_w_mlp_in")
    gb, db, mb, vb = _adam_big(p_w2, w_mlp_out[0], m_w_mlp_out[0], v_w_mlp_out[0], "adam_w_mlp_out")
    cut = lambda a: a.T[None]

    g_pack = _split_wait("gather", s_send, s_recv, pack_thru, pack_land, go[0:8, 0:128] + ga[0:8, 0:128]
                         + gb[0:8, 0:128], "gather_small_wait")
    summed = _sum_small(g_pack)
    small_g = {}
    at = 0
    for name, rows in _SMALL:
        small_g[name] = summed[at:at + rows]
        at += _pad8(rows)
    loss_out = small_g["loss"][0, 0]
    wg_cols = KEY // N_DEV
    cw_cols = CONV // N_DEV
    g_small_list = [
        small_g["dg1"].reshape(1, D_MODEL),
        lax.dynamic_slice_in_dim(small_g["dwg"].reshape(RANK, KEY), me * wg_cols, wg_cols, axis=1)[None],
        small_g["dbg"].reshape(1, KEY), small_g["dgn"].reshape(1, DV),
        lax.dynamic_slice_in_dim(small_g["dconv_w"].reshape(CONV_W, CONV), me * cw_cols, cw_cols, axis=1)[None],
        small_g["dconv_b"].reshape(1, CONV), small_g["dcn_g"].reshape(1, CONV), small_g["dcn_b"].reshape(1, CONV),
        small_g["dg2"].reshape(1, D_MODEL), small_g["dgf"].reshape(1, D_MODEL),
    ]
    row = lambda a: a.reshape(1, D_MODEL)
    w_small = [norm1_g, w_gate_up, b_gate, gla_norm_g, conv_w, conv_b, conv_norm_g, conv_norm_b, norm2_g,
               row(final_norm_g)]
    m_small = [m_norm1_g, m_w_gate_up, m_b_gate, m_gla_norm_g, m_conv_w, m_conv_b, m_conv_norm_g, m_conv_norm_b,
               m_norm2_g, row(m_final_norm_g)]
    v_small = [v_norm1_g, v_w_gate_up, v_b_gate, v_gla_norm_g, v_conv_w, v_conv_b, v_conv_norm_g, v_conv_norm_b,
               v_norm2_g, row(v_final_norm_g)]
    d_small, nm_small, nv_small = _adam_small(g_small_list, w_small, m_small, v_small)
    flat = lambda lst: list(lst[:-1]) + [lst[-1].reshape(D_MODEL)]
    g_small_list, d_small, nm_small, nv_small = flat(g_small_list), flat(d_small), flat(nm_small), flat(nv_small)

    def order(s, w_in_v, w_out_v, w1_v, w2_v):
        return [s[0], w_in_v, s[1], s[2], s[3], s[4], s[5], s[6], s[7], w_out_v, s[8], w1_v, w2_v, s[9]]

    grads = order(g_small_list, cut(gi), go[None], ga[None], gb[None])
    deltas = order(d_small, cut(di), do[None], da_[None], db[None])
    new_m = order(nm_small, cut(mi), mo[None], ma[None], mb[None])
    new_v = order(nv_small, cut(vi), vo[None], va[None], vb[None])
    return (loss_out, dx[None], *grads, *deltas, *new_m, *new_v)
```

```python
from typing import NamedTuple

import jax
import jax.numpy as jnp
from jax import lax
from jax.experimental import pallas as pl
from jax.experimental.pallas import tpu as pltpu

F32 = jnp.float32
BF16 = jnp.bfloat16

N_DEV = 8
D_MODEL = 1024
HEADS = 4
DK = 64
DV = 128
KEY = HEADS * DK
VAL = HEADS * DV
RANK = 16
CONV = 512
GROUPS = 8
CONV_W = 31
HALO = 32
SUBLANES = 8
LANES = 128
STRIP = 32
FWD_STRIP = 16
TOKEN_TILE = 512
MLP_TILE = 256
WGRAD_TILE = 4096
WGRAD_BLOCK = 1024
ADAM_ROWS = 128
RING = 3
D_FF = 4096
D_IN = 2576
SHARD_IN = D_IN // N_DEV
CHUNK = 64
SUB = 256
EPS = 1e-6
GATE_TAU = 16.0
Q_SCALE = DK ** -0.5

P_Q, P_K, P_V, P_G, P_CI, P_CG, P_Z = 0, 256, 512, 1024, 1536, 2048, 2560
D_INP = 2688
Z_PAD = D_INP - P_Z
OFF_Z = 1536
OFF_C = OFF_Z + RANK
D_GLA = OFF_Z + Z_PAD

ADAM_LR = 0.001
ADAM_B1 = 0.9
ADAM_B2 = 0.999
ADAM_EPS = 1e-08
ADAM_WD = 0.01
ADAM_STEP = 10

V7X_VMEM_BYTES = 64 * 1024 * 1024
VMEM_LIMIT = V7X_VMEM_BYTES * 7 // 8

MESH = pl.DeviceIdType.MESH
ANY = pl.BlockSpec(memory_space=pl.ANY)


def _nn(a, b):
    return jnp.dot(a, b, preferred_element_type=F32)


def _nt(a, b):
    return lax.dot_general(a, b, (((1,), (1,)), ((), ())), preferred_element_type=F32)


def _tn(a, b):
    return lax.dot_general(a, b, (((0,), (0,)), ((), ())), preferred_element_type=F32)


def _params(sem=None):
    return pltpu.CompilerParams(dimension_semantics=sem, vmem_limit_bytes=VMEM_LIMIT)


def _const(shape):
    return pl.BlockSpec(shape, lambda *_: (0,) * len(shape), pipeline_mode=pl.Buffered(1))


def _colsum(v):
    return jnp.sum(v, axis=0, keepdims=True)


def _rowmean(v):
    return jnp.mean(v, axis=-1, keepdims=True)


def _split_bf16(v):
    hi = v.astype(BF16)
    return hi, (v - hi.astype(F32)).astype(BF16)


def _my_place():
    return lax.axis_index("x"), lax.axis_index("y"), lax.axis_index("c")


def _peer(j):
    x, y, c = _my_place()
    jx, jy, jc = (j >> 2) & 1, (j >> 1) & 1, j & 1
    px = 1 - x if jx else x
    py = 1 - y if jy else y
    pc = 1 - c if jc else c
    return (px, py, pc), 4 * px + 2 * py + pc


def _comm_plan(kinds, ins, outs, send_sems, recv_sems, local_sems, receives=True):
    x, y, c = _my_place()
    me = 4 * x + 2 * y + c
    own = lambda k, idx: ins[k] if kinds[k] == "gather" else ins[k].at[idx]
    local = [pltpu.make_async_copy(own(k, me), outs[k].at[me], local_sems.at[k]) for k in range(len(kinds))]
    sends, recvs = [], []
    for j in range(1, N_DEV):
        peer, peer_idx = _peer(j)
        for k in range(len(kinds)):
            sems = dict(send_sem=send_sems.at[k, j - 1], recv_sem=recv_sems.at[k, j - 1], device_id=peer,
                        device_id_type=MESH)
            sends.append(pltpu.make_async_remote_copy(src_ref=own(k, peer_idx), dst_ref=outs[k].at[me], **sems))
            if receives:
                recvs.append(pltpu.make_async_remote_copy(src_ref=own(k, me), dst_ref=outs[k].at[peer_idx], **sems))
    return local, sends, recvs


def _comm_start(plan):
    local, sends, _ = plan
    for cp in local + sends:
        cp.start()


def _comm_wait(plan):
    local, sends, recvs = plan
    for cp in recvs:
        cp.wait_recv()
    for cp in sends:
        cp.wait_send()
    for cp in local:
        cp.wait()


def _comm_scratch(n):
    return [pltpu.SemaphoreType.DMA((n, N_DEV - 1)), pltpu.SemaphoreType.DMA((n, N_DEV - 1)),
            pltpu.SemaphoreType.DMA((n,))]


def _comm_out_shapes(kinds, arrays):
    return [jax.ShapeDtypeStruct(((N_DEV,) + a.shape) if kind == "gather" else a.shape, a.dtype)
            for kind, a in zip(kinds, arrays)]


def _gather_two_level(shards, name):
    n = len(shards)

    def body(*refs):
        ins, outs = refs[:n], refs[n:2 * n]
        send_sems, recv_sems, local_sems = refs[2 * n:]
        x, y, c = _my_place()
        index = lambda px, py, pc: 4 * px + 2 * py + pc
        me, sibling = (x, y, c), (x, y, 1 - c)
        chips = [(1 - x, y), (x, 1 - y), (1 - x, 1 - y)]

        def copy(k, slot, block, to, src=None):
            rows = outs[k].at[index(*block)]
            return pltpu.make_async_remote_copy(
                src_ref=rows if src is None else src, dst_ref=rows, send_sem=send_sems.at[k, slot],
                recv_sem=recv_sems.at[k, slot], device_id=to, device_id_type=MESH)

        local = [pltpu.make_async_copy(ins[k], outs[k].at[index(*me)], local_sems.at[k]) for k in range(n)]
        first = []
        for k in range(n):
            first.append(copy(k, 0, me, sibling, src=ins[k]))
            first += [copy(k, 1 + j, me, (*chip, c), src=ins[k]) for j, chip in enumerate(chips)]
        for cp in local + first:
            cp.start()
        passed = []
        for j, chip in enumerate(chips):
            for k in range(n):
                copy(k, 1 + j, (*chip, c), me).wait_recv()
                cp = copy(k, 4 + j, (*chip, c), sibling)
                cp.start()
                passed.append(cp)
        for k in range(n):
            copy(k, 0, sibling, me).wait_recv()
        for j, chip in enumerate(chips):
            for k in range(n):
                copy(k, 4 + j, (*chip, 1 - c), me).wait_recv()
        for cp in first + passed:
            cp.wait_send()
        for cp in local:
            cp.wait()

    return pl.pallas_call(
        body, name=name, out_shape=_comm_out_shapes(["gather"] * n, shards), in_specs=[ANY] * n, out_specs=[ANY] * n,
        scratch_shapes=_comm_scratch(n),
    )(*shards)


HBM = pl.BlockSpec(memory_space=pltpu.HBM)
SEM = pl.BlockSpec(memory_space=pltpu.SEMAPHORE)
DATAFLOW = pltpu.SideEffectType.DATAFLOW_SIDE_EFFECTING


def _split_start(kind, part, land, name):
    def body(src_ref, land_ref, send_sems, recv_sems, src_thru, land_thru, token):
        x, y, c = _my_place()
        me = 4 * x + 2 * y + c
        for j in range(1, N_DEV):
            peer, peer_idx = _peer(j)
            pltpu.make_async_remote_copy(
                src_ref=src_ref.at[peer_idx] if kind == "exchange" else src_ref, dst_ref=land_ref.at[me],
                send_sem=send_sems.at[j - 1], recv_sem=recv_sems.at[j - 1], device_id=peer,
                device_id_type=MESH).start()
        token[...] = jnp.zeros_like(token)

    return pl.pallas_call(
        body, name=name,
        out_shape=(pltpu.SemaphoreType.DMA((N_DEV - 1,)), pltpu.SemaphoreType.DMA((N_DEV - 1,)),
                   pltpu.HBM(part.shape, part.dtype), pltpu.HBM(land.shape, land.dtype),
                   jax.ShapeDtypeStruct((SUBLANES, LANES), F32)),
        in_specs=(HBM, HBM), out_specs=(SEM, SEM, HBM, HBM, pl.BlockSpec(memory_space=pltpu.VMEM)),
        input_output_aliases={0: 2, 1: 3},
        compiler_params=pltpu.CompilerParams(has_side_effects=DATAFLOW),
    )(pltpu.with_memory_space_constraint(part, pltpu.HBM), pltpu.with_memory_space_constraint(land, pltpu.HBM))


def _split_wait(kind, send_sems, recv_sems, part_thru, land_thru, after, name):
    def body(src_ref, land_ref, send_sems, recv_sems, after_ref, src_dead, got_ref):
        x, y, c = _my_place()
        me = 4 * x + 2 * y + c
        own = lambda idx: src_ref.at[idx] if kind == "exchange" else src_ref
        for j in range(1, N_DEV):
            peer, peer_idx = _peer(j)
            sems = dict(send_sem=send_sems.at[j - 1], recv_sem=recv_sems.at[j - 1], device_id=peer,
                        device_id_type=MESH)
            pltpu.make_async_remote_copy(src_ref=own(peer_idx), dst_ref=land_ref.at[me], **sems).wait_send()
            pltpu.make_async_remote_copy(src_ref=own(me), dst_ref=land_ref.at[peer_idx], **sems).wait_recv()

    return pl.pallas_call(
        body, name=name,
        out_shape=(pltpu.HBM(part_thru.shape, part_thru.dtype), pltpu.HBM(land_thru.shape, land_thru.dtype)),
        in_specs=(HBM, HBM, SEM, SEM, ANY), out_specs=(HBM, HBM), input_output_aliases={0: 0, 1: 1},
        compiler_params=pltpu.CompilerParams(has_side_effects=DATAFLOW),
    )(part_thru, land_thru, send_sems, recv_sems, after)[1]


def _hosted_comm(kinds, n_in, n_out, n_comm, n_steps):
    def plan_of(refs, receives):
        ins = refs[n_in:n_in + n_comm]
        outs = refs[n_in + n_comm + n_out:n_in + 2 * n_comm + n_out]
        return _comm_plan(kinds, ins, outs, *refs[-3:], receives=receives)

    def start(refs):
        @pl.when(pl.program_id(0) == 0)
        def _():
            _comm_start(plan_of(refs, False))

    def wait(refs):
        @pl.when(pl.program_id(0) == n_steps - 1)
        def _():
            _comm_wait(plan_of(refs, True))

    return start, wait


def _z_lanes():
    return lax.broadcasted_iota(jnp.int32, (1, Z_PAD), 1) < RANK


def _inproj_fwd(x, g1, w_in_t):
    T = x.shape[0]
    tm = min(T, TOKEN_TILE)
    nt = T // tm

    def body(g_ref, w_ref, x_hbm, proj_ref, xn_ref, x_buf, sems):
        step = pl.program_id(0)

        def fetch(at):
            start = at * tm if isinstance(at, int) else pl.multiple_of(at * tm, tm)
            return pltpu.make_async_copy(x_hbm.at[pl.ds(start, tm), :], x_buf.at[at % RING], sems.at[at % RING])

        @pl.when(step == 0)
        def _():
            for at in range(min(RING - 1, nt)):
                fetch(at).start()

        @pl.when(step + (RING - 1) < nt)
        def _():
            fetch(step + (RING - 1)).start()

        fetch(step).wait()
        xv = x_buf[step % RING]
        r = lax.rsqrt(_rowmean(xv * xv) + EPS)
        xn = (xv * r * g_ref[...]).astype(BF16)
        xn_ref[...] = xn
        proj_ref[:, 0:P_CI] = _nt(xn, w_ref[0:OFF_Z, :]).astype(BF16)
        proj_ref[:, P_CI:P_Z] = _nt(xn, w_ref[OFF_C:D_IN, :]).astype(BF16)
        proj_ref[:, P_Z:] = jnp.where(_z_lanes(), _nt(xn, w_ref[OFF_Z:OFF_Z + Z_PAD, :]), 0.0).astype(BF16)

    return pl.pallas_call(
        body, name="inproj_fwd", grid=(T // tm,),
        out_shape=[jax.ShapeDtypeStruct((T, D_INP), BF16), jax.ShapeDtypeStruct((T, D_MODEL), BF16)],
        in_specs=[_const((1, D_MODEL)), _const((D_IN, D_MODEL)), ANY],
        out_specs=[pl.BlockSpec((tm, D_INP), lambda i: (i, 0)), pl.BlockSpec((tm, D_MODEL), lambda i: (i, 0))],
        scratch_shapes=[pltpu.VMEM((RING, tm, D_MODEL), F32), pltpu.SemaphoreType.DMA((RING,))],
        compiler_params=_params(("arbitrary",)),
    )(g1, w_in_t, x)


def _head_masks():
    lane = lax.broadcasted_iota(jnp.int32, (1, KEY), 1)
    return [((lane >= h * DK) & (lane < (h + 1) * DK)).astype(F32) for h in range(HEADS)]


class _Mats(NamedTuple):
    tri: jax.Array
    tri_t: jax.Array
    same: jax.Array
    mid: jax.Array
    causal: jax.Array
    causal_t: jax.Array
    heads: jax.Array


def _chunk_matrices():
    r = lax.broadcasted_iota(jnp.int32, (SUB, SUB), 0)
    c = lax.broadcasted_iota(jnp.int32, (SUB, SUB), 1)
    shift = CHUNK.bit_length() - 1
    same = jnp.right_shift(r, shift) == jnp.right_shift(c, shift)
    causal = same & (r >= c)
    causal_t = same & (r <= c)
    mid = same & ((c & (CHUNK - 1)) < CHUNK // 2)
    hr = jnp.right_shift(lax.broadcasted_iota(jnp.int32, (VAL, KEY), 0), DV.bit_length() - 1)
    hc = jnp.right_shift(lax.broadcasted_iota(jnp.int32, (VAL, KEY), 1), DK.bit_length() - 1)
    return _Mats(tri=causal.astype(BF16), tri_t=causal_t.astype(BF16), same=same.astype(BF16), mid=mid.astype(BF16),
                 causal=causal, causal_t=causal_t, heads=hr == hc)


class _Decay(NamedTuple):
    al: jax.Array
    q: jax.Array
    k: jax.Array
    eb: jax.Array
    ebm: jax.Array
    emb: jax.Array
    elb: jax.Array
    ebl: jax.Array


def _decay_terms(z, q, k, wg, bg, mats):
    al = _nn(z, wg) + bg
    la = (jnp.minimum(al, 0.0) - jnp.log(1.0 + jnp.exp(-jnp.abs(al)))) * (1.0 / GATE_TAU)
    hi, lo = _split_bf16(la)
    cum = lambda m: _nn(m, hi) + _nn(m, lo)
    b, b_last, b_mid = cum(mats.tri), cum(mats.same), cum(mats.mid)
    return _Decay(al=al, q=q.astype(F32) * Q_SCALE, k=k.astype(F32), eb=jnp.exp(b), ebm=jnp.exp(b - b_mid),
                  emb=jnp.exp(b_mid - b), elb=jnp.exp(b_last - b), ebl=jnp.exp(b_last))


def _gla_fwd_tile(q_ref, k_ref, v_ref, g_ref, z_ref, wg_ref, bg_ref, gn_ref, mix_ref, o_ref, st_ref, state, tb):
    @pl.when(pl.program_id(0) == 0)
    def _():
        state[...] = jnp.zeros_like(state)

    mats = _chunk_matrices()
    masks = _head_masks()
    wgv, bgv = wg_ref[...], bg_ref[...]

    for sb in range(tb // SUB):
        rows = slice(sb * SUB, (sb + 1) * SUB)
        d = _decay_terms(z_ref[rows, :], q_ref[rows, :], k_ref[rows, :], wgv, bgv, mats)
        kem_b = (d.k * d.emb).astype(BF16)
        qem = d.q * d.ebm
        for h in range(HEADS):
            cols = slice(h * DV, (h + 1) * DV)
            a = jnp.where(mats.causal, _nt((qem * masks[h]).astype(BF16), kem_b), 0.0)
            o_ref[rows, cols] = _nn(a.astype(BF16), v_ref[rows, cols])
        qe0_b = (d.q * d.eb).astype(BF16)
        kdec_b = (d.k * d.elb).astype(BF16)
        for c in range(SUB // CHUNK):
            loc = slice(c * CHUNK, (c + 1) * CHUNK)
            glob = slice(sb * SUB + c * CHUNK, sb * SUB + (c + 1) * CHUNK)
            st = state[...]
            st_b = st.astype(BF16)
            st_ref[sb * (SUB // CHUNK) + c] = st_b
            o_ref[glob, :] += _nt(qe0_b[loc], st_b)
            u = _tn(v_ref[glob, :], kdec_b[loc])
            state[...] = st * d.ebl[c * CHUNK:c * CHUNK + 1] + jnp.where(mats.heads, u, 0.0)

    gnv = gn_ref[...]
    for h in range(HEADS):
        cols = slice(h * DV, (h + 1) * DV)
        oh = o_ref[:, cols]
        r = lax.rsqrt(_rowmean(oh * oh) + EPS)
        gh = g_ref[:, cols].astype(F32)
        mix_ref[:, cols] = (oh * r * gnv * (gh * jax.nn.sigmoid(gh))).astype(BF16)


def _group_mean(v, gmat):
    return _nn(v.astype(BF16), gmat)


def _shifted_copies(buf, sh, rows):
    for k in range(1, SUBLANES):
        sh[k - 1] = buf[pl.ds(k, rows), :]


def _tap(buf, sh, off, r0, n):
    k, base = off % SUBLANES, off - off % SUBLANES
    rows = pl.ds(r0 + base if isinstance(r0, int) else pl.multiple_of(r0 + base, SUBLANES), n)
    return buf[rows, :] if k == 0 else sh[k - 1, rows, :]


def _conv_fwd_tile(ci_ref, cg_ref, w_ref, b_ref, g_ref, be_ref, gm_ref, mix_ref, uc_ref, ubuf, ush, tm):
    sh_rows = tm + HALO - SUBLANES

    @pl.when(pl.program_id(0) == 0)
    def _():
        ubuf[0:HALO, :] = jnp.zeros((HALO, CONV), F32)

    ubuf[HALO:, :] = ci_ref[...].astype(F32) * jax.nn.sigmoid(cg_ref[...].astype(F32))
    _shifted_copies(ubuf, ush, sh_rows)
    for s in range(tm // FWD_STRIP):
        acc = jnp.zeros((FWD_STRIP, CONV), F32) + b_ref[...]
        for j in range(CONV_W):
            acc = acc + w_ref[j:j + 1, :] * _tap(ubuf, ush, HALO - (CONV_W - 1) + j, s * FWD_STRIP, FWD_STRIP)
        uc_ref[s * FWD_STRIP:(s + 1) * FWD_STRIP, :] = acc
    ubuf[0:HALO, :] = ubuf[tm:tm + HALO, :]
    gm = gm_ref[...]
    ucv = uc_ref[...]
    d = ucv - _group_mean(ucv, gm)
    var = _group_mean(d * d, gm)
    yn = d * lax.rsqrt(var + EPS) * g_ref[...] + be_ref[...]
    mix_ref[...] = (yn * jax.nn.sigmoid(yn)).astype(BF16)


def _mix_fwd(proj, wg, bg, gn, conv_w, conv_b, cn_g, cn_b, gmat, shards):
    T = proj.shape[0]
    tb = min(T, TOKEN_TILE)
    cpb = tb // CHUNK
    n_comm = len(shards)
    kinds = ["gather"] * n_comm
    comm_start, comm_wait = _hosted_comm(kinds, 15, 5, n_comm, T // tb)

    def body(*refs):
        gla_in, conv_in = refs[:8], refs[8:15]
        gla_out, conv_out = refs[15 + n_comm:18 + n_comm], refs[18 + n_comm:20 + n_comm]
        state, ubuf, ush = refs[20 + 2 * n_comm:23 + 2 * n_comm]
        comm_start(refs)
        _gla_fwd_tile(*gla_in, *gla_out, state, tb)
        _conv_fwd_tile(*conv_in, *conv_out, ubuf, ush, tb)
        comm_wait(refs)

    nc = T // CHUNK
    tok = lambda w, col: pl.BlockSpec((tb, w), lambda i: (i, col))
    res = pl.pallas_call(
        body, name="mix_fwd", grid=(T // tb,),
        out_shape=[jax.ShapeDtypeStruct((T, VAL), BF16), jax.ShapeDtypeStruct((T, VAL), F32),
                   jax.ShapeDtypeStruct((nc, VAL, KEY), BF16), jax.ShapeDtypeStruct((T, CONV), BF16),
                   jax.ShapeDtypeStruct((T, CONV), F32)] + _comm_out_shapes(kinds, shards),
        in_specs=[tok(KEY, P_Q // KEY), tok(KEY, P_K // KEY), tok(VAL, P_V // VAL), tok(VAL, P_G // VAL),
                  tok(Z_PAD, P_Z // Z_PAD), _const((Z_PAD, KEY)), _const((1, KEY)), _const((1, DV)),
                  tok(CONV, P_CI // CONV), tok(CONV, P_CG // CONV), _const((HALO, CONV)), _const((1, CONV)),
                  _const((1, CONV)), _const((1, CONV)), _const((CONV, CONV))] + [ANY] * n_comm,
        out_specs=[tok(VAL, 0), tok(VAL, 0), pl.BlockSpec((cpb, VAL, KEY), lambda i: (i, 0, 0)), tok(CONV, 0),
                   tok(CONV, 0)] + [ANY] * n_comm,
        scratch_shapes=[pltpu.VMEM((VAL, KEY), F32), pltpu.VMEM((tb + HALO, CONV), F32),
                        pltpu.VMEM((SUBLANES - 1, tb + HALO - SUBLANES, CONV), F32)] + _comm_scratch(n_comm),
        compiler_params=_params(("arbitrary",)),
    )(proj, proj, proj, proj, proj, wg, bg, gn, proj, proj, conv_w, conv_b, cn_g, cn_b, gmat, *shards)
    return res[0], res[1], res[2], res[3], res[4], res[5:]


def _rms_bwd(dy, xhat, r, g):
    dyg = dy * g
    return r * (dyg - xhat * _rowmean(dyg * xhat))


def _mlp_fwd_bwd(x, mix_a, mix_c, tgt, w_out, g2, w1t, w2, gf):
    T = x.shape[0]
    tm = min(T, MLP_TILE)
    inv_d = 1.0 / D_MODEL

    def body(x_ref, ma_ref, mc_ref, t_ref, wo_ref, g2_ref, w1_ref, w2_ref, gf_ref,
             dh1_ref, dmix_ref, hn_ref, ff_ref, da_ref, dh2_ref, loss_ref, dgf_ref, dg2_ref):
        @pl.when(pl.program_id(0) == 0)
        def _():
            loss_ref[...] = jnp.zeros_like(loss_ref)
            dgf_ref[...] = jnp.zeros_like(dgf_ref)
            dg2_ref[...] = jnp.zeros_like(dg2_ref)

        g2v, gfv = g2_ref[...], gf_ref[...]
        h1 = x_ref[...] + _nn(ma_ref[...], wo_ref[0:VAL, :]) + _nn(mc_ref[...], wo_ref[VAL:, :])
        r2 = lax.rsqrt(_rowmean(h1 * h1) + EPS)
        h1hat = h1 * r2
        hn = (h1hat * g2v).astype(BF16)
        hn_ref[...] = hn
        relu_a = jnp.maximum(_nt(hn, w1_ref[...]), 0.0)
        ff = (relu_a * relu_a).astype(BF16)
        ff_ref[...] = ff
        h2 = h1 + _nn(ff, w2_ref[...])
        rf = lax.rsqrt(_rowmean(h2 * h2) + EPS)
        h2hat = h2 * rf
        err = h2hat * gfv - t_ref[...]
        loss_ref[...] += (0.5 * inv_d) * _colsum(jnp.sum(err * err, axis=1, keepdims=True))
        dy = err * inv_d
        dgf_ref[...] += _colsum(dy * h2hat)
        dh2 = _rms_bwd(dy, h2hat, rf, gfv)
        dh2_b = dh2.astype(BF16)
        dh2_ref[...] = dh2_b
        da = (_nt(dh2_b, w2_ref[...]) * (2.0 * relu_a)).astype(BF16)
        da_ref[...] = da
        dhn = _nn(da, w1_ref[...])
        dg2_ref[...] += _colsum(dhn * h1hat)
        dh1 = dh2 + _rms_bwd(dhn, h1hat, r2, g2v)
        dh1_ref[...] = dh1
        dmix_ref[...] = _nt(dh1.astype(BF16), wo_ref[...]).astype(BF16)

    tok = lambda w: pl.BlockSpec((tm, w), lambda i: (i, 0))
    return pl.pallas_call(
        body, name="mlp_fwd_bwd", grid=(T // tm,),
        out_shape=[jax.ShapeDtypeStruct((T, D_MODEL), F32), jax.ShapeDtypeStruct((T, D_MODEL), BF16),
                   jax.ShapeDtypeStruct((T, D_MODEL), BF16), jax.ShapeDtypeStruct((T, D_FF), BF16),
                   jax.ShapeDtypeStruct((T, D_FF), BF16), jax.ShapeDtypeStruct((T, D_MODEL), BF16),
                   jax.ShapeDtypeStruct((1, 1), F32), jax.ShapeDtypeStruct((1, D_MODEL), F32),
                   jax.ShapeDtypeStruct((1, D_MODEL), F32)],
        in_specs=[tok(D_MODEL), tok(VAL), tok(CONV), tok(D_MODEL), _const((D_MODEL, D_MODEL)), _const((1, D_MODEL)),
                  _const((D_FF, D_MODEL)), _const((D_FF, D_MODEL)), _const((1, D_MODEL))],
        out_specs=[tok(D_MODEL), tok(D_MODEL), tok(D_MODEL), tok(D_FF), tok(D_FF), tok(D_MODEL),
                   pl.BlockSpec((1, 1), lambda i: (0, 0)), pl.BlockSpec((1, D_MODEL), lambda i: (0, 0)),
                   pl.BlockSpec((1, D_MODEL), lambda i: (0, 0))],
        compiler_params=_params(("arbitrary",)),
    )(x, mix_a, mix_c, tgt, w_out, g2, w1t, w2, gf)


def _silu_grad(v, s):
    return s * (1.0 + v * (1.0 - s))


def _conv_bwd_tile(ci_ref, cg_ref, uc_ref, dm_ref, w_ref, g_ref, be_ref, gm_ref,
                   dpc_ref, dw_ref, db_ref, dg_ref, dbe_ref, dbuf, dsh, dwacc, tm, nt):
    step = pl.program_id(0)
    sh_rows = tm + HALO - SUBLANES

    @pl.when(step == 0)
    def _():
        dbuf[tm:, :] = jnp.zeros((HALO, CONV), F32)
        dwacc[...] = jnp.zeros_like(dwacc)
        db_ref[...] = jnp.zeros_like(db_ref)
        dg_ref[...] = jnp.zeros_like(dg_ref)
        dbe_ref[...] = jnp.zeros_like(dbe_ref)

    gm, gv = gm_ref[...], g_ref[...]
    ucv = uc_ref[...]
    d = ucv - _group_mean(ucv, gm)
    rs = lax.rsqrt(_group_mean(d * d, gm) + EPS)
    yhat = d * rs
    yn = yhat * gv + be_ref[...]
    dyn = dm_ref[...].astype(F32) * _silu_grad(yn, jax.nn.sigmoid(yn))
    dg_ref[...] += _colsum(dyn * yhat)
    dbe_ref[...] += _colsum(dyn)
    dyh = dyn * gv
    duc = rs * (dyh - _group_mean(dyh, gm) - yhat * _group_mean(dyh * yhat, gm))
    db_ref[...] += _colsum(duc)
    dbuf[0:tm, :] = duc
    _shifted_copies(dbuf, dsh, sh_rows)

    def strip(s, carry):
        r0 = pl.multiple_of(s * STRIP, STRIP)
        rows = pl.ds(r0, STRIP)
        cin = ci_ref[rows, :].astype(F32)
        sg = jax.nn.sigmoid(cg_ref[rows, :].astype(F32))
        u = cin * sg
        du = jnp.zeros((STRIP, CONV), F32)
        for j in range(CONV_W):
            dj = _tap(dbuf, dsh, CONV_W - 1 - j, r0, STRIP)
            du = du + w_ref[j:j + 1, :] * dj
            p = u * dj
            fold = p[0:SUBLANES]
            for q in range(1, STRIP // SUBLANES):
                fold = fold + p[q * SUBLANES:(q + 1) * SUBLANES, :]
            dwacc[j * SUBLANES:(j + 1) * SUBLANES, :] += fold
        dpc_ref[rows, 0:CONV] = (du * sg).astype(BF16)
        dpc_ref[rows, CONV:] = (du * cin * sg * (1.0 - sg)).astype(BF16)
        return carry

    lax.fori_loop(0, tm // STRIP, strip, 0)
    dbuf[tm:, :] = dbuf[0:HALO, :]

    @pl.when(step == nt - 1)
    def _():
        dw_ref[...] = jnp.zeros_like(dw_ref)
        for j in range(CONV_W):
            dw_ref[j:j + 1, :] = _colsum(dwacc[j * SUBLANES:(j + 1) * SUBLANES, :])


def _conv_bwd(proj, uc, dmix, conv_w, cn_g, cn_b, gmat):
    T = proj.shape[0]
    tm = min(T, TOKEN_TILE)
    nt = T // tm
    sh_rows = tm + HALO - SUBLANES

    def body(*refs):
        _conv_bwd_tile(*refs, tm, nt)

    rev = lambda i: nt - 1 - i
    tile = lambda col: pl.BlockSpec((tm, CONV), lambda i: (rev(i), col))
    acc = lambda rows: pl.BlockSpec((rows, CONV), lambda i: (0, 0))
    return pl.pallas_call(
        body, name="conv_bwd", grid=(nt,),
        out_shape=[jax.ShapeDtypeStruct((T, 2 * CONV), BF16),
                   jax.ShapeDtypeStruct((HALO, CONV), F32), jax.ShapeDtypeStruct((1, CONV), F32),
                   jax.ShapeDtypeStruct((1, CONV), F32), jax.ShapeDtypeStruct((1, CONV), F32)],
        in_specs=[tile(P_CI // CONV), tile(P_CG // CONV), tile(0), tile(1),
                  _const((HALO, CONV)), _const((1, CONV)), _const((1, CONV)), _const((CONV, CONV))],
        out_specs=[pl.BlockSpec((tm, 2 * CONV), lambda i: (rev(i), 0)), acc(HALO), acc(1), acc(1), acc(1)],
        scratch_shapes=[pltpu.VMEM((tm + HALO, CONV), F32), pltpu.VMEM((SUBLANES - 1, sh_rows, CONV), F32),
                        pltpu.VMEM((HALO * SUBLANES, CONV), F32)],
        compiler_params=_params(("arbitrary",)),
    )(proj, proj, uc, dmix, conv_w, cn_g, cn_b, gmat)


def _gla_bwd_tile(q_ref, k_ref, v_ref, g_ref, z_ref, o_ref, st_ref, dm_ref, wg_ref, bg_ref, gn_ref,
                  dpg_ref, dwg_ref, dbg_ref, dgn_ref, dstate, do_scr, dv_scr, tb):
    @pl.when(pl.program_id(0) == 0)
    def _():
        dstate[...] = jnp.zeros_like(dstate)
        dwg_ref[...] = jnp.zeros_like(dwg_ref)
        dbg_ref[...] = jnp.zeros_like(dbg_ref)
        dgn_ref[...] = jnp.zeros_like(dgn_ref)

    gnv = gn_ref[...]
    dgn = jnp.zeros((1, DV), F32)
    for h in range(HEADS):
        cols = slice(h * DV, (h + 1) * DV)
        oh = o_ref[:, cols]
        r = lax.rsqrt(_rowmean(oh * oh) + EPS)
        ohat = oh * r
        gh = g_ref[:, cols].astype(F32)
        sg = jax.nn.sigmoid(gh)
        dmx = dm_ref[:, cols].astype(F32)
        don = dmx * (gh * sg)
        dpg_ref[:, P_G + h * DV:P_G + (h + 1) * DV] = (dmx * (ohat * gnv) * _silu_grad(gh, sg)).astype(BF16)
        dgn = dgn + _colsum(don * ohat)
        do_scr[:, cols] = _rms_bwd(don, ohat, r, gnv)
    dgn_ref[...] += dgn

    mats = _chunk_matrices()
    masks = _head_masks()
    wgv, bgv = wg_ref[...], bg_ref[...]
    n_chunks = SUB // CHUNK

    for sb in reversed(range(tb // SUB)):
        rows = slice(sb * SUB, (sb + 1) * SUB)
        zs = z_ref[rows, :]
        d = _decay_terms(zs, q_ref[rows, :], k_ref[rows, :], wgv, bgv, mats)
        qem = d.q * d.ebm
        qem_b = qem.astype(BF16)
        kem_b = (d.k * d.emb).astype(BF16)
        dq = jnp.zeros((SUB, KEY), F32)
        dk = jnp.zeros((SUB, KEY), F32)
        for h in range(HEADS):
            hm = masks[h]
            cols = slice(h * DV, (h + 1) * DV)
            do_b = do_scr[rows, cols].astype(BF16)
            vh = v_ref[rows, cols]
            da = jnp.where(mats.causal, _nt(do_b, vh), 0.0).astype(BF16)
            da_t = jnp.where(mats.causal_t, _nt(vh, do_b), 0.0).astype(BF16)
            a_t = jnp.where(mats.causal_t, _nt(kem_b, (qem * hm).astype(BF16)), 0.0).astype(BF16)
            dq = dq + hm * _nn(da, kem_b)
            dk = dk + hm * _nn(da_t, qem_b)
            dv_scr[rows, cols] = _nn(a_t, do_b)
        dq = dq * d.ebm
        dk = dk * d.emb

        qe0_b = (d.q * d.eb).astype(BF16)
        kdec_b = (d.k * d.elb).astype(BF16)
        dq_st, dk_st, last = [None] * n_chunks, [None] * n_chunks, [None] * n_chunks
        for c in reversed(range(n_chunks)):
            loc = slice(c * CHUNK, (c + 1) * CHUNK)
            glob = slice(sb * SUB + c * CHUNK, sb * SUB + (c + 1) * CHUNK)
            st_b = st_ref[sb * n_chunks + c]
            ds = dstate[...]
            ds_b = ds.astype(BF16)
            do_c = do_scr[glob, :].astype(BF16)
            ebl_c = d.ebl[c * CHUNK:c * CHUNK + 1]
            dk_c = _nn(v_ref[glob, :], ds_b) * d.elb[loc]
            dq_st[c] = _nn(do_c, st_b) * d.eb[loc]
            dk_st[c] = dk_c
            last_c = _colsum(d.k[loc] * dk_c) + ebl_c * _colsum(st_b.astype(F32) * ds)
            last[c] = jnp.broadcast_to(last_c, (CHUNK, KEY))
            dpg_ref[glob, P_V:P_G] = (dv_scr[glob, :] + _nt(kdec_b[loc], ds_b)).astype(BF16)
            dstate[...] = ds * ebl_c + jnp.where(mats.heads, _tn(do_c, qe0_b[loc]), 0.0)
        dq = dq + jnp.concatenate(dq_st, axis=0)
        dk = dk + jnp.concatenate(dk_st, axis=0)
        dpg_ref[rows, P_Q:P_K] = (dq * Q_SCALE).astype(BF16)
        dpg_ref[rows, P_K:P_V] = dk.astype(BF16)
        hi, lo = _split_bf16(d.q * dq - d.k * dk)
        dla = _nn(mats.tri_t, hi) + _nn(mats.tri_t, lo) + jnp.concatenate(last, axis=0)
        dal = dla * (1.0 / GATE_TAU) * jax.nn.sigmoid(-d.al)
        dal_b = dal.astype(BF16)
        dpg_ref[rows, OFF_Z:] = _nt(dal_b, wgv).astype(BF16)
        dwg_ref[...] += _tn(zs, dal_b)
        dbg_ref[...] += _colsum(dal)


def _gla_bwd(proj, o, states, dmix, wg, bg, gn, parts):
    T = proj.shape[0]
    tb = min(T, TOKEN_TILE)
    cpb = tb // CHUNK
    nb = T // tb
    n_comm = len(parts)
    kinds = ["exchange"] * n_comm
    comm_start, comm_wait = _hosted_comm(kinds, 11, 4, n_comm, nb)

    def body(*refs):
        comm_start(refs)
        _gla_bwd_tile(*refs[:11], *refs[11 + n_comm:15 + n_comm], *refs[15 + 2 * n_comm:18 + 2 * n_comm], tb)
        comm_wait(refs)

    rev = lambda i: nb - 1 - i
    blk = lambda w, col: pl.BlockSpec((tb, w), lambda i: (rev(i), col))
    res = pl.pallas_call(
        body, name="gla_bwd", grid=(nb,),
        out_shape=[jax.ShapeDtypeStruct((T, D_GLA), BF16), jax.ShapeDtypeStruct((Z_PAD, KEY), F32),
                   jax.ShapeDtypeStruct((1, KEY), F32), jax.ShapeDtypeStruct((1, DV), F32)]
        + _comm_out_shapes(kinds, parts),
        in_specs=[blk(KEY, P_Q // KEY), blk(KEY, P_K // KEY), blk(VAL, P_V // VAL), blk(VAL, P_G // VAL),
                  blk(Z_PAD, P_Z // Z_PAD), blk(VAL, 0),
                  pl.BlockSpec((cpb, VAL, KEY), lambda i: (rev(i), 0, 0)), blk(VAL, 0),
                  _const((Z_PAD, KEY)), _const((1, KEY)), _const((1, DV))] + [ANY] * n_comm,
        out_specs=[blk(D_GLA, 0),
                   pl.BlockSpec((Z_PAD, KEY), lambda i: (0, 0)), pl.BlockSpec((1, KEY), lambda i: (0, 0)),
                   pl.BlockSpec((1, DV), lambda i: (0, 0))] + [ANY] * n_comm,
        scratch_shapes=[pltpu.VMEM((VAL, KEY), F32), pltpu.VMEM((tb, VAL), F32), pltpu.VMEM((tb, VAL), F32)]
        + _comm_scratch(n_comm),
        compiler_params=_params(("arbitrary",)),
    )(proj, proj, proj, proj, proj, o, states, dmix, wg, bg, gn, *parts)
    return res[:4], res[4:]


def _inproj_bwd(x, g1, w_in_t, dh1, dp_gla, dp_conv):
    T = x.shape[0]
    tm = min(T, TOKEN_TILE)
    nt = T // tm
    streams = [x, dh1, dp_gla, dp_conv]

    def body(g_ref, w_ref, x_hbm, dh1_hbm, dpg_hbm, dpc_hbm, dx_ref, dg1_ref, x_buf, dh1_buf, dpg_buf, dpc_buf, sems):
        step = pl.program_id(0)
        hbm, bufs = [x_hbm, dh1_hbm, dpg_hbm, dpc_hbm], [x_buf, dh1_buf, dpg_buf, dpc_buf]

        def fetch(at, k):
            start = at * tm if isinstance(at, int) else pl.multiple_of(at * tm, tm)
            return pltpu.make_async_copy(hbm[k].at[pl.ds(start, tm), :], bufs[k].at[at % RING], sems.at[k, at % RING])

        @pl.when(step == 0)
        def _():
            dg1_ref[...] = jnp.zeros_like(dg1_ref)
            for at in range(min(RING - 1, nt)):
                for k in range(len(streams)):
                    fetch(at, k).start()

        @pl.when(step + (RING - 1) < nt)
        def _():
            for k in range(len(streams)):
                fetch(step + (RING - 1), k).start()

        for k in range(len(streams)):
            fetch(step, k).wait()
        slot = step % RING
        dxn = _nn(dpg_buf[slot], w_ref[0:D_GLA, :]) + _nn(dpc_buf[slot], w_ref[OFF_C:D_IN, :])
        xv = x_buf[slot]
        r = lax.rsqrt(_rowmean(xv * xv) + EPS)
        xhat = xv * r
        dg1_ref[...] += _colsum(dxn * xhat)
        dx_ref[...] = dh1_buf[slot] + _rms_bwd(dxn, xhat, r, g_ref[...])

    return pl.pallas_call(
        body, name="inproj_bwd", grid=(nt,),
        out_shape=[jax.ShapeDtypeStruct((T, D_MODEL), F32), jax.ShapeDtypeStruct((1, D_MODEL), F32)],
        in_specs=[_const((1, D_MODEL)), _const((D_IN, D_MODEL))] + [ANY] * len(streams),
        out_specs=[pl.BlockSpec((tm, D_MODEL), lambda i: (i, 0)), pl.BlockSpec((1, D_MODEL), lambda i: (0, 0))],
        scratch_shapes=[pltpu.VMEM((RING, tm, s.shape[1]), s.dtype) for s in streams]
        + [pltpu.SemaphoreType.DMA((len(streams), RING))],
        compiler_params=_params(("arbitrary",)),
    )(g1, w_in_t, x, dh1, dp_gla, dp_conv)


def _wgrad_in(xn, dp_gla, dp_conv):
    T = xn.shape[0]
    tt = min(T, WGRAD_TILE // 2)
    nt = T // tt

    def body(xn_ref, dpg_ref, dpc_ref, o_ref, acc):
        @pl.when(pl.program_id(0) == 0)
        def _():
            acc[...] = jnp.zeros_like(acc)

        xv = xn_ref[...]
        acc[0:OFF_C, :] += _tn(dpg_ref[...], xv)[0:OFF_C]
        acc[OFF_C:, :] += _tn(dpc_ref[...], xv)

        @pl.when(pl.program_id(0) == nt - 1)
        def _():
            o_ref[...] = acc[...].astype(BF16)

    tok = lambda w: pl.BlockSpec((tt, w), lambda t: (t, 0))
    return pl.pallas_call(
        body, name="wgrad_in", grid=(nt,), out_shape=jax.ShapeDtypeStruct((D_IN, D_MODEL), BF16),
        in_specs=[tok(D_MODEL), tok(D_GLA), tok(2 * CONV)],
        out_specs=pl.BlockSpec((D_IN, D_MODEL), lambda t: (0, 0), pipeline_mode=pl.Buffered(1)),
        scratch_shapes=[pltpu.VMEM((D_IN, D_MODEL), F32)],
        compiler_params=_params(("arbitrary",)),
    )(xn, dp_gla, dp_conv)


def _wgrad_out(mix_a, mix_c, dh1):
    T = dh1.shape[0]
    tt = min(T, WGRAD_TILE // 2)
    nt = T // tt

    def body(a_ref, c_ref, b_ref, o_ref, acc):
        @pl.when(pl.program_id(0) == 0)
        def _():
            acc[...] = jnp.zeros_like(acc)

        b = b_ref[...].astype(BF16)
        acc[0:VAL, :] += _tn(a_ref[...], b)
        acc[VAL:, :] += _tn(c_ref[...], b)

        @pl.when(pl.program_id(0) == nt - 1)
        def _():
            o_ref[...] = acc[...].astype(BF16)

    tok = lambda w: pl.BlockSpec((tt, w), lambda t: (t, 0))
    return pl.pallas_call(
        body, name="wgrad_out", grid=(nt,), out_shape=jax.ShapeDtypeStruct((D_MODEL, D_MODEL), BF16),
        in_specs=[tok(VAL), tok(CONV), tok(D_MODEL)],
        out_specs=pl.BlockSpec((D_MODEL, D_MODEL), lambda t: (0, 0)),
        scratch_shapes=[pltpu.VMEM((D_MODEL, D_MODEL), F32)],
        compiler_params=_params(("arbitrary",)),
    )(mix_a, mix_c, dh1)


def _wgrad(a, b, name, tk, tn, col_block=None):
    T, K = a.shape
    N = b.shape[1]
    tt = min(T, WGRAD_TILE)
    nt = T // tt

    def body(a_ref, b_ref, o_ref, acc):
        @pl.when(pl.program_id(2) == 0)
        def _():
            acc[...] = jnp.zeros_like(acc)

        acc[...] += _tn(a_ref[...], b_ref[...].astype(BF16))

        @pl.when(pl.program_id(2) == nt - 1)
        def _():
            if col_block is None:
                o_ref[...] = acc[...].astype(BF16)
            else:
                for q in range(tn // col_block):
                    o_ref[q] = acc[:, q * col_block:(q + 1) * col_block].astype(BF16)

    if col_block is None:
        out_shape = jax.ShapeDtypeStruct((K, N), BF16)
        out_spec = pl.BlockSpec((tk, tn), lambda i, j, t: (i, j))
    else:
        assert tk == K
        out_shape = jax.ShapeDtypeStruct((N // col_block, K, col_block), BF16)
        out_spec = pl.BlockSpec((tn // col_block, tk, col_block), lambda i, j, t: (j, 0, 0))
    return pl.pallas_call(
        body, name=name, grid=(K // tk, N // tn, nt), out_shape=out_shape,
        in_specs=[pl.BlockSpec((tt, tk), lambda i, j, t: (t, i)), pl.BlockSpec((tt, tn), lambda i, j, t: (t, j))],
        out_specs=out_spec, scratch_shapes=[pltpu.VMEM((tk, tn), F32)],
        compiler_params=_params(("arbitrary", "arbitrary", "arbitrary")),
    )(a, b)


def _adam_math(w, g, m, v):
    m = ADAM_B1 * m + (1.0 - ADAM_B1) * g
    v = ADAM_B2 * v + (1.0 - ADAM_B2) * (g * g)
    m_hat = m / (1.0 - ADAM_B1 ** ADAM_STEP)
    v_hat = v / (1.0 - ADAM_B2 ** ADAM_STEP)
    delta = -ADAM_LR * (m_hat / (jnp.sqrt(v_hat) + ADAM_EPS) + ADAM_WD * w)
    return delta, m, v


def _sum8(ref):
    g = ref[0].astype(F32)
    for s in range(1, N_DEV):
        g = g + ref[s].astype(F32)
    return g


def _adam_big(parts, w, m, v, name):
    R, C = w.shape
    tr = ADAM_ROWS if R % ADAM_ROWS == 0 else R

    def body(p_ref, w_ref, m_ref, v_ref, g_ref, d_ref, nm_ref, nv_ref):
        g = _sum8(p_ref)
        g_ref[...] = g
        d_ref[...], nm_ref[...], nv_ref[...] = _adam_math(w_ref[...], g, m_ref[...], v_ref[...])

    row = pl.BlockSpec((tr, C), lambda i: (i, 0))
    return pl.pallas_call(
        body, name=name, grid=(R // tr,), out_shape=[jax.ShapeDtypeStruct((R, C), F32)] * 4,
        in_specs=[pl.BlockSpec((N_DEV, tr, C), lambda i: (0, i, 0)), row, row, row], out_specs=[row] * 4,
        compiler_params=_params(("arbitrary",)),
    )(parts, w, m, v)


def _sum_small(parts):
    def body(p_ref, o_ref):
        o_ref[...] = _sum8(p_ref)

    return pl.pallas_call(body, name="sum_small", out_shape=jax.ShapeDtypeStruct(parts.shape[1:], F32))(parts)


def _adam_small(gs, ws, ms, vs):
    n = len(gs)

    def body(*refs):
        g_refs, w_refs, m_refs, v_refs = refs[:n], refs[n:2 * n], refs[2 * n:3 * n], refs[3 * n:4 * n]
        outs = refs[4 * n:]
        for i in range(n):
            d, nm, nv = _adam_math(w_refs[i][...], g_refs[i][...], m_refs[i][...], v_refs[i][...])
            outs[i][...] = d
            outs[n + i][...] = nm
            outs[2 * n + i][...] = nv

    shapes = [jax.ShapeDtypeStruct(w.shape, F32) for w in ws]
    res = pl.pallas_call(body, name="adam_small", out_shape=shapes * 3)(*gs, *ws, *ms, *vs)
    return res[:n], res[n:2 * n], res[2 * n:]


def _group_matrix():
    gi = lax.broadcasted_iota(jnp.int32, (CONV, CONV), 0) // (CONV // GROUPS)
    gj = lax.broadcasted_iota(jnp.int32, (CONV, CONV), 1) // (CONV // GROUPS)
    return jnp.where(gi == gj, GROUPS / CONV, 0.0).astype(BF16)


_SMALL = [("loss", 8), ("dg1", 8), ("dbg", 2), ("dgn", 1), ("dconv_b", 4), ("dcn_g", 4), ("dcn_b", 4), ("dg2", 8),
          ("dgf", 8), ("dwg", 32), ("dconv_w", 124)]


def _pad8(rows):
    return -(-rows // 8) * 8


def kernel(x, norm1_g, w_in, w_gate_up, b_gate, gla_norm_g, conv_w, conv_b, conv_norm_g, conv_norm_b, w_out, norm2_g, w_mlp_in, w_mlp_out, final_norm_g, loss_target, m_norm1_g, m_w_in, m_w_gate_up, m_b_gate, m_gla_norm_g, m_conv_w, m_conv_b, m_conv_norm_g, m_conv_norm_b, m_w_out, m_norm2_g, m_w_mlp_in, m_w_mlp_out, m_final_norm_g, v_norm1_g, v_w_in, v_w_gate_up, v_b_gate, v_gla_norm_g, v_conv_w, v_conv_b, v_conv_norm_g, v_conv_norm_b, v_w_out, v_norm2_g, v_w_mlp_in, v_w_mlp_out, v_final_norm_g):
    x_idx = lax.axis_index("x")
    y_idx = lax.axis_index("y")
    c_idx = lax.axis_index("c")
    me = 4 * x_idx + 2 * y_idx + c_idx
    xs, tgt = x[0], loss_target[0]
    gf = final_norm_g.reshape(1, D_MODEL)
    gmat = _group_matrix()

    small_shard = jnp.zeros((_pad8(RANK + CONV_W), LANES), F32)
    small_shard = small_shard.at[0:RANK, 0:KEY // N_DEV].set(w_gate_up[0])
    small_shard = small_shard.at[RANK:RANK + CONV_W, 0:CONV // N_DEV].set(conv_w[0])
    g_in, g_small = _gather_two_level([w_in[0].T.astype(BF16), small_shard], "gather_w_in")
    w_in_t = g_in.reshape(D_IN, D_MODEL)
    wg_full = jnp.concatenate([g_small[d, 0:RANK, 0:KEY // N_DEV] for d in range(N_DEV)], axis=1)
    wg_pad = jnp.pad(wg_full, ((0, Z_PAD - RANK), (0, 0))).astype(BF16)
    conv_w_full = jnp.concatenate([g_small[d, RANK:RANK + CONV_W, 0:CONV // N_DEV] for d in range(N_DEV)], axis=1)
    conv_w_pad = jnp.pad(conv_w_full, ((0, HALO - CONV_W), (0, 0)))

    proj, xn = _inproj_fwd(xs, norm1_g, w_in_t)
    mix_a, o, states, mix_c, uc, (g_out, g_w1, g_w2) = _mix_fwd(
        proj, wg_pad, b_gate, gla_norm_g, conv_w_pad, conv_b, conv_norm_g, conv_norm_b, gmat,
        [w_out[0].astype(BF16), w_mlp_in[0].T.astype(BF16), w_mlp_out[0].astype(BF16)])
    w_out_full = g_out.reshape(D_MODEL, D_MODEL)
    w1t_full = g_w1.reshape(D_FF, D_MODEL)
    w2_full = g_w2.reshape(D_FF, D_MODEL)
    dh1, dmix, hn, ff, da, dh2, loss, dgf, dg2 = _mlp_fwd_bwd(xs, mix_a, mix_c, tgt, w_out_full, norm2_g, w1t_full,
                                                              w2_full, gf)

    dw1 = _wgrad(hn, da, "wgrad_mlp_in", WGRAD_BLOCK, WGRAD_BLOCK, col_block=D_FF // N_DEV)
    dw2 = _wgrad(ff, dh2, "wgrad_mlp_out", WGRAD_BLOCK, WGRAD_BLOCK)
    dw_out = _wgrad_out(mix_a, mix_c, dh1)
    dp_conv, dconv_w, dconv_b, dcn_g, dcn_b = _conv_bwd(proj, uc, dmix, conv_w_pad, conv_norm_g, conv_norm_b, gmat)
    (dp_gla, dwg, dbg, dgn), (p_w1, p_w2, p_out) = _gla_bwd(
        proj, o, states, dmix, wg_pad, b_gate, gla_norm_g,
        [dw1, dw2.reshape(N_DEV, D_FF // N_DEV, D_MODEL), dw_out.reshape(N_DEV, D_MODEL // N_DEV, D_MODEL)])
    dw_in = _wgrad_in(xn, dp_gla, dp_conv).reshape(N_DEV, SHARD_IN, D_MODEL)
    send_sems, recv_sems, dw_in_thru, land, token = _split_start("exchange", dw_in, jnp.copy(dw_in),
                                                                 "exchange_w_in_start")
    dx, dg1 = _inproj_bwd(xs, norm1_g + token[0:1, 0:1], w_in_t, dh1, dp_gla, dp_conv)
    p_in = _split_wait("exchange", send_sems, recv_sems, dw_in_thru, land, dg1, "exchange_w_in_wait")

    small = dict(loss=jnp.zeros((SUBLANES, LANES), F32) + loss, dg1=dg1, dbg=dbg, dgn=dgn, dconv_b=dconv_b, dcn_g=dcn_g,
                 dcn_b=dcn_b, dg2=dg2, dgf=dgf, dwg=dwg[0:RANK], dconv_w=dconv_w[0:CONV_W])
    pack = jnp.concatenate([jnp.pad(small[name].reshape(rows, LANES), ((0, _pad8(rows) - rows), (0, 0)))
                            for name, rows in _SMALL], axis=0)
    s_send, s_recv, pack_thru, pack_land, s_token = _split_start(
        "gather", pack, jnp.broadcast_to(pack, (N_DEV,) + pack.shape) + 0.0, "gather_small_start")

    gi, di, mi, vi = _adam_big(p_in, w_in[0].T, m_w_in[0].T, v_w_in[0].T, "adam_w_in")
    go, do, mo, vo = _adam_big(p_out, w_out[0] + s_token[0:1, 0:1], m_w_out[0], v_w_out[0], "adam_w_out")
    ga, da_, ma, va = _adam_big(p_w1, w_mlp_in[0], m_w_mlp_in[0], v_w_mlp_in[0], "adam_w_mlp_in")
    gb, db, mb, vb = _adam_big(p_w2, w_mlp_out[0], m_w_mlp_out[0], v_w_mlp_out[0], "adam_w_mlp_out")
    cut = lambda a: a.T[None]

    g_pack = _split_wait("gather", s_send, s_recv, pack_thru, pack_land, go[0:8, 0:128] + ga[0:8, 0:128]
                         + gb[0:8, 0:128], "gather_small_wait")
    summed = _sum_small(g_pack)
    small_g = {}
    at = 0
    for name, rows in _SMALL:
        small_g[name] = summed[at:at + rows]
        at += _pad8(rows)
    loss_out = small_g["loss"][0, 0]
    wg_cols = KEY // N_DEV
    cw_cols = CONV // N_DEV
    g_small_list = [
        small_g["dg1"].reshape(1, D_MODEL),
        lax.dynamic_slice_in_dim(small_g["dwg"].reshape(RANK, KEY), me * wg_cols, wg_cols, axis=1)[None],
        small_g["dbg"].reshape(1, KEY), small_g["dgn"].reshape(1, DV),
        lax.dynamic_slice_in_dim(small_g["dconv_w"].reshape(CONV_W, CONV), me * cw_cols, cw_cols, axis=1)[None],
        small_g["dconv_b"].reshape(1, CONV), small_g["dcn_g"].reshape(1, CONV), small_g["dcn_b"].reshape(1, CONV),
        small_g["dg2"].reshape(1, D_MODEL), small_g["dgf"].reshape(1, D_MODEL),
    ]
    row = lambda a: a.reshape(1, D_MODEL)
    w_small = [norm1_g, w_gate_up, b_gate, gla_norm_g, conv_w, conv_b, conv_norm_g, conv_norm_b, norm2_g,
               row(final_norm_g)]
    m_small = [m_norm1_g, m_w_gate_up, m_b_gate, m_gla_norm_g, m_conv_w, m_conv_b, m_conv_norm_g, m_conv_norm_b,
               m_norm2_g, row(m_final_norm_g)]
    v_small = [v_norm1_g, v_w_gate_up, v_b_gate, v_gla_norm_g, v_conv_w, v_conv_b, v_conv_norm_g, v_conv_norm_b,
               v_norm2_g, row(v_final_norm_g)]
    d_small, nm_small, nv_small = _adam_small(g_small_list, w_small, m_small, v_small)
    flat = lambda lst: list(lst[:-1]) + [lst[-1].reshape(D_MODEL)]
    g_small_list, d_small, nm_small, nv_small = flat(g_small_list), flat(d_small), flat(nm_small), flat(nv_small)

    def order(s, w_in_v, w_out_v, w1_v, w2_v):
        return [s[0], w_in_v, s[1], s[2], s[3], s[4], s[5], s[6], s[7], w_out_v, s[8], w1_v, w2_v, s[9]]

    grads = order(g_small_list, cut(gi), go[None], ga[None], gb[None])
    deltas = order(d_small, cut(di), do[None], da_[None], db[None])
    new_m = order(nm_small, cut(mi), mo[None], ma[None], mb[None])
    new_v = order(nv_small, cut(vi), vo[None], va[None], vb[None])
    return (loss_out, dx[None], *grads, *deltas, *new_m, *new_v)
```

```python
from typing import NamedTuple

import jax
import jax.numpy as jnp
from jax import lax
from jax.experimental import pallas as pl
from jax.experimental.pallas import tpu as pltpu

F32 = jnp.float32
BF16 = jnp.bfloat16

N_DEV = 8
D_MODEL = 1024
HEADS = 4
DK = 64
DV = 128
KEY = HEADS * DK
VAL = HEADS * DV
RANK = 16
CONV = 512
GROUPS = 8
CONV_W = 31
HALO = 32
SUBLANES = 8
LANES = 128
STRIP = 32
FWD_STRIP = 16
TOKEN_TILE = 512
MLP_TILE = 256
WGRAD_TILE = 4096
WGRAD_BLOCK = 1024
ADAM_ROWS = 128
RING = 3
D_FF = 4096
D_IN = 2576
SHARD_IN = D_IN // N_DEV
CHUNK = 64
SUB = 256
EPS = 1e-6
GATE_TAU = 16.0
Q_SCALE = DK ** -0.5

P_Q, P_K, P_V, P_G, P_CI, P_CG, P_Z = 0, 256, 512, 1024, 1536, 2048, 2560
D_INP = 2688
Z_PAD = D_INP - P_Z
OFF_Z = 1536
OFF_C = OFF_Z + RANK
D_GLA = OFF_Z + Z_PAD

ADAM_LR = 0.001
ADAM_B1 = 0.9
ADAM_B2 = 0.999
ADAM_EPS = 1e-08
ADAM_WD = 0.01
ADAM_STEP = 10

V7X_VMEM_BYTES = 64 * 1024 * 1024
VMEM_LIMIT = V7X_VMEM_BYTES * 7 // 8

MESH = pl.DeviceIdType.MESH
ANY = pl.BlockSpec(memory_space=pl.ANY)


def _nn(a, b):
    return jnp.dot(a, b, preferred_element_type=F32)


def _nt(a, b):
    return lax.dot_general(a, b, (((1,), (1,)), ((), ())), preferred_element_type=F32)


def _tn(a, b):
    return lax.dot_general(a, b, (((0,), (0,)), ((), ())), preferred_element_type=F32)


def _params(sem=None):
    return pltpu.CompilerParams(dimension_semantics=sem, vmem_limit_bytes=VMEM_LIMIT)


def _const(shape):
    return pl.BlockSpec(shape, lambda *_: (0,) * len(shape), pipeline_mode=pl.Buffered(1))


def _colsum(v):
    return jnp.sum(v, axis=0, keepdims=True)


def _rowmean(v):
    return jnp.mean(v, axis=-1, keepdims=True)


def _split_bf16(v):
    hi = v.astype(BF16)
    return hi, (v - hi.astype(F32)).astype(BF16)


def _my_place():
    return lax.axis_index("x"), lax.axis_index("y"), lax.axis_index("c")


def _peer(j):
    x, y, c = _my_place()
    jx, jy, jc = (j >> 2) & 1, (j >> 1) & 1, j & 1
    px = 1 - x if jx else x
    py = 1 - y if jy else y
    pc = 1 - c if jc else c
    return (px, py, pc), 4 * px + 2 * py + pc


def _comm_plan(kinds, ins, outs, send_sems, recv_sems, local_sems, receives=True):
    x, y, c = _my_place()
    me = 4 * x + 2 * y + c
    own = lambda k, idx: ins[k] if kinds[k] == "gather" else ins[k].at[idx]
    local = [pltpu.make_async_copy(own(k, me), outs[k].at[me], local_sems.at[k]) for k in range(len(kinds))]
    sends, recvs = [], []
    for j in range(1, N_DEV):
        peer, peer_idx = _peer(j)
        for k in range(len(kinds)):
            sems = dict(send_sem=send_sems.at[k, j - 1], recv_sem=recv_sems.at[k, j - 1], device_id=peer,
                        device_id_type=MESH)
            sends.append(pltpu.make_async_remote_copy(src_ref=own(k, peer_idx), dst_ref=outs[k].at[me], **sems))
            if receives:
                recvs.append(pltpu.make_async_remote_copy(src_ref=own(k, me), dst_ref=outs[k].at[peer_idx], **sems))
    return local, sends, recvs


def _comm_start(plan):
    local, sends, _ = plan
    for cp in local + sends:
        cp.start()


def _comm_wait(plan):
    local, sends, recvs = plan
    for cp in recvs:
        cp.wait_recv()
    for cp in sends:
        cp.wait_send()
    for cp in local:
        cp.wait()


def _comm_scratch(n):
    return [pltpu.SemaphoreType.DMA((n, N_DEV - 1)), pltpu.SemaphoreType.DMA((n, N_DEV - 1)),
            pltpu.SemaphoreType.DMA((n,))]


def _comm_out_shapes(kinds, arrays):
    return [jax.ShapeDtypeStruct(((N_DEV,) + a.shape) if kind == "gather" else a.shape, a.dtype)
            for kind, a in zip(kinds, arrays)]


def _gather_two_level(shards, name):
    n = len(shards)

    def body(*refs):
        ins, outs = refs[:n], refs[n:2 * n]
        send_sems, recv_sems, local_sems = refs[2 * n:]
        x, y, c = _my_place()
        index = lambda px, py, pc: 4 * px + 2 * py + pc
        me, sibling = (x, y, c), (x, y, 1 - c)
        chips = [(1 - x, y), (x, 1 - y), (1 - x, 1 - y)]

        def copy(k, slot, block, to, src=None):
            rows = outs[k].at[index(*block)]
            return pltpu.make_async_remote_copy(
                src_ref=rows if src is None else src, dst_ref=rows, send_sem=send_sems.at[k, slot],
                recv_sem=recv_sems.at[k, slot], device_id=to, device_id_type=MESH)

        local = [pltpu.make_async_copy(ins[k], outs[k].at[index(*me)], local_sems.at[k]) for k in range(n)]
        first = []
        for k in range(n):
            first.append(copy(k, 0, me, sibling, src=ins[k]))
            first += [copy(k, 1 + j, me, (*chip, c), src=ins[k]) for j, chip in enumerate(chips)]
        for cp in local + first:
            cp.start()
        passed = []
        for j, chip in enumerate(chips):
            for k in range(n):
                copy(k, 1 + j, (*chip, c), me).wait_recv()
                cp = copy(k, 4 + j, (*chip, c), sibling)
                cp.start()
                passed.append(cp)
        for k in range(n):
            copy(k, 0, sibling, me).wait_recv()
        for j, chip in enumerate(chips):
            for k in range(n):
                copy(k, 4 + j, (*chip, 1 - c), me).wait_recv()
        for cp in first + passed:
            cp.wait_send()
        for cp in local:
            cp.wait()

    return pl.pallas_call(
        body, name=name, out_shape=_comm_out_shapes(["gather"] * n, shards), in_specs=[ANY] * n, out_specs=[ANY] * n,
        scratch_shapes=_comm_scratch(n),
    )(*shards)


HBM = pl.BlockSpec(memory_space=pltpu.HBM)
SEM = pl.BlockSpec(memory_space=pltpu.SEMAPHORE)
DATAFLOW = pltpu.SideEffectType.DATAFLOW_SIDE_EFFECTING


def _split_start(kind, part, land, name):
    def body(src_ref, land_ref, send_sems, recv_sems, src_thru, land_thru, token):
        x, y, c = _my_place()
        me = 4 * x + 2 * y + c
        for j in range(1, N_DEV):
            peer, peer_idx = _peer(j)
            pltpu.make_async_remote_copy(
                src_ref=src_ref.at[peer_idx] if kind == "exchange" else src_ref, dst_ref=land_ref.at[me],
                send_sem=send_sems.at[j - 1], recv_sem=recv_sems.at[j - 1], device_id=peer,
                device_id_type=MESH).start()
        token[...] = jnp.zeros_like(token)

    return pl.pallas_call(
        body, name=name,
        out_shape=(pltpu.SemaphoreType.DMA((N_DEV - 1,)), pltpu.SemaphoreType.DMA((N_DEV - 1,)),
                   pltpu.HBM(part.shape, part.dtype), pltpu.HBM(land.shape, land.dtype),
                   jax.ShapeDtypeStruct((SUBLANES, LANES), F32)),
        in_specs=(HBM, HBM), out_specs=(SEM, SEM, HBM, HBM, pl.BlockSpec(memory_space=pltpu.VMEM)),
        input_output_aliases={0: 2, 1: 3},
        compiler_params=pltpu.CompilerParams(has_side_effects=DATAFLOW),
    )(pltpu.with_memory_space_constraint(part, pltpu.HBM), pltpu.with_memory_space_constraint(land, pltpu.HBM))


def _split_wait(kind, send_sems, recv_sems, part_thru, land_thru, after, name):
    def body(src_ref, land_ref, send_sems, recv_sems, after_ref, src_dead, got_ref):
        x, y, c = _my_place()
        me = 4 * x + 2 * y + c
        own = lambda idx: src_ref.at[idx] if kind == "exchange" else src_ref
        for j in range(1, N_DEV):
            peer, peer_idx = _peer(j)
            sems = dict(send_sem=send_sems.at[j - 1], recv_sem=recv_sems.at[j - 1], device_id=peer,
                        device_id_type=MESH)
            pltpu.make_async_remote_copy(src_ref=own(peer_idx), dst_ref=land_ref.at[me], **sems).wait_send()
            pltpu.make_async_remote_copy(src_ref=own(me), dst_ref=land_ref.at[peer_idx], **sems).wait_recv()

    return pl.pallas_call(
        body, name=name,
        out_shape=(pltpu.HBM(part_thru.shape, part_thru.dtype), pltpu.HBM(land_thru.shape, land_thru.dtype)),
        in_specs=(HBM, HBM, SEM, SEM, ANY), out_specs=(HBM, HBM), input_output_aliases={0: 0, 1: 1},
        compiler_params=pltpu.CompilerParams(has_side_effects=DATAFLOW),
    )(part_thru, land_thru, send_sems, recv_sems, after)[1]


def _split_wait_sum(send_sems, recv_sems, part_thru, land_thru, after, name):
    def body(src_ref, land_ref, send_sems, recv_sems, after_ref, src_dead, got_ref, sum_ref, buf, sem):
        x, y, c = _my_place()
        me = 4 * x + 2 * y + c
        for j in range(1, N_DEV):
            peer, peer_idx = _peer(j)
            sems = dict(send_sem=send_sems.at[j - 1], recv_sem=recv_sems.at[j - 1], device_id=peer,
                        device_id_type=MESH)
            pltpu.make_async_remote_copy(src_ref=src_ref, dst_ref=land_ref.at[me], **sems).wait_send()
            pltpu.make_async_remote_copy(src_ref=src_ref, dst_ref=land_ref.at[peer_idx], **sems).wait_recv()
        fetch = pltpu.make_async_copy(land_ref, buf, sem.at[0])
        fetch.start()
        fetch.wait()
        sum_ref[...] = _sum8(buf)

    return pl.pallas_call(
        body, name=name,
        out_shape=(pltpu.HBM(part_thru.shape, part_thru.dtype), pltpu.HBM(land_thru.shape, land_thru.dtype),
                   jax.ShapeDtypeStruct(land_thru.shape[1:], F32)),
        in_specs=(HBM, HBM, SEM, SEM, ANY), out_specs=(HBM, HBM, pl.BlockSpec(memory_space=pltpu.VMEM)),
        input_output_aliases={0: 0, 1: 1},
        scratch_shapes=[pltpu.VMEM(land_thru.shape, F32), pltpu.SemaphoreType.DMA((1,))],
        compiler_params=pltpu.CompilerParams(has_side_effects=DATAFLOW),
    )(part_thru, land_thru, send_sems, recv_sems, after)[2]


def _hosted_comm(kinds, n_in, n_out, n_comm, n_steps):
    def plan_of(refs, receives):
        ins = refs[n_in:n_in + n_comm]
        outs = refs[n_in + n_comm + n_out:n_in + 2 * n_comm + n_out]
        return _comm_plan(kinds, ins, outs, *refs[-3:], receives=receives)

    def start(refs):
        @pl.when(pl.program_id(0) == 0)
        def _():
            _comm_start(plan_of(refs, False))

    def wait(refs):
        @pl.when(pl.program_id(0) == n_steps - 1)
        def _():
            _comm_wait(plan_of(refs, True))

    return start, wait


def _z_lanes():
    return lax.broadcasted_iota(jnp.int32, (1, Z_PAD), 1) < RANK


def _inproj_fwd(x, g1, w_in_t):
    T = x.shape[0]
    tm = min(T, TOKEN_TILE)
    nt = T // tm

    def body(g_ref, w_ref, x_hbm, proj_ref, xn_ref, x_buf, sems):
        step = pl.program_id(0)

        def fetch(at):
            start = at * tm if isinstance(at, int) else pl.multiple_of(at * tm, tm)
            return pltpu.make_async_copy(x_hbm.at[pl.ds(start, tm), :], x_buf.at[at % RING], sems.at[at % RING])

        @pl.when(step == 0)
        def _():
            for at in range(min(RING - 1, nt)):
                fetch(at).start()

        @pl.when(step + (RING - 1) < nt)
        def _():
            fetch(step + (RING - 1)).start()

        fetch(step).wait()
        xv = x_buf[step % RING]
        r = lax.rsqrt(_rowmean(xv * xv) + EPS)
        xn = (xv * r * g_ref[...]).astype(BF16)
        xn_ref[...] = xn
        proj_ref[:, 0:P_CI] = _nt(xn, w_ref[0:OFF_Z, :]).astype(BF16)
        proj_ref[:, P_CI:P_Z] = _nt(xn, w_ref[OFF_C:D_IN, :]).astype(BF16)
        proj_ref[:, P_Z:] = jnp.where(_z_lanes(), _nt(xn, w_ref[OFF_Z:OFF_Z + Z_PAD, :]), 0.0).astype(BF16)

    return pl.pallas_call(
        body, name="inproj_fwd", grid=(T // tm,),
        out_shape=[jax.ShapeDtypeStruct((T, D_INP), BF16), jax.ShapeDtypeStruct((T, D_MODEL), BF16)],
        in_specs=[_const((1, D_MODEL)), _const((D_IN, D_MODEL)), ANY],
        out_specs=[pl.BlockSpec((tm, D_INP), lambda i: (i, 0)), pl.BlockSpec((tm, D_MODEL), lambda i: (i, 0))],
        scratch_shapes=[pltpu.VMEM((RING, tm, D_MODEL), F32), pltpu.SemaphoreType.DMA((RING,))],
        compiler_params=_params(("arbitrary",)),
    )(g1, w_in_t, x)


def _head_masks():
    lane = lax.broadcasted_iota(jnp.int32, (1, KEY), 1)
    return [((lane >= h * DK) & (lane < (h + 1) * DK)).astype(F32) for h in range(HEADS)]


class _Mats(NamedTuple):
    tri: jax.Array
    tri_t: jax.Array
    same: jax.Array
    mid: jax.Array
    causal: jax.Array
    causal_t: jax.Array
    heads: jax.Array


def _chunk_matrices():
    r = lax.broadcasted_iota(jnp.int32, (SUB, SUB), 0)
    c = lax.broadcasted_iota(jnp.int32, (SUB, SUB), 1)
    shift = CHUNK.bit_length() - 1
    same = jnp.right_shift(r, shift) == jnp.right_shift(c, shift)
    causal = same & (r >= c)
    causal_t = same & (r <= c)
    mid = same & ((c & (CHUNK - 1)) < CHUNK // 2)
    hr = jnp.right_shift(lax.broadcasted_iota(jnp.int32, (VAL, KEY), 0), DV.bit_length() - 1)
    hc = jnp.right_shift(lax.broadcasted_iota(jnp.int32, (VAL, KEY), 1), DK.bit_length() - 1)
    return _Mats(tri=causal.astype(BF16), tri_t=causal_t.astype(BF16), same=same.astype(BF16), mid=mid.astype(BF16),
                 causal=causal, causal_t=causal_t, heads=hr == hc)


class _Decay(NamedTuple):
    al: jax.Array
    q: jax.Array
    k: jax.Array
    eb: jax.Array
    ebm: jax.Array
    emb: jax.Array
    elb: jax.Array
    ebl: jax.Array


def _decay_terms(z, q, k, wg, bg, mats):
    al = _nn(z, wg) + bg
    la = (jnp.minimum(al, 0.0) - jnp.log(1.0 + jnp.exp(-jnp.abs(al)))) * (1.0 / GATE_TAU)
    hi, lo = _split_bf16(la)
    cum = lambda m: _nn(m, hi) + _nn(m, lo)
    b, b_last, b_mid = cum(mats.tri), cum(mats.same), cum(mats.mid)
    return _Decay(al=al, q=q.astype(F32) * Q_SCALE, k=k.astype(F32), eb=jnp.exp(b), ebm=jnp.exp(b - b_mid),
                  emb=jnp.exp(b_mid - b), elb=jnp.exp(b_last - b), ebl=jnp.exp(b_last))


def _gla_fwd_tile(q_ref, k_ref, v_ref, g_ref, z_ref, wg_ref, bg_ref, gn_ref, mix_ref, o_ref, st_ref, state, tb):
    @pl.when(pl.program_id(0) == 0)
    def _():
        state[...] = jnp.zeros_like(state)

    mats = _chunk_matrices()
    masks = _head_masks()
    wgv, bgv = wg_ref[...], bg_ref[...]

    for sb in range(tb // SUB):
        rows = slice(sb * SUB, (sb + 1) * SUB)
        d = _decay_terms(z_ref[rows, :], q_ref[rows, :], k_ref[rows, :], wgv, bgv, mats)
        kem_b = (d.k * d.emb).astype(BF16)
        qem = d.q * d.ebm
        for h in range(HEADS):
            cols = slice(h * DV, (h + 1) * DV)
            a = jnp.where(mats.causal, _nt((qem * masks[h]).astype(BF16), kem_b), 0.0)
            o_ref[rows, cols] = _nn(a.astype(BF16), v_ref[rows, cols])
        qe0_b = (d.q * d.eb).astype(BF16)
        kdec_b = (d.k * d.elb).astype(BF16)
        for c in range(SUB // CHUNK):
            loc = slice(c * CHUNK, (c + 1) * CHUNK)
            glob = slice(sb * SUB + c * CHUNK, sb * SUB + (c + 1) * CHUNK)
            st = state[...]
            st_b = st.astype(BF16)
            st_ref[sb * (SUB // CHUNK) + c] = st_b
            o_ref[glob, :] += _nt(qe0_b[loc], st_b)
            u = _tn(v_ref[glob, :], kdec_b[loc])
            state[...] = st * d.ebl[c * CHUNK:c * CHUNK + 1] + jnp.where(mats.heads, u, 0.0)

    gnv = gn_ref[...]
    for h in range(HEADS):
        cols = slice(h * DV, (h + 1) * DV)
        oh = o_ref[:, cols]
        r = lax.rsqrt(_rowmean(oh * oh) + EPS)
        gh = g_ref[:, cols].astype(F32)
        mix_ref[:, cols] = (oh * r * gnv * (gh * jax.nn.sigmoid(gh))).astype(BF16)


def _group_mean(v, gmat):
    return _nn(v.astype(BF16), gmat)


def _shifted_copies(buf, sh, rows):
    for k in range(1, SUBLANES):
        sh[k - 1] = buf[pl.ds(k, rows), :]


def _tap(buf, sh, off, r0, n):
    k, base = off % SUBLANES, off - off % SUBLANES
    rows = pl.ds(r0 + base if isinstance(r0, int) else pl.multiple_of(r0 + base, SUBLANES), n)
    return buf[rows, :] if k == 0 else sh[k - 1, rows, :]


def _conv_fwd_tile(ci_ref, cg_ref, w_ref, b_ref, g_ref, be_ref, gm_ref, mix_ref, uc_ref, ubuf, ush, tm):
    sh_rows = tm + HALO - SUBLANES

    @pl.when(pl.program_id(0) == 0)
    def _():
        ubuf[0:HALO, :] = jnp.zeros((HALO, CONV), F32)

    ubuf[HALO:, :] = ci_ref[...].astype(F32) * jax.nn.sigmoid(cg_ref[...].astype(F32))
    _shifted_copies(ubuf, ush, sh_rows)
    for s in range(tm // FWD_STRIP):
        acc = jnp.zeros((FWD_STRIP, CONV), F32) + b_ref[...]
        for j in range(CONV_W):
            acc = acc + w_ref[j:j + 1, :] * _tap(ubuf, ush, HALO - (CONV_W - 1) + j, s * FWD_STRIP, FWD_STRIP)
        uc_ref[s * FWD_STRIP:(s + 1) * FWD_STRIP, :] = acc
    ubuf[0:HALO, :] = ubuf[tm:tm + HALO, :]
    gm = gm_ref[...]
    ucv = uc_ref[...]
    d = ucv - _group_mean(ucv, gm)
    var = _group_mean(d * d, gm)
    yn = d * lax.rsqrt(var + EPS) * g_ref[...] + be_ref[...]
    mix_ref[...] = (yn * jax.nn.sigmoid(yn)).astype(BF16)


def _mix_fwd(proj, wg, bg, gn, conv_w, conv_b, cn_g, cn_b, gmat, shards):
    T = proj.shape[0]
    tb = min(T, TOKEN_TILE)
    cpb = tb // CHUNK
    n_comm = len(shards)
    kinds = ["gather"] * n_comm
    comm_start, comm_wait = _hosted_comm(kinds, 15, 5, n_comm, T // tb)

    def body(*refs):
        gla_in, conv_in = refs[:8], refs[8:15]
        gla_out, conv_out = refs[15 + n_comm:18 + n_comm], refs[18 + n_comm:20 + n_comm]
        state, ubuf, ush = refs[20 + 2 * n_comm:23 + 2 * n_comm]
        comm_start(refs)
        _gla_fwd_tile(*gla_in, *gla_out, state, tb)
        _conv_fwd_tile(*conv_in, *conv_out, ubuf, ush, tb)
        comm_wait(refs)

    nc = T // CHUNK
    tok = lambda w, col: pl.BlockSpec((tb, w), lambda i: (i, col))
    res = pl.pallas_call(
        body, name="mix_fwd", grid=(T // tb,),
        out_shape=[jax.ShapeDtypeStruct((T, VAL), BF16), jax.ShapeDtypeStruct((T, VAL), F32),
                   jax.ShapeDtypeStruct((nc, VAL, KEY), BF16), jax.ShapeDtypeStruct((T, CONV), BF16),
                   jax.ShapeDtypeStruct((T, CONV), F32)] + _comm_out_shapes(kinds, shards),
        in_specs=[tok(KEY, P_Q // KEY), tok(KEY, P_K // KEY), tok(VAL, P_V // VAL), tok(VAL, P_G // VAL),
                  tok(Z_PAD, P_Z // Z_PAD), _const((Z_PAD, KEY)), _const((1, KEY)), _const((1, DV)),
                  tok(CONV, P_CI // CONV), tok(CONV, P_CG // CONV), _const((HALO, CONV)), _const((1, CONV)),
                  _const((1, CONV)), _const((1, CONV)), _const((CONV, CONV))] + [ANY] * n_comm,
        out_specs=[tok(VAL, 0), tok(VAL, 0), pl.BlockSpec((cpb, VAL, KEY), lambda i: (i, 0, 0)), tok(CONV, 0),
                   tok(CONV, 0)] + [ANY] * n_comm,
        scratch_shapes=[pltpu.VMEM((VAL, KEY), F32), pltpu.VMEM((tb + HALO, CONV), F32),
                        pltpu.VMEM((SUBLANES - 1, tb + HALO - SUBLANES, CONV), F32)] + _comm_scratch(n_comm),
        compiler_params=_params(("arbitrary",)),
    )(proj, proj, proj, proj, proj, wg, bg, gn, proj, proj, conv_w, conv_b, cn_g, cn_b, gmat, *shards)
    return res[0], res[1], res[2], res[3], res[4], res[5:]


def _rms_bwd(dy, xhat, r, g):
    dyg = dy * g
    return r * (dyg - xhat * _rowmean(dyg * xhat))


def _mlp_fwd_bwd(x, mix_a, mix_c, tgt, w_out, g2, w1t, w2, gf):
    T = x.shape[0]
    tm = min(T, MLP_TILE)
    inv_d = 1.0 / D_MODEL

    def body(x_ref, ma_ref, mc_ref, t_ref, wo_ref, g2_ref, w1_ref, w2_ref, gf_ref,
             dh1_ref, dmix_ref, hn_ref, ff_ref, da_ref, dh2_ref, loss_ref, dgf_ref, dg2_ref):
        @pl.when(pl.program_id(0) == 0)
        def _():
            loss_ref[...] = jnp.zeros_like(loss_ref)
            dgf_ref[...] = jnp.zeros_like(dgf_ref)
            dg2_ref[...] = jnp.zeros_like(dg2_ref)

        g2v, gfv = g2_ref[...], gf_ref[...]
        h1 = x_ref[...] + _nn(ma_ref[...], wo_ref[0:VAL, :]) + _nn(mc_ref[...], wo_ref[VAL:, :])
        r2 = lax.rsqrt(_rowmean(h1 * h1) + EPS)
        h1hat = h1 * r2
        hn = (h1hat * g2v).astype(BF16)
        hn_ref[...] = hn
        relu_a = jnp.maximum(_nt(hn, w1_ref[...]), 0.0)
        ff = (relu_a * relu_a).astype(BF16)
        ff_ref[...] = ff
        h2 = h1 + _nn(ff, w2_ref[...])
        rf = lax.rsqrt(_rowmean(h2 * h2) + EPS)
        h2hat = h2 * rf
        err = h2hat * gfv - t_ref[...]
        loss_ref[...] += (0.5 * inv_d) * _colsum(jnp.sum(err * err, axis=1, keepdims=True))
        dy = err * inv_d
        dgf_ref[...] += _colsum(dy * h2hat)
        dh2 = _rms_bwd(dy, h2hat, rf, gfv)
        dh2_b = dh2.astype(BF16)
        dh2_ref[...] = dh2_b
        da = (_nt(dh2_b, w2_ref[...]) * (2.0 * relu_a)).astype(BF16)
        da_ref[...] = da
        dhn = _nn(da, w1_ref[...])
        dg2_ref[...] += _colsum(dhn * h1hat)
        dh1 = dh2 + _rms_bwd(dhn, h1hat, r2, g2v)
        dh1_ref[...] = dh1
        dmix_ref[...] = _nt(dh1.astype(BF16), wo_ref[...]).astype(BF16)

    tok = lambda w: pl.BlockSpec((tm, w), lambda i: (i, 0))
    return pl.pallas_call(
        body, name="mlp_fwd_bwd", grid=(T // tm,),
        out_shape=[jax.ShapeDtypeStruct((T, D_MODEL), F32), jax.ShapeDtypeStruct((T, D_MODEL), BF16),
                   jax.ShapeDtypeStruct((T, D_MODEL), BF16), jax.ShapeDtypeStruct((T, D_FF), BF16),
                   jax.ShapeDtypeStruct((T, D_FF), BF16), jax.ShapeDtypeStruct((T, D_MODEL), BF16),
                   jax.ShapeDtypeStruct((1, 1), F32), jax.ShapeDtypeStruct((1, D_MODEL), F32),
                   jax.ShapeDtypeStruct((1, D_MODEL), F32)],
        in_specs=[tok(D_MODEL), tok(VAL), tok(CONV), tok(D_MODEL), _const((D_MODEL, D_MODEL)), _const((1, D_MODEL)),
                  _const((D_FF, D_MODEL)), _const((D_FF, D_MODEL)), _const((1, D_MODEL))],
        out_specs=[tok(D_MODEL), tok(D_MODEL), tok(D_MODEL), tok(D_FF), tok(D_FF), tok(D_MODEL),
                   pl.BlockSpec((1, 1), lambda i: (0, 0)), pl.BlockSpec((1, D_MODEL), lambda i: (0, 0)),
                   pl.BlockSpec((1, D_MODEL), lambda i: (0, 0))],
        compiler_params=_params(("arbitrary",)),
    )(x, mix_a, mix_c, tgt, w_out, g2, w1t, w2, gf)


def _silu_grad(v, s):
    return s * (1.0 + v * (1.0 - s))


def _conv_bwd_tile(ci_ref, cg_ref, uc_ref, dm_ref, w_ref, g_ref, be_ref, gm_ref,
                   dpc_ref, dw_ref, db_ref, dg_ref, dbe_ref, dbuf, dsh, dwacc, tm, nt):
    step = pl.program_id(0)
    sh_rows = tm + HALO - SUBLANES

    @pl.when(step == 0)
    def _():
        dbuf[tm:, :] = jnp.zeros((HALO, CONV), F32)
        dwacc[...] = jnp.zeros_like(dwacc)
        db_ref[...] = jnp.zeros_like(db_ref)
        dg_ref[...] = jnp.zeros_like(dg_ref)
        dbe_ref[...] = jnp.zeros_like(dbe_ref)

    gm, gv = gm_ref[...], g_ref[...]
    ucv = uc_ref[...]
    d = ucv - _group_mean(ucv, gm)
    rs = lax.rsqrt(_group_mean(d * d, gm) + EPS)
    yhat = d * rs
    yn = yhat * gv + be_ref[...]
    dyn = dm_ref[...].astype(F32) * _silu_grad(yn, jax.nn.sigmoid(yn))
    dg_ref[...] += _colsum(dyn * yhat)
    dbe_ref[...] += _colsum(dyn)
    dyh = dyn * gv
    duc = rs * (dyh - _group_mean(dyh, gm) - yhat * _group_mean(dyh * yhat, gm))
    db_ref[...] += _colsum(duc)
    dbuf[0:tm, :] = duc
    _shifted_copies(dbuf, dsh, sh_rows)

    def strip(s, carry):
        r0 = pl.multiple_of(s * STRIP, STRIP)
        rows = pl.ds(r0, STRIP)
        cin = ci_ref[rows, :].astype(F32)
        sg = jax.nn.sigmoid(cg_ref[rows, :].astype(F32))
        u = cin * sg
        du = jnp.zeros((STRIP, CONV), F32)
        for j in range(CONV_W):
            dj = _tap(dbuf, dsh, CONV_W - 1 - j, r0, STRIP)
            du = du + w_ref[j:j + 1, :] * dj
            p = u * dj
            fold = p[0:SUBLANES]
            for q in range(1, STRIP // SUBLANES):
                fold = fold + p[q * SUBLANES:(q + 1) * SUBLANES, :]
            dwacc[j * SUBLANES:(j + 1) * SUBLANES, :] += fold
        dpc_ref[rows, 0:CONV] = (du * sg).astype(BF16)
        dpc_ref[rows, CONV:] = (du * cin * sg * (1.0 - sg)).astype(BF16)
        return carry

    lax.fori_loop(0, tm // STRIP, strip, 0)
    dbuf[tm:, :] = dbuf[0:HALO, :]

    @pl.when(step == nt - 1)
    def _():
        dw_ref[...] = jnp.zeros_like(dw_ref)
        for j in range(CONV_W):
            dw_ref[j:j + 1, :] = _colsum(dwacc[j * SUBLANES:(j + 1) * SUBLANES, :])


def _conv_bwd(proj, uc, dmix, conv_w, cn_g, cn_b, gmat):
    T = proj.shape[0]
    tm = min(T, TOKEN_TILE)
    nt = T // tm
    sh_rows = tm + HALO - SUBLANES

    def body(*refs):
        _conv_bwd_tile(*refs, tm, nt)

    rev = lambda i: nt - 1 - i
    tile = lambda col: pl.BlockSpec((tm, CONV), lambda i: (rev(i), col))
    acc = lambda rows: pl.BlockSpec((rows, CONV), lambda i: (0, 0))
    return pl.pallas_call(
        body, name="conv_bwd", grid=(nt,),
        out_shape=[jax.ShapeDtypeStruct((T, 2 * CONV), BF16),
                   jax.ShapeDtypeStruct((HALO, CONV), F32), jax.ShapeDtypeStruct((1, CONV), F32),
                   jax.ShapeDtypeStruct((1, CONV), F32), jax.ShapeDtypeStruct((1, CONV), F32)],
        in_specs=[tile(P_CI // CONV), tile(P_CG // CONV), tile(0), tile(1),
                  _const((HALO, CONV)), _const((1, CONV)), _const((1, CONV)), _const((CONV, CONV))],
        out_specs=[pl.BlockSpec((tm, 2 * CONV), lambda i: (rev(i), 0)), acc(HALO), acc(1), acc(1), acc(1)],
        scratch_shapes=[pltpu.VMEM((tm + HALO, CONV), F32), pltpu.VMEM((SUBLANES - 1, sh_rows, CONV), F32),
                        pltpu.VMEM((HALO * SUBLANES, CONV), F32)],
        compiler_params=_params(("arbitrary",)),
    )(proj, proj, uc, dmix, conv_w, cn_g, cn_b, gmat)


def _gla_bwd_tile(q_ref, k_ref, v_ref, g_ref, z_ref, o_ref, st_ref, dm_ref, wg_ref, bg_ref, gn_ref,
                  dpg_ref, dwg_ref, dbg_ref, dgn_ref, dstate, do_scr, dv_scr, tb):
    @pl.when(pl.program_id(0) == 0)
    def _():
        dstate[...] = jnp.zeros_like(dstate)
        dwg_ref[...] = jnp.zeros_like(dwg_ref)
        dbg_ref[...] = jnp.zeros_like(dbg_ref)
        dgn_ref[...] = jnp.zeros_like(dgn_ref)

    gnv = gn_ref[...]
    dgn = jnp.zeros((1, DV), F32)
    for h in range(HEADS):
        cols = slice(h * DV, (h + 1) * DV)
        oh = o_ref[:, cols]
        r = lax.rsqrt(_rowmean(oh * oh) + EPS)
        ohat = oh * r
        gh = g_ref[:, cols].astype(F32)
        sg = jax.nn.sigmoid(gh)
        dmx = dm_ref[:, cols].astype(F32)
        don = dmx * (gh * sg)
        dpg_ref[:, P_G + h * DV:P_G + (h + 1) * DV] = (dmx * (ohat * gnv) * _silu_grad(gh, sg)).astype(BF16)
        dgn = dgn + _colsum(don * ohat)
        do_scr[:, cols] = _rms_bwd(don, ohat, r, gnv)
    dgn_ref[...] += dgn

    mats = _chunk_matrices()
    masks = _head_masks()
    wgv, bgv = wg_ref[...], bg_ref[...]
    n_chunks = SUB // CHUNK

    for sb in reversed(range(tb // SUB)):
        rows = slice(sb * SUB, (sb + 1) * SUB)
        zs = z_ref[rows, :]
        d = _decay_terms(zs, q_ref[rows, :], k_ref[rows, :], wgv, bgv, mats)
        qem = d.q * d.ebm
        qem_b = qem.astype(BF16)
        kem_b = (d.k * d.emb).astype(BF16)
        dq = jnp.zeros((SUB, KEY), F32)
        dk = jnp.zeros((SUB, KEY), F32)
        for h in range(HEADS):
            hm = masks[h]
            cols = slice(h * DV, (h + 1) * DV)
            do_b = do_scr[rows, cols].astype(BF16)
            vh = v_ref[rows, cols]
            da = jnp.where(mats.causal, _nt(do_b, vh), 0.0).astype(BF16)
            da_t = jnp.where(mats.causal_t, _nt(vh, do_b), 0.0).astype(BF16)
            a_t = jnp.where(mats.causal_t, _nt(kem_b, (qem * hm).astype(BF16)), 0.0).astype(BF16)
            dq = dq + hm * _nn(da, kem_b)
            dk = dk + hm * _nn(da_t, qem_b)
            dv_scr[rows, cols] = _nn(a_t, do_b)
        dq = dq * d.ebm
        dk = dk * d.emb

        qe0_b = (d.q * d.eb).astype(BF16)
        kdec_b = (d.k * d.elb).astype(BF16)
        dq_st, dk_st, last = [None] * n_chunks, [None] * n_chunks, [None] * n_chunks
        for c in reversed(range(n_chunks)):
            loc = slice(c * CHUNK, (c + 1) * CHUNK)
            glob = slice(sb * SUB + c * CHUNK, sb * SUB + (c + 1) * CHUNK)
            st_b = st_ref[sb * n_chunks + c]
            ds = dstate[...]
            ds_b = ds.astype(BF16)
            do_c = do_scr[glob, :].astype(BF16)
            ebl_c = d.ebl[c * CHUNK:c * CHUNK + 1]
            dk_c = _nn(v_ref[glob, :], ds_b) * d.elb[loc]
            dq_st[c] = _nn(do_c, st_b) * d.eb[loc]
            dk_st[c] = dk_c
            last_c = _colsum(d.k[loc] * dk_c) + ebl_c * _colsum(st_b.astype(F32) * ds)
            last[c] = jnp.broadcast_to(last_c, (CHUNK, KEY))
            dpg_ref[glob, P_V:P_G] = (dv_scr[glob, :] + _nt(kdec_b[loc], ds_b)).astype(BF16)
            dstate[...] = ds * ebl_c + jnp.where(mats.heads, _tn(do_c, qe0_b[loc]), 0.0)
        dq = dq + jnp.concatenate(dq_st, axis=0)
        dk = dk + jnp.concatenate(dk_st, axis=0)
        dpg_ref[rows, P_Q:P_K] = (dq * Q_SCALE).astype(BF16)
        dpg_ref[rows, P_K:P_V] = dk.astype(BF16)
        hi, lo = _split_bf16(d.q * dq - d.k * dk)
        dla = _nn(mats.tri_t, hi) + _nn(mats.tri_t, lo) + jnp.concatenate(last, axis=0)
        dal = dla * (1.0 / GATE_TAU) * jax.nn.sigmoid(-d.al)
        dal_b = dal.astype(BF16)
        dpg_ref[rows, OFF_Z:] = _nt(dal_b, wgv).astype(BF16)
        dwg_ref[...] += _tn(zs, dal_b)
        dbg_ref[...] += _colsum(dal)


def _gla_bwd(proj, o, states, dmix, wg, bg, gn, parts):
    T = proj.shape[0]
    tb = min(T, TOKEN_TILE)
    cpb = tb // CHUNK
    nb = T // tb
    n_comm = len(parts)
    kinds = ["exchange"] * n_comm
    comm_start, comm_wait = _hosted_comm(kinds, 11, 4, n_comm, nb)

    def body(*refs):
        comm_start(refs)
        _gla_bwd_tile(*refs[:11], *refs[11 + n_comm:15 + n_comm], *refs[15 + 2 * n_comm:18 + 2 * n_comm], tb)
        comm_wait(refs)

    rev = lambda i: nb - 1 - i
    blk = lambda w, col: pl.BlockSpec((tb, w), lambda i: (rev(i), col))
    res = pl.pallas_call(
        body, name="gla_bwd", grid=(nb,),
        out_shape=[jax.ShapeDtypeStruct((T, D_GLA), BF16), jax.ShapeDtypeStruct((Z_PAD, KEY), F32),
                   jax.ShapeDtypeStruct((1, KEY), F32), jax.ShapeDtypeStruct((1, DV), F32)]
        + _comm_out_shapes(kinds, parts),
        in_specs=[blk(KEY, P_Q // KEY), blk(KEY, P_K // KEY), blk(VAL, P_V // VAL), blk(VAL, P_G // VAL),
                  blk(Z_PAD, P_Z // Z_PAD), blk(VAL, 0),
                  pl.BlockSpec((cpb, VAL, KEY), lambda i: (rev(i), 0, 0)), blk(VAL, 0),
                  _const((Z_PAD, KEY)), _const((1, KEY)), _const((1, DV))] + [ANY] * n_comm,
        out_specs=[blk(D_GLA, 0),
                   pl.BlockSpec((Z_PAD, KEY), lambda i: (0, 0)), pl.BlockSpec((1, KEY), lambda i: (0, 0)),
                   pl.BlockSpec((1, DV), lambda i: (0, 0))] + [ANY] * n_comm,
        scratch_shapes=[pltpu.VMEM((VAL, KEY), F32), pltpu.VMEM((tb, VAL), F32), pltpu.VMEM((tb, VAL), F32)]
        + _comm_scratch(n_comm),
        compiler_params=_params(("arbitrary",)),
    )(proj, proj, proj, proj, proj, o, states, dmix, wg, bg, gn, *parts)
    return res[:4], res[4:]


def _inproj_bwd(x, g1, w_in_t, dh1, dp_gla, dp_conv):
    T = x.shape[0]
    tm = min(T, TOKEN_TILE)
    nt = T // tm
    streams = [x, dh1, dp_gla, dp_conv]

    def body(g_ref, w_ref, x_hbm, dh1_hbm, dpg_hbm, dpc_hbm, dx_ref, dg1_ref, x_buf, dh1_buf, dpg_buf, dpc_buf, sems):
        step = pl.program_id(0)
        hbm, bufs = [x_hbm, dh1_hbm, dpg_hbm, dpc_hbm], [x_buf, dh1_buf, dpg_buf, dpc_buf]

        def fetch(at, k):
            start = at * tm if isinstance(at, int) else pl.multiple_of(at * tm, tm)
            return pltpu.make_async_copy(hbm[k].at[pl.ds(start, tm), :], bufs[k].at[at % RING], sems.at[k, at % RING])

        @pl.when(step == 0)
        def _():
            dg1_ref[...] = jnp.zeros_like(dg1_ref)
            for at in range(min(RING - 1, nt)):
                for k in range(len(streams)):
                    fetch(at, k).start()

        @pl.when(step + (RING - 1) < nt)
        def _():
            for k in range(len(streams)):
                fetch(step + (RING - 1), k).start()

        for k in range(len(streams)):
            fetch(step, k).wait()
        slot = step % RING
        dxn = _nn(dpg_buf[slot], w_ref[0:D_GLA, :]) + _nn(dpc_buf[slot], w_ref[OFF_C:D_IN, :])
        xv = x_buf[slot]
        r = lax.rsqrt(_rowmean(xv * xv) + EPS)
        xhat = xv * r
        dg1_ref[...] += _colsum(dxn * xhat)
        dx_ref[...] = dh1_buf[slot] + _rms_bwd(dxn, xhat, r, g_ref[...])

    return pl.pallas_call(
        body, name="inproj_bwd", grid=(nt,),
        out_shape=[jax.ShapeDtypeStruct((T, D_MODEL), F32), jax.ShapeDtypeStruct((1, D_MODEL), F32)],
        in_specs=[_const((1, D_MODEL)), _const((D_IN, D_MODEL))] + [ANY] * len(streams),
        out_specs=[pl.BlockSpec((tm, D_MODEL), lambda i: (i, 0)), pl.BlockSpec((1, D_MODEL), lambda i: (0, 0))],
        scratch_shapes=[pltpu.VMEM((RING, tm, s.shape[1]), s.dtype) for s in streams]
        + [pltpu.SemaphoreType.DMA((len(streams), RING))],
        compiler_params=_params(("arbitrary",)),
    )(g1, w_in_t, x, dh1, dp_gla, dp_conv)


def _wgrad_in(xn, dp_gla, dp_conv):
    T = xn.shape[0]
    tt = min(T, WGRAD_TILE // 2)
    nt = T // tt

    def body(xn_ref, dpg_ref, dpc_ref, o_ref, acc):
        @pl.when(pl.program_id(0) == 0)
        def _():
            acc[...] = jnp.zeros_like(acc)

        xv = xn_ref[...]
        acc[0:OFF_C, :] += _tn(dpg_ref[...], xv)[0:OFF_C]
        acc[OFF_C:, :] += _tn(dpc_ref[...], xv)

        @pl.when(pl.program_id(0) == nt - 1)
        def _():
            o_ref[...] = acc[...].astype(BF16)

    tok = lambda w: pl.BlockSpec((tt, w), lambda t: (t, 0))
    return pl.pallas_call(
        body, name="wgrad_in", grid=(nt,), out_shape=jax.ShapeDtypeStruct((D_IN, D_MODEL), BF16),
        in_specs=[tok(D_MODEL), tok(D_GLA), tok(2 * CONV)],
        out_specs=pl.BlockSpec((D_IN, D_MODEL), lambda t: (0, 0), pipeline_mode=pl.Buffered(1)),
        scratch_shapes=[pltpu.VMEM((D_IN, D_MODEL), F32)],
        compiler_params=_params(("arbitrary",)),
    )(xn, dp_gla, dp_conv)


def _wgrad_out(mix_a, mix_c, dh1):
    T = dh1.shape[0]
    tt = min(T, WGRAD_TILE // 2)
    nt = T // tt

    def body(a_ref, c_ref, b_ref, o_ref, acc):
        @pl.when(pl.program_id(0) == 0)
        def _():
            acc[...] = jnp.zeros_like(acc)

        b = b_ref[...].astype(BF16)
        acc[0:VAL, :] += _tn(a_ref[...], b)
        acc[VAL:, :] += _tn(c_ref[...], b)

        @pl.when(pl.program_id(0) == nt - 1)
        def _():
            o_ref[...] = acc[...].astype(BF16)

    tok = lambda w: pl.BlockSpec((tt, w), lambda t: (t, 0))
    return pl.pallas_call(
        body, name="wgrad_out", grid=(nt,), out_shape=jax.ShapeDtypeStruct((D_MODEL, D_MODEL), BF16),
        in_specs=[tok(VAL), tok(CONV), tok(D_MODEL)],
        out_specs=pl.BlockSpec((D_MODEL, D_MODEL), lambda t: (0, 0)),
        scratch_shapes=[pltpu.VMEM((D_MODEL, D_MODEL), F32)],
        compiler_params=_params(("arbitrary",)),
    )(mix_a, mix_c, dh1)


def _wgrad(a, b, name, tk, tn, col_block=None):
    T, K = a.shape
    N = b.shape[1]
    tt = min(T, WGRAD_TILE)
    nt = T // tt

    def body(a_ref, b_ref, o_ref, acc):
        @pl.when(pl.program_id(2) == 0)
        def _():
            acc[...] = jnp.zeros_like(acc)

        acc[...] += _tn(a_ref[...], b_ref[...].astype(BF16))

        @pl.when(pl.program_id(2) == nt - 1)
        def _():
            if col_block is None:
                o_ref[...] = acc[...].astype(BF16)
            else:
                for q in range(tn // col_block):
                    o_ref[q] = acc[:, q * col_block:(q + 1) * col_block].astype(BF16)

    if col_block is None:
        out_shape = jax.ShapeDtypeStruct((K, N), BF16)
        out_spec = pl.BlockSpec((tk, tn), lambda i, j, t: (i, j))
    else:
        assert tk == K
        out_shape = jax.ShapeDtypeStruct((N // col_block, K, col_block), BF16)
        out_spec = pl.BlockSpec((tn // col_block, tk, col_block), lambda i, j, t: (j, 0, 0))
    return pl.pallas_call(
        body, name=name, grid=(K // tk, N // tn, nt), out_shape=out_shape,
        in_specs=[pl.BlockSpec((tt, tk), lambda i, j, t: (t, i)), pl.BlockSpec((tt, tn), lambda i, j, t: (t, j))],
        out_specs=out_spec, scratch_shapes=[pltpu.VMEM((tk, tn), F32)],
        compiler_params=_params(("arbitrary", "arbitrary", "arbitrary")),
    )(a, b)


def _adam_math(w, g, m, v):
    m = ADAM_B1 * m + (1.0 - ADAM_B1) * g
    v = ADAM_B2 * v + (1.0 - ADAM_B2) * (g * g)
    m_hat = m / (1.0 - ADAM_B1 ** ADAM_STEP)
    v_hat = v / (1.0 - ADAM_B2 ** ADAM_STEP)
    delta = -ADAM_LR * (m_hat / (jnp.sqrt(v_hat) + ADAM_EPS) + ADAM_WD * w)
    return delta, m, v


def _sum8(ref):
    g = ref[0].astype(F32)
    for s in range(1, N_DEV):
        g = g + ref[s].astype(F32)
    return g


def _adam_big(parts, w, m, v, name):
    R, C = w.shape
    tr = ADAM_ROWS if R % ADAM_ROWS == 0 else R

    def body(p_ref, w_ref, m_ref, v_ref, g_ref, d_ref, nm_ref, nv_ref):
        g = _sum8(p_ref)
        g_ref[...] = g
        d_ref[...], nm_ref[...], nv_ref[...] = _adam_math(w_ref[...], g, m_ref[...], v_ref[...])

    row = pl.BlockSpec((tr, C), lambda i: (i, 0))
    return pl.pallas_call(
        body, name=name, grid=(R // tr,), out_shape=[jax.ShapeDtypeStruct((R, C), F32)] * 4,
        in_specs=[pl.BlockSpec((N_DEV, tr, C), lambda i: (0, i, 0)), row, row, row], out_specs=[row] * 4,
        compiler_params=_params(("arbitrary",)),
    )(parts, w, m, v)


def _sum_small(parts):
    def body(p_ref, o_ref):
        o_ref[...] = _sum8(p_ref)

    return pl.pallas_call(body, name="sum_small", out_shape=jax.ShapeDtypeStruct(parts.shape[1:], F32))(parts)


def _adam_small(gs, ws, ms, vs):
    n = len(gs)

    def body(*refs):
        g_refs, w_refs, m_refs, v_refs = refs[:n], refs[n:2 * n], refs[2 * n:3 * n], refs[3 * n:4 * n]
        outs = refs[4 * n:]
        for i in range(n):
            d, nm, nv = _adam_math(w_refs[i][...], g_refs[i][...], m_refs[i][...], v_refs[i][...])
            outs[i][...] = d
            outs[n + i][...] = nm
            outs[2 * n + i][...] = nv

    shapes = [jax.ShapeDtypeStruct(w.shape, F32) for w in ws]
    res = pl.pallas_call(body, name="adam_small", out_shape=shapes * 3)(*gs, *ws, *ms, *vs)
    return res[:n], res[n:2 * n], res[2 * n:]


def _group_matrix():
    gi = lax.broadcasted_iota(jnp.int32, (CONV, CONV), 0) // (CONV // GROUPS)
    gj = lax.broadcasted_iota(jnp.int32, (CONV, CONV), 1) // (CONV // GROUPS)
    return jnp.where(gi == gj, GROUPS / CONV, 0.0).astype(BF16)


_SMALL = [("loss", 8), ("dg1", 8), ("dbg", 2), ("dgn", 1), ("dconv_b", 4), ("dcn_g", 4), ("dcn_b", 4), ("dg2", 8),
          ("dgf", 8), ("dwg", 32), ("dconv_w", 124)]


def _pad8(rows):
    return -(-rows // 8) * 8


def kernel(x, norm1_g, w_in, w_gate_up, b_gate, gla_norm_g, conv_w, conv_b, conv_norm_g, conv_norm_b, w_out, norm2_g, w_mlp_in, w_mlp_out, final_norm_g, loss_target, m_norm1_g, m_w_in, m_w_gate_up, m_b_gate, m_gla_norm_g, m_conv_w, m_conv_b, m_conv_norm_g, m_conv_norm_b, m_w_out, m_norm2_g, m_w_mlp_in, m_w_mlp_out, m_final_norm_g, v_norm1_g, v_w_in, v_w_gate_up, v_b_gate, v_gla_norm_g, v_conv_w, v_conv_b, v_conv_norm_g, v_conv_norm_b, v_w_out, v_norm2_g, v_w_mlp_in, v_w_mlp_out, v_final_norm_g):
    x_idx = lax.axis_index("x")
    y_idx = lax.axis_index("y")
    c_idx = lax.axis_index("c")
    me = 4 * x_idx + 2 * y_idx + c_idx
    xs, tgt = x[0], loss_target[0]
    gf = final_norm_g.reshape(1, D_MODEL)
    gmat = _group_matrix()

    small_shard = jnp.zeros((_pad8(RANK + CONV_W), LANES), F32)
    small_shard = small_shard.at[0:RANK, 0:KEY // N_DEV].set(w_gate_up[0])
    small_shard = small_shard.at[RANK:RANK + CONV_W, 0:CONV // N_DEV].set(conv_w[0])
    g_in, g_small = _gather_two_level([w_in[0].T.astype(BF16), small_shard], "gather_w_in")
    w_in_t = g_in.reshape(D_IN, D_MODEL)
    wg_full = jnp.concatenate([g_small[d, 0:RANK, 0:KEY // N_DEV] for d in range(N_DEV)], axis=1)
    wg_pad = jnp.pad(wg_full, ((0, Z_PAD - RANK), (0, 0))).astype(BF16)
    conv_w_full = jnp.concatenate([g_small[d, RANK:RANK + CONV_W, 0:CONV // N_DEV] for d in range(N_DEV)], axis=1)
    conv_w_pad = jnp.pad(conv_w_full, ((0, HALO - CONV_W), (0, 0)))

    proj, xn = _inproj_fwd(xs, norm1_g, w_in_t)
    mix_a, o, states, mix_c, uc, (g_out, g_w1, g_w2) = _mix_fwd(
        proj, wg_pad, b_gate, gla_norm_g, conv_w_pad, conv_b, conv_norm_g, conv_norm_b, gmat,
        [w_out[0].astype(BF16), w_mlp_in[0].T.astype(BF16), w_mlp_out[0].astype(BF16)])
    w_out_full = g_out.reshape(D_MODEL, D_MODEL)
    w1t_full = g_w1.reshape(D_FF, D_MODEL)
    w2_full = g_w2.reshape(D_FF, D_MODEL)
    dh1, dmix, hn, ff, da, dh2, loss, dgf, dg2 = _mlp_fwd_bwd(xs, mix_a, mix_c, tgt, w_out_full, norm2_g, w1t_full,
                                                              w2_full, gf)

    dw1 = _wgrad(hn, da, "wgrad_mlp_in", WGRAD_BLOCK, WGRAD_BLOCK, col_block=D_FF // N_DEV)
    dw2 = _wgrad(ff, dh2, "wgrad_mlp_out", WGRAD_BLOCK, WGRAD_BLOCK)
    dw_out = _wgrad_out(mix_a, mix_c, dh1)
    dp_conv, dconv_w, dconv_b, dcn_g, dcn_b = _conv_bwd(proj, uc, dmix, conv_w_pad, conv_norm_g, conv_norm_b, gmat)
    (dp_gla, dwg, dbg, dgn), (p_w1, p_w2, p_out) = _gla_bwd(
        proj, o, states, dmix, wg_pad, b_gate, gla_norm_g,
        [dw1, dw2.reshape(N_DEV, D_FF // N_DEV, D_MODEL), dw_out.reshape(N_DEV, D_MODEL // N_DEV, D_MODEL)])
    dw_in = _wgrad_in(xn, dp_gla, dp_conv).reshape(N_DEV, SHARD_IN, D_MODEL)
    send_sems, recv_sems, dw_in_thru, land, token = _split_start("exchange", dw_in, jnp.copy(dw_in),
                                                                 "exchange_w_in_start")
    dx, dg1 = _inproj_bwd(xs, norm1_g + token[0:1, 0:1], w_in_t, dh1, dp_gla, dp_conv)
    p_in = _split_wait("exchange", send_sems, recv_sems, dw_in_thru, land, dg1, "exchange_w_in_wait")

    small = dict(loss=jnp.zeros((SUBLANES, LANES), F32) + loss, dg1=dg1, dbg=dbg, dgn=dgn, dconv_b=dconv_b, dcn_g=dcn_g,
                 dcn_b=dcn_b, dg2=dg2, dgf=dgf, dwg=dwg[0:RANK], dconv_w=dconv_w[0:CONV_W])
    pack = jnp.concatenate([jnp.pad(small[name].reshape(rows, LANES), ((0, _pad8(rows) - rows), (0, 0)))
                            for name, rows in _SMALL], axis=0)
    s_send, s_recv, pack_thru, pack_land, s_token = _split_start(
        "gather", pack, jnp.broadcast_to(pack, (N_DEV,) + pack.shape) + 0.0, "gather_small_start")

    gi, di, mi, vi = _adam_big(p_in, w_in[0].T, m_w_in[0].T, v_w_in[0].T, "adam_w_in")
    go, do, mo, vo = _adam_big(p_out, w_out[0] + s_token[0:1, 0:1], m_w_out[0], v_w_out[0], "adam_w_out")
    ga, da_, ma, va = _adam_big(p_w1, w_mlp_in[0], m_w_mlp_in[0], v_w_mlp_in[0], "adam_w_mlp_in")
    gb, db, mb, vb = _adam_big(p_w2, w_mlp_out[0], m_w_mlp_out[0], v_w_mlp_out[0], "adam_w_mlp_out")
    cut = lambda a: a.T[None]

    summed = _split_wait_sum(s_send, s_recv, pack_thru, pack_land, go[0:8, 0:128] + ga[0:8, 0:128]
                             + gb[0:8, 0:128], "gather_small_wait_sum")
    small_g = {}
    at = 0
    for name, rows in _SMALL:
        small_g[name] = summed[at:at + rows]
        at += _pad8(rows)
    loss_out = small_g["loss"][0, 0]
    wg_cols = KEY // N_DEV
    cw_cols = CONV // N_DEV
    g_small_list = [
        small_g["dg1"].reshape(1, D_MODEL),
        lax.dynamic_slice_in_dim(small_g["dwg"].reshape(RANK, KEY), me * wg_cols, wg_cols, axis=1)[None],
        small_g["dbg"].reshape(1, KEY), small_g["dgn"].reshape(1, DV),
        lax.dynamic_slice_in_dim(small_g["dconv_w"].reshape(CONV_W, CONV), me * cw_cols, cw_cols, axis=1)[None],
        small_g["dconv_b"].reshape(1, CONV), small_g["dcn_g"].reshape(1, CONV), small_g["dcn_b"].reshape(1, CONV),
        small_g["dg2"].reshape(1, D_MODEL), small_g["dgf"].reshape(1, D_MODEL),
    ]
    row = lambda a: a.reshape(1, D_MODEL)
    w_small = [norm1_g, w_gate_up, b_gate, gla_norm_g, conv_w, conv_b, conv_norm_g, conv_norm_b, norm2_g,
               row(final_norm_g)]
    m_small = [m_norm1_g, m_w_gate_up, m_b_gate, m_gla_norm_g, m_conv_w, m_conv_b, m_conv_norm_g, m_conv_norm_b,
               m_norm2_g, row(m_final_norm_g)]
    v_small = [v_norm1_g, v_w_gate_up, v_b_gate, v_gla_norm_g, v_conv_w, v_conv_b, v_conv_norm_g, v_conv_norm_b,
               v_norm2_g, row(v_final_norm_g)]
    d_small, nm_small, nv_small = _adam_small(g_small_list, w_small, m_small, v_small)
    flat = lambda lst: list(lst[:-1]) + [lst[-1].reshape(D_MODEL)]
    g_small_list, d_small, nm_small, nv_small = flat(g_small_list), flat(d_small), flat(nm_small), flat(nv_small)

    def order(s, w_in_v, w_out_v, w1_v, w2_v):
        return [s[0], w_in_v, s[1], s[2], s[3], s[4], s[5], s[6], s[7], w_out_v, s[8], w1_v, w2_v, s[9]]

    grads = order(g_small_list, cut(gi), go[None], ga[None], gb[None])
    deltas = order(d_small, cut(di), do[None], da_[None], db[None])
    new_m = order(nm_small, cut(mi), mo[None], ma[None], mb[None])
    new_v = order(nv_small, cut(vi), vo[None], va[None], vb[None])
    return (loss_out, dx[None], *grads, *deltas, *new_m, *new_v)
```
